```python
import math
import jax
import jax.numpy as jnp
from jax import lax
import numpy as np

D_MODEL = 2048
BATCH = 8
SEQ = 4096
DEPTH = 4

N_MIXERS = 3
D_INNER = D_MODEL
CHUNK = 128
SG_GROUPS = 16
SG_GROUP_DIM = D_INNER // SG_GROUPS
SG_COLS = 3 * D_INNER
HEAD_DIM = 64
SWA_HEADS = D_INNER // HEAD_DIM
SWA_KV_HEADS = SWA_HEADS // 8
SWA_REP = SWA_HEADS // SWA_KV_HEADS
WINDOW = 128
BLOCK = 128
ROPE_THETA = 10000.0
SWA_COLS = 2 * D_INNER + 2 * SWA_KV_HEADS * HEAD_DIM
RWKV_HEAD_DIM = 64
RWKV_HEADS = D_INNER // RWKV_HEAD_DIM
DECAY_LORA = 96
AAA_LORA = 96
RWKV_COLS = 4 * D_INNER + DECAY_LORA + AAA_LORA
DECAY_SCALE = math.exp(-0.5)
GN_EPS = 64e-5
RMS_EPS = 1e-6
LN_EPS = 1e-5
N_A = (DEPTH + 2) // 3
N_B = (DEPTH + 1) // 3
N_C = DEPTH // 3

kernel_name = 'hybrid_sgmlp_swa_rwkv7_adaln'


def rms_norm(x, g):
    xf = x.astype(jnp.float32)
    y = xf * lax.rsqrt(jnp.mean(xf * xf, axis=-1, keepdims=True) + RMS_EPS)
    return (y * g.astype(jnp.float32)).astype(x.dtype)


def token_shift(t):
    return jnp.concatenate([jnp.zeros_like(t[:, :1]), t[:, :-1]], axis=1)


def rope(x, positions):
    half = HEAD_DIM // 2
    inv_freq = ROPE_THETA ** (-jnp.arange(half, dtype=jnp.float32) / half)
    ang = positions.astype(jnp.float32)[..., None] * inv_freq
    cos = jnp.cos(ang)[:, :, None, :]
    sin = jnp.sin(ang)[:, :, None, :]
    xf = x.astype(jnp.float32)
    x1, x2 = xf[..., :half], xf[..., half:]
    return jnp.concatenate([x1 * cos - x2 * sin, x2 * cos + x1 * sin], axis=-1).astype(x.dtype)


def chunked_spatial_gating(p, ln_g, ln_b, w_s, b_s):
    B, T, _ = p.shape
    u, v, z = jnp.split(p, 3, axis=-1)
    u = jax.nn.gelu(u)
    vf = jax.nn.gelu(v).astype(jnp.float32)
    mean = jnp.mean(vf, axis=-1, keepdims=True)
    var = jnp.mean(jnp.square(vf - mean), axis=-1, keepdims=True)
    v = ((vf - mean) * lax.rsqrt(var + LN_EPS) * ln_g.astype(jnp.float32) + ln_b.astype(jnp.float32)).astype(p.dtype)
    nc = T // CHUNK
    v = v.reshape(B, nc, CHUNK, SG_GROUPS, SG_GROUP_DIM)
    causal = jnp.tril(jnp.ones((CHUNK, CHUNK), dtype=bool))
    w = jnp.where(causal[None], w_s, jnp.zeros_like(w_s))
    f = jnp.einsum('gts,bnsgc->bntgc', w, v) + b_s.T[:, :, None]
    f = f.reshape(B, T, D_INNER)
    return u * f * jax.nn.silu(z)


def sliding_window_attention(p, positions, sinks):
    B, T, _ = p.shape
    kvw = SWA_KV_HEADS * HEAD_DIM
    q, k, v, z = jnp.split(p, [D_INNER, D_INNER + kvw, D_INNER + 2 * kvw], axis=-1)
    q = rope(q.reshape(B, T, SWA_HEADS, HEAD_DIM), positions)
    k = rope(k.reshape(B, T, SWA_KV_HEADS, HEAD_DIM), positions)
    v = v.reshape(B, T, SWA_KV_HEADS, HEAD_DIM)
    nb = T // BLOCK
    qb = q.reshape(B, nb, BLOCK, SWA_KV_HEADS, SWA_REP, HEAD_DIM)

    def with_prev(t):
        tb = t.reshape(B, nb, BLOCK, SWA_KV_HEADS, HEAD_DIM)
        prev = jnp.concatenate([jnp.zeros_like(tb[:, :1]), tb[:, :-1]], axis=1)
        return jnp.concatenate([prev, tb], axis=2)

    kb, vb = with_prev(k), with_prev(v)
    s = jnp.einsum('bnqgrd,bnkgd->bngrqk', qb, kb,
                   preferred_element_type=jnp.float32) * (HEAD_DIM ** -0.5)
    qi = jnp.arange(BLOCK)[:, None]
    kj = jnp.arange(2 * BLOCK)[None, :]
    rel = qi + BLOCK - kj
    band = (rel >= 0) & (rel < WINDOW)
    key_pos = jnp.arange(nb)[:, None] * BLOCK + jnp.arange(2 * BLOCK)[None, :] - BLOCK
    mask = band[None] & (key_pos >= 0)[:, None, :]
    s = jnp.where(mask[None, :, None, None], s, -jnp.inf)
    sink = sinks.astype(jnp.float32).reshape(SWA_KV_HEADS, SWA_REP)[None, None, :, :, None, None]
    m = jnp.maximum(jnp.max(s, axis=-1, keepdims=True), sink)
    e = jnp.exp(s - m)
    denom = jnp.sum(e, axis=-1, keepdims=True) + jnp.exp(sink - m)
    prob = (e / denom).astype(p.dtype)
    o = jnp.einsum('bngrqk,bnkgd->bnqgrd', prob, vb).reshape(B, T, D_INNER)
    return o * jax.nn.silu(z)


def rwkv7_time_mix(p, mu, w0, w_lora, a0, a_lora, k_k, k_a, r_k, gn_g, gn_b):
    B, T, _ = p.shape
    H, N = RWKV_HEADS, RWKV_HEAD_DIM
    p = p + (token_shift(p) - p) * mu
    r, k, v, z, dw, da = jnp.split(p, [D_INNER, 2 * D_INNER, 3 * D_INNER, 4 * D_INNER,
                                       4 * D_INNER + DECAY_LORA], axis=-1)
    decay = jnp.exp(-DECAY_SCALE * jax.nn.sigmoid((w0 + jnp.tanh(dw) @ w_lora).astype(jnp.float32)))
    a = jax.nn.sigmoid((a0 + da @ a_lora).astype(jnp.float32))

    def heads(t):
        return t.astype(jnp.float32).reshape(B, T, H, N)

    r, k, v, decay, a = heads(r), heads(k), heads(v), heads(decay), heads(a)
    kk = k * k_k.astype(jnp.float32).reshape(H, N)
    kk = kk / jnp.maximum(jnp.sqrt(jnp.sum(kk * kk, axis=-1, keepdims=True)), 1e-12)
    k = k * (1.0 + (a - 1.0) * k_a.astype(jnp.float32).reshape(H, N))
    b_vec = kk * a

    def step(S, inp):
        r_t, w_t, k_t, v_t, kk_t, b_t = inp
        sa = jnp.einsum('bhvk,bhk->bhv', S, kk_t)
        S = S * w_t[:, :, None, :] - sa[..., None] * b_t[:, :, None, :] + v_t[..., None] * k_t[:, :, None, :]
        y = jnp.einsum('bhvk,bhk->bhv', S, r_t)
        return S, y

    xs = tuple(jnp.moveaxis(t, 1, 0) for t in (r, decay, k, v, kk, b_vec))
    S0 = jnp.zeros((B, H, N, N), jnp.float32)
    _, y = lax.scan(step, S0, xs)
    y = jnp.moveaxis(y, 0, 1)
    mean = jnp.mean(y, axis=-1, keepdims=True)
    var = jnp.mean(jnp.square(y - mean), axis=-1, keepdims=True)
    y = (y - mean) * lax.rsqrt(var + GN_EPS) * gn_g.astype(jnp.float32).reshape(H, N) \
        + gn_b.astype(jnp.float32).reshape(H, N)
    y = y + jnp.sum(r * k * r_k.astype(jnp.float32), axis=-1, keepdims=True) * v
    y = y.reshape(B, T, D_INNER).astype(p.dtype)
    return y * jax.nn.silu(z)


def _fwd_setup_inputs(seed: int = 0) -> dict:
    key = jax.random.key(seed)
    keys = jax.random.split(key, 32)

    def nrm(i, shape, scale):
        return jax.random.normal(keys[i], shape, jnp.float32) * scale

    positions = jax.random.randint(keys[2], (BATCH, 1), 0, 1024, dtype=jnp.int32) \
        + jnp.arange(SEQ, dtype=jnp.int32)[None, :]
    return {
        'x': nrm(0, (BATCH, SEQ, D_MODEL), 1.0),
        'c': nrm(1, (BATCH, D_MODEL), 1.0),
        'positions': positions,
        'norm_g': 1.0 + nrm(3, (DEPTH, D_MODEL), 0.1),
        'mod_w': nrm(4, (DEPTH, D_MODEL, 3 * D_MODEL), 0.5 * D_MODEL ** -0.5),
        'mod_b': nrm(5, (DEPTH, 3 * D_MODEL), 0.02),
        'final_norm_g': 1.0 + nrm(6, (D_MODEL,), 0.1),
        'sg_w_in': nrm(7, (N_A, D_MODEL, SG_COLS), D_MODEL ** -0.5),
        'sg_w_out': nrm(8, (N_A, D_INNER, D_MODEL), D_INNER ** -0.5),
        'sg_ln_g': 1.0 + nrm(9, (N_A, D_INNER), 0.1),
        'sg_ln_b': nrm(10, (N_A, D_INNER), 0.02),
        'sg_w_spatial': nrm(11, (N_A, SG_GROUPS, CHUNK, CHUNK), CHUNK ** -0.5),
        'sg_b_spatial': 1.0 + nrm(12, (N_A, SG_GROUPS, CHUNK), 0.1),
        'swa_w_in': nrm(13, (N_B, D_MODEL, SWA_COLS), D_MODEL ** -0.5),
        'swa_w_out': nrm(14, (N_B, D_INNER, D_MODEL), D_INNER ** -0.5),
        'swa_sinks': nrm(15, (N_B, SWA_HEADS), 1.0),
        'rwkv_w_in': nrm(16, (N_C, D_MODEL, RWKV_COLS), D_MODEL ** -0.5),
        'rwkv_w_out': nrm(17, (N_C, D_INNER, D_MODEL), D_INNER ** -0.5),
        'rwkv_mu': jax.random.uniform(keys[18], (N_C, RWKV_COLS), jnp.float32),
        'rwkv_w0': jax.random.uniform(keys[19], (N_C, D_INNER), jnp.float32, -4.0, 1.0),
        'rwkv_w_lora': nrm(20, (N_C, DECAY_LORA, D_INNER), DECAY_LORA ** -0.5),
        'rwkv_a0': nrm(21, (N_C, D_INNER), 0.5),
        'rwkv_a_lora': nrm(22, (N_C, AAA_LORA, D_INNER), 0.5 * AAA_LORA ** -0.5),
        'rwkv_k_k': 0.85 + nrm(23, (N_C, D_INNER), 0.1),
        'rwkv_k_a': 1.0 + nrm(24, (N_C, D_INNER), 0.1),
        'rwkv_r_k': nrm(25, (N_C, RWKV_HEADS, RWKV_HEAD_DIM), 0.1),
        'rwkv_gn_g': 1.0 + nrm(26, (N_C, D_INNER), 0.1),
        'rwkv_gn_b': nrm(27, (N_C, D_INNER), 0.02),
    }


def _fwd_reference(x, c, positions, norm_g, mod_w, mod_b, final_norm_g,
              sg_w_in, sg_w_out, sg_ln_g, sg_ln_b, sg_w_spatial, sg_b_spatial,
              swa_w_in, swa_w_out, swa_sinks,
              rwkv_w_in, rwkv_w_out, rwkv_mu, rwkv_w0, rwkv_w_lora, rwkv_a0, rwkv_a_lora,
              rwkv_k_k, rwkv_k_a, rwkv_r_k, rwkv_gn_g, rwkv_gn_b):
    cond = jax.nn.silu(c)
    for i in range(DEPTH):
        kind, j = i % N_MIXERS, i // N_MIXERS
        mod = (cond @ mod_w[i] + mod_b[i])[:, None, :]
        shift, scale, gate = jnp.split(mod, 3, axis=-1)
        h = rms_norm(x, norm_g[i]) * (1.0 + scale) + shift
        if kind == 0:
            y = chunked_spatial_gating(h @ sg_w_in[j], sg_ln_g[j], sg_ln_b[j],
                                       sg_w_spatial[j], sg_b_spatial[j]) @ sg_w_out[j]
        elif kind == 1:
            y = sliding_window_attention(h @ swa_w_in[j], positions, swa_sinks[j]) @ swa_w_out[j]
        else:
            y = rwkv7_time_mix(h @ rwkv_w_in[j], rwkv_mu[j], rwkv_w0[j], rwkv_w_lora[j],
                               rwkv_a0[j], rwkv_a_lora[j], rwkv_k_k[j], rwkv_k_a[j],
                               rwkv_r_k[j], rwkv_gn_g[j], rwkv_gn_b[j]) @ rwkv_w_out[j]
        x = x + gate * y
    return rms_norm(x, final_norm_g)


import jax as _jax
import jax.numpy as _jnp

TWIN_FORMAT = 'train_step'
FWD_PARAMS = ['x', 'c', 'positions', 'norm_g', 'mod_w', 'mod_b', 'final_norm_g', 'sg_w_in', 'sg_w_out', 'sg_ln_g', 'sg_ln_b', 'sg_w_spatial', 'sg_b_spatial', 'swa_w_in', 'swa_w_out', 'swa_sinks', 'rwkv_w_in', 'rwkv_w_out', 'rwkv_mu', 'rwkv_w0', 'rwkv_w_lora', 'rwkv_a0', 'rwkv_a_lora', 'rwkv_k_k', 'rwkv_k_a', 'rwkv_r_k', 'rwkv_gn_g', 'rwkv_gn_b']
TWIN_WEIGHTS = ['norm_g', 'mod_w', 'mod_b', 'final_norm_g', 'sg_w_in', 'sg_w_out', 'sg_ln_g', 'sg_ln_b', 'sg_w_spatial', 'sg_b_spatial', 'swa_w_in', 'swa_w_out', 'swa_sinks', 'rwkv_w_in', 'rwkv_w_out', 'rwkv_mu', 'rwkv_w0', 'rwkv_w_lora', 'rwkv_a0', 'rwkv_a_lora', 'rwkv_k_k', 'rwkv_k_a', 'rwkv_r_k', 'rwkv_gn_g', 'rwkv_gn_b']
TWIN_DIFF_INPUT = 'x'
TWIN_INPUTS = ['x', 'c', 'positions', 'norm_g', 'mod_w', 'mod_b', 'final_norm_g', 'sg_w_in', 'sg_w_out', 'sg_ln_g', 'sg_ln_b', 'sg_w_spatial', 'sg_b_spatial', 'swa_w_in', 'swa_w_out', 'swa_sinks', 'rwkv_w_in', 'rwkv_w_out', 'rwkv_mu', 'rwkv_w0', 'rwkv_w_lora', 'rwkv_a0', 'rwkv_a_lora', 'rwkv_k_k', 'rwkv_k_a', 'rwkv_r_k', 'rwkv_gn_g', 'rwkv_gn_b', 'loss_target', 'm_norm_g', 'm_mod_w', 'm_mod_b', 'm_final_norm_g', 'm_sg_w_in', 'm_sg_w_out', 'm_sg_ln_g', 'm_sg_ln_b', 'm_sg_w_spatial', 'm_sg_b_spatial', 'm_swa_w_in', 'm_swa_w_out', 'm_swa_sinks', 'm_rwkv_w_in', 'm_rwkv_w_out', 'm_rwkv_mu', 'm_rwkv_w0', 'm_rwkv_w_lora', 'm_rwkv_a0', 'm_rwkv_a_lora', 'm_rwkv_k_k', 'm_rwkv_k_a', 'm_rwkv_r_k', 'm_rwkv_gn_g', 'm_rwkv_gn_b', 'v_norm_g', 'v_mod_w', 'v_mod_b', 'v_final_norm_g', 'v_sg_w_in', 'v_sg_w_out', 'v_sg_ln_g', 'v_sg_ln_b', 'v_sg_w_spatial', 'v_sg_b_spatial', 'v_swa_w_in', 'v_swa_w_out', 'v_swa_sinks', 'v_rwkv_w_in', 'v_rwkv_w_out', 'v_rwkv_mu', 'v_rwkv_w0', 'v_rwkv_w_lora', 'v_rwkv_a0', 'v_rwkv_a_lora', 'v_rwkv_k_k', 'v_rwkv_k_a', 'v_rwkv_r_k', 'v_rwkv_gn_g', 'v_rwkv_gn_b']
TWIN_OUTPUTS = ['loss', 'grad_x', 'grad_norm_g', 'grad_mod_w', 'grad_mod_b', 'grad_final_norm_g', 'grad_sg_w_in', 'grad_sg_w_out', 'grad_sg_ln_g', 'grad_sg_ln_b', 'grad_sg_w_spatial', 'grad_sg_b_spatial', 'grad_swa_w_in', 'grad_swa_w_out', 'grad_swa_sinks', 'grad_rwkv_w_in', 'grad_rwkv_w_out', 'grad_rwkv_mu', 'grad_rwkv_w0', 'grad_rwkv_w_lora', 'grad_rwkv_a0', 'grad_rwkv_a_lora', 'grad_rwkv_k_k', 'grad_rwkv_k_a', 'grad_rwkv_r_k', 'grad_rwkv_gn_g', 'grad_rwkv_gn_b', 'delta_norm_g', 'delta_mod_w', 'delta_mod_b', 'delta_final_norm_g', 'delta_sg_w_in', 'delta_sg_w_out', 'delta_sg_ln_g', 'delta_sg_ln_b', 'delta_sg_w_spatial', 'delta_sg_b_spatial', 'delta_swa_w_in', 'delta_swa_w_out', 'delta_swa_sinks', 'delta_rwkv_w_in', 'delta_rwkv_w_out', 'delta_rwkv_mu', 'delta_rwkv_w0', 'delta_rwkv_w_lora', 'delta_rwkv_a0', 'delta_rwkv_a_lora', 'delta_rwkv_k_k', 'delta_rwkv_k_a', 'delta_rwkv_r_k', 'delta_rwkv_gn_g', 'delta_rwkv_gn_b', 'new_m_norm_g', 'new_m_mod_w', 'new_m_mod_b', 'new_m_final_norm_g', 'new_m_sg_w_in', 'new_m_sg_w_out', 'new_m_sg_ln_g', 'new_m_sg_ln_b', 'new_m_sg_w_spatial', 'new_m_sg_b_spatial', 'new_m_swa_w_in', 'new_m_swa_w_out', 'new_m_swa_sinks', 'new_m_rwkv_w_in', 'new_m_rwkv_w_out', 'new_m_rwkv_mu', 'new_m_rwkv_w0', 'new_m_rwkv_w_lora', 'new_m_rwkv_a0', 'new_m_rwkv_a_lora', 'new_m_rwkv_k_k', 'new_m_rwkv_k_a', 'new_m_rwkv_r_k', 'new_m_rwkv_gn_g', 'new_m_rwkv_gn_b', 'new_v_norm_g', 'new_v_mod_w', 'new_v_mod_b', 'new_v_final_norm_g', 'new_v_sg_w_in', 'new_v_sg_w_out', 'new_v_sg_ln_g', 'new_v_sg_ln_b', 'new_v_sg_w_spatial', 'new_v_sg_b_spatial', 'new_v_swa_w_in', 'new_v_swa_w_out', 'new_v_swa_sinks', 'new_v_rwkv_w_in', 'new_v_rwkv_w_out', 'new_v_rwkv_mu', 'new_v_rwkv_w0', 'new_v_rwkv_w_lora', 'new_v_rwkv_a0', 'new_v_rwkv_a_lora', 'new_v_rwkv_k_k', 'new_v_rwkv_k_a', 'new_v_rwkv_r_k', 'new_v_rwkv_gn_g', 'new_v_rwkv_gn_b']
TWIN_LEAF_KINDS = {'loss': 'loss', 'grad_x': 'grad_x', 'grad_norm_g': 'grad_w', 'grad_mod_w': 'grad_w', 'grad_mod_b': 'grad_w', 'grad_final_norm_g': 'grad_w', 'grad_sg_w_in': 'grad_w', 'grad_sg_w_out': 'grad_w', 'grad_sg_ln_g': 'grad_w', 'grad_sg_ln_b': 'grad_w', 'grad_sg_w_spatial': 'grad_w', 'grad_sg_b_spatial': 'grad_w', 'grad_swa_w_in': 'grad_w', 'grad_swa_w_out': 'grad_w', 'grad_swa_sinks': 'grad_w', 'grad_rwkv_w_in': 'grad_w', 'grad_rwkv_w_out': 'grad_w', 'grad_rwkv_mu': 'grad_w', 'grad_rwkv_w0': 'grad_w', 'grad_rwkv_w_lora': 'grad_w', 'grad_rwkv_a0': 'grad_w', 'grad_rwkv_a_lora': 'grad_w', 'grad_rwkv_k_k': 'grad_w', 'grad_rwkv_k_a': 'grad_w', 'grad_rwkv_r_k': 'grad_w', 'grad_rwkv_gn_g': 'grad_w', 'grad_rwkv_gn_b': 'grad_w', 'delta_norm_g': 'delta_w', 'delta_mod_w': 'delta_w', 'delta_mod_b': 'delta_w', 'delta_final_norm_g': 'delta_w', 'delta_sg_w_in': 'delta_w', 'delta_sg_w_out': 'delta_w', 'delta_sg_ln_g': 'delta_w', 'delta_sg_ln_b': 'delta_w', 'delta_sg_w_spatial': 'delta_w', 'delta_sg_b_spatial': 'delta_w', 'delta_swa_w_in': 'delta_w', 'delta_swa_w_out': 'delta_w', 'delta_swa_sinks': 'delta_w', 'delta_rwkv_w_in': 'delta_w', 'delta_rwkv_w_out': 'delta_w', 'delta_rwkv_mu': 'delta_w', 'delta_rwkv_w0': 'delta_w', 'delta_rwkv_w_lora': 'delta_w', 'delta_rwkv_a0': 'delta_w', 'delta_rwkv_a_lora': 'delta_w', 'delta_rwkv_k_k': 'delta_w', 'delta_rwkv_k_a': 'delta_w', 'delta_rwkv_r_k': 'delta_w', 'delta_rwkv_gn_g': 'delta_w', 'delta_rwkv_gn_b': 'delta_w', 'new_m_norm_g': 'new_m', 'new_m_mod_w': 'new_m', 'new_m_mod_b': 'new_m', 'new_m_final_norm_g': 'new_m', 'new_m_sg_w_in': 'new_m', 'new_m_sg_w_out': 'new_m', 'new_m_sg_ln_g': 'new_m', 'new_m_sg_ln_b': 'new_m', 'new_m_sg_w_spatial': 'new_m', 'new_m_sg_b_spatial': 'new_m', 'new_m_swa_w_in': 'new_m', 'new_m_swa_w_out': 'new_m', 'new_m_swa_sinks': 'new_m', 'new_m_rwkv_w_in': 'new_m', 'new_m_rwkv_w_out': 'new_m', 'new_m_rwkv_mu': 'new_m', 'new_m_rwkv_w0': 'new_m', 'new_m_rwkv_w_lora': 'new_m', 'new_m_rwkv_a0': 'new_m', 'new_m_rwkv_a_lora': 'new_m', 'new_m_rwkv_k_k': 'new_m', 'new_m_rwkv_k_a': 'new_m', 'new_m_rwkv_r_k': 'new_m', 'new_m_rwkv_gn_g': 'new_m', 'new_m_rwkv_gn_b': 'new_m', 'new_v_norm_g': 'new_v', 'new_v_mod_w': 'new_v', 'new_v_mod_b': 'new_v', 'new_v_final_norm_g': 'new_v', 'new_v_sg_w_in': 'new_v', 'new_v_sg_w_out': 'new_v', 'new_v_sg_ln_g': 'new_v', 'new_v_sg_ln_b': 'new_v', 'new_v_sg_w_spatial': 'new_v', 'new_v_sg_b_spatial': 'new_v', 'new_v_swa_w_in': 'new_v', 'new_v_swa_w_out': 'new_v', 'new_v_swa_sinks': 'new_v', 'new_v_rwkv_w_in': 'new_v', 'new_v_rwkv_w_out': 'new_v', 'new_v_rwkv_mu': 'new_v', 'new_v_rwkv_w0': 'new_v', 'new_v_rwkv_w_lora': 'new_v', 'new_v_rwkv_a0': 'new_v', 'new_v_rwkv_a_lora': 'new_v', 'new_v_rwkv_k_k': 'new_v', 'new_v_rwkv_k_a': 'new_v', 'new_v_rwkv_r_k': 'new_v', 'new_v_rwkv_gn_g': 'new_v', 'new_v_rwkv_gn_b': 'new_v'}


def _forward(args):
    return _fwd_reference(*[args[k] for k in FWD_PARAMS])


def _output_shape():
    def fwd():
        inp = _fwd_setup_inputs(0)
        return _fwd_reference(*[inp[k] for k in FWD_PARAMS])
    out = _jax.eval_shape(fwd)
    return out.shape, out.dtype

N_MICROBATCH = 1
ADAM_LR = 0.001
ADAM_B1 = 0.9
ADAM_B2 = 0.999
ADAM_EPS = 1e-08
ADAM_WD = 0.01
ADAM_STEP = 10
PER_EXAMPLE_BATCH_AXIS = {'x': 0, 'c': 0, 'positions': 0, 'loss_target': 0}
SHARED_INPUTS = []
_WEIGHT_DTYPES = {'norm_g': _jnp.float32, 'mod_w': _jnp.float32, 'mod_b': _jnp.float32, 'final_norm_g': _jnp.float32, 'sg_w_in': _jnp.float32, 'sg_w_out': _jnp.float32, 'sg_ln_g': _jnp.float32, 'sg_ln_b': _jnp.float32, 'sg_w_spatial': _jnp.float32, 'sg_b_spatial': _jnp.float32, 'swa_w_in': _jnp.float32, 'swa_w_out': _jnp.float32, 'swa_sinks': _jnp.float32, 'rwkv_w_in': _jnp.float32, 'rwkv_w_out': _jnp.float32, 'rwkv_mu': _jnp.float32, 'rwkv_w0': _jnp.float32, 'rwkv_w_lora': _jnp.float32, 'rwkv_a0': _jnp.float32, 'rwkv_a_lora': _jnp.float32, 'rwkv_k_k': _jnp.float32, 'rwkv_k_a': _jnp.float32, 'rwkv_r_k': _jnp.float32, 'rwkv_gn_g': _jnp.float32, 'rwkv_gn_b': _jnp.float32}
MOMENT_SCALE = {'norm_g': 2.389469e-02, 'mod_w': 3.071151e-02, 'mod_b': 6.071266e-02, 'final_norm_g': 1.608620e+01, 'sg_w_in': 1.478703e-02, 'sg_w_out': 1.610114e-02, 'sg_ln_g': 9.164785e-03, 'sg_ln_b': 8.987110e-03, 'sg_w_spatial': 9.147289e-03, 'sg_b_spatial': 1.295388e-02, 'swa_w_in': 8.037497e-03, 'swa_w_out': 7.237263e-03, 'swa_sinks': 3.143580e-03, 'rwkv_w_in': 1.610842e-02, 'rwkv_w_out': 1.527324e-02, 'rwkv_mu': 2.519576e-02, 'rwkv_w0': 7.761885e-03, 'rwkv_w_lora': 2.013256e-03, 'rwkv_a0': 6.678540e-03, 'rwkv_a_lora': 6.170342e-03, 'rwkv_k_k': 1.189553e-02, 'rwkv_k_a': 2.132560e-02, 'rwkv_r_k': 4.968083e-02, 'rwkv_gn_g': 1.536680e-02, 'rwkv_gn_b': 1.471949e-02}


def _to_microbatches(a, axis):
    t = _jnp.moveaxis(a, axis, 0)
    t = t.reshape((N_MICROBATCH, t.shape[0] // N_MICROBATCH) + t.shape[1:])
    return _jnp.moveaxis(t, 1, axis + 1)


def setup_inputs(seed: int = 0) -> dict:
    inp = _fwd_setup_inputs(seed)
    key = _jax.random.fold_in(_jax.random.key(seed), 7919)
    shape, _ = _output_shape()
    out = dict(inp)
    out["loss_target"] = _jax.random.normal(_jax.random.fold_in(key, 0), shape, _jnp.float32)
    for i, name in enumerate(TWIN_WEIGHTS):
        w = inp[name].astype(_jnp.float32)
        if MOMENT_SCALE is None:
            s = _jnp.sqrt(_jnp.mean(_jnp.square(w)) + 1e-30)
        else:
            s = MOMENT_SCALE[name]
        km, kv = _jax.random.split(_jax.random.fold_in(key, i + 1))
        out[name] = w
        out["m_" + name] = s * _jax.random.normal(km, w.shape, _jnp.float32)
        out["v_" + name] = (s * s) * _jax.random.uniform(kv, w.shape, _jnp.float32, 0.5, 1.5)
    if N_MICROBATCH > 1:
        for name, axis in PER_EXAMPLE_BATCH_AXIS.items():
            out[name] = _to_microbatches(out[name], axis)
    return {'x': out['x'], 'c': out['c'], 'positions': out['positions'], 'norm_g': out['norm_g'], 'mod_w': out['mod_w'], 'mod_b': out['mod_b'], 'final_norm_g': out['final_norm_g'], 'sg_w_in': out['sg_w_in'], 'sg_w_out': out['sg_w_out'], 'sg_ln_g': out['sg_ln_g'], 'sg_ln_b': out['sg_ln_b'], 'sg_w_spatial': out['sg_w_spatial'], 'sg_b_spatial': out['sg_b_spatial'], 'swa_w_in': out['swa_w_in'], 'swa_w_out': out['swa_w_out'], 'swa_sinks': out['swa_sinks'], 'rwkv_w_in': out['rwkv_w_in'], 'rwkv_w_out': out['rwkv_w_out'], 'rwkv_mu': out['rwkv_mu'], 'rwkv_w0': out['rwkv_w0'], 'rwkv_w_lora': out['rwkv_w_lora'], 'rwkv_a0': out['rwkv_a0'], 'rwkv_a_lora': out['rwkv_a_lora'], 'rwkv_k_k': out['rwkv_k_k'], 'rwkv_k_a': out['rwkv_k_a'], 'rwkv_r_k': out['rwkv_r_k'], 'rwkv_gn_g': out['rwkv_gn_g'], 'rwkv_gn_b': out['rwkv_gn_b'], 'loss_target': out['loss_target'], 'm_norm_g': out['m_norm_g'], 'm_mod_w': out['m_mod_w'], 'm_mod_b': out['m_mod_b'], 'm_final_norm_g': out['m_final_norm_g'], 'm_sg_w_in': out['m_sg_w_in'], 'm_sg_w_out': out['m_sg_w_out'], 'm_sg_ln_g': out['m_sg_ln_g'], 'm_sg_ln_b': out['m_sg_ln_b'], 'm_sg_w_spatial': out['m_sg_w_spatial'], 'm_sg_b_spatial': out['m_sg_b_spatial'], 'm_swa_w_in': out['m_swa_w_in'], 'm_swa_w_out': out['m_swa_w_out'], 'm_swa_sinks': out['m_swa_sinks'], 'm_rwkv_w_in': out['m_rwkv_w_in'], 'm_rwkv_w_out': out['m_rwkv_w_out'], 'm_rwkv_mu': out['m_rwkv_mu'], 'm_rwkv_w0': out['m_rwkv_w0'], 'm_rwkv_w_lora': out['m_rwkv_w_lora'], 'm_rwkv_a0': out['m_rwkv_a0'], 'm_rwkv_a_lora': out['m_rwkv_a_lora'], 'm_rwkv_k_k': out['m_rwkv_k_k'], 'm_rwkv_k_a': out['m_rwkv_k_a'], 'm_rwkv_r_k': out['m_rwkv_r_k'], 'm_rwkv_gn_g': out['m_rwkv_gn_g'], 'm_rwkv_gn_b': out['m_rwkv_gn_b'], 'v_norm_g': out['v_norm_g'], 'v_mod_w': out['v_mod_w'], 'v_mod_b': out['v_mod_b'], 'v_final_norm_g': out['v_final_norm_g'], 'v_sg_w_in': out['v_sg_w_in'], 'v_sg_w_out': out['v_sg_w_out'], 'v_sg_ln_g': out['v_sg_ln_g'], 'v_sg_ln_b': out['v_sg_ln_b'], 'v_sg_w_spatial': out['v_sg_w_spatial'], 'v_sg_b_spatial': out['v_sg_b_spatial'], 'v_swa_w_in': out['v_swa_w_in'], 'v_swa_w_out': out['v_swa_w_out'], 'v_swa_sinks': out['v_swa_sinks'], 'v_rwkv_w_in': out['v_rwkv_w_in'], 'v_rwkv_w_out': out['v_rwkv_w_out'], 'v_rwkv_mu': out['v_rwkv_mu'], 'v_rwkv_w0': out['v_rwkv_w0'], 'v_rwkv_w_lora': out['v_rwkv_w_lora'], 'v_rwkv_a0': out['v_rwkv_a0'], 'v_rwkv_a_lora': out['v_rwkv_a_lora'], 'v_rwkv_k_k': out['v_rwkv_k_k'], 'v_rwkv_k_a': out['v_rwkv_k_a'], 'v_rwkv_r_k': out['v_rwkv_r_k'], 'v_rwkv_gn_g': out['v_rwkv_gn_g'], 'v_rwkv_gn_b': out['v_rwkv_gn_b']}


def _loss(weights, diff, rest, loss_target):
    with _jax.named_scope("forward"):
        args = {**rest, TWIN_DIFF_INPUT: diff, **{k: w.astype(_WEIGHT_DTYPES[k]) for k, w in weights.items()}}
        y = _forward(args)
    with _jax.named_scope("loss_head"):
        err = _jnp.square(y.astype(_jnp.float32) - loss_target)
        return 0.5 * _jnp.sum(_jnp.mean(err, axis=-1)) if err.ndim else 0.5 * err


def _adamw(w, g, m, v):
    m = ADAM_B1 * m + (1.0 - ADAM_B1) * g
    v = ADAM_B2 * v + (1.0 - ADAM_B2) * _jnp.square(g)
    m_hat = m / (1.0 - ADAM_B1 ** ADAM_STEP)
    v_hat = v / (1.0 - ADAM_B2 ** ADAM_STEP)
    delta = -ADAM_LR * (m_hat / (_jnp.sqrt(v_hat) + ADAM_EPS) + ADAM_WD * w)
    return delta, m, v


def reference(x, c, positions, norm_g, mod_w, mod_b, final_norm_g, sg_w_in, sg_w_out, sg_ln_g, sg_ln_b, sg_w_spatial, sg_b_spatial, swa_w_in, swa_w_out, swa_sinks, rwkv_w_in, rwkv_w_out, rwkv_mu, rwkv_w0, rwkv_w_lora, rwkv_a0, rwkv_a_lora, rwkv_k_k, rwkv_k_a, rwkv_r_k, rwkv_gn_g, rwkv_gn_b, loss_target, m_norm_g, m_mod_w, m_mod_b, m_final_norm_g, m_sg_w_in, m_sg_w_out, m_sg_ln_g, m_sg_ln_b, m_sg_w_spatial, m_sg_b_spatial, m_swa_w_in, m_swa_w_out, m_swa_sinks, m_rwkv_w_in, m_rwkv_w_out, m_rwkv_mu, m_rwkv_w0, m_rwkv_w_lora, m_rwkv_a0, m_rwkv_a_lora, m_rwkv_k_k, m_rwkv_k_a, m_rwkv_r_k, m_rwkv_gn_g, m_rwkv_gn_b, v_norm_g, v_mod_w, v_mod_b, v_final_norm_g, v_sg_w_in, v_sg_w_out, v_sg_ln_g, v_sg_ln_b, v_sg_w_spatial, v_sg_b_spatial, v_swa_w_in, v_swa_w_out, v_swa_sinks, v_rwkv_w_in, v_rwkv_w_out, v_rwkv_mu, v_rwkv_w0, v_rwkv_w_lora, v_rwkv_a0, v_rwkv_a_lora, v_rwkv_k_k, v_rwkv_k_a, v_rwkv_r_k, v_rwkv_gn_g, v_rwkv_gn_b):
    given = dict(x=x, c=c, positions=positions, norm_g=norm_g, mod_w=mod_w, mod_b=mod_b, final_norm_g=final_norm_g, sg_w_in=sg_w_in, sg_w_out=sg_w_out, sg_ln_g=sg_ln_g, sg_ln_b=sg_ln_b, sg_w_spatial=sg_w_spatial, sg_b_spatial=sg_b_spatial, swa_w_in=swa_w_in, swa_w_out=swa_w_out, swa_sinks=swa_sinks, rwkv_w_in=rwkv_w_in, rwkv_w_out=rwkv_w_out, rwkv_mu=rwkv_mu, rwkv_w0=rwkv_w0, rwkv_w_lora=rwkv_w_lora, rwkv_a0=rwkv_a0, rwkv_a_lora=rwkv_a_lora, rwkv_k_k=rwkv_k_k, rwkv_k_a=rwkv_k_a, rwkv_r_k=rwkv_r_k, rwkv_gn_g=rwkv_gn_g, rwkv_gn_b=rwkv_gn_b, loss_target=loss_target, m_norm_g=m_norm_g, m_mod_w=m_mod_w, m_mod_b=m_mod_b, m_final_norm_g=m_final_norm_g, m_sg_w_in=m_sg_w_in, m_sg_w_out=m_sg_w_out, m_sg_ln_g=m_sg_ln_g, m_sg_ln_b=m_sg_ln_b, m_sg_w_spatial=m_sg_w_spatial, m_sg_b_spatial=m_sg_b_spatial, m_swa_w_in=m_swa_w_in, m_swa_w_out=m_swa_w_out, m_swa_sinks=m_swa_sinks, m_rwkv_w_in=m_rwkv_w_in, m_rwkv_w_out=m_rwkv_w_out, m_rwkv_mu=m_rwkv_mu, m_rwkv_w0=m_rwkv_w0, m_rwkv_w_lora=m_rwkv_w_lora, m_rwkv_a0=m_rwkv_a0, m_rwkv_a_lora=m_rwkv_a_lora, m_rwkv_k_k=m_rwkv_k_k, m_rwkv_k_a=m_rwkv_k_a, m_rwkv_r_k=m_rwkv_r_k, m_rwkv_gn_g=m_rwkv_gn_g, m_rwkv_gn_b=m_rwkv_gn_b, v_norm_g=v_norm_g, v_mod_w=v_mod_w, v_mod_b=v_mod_b, v_final_norm_g=v_final_norm_g, v_sg_w_in=v_sg_w_in, v_sg_w_out=v_sg_w_out, v_sg_ln_g=v_sg_ln_g, v_sg_ln_b=v_sg_ln_b, v_sg_w_spatial=v_sg_w_spatial, v_sg_b_spatial=v_sg_b_spatial, v_swa_w_in=v_swa_w_in, v_swa_w_out=v_swa_w_out, v_swa_sinks=v_swa_sinks, v_rwkv_w_in=v_rwkv_w_in, v_rwkv_w_out=v_rwkv_w_out, v_rwkv_mu=v_rwkv_mu, v_rwkv_w0=v_rwkv_w0, v_rwkv_w_lora=v_rwkv_w_lora, v_rwkv_a0=v_rwkv_a0, v_rwkv_a_lora=v_rwkv_a_lora, v_rwkv_k_k=v_rwkv_k_k, v_rwkv_k_a=v_rwkv_k_a, v_rwkv_r_k=v_rwkv_r_k, v_rwkv_gn_g=v_rwkv_gn_g, v_rwkv_gn_b=v_rwkv_gn_b)
    weights = {n: given[n] for n in TWIN_WEIGHTS}
    shared = {n: given[n] for n in SHARED_INPUTS}
    per_example = {n: given[n] for n in ['x', 'c', 'positions']}
    grad_fn = _jax.value_and_grad(_loss, argnums=(0, 1))

    def one_microbatch(ex, loss_target):
        ex = dict(ex)
        diff = ex.pop(TWIN_DIFF_INPUT)
        return grad_fn(weights, diff, {**shared, **ex}, loss_target)

    if N_MICROBATCH == 1:
        loss, (grad_w, grad_x) = one_microbatch(per_example, given["loss_target"])
    else:
        def body(carry, xs):
            loss_sum, grad_sum = carry
            l_k, (gw_k, gx_k) = one_microbatch(xs[0], xs[1])
            with _jax.named_scope("update"):
                return (loss_sum + l_k, _jax.tree.map(_jnp.add, grad_sum, gw_k)), gx_k

        init = (_jnp.zeros((), _jnp.float32), _jax.tree.map(_jnp.zeros_like, weights))
        (loss, grad_w), grad_x = _jax.lax.scan(body, init, (per_example, given["loss_target"]))
    with _jax.named_scope("update"):
        delta_w, new_m, new_v = {}, {}, {}
        for n in TWIN_WEIGHTS:
            delta_w[n], new_m[n], new_v[n] = _adamw(weights[n], grad_w[n], given["m_" + n], given["v_" + n])
    return (loss, grad_x, *[grad_w[n] for n in TWIN_WEIGHTS], *[delta_w[n] for n in TWIN_WEIGHTS],
            *[new_m[n] for n in TWIN_WEIGHTS], *[new_v[n] for n in TWIN_WEIGHTS])
```

```python
import functools
import math

import jax
import jax.numpy as jnp
from jax import lax
from jax.experimental import pallas as pl
from jax.experimental.pallas import tpu as pltpu

F32 = jnp.float32
BF16 = jnp.bfloat16
HIGHEST = lax.Precision.HIGHEST

D = 2048
DEPTH = 4
CHUNK = 128
SG_GROUPS = 16
HEAD = 64
N_HEADS = D // HEAD
KV_HEADS = 4
KVW = KV_HEADS * HEAD
ROPE_THETA = 10000.0
LORA = 96
LORA_PAD = 128
DECAY_SCALE = math.exp(-0.5)
GN_EPS = 64e-5
RMS_EPS = 1e-6
LN_EPS = 1e-5
ADAM_LR, ADAM_B1, ADAM_B2, ADAM_EPS, ADAM_WD, ADAM_STEP = 0.001, 0.9, 0.999, 1e-08, 0.01, 10
LANES = 128
NEG = -1e30
VMEM_LIMIT = 56 * 1024 * 1024

MESHT = pl.DeviceIdType.MESH


def _cparams(*sem):
    return pltpu.CompilerParams(dimension_semantics=sem, vmem_limit_bytes=VMEM_LIMIT)


_NN = (((1,), (0,)), ((), ()))
_NT = (((1,), (1,)), ((), ()))
_TN = (((0,), (0,)), ((), ()))


def _dg(a, b, dims):
    return lax.dot_general(a, b, dims, preferred_element_type=F32)


@jax.custom_vjp
def _bdot_nn(a, b):
    return _dg(a.astype(BF16), b.astype(BF16), _NN)


def _bdot_nn_fwd(a, b):
    a, b = a.astype(BF16), b.astype(BF16)
    return _dg(a, b, _NN), (a, b)


def _bdot_nn_bwd(res, ct):
    a, b = res
    ct = ct.astype(BF16)
    return _dg(ct, b, _NT), _dg(a, ct, _TN)


_bdot_nn.defvjp(_bdot_nn_fwd, _bdot_nn_bwd)


@jax.custom_vjp
def _bdot_nt(a, b):
    return _dg(a.astype(BF16), b.astype(BF16), _NT)


def _bdot_nt_fwd(a, b):
    a, b = a.astype(BF16), b.astype(BF16)
    return _dg(a, b, _NT), (a, b)


def _bdot_nt_bwd(res, ct):
    a, b = res
    ct = ct.astype(BF16)
    return _dg(ct, b, _NN), _dg(ct, a, _TN)


_bdot_nt.defvjp(_bdot_nt_fwd, _bdot_nt_bwd)


def _tile(n, cap):
    if n <= cap:
        return n
    return max(d for d in range(LANES, cap + 1, LANES) if n % d == 0)


def _matmul(a, b, form, name, out_dtype=F32, tm=1024, tn=512, tk=4096):
    if form == "nn":
        (m, k), n = a.shape, b.shape[1]
    elif form == "nt":
        (m, k), n = a.shape, b.shape[0]
    else:
        (k, m), n = a.shape, b.shape[1]
    tm, tn, tk = _tile(m, tm), _tile(n, tn), _tile(k, tk)
    assert m % tm == 0 and n % tn == 0 and k % tk == 0, (name, a.shape, b.shape)
    nk = k // tk
    dims = {"nn": _NN, "nt": _NT, "tn": _TN}[form]
    a_spec = pl.BlockSpec((tk, tm), lambda i, j, l: (l, i)) if form == "tn" else pl.BlockSpec((tm, tk), lambda i, j, l: (i, l))
    b_spec = pl.BlockSpec((tn, tk), lambda i, j, l: (j, l)) if form == "nt" else pl.BlockSpec((tk, tn), lambda i, j, l: (l, j))

    def body(a_ref, b_ref, o_ref, acc_ref):
        part = _dg(a_ref[...], b_ref[...], dims)
        if nk == 1:
            o_ref[...] = part.astype(out_dtype)
        else:
            l = pl.program_id(2)

            @pl.when(l == 0)
            def _():
                acc_ref[...] = part

            @pl.when(l > 0)
            def _():
                acc_ref[...] += part

            @pl.when(l == nk - 1)
            def _():
                o_ref[...] = acc_ref[...].astype(out_dtype)

    return pl.pallas_call(
        body, name=name, grid=(m // tm, n // tn, nk),
        in_specs=[a_spec, b_spec], out_specs=pl.BlockSpec((tm, tn), lambda i, j, l: (i, j)),
        out_shape=jax.ShapeDtypeStruct((m, n), out_dtype),
        scratch_shapes=[pltpu.VMEM((tm, tn) if nk > 1 else (8, LANES), F32)],
        compiler_params=_cparams("parallel", "parallel", "arbitrary"),
    )(a, b)


TB_NORM = 256


def _f_norm_mod(x, g, shift, scale):
    xn = x * lax.rsqrt(jnp.mean(x * x, axis=-1, keepdims=True) + RMS_EPS)
    return (xn * g) * (1.0 + scale) + shift


def _row_spec(width, tb=TB_NORM):
    return pl.BlockSpec((tb, width), lambda i: (i, 0))


def _vec_spec(width, rows=1):
    return pl.BlockSpec((rows, width), lambda i: (0, 0))


def _norm_mod_fwd(x, g, shift, scale, name):
    t = x.shape[0]

    def body(x_ref, g_ref, sh_ref, sc_ref, h_ref):
        h_ref[...] = _f_norm_mod(x_ref[...], g_ref[...], sh_ref[...], sc_ref[...]).astype(BF16)

    return pl.pallas_call(
        body, name=name, grid=(t // TB_NORM,),
        in_specs=[_row_spec(D), _vec_spec(D), _vec_spec(D), _vec_spec(D)], out_specs=_row_spec(D),
        out_shape=jax.ShapeDtypeStruct((t, D), BF16), compiler_params=_cparams("parallel"),
    )(x, g, shift, scale)


def _accum(ref, val, first):
    @pl.when(first)
    def _():
        ref[...] = val

    @pl.when(jnp.logical_not(first))
    def _():
        ref[...] += val


def _norm_mod_bwd(x, g, shift, scale, dh, dx_res, name, dh2=None):
    t = x.shape[0]
    dhs = [dh] if dh2 is None else [dh, dh2]

    def body(x_ref, g_ref, sh_ref, sc_ref, dr_ref, *refs):
        dh_refs, (dx_ref, dg_ref, dsh_ref, dsc_ref) = refs[:len(dhs)], refs[len(dhs):]
        _, vjp = jax.vjp(_f_norm_mod, x_ref[...], g_ref[...], sh_ref[...], sc_ref[...])
        dh_all = dh_refs[0][...]
        for r in dh_refs[1:]:
            dh_all = dh_all + r[...]
        dx, dg, dsh, dsc = vjp(dh_all)
        dx_ref[...] = dx + dr_ref[...]
        first = pl.program_id(0) == 0
        _accum(dg_ref, dg, first)
        _accum(dsh_ref, dsh, first)
        _accum(dsc_ref, dsc, first)

    vec = jax.ShapeDtypeStruct((1, D), F32)
    return pl.pallas_call(
        body, name=name, grid=(t // TB_NORM,),
        in_specs=[_row_spec(D), _vec_spec(D), _vec_spec(D), _vec_spec(D), _row_spec(D)] + [_row_spec(D)] * len(dhs),
        out_specs=[_row_spec(D), _vec_spec(D), _vec_spec(D), _vec_spec(D)],
        out_shape=[jax.ShapeDtypeStruct((t, D), F32), vec, vec, vec], compiler_params=_cparams("arbitrary"),
    )(x, g, shift, scale, dx_res, *dhs)


def _resid_gate(x, y, gate, name):
    t = x.shape[0]

    def body(x_ref, y_ref, g_ref, o_ref):
        o_ref[...] = x_ref[...] + g_ref[...] * y_ref[...]

    return pl.pallas_call(
        body, name=name, grid=(t // TB_NORM,),
        in_specs=[_row_spec(D), _row_spec(D), _vec_spec(D)], out_specs=_row_spec(D),
        out_shape=jax.ShapeDtypeStruct((t, D), F32), compiler_params=_cparams("parallel"),
    )(x, y, gate)


def _gate_bwd(dx, y, gate, name):
    t = dx.shape[0]

    def body(dx_ref, y_ref, g_ref, dy_ref, dg_ref):
        dxv = dx_ref[...]
        dy_ref[...] = (dxv * g_ref[...]).astype(BF16)
        _accum(dg_ref, jnp.sum(dxv * y_ref[...], axis=0, keepdims=True), pl.program_id(0) == 0)

    return pl.pallas_call(
        body, name=name, grid=(t // TB_NORM,),
        in_specs=[_row_spec(D), _row_spec(D), _vec_spec(D)], out_specs=[_row_spec(D), _vec_spec(D)],
        out_shape=[jax.ShapeDtypeStruct((t, D), BF16), jax.ShapeDtypeStruct((1, D), F32)],
        compiler_params=_cparams("arbitrary"),
    )(dx, y, gate)


def _f_final(x, g, target):
    xn = x * lax.rsqrt(jnp.mean(x * x, axis=-1, keepdims=True) + RMS_EPS)
    err = xn * g - target
    return 0.5 * jnp.sum(jnp.mean(err * err, axis=-1, keepdims=True), axis=0, keepdims=True)


def _final_loss_grad(x, g, target, name):
    t = x.shape[0]

    def body(x_ref, g_ref, t_ref, loss_ref, dx_ref, dg_ref):
        loss, vjp = jax.vjp(_f_final, x_ref[...], g_ref[...], t_ref[...])
        dx, dg, _ = vjp(jnp.ones((1, 1), F32))
        dx_ref[...] = dx
        first = pl.program_id(0) == 0
        _accum(dg_ref, dg, first)
        _accum(loss_ref, jnp.broadcast_to(loss, (1, LANES)), first)

    return pl.pallas_call(
        body, name=name, grid=(t // TB_NORM,),
        in_specs=[_row_spec(D), _vec_spec(D), _row_spec(D)],
        out_specs=[_vec_spec(LANES), _row_spec(D), _vec_spec(D)],
        out_shape=[jax.ShapeDtypeStruct((1, LANES), F32), jax.ShapeDtypeStruct((t, D), F32), jax.ShapeDtypeStruct((1, D), F32)],
        compiler_params=_cparams("arbitrary"),
    )(x, g, target)


def _group_selector():
    gi = lax.broadcasted_iota(jnp.int32, (LANES, D), 0)
    ci = lax.broadcasted_iota(jnp.int32, (LANES, D), 1)
    return (ci // (D // SG_GROUPS) == gi).astype(F32)


def _f_sg(p, ln_g, ln_b, w_s, bs_t):
    u, v, z = p[:, :D], p[:, D:2 * D], p[:, 2 * D:]
    u = jax.nn.gelu(u)
    vf = jax.nn.gelu(v)
    mean = jnp.mean(vf, axis=-1, keepdims=True)
    var = jnp.mean(jnp.square(vf - mean), axis=-1, keepdims=True)
    vn = (vf - mean) * lax.rsqrt(var + LN_EPS) * ln_g + ln_b
    ti = lax.broadcasted_iota(jnp.int32, (CHUNK, CHUNK), 0)
    si = lax.broadcasted_iota(jnp.int32, (CHUNK, CHUNK), 1)
    causal = si <= ti
    cg = D // SG_GROUPS
    f = jnp.concatenate(
        [_bdot_nn(jnp.where(causal, w_s[g], 0.0), vn[:, g * cg:(g + 1) * cg]) for g in range(SG_GROUPS)], axis=1)
    f = f + jnp.dot(bs_t, _group_selector(), precision=HIGHEST, preferred_element_type=F32)
    return u * f * jax.nn.silu(z)


def _sg_specs():
    return [pl.BlockSpec((CHUNK, 3 * D), lambda i: (i, 0)), _vec_spec(D), _vec_spec(D),
            pl.BlockSpec((SG_GROUPS, CHUNK, CHUNK), lambda i: (0, 0, 0)), _vec_spec(LANES, CHUNK)]


def _sg_fwd(p, ln_g, ln_b, w_s, bs_t, name):
    t = p.shape[0]

    def body(p_ref, lg_ref, lb_ref, w_ref, b_ref, o_ref):
        o_ref[...] = _f_sg(p_ref[...], lg_ref[...], lb_ref[...], w_ref[...], b_ref[...]).astype(BF16)

    return pl.pallas_call(
        body, name=name, grid=(t // CHUNK,), in_specs=_sg_specs(), out_specs=_row_spec(D, CHUNK),
        out_shape=jax.ShapeDtypeStruct((t, D), BF16), compiler_params=_cparams("parallel"),
    )(p, ln_g, ln_b, w_s, bs_t)


def _sg_bwd(p, ln_g, ln_b, w_s, bs_t, dout, name):
    t = p.shape[0]

    def body(p_ref, lg_ref, lb_ref, w_ref, b_ref, do_ref, dp_ref, dlg_ref, dlb_ref, dw_ref, db_ref):
        _, vjp = jax.vjp(_f_sg, p_ref[...], lg_ref[...], lb_ref[...], w_ref[...], b_ref[...])
        dp, dlg, dlb, dw, db = vjp(do_ref[...])
        dp_ref[...] = dp.astype(BF16)
        first = pl.program_id(0) == 0
        _accum(dlg_ref, dlg, first)
        _accum(dlb_ref, dlb, first)
        _accum(dw_ref, dw, first)
        _accum(db_ref, db, first)

    vec = jax.ShapeDtypeStruct((1, D), F32)
    return pl.pallas_call(
        body, name=name, grid=(t // CHUNK,), in_specs=_sg_specs() + [_row_spec(D, CHUNK)],
        out_specs=[pl.BlockSpec((CHUNK, 3 * D), lambda i: (i, 0)), _vec_spec(D), _vec_spec(D),
                   pl.BlockSpec((SG_GROUPS, CHUNK, CHUNK), lambda i: (0, 0, 0)), _vec_spec(LANES, CHUNK)],
        out_shape=[jax.ShapeDtypeStruct((t, 3 * D), BF16), vec, vec,
                   jax.ShapeDtypeStruct((SG_GROUPS, CHUNK, CHUNK), F32), jax.ShapeDtypeStruct((CHUNK, LANES), F32)],
        compiler_params=_cparams("arbitrary"),
    )(p, ln_g, ln_b, w_s, bs_t, dout)


SWA_COLS = 2 * D + 2 * KVW
KV_BLOCK = 2 * KVW


def _lane_roll(x, shift):
    return pltpu.roll(x, shift, 1)


def _rot_half(x):
    w = x.shape[1]
    lane = lax.broadcasted_iota(jnp.int32, x.shape, 1)
    return jnp.where(lane % HEAD < HEAD // 2, -_lane_roll(x, w - HEAD // 2), _lane_roll(x, HEAD // 2))


@jax.custom_vjp
def _rope(x, cos, sin):
    return x * cos + _rot_half(x) * sin


def _rope_fwd(x, cos, sin):
    return _rope(x, cos, sin), (cos, sin)


def _rope_bwd(res, ct):
    cos, sin = res
    return ct * cos - _rot_half(ct) * sin, jnp.zeros_like(cos), jnp.zeros_like(sin)


_rope.defvjp(_rope_fwd, _rope_bwd)


@jax.custom_vjp
def _swap_halves(x):
    return _lane_roll(x, HEAD)


_swap_halves.defvjp(lambda x: (_lane_roll(x, HEAD), None), lambda _, ct: (_lane_roll(ct, HEAD),))


def _f_swa(pq, pkv, cos, sin, cosp, sinp, sink_row, valid):
    reps = D // LANES
    q = _rope(pq[:, :D], jnp.tile(cos, (1, reps)), jnp.tile(sin, (1, reps))) * (HEAD ** -0.5)
    k = _rope(pq[:, D:D + KVW], jnp.tile(cos, (1, KVW // LANES)), jnp.tile(sin, (1, KVW // LANES)))
    kp = _rope(pkv[:, :KVW], jnp.tile(cosp, (1, KVW // LANES)), jnp.tile(sinp, (1, KVW // LANES)))
    v, vp, z = pq[:, D + KVW:D + 2 * KVW], pkv[:, KVW:], pq[:, D + 2 * KVW:]
    kcat = jnp.concatenate([kp, k], axis=0)
    vcat = jnp.concatenate([vp, v], axis=0)
    lane = lax.broadcasted_iota(jnp.int32, (2 * CHUNK, LANES), 1)
    lo = lane < HEAD
    hlane = lax.broadcasted_iota(jnp.int32, (1, LANES), 1)

    def halves(cat, g):
        blk = cat[:, (g // 2) * LANES:(g // 2 + 1) * LANES]
        other = _swap_halves(blk)
        if g % 2 == 0:
            return jnp.where(lo, blk, 0.0), jnp.where(lo, 0.0, other)
        return jnp.where(lo, other, 0.0), jnp.where(lo, 0.0, blk)

    def probs(s, head):
        sink = jnp.sum(jnp.where(hlane == head, sink_row, 0.0), axis=1, keepdims=True)
        s = jnp.where(valid, s, NEG)
        m = lax.stop_gradient(jnp.maximum(jnp.max(s, axis=1, keepdims=True), sink))
        e = jnp.exp(s - m)
        return e / (jnp.sum(e, axis=1, keepdims=True) + jnp.exp(sink - m))

    outs = []
    rep = N_HEADS // KV_HEADS
    for g in range(KV_HEADS):
        k_lo, k_hi = halves(kcat, g)
        v_lo, v_hi = halves(vcat, g)
        for j in range(g * rep // 2, (g + 1) * rep // 2):
            qp = q[:, j * LANES:(j + 1) * LANES]
            p_a = probs(_bdot_nt(qp, k_lo), 2 * j)
            p_b = probs(_bdot_nt(qp, k_hi), 2 * j + 1)
            outs.append(_bdot_nn(p_a, v_lo) + _bdot_nn(p_b, v_hi))
    return jnp.concatenate(outs, axis=1) * jax.nn.silu(z)


def _swa_valid(block):
    qi = lax.broadcasted_iota(jnp.int32, (CHUNK, 2 * CHUNK), 0)
    kj = lax.broadcasted_iota(jnp.int32, (CHUNK, 2 * CHUNK), 1)
    rel = qi + CHUNK - kj
    return (rel >= 0) & (rel < CHUNK) & ((kj >= CHUNK) | (block > 0))


def _swa_specs(blk):
    prev = lambda i: jnp.maximum(blk(i) - 1, 0)
    kv_col = D // KV_BLOCK
    return [pl.BlockSpec((CHUNK, SWA_COLS), lambda i: (blk(i), 0)),
            pl.BlockSpec((CHUNK, KV_BLOCK), lambda i: (prev(i), kv_col)),
            pl.BlockSpec((CHUNK, LANES), lambda i: (blk(i), 0)), pl.BlockSpec((CHUNK, LANES), lambda i: (blk(i), 0)),
            pl.BlockSpec((CHUNK, LANES), lambda i: (prev(i), 0)), pl.BlockSpec((CHUNK, LANES), lambda i: (prev(i), 0)),
            _vec_spec(LANES)]


def _swa_fwd(p, cos, sin, sink_row, name):
    t = p.shape[0]

    def body(pq_ref, pkv_ref, c_ref, s_ref, cp_ref, sp_ref, sk_ref, o_ref):
        valid = _swa_valid(pl.program_id(0))
        o_ref[...] = _f_swa(pq_ref[...], pkv_ref[...], c_ref[...], s_ref[...], cp_ref[...], sp_ref[...],
                            sk_ref[...], valid).astype(BF16)

    return pl.pallas_call(
        body, name=name, grid=(t // CHUNK,), in_specs=_swa_specs(lambda i: i), out_specs=_row_spec(D, CHUNK),
        out_shape=jax.ShapeDtypeStruct((t, D), BF16), compiler_params=_cparams("parallel"),
    )(p, p, cos, sin, cos, sin, sink_row)


def _swa_bwd(p, cos, sin, sink_row, dout, name):
    t = p.shape[0]
    nb = t // CHUNK
    blk = lambda i: nb - 1 - i

    def body(pq_ref, pkv_ref, c_ref, s_ref, cp_ref, sp_ref, sk_ref, do_ref, dp_ref, dsk_ref, pend_ref):
        i = pl.program_id(0)
        valid = _swa_valid(blk(i))
        f = functools.partial(_f_swa, valid=valid)
        _, vjp = jax.vjp(f, pq_ref[...], pkv_ref[...], c_ref[...], s_ref[...], cp_ref[...], sp_ref[...], sk_ref[...])
        dpq, dpkv, _, _, _, _, dsk = vjp(do_ref[...])

        @pl.when(i == 0)
        def _():
            pend_ref[...] = jnp.zeros_like(pend_ref)

        dp_ref[...] = jnp.concatenate(
            [dpq[:, :D], dpq[:, D:D + KV_BLOCK] + pend_ref[...], dpq[:, D + KV_BLOCK:]], axis=1).astype(BF16)
        pend_ref[...] = dpkv
        _accum(dsk_ref, dsk, i == 0)

    return pl.pallas_call(
        body, name=name, grid=(nb,),
        in_specs=_swa_specs(blk) + [pl.BlockSpec((CHUNK, D), lambda i: (blk(i), 0))],
        out_specs=[pl.BlockSpec((CHUNK, SWA_COLS), lambda i: (blk(i), 0)), _vec_spec(LANES)],
        out_shape=[jax.ShapeDtypeStruct((t, SWA_COLS), BF16), jax.ShapeDtypeStruct((1, LANES), F32)],
        scratch_shapes=[pltpu.VMEM((CHUNK, KV_BLOCK), F32)], compiler_params=_cparams("arbitrary"),
    )(p, p, cos, sin, cos, sin, sink_row, dout)


RW_MAIN = 4 * D
RW_LO = 2 * LORA_PAD
VM = LANES // N_HEADS
VD = HEAD // VM
S_ROWS = VD * HEAD
TB_RW = 128
TB_K = 32
TB_SCAN = 16


def _to_k_layout(x):
    t = x.shape[0]
    return jnp.tile(x.reshape(t, N_HEADS, HEAD).transpose(0, 2, 1), (1, 1, VM))


def _from_k_layout(x4):
    t = x4.shape[0]
    return x4[:, :, :N_HEADS].transpose(0, 2, 1).reshape(t, D)


def _to_v_layout(x):
    t = x.shape[0]
    return x.reshape(t, N_HEADS, HEAD).transpose(0, 2, 1).reshape(t, VD, LANES)


def _from_v_layout(xv):
    t = xv.shape[0]
    return xv.reshape(t, HEAD, N_HEADS).transpose(0, 2, 1).reshape(t, D)


def _param_k_layout(w):
    return jnp.tile(w.reshape(N_HEADS, HEAD).T, (1, VM))


def _param_v_layout(w):
    return w.reshape(N_HEADS, HEAD).T.reshape(VD, LANES)


def _f_rwkv_lora(xs_lo, w0, a0, wl, al):
    decay = jnp.exp(-DECAY_SCALE * jax.nn.sigmoid(w0 + _bdot_nn(jnp.tanh(xs_lo[:, :LORA_PAD]), wl)))
    a = jax.nn.sigmoid(a0 + _bdot_nn(xs_lo[:, LORA_PAD:], al))
    return decay, a


def _prev_rows_spec(width, tb):
    return pl.BlockSpec((8, width), lambda i: (jnp.maximum(i * (tb // 8) - 1, 0), 0))


def _token_shift_lerp(p, prev8, mu, first):
    rows = lax.broadcasted_iota(jnp.int32, p.shape, 0)
    prev = jnp.where(first, 0.0, prev8[7:8, :])
    shifted = jnp.where(rows == 0, prev, pltpu.roll(p, 1, 0))
    return p + (shifted - p) * mu


def _rwkv_pre_fwd(p_main, p_lo, mu_main, mu_lo, w0, a0, wl, al, name):
    t = p_main.shape[0]
    tb = TB_RW

    def body(pm_ref, pmp_ref, pl_ref, plp_ref, mm_ref, ml_ref, w0_ref, a0_ref, wl_ref, al_ref,
             xm_ref, xl_ref, dec_ref, a_ref):
        first = pl.program_id(0) == 0
        xm_ref[...] = _token_shift_lerp(pm_ref[...], pmp_ref[...], mm_ref[...], first)
        xs_lo = _token_shift_lerp(pl_ref[...], plp_ref[...], ml_ref[...], first)
        xl_ref[...] = xs_lo
        dec_ref[...], a_ref[...] = _f_rwkv_lora(xs_lo, w0_ref[...], a0_ref[...], wl_ref[...], al_ref[...])

    return pl.pallas_call(
        body, name=name, grid=(t // tb,),
        in_specs=[_row_spec(RW_MAIN, tb), _prev_rows_spec(RW_MAIN, tb), _row_spec(RW_LO, tb), _prev_rows_spec(RW_LO, tb),
                  _vec_spec(RW_MAIN), _vec_spec(RW_LO), _vec_spec(D), _vec_spec(D),
                  _vec_spec(D, LORA_PAD), _vec_spec(D, LORA_PAD)],
        out_specs=[_row_spec(RW_MAIN, tb), _row_spec(RW_LO, tb), _row_spec(D, tb), _row_spec(D, tb)],
        out_shape=[jax.ShapeDtypeStruct((t, RW_MAIN), F32), jax.ShapeDtypeStruct((t, RW_LO), F32),
                   jax.ShapeDtypeStruct((t, D), F32), jax.ShapeDtypeStruct((t, D), F32)],
        compiler_params=_cparams("parallel"),
    )(p_main, p_main, p_lo, p_lo, mu_main, mu_lo, w0, a0, wl, al)


def _rwkv_lora_bwd(xs_lo, w0, a0, wl, al, ddecay, da, name):
    t = xs_lo.shape[0]
    tb = TB_NORM

    def body(x_ref, w0_ref, a0_ref, wl_ref, al_ref, dd_ref, da_ref, dx_ref, dw0_ref, da0_ref, dwl_ref, dal_ref):
        _, vjp = jax.vjp(_f_rwkv_lora, x_ref[...], w0_ref[...], a0_ref[...], wl_ref[...], al_ref[...])
        dx, dw0, da0, dwl, dal = vjp((dd_ref[...], da_ref[...]))
        dx_ref[...] = dx
        first = pl.program_id(0) == 0
        _accum(dw0_ref, dw0, first)
        _accum(da0_ref, da0, first)
        _accum(dwl_ref, dwl, first)
        _accum(dal_ref, dal, first)

    vec = jax.ShapeDtypeStruct((1, D), F32)
    lor = jax.ShapeDtypeStruct((LORA_PAD, D), F32)
    return pl.pallas_call(
        body, name=name, grid=(t // tb,),
        in_specs=[_row_spec(RW_LO), _vec_spec(D), _vec_spec(D), _vec_spec(D, LORA_PAD), _vec_spec(D, LORA_PAD),
                  _row_spec(D), _row_spec(D)],
        out_specs=[_row_spec(RW_LO), _vec_spec(D), _vec_spec(D), _vec_spec(D, LORA_PAD), _vec_spec(D, LORA_PAD)],
        out_shape=[jax.ShapeDtypeStruct((t, RW_LO), F32), vec, vec, lor, lor], compiler_params=_cparams("arbitrary"),
    )(xs_lo, w0, a0, wl, al, ddecay, da)


def _lerp_bwd(p, dxs, mu, name):
    t, width = p.shape
    tb = TB_RW
    nb = t // tb

    def body(p_ref, pp_ref, d_ref, dn_ref, mu_ref, dp_ref, dmu_ref):
        i = pl.program_id(0)
        pv, dv, mu_v = p_ref[...], d_ref[...], mu_ref[...]
        rows = lax.broadcasted_iota(jnp.int32, pv.shape, 0)
        prev = jnp.where(i == 0, 0.0, pp_ref[7:8, :])
        shifted = jnp.where(rows == 0, prev, pltpu.roll(pv, 1, 0))
        nxt = jnp.where(i == nb - 1, 0.0, dn_ref[0:1, :])
        d_next = jnp.where(rows == tb - 1, nxt, pltpu.roll(dv, tb - 1, 0))
        dp_ref[...] = (dv * (1.0 - mu_v) + d_next * mu_v).astype(BF16)
        _accum(dmu_ref, jnp.sum(dv * (shifted - pv), axis=0, keepdims=True), i == 0)

    return pl.pallas_call(
        body, name=name, grid=(nb,),
        in_specs=[_row_spec(width, tb), _prev_rows_spec(width, tb), _row_spec(width, tb),
                  pl.BlockSpec((8, width), lambda i: (jnp.minimum((i + 1) * (tb // 8), t // 8 - 1), 0)), _vec_spec(width)],
        out_specs=[_row_spec(width, tb), _vec_spec(width)],
        out_shape=[jax.ShapeDtypeStruct((t, width), BF16), jax.ShapeDtypeStruct((1, width), F32)],
        compiler_params=_cparams("arbitrary"),
    )(p, p, dxs, dxs, mu)


def _f_kprep(k, a, r, kkp, kap, rkp):
    kk = k * kkp
    kk = kk / jnp.maximum(jnp.sqrt(jnp.sum(kk * kk, axis=1, keepdims=True)), 1e-12)
    k2 = k * (1.0 + (a - 1.0) * kap)
    rk = jnp.sum(r * k2 * rkp, axis=1, keepdims=True)
    return kk, k2, kk * a, rk


def _k_spec(rows=HEAD, tb=TB_K):
    return pl.BlockSpec((tb, rows, LANES), lambda i: (i, 0, 0))


def _kparam_spec(rows=HEAD):
    return pl.BlockSpec((rows, LANES), lambda i: (0, 0))


def _rwkv_kprep_fwd(k4, a4, r4, kkp, kap, rkp, name):
    t = k4.shape[0]

    def body(k_ref, a_ref, r_ref, kkp_ref, kap_ref, rkp_ref, kk_ref, k2_ref, b_ref, rk_ref):
        kk_ref[...], k2_ref[...], b_ref[...], rk_ref[...] = _f_kprep(
            k_ref[...], a_ref[...], r_ref[...], kkp_ref[...], kap_ref[...], rkp_ref[...])

    big = jax.ShapeDtypeStruct((t, HEAD, LANES), F32)
    return pl.pallas_call(
        body, name=name, grid=(t // TB_K,),
        in_specs=[_k_spec(), _k_spec(), _k_spec(), _kparam_spec(), _kparam_spec(), _kparam_spec()],
        out_specs=[_k_spec(), _k_spec(), _k_spec(), _k_spec(1)],
        out_shape=[big, big, big, jax.ShapeDtypeStruct((t, 1, LANES), F32)], compiler_params=_cparams("parallel"),
    )(k4, a4, r4, kkp, kap, rkp)


def _rwkv_kprep_bwd(k4, a4, r4, kkp, kap, rkp, dkk, dk2, db, drk, dr_scan, name):
    t = k4.shape[0]

    def body(k_ref, a_ref, r_ref, kkp_ref, kap_ref, rkp_ref, dkk_ref, dk2_ref, db_ref, drk_ref, drs_ref,
             dk_ref, da_ref, dr_ref, dkkp_ref, dkap_ref, drkp_ref):
        _, vjp = jax.vjp(_f_kprep, k_ref[...], a_ref[...], r_ref[...], kkp_ref[...], kap_ref[...], rkp_ref[...])
        dk, da, dr, dkkp, dkap, drkp = vjp((dkk_ref[...], dk2_ref[...], db_ref[...], drk_ref[...]))
        dk_ref[...] = dk
        da_ref[...] = da
        dr_ref[...] = dr + drs_ref[...]
        first = pl.program_id(0) == 0
        _accum(dkkp_ref, dkkp, first)
        _accum(dkap_ref, dkap, first)
        _accum(drkp_ref, drkp, first)

    big = jax.ShapeDtypeStruct((t, HEAD, LANES), F32)
    par = jax.ShapeDtypeStruct((HEAD, LANES), F32)
    return pl.pallas_call(
        body, name=name, grid=(t // TB_K,),
        in_specs=[_k_spec(), _k_spec(), _k_spec(), _kparam_spec(), _kparam_spec(), _kparam_spec(),
                  _k_spec(), _k_spec(), _k_spec(), _k_spec(1), _k_spec()],
        out_specs=[_k_spec(), _k_spec(), _k_spec(), _kparam_spec(), _kparam_spec(), _kparam_spec()],
        out_shape=[big, big, big, par, par, par], compiler_params=_cparams("arbitrary"),
    )(k4, a4, r4, kkp, kap, rkp, dkk, dk2, db, drk, dr_scan)


def _lane_group_sum2d(x):
    x = x + pltpu.roll(x, N_HEADS, 1)
    return x + pltpu.roll(x, 2 * N_HEADS, 1)


@jax.custom_vjp
def _lane_group_sum(x):
    return _lane_group_sum2d(x.reshape(-1, LANES)).reshape(x.shape)


_lane_group_sum.defvjp(lambda x: (_lane_group_sum(x), None), lambda _, ct: (_lane_group_sum(ct),))


def _f_post(y, v, rk, g, b):
    mean = _lane_group_sum(jnp.sum(y, axis=1, keepdims=True)) * (1.0 / HEAD)
    yc = y - mean
    var = _lane_group_sum(jnp.sum(yc * yc, axis=1, keepdims=True)) * (1.0 / HEAD)
    return yc * lax.rsqrt(var + GN_EPS) * g + b + rk * v


def _rwkv_post_fwd(y, v, rk, g, b, name):
    t = y.shape[0]

    def body(y_ref, v_ref, rk_ref, g_ref, b_ref, o_ref):
        o_ref[...] = _f_post(y_ref[...], v_ref[...], rk_ref[...], g_ref[...], b_ref[...])

    return pl.pallas_call(
        body, name=name, grid=(t // TB_K,),
        in_specs=[_k_spec(VD), _k_spec(VD), _k_spec(1), _kparam_spec(VD), _kparam_spec(VD)], out_specs=_k_spec(VD),
        out_shape=jax.ShapeDtypeStruct((t, VD, LANES), F32), compiler_params=_cparams("parallel"),
    )(y, v, rk, g, b)


def _rwkv_post_bwd(y, v, rk, g, b, do, name):
    t = y.shape[0]

    def body(y_ref, v_ref, rk_ref, g_ref, b_ref, do_ref, dy_ref, dv_ref, drk_ref, dg_ref, db_ref):
        _, vjp = jax.vjp(_f_post, y_ref[...], v_ref[...], rk_ref[...], g_ref[...], b_ref[...])
        dy, dv, drk, dg, db = vjp(do_ref[...])
        dy_ref[...] = dy
        dv_ref[...] = dv
        drk_ref[...] = _lane_group_sum(drk)
        first = pl.program_id(0) == 0
        _accum(dg_ref, dg, first)
        _accum(db_ref, db, first)

    vl = jax.ShapeDtypeStruct((t, VD, LANES), F32)
    par = jax.ShapeDtypeStruct((VD, LANES), F32)
    return pl.pallas_call(
        body, name=name, grid=(t // TB_K,),
        in_specs=[_k_spec(VD), _k_spec(VD), _k_spec(1), _kparam_spec(VD), _kparam_spec(VD), _k_spec(VD)],
        out_specs=[_k_spec(VD), _k_spec(VD), _k_spec(1), _kparam_spec(VD), _kparam_spec(VD)],
        out_shape=[vl, vl, jax.ShapeDtypeStruct((t, 1, LANES), F32), par, par], compiler_params=_cparams("arbitrary"),
    )(y, v, rk, g, b, do)


def _f_gate(o, z):
    return o * jax.nn.silu(z)


def _z_spec(tb=TB_NORM):
    return pl.BlockSpec((tb, D), lambda i: (i, 3))


def _rwkv_gate_fwd(o, xs_main, name):
    t = o.shape[0]

    def body(o_ref, z_ref, u_ref):
        u_ref[...] = _f_gate(o_ref[...], z_ref[...]).astype(BF16)

    return pl.pallas_call(
        body, name=name, grid=(t // TB_NORM,), in_specs=[_row_spec(D), _z_spec()], out_specs=_row_spec(D),
        out_shape=jax.ShapeDtypeStruct((t, D), BF16), compiler_params=_cparams("parallel"),
    )(o, xs_main)


def _rwkv_gate_bwd(o, xs_main, du, name):
    t = o.shape[0]

    def body(o_ref, z_ref, du_ref, do_ref, dz_ref):
        _, vjp = jax.vjp(_f_gate, o_ref[...], z_ref[...])
        do_ref[...], dz_ref[...] = vjp(du_ref[...])

    full = jax.ShapeDtypeStruct((t, D), F32)
    return pl.pallas_call(
        body, name=name, grid=(t // TB_NORM,), in_specs=[_row_spec(D), _z_spec(), _row_spec(D)],
        out_specs=[_row_spec(D), _row_spec(D)], out_shape=[full, full], compiler_params=_cparams("parallel"),
    )(o, xs_main, du)


def _colsum(x):
    return jnp.sum(x, axis=0, keepdims=True)


def _rwkv_scan_fwd(r4, w4, k24, kk4, b4, v, name):
    t = r4.shape[0]
    tb = TB_SCAN

    def body(r_ref, w_ref, k2_ref, kk_ref, b_ref, v_ref, y_ref, sall_ref, s_scr):
        @pl.when(pl.program_id(0) == 0)
        def _():
            s_scr[...] = jnp.zeros_like(s_scr)

        def step(tt, carry):
            w, kk, b, k2, r = w_ref[tt], kk_ref[tt], b_ref[tt], k2_ref[tt], r_ref[tt]
            for vd in range(VD):
                rows = pl.ds(vd * HEAD, HEAD)
                s = s_scr[rows, :]
                sa = _colsum(s * kk)
                s = s * w - sa * b + v_ref[tt, pl.ds(vd, 1), :] * k2
                s_scr[rows, :] = s
                sall_ref[tt, rows, :] = s
                y_ref[tt, pl.ds(vd, 1), :] = _colsum(s * r)
            return carry

        lax.fori_loop(0, tb, step, 0)

    return pl.pallas_call(
        body, name=name, grid=(t // tb,),
        in_specs=[_k_spec(HEAD, tb)] * 5 + [_k_spec(VD, tb)],
        out_specs=[_k_spec(VD, tb), _k_spec(S_ROWS, tb)],
        out_shape=[jax.ShapeDtypeStruct((t, VD, LANES), F32), jax.ShapeDtypeStruct((t, S_ROWS, LANES), F32)],
        scratch_shapes=[pltpu.VMEM((S_ROWS, LANES), F32)], compiler_params=_cparams("arbitrary"),
    )(r4, w4, k24, kk4, b4, v)


def _rwkv_scan_bwd(dy, s_all, r4, w4, k24, kk4, b4, v, name):
    t = r4.shape[0]
    tb = TB_SCAN
    nb = t // tb
    blk = lambda i: nb - 1 - i

    def body(dy_ref, sall_ref, sprev_ref, r_ref, w_ref, k2_ref, kk_ref, b_ref, v_ref,
             dr_ref, dw_ref, dk2_ref, dkk_ref, db_ref, dv_ref, ds_scr):
        i = pl.program_id(0)

        @pl.when(i == 0)
        def _():
            ds_scr[...] = jnp.zeros_like(ds_scr)

        at_start = blk(i) == 0

        def step(j, carry):
            tt = tb - 1 - j
            tp = jnp.maximum(tt - 1, 0)
            w, kk, b, k2, r = w_ref[tt], kk_ref[tt], b_ref[tt], k2_ref[tt], r_ref[tt]
            dr = dw = dk2 = dkk = db = jnp.zeros((HEAD, LANES), F32)
            for vd in range(VD):
                rows = pl.ds(vd * HEAD, HEAD)
                s_t = sall_ref[tt, rows, :]
                s_p = jnp.where(tt == 0, jnp.where(at_start, 0.0, sprev_ref[0, rows, :]), sall_ref[tp, rows, :])
                dyv = dy_ref[tt, pl.ds(vd, 1), :]
                ds = ds_scr[rows, :] + dyv * r
                dr = dr + s_t * dyv
                dv_ref[tt, pl.ds(vd, 1), :] = _colsum(ds * k2)
                dk2 = dk2 + ds * v_ref[tt, pl.ds(vd, 1), :]
                dsa = -_colsum(ds * b)
                sa = _colsum(s_p * kk)
                db = db - ds * sa
                dw = dw + ds * s_p
                dkk = dkk + s_p * dsa
                ds_scr[rows, :] = ds * w + dsa * kk
            dr_ref[tt] = _lane_group_sum2d(dr)
            dw_ref[tt] = _lane_group_sum2d(dw)
            dk2_ref[tt] = _lane_group_sum2d(dk2)
            dkk_ref[tt] = _lane_group_sum2d(dkk)
            db_ref[tt] = _lane_group_sum2d(db)
            return carry

        lax.fori_loop(0, tb, step, 0)

    rk = lambda rows: pl.BlockSpec((tb, rows, LANES), lambda i: (blk(i), 0, 0))
    big = jax.ShapeDtypeStruct((t, HEAD, LANES), F32)
    return pl.pallas_call(
        body, name=name, grid=(nb,),
        in_specs=[rk(VD), rk(S_ROWS),
                  pl.BlockSpec((1, S_ROWS, LANES), lambda i: (jnp.maximum(blk(i) * tb - 1, 0), 0, 0))]
        + [rk(HEAD)] * 5 + [rk(VD)],
        out_specs=[rk(HEAD)] * 5 + [rk(VD)],
        out_shape=[big] * 5 + [jax.ShapeDtypeStruct((t, VD, LANES), F32)],
        scratch_shapes=[pltpu.VMEM((S_ROWS, LANES), F32)], compiler_params=_cparams("arbitrary"),
    )(dy, s_all, s_all, r4, w4, k24, kk4, b4, v)


def _rwkv_mixer_fwd(p_main, p_lo, prm, tag):
    xs_main, xs_lo, decay, a = _rwkv_pre_fwd(p_main, p_lo, prm["mu_main"], prm["mu_lo"], prm["w0"], prm["a0"],
                                             prm["wl"], prm["al"], tag + "_pre")
    r4, k4, w4, a4 = (_to_k_layout(x) for x in (xs_main[:, :D], xs_main[:, D:2 * D], decay, a))
    v = _to_v_layout(xs_main[:, 2 * D:3 * D])
    kk4, k24, b4, rk = _rwkv_kprep_fwd(k4, a4, r4, prm["kkp"], prm["kap"], prm["rkp"], tag + "_kprep")
    y, s_all = _rwkv_scan_fwd(r4, w4, k24, kk4, b4, v, tag + "_scan")
    o = _from_v_layout(_rwkv_post_fwd(y, v, rk, prm["gn_g"], prm["gn_b"], tag + "_post"))
    u = _rwkv_gate_fwd(o, xs_main, tag + "_gate")
    saved = dict(xs_main=xs_main, xs_lo=xs_lo, r4=r4, k4=k4, w4=w4, a4=a4, v=v, kk4=kk4, k24=k24, b4=b4, rk=rk,
                 y=y, s_all=s_all, o=o)
    return u, saved


def _rwkv_mixer_bwd(p_main, p_lo, prm, sv, du, tag):
    do, dz = _rwkv_gate_bwd(sv["o"], sv["xs_main"], du, tag + "_gate_b")
    dy, dv_post, drk, dgn_g, dgn_b = _rwkv_post_bwd(sv["y"], sv["v"], sv["rk"], prm["gn_g"], prm["gn_b"],
                                                    _to_v_layout(do), tag + "_post_b")
    dr_s, dw4, dk24, dkk4, db4, dv_scan = _rwkv_scan_bwd(dy, sv["s_all"], sv["r4"], sv["w4"], sv["k24"], sv["kk4"],
                                                         sv["b4"], sv["v"], tag + "_scan_b")
    dk4, da4, dr4, dkkp, dkap, drkp = _rwkv_kprep_bwd(sv["k4"], sv["a4"], sv["r4"], prm["kkp"], prm["kap"], prm["rkp"],
                                                      dkk4, dk24, db4, drk, dr_s, tag + "_kprep_b")
    dxs_lo, dw0, da0, dwl, dal = _rwkv_lora_bwd(sv["xs_lo"], prm["w0"], prm["a0"], prm["wl"], prm["al"],
                                                _from_k_layout(dw4), _from_k_layout(da4), tag + "_lora_b")
    dxs_main = jnp.concatenate([_from_k_layout(dr4), _from_k_layout(dk4), _from_v_layout(dv_post + dv_scan), dz], axis=1)
    dp_main, dmu_main = _lerp_bwd(p_main, dxs_main, prm["mu_main"], tag + "_lerp_main_b")
    dp_lo, dmu_lo = _lerp_bwd(p_lo, dxs_lo, prm["mu_lo"], tag + "_lerp_lo_b")
    grads = dict(mu_main=dmu_main, mu_lo=dmu_lo, w0=dw0, a0=da0, wl=dwl, al=dal, kkp=dkkp, kap=dkap, rkp=drkp,
                 gn_g=dgn_g, gn_b=dgn_b)
    return dp_main, dp_lo, grads


N_DEV = 8
N_CHIPS = 4
ANY = pl.BlockSpec(memory_space=pl.ANY)


def _place():
    return lax.axis_index("x"), lax.axis_index("y"), lax.axis_index("c")


def _remote(src, dst, send_sems, recv_sems, k, dev):
    return pltpu.make_async_remote_copy(src_ref=src, dst_ref=dst, send_sem=send_sems.at[k], recv_sem=recv_sems.at[k],
                                        device_id=dev, device_id_type=MESHT)


def _all_gather8(v, name):
    def body(v_ref, out_ref, send_sems, recv_sems, local_sem):
        x, y, c = _place()
        me = 4 * x + 2 * y + c
        mine = pltpu.make_async_copy(v_ref, out_ref.at[me], local_sem)
        mine.start()
        peers = [(x ^ (k >> 2), y ^ ((k >> 1) & 1), c ^ (k & 1)) for k in range(1, N_DEV)]
        sends = [_remote(v_ref, out_ref.at[me], send_sems, recv_sems, k, peer) for k, peer in enumerate(peers)]
        for cp in sends:
            cp.start()
        for k, (px, py, pc) in enumerate(peers):
            _remote(v_ref, out_ref.at[4 * px + 2 * py + pc], send_sems, recv_sems, k, (x, y, c)).wait_recv()
        for cp in sends:
            cp.wait_send()
        mine.wait()

    return pl.pallas_call(
        body, name=name, in_specs=[ANY], out_specs=ANY,
        out_shape=jax.ShapeDtypeStruct((N_DEV,) + v.shape, v.dtype),
        scratch_shapes=[pltpu.SemaphoreType.DMA((N_DEV - 1,)), pltpu.SemaphoreType.DMA((N_DEV - 1,)),
                        pltpu.SemaphoreType.DMA(())],
    )(v)


def _other_chips(x, y):
    return [(1 - x, y), (x, 1 - y), (1 - x, 1 - y)]


def _chip_gather(v, name):
    def body(v_ref, out_ref, send_sems, recv_sems, local_sem):
        x, y, c = _place()
        me = 2 * x + y
        mine = pltpu.make_async_copy(v_ref, out_ref.at[me], local_sem)
        mine.start()
        chips = _other_chips(x, y)
        sends = [_remote(v_ref.at[c], out_ref.at[me, c], send_sems, recv_sems, j, (cx, cy, c))
                 for j, (cx, cy) in enumerate(chips)]
        for cp in sends:
            cp.start()
        for j, (cx, cy) in enumerate(chips):
            landed = out_ref.at[2 * cx + cy, c]
            _remote(v_ref.at[c], landed, send_sems, recv_sems, j, (x, y, c)).wait_recv()
            fwd = _remote(landed, landed, send_sems, recv_sems, 3 + j, (x, y, 1 - c))
            fwd.start()
            sends.append(fwd)
        for j, (cx, cy) in enumerate(chips):
            _remote(v_ref.at[c], out_ref.at[2 * cx + cy, 1 - c], send_sems, recv_sems, 3 + j, (x, y, c)).wait_recv()
        for cp in sends:
            cp.wait_send()
        mine.wait()

    return pl.pallas_call(
        body, name=name, in_specs=[ANY], out_specs=ANY,
        out_shape=jax.ShapeDtypeStruct((N_CHIPS,) + v.shape, v.dtype),
        scratch_shapes=[pltpu.SemaphoreType.DMA((6,)), pltpu.SemaphoreType.DMA((6,)), pltpu.SemaphoreType.DMA(())],
    )(v)


RS_W = 1024
RS_TR = 256


def _rs_pair_exchange(g, name):
    _, _, rows, width = g.shape

    def body(g_ref, got_ref, send_sems, recv_sems):
        x, y, c = _place()
        sends = [_remote(g_ref.at[s, 1 - c], got_ref.at[s], send_sems, recv_sems, s, (x, y, 1 - c)) for s in range(N_CHIPS)]
        for cp in sends:
            cp.start()
        for cp in sends:
            cp.wait()

    return pl.pallas_call(
        body, name=name, in_specs=[ANY], out_specs=ANY, out_shape=jax.ShapeDtypeStruct((N_CHIPS, rows, width), g.dtype),
        scratch_shapes=[pltpu.SemaphoreType.DMA((N_CHIPS,)), pltpu.SemaphoreType.DMA((N_CHIPS,))],
    )(g)


def _rs_pair_add(g, got, c_arr, name):
    _, _, rows, width = g.shape

    def body(c_ref, g_ref, got_ref, p_ref):
        p_ref[...] = (g_ref[...][:, 0] + got_ref[...]).astype(BF16)

    return pl.pallas_call(
        body, name=name,
        grid_spec=pltpu.PrefetchScalarGridSpec(
            num_scalar_prefetch=1, grid=(rows // RS_TR,),
            in_specs=[pl.BlockSpec((N_CHIPS, 1, RS_TR, width), lambda i, c_ref: (0, c_ref[0], i, 0)),
                      pl.BlockSpec((N_CHIPS, RS_TR, width), lambda i, c_ref: (0, i, 0))],
            out_specs=pl.BlockSpec((N_CHIPS, RS_TR, width), lambda i, c_ref: (0, i, 0))),
        out_shape=jax.ShapeDtypeStruct((N_CHIPS, rows, width), BF16), compiler_params=_cparams("parallel"),
    )(c_arr, g, got)


def _rs_chip_exchange(p, name):
    def body(p_ref, q_ref, send_sems, recv_sems, local_sem):
        x, y, c = _place()
        me = 2 * x + y
        mine = pltpu.make_async_copy(p_ref.at[me], q_ref.at[me], local_sem)
        mine.start()
        chips = _other_chips(x, y)
        sends = [_remote(p_ref.at[2 * cx + cy], q_ref.at[me], send_sems, recv_sems, j, (cx, cy, c))
                 for j, (cx, cy) in enumerate(chips)]
        for cp in sends:
            cp.start()
        for j, (cx, cy) in enumerate(chips):
            _remote(p_ref.at[me], q_ref.at[2 * cx + cy], send_sems, recv_sems, j, (x, y, c)).wait_recv()
        for cp in sends:
            cp.wait_send()
        mine.wait()

    return pl.pallas_call(
        body, name=name, in_specs=[ANY], out_specs=ANY, out_shape=jax.ShapeDtypeStruct(p.shape, p.dtype),
        scratch_shapes=[pltpu.SemaphoreType.DMA((3,)), pltpu.SemaphoreType.DMA((3,)), pltpu.SemaphoreType.DMA(())],
    )(p)


def _rs_chip_add(q, name):
    _, rows, width = q.shape

    def body(q_ref, r_ref):
        qv = q_ref[...].astype(F32)
        r_ref[...] = ((qv[0] + qv[1]) + qv[2]) + qv[3]

    return pl.pallas_call(
        body, name=name, grid=(rows // RS_TR,),
        in_specs=[pl.BlockSpec((N_CHIPS, RS_TR, width), lambda i: (0, i, 0))],
        out_specs=pl.BlockSpec((RS_TR, width), lambda i: (i, 0)),
        out_shape=jax.ShapeDtypeStruct((rows, width), F32), compiler_params=_cparams("parallel"),
    )(q)


def _rs_pair_share(r, name):
    def body(r_ref, out_ref, send_sems, recv_sems, local_sem):
        x, y, c = _place()
        mine = pltpu.make_async_copy(r_ref, out_ref.at[c], local_sem)
        mine.start()
        cp = _remote(r_ref, out_ref.at[c], send_sems, recv_sems, 0, (x, y, 1 - c))
        cp.start()
        _remote(r_ref, out_ref.at[1 - c], send_sems, recv_sems, 0, (x, y, c)).wait_recv()
        cp.wait_send()
        mine.wait()

    return pl.pallas_call(
        body, name=name, in_specs=[ANY], out_specs=ANY, out_shape=jax.ShapeDtypeStruct((2,) + r.shape, r.dtype),
        scratch_shapes=[pltpu.SemaphoreType.DMA((1,)), pltpu.SemaphoreType.DMA((1,)), pltpu.SemaphoreType.DMA(())],
    )(r)


def _reduce_scatter(g, c_arr, tag):
    got = _rs_pair_exchange(g, tag + "_pair_x")
    p = _rs_pair_add(g, got, c_arr, tag + "_pair_add")
    q = _rs_chip_exchange(p, tag + "_chip_x")
    r = _rs_chip_add(q, tag + "_chip_add")
    return _rs_pair_share(r, tag + "_share")


def _sum8(a, name):
    _, rows, width = a.shape
    tr = 8 * (rows // 8 if rows <= 64 else 7)
    assert rows % tr == 0

    def body(a_ref, o_ref):
        acc = a_ref[0]
        for d in range(1, N_DEV):
            acc = acc + a_ref[d]
        o_ref[...] = acc

    return pl.pallas_call(
        body, name=name, grid=(rows // tr,), in_specs=[pl.BlockSpec((N_DEV, tr, width), lambda i: (0, i, 0))],
        out_specs=pl.BlockSpec((tr, width), lambda i: (i, 0)), out_shape=jax.ShapeDtypeStruct((rows, width), F32),
        compiler_params=_cparams("parallel"),
    )(a)


MOD_COLS = 3 * D // N_CHIPS
MOD_TK = 512


def _mod_partial(c_all, mod_w, name):
    nk = D // MOD_TK

    def body(c_ref, w_ref, o_ref):
        l = pl.program_id(1)
        part = _bdot_nn(jax.nn.silu(c_ref[...]), w_ref[0])
        _accum(o_ref.at[0], part, l == 0)

    return pl.pallas_call(
        body, name=name, grid=(DEPTH, nk),
        in_specs=[pl.BlockSpec((N_DEV, MOD_TK), lambda i, l: (0, l)), pl.BlockSpec((1, MOD_TK, MOD_COLS), lambda i, l: (i, l, 0))],
        out_specs=pl.BlockSpec((1, N_DEV, MOD_COLS), lambda i, l: (i, 0, 0)),
        out_shape=jax.ShapeDtypeStruct((DEPTH, N_DEV, MOD_COLS), F32), compiler_params=_cparams("parallel", "arbitrary"),
    )(c_all, mod_w)


def _mod_w_grad(c_all, dmod, name):
    def body(c_ref, d_ref, o_ref):
        o_ref[0] = _dg(jax.nn.silu(c_ref[...]).astype(BF16), d_ref[0].astype(BF16), _TN)

    return pl.pallas_call(
        body, name=name, grid=(DEPTH, D // MOD_TK),
        in_specs=[pl.BlockSpec((N_DEV, MOD_TK), lambda i, l: (0, l)), pl.BlockSpec((1, N_DEV, MOD_COLS), lambda i, l: (i, 0, 0))],
        out_specs=pl.BlockSpec((1, MOD_TK, MOD_COLS), lambda i, l: (i, l, 0)),
        out_shape=jax.ShapeDtypeStruct((DEPTH, D, MOD_COLS), F32), compiler_params=_cparams("parallel", "parallel"),
    )(c_all, dmod)


ADAM_BLOCK_BYTES = 1 << 20


def _adamw(w, g, m, v, name):
    shape = w.shape
    cols = shape[-1]
    rows = w.size // cols
    w, g, m, v = (a.reshape(rows, cols) for a in (w, g, m, v))
    cap = max(8, ADAM_BLOCK_BYTES // (4 * cols))
    tr = rows if rows <= cap else max(d for d in range(8, cap + 1, 8) if rows % d == 0)
    c1 = 1.0 - ADAM_B1 ** ADAM_STEP
    c2 = 1.0 - ADAM_B2 ** ADAM_STEP

    def body(w_ref, g_ref, m_ref, v_ref, d_ref, nm_ref, nv_ref):
        gv = g_ref[...]
        mn = ADAM_B1 * m_ref[...] + (1.0 - ADAM_B1) * gv
        vn = ADAM_B2 * v_ref[...] + (1.0 - ADAM_B2) * (gv * gv)
        nm_ref[...] = mn
        nv_ref[...] = vn
        d_ref[...] = -ADAM_LR * ((mn / c1) / (jnp.sqrt(vn / c2) + ADAM_EPS) + ADAM_WD * w_ref[...])

    spec = pl.BlockSpec((tr, cols), lambda i: (i, 0))
    out = jax.ShapeDtypeStruct((rows, cols), F32)
    d, nm, nv = pl.pallas_call(
        body, name=name, grid=(rows // tr,), in_specs=[spec] * 4, out_specs=[spec] * 3, out_shape=[out] * 3,
        compiler_params=_cparams("parallel"),
    )(w, g, m, v)
    return d.reshape(shape), nm.reshape(shape), nv.reshape(shape)


W_NAMES = ("norm_g", "mod_w", "mod_b", "final_norm_g", "sg_w_in", "sg_w_out", "sg_ln_g", "sg_ln_b", "sg_w_spatial",
           "sg_b_spatial", "swa_w_in", "swa_w_out", "swa_sinks", "rwkv_w_in", "rwkv_w_out", "rwkv_mu", "rwkv_w0",
           "rwkv_w_lora", "rwkv_a0", "rwkv_a_lora", "rwkv_k_k", "rwkv_k_a", "rwkv_r_k", "rwkv_gn_g", "rwkv_gn_b")
BIG = {"sg_w_in": 2, "sg_w_out": 1, "swa_w_in": 2, "swa_w_out": 1, "rwkv_w_in": 2, "rwkv_w_out": 1}
SMALL = {"sg_ln_g": 1, "sg_ln_b": 1, "rwkv_mu": 1, "rwkv_w0": 1, "rwkv_w_lora": 2, "rwkv_a0": 1, "rwkv_a_lora": 2,
         "rwkv_k_k": 1, "rwkv_k_a": 1, "rwkv_gn_g": 1, "rwkv_gn_b": 1}
SHARDED = {**BIG, **SMALL}
REPLICATED = ("norm_g", "final_norm_g", "sg_w_spatial", "sg_b_spatial", "swa_sinks", "rwkv_r_k")
KINDS = ("sg", "swa", "rwkv", "sg")


def _pad_to(flat, n):
    return jnp.pad(flat, (0, n - flat.shape[0]))


def _round_up(n, m):
    return -(-n // m) * m


def _join_shards(gathered, axis):
    return jnp.concatenate([gathered[s] for s in range(N_CHIPS)], axis=axis)


def _chip_blocks(full, axis):
    return jnp.stack(jnp.split(full, N_CHIPS, axis=axis)).reshape(N_CHIPS, -1)


def _gather_big(w, axis, name):
    cols = w.shape[-1]
    halves = w.astype(BF16).reshape(2, -1, cols)
    return _join_shards(_chip_gather(halves, name).reshape((N_CHIPS,) + w.shape), axis)


def _gather_small(shards, name):
    flat = jnp.concatenate([shards[n].reshape(-1) for n in SMALL])
    rows = _round_up(flat.shape[0], 2 * 8 * LANES) // (2 * LANES)
    got = _chip_gather(_pad_to(flat, 2 * rows * LANES).reshape(2, rows, LANES), name).reshape(N_CHIPS, -1)
    out, off = {}, 0
    for n, axis in SMALL.items():
        size = shards[n].size
        out[n] = _join_shards(got[:, off:off + size].reshape((N_CHIPS,) + shards[n].shape), axis)
        off += size
    return out


def _lora_pad_rows(w):
    return jnp.pad(w, ((0, LORA_PAD - LORA), (0, 0)))


def _lo_cols(a):
    z = jnp.zeros(a.shape[:-1] + (LORA_PAD - LORA,), a.dtype)
    return jnp.concatenate([a[..., :LORA], z, a[..., LORA:], z], axis=-1)


def _lo_cols_inv(a):
    return jnp.concatenate([a[..., :LORA], a[..., LORA_PAD:LORA_PAD + LORA]], axis=-1)


def _k_param_inv(p4):
    return p4[:, :N_HEADS].T.reshape(-1)


def _v_param_inv(pv):
    return pv.reshape(HEAD, N_HEADS).T.reshape(-1)


def kernel(x, c, positions, norm_g, mod_w, mod_b, final_norm_g, sg_w_in, sg_w_out, sg_ln_g, sg_ln_b, sg_w_spatial,
           sg_b_spatial, swa_w_in, swa_w_out, swa_sinks, rwkv_w_in, rwkv_w_out, rwkv_mu, rwkv_w0, rwkv_w_lora, rwkv_a0,
           rwkv_a_lora, rwkv_k_k, rwkv_k_a, rwkv_r_k, rwkv_gn_g, rwkv_gn_b, loss_target, m_norm_g, m_mod_w, m_mod_b,
           m_final_norm_g, m_sg_w_in, m_sg_w_out, m_sg_ln_g, m_sg_ln_b, m_sg_w_spatial, m_sg_b_spatial, m_swa_w_in,
           m_swa_w_out, m_swa_sinks, m_rwkv_w_in, m_rwkv_w_out, m_rwkv_mu, m_rwkv_w0, m_rwkv_w_lora, m_rwkv_a0,
           m_rwkv_a_lora, m_rwkv_k_k, m_rwkv_k_a, m_rwkv_r_k, m_rwkv_gn_g, m_rwkv_gn_b, v_norm_g, v_mod_w, v_mod_b,
           v_final_norm_g, v_sg_w_in, v_sg_w_out, v_sg_ln_g, v_sg_ln_b, v_sg_w_spatial, v_sg_b_spatial, v_swa_w_in,
           v_swa_w_out, v_swa_sinks, v_rwkv_w_in, v_rwkv_w_out, v_rwkv_mu, v_rwkv_w0, v_rwkv_w_lora, v_rwkv_a0,
           v_rwkv_a_lora, v_rwkv_k_k, v_rwkv_k_a, v_rwkv_r_k, v_rwkv_gn_g, v_rwkv_gn_b):
    given = dict(locals())
    w = {n: given[n] for n in W_NAMES}
    xi, yi, ci = _place()
    chip = 2 * xi + yi
    me = 4 * xi + 2 * yi + ci
    c_arr = ci.astype(jnp.int32).reshape(1)
    t = x.shape[1]
    xs = [x[0]]

    c_all = _all_gather8(c, "gather_c")[:, 0, :]
    mod_part = _mod_partial(c_all, mod_w, "mod_fwd")
    mod_all = _all_gather8(mod_part, "gather_mod")[::2]
    mod_mine = lax.dynamic_index_in_dim(mod_all, me, axis=2, keepdims=False)
    mod = mod_mine.transpose(1, 0, 2).reshape(DEPTH, 3 * D) + mod_b
    shift, scale, gate = mod[:, :D], mod[:, D:2 * D], mod[:, 2 * D:]

    full = {n: _gather_big(w[n], axis, "gather_" + n) for n, axis in BIG.items()}
    full.update(_gather_small(w, "gather_small"))
    rw_in = full["rwkv_w_in"][0]
    rw_main, rw_lo = rw_in[:, :RW_MAIN], _lo_cols(rw_in[:, RW_MAIN:])
    mu = full["rwkv_mu"][0]
    rw_prm = dict(mu_main=mu[None, :RW_MAIN], mu_lo=_lo_cols(mu[None, RW_MAIN:]), w0=full["rwkv_w0"], a0=full["rwkv_a0"],
                  wl=_lora_pad_rows(full["rwkv_w_lora"][0]), al=_lora_pad_rows(full["rwkv_a_lora"][0]),
                  kkp=_param_k_layout(full["rwkv_k_k"][0]), kap=_param_k_layout(full["rwkv_k_a"][0]),
                  rkp=_param_k_layout(rwkv_r_k.reshape(-1)),
                  gn_g=_param_v_layout(full["rwkv_gn_g"][0]), gn_b=_param_v_layout(full["rwkv_gn_b"][0]))
    bs_t = [jnp.pad(sg_b_spatial[j].T, ((0, 0), (0, LANES - SG_GROUPS))) for j in range(2)]
    sink_row = jnp.pad(swa_sinks, ((0, 0), (0, LANES - N_HEADS)))
    inv_freq = ROPE_THETA ** (-jnp.arange(HEAD // 2, dtype=F32) / (HEAD // 2))
    ang = positions[0].astype(F32)[:, None] * inv_freq
    cos, sin = jnp.tile(jnp.cos(ang), (1, LANES * 2 // HEAD)), jnp.tile(jnp.sin(ang), (1, LANES * 2 // HEAD))

    def row(a, i):
        return a[i:i + 1]

    hs, ps, us, ys, rw_saved = [], [], [], [], None
    for i, kind in enumerate(KINDS):
        j = i // 3
        tag = f"l{i}_{kind}"
        h = _norm_mod_fwd(xs[i], row(norm_g, i), row(shift, i), row(scale, i), tag + "_norm")
        if kind == "sg":
            p = _matmul(h, full["sg_w_in"][j], "nn", tag + "_in")
            u = _sg_fwd(p, row(full["sg_ln_g"], j), row(full["sg_ln_b"], j), sg_w_spatial[j], bs_t[j], tag + "_mix")
            w_out = full["sg_w_out"][j]
        elif kind == "swa":
            p = _matmul(h, full["swa_w_in"][0], "nn", tag + "_in")
            u = _swa_fwd(p, cos, sin, sink_row, tag + "_mix")
            w_out = full["swa_w_out"][0]
        else:
            p = (_matmul(h, rw_main, "nn", tag + "_in"), _matmul(h, rw_lo, "nn", tag + "_in_lo"))
            u, rw_saved = _rwkv_mixer_fwd(p[0], p[1], rw_prm, tag)
            w_out = full["rwkv_w_out"][0]
        y = _matmul(u, w_out, "nn", tag + "_out")
        xs.append(_resid_gate(xs[i], y, row(gate, i), tag + "_resid"))
        hs.append(h), ps.append(p), us.append(u), ys.append(y)

    loss_part, dx, d_final_g = _final_loss_grad(xs[DEPTH], final_norm_g[None], loss_target[0], "loss")
    loss = lax.psum(loss_part[0, 0], ("x", "y", "c"))

    gfull = {n: [None, None] for n in ("sg_w_in", "sg_w_out", "sg_ln_g", "sg_ln_b", "sg_w_spatial", "sg_b_spatial")}
    d_norm_g, d_mod = [None] * DEPTH, [None] * DEPTH
    for i in reversed(range(DEPTH)):
        kind, j = KINDS[i], i // 3
        tag = f"l{i}_{kind}_b"
        dy, d_gate = _gate_bwd(dx, ys[i], row(gate, i), tag + "_gate")
        if kind == "sg":
            w_in, w_out = full["sg_w_in"][j], full["sg_w_out"][j]
        elif kind == "swa":
            w_in, w_out = full["swa_w_in"][0], full["swa_w_out"][0]
        else:
            w_in, w_out = None, full["rwkv_w_out"][0]
        du = _matmul(dy, w_out, "nt", tag + "_du")
        dw_out = _matmul(us[i], dy, "tn", tag + "_dwout")
        if kind == "sg":
            dp, dlg, dlb, dws, dbs = _sg_bwd(ps[i], row(full["sg_ln_g"], j), row(full["sg_ln_b"], j), sg_w_spatial[j],
                                             bs_t[j], du, tag + "_mix")
            gfull["sg_w_out"][j], gfull["sg_ln_g"][j], gfull["sg_ln_b"][j] = dw_out, dlg[0], dlb[0]
            gfull["sg_w_spatial"][j], gfull["sg_b_spatial"][j] = dws, dbs[:, :SG_GROUPS].T
            gfull["sg_w_in"][j] = _matmul(hs[i], dp, "tn", tag + "_dwin")
            dh, dh2 = _matmul(dp, w_in, "nt", tag + "_dh"), None
        elif kind == "swa":
            dp, dsk = _swa_bwd(ps[i], cos, sin, sink_row, du, tag + "_mix")
            gfull["swa_w_out"], gfull["swa_sinks"] = dw_out[None], dsk[:, :N_HEADS]
            gfull["swa_w_in"] = _matmul(hs[i], dp, "tn", tag + "_dwin")[None]
            dh, dh2 = _matmul(dp, w_in, "nt", tag + "_dh"), None
        else:
            dpm, dpl, rg = _rwkv_mixer_bwd(ps[i][0], ps[i][1], rw_prm, rw_saved, du, tag)
            dw_main = _matmul(hs[i], dpm, "tn", tag + "_dwin")
            dw_lo = _matmul(hs[i], dpl, "tn", tag + "_dwin_lo")
            gfull["rwkv_w_in"] = jnp.concatenate([dw_main, _lo_cols_inv(dw_lo)], axis=1)[None]
            gfull["rwkv_w_out"] = dw_out[None]
            gfull["rwkv_mu"] = jnp.concatenate([rg["mu_main"], _lo_cols_inv(rg["mu_lo"])], axis=1)
            gfull["rwkv_w0"], gfull["rwkv_a0"] = rg["w0"], rg["a0"]
            gfull["rwkv_w_lora"], gfull["rwkv_a_lora"] = rg["wl"][None, :LORA], rg["al"][None, :LORA]
            gfull["rwkv_k_k"], gfull["rwkv_k_a"] = _k_param_inv(rg["kkp"])[None], _k_param_inv(rg["kap"])[None]
            gfull["rwkv_r_k"] = _k_param_inv(rg["rkp"]).reshape(1, N_HEADS, HEAD)
            gfull["rwkv_gn_g"], gfull["rwkv_gn_b"] = _v_param_inv(rg["gn_g"])[None], _v_param_inv(rg["gn_b"])[None]
            dh, dh2 = _matmul(dpm, rw_main, "nt", tag + "_dh"), _matmul(dpl, rw_lo, "nt", tag + "_dh_lo")
        dx, dg, dsh, dsc = _norm_mod_bwd(xs[i], row(norm_g, i), row(shift, i), row(scale, i), dh, dx, tag + "_norm", dh2)
        d_norm_g[i] = dg[0]
        d_mod[i] = jnp.concatenate([dsh[0], dsc[0], d_gate[0]])
    for n in ("sg_w_in", "sg_w_out", "sg_ln_g", "sg_ln_b", "sg_w_spatial", "sg_b_spatial"):
        gfull[n] = jnp.stack(gfull[n])
    gfull["norm_g"], gfull["final_norm_g"] = jnp.stack(d_norm_g), d_final_g[0]

    packed = jnp.concatenate([_chip_blocks(gfull[n], axis) for n, axis in SHARDED.items()], axis=1)
    rs_rows = _round_up(packed.shape[1], 2 * RS_TR * RS_W) // (2 * RS_W)
    packed = jnp.pad(packed, ((0, 0), (0, 2 * rs_rows * RS_W - packed.shape[1]))).reshape(N_CHIPS, 2, rs_rows, RS_W)
    reduced = _reduce_scatter(packed, c_arr, "rs").reshape(-1)
    grads, off = {}, 0
    for n in SHARDED:
        grads[n] = reduced[off:off + w[n].size].reshape(w[n].shape)
        off += w[n].size

    rep_flat = jnp.concatenate([jnp.stack(d_mod).reshape(-1)] + [gfull[n].reshape(-1) for n in REPLICATED])
    rep_rows = _round_up(rep_flat.shape[0], 56 * RS_W) // RS_W
    rep_all = _all_gather8(_pad_to(rep_flat, rep_rows * RS_W).reshape(rep_rows, RS_W), "gather_rep")
    rep_sum = _sum8(rep_all, "sum_rep").reshape(-1)
    grads["mod_b"] = rep_sum[:DEPTH * 3 * D].reshape(DEPTH, 3 * D)
    off = DEPTH * 3 * D
    for n in REPLICATED:
        grads[n] = rep_sum[off:off + w[n].size].reshape(w[n].shape)
        off += w[n].size
    dmod_all = rep_all.reshape(N_DEV, -1)[:, :DEPTH * 3 * D].reshape(N_DEV, DEPTH, 3 * D)
    dmod_cols = lax.dynamic_slice_in_dim(dmod_all, chip * MOD_COLS, MOD_COLS, axis=2).transpose(1, 0, 2)
    grads["mod_w"] = _mod_w_grad(c_all, dmod_cols, "mod_w_grad")

    deltas, new_m, new_v = {}, {}, {}
    for n in W_NAMES:
        deltas[n], new_m[n], new_v[n] = _adamw(w[n], grads[n], given["m_" + n], given["v_" + n], "adamw_" + n)
    return (loss, dx[None], *[grads[n] for n in W_NAMES], *[deltas[n] for n in W_NAMES],
            *[new_m[n] for n in W_NAMES], *[new_v[n] for n in W_NAMES])
```

```python
import functools
import math

import jax
import jax.numpy as jnp
from jax import lax
from jax.experimental import pallas as pl
from jax.experimental.pallas import tpu as pltpu

F32 = jnp.float32
BF16 = jnp.bfloat16
HIGHEST = lax.Precision.HIGHEST

D = 2048
DEPTH = 4
CHUNK = 128
SG_GROUPS = 16
HEAD = 64
N_HEADS = D // HEAD
KV_HEADS = 4
KVW = KV_HEADS * HEAD
ROPE_THETA = 10000.0
LORA = 96
LORA_PAD = 128
DECAY_SCALE = math.exp(-0.5)
GN_EPS = 64e-5
RMS_EPS = 1e-6
LN_EPS = 1e-5
ADAM_LR, ADAM_B1, ADAM_B2, ADAM_EPS, ADAM_WD, ADAM_STEP = 0.001, 0.9, 0.999, 1e-08, 0.01, 10
LANES = 128
SUB = 8
NEG = -1e30
VMEM_LIMIT = 56 * 1024 * 1024

MESHT = pl.DeviceIdType.MESH


def _cparams(*sem):
    return pltpu.CompilerParams(dimension_semantics=sem, vmem_limit_bytes=VMEM_LIMIT)


_NN = (((1,), (0,)), ((), ()))
_NT = (((1,), (1,)), ((), ()))
_TN = (((0,), (0,)), ((), ()))


def _dg(a, b, dims):
    return lax.dot_general(a, b, dims, preferred_element_type=F32)


@jax.custom_vjp
def _bdot_nn(a, b):
    return _dg(a.astype(BF16), b.astype(BF16), _NN)


def _bdot_nn_fwd(a, b):
    a, b = a.astype(BF16), b.astype(BF16)
    return _dg(a, b, _NN), (a, b)


def _bdot_nn_bwd(res, ct):
    a, b = res
    ct = ct.astype(BF16)
    return _dg(ct, b, _NT), _dg(a, ct, _TN)


_bdot_nn.defvjp(_bdot_nn_fwd, _bdot_nn_bwd)


@jax.custom_vjp
def _bdot_nt(a, b):
    return _dg(a.astype(BF16), b.astype(BF16), _NT)


def _bdot_nt_fwd(a, b):
    a, b = a.astype(BF16), b.astype(BF16)
    return _dg(a, b, _NT), (a, b)


def _bdot_nt_bwd(res, ct):
    a, b = res
    ct = ct.astype(BF16)
    return _dg(ct, b, _NN), _dg(ct, a, _TN)


_bdot_nt.defvjp(_bdot_nt_fwd, _bdot_nt_bwd)


def _tile(n, cap):
    if n <= cap:
        return n
    return max(d for d in range(LANES, cap + 1, LANES) if n % d == 0)


def _matmul(a, b, form, name, out_dtype=F32, blocked=False, tm=1024, tn=512, tk=4096):
    if form == "nn":
        (m, k), n = a.shape, (N_CHIPS * b.shape[2] if blocked else b.shape[1])
    elif form == "nt":
        m, k, n = a.shape[0], a.shape[1], (b.shape[1] if blocked else b.shape[0])
    else:
        (k, m), n = a.shape, b.shape[1]
    per_chip = (k if form == "nt" else n) // N_CHIPS
    if blocked and form == "nt":
        tk = _tile(per_chip, tk)
    elif blocked:
        tn = _tile(per_chip, tn)
    tm, tn, tk = _tile(m, tm), _tile(n, tn), _tile(k, tk)
    assert m % tm == 0 and n % tn == 0 and k % tk == 0, (name, a.shape, b.shape)
    nk = k // tk
    dims = {"nn": _NN, "nt": _NT, "tn": _TN}[form]
    a_spec = pl.BlockSpec((tk, tm), lambda i, j, l: (l, i)) if form == "tn" else pl.BlockSpec((tm, tk), lambda i, j, l: (i, l))
    b_spec = pl.BlockSpec((tn, tk), lambda i, j, l: (j, l)) if form == "nt" else pl.BlockSpec((tk, tn), lambda i, j, l: (l, j))
    o_spec = pl.BlockSpec((tm, tn), lambda i, j, l: (i, j))
    o_shape = (m, n)
    if blocked and form == "nn":
        pc = per_chip // tn
        b_spec = pl.BlockSpec((None, tk, tn), lambda i, j, l: (j // pc, l, j % pc))
    elif blocked and form == "nt":
        pc = per_chip // tk
        b_spec = pl.BlockSpec((None, tn, tk), lambda i, j, l: (l // pc, j, l % pc))
    elif blocked:
        pc = per_chip // tn
        o_spec = pl.BlockSpec((None, tm, tn), lambda i, j, l: (j // pc, i, j % pc))
        o_shape = (N_CHIPS, m, per_chip)

    def body(a_ref, b_ref, o_ref, acc_ref):
        part = _dg(a_ref[...], b_ref[...], dims)
        if nk == 1:
            o_ref[...] = part.astype(out_dtype)
        else:
            l = pl.program_id(2)

            @pl.when(l == 0)
            def _():
                acc_ref[...] = part

            @pl.when(l > 0)
            def _():
                acc_ref[...] += part

            @pl.when(l == nk - 1)
            def _():
                o_ref[...] = acc_ref[...].astype(out_dtype)

    return pl.pallas_call(
        body, name=name, grid=(m // tm, n // tn, nk),
        in_specs=[a_spec, b_spec], out_specs=o_spec, out_shape=jax.ShapeDtypeStruct(o_shape, out_dtype),
        scratch_shapes=[pltpu.VMEM((tm, tn) if nk > 1 else (8, LANES), F32)],
        compiler_params=_cparams("parallel", "parallel", "arbitrary"),
    )(a, b)


TB_NORM = 256


def _f_norm_mod(x, g, shift, scale):
    xn = x * lax.rsqrt(jnp.mean(x * x, axis=-1, keepdims=True) + RMS_EPS)
    return (xn * g) * (1.0 + scale) + shift


def _row_spec(width, tb=TB_NORM):
    return pl.BlockSpec((tb, width), lambda i: (i, 0))


def _vec_spec(width, rows=1):
    return pl.BlockSpec((rows, width), lambda i: (0, 0))


def _norm_mod_fwd(x, g, shift, scale, name):
    t = x.shape[0]

    def body(x_ref, g_ref, sh_ref, sc_ref, h_ref):
        h_ref[...] = _f_norm_mod(x_ref[...], g_ref[...], sh_ref[...], sc_ref[...]).astype(BF16)

    return pl.pallas_call(
        body, name=name, grid=(t // TB_NORM,),
        in_specs=[_row_spec(D), _vec_spec(D), _vec_spec(D), _vec_spec(D)], out_specs=_row_spec(D),
        out_shape=jax.ShapeDtypeStruct((t, D), BF16), compiler_params=_cparams("parallel"),
    )(x, g, shift, scale)


def _accum(ref, val, first):
    @pl.when(first)
    def _():
        ref[...] = val

    @pl.when(jnp.logical_not(first))
    def _():
        ref[...] += val


def _norm_mod_bwd(x, g, shift, scale, dh, dx_res, name, dh2=None):
    t = x.shape[0]
    dhs = [dh] if dh2 is None else [dh, dh2]

    def body(x_ref, g_ref, sh_ref, sc_ref, dr_ref, *refs):
        dh_refs, (dx_ref, dg_ref, dsh_ref, dsc_ref) = refs[:len(dhs)], refs[len(dhs):]
        _, vjp = jax.vjp(_f_norm_mod, x_ref[...], g_ref[...], sh_ref[...], sc_ref[...])
        dh_all = dh_refs[0][...]
        for r in dh_refs[1:]:
            dh_all = dh_all + r[...]
        dx, dg, dsh, dsc = vjp(dh_all)
        dx_ref[...] = dx + dr_ref[...]
        first = pl.program_id(0) == 0
        _accum(dg_ref, dg, first)
        _accum(dsh_ref, dsh, first)
        _accum(dsc_ref, dsc, first)

    vec = jax.ShapeDtypeStruct((1, D), F32)
    return pl.pallas_call(
        body, name=name, grid=(t // TB_NORM,),
        in_specs=[_row_spec(D), _vec_spec(D), _vec_spec(D), _vec_spec(D), _row_spec(D)] + [_row_spec(D)] * len(dhs),
        out_specs=[_row_spec(D), _vec_spec(D), _vec_spec(D), _vec_spec(D)],
        out_shape=[jax.ShapeDtypeStruct((t, D), F32), vec, vec, vec], compiler_params=_cparams("arbitrary"),
    )(x, g, shift, scale, dx_res, *dhs)


def _resid_gate(x, y, gate, name):
    t = x.shape[0]

    def body(x_ref, y_ref, g_ref, o_ref):
        o_ref[...] = x_ref[...] + g_ref[...] * y_ref[...]

    return pl.pallas_call(
        body, name=name, grid=(t // TB_NORM,),
        in_specs=[_row_spec(D), _row_spec(D), _vec_spec(D)], out_specs=_row_spec(D),
        out_shape=jax.ShapeDtypeStruct((t, D), F32), compiler_params=_cparams("parallel"),
    )(x, y, gate)


def _gate_bwd(dx, y, gate, name):
    t = dx.shape[0]

    def body(dx_ref, y_ref, g_ref, dy_ref, dg_ref):
        dxv = dx_ref[...]
        dy_ref[...] = (dxv * g_ref[...]).astype(BF16)
        _accum(dg_ref, jnp.sum(dxv * y_ref[...], axis=0, keepdims=True), pl.program_id(0) == 0)

    return pl.pallas_call(
        body, name=name, grid=(t // TB_NORM,),
        in_specs=[_row_spec(D), _row_spec(D), _vec_spec(D)], out_specs=[_row_spec(D), _vec_spec(D)],
        out_shape=[jax.ShapeDtypeStruct((t, D), BF16), jax.ShapeDtypeStruct((1, D), F32)],
        compiler_params=_cparams("arbitrary"),
    )(dx, y, gate)


def _f_final(x, g, target):
    xn = x * lax.rsqrt(jnp.mean(x * x, axis=-1, keepdims=True) + RMS_EPS)
    err = xn * g - target
    return 0.5 * jnp.sum(jnp.mean(err * err, axis=-1, keepdims=True), axis=0, keepdims=True)


def _final_loss_grad(x, g, target, name):
    t = x.shape[0]

    def body(x_ref, g_ref, t_ref, loss_ref, dx_ref, dg_ref):
        loss, vjp = jax.vjp(_f_final, x_ref[...], g_ref[...], t_ref[...])
        dx, dg, _ = vjp(jnp.ones((1, 1), F32))
        dx_ref[...] = dx
        first = pl.program_id(0) == 0
        _accum(dg_ref, dg, first)
        _accum(loss_ref, jnp.broadcast_to(loss, (1, LANES)), first)

    return pl.pallas_call(
        body, name=name, grid=(t // TB_NORM,),
        in_specs=[_row_spec(D), _vec_spec(D), _row_spec(D)],
        out_specs=[_vec_spec(LANES), _row_spec(D), _vec_spec(D)],
        out_shape=[jax.ShapeDtypeStruct((1, LANES), F32), jax.ShapeDtypeStruct((t, D), F32), jax.ShapeDtypeStruct((1, D), F32)],
        compiler_params=_cparams("arbitrary"),
    )(x, g, target)


def _group_selector():
    gi = lax.broadcasted_iota(jnp.int32, (LANES, D), 0)
    ci = lax.broadcasted_iota(jnp.int32, (LANES, D), 1)
    return (ci // (D // SG_GROUPS) == gi).astype(F32)


def _f_sg(p, ln_g, ln_b, w_s, bs_t):
    u, v, z = p[:, :D], p[:, D:2 * D], p[:, 2 * D:]
    u = jax.nn.gelu(u)
    vf = jax.nn.gelu(v)
    mean = jnp.mean(vf, axis=-1, keepdims=True)
    var = jnp.mean(jnp.square(vf - mean), axis=-1, keepdims=True)
    vn = (vf - mean) * lax.rsqrt(var + LN_EPS) * ln_g + ln_b
    ti = lax.broadcasted_iota(jnp.int32, (CHUNK, CHUNK), 0)
    si = lax.broadcasted_iota(jnp.int32, (CHUNK, CHUNK), 1)
    causal = si <= ti
    cg = D // SG_GROUPS
    f = jnp.concatenate(
        [_bdot_nn(jnp.where(causal, w_s[g], 0.0), vn[:, g * cg:(g + 1) * cg]) for g in range(SG_GROUPS)], axis=1)
    f = f + jnp.dot(bs_t, _group_selector(), precision=HIGHEST, preferred_element_type=F32)
    return u * f * jax.nn.silu(z)


def _sg_specs():
    return [pl.BlockSpec((CHUNK, 3 * D), lambda i: (i, 0)), _vec_spec(D), _vec_spec(D),
            pl.BlockSpec((SG_GROUPS, CHUNK, CHUNK), lambda i: (0, 0, 0)), _vec_spec(LANES, CHUNK)]


def _sg_fwd(p, ln_g, ln_b, w_s, bs_t, name):
    t = p.shape[0]

    def body(p_ref, lg_ref, lb_ref, w_ref, b_ref, o_ref):
        o_ref[...] = _f_sg(p_ref[...], lg_ref[...], lb_ref[...], w_ref[...], b_ref[...]).astype(BF16)

    return pl.pallas_call(
        body, name=name, grid=(t // CHUNK,), in_specs=_sg_specs(), out_specs=_row_spec(D, CHUNK),
        out_shape=jax.ShapeDtypeStruct((t, D), BF16), compiler_params=_cparams("parallel"),
    )(p, ln_g, ln_b, w_s, bs_t)


def _sg_bwd(p, ln_g, ln_b, w_s, bs_t, dout, name):
    t = p.shape[0]

    def body(p_ref, lg_ref, lb_ref, w_ref, b_ref, do_ref, dp_ref, dlg_ref, dlb_ref, dw_ref, db_ref):
        _, vjp = jax.vjp(_f_sg, p_ref[...], lg_ref[...], lb_ref[...], w_ref[...], b_ref[...])
        dp, dlg, dlb, dw, db = vjp(do_ref[...])
        dp_ref[...] = dp.astype(BF16)
        first = pl.program_id(0) == 0
        _accum(dlg_ref, dlg, first)
        _accum(dlb_ref, dlb, first)
        _accum(dw_ref, dw, first)
        _accum(db_ref, db, first)

    vec = jax.ShapeDtypeStruct((1, D), F32)
    return pl.pallas_call(
        body, name=name, grid=(t // CHUNK,), in_specs=_sg_specs() + [_row_spec(D, CHUNK)],
        out_specs=[pl.BlockSpec((CHUNK, 3 * D), lambda i: (i, 0)), _vec_spec(D), _vec_spec(D),
                   pl.BlockSpec((SG_GROUPS, CHUNK, CHUNK), lambda i: (0, 0, 0)), _vec_spec(LANES, CHUNK)],
        out_shape=[jax.ShapeDtypeStruct((t, 3 * D), BF16), vec, vec,
                   jax.ShapeDtypeStruct((SG_GROUPS, CHUNK, CHUNK), F32), jax.ShapeDtypeStruct((CHUNK, LANES), F32)],
        compiler_params=_cparams("arbitrary"),
    )(p, ln_g, ln_b, w_s, bs_t, dout)


SWA_COLS = 2 * D + 2 * KVW
KV_BLOCK = 2 * KVW


def _lane_roll(x, shift):
    return pltpu.roll(x, shift, 1)


def _rot_half(x):
    w = x.shape[1]
    lane = lax.broadcasted_iota(jnp.int32, x.shape, 1)
    return jnp.where(lane % HEAD < HEAD // 2, -_lane_roll(x, w - HEAD // 2), _lane_roll(x, HEAD // 2))


@jax.custom_vjp
def _rope(x, cos, sin):
    return x * cos + _rot_half(x) * sin


def _rope_fwd(x, cos, sin):
    return _rope(x, cos, sin), (cos, sin)


def _rope_bwd(res, ct):
    cos, sin = res
    return ct * cos - _rot_half(ct) * sin, jnp.zeros_like(cos), jnp.zeros_like(sin)


_rope.defvjp(_rope_fwd, _rope_bwd)


@jax.custom_vjp
def _swap_halves(x):
    return _lane_roll(x, HEAD)


_swap_halves.defvjp(lambda x: (_lane_roll(x, HEAD), None), lambda _, ct: (_lane_roll(ct, HEAD),))


def _f_swa(pq, pkv, cos, sin, cosp, sinp, sink_row, valid):
    reps = D // LANES
    q = _rope(pq[:, :D], jnp.tile(cos, (1, reps)), jnp.tile(sin, (1, reps))) * (HEAD ** -0.5)
    k = _rope(pq[:, D:D + KVW], jnp.tile(cos, (1, KVW // LANES)), jnp.tile(sin, (1, KVW // LANES)))
    kp = _rope(pkv[:, :KVW], jnp.tile(cosp, (1, KVW // LANES)), jnp.tile(sinp, (1, KVW // LANES)))
    v, vp, z = pq[:, D + KVW:D + 2 * KVW], pkv[:, KVW:], pq[:, D + 2 * KVW:]
    kcat = jnp.concatenate([kp, k], axis=0)
    vcat = jnp.concatenate([vp, v], axis=0)
    lane = lax.broadcasted_iota(jnp.int32, (2 * CHUNK, LANES), 1)
    lo = lane < HEAD
    hlane = lax.broadcasted_iota(jnp.int32, (1, LANES), 1)

    def halves(cat, g):
        blk = cat[:, (g // 2) * LANES:(g // 2 + 1) * LANES]
        other = _swap_halves(blk)
        if g % 2 == 0:
            return jnp.where(lo, blk, 0.0), jnp.where(lo, 0.0, other)
        return jnp.where(lo, other, 0.0), jnp.where(lo, 0.0, blk)

    def probs(s, head):
        sink = jnp.sum(jnp.where(hlane == head, sink_row, 0.0), axis=1, keepdims=True)
        s = jnp.where(valid, s, NEG)
        m = lax.stop_gradient(jnp.maximum(jnp.max(s, axis=1, keepdims=True), sink))
        e = jnp.exp(s - m)
        return e / (jnp.sum(e, axis=1, keepdims=True) + jnp.exp(sink - m))

    outs = []
    rep = N_HEADS // KV_HEADS
    for g in range(KV_HEADS):
        k_lo, k_hi = halves(kcat, g)
        v_lo, v_hi = halves(vcat, g)
        for j in range(g * rep // 2, (g + 1) * rep // 2):
            qp = q[:, j * LANES:(j + 1) * LANES]
            p_a = probs(_bdot_nt(qp, k_lo), 2 * j)
            p_b = probs(_bdot_nt(qp, k_hi), 2 * j + 1)
            outs.append(_bdot_nn(p_a, v_lo) + _bdot_nn(p_b, v_hi))
    return jnp.concatenate(outs, axis=1) * jax.nn.silu(z)


def _swa_valid(block):
    qi = lax.broadcasted_iota(jnp.int32, (CHUNK, 2 * CHUNK), 0)
    kj = lax.broadcasted_iota(jnp.int32, (CHUNK, 2 * CHUNK), 1)
    rel = qi + CHUNK - kj
    return (rel >= 0) & (rel < CHUNK) & ((kj >= CHUNK) | (block > 0))


def _swa_specs(blk):
    prev = lambda i: jnp.maximum(blk(i) - 1, 0)
    kv_col = D // KV_BLOCK
    return [pl.BlockSpec((CHUNK, SWA_COLS), lambda i: (blk(i), 0)),
            pl.BlockSpec((CHUNK, KV_BLOCK), lambda i: (prev(i), kv_col)),
            pl.BlockSpec((CHUNK, LANES), lambda i: (blk(i), 0)), pl.BlockSpec((CHUNK, LANES), lambda i: (blk(i), 0)),
            pl.BlockSpec((CHUNK, LANES), lambda i: (prev(i), 0)), pl.BlockSpec((CHUNK, LANES), lambda i: (prev(i), 0)),
            _vec_spec(LANES)]


def _swa_fwd(p, cos, sin, sink_row, name):
    t = p.shape[0]

    def body(pq_ref, pkv_ref, c_ref, s_ref, cp_ref, sp_ref, sk_ref, o_ref):
        valid = _swa_valid(pl.program_id(0))
        o_ref[...] = _f_swa(pq_ref[...], pkv_ref[...], c_ref[...], s_ref[...], cp_ref[...], sp_ref[...],
                            sk_ref[...], valid).astype(BF16)

    return pl.pallas_call(
        body, name=name, grid=(t // CHUNK,), in_specs=_swa_specs(lambda i: i), out_specs=_row_spec(D, CHUNK),
        out_shape=jax.ShapeDtypeStruct((t, D), BF16), compiler_params=_cparams("parallel"),
    )(p, p, cos, sin, cos, sin, sink_row)


def _swa_bwd(p, cos, sin, sink_row, dout, name):
    t = p.shape[0]
    nb = t // CHUNK
    blk = lambda i: nb - 1 - i

    def body(pq_ref, pkv_ref, c_ref, s_ref, cp_ref, sp_ref, sk_ref, do_ref, dp_ref, dsk_ref, pend_ref):
        i = pl.program_id(0)
        valid = _swa_valid(blk(i))
        f = functools.partial(_f_swa, valid=valid)
        _, vjp = jax.vjp(f, pq_ref[...], pkv_ref[...], c_ref[...], s_ref[...], cp_ref[...], sp_ref[...], sk_ref[...])
        dpq, dpkv, _, _, _, _, dsk = vjp(do_ref[...])

        @pl.when(i == 0)
        def _():
            pend_ref[...] = jnp.zeros_like(pend_ref)

        dp_ref[...] = jnp.concatenate(
            [dpq[:, :D], dpq[:, D:D + KV_BLOCK] + pend_ref[...], dpq[:, D + KV_BLOCK:]], axis=1).astype(BF16)
        pend_ref[...] = dpkv
        _accum(dsk_ref, dsk, i == 0)

    return pl.pallas_call(
        body, name=name, grid=(nb,),
        in_specs=_swa_specs(blk) + [pl.BlockSpec((CHUNK, D), lambda i: (blk(i), 0))],
        out_specs=[pl.BlockSpec((CHUNK, SWA_COLS), lambda i: (blk(i), 0)), _vec_spec(LANES)],
        out_shape=[jax.ShapeDtypeStruct((t, SWA_COLS), BF16), jax.ShapeDtypeStruct((1, LANES), F32)],
        scratch_shapes=[pltpu.VMEM((CHUNK, KV_BLOCK), F32)], compiler_params=_cparams("arbitrary"),
    )(p, p, cos, sin, cos, sin, sink_row, dout)


RW_MAIN = 4 * D
RW_LO = 2 * LORA_PAD
VM = LANES // N_HEADS
VD = HEAD // VM
S_ROWS = VD * HEAD
TB_RW = 128
TB_K = 32
TB_SCAN = 16


def _to_k_layout(x):
    t = x.shape[0]
    return jnp.tile(x.reshape(t, N_HEADS, HEAD).transpose(0, 2, 1), (1, 1, VM))


def _from_k_layout(x4):
    t = x4.shape[0]
    return x4[:, :, :N_HEADS].transpose(0, 2, 1).reshape(t, D)


def _to_v_layout(x):
    t = x.shape[0]
    return x.reshape(t, N_HEADS, HEAD).transpose(0, 2, 1).reshape(t, VD, LANES)


def _from_v_layout(xv):
    t = xv.shape[0]
    return xv.reshape(t, HEAD, N_HEADS).transpose(0, 2, 1).reshape(t, D)


def _param_k_layout(w):
    return jnp.tile(w.reshape(N_HEADS, HEAD).T, (1, VM))


def _param_v_layout(w):
    return w.reshape(N_HEADS, HEAD).T.reshape(VD, LANES)


def _f_rwkv_lora(xs_lo, w0, a0, wl, al):
    decay = jnp.exp(-DECAY_SCALE * jax.nn.sigmoid(w0 + _bdot_nn(jnp.tanh(xs_lo[:, :LORA_PAD]), wl)))
    a = jax.nn.sigmoid(a0 + _bdot_nn(xs_lo[:, LORA_PAD:], al))
    return decay, a


def _prev_rows_spec(width, tb):
    return pl.BlockSpec((8, width), lambda i: (jnp.maximum(i * (tb // 8) - 1, 0), 0))


def _token_shift_lerp(p, prev8, mu, first):
    rows = lax.broadcasted_iota(jnp.int32, p.shape, 0)
    prev = jnp.where(first, 0.0, prev8[7:8, :])
    shifted = jnp.where(rows == 0, prev, pltpu.roll(p, 1, 0))
    return p + (shifted - p) * mu


def _rwkv_pre_fwd(p_main, p_lo, mu_main, mu_lo, w0, a0, wl, al, name):
    t = p_main.shape[0]
    tb = TB_RW

    def body(pm_ref, pmp_ref, pl_ref, plp_ref, mm_ref, ml_ref, w0_ref, a0_ref, wl_ref, al_ref,
             xm_ref, xl_ref, dec_ref, a_ref):
        first = pl.program_id(0) == 0
        xm_ref[...] = _token_shift_lerp(pm_ref[...], pmp_ref[...], mm_ref[...], first)
        xs_lo = _token_shift_lerp(pl_ref[...], plp_ref[...], ml_ref[...], first)
        xl_ref[...] = xs_lo
        dec_ref[...], a_ref[...] = _f_rwkv_lora(xs_lo, w0_ref[...], a0_ref[...], wl_ref[...], al_ref[...])

    return pl.pallas_call(
        body, name=name, grid=(t // tb,),
        in_specs=[_row_spec(RW_MAIN, tb), _prev_rows_spec(RW_MAIN, tb), _row_spec(RW_LO, tb), _prev_rows_spec(RW_LO, tb),
                  _vec_spec(RW_MAIN), _vec_spec(RW_LO), _vec_spec(D), _vec_spec(D),
                  _vec_spec(D, LORA_PAD), _vec_spec(D, LORA_PAD)],
        out_specs=[_row_spec(RW_MAIN, tb), _row_spec(RW_LO, tb), _row_spec(D, tb), _row_spec(D, tb)],
        out_shape=[jax.ShapeDtypeStruct((t, RW_MAIN), F32), jax.ShapeDtypeStruct((t, RW_LO), F32),
                   jax.ShapeDtypeStruct((t, D), F32), jax.ShapeDtypeStruct((t, D), F32)],
        compiler_params=_cparams("parallel"),
    )(p_main, p_main, p_lo, p_lo, mu_main, mu_lo, w0, a0, wl, al)


def _rwkv_lora_bwd(xs_lo, w0, a0, wl, al, ddecay, da, name):
    t = xs_lo.shape[0]
    tb = TB_NORM

    def body(x_ref, w0_ref, a0_ref, wl_ref, al_ref, dd_ref, da_ref, dx_ref, dw0_ref, da0_ref, dwl_ref, dal_ref):
        _, vjp = jax.vjp(_f_rwkv_lora, x_ref[...], w0_ref[...], a0_ref[...], wl_ref[...], al_ref[...])
        dx, dw0, da0, dwl, dal = vjp((dd_ref[...], da_ref[...]))
        dx_ref[...] = dx
        first = pl.program_id(0) == 0
        _accum(dw0_ref, dw0, first)
        _accum(da0_ref, da0, first)
        _accum(dwl_ref, dwl, first)
        _accum(dal_ref, dal, first)

    vec = jax.ShapeDtypeStruct((1, D), F32)
    lor = jax.ShapeDtypeStruct((LORA_PAD, D), F32)
    return pl.pallas_call(
        body, name=name, grid=(t // tb,),
        in_specs=[_row_spec(RW_LO), _vec_spec(D), _vec_spec(D), _vec_spec(D, LORA_PAD), _vec_spec(D, LORA_PAD),
                  _row_spec(D), _row_spec(D)],
        out_specs=[_row_spec(RW_LO), _vec_spec(D), _vec_spec(D), _vec_spec(D, LORA_PAD), _vec_spec(D, LORA_PAD)],
        out_shape=[jax.ShapeDtypeStruct((t, RW_LO), F32), vec, vec, lor, lor], compiler_params=_cparams("arbitrary"),
    )(xs_lo, w0, a0, wl, al, ddecay, da)


def _lerp_bwd(p, dxs, mu, name):
    t, width = p.shape
    tb = TB_RW
    nb = t // tb

    def body(p_ref, pp_ref, d_ref, dn_ref, mu_ref, dp_ref, dmu_ref):
        i = pl.program_id(0)
        pv, dv, mu_v = p_ref[...], d_ref[...], mu_ref[...]
        rows = lax.broadcasted_iota(jnp.int32, pv.shape, 0)
        prev = jnp.where(i == 0, 0.0, pp_ref[7:8, :])
        shifted = jnp.where(rows == 0, prev, pltpu.roll(pv, 1, 0))
        nxt = jnp.where(i == nb - 1, 0.0, dn_ref[0:1, :])
        d_next = jnp.where(rows == tb - 1, nxt, pltpu.roll(dv, tb - 1, 0))
        dp_ref[...] = (dv * (1.0 - mu_v) + d_next * mu_v).astype(BF16)
        _accum(dmu_ref, jnp.sum(dv * (shifted - pv), axis=0, keepdims=True), i == 0)

    return pl.pallas_call(
        body, name=name, grid=(nb,),
        in_specs=[_row_spec(width, tb), _prev_rows_spec(width, tb), _row_spec(width, tb),
                  pl.BlockSpec((8, width), lambda i: (jnp.minimum((i + 1) * (tb // 8), t // 8 - 1), 0)), _vec_spec(width)],
        out_specs=[_row_spec(width, tb), _vec_spec(width)],
        out_shape=[jax.ShapeDtypeStruct((t, width), BF16), jax.ShapeDtypeStruct((1, width), F32)],
        compiler_params=_cparams("arbitrary"),
    )(p, p, dxs, dxs, mu)


def _f_kprep(k, a, r, kkp, kap, rkp):
    kk = k * kkp
    kk = kk / jnp.maximum(jnp.sqrt(jnp.sum(kk * kk, axis=1, keepdims=True)), 1e-12)
    k2 = k * (1.0 + (a - 1.0) * kap)
    rk = jnp.sum(r * k2 * rkp, axis=1, keepdims=True)
    return kk, k2, kk * a, rk


def _k_spec(rows=HEAD, tb=TB_K):
    return pl.BlockSpec((tb, rows, LANES), lambda i: (i, 0, 0))


def _kparam_spec(rows=HEAD):
    return pl.BlockSpec((rows, LANES), lambda i: (0, 0))


def _rwkv_kprep_fwd(k4, a4, r4, kkp, kap, rkp, name):
    t = k4.shape[0]

    def body(k_ref, a_ref, r_ref, kkp_ref, kap_ref, rkp_ref, kk_ref, k2_ref, b_ref, rk_ref):
        kk_ref[...], k2_ref[...], b_ref[...], rk_ref[...] = _f_kprep(
            k_ref[...], a_ref[...], r_ref[...], kkp_ref[...], kap_ref[...], rkp_ref[...])

    big = jax.ShapeDtypeStruct((t, HEAD, LANES), F32)
    return pl.pallas_call(
        body, name=name, grid=(t // TB_K,),
        in_specs=[_k_spec(), _k_spec(), _k_spec(), _kparam_spec(), _kparam_spec(), _kparam_spec()],
        out_specs=[_k_spec(), _k_spec(), _k_spec(), _k_spec(1)],
        out_shape=[big, big, big, jax.ShapeDtypeStruct((t, 1, LANES), F32)], compiler_params=_cparams("parallel"),
    )(k4, a4, r4, kkp, kap, rkp)


def _rwkv_kprep_bwd(k4, a4, r4, kkp, kap, rkp, dkk, dk2, db, drk, dr_scan, dw_scan, name):
    t = k4.shape[0]

    def body(k_ref, a_ref, r_ref, kkp_ref, kap_ref, rkp_ref, dkk_ref, dk2_ref, db_ref, drk_ref, drs_ref, dws_ref,
             dk_ref, da_ref, dr_ref, dw_ref, dkkp_ref, dkap_ref, drkp_ref):
        _, vjp = jax.vjp(_f_kprep, k_ref[...], a_ref[...], r_ref[...], kkp_ref[...], kap_ref[...], rkp_ref[...])
        dk, da, dr, dkkp, dkap, drkp = vjp((_lane_group_sum(dkk_ref[...]), _lane_group_sum(dk2_ref[...]),
                                            _lane_group_sum(db_ref[...]), drk_ref[...]))
        dk_ref[...] = dk
        da_ref[...] = da
        dr_ref[...] = dr + _lane_group_sum(drs_ref[...])
        dw_ref[...] = _lane_group_sum(dws_ref[...])
        first = pl.program_id(0) == 0
        _accum(dkkp_ref, dkkp, first)
        _accum(dkap_ref, dkap, first)
        _accum(drkp_ref, drkp, first)

    big = jax.ShapeDtypeStruct((t, HEAD, LANES), F32)
    par = jax.ShapeDtypeStruct((HEAD, LANES), F32)
    return pl.pallas_call(
        body, name=name, grid=(t // TB_K,),
        in_specs=[_k_spec(), _k_spec(), _k_spec(), _kparam_spec(), _kparam_spec(), _kparam_spec(),
                  _k_spec(), _k_spec(), _k_spec(), _k_spec(1), _k_spec(), _k_spec()],
        out_specs=[_k_spec(), _k_spec(), _k_spec(), _k_spec(), _kparam_spec(), _kparam_spec(), _kparam_spec()],
        out_shape=[big, big, big, big, par, par, par], compiler_params=_cparams("arbitrary"),
    )(k4, a4, r4, kkp, kap, rkp, dkk, dk2, db, drk, dr_scan, dw_scan)


def _lane_group_sum2d(x):
    x = x + pltpu.roll(x, N_HEADS, 1)
    return x + pltpu.roll(x, 2 * N_HEADS, 1)


@jax.custom_vjp
def _lane_group_sum(x):
    return _lane_group_sum2d(x.reshape(-1, LANES)).reshape(x.shape)


_lane_group_sum.defvjp(lambda x: (_lane_group_sum(x), None), lambda _, ct: (_lane_group_sum(ct),))


def _f_post(y, v, rk, g, b):
    mean = _lane_group_sum(jnp.sum(y, axis=1, keepdims=True)) * (1.0 / HEAD)
    yc = y - mean
    var = _lane_group_sum(jnp.sum(yc * yc, axis=1, keepdims=True)) * (1.0 / HEAD)
    return yc * lax.rsqrt(var + GN_EPS) * g + b + rk * v


def _rwkv_post_fwd(y, v, rk, g, b, name):
    t = y.shape[0]

    def body(y_ref, v_ref, rk_ref, g_ref, b_ref, o_ref):
        o_ref[...] = _f_post(y_ref[...], v_ref[...], rk_ref[...], g_ref[...], b_ref[...])

    return pl.pallas_call(
        body, name=name, grid=(t // TB_K,),
        in_specs=[_k_spec(VD), _k_spec(VD), _k_spec(1), _kparam_spec(VD), _kparam_spec(VD)], out_specs=_k_spec(VD),
        out_shape=jax.ShapeDtypeStruct((t, VD, LANES), F32), compiler_params=_cparams("parallel"),
    )(y, v, rk, g, b)


def _rwkv_post_bwd(y, v, rk, g, b, do, name):
    t = y.shape[0]

    def body(y_ref, v_ref, rk_ref, g_ref, b_ref, do_ref, dy_ref, dv_ref, drk_ref, dg_ref, db_ref):
        _, vjp = jax.vjp(_f_post, y_ref[...], v_ref[...], rk_ref[...], g_ref[...], b_ref[...])
        dy, dv, drk, dg, db = vjp(do_ref[...])
        dy_ref[...] = dy
        dv_ref[...] = dv
        drk_ref[...] = _lane_group_sum(drk)
        first = pl.program_id(0) == 0
        _accum(dg_ref, dg, first)
        _accum(db_ref, db, first)

    vl = jax.ShapeDtypeStruct((t, VD, LANES), F32)
    par = jax.ShapeDtypeStruct((VD, LANES), F32)
    return pl.pallas_call(
        body, name=name, grid=(t // TB_K,),
        in_specs=[_k_spec(VD), _k_spec(VD), _k_spec(1), _kparam_spec(VD), _kparam_spec(VD), _k_spec(VD)],
        out_specs=[_k_spec(VD), _k_spec(VD), _k_spec(1), _kparam_spec(VD), _kparam_spec(VD)],
        out_shape=[vl, vl, jax.ShapeDtypeStruct((t, 1, LANES), F32), par, par], compiler_params=_cparams("arbitrary"),
    )(y, v, rk, g, b, do)


def _f_gate(o, z):
    return o * jax.nn.silu(z)


def _z_spec(tb=TB_NORM):
    return pl.BlockSpec((tb, D), lambda i: (i, 3))


def _rwkv_gate_fwd(o, xs_main, name):
    t = o.shape[0]

    def body(o_ref, z_ref, u_ref):
        u_ref[...] = _f_gate(o_ref[...], z_ref[...]).astype(BF16)

    return pl.pallas_call(
        body, name=name, grid=(t // TB_NORM,), in_specs=[_row_spec(D), _z_spec()], out_specs=_row_spec(D),
        out_shape=jax.ShapeDtypeStruct((t, D), BF16), compiler_params=_cparams("parallel"),
    )(o, xs_main)


def _rwkv_gate_bwd(o, xs_main, du, name):
    t = o.shape[0]

    def body(o_ref, z_ref, du_ref, do_ref, dz_ref):
        _, vjp = jax.vjp(_f_gate, o_ref[...], z_ref[...])
        do_ref[...], dz_ref[...] = vjp(du_ref[...])

    full = jax.ShapeDtypeStruct((t, D), F32)
    return pl.pallas_call(
        body, name=name, grid=(t // TB_NORM,), in_specs=[_row_spec(D), _z_spec(), _row_spec(D)],
        out_specs=[_row_spec(D), _row_spec(D)], out_shape=[full, full], compiler_params=_cparams("parallel"),
    )(o, xs_main, du)


def _colsum(x):
    return jnp.sum(x, axis=0, keepdims=True)


def _rwkv_scan_fwd(r4, w4, k24, kk4, b4, v, name):
    t = r4.shape[0]
    tb = TB_SCAN

    def body(r_ref, w_ref, k2_ref, kk_ref, b_ref, v_ref, y_ref, sall_ref, sa_ref, s_scr):
        @pl.when(pl.program_id(0) == 0)
        def _():
            s_scr[...] = jnp.zeros_like(s_scr)

        sall_ref[0] = s_scr[...]

        def step(tt, dst):
            kk = kk_ref[tt]
            sas = []
            for vd in range(VD):
                sa = _colsum(sall_ref[tt, pl.ds(vd * HEAD, HEAD), :] * kk)
                sa_ref[tt, pl.ds(vd, 1), :] = sa
                sas.append(sa)
            w, b, k2, r = w_ref[tt], b_ref[tt], k2_ref[tt], r_ref[tt]
            for vd in range(VD):
                rows = pl.ds(vd * HEAD, HEAD)
                s = sall_ref[tt, rows, :] * w - sas[vd] * b + v_ref[tt, pl.ds(vd, 1), :] * k2
                dst[rows, :] = s
                y_ref[tt, pl.ds(vd, 1), :] = _colsum(s * r)

        def loop_step(tt, carry):
            step(tt, sall_ref.at[tt + 1])
            return carry

        lax.fori_loop(0, tb - 1, loop_step, 0)
        step(tb - 1, s_scr)

    vl = jax.ShapeDtypeStruct((t, VD, LANES), F32)
    return pl.pallas_call(
        body, name=name, grid=(t // tb,),
        in_specs=[_k_spec(HEAD, tb)] * 5 + [_k_spec(VD, tb)],
        out_specs=[_k_spec(VD, tb), _k_spec(S_ROWS, tb), _k_spec(VD, tb)],
        out_shape=[vl, jax.ShapeDtypeStruct((t, S_ROWS, LANES), F32), vl],
        scratch_shapes=[pltpu.VMEM((S_ROWS, LANES), F32)], compiler_params=_cparams("arbitrary"),
    )(r4, w4, k24, kk4, b4, v)


def _rwkv_scan_bwd(dy, s_all, sa_all, r4, w4, k24, kk4, b4, v, name):
    t = r4.shape[0]
    tb = TB_SCAN
    nb = t // tb
    blk = lambda i: nb - 1 - i

    def body(dy_ref, sall_ref, sa_ref, r_ref, w_ref, k2_ref, kk_ref, b_ref, v_ref,
             dr_ref, dw_ref, dk2_ref, dkk_ref, db_ref, dv_ref, ds_scr):
        @pl.when(pl.program_id(0) == 0)
        def _():
            ds_scr[...] = jnp.zeros_like(ds_scr)

        def step(j, carry):
            tt = tb - 1 - j
            vrow = lambda ref, vd: ref[tt, pl.ds(vd, 1), :]
            srows = lambda vd: pl.ds(vd * HEAD, HEAD)
            r, k2, b = r_ref[tt], k2_ref[tt], b_ref[tt]
            dsas = []
            for vd in range(VD):
                ds = ds_scr[srows(vd), :] + vrow(dy_ref, vd) * r
                ds_scr[srows(vd), :] = ds
                dv_ref[tt, pl.ds(vd, 1), :] = _colsum(ds * k2)
                dsas.append(-_colsum(ds * b))
            zero = jnp.zeros((HEAD, LANES), F32)
            dk2, q, sady, vdy = zero, zero, 0.0, 0.0
            for vd in range(VD):
                dyv = vrow(dy_ref, vd)
                dk2 = dk2 + ds_scr[srows(vd), :] * vrow(v_ref, vd)
                q = q + sall_ref[tt, srows(vd), :] * dyv
                sady = sady + vrow(sa_ref, vd) * dyv
                vdy = vdy + vrow(v_ref, vd) * dyv
            dk2_ref[tt] = dk2
            dr_ref[tt] = w_ref[tt] * q - b_ref[tt] * sady + k2_ref[tt] * vdy
            dw, dkk = zero, zero
            for vd in range(VD):
                sp = sall_ref[tt, srows(vd), :]
                dw = dw + ds_scr[srows(vd), :] * sp
                dkk = dkk + sp * dsas[vd]
            dw_ref[tt] = dw
            dkk_ref[tt] = dkk
            w, kk = w_ref[tt], kk_ref[tt]
            db = zero
            for vd in range(VD):
                ds = ds_scr[srows(vd), :]
                db = db - ds * vrow(sa_ref, vd)
                ds_scr[srows(vd), :] = ds * w + dsas[vd] * kk
            db_ref[tt] = db
            return carry

        lax.fori_loop(0, tb, step, 0)

    rk = lambda rows: pl.BlockSpec((tb, rows, LANES), lambda i: (blk(i), 0, 0))
    big = jax.ShapeDtypeStruct((t, HEAD, LANES), F32)
    return pl.pallas_call(
        body, name=name, grid=(nb,),
        in_specs=[rk(VD), rk(S_ROWS), rk(VD)] + [rk(HEAD)] * 5 + [rk(VD)],
        out_specs=[rk(HEAD)] * 5 + [rk(VD)],
        out_shape=[big] * 5 + [jax.ShapeDtypeStruct((t, VD, LANES), F32)],
        scratch_shapes=[pltpu.VMEM((S_ROWS, LANES), F32)], compiler_params=_cparams("arbitrary"),
    )(dy, s_all, sa_all, r4, w4, k24, kk4, b4, v)


def _rwkv_mixer_fwd(p_main, p_lo, prm, tag):
    xs_main, xs_lo, decay, a = _rwkv_pre_fwd(p_main, p_lo, prm["mu_main"], prm["mu_lo"], prm["w0"], prm["a0"],
                                             prm["wl"], prm["al"], tag + "_pre")
    r4, k4, w4, a4 = (_to_k_layout(x) for x in (xs_main[:, :D], xs_main[:, D:2 * D], decay, a))
    v = _to_v_layout(xs_main[:, 2 * D:3 * D])
    kk4, k24, b4, rk = _rwkv_kprep_fwd(k4, a4, r4, prm["kkp"], prm["kap"], prm["rkp"], tag + "_kprep")
    y, s_all, sa_all = _rwkv_scan_fwd(r4, w4, k24, kk4, b4, v, tag + "_scan")
    o = _from_v_layout(_rwkv_post_fwd(y, v, rk, prm["gn_g"], prm["gn_b"], tag + "_post"))
    u = _rwkv_gate_fwd(o, xs_main, tag + "_gate")
    saved = dict(xs_main=xs_main, xs_lo=xs_lo, r4=r4, k4=k4, w4=w4, a4=a4, v=v, kk4=kk4, k24=k24, b4=b4, rk=rk,
                 y=y, s_all=s_all, sa_all=sa_all, o=o)
    return u, saved


def _rwkv_mixer_bwd(p_main, p_lo, prm, sv, du, tag):
    do, dz = _rwkv_gate_bwd(sv["o"], sv["xs_main"], du, tag + "_gate_b")
    dy, dv_post, drk, dgn_g, dgn_b = _rwkv_post_bwd(sv["y"], sv["v"], sv["rk"], prm["gn_g"], prm["gn_b"],
                                                    _to_v_layout(do), tag + "_post_b")
    dr_s, dw_s, dk24, dkk4, db4, dv_scan = _rwkv_scan_bwd(dy, sv["s_all"], sv["sa_all"], sv["r4"], sv["w4"], sv["k24"],
                                                          sv["kk4"], sv["b4"], sv["v"], tag + "_scan_b")
    dk4, da4, dr4, dw4, dkkp, dkap, drkp = _rwkv_kprep_bwd(sv["k4"], sv["a4"], sv["r4"], prm["kkp"], prm["kap"],
                                                           prm["rkp"], dkk4, dk24, db4, drk, dr_s, dw_s, tag + "_kprep_b")
    dxs_lo, dw0, da0, dwl, dal = _rwkv_lora_bwd(sv["xs_lo"], prm["w0"], prm["a0"], prm["wl"], prm["al"],
                                                _from_k_layout(dw4), _from_k_layout(da4), tag + "_lora_b")
    dxs_main = jnp.concatenate([_from_k_layout(dr4), _from_k_layout(dk4), _from_v_layout(dv_post + dv_scan), dz], axis=1)
    dp_main, dmu_main = _lerp_bwd(p_main, dxs_main, prm["mu_main"], tag + "_lerp_main_b")
    dp_lo, dmu_lo = _lerp_bwd(p_lo, dxs_lo, prm["mu_lo"], tag + "_lerp_lo_b")
    grads = dict(mu_main=dmu_main, mu_lo=dmu_lo, w0=dw0, a0=da0, wl=dwl, al=dal, kkp=dkkp, kap=dkap, rkp=drkp,
                 gn_g=dgn_g, gn_b=dgn_b)
    return dp_main, dp_lo, grads


N_DEV = 8
N_CHIPS = 4
ANY = pl.BlockSpec(memory_space=pl.ANY)


def _place():
    return lax.axis_index("x"), lax.axis_index("y"), lax.axis_index("c")


def _remote(src, dst, send_sems, recv_sems, k, dev):
    return pltpu.make_async_remote_copy(src_ref=src, dst_ref=dst, send_sem=send_sems.at[k], recv_sem=recv_sems.at[k],
                                        device_id=dev, device_id_type=MESHT)


def _all_gather8(v, name):
    def body(v_ref, out_ref, send_sems, recv_sems, local_sem):
        x, y, c = _place()
        me = 4 * x + 2 * y + c
        mine = pltpu.make_async_copy(v_ref, out_ref.at[me], local_sem)
        mine.start()
        peers = [(x ^ (k >> 2), y ^ ((k >> 1) & 1), c ^ (k & 1)) for k in range(1, N_DEV)]
        sends = [_remote(v_ref, out_ref.at[me], send_sems, recv_sems, k, peer) for k, peer in enumerate(peers)]
        for cp in sends:
            cp.start()
        for k, (px, py, pc) in enumerate(peers):
            _remote(v_ref, out_ref.at[4 * px + 2 * py + pc], send_sems, recv_sems, k, (x, y, c)).wait_recv()
        for cp in sends:
            cp.wait_send()
        mine.wait()

    return pl.pallas_call(
        body, name=name, in_specs=[ANY], out_specs=ANY,
        out_shape=jax.ShapeDtypeStruct((N_DEV,) + v.shape, v.dtype),
        scratch_shapes=[pltpu.SemaphoreType.DMA((N_DEV - 1,)), pltpu.SemaphoreType.DMA((N_DEV - 1,)),
                        pltpu.SemaphoreType.DMA(())],
    )(v)


def _other_chips(x, y):
    return [(1 - x, y), (x, 1 - y), (1 - x, 1 - y)]


def _chip_gather(v, name):
    def body(v_ref, out_ref, send_sems, recv_sems, local_sem):
        x, y, c = _place()
        me = 2 * x + y
        mine = pltpu.make_async_copy(v_ref, out_ref.at[me], local_sem)
        mine.start()
        chips = _other_chips(x, y)
        sends = [_remote(v_ref.at[c], out_ref.at[me, c], send_sems, recv_sems, j, (cx, cy, c))
                 for j, (cx, cy) in enumerate(chips)]
        for cp in sends:
            cp.start()
        for j, (cx, cy) in enumerate(chips):
            landed = out_ref.at[2 * cx + cy, c]
            _remote(v_ref.at[c], landed, send_sems, recv_sems, j, (x, y, c)).wait_recv()
            fwd = _remote(landed, landed, send_sems, recv_sems, 3 + j, (x, y, 1 - c))
            fwd.start()
            sends.append(fwd)
        for j, (cx, cy) in enumerate(chips):
            _remote(v_ref.at[c], out_ref.at[2 * cx + cy, 1 - c], send_sems, recv_sems, 3 + j, (x, y, c)).wait_recv()
        for cp in sends:
            cp.wait_send()
        mine.wait()

    return pl.pallas_call(
        body, name=name, in_specs=[ANY], out_specs=ANY,
        out_shape=jax.ShapeDtypeStruct((N_CHIPS,) + v.shape, v.dtype),
        scratch_shapes=[pltpu.SemaphoreType.DMA((6,)), pltpu.SemaphoreType.DMA((6,)), pltpu.SemaphoreType.DMA(())],
    )(v)


RS_W = 1024
RS_BLOCK_BYTES = 4 << 20


def _dma_sems(n):
    return [pltpu.SemaphoreType.DMA((n,)), pltpu.SemaphoreType.DMA((n,))]


def _rs_pair_exchange(gs, name):
    n = len(gs)

    def body(*refs):
        g_refs, got_refs, (send_sems, recv_sems) = refs[:n], refs[n:2 * n], refs[2 * n:]
        x, y, c = _place()
        sends = [_remote(g_refs[i].at[s, 1 - c], got_refs[i].at[s], send_sems, recv_sems, N_CHIPS * i + s, (x, y, 1 - c))
                 for i in range(n) for s in range(N_CHIPS)]
        for cp in sends:
            cp.start()
        for cp in sends:
            cp.wait()

    return pl.pallas_call(
        body, name=name, in_specs=[ANY] * n, out_specs=[ANY] * n,
        out_shape=[jax.ShapeDtypeStruct((N_CHIPS,) + g.shape[2:], g.dtype) for g in gs],
        scratch_shapes=_dma_sems(N_CHIPS * n),
    )(*gs)


def _rs_rows(rows, cols):
    cap = max(16, RS_BLOCK_BYTES // (N_CHIPS * 4 * cols))
    return rows if rows <= cap else max(d for d in range(16, cap + 1, 16) if rows % d == 0)


def _rs_pair_add(g, got, c_arr, name):
    _, _, rows, width = g.shape
    tr = _rs_rows(rows, width)

    def body(c_ref, g_ref, got_ref, p_ref):
        p_ref[...] = (g_ref[...] + got_ref[...]).astype(BF16)

    return pl.pallas_call(
        body, name=name,
        grid_spec=pltpu.PrefetchScalarGridSpec(
            num_scalar_prefetch=1, grid=(rows // tr,),
            in_specs=[pl.BlockSpec((N_CHIPS, None, tr, width), lambda i, c_ref: (0, c_ref[0], i, 0)),
                      pl.BlockSpec((N_CHIPS, tr, width), lambda i, c_ref: (0, i, 0))],
            out_specs=pl.BlockSpec((N_CHIPS, tr, width), lambda i, c_ref: (0, i, 0))),
        out_shape=jax.ShapeDtypeStruct((N_CHIPS, rows, width), BF16), compiler_params=_cparams("parallel"),
    )(c_arr, g, got)


def _rs_chip_exchange(ps, name):
    n = len(ps)

    def body(*refs):
        p_refs, q_refs, (send_sems, recv_sems, local_sems) = refs[:n], refs[n:2 * n], refs[2 * n:]
        x, y, c = _place()
        me = 2 * x + y
        chips = _other_chips(x, y)
        local = [pltpu.make_async_copy(p_refs[i].at[me], q_refs[i].at[me], local_sems.at[i]) for i in range(n)]
        sends = [_remote(p_refs[i].at[2 * cx + cy], q_refs[i].at[me], send_sems, recv_sems, 3 * i + j, (cx, cy, c))
                 for i in range(n) for j, (cx, cy) in enumerate(chips)]
        for cp in local + sends:
            cp.start()
        for i in range(n):
            for j, (cx, cy) in enumerate(chips):
                _remote(p_refs[i].at[me], q_refs[i].at[2 * cx + cy], send_sems, recv_sems, 3 * i + j, (x, y, c)).wait_recv()
        for cp in sends:
            cp.wait_send()
        for cp in local:
            cp.wait()

    return pl.pallas_call(
        body, name=name, in_specs=[ANY] * n, out_specs=[ANY] * n,
        out_shape=[jax.ShapeDtypeStruct(p.shape, p.dtype) for p in ps],
        scratch_shapes=_dma_sems(3 * n) + [pltpu.SemaphoreType.DMA((n,))],
    )(*ps)


def _rs_chip_add(q, name):
    _, rows, width = q.shape
    tr = _rs_rows(rows, width)

    def body(q_ref, r_ref):
        qv = q_ref[...].astype(F32)
        r_ref[...] = ((qv[0] + qv[1]) + qv[2]) + qv[3]

    return pl.pallas_call(
        body, name=name, grid=(rows // tr,),
        in_specs=[pl.BlockSpec((N_CHIPS, tr, width), lambda i: (0, i, 0))],
        out_specs=pl.BlockSpec((tr, width), lambda i: (i, 0)),
        out_shape=jax.ShapeDtypeStruct((rows, width), F32), compiler_params=_cparams("parallel"),
    )(q)


def _rs_pair_share(rs, name):
    n = len(rs)

    def body(*refs):
        r_refs, out_refs, (send_sems, recv_sems, local_sems) = refs[:n], refs[n:2 * n], refs[2 * n:]
        x, y, c = _place()
        local = [pltpu.make_async_copy(r_refs[i], out_refs[i].at[c], local_sems.at[i]) for i in range(n)]
        sends = [_remote(r_refs[i], out_refs[i].at[c], send_sems, recv_sems, i, (x, y, 1 - c)) for i in range(n)]
        for cp in local + sends:
            cp.start()
        for i in range(n):
            _remote(r_refs[i], out_refs[i].at[1 - c], send_sems, recv_sems, i, (x, y, c)).wait_recv()
        for cp in sends:
            cp.wait_send()
        for cp in local:
            cp.wait()

    return pl.pallas_call(
        body, name=name, in_specs=[ANY] * n, out_specs=[ANY] * n,
        out_shape=[jax.ShapeDtypeStruct((2,) + r.shape, r.dtype) for r in rs],
        scratch_shapes=_dma_sems(n) + [pltpu.SemaphoreType.DMA((n,))],
    )(*rs)


def _reduce_scatter(gs, c_arr, tag):
    gots = _rs_pair_exchange(gs, tag + "_pair_x")
    ps = [_rs_pair_add(g, got, c_arr, f"{tag}_pair_add{i}") for i, (g, got) in enumerate(zip(gs, gots))]
    qs = _rs_chip_exchange(ps, tag + "_chip_x")
    rs = [_rs_chip_add(q, f"{tag}_chip_add{i}") for i, q in enumerate(qs)]
    return _rs_pair_share(rs, tag + "_share")


def _sum8(a, name):
    _, rows, width = a.shape
    tr = 8 * (rows // 8 if rows <= 64 else 7)
    assert rows % tr == 0

    def body(a_ref, o_ref):
        acc = a_ref[0]
        for d in range(1, N_DEV):
            acc = acc + a_ref[d]
        o_ref[...] = acc

    return pl.pallas_call(
        body, name=name, grid=(rows // tr,), in_specs=[pl.BlockSpec((N_DEV, tr, width), lambda i: (0, i, 0))],
        out_specs=pl.BlockSpec((tr, width), lambda i: (i, 0)), out_shape=jax.ShapeDtypeStruct((rows, width), F32),
        compiler_params=_cparams("parallel"),
    )(a)


MOD_COLS = 3 * D // N_CHIPS
MOD_TK = 512


def _mod_partial(c_all, mod_w, name):
    nk = D // MOD_TK

    def body(c_ref, w_ref, o_ref):
        l = pl.program_id(1)
        part = _bdot_nn(jax.nn.silu(c_ref[...]), w_ref[0])
        _accum(o_ref.at[0], part, l == 0)

    return pl.pallas_call(
        body, name=name, grid=(DEPTH, nk),
        in_specs=[pl.BlockSpec((N_DEV, MOD_TK), lambda i, l: (0, l)), pl.BlockSpec((1, MOD_TK, MOD_COLS), lambda i, l: (i, l, 0))],
        out_specs=pl.BlockSpec((1, N_DEV, MOD_COLS), lambda i, l: (i, 0, 0)),
        out_shape=jax.ShapeDtypeStruct((DEPTH, N_DEV, MOD_COLS), F32), compiler_params=_cparams("parallel", "arbitrary"),
    )(c_all, mod_w)


def _mod_w_grad(c_all, dmod, name):
    def body(c_ref, d_ref, o_ref):
        o_ref[0] = _dg(jax.nn.silu(c_ref[...]).astype(BF16), d_ref[0].astype(BF16), _TN)

    return pl.pallas_call(
        body, name=name, grid=(DEPTH, D // MOD_TK),
        in_specs=[pl.BlockSpec((N_DEV, MOD_TK), lambda i, l: (0, l)), pl.BlockSpec((1, N_DEV, MOD_COLS), lambda i, l: (i, 0, 0))],
        out_specs=pl.BlockSpec((1, MOD_TK, MOD_COLS), lambda i, l: (i, l, 0)),
        out_shape=jax.ShapeDtypeStruct((DEPTH, D, MOD_COLS), F32), compiler_params=_cparams("parallel", "parallel"),
    )(c_all, dmod)


ADAM_BLOCK_BYTES = 1 << 20


def _adamw(w, g, m, v, name):
    shape = w.shape
    cols = shape[-1]
    rows = w.size // cols
    w, g, m, v = (a.reshape(rows, cols) for a in (w, g, m, v))
    cap = max(8, ADAM_BLOCK_BYTES // (4 * cols))
    tr = rows if rows <= cap else max(d for d in range(8, cap + 1, 8) if rows % d == 0)
    c1 = 1.0 - ADAM_B1 ** ADAM_STEP
    c2 = 1.0 - ADAM_B2 ** ADAM_STEP

    def body(w_ref, g_ref, m_ref, v_ref, d_ref, nm_ref, nv_ref):
        gv = g_ref[...]
        mn = ADAM_B1 * m_ref[...] + (1.0 - ADAM_B1) * gv
        vn = ADAM_B2 * v_ref[...] + (1.0 - ADAM_B2) * (gv * gv)
        nm_ref[...] = mn
        nv_ref[...] = vn
        d_ref[...] = -ADAM_LR * ((mn / c1) / (jnp.sqrt(vn / c2) + ADAM_EPS) + ADAM_WD * w_ref[...])

    spec = pl.BlockSpec((tr, cols), lambda i: (i, 0))
    out = jax.ShapeDtypeStruct((rows, cols), F32)
    d, nm, nv = pl.pallas_call(
        body, name=name, grid=(rows // tr,), in_specs=[spec] * 4, out_specs=[spec] * 3, out_shape=[out] * 3,
        compiler_params=_cparams("parallel"),
    )(w, g, m, v)
    return d.reshape(shape), nm.reshape(shape), nv.reshape(shape)


W_NAMES = ("norm_g", "mod_w", "mod_b", "final_norm_g", "sg_w_in", "sg_w_out", "sg_ln_g", "sg_ln_b", "sg_w_spatial",
           "sg_b_spatial", "swa_w_in", "swa_w_out", "swa_sinks", "rwkv_w_in", "rwkv_w_out", "rwkv_mu", "rwkv_w0",
           "rwkv_w_lora", "rwkv_a0", "rwkv_a_lora", "rwkv_k_k", "rwkv_k_a", "rwkv_r_k", "rwkv_gn_g", "rwkv_gn_b")
SMALL = {"sg_ln_g": 1, "sg_ln_b": 1, "rwkv_mu": 1, "rwkv_w0": 1, "rwkv_w_lora": 2, "rwkv_a0": 1, "rwkv_a_lora": 2,
         "rwkv_k_k": 1, "rwkv_k_a": 1, "rwkv_gn_g": 1, "rwkv_gn_b": 1}
REPLICATED = ("norm_g", "final_norm_g", "sg_w_spatial", "sg_b_spatial", "swa_sinks", "rwkv_r_k")
KINDS = ("sg", "swa", "rwkv", "sg")


def _pad_to(flat, n):
    return jnp.pad(flat, (0, n - flat.shape[0]))


def _round_up(n, m):
    return -(-n // m) * m


def _join_shards(gathered, axis):
    return jnp.concatenate([gathered[s] for s in range(N_CHIPS)], axis=axis)


def _chip_blocks(full, axis):
    return jnp.stack(jnp.split(full, N_CHIPS, axis=axis)).reshape(N_CHIPS, -1)


def _gather_big(w, name):
    rows, cols = w.shape
    return _chip_gather(w.astype(BF16).reshape(2, rows // 2, cols), name).reshape(N_CHIPS, rows, cols)


def _gather_small(shards, name):
    flat = jnp.concatenate([shards[n].reshape(-1) for n in SMALL])
    rows = _round_up(flat.shape[0], 2 * 8 * LANES) // (2 * LANES)
    got = _chip_gather(_pad_to(flat, 2 * rows * LANES).reshape(2, rows, LANES), name).reshape(N_CHIPS, -1)
    out, off = {}, 0
    for n, axis in SMALL.items():
        size = shards[n].size
        out[n] = _join_shards(got[:, off:off + size].reshape((N_CHIPS,) + shards[n].shape), axis)
        off += size
    return out


def _lora_pad_rows(w):
    return jnp.pad(w, ((0, LORA_PAD - LORA), (0, 0)))


def _lo_cols(a):
    z = jnp.zeros(a.shape[:-1] + (LORA_PAD - LORA,), a.dtype)
    return jnp.concatenate([a[..., :LORA], z, a[..., LORA:], z], axis=-1)


def _lo_cols_inv(a):
    return jnp.concatenate([a[..., :LORA], a[..., LORA_PAD:LORA_PAD + LORA]], axis=-1)


def _k_param_inv(p4):
    return p4[:, :N_HEADS].T.reshape(-1)


def _v_param_inv(pv):
    return pv.reshape(HEAD, N_HEADS).T.reshape(-1)


def kernel(x, c, positions, norm_g, mod_w, mod_b, final_norm_g, sg_w_in, sg_w_out, sg_ln_g, sg_ln_b, sg_w_spatial,
           sg_b_spatial, swa_w_in, swa_w_out, swa_sinks, rwkv_w_in, rwkv_w_out, rwkv_mu, rwkv_w0, rwkv_w_lora, rwkv_a0,
           rwkv_a_lora, rwkv_k_k, rwkv_k_a, rwkv_r_k, rwkv_gn_g, rwkv_gn_b, loss_target, m_norm_g, m_mod_w, m_mod_b,
           m_final_norm_g, m_sg_w_in, m_sg_w_out, m_sg_ln_g, m_sg_ln_b, m_sg_w_spatial, m_sg_b_spatial, m_swa_w_in,
           m_swa_w_out, m_swa_sinks, m_rwkv_w_in, m_rwkv_w_out, m_rwkv_mu, m_rwkv_w0, m_rwkv_w_lora, m_rwkv_a0,
           m_rwkv_a_lora, m_rwkv_k_k, m_rwkv_k_a, m_rwkv_r_k, m_rwkv_gn_g, m_rwkv_gn_b, v_norm_g, v_mod_w, v_mod_b,
           v_final_norm_g, v_sg_w_in, v_sg_w_out, v_sg_ln_g, v_sg_ln_b, v_sg_w_spatial, v_sg_b_spatial, v_swa_w_in,
           v_swa_w_out, v_swa_sinks, v_rwkv_w_in, v_rwkv_w_out, v_rwkv_mu, v_rwkv_w0, v_rwkv_w_lora, v_rwkv_a0,
           v_rwkv_a_lora, v_rwkv_k_k, v_rwkv_k_a, v_rwkv_r_k, v_rwkv_gn_g, v_rwkv_gn_b):
    given = dict(locals())
    w = {n: given[n] for n in W_NAMES}
    xi, yi, ci = _place()
    chip = 2 * xi + yi
    me = 4 * xi + 2 * yi + ci
    c_arr = ci.astype(jnp.int32).reshape(1)
    xs = [x[0]]

    c_all = _all_gather8(c, "gather_c")[:, 0, :]
    mod_part = _mod_partial(c_all, mod_w, "mod_fwd")
    mod_all = _all_gather8(mod_part, "gather_mod")[::2]
    mod_mine = lax.dynamic_index_in_dim(mod_all, me, axis=2, keepdims=False)
    mod = mod_mine.transpose(1, 0, 2).reshape(DEPTH, 3 * D) + mod_b
    shift, scale, gate = mod[:, :D], mod[:, D:2 * D], mod[:, 2 * D:]

    sg_in = [_gather_big(sg_w_in[j], f"gather_sg_w_in{j}") for j in range(2)]
    sg_out = [_gather_big(sg_w_out[j], f"gather_sg_w_out{j}").reshape(D, D) for j in range(2)]
    swa_in = _gather_big(swa_w_in[0], "gather_swa_w_in")
    swa_out = _gather_big(swa_w_out[0], "gather_swa_w_out").reshape(D, D)
    rw_in = _join_shards(_gather_big(rwkv_w_in[0], "gather_rwkv_w_in"), axis=1)
    rw_out = _gather_big(rwkv_w_out[0], "gather_rwkv_w_out").reshape(D, D)
    full = _gather_small(w, "gather_small")
    rw_main, rw_lo = rw_in[:, :RW_MAIN], _lo_cols(rw_in[:, RW_MAIN:])
    mu = full["rwkv_mu"][0]
    rw_prm = dict(mu_main=mu[None, :RW_MAIN], mu_lo=_lo_cols(mu[None, RW_MAIN:]), w0=full["rwkv_w0"], a0=full["rwkv_a0"],
                  wl=_lora_pad_rows(full["rwkv_w_lora"][0]), al=_lora_pad_rows(full["rwkv_a_lora"][0]),
                  kkp=_param_k_layout(full["rwkv_k_k"][0]), kap=_param_k_layout(full["rwkv_k_a"][0]),
                  rkp=_param_k_layout(rwkv_r_k.reshape(-1)),
                  gn_g=_param_v_layout(full["rwkv_gn_g"][0]), gn_b=_param_v_layout(full["rwkv_gn_b"][0]))
    bs_t = [jnp.pad(sg_b_spatial[j].T, ((0, 0), (0, LANES - SG_GROUPS))) for j in range(2)]
    sink_row = jnp.pad(swa_sinks, ((0, 0), (0, LANES - N_HEADS)))
    inv_freq = ROPE_THETA ** (-jnp.arange(HEAD // 2, dtype=F32) / (HEAD // 2))
    ang = positions[0].astype(F32)[:, None] * inv_freq
    cos, sin = jnp.tile(jnp.cos(ang), (1, LANES * 2 // HEAD)), jnp.tile(jnp.sin(ang), (1, LANES * 2 // HEAD))

    def row(a, i):
        return a[i:i + 1]

    hs, ps, us, ys, rw_saved = [], [], [], [], None
    for i, kind in enumerate(KINDS):
        j = i // 3
        tag = f"l{i}_{kind}"
        h = _norm_mod_fwd(xs[i], row(norm_g, i), row(shift, i), row(scale, i), tag + "_norm")
        if kind == "sg":
            p = _matmul(h, sg_in[j], "nn", tag + "_in", blocked=True)
            u = _sg_fwd(p, row(full["sg_ln_g"], j), row(full["sg_ln_b"], j), sg_w_spatial[j], bs_t[j], tag + "_mix")
            w_out = sg_out[j]
        elif kind == "swa":
            p = _matmul(h, swa_in, "nn", tag + "_in", blocked=True)
            u = _swa_fwd(p, cos, sin, sink_row, tag + "_mix")
            w_out = swa_out
        else:
            p = (_matmul(h, rw_main, "nn", tag + "_in"), _matmul(h, rw_lo, "nn", tag + "_in_lo"))
            u, rw_saved = _rwkv_mixer_fwd(p[0], p[1], rw_prm, tag)
            w_out = rw_out
        y = _matmul(u, w_out, "nn", tag + "_out")
        xs.append(_resid_gate(xs[i], y, row(gate, i), tag + "_resid"))
        hs.append(h), ps.append(p), us.append(u), ys.append(y)

    loss_part, dx, d_final_g = _final_loss_grad(xs[DEPTH], final_norm_g[None], loss_target[0], "loss")
    loss = lax.psum(loss_part[0, 0], ("x", "y", "c"))

    gfull = {n: [None, None] for n in ("sg_ln_g", "sg_ln_b", "sg_w_spatial", "sg_b_spatial")}
    gbig = {}
    d_norm_g, d_mod = [None] * DEPTH, [None] * DEPTH
    for i in reversed(range(DEPTH)):
        kind, j = KINDS[i], i // 3
        tag = f"l{i}_{kind}_b"
        dy, d_gate = _gate_bwd(dx, ys[i], row(gate, i), tag + "_gate")
        w_out = {"sg": sg_out[j], "swa": swa_out, "rwkv": rw_out}[kind]
        du = _matmul(dy, w_out, "nt", tag + "_du")
        dw_out = _matmul(us[i], dy, "tn", tag + "_dwout").reshape(N_CHIPS, D // N_CHIPS, D)
        if kind == "sg":
            dp, dlg, dlb, dws, dbs = _sg_bwd(ps[i], row(full["sg_ln_g"], j), row(full["sg_ln_b"], j), sg_w_spatial[j],
                                             bs_t[j], du, tag + "_mix")
            gfull["sg_ln_g"][j], gfull["sg_ln_b"][j] = dlg[0], dlb[0]
            gfull["sg_w_spatial"][j], gfull["sg_b_spatial"][j] = dws, dbs[:, :SG_GROUPS].T
            gbig[f"sg_w_in{j}"] = _matmul(hs[i], dp, "tn", tag + "_dwin", blocked=True)
            gbig[f"sg_w_out{j}"] = dw_out
            dh, dh2 = _matmul(dp, sg_in[j], "nt", tag + "_dh", blocked=True), None
        elif kind == "swa":
            dp, dsk = _swa_bwd(ps[i], cos, sin, sink_row, du, tag + "_mix")
            gfull["swa_sinks"] = dsk[:, :N_HEADS]
            gbig["swa_w_in"] = _matmul(hs[i], dp, "tn", tag + "_dwin", blocked=True)
            gbig["swa_w_out"] = dw_out
            dh, dh2 = _matmul(dp, swa_in, "nt", tag + "_dh", blocked=True), None
        else:
            dpm, dpl, rg = _rwkv_mixer_bwd(ps[i][0], ps[i][1], rw_prm, rw_saved, du, tag)
            dw_main = _matmul(hs[i], dpm, "tn", tag + "_dwin")
            dw_lo = _matmul(hs[i], dpl, "tn", tag + "_dwin_lo")
            dw_in = jnp.concatenate([dw_main, _lo_cols_inv(dw_lo)], axis=1)
            gbig["rwkv_w_in"] = dw_in.reshape(D, N_CHIPS, -1).transpose(1, 0, 2)
            gbig["rwkv_w_out"] = dw_out
            gfull["rwkv_mu"] = jnp.concatenate([rg["mu_main"], _lo_cols_inv(rg["mu_lo"])], axis=1)
            gfull["rwkv_w0"], gfull["rwkv_a0"] = rg["w0"], rg["a0"]
            gfull["rwkv_w_lora"], gfull["rwkv_a_lora"] = rg["wl"][None, :LORA], rg["al"][None, :LORA]
            gfull["rwkv_k_k"], gfull["rwkv_k_a"] = _k_param_inv(rg["kkp"])[None], _k_param_inv(rg["kap"])[None]
            gfull["rwkv_r_k"] = _k_param_inv(rg["rkp"]).reshape(1, N_HEADS, HEAD)
            gfull["rwkv_gn_g"], gfull["rwkv_gn_b"] = _v_param_inv(rg["gn_g"])[None], _v_param_inv(rg["gn_b"])[None]
            dh, dh2 = _matmul(dpm, rw_main, "nt", tag + "_dh"), _matmul(dpl, rw_lo, "nt", tag + "_dh_lo")
        dx, dg, dsh, dsc = _norm_mod_bwd(xs[i], row(norm_g, i), row(shift, i), row(scale, i), dh, dx, tag + "_norm", dh2)
        d_norm_g[i] = dg[0]
        d_mod[i] = jnp.concatenate([dsh[0], dsc[0], d_gate[0]])
    for n in ("sg_ln_g", "sg_ln_b", "sg_w_spatial", "sg_b_spatial"):
        gfull[n] = jnp.stack(gfull[n])
    gfull["norm_g"], gfull["final_norm_g"] = jnp.stack(d_norm_g), d_final_g[0]

    small = jnp.concatenate([_chip_blocks(gfull[n], axis) for n, axis in SMALL.items()], axis=1)
    small_rows = _round_up(small.shape[1], 2 * 16 * LANES) // (2 * LANES)
    small = jnp.pad(small, ((0, 0), (0, 2 * small_rows * LANES - small.shape[1])))
    rs_names = sorted(gbig)
    rs_in = [gbig[n].reshape(N_CHIPS, 2, gbig[n].shape[1] // 2, gbig[n].shape[2]) for n in rs_names]
    rs_out = _reduce_scatter(rs_in + [small.reshape(N_CHIPS, 2, small_rows, LANES)], c_arr, "rs")
    red = {n: r.reshape(-1, r.shape[2]) for n, r in zip(rs_names, rs_out)}
    grads = {"sg_w_in": jnp.stack([red["sg_w_in0"], red["sg_w_in1"]]), "sg_w_out": jnp.stack([red["sg_w_out0"], red["sg_w_out1"]])}
    for n in ("swa_w_in", "swa_w_out", "rwkv_w_in", "rwkv_w_out"):
        grads[n] = red[n][None]
    small_red, off = rs_out[-1].reshape(-1), 0
    for n in SMALL:
        grads[n] = small_red[off:off + w[n].size].reshape(w[n].shape)
        off += w[n].size

    rep_flat = jnp.concatenate([jnp.stack(d_mod).reshape(-1)] + [gfull[n].reshape(-1) for n in REPLICATED])
    rep_rows = _round_up(rep_flat.shape[0], 56 * RS_W) // RS_W
    rep_all = _all_gather8(_pad_to(rep_flat, rep_rows * RS_W).reshape(rep_rows, RS_W), "gather_rep")
    rep_sum = _sum8(rep_all, "sum_rep").reshape(-1)
    grads["mod_b"] = rep_sum[:DEPTH * 3 * D].reshape(DEPTH, 3 * D)
    off = DEPTH * 3 * D
    for n in REPLICATED:
        grads[n] = rep_sum[off:off + w[n].size].reshape(w[n].shape)
        off += w[n].size
    dmod_all = rep_all.reshape(N_DEV, -1)[:, :DEPTH * 3 * D].reshape(N_DEV, DEPTH, 3 * D)
    dmod_cols = lax.dynamic_slice_in_dim(dmod_all, chip * MOD_COLS, MOD_COLS, axis=2).transpose(1, 0, 2)
    grads["mod_w"] = _mod_w_grad(c_all, dmod_cols, "mod_w_grad")

    deltas, new_m, new_v = {}, {}, {}
    for n in W_NAMES:
        deltas[n], new_m[n], new_v[n] = _adamw(w[n], grads[n], given["m_" + n], given["v_" + n], "adamw_" + n)
    return (loss, dx[None], *[grads[n] for n in W_NAMES], *[deltas[n] for n in W_NAMES],
            *[new_m[n] for n in W_NAMES], *[new_v[n] for n in W_NAMES])
```

```python
import functools
import math

import jax
import jax.numpy as jnp
from jax import lax
from jax.experimental import pallas as pl
from jax.experimental.pallas import tpu as pltpu

F32 = jnp.float32
BF16 = jnp.bfloat16
HIGHEST = lax.Precision.HIGHEST

D = 2048
DEPTH = 4
CHUNK = 128
SG_GROUPS = 16
HEAD = 64
N_HEADS = D // HEAD
KV_HEADS = 4
KVW = KV_HEADS * HEAD
ROPE_THETA = 10000.0
LORA = 96
LORA_PAD = 128
DECAY_SCALE = math.exp(-0.5)
GN_EPS = 64e-5
RMS_EPS = 1e-6
LN_EPS = 1e-5
ADAM_LR, ADAM_B1, ADAM_B2, ADAM_EPS, ADAM_WD, ADAM_STEP = 0.001, 0.9, 0.999, 1e-08, 0.01, 10
LANES = 128
SUB = 8
NEG = -1e30
VMEM_LIMIT = 56 * 1024 * 1024

MESHT = pl.DeviceIdType.MESH


def _cparams(*sem):
    return pltpu.CompilerParams(dimension_semantics=sem, vmem_limit_bytes=VMEM_LIMIT)


_NN = (((1,), (0,)), ((), ()))
_NT = (((1,), (1,)), ((), ()))
_TN = (((0,), (0,)), ((), ()))


def _dg(a, b, dims):
    return lax.dot_general(a, b, dims, preferred_element_type=F32)


@jax.custom_vjp
def _bdot_nn(a, b):
    return _dg(a.astype(BF16), b.astype(BF16), _NN)


def _bdot_nn_fwd(a, b):
    a, b = a.astype(BF16), b.astype(BF16)
    return _dg(a, b, _NN), (a, b)


def _bdot_nn_bwd(res, ct):
    a, b = res
    ct = ct.astype(BF16)
    return _dg(ct, b, _NT), _dg(a, ct, _TN)


_bdot_nn.defvjp(_bdot_nn_fwd, _bdot_nn_bwd)


@jax.custom_vjp
def _bdot_nt(a, b):
    return _dg(a.astype(BF16), b.astype(BF16), _NT)


def _bdot_nt_fwd(a, b):
    a, b = a.astype(BF16), b.astype(BF16)
    return _dg(a, b, _NT), (a, b)


def _bdot_nt_bwd(res, ct):
    a, b = res
    ct = ct.astype(BF16)
    return _dg(ct, b, _NN), _dg(ct, a, _TN)


_bdot_nt.defvjp(_bdot_nt_fwd, _bdot_nt_bwd)


def _tile(n, cap):
    if n <= cap:
        return n
    return max(d for d in range(LANES, cap + 1, LANES) if n % d == 0)


def _matmul(a, b, form, name, out_dtype=F32, blocked=False, tm=1024, tn=512, tk=4096):
    if form == "nn":
        (m, k), n = a.shape, (N_CHIPS * b.shape[2] if blocked else b.shape[1])
    elif form == "nt":
        m, k, n = a.shape[0], a.shape[1], (b.shape[1] if blocked else b.shape[0])
    else:
        (k, m), n = a.shape, b.shape[1]
    per_chip = (k if form == "nt" else n) // N_CHIPS
    if blocked and form == "nt":
        tk = _tile(per_chip, tk)
    elif blocked:
        tn = _tile(per_chip, tn)
    tm, tn, tk = _tile(m, tm), _tile(n, tn), _tile(k, tk)
    assert m % tm == 0 and n % tn == 0 and k % tk == 0, (name, a.shape, b.shape)
    nk = k // tk
    dims = {"nn": _NN, "nt": _NT, "tn": _TN}[form]
    a_spec = pl.BlockSpec((tk, tm), lambda i, j, l: (l, i)) if form == "tn" else pl.BlockSpec((tm, tk), lambda i, j, l: (i, l))
    b_spec = pl.BlockSpec((tn, tk), lambda i, j, l: (j, l)) if form == "nt" else pl.BlockSpec((tk, tn), lambda i, j, l: (l, j))
    o_spec = pl.BlockSpec((tm, tn), lambda i, j, l: (i, j))
    o_shape = (m, n)
    if blocked and form == "nn":
        pc = per_chip // tn
        b_spec = pl.BlockSpec((None, tk, tn), lambda i, j, l: (j // pc, l, j % pc))
    elif blocked and form == "nt":
        pc = per_chip // tk
        b_spec = pl.BlockSpec((None, tn, tk), lambda i, j, l: (l // pc, j, l % pc))
    elif blocked:
        pc = per_chip // tn
        o_spec = pl.BlockSpec((None, tm, tn), lambda i, j, l: (j // pc, i, j % pc))
        o_shape = (N_CHIPS, m, per_chip)

    def body(a_ref, b_ref, o_ref, acc_ref):
        part = _dg(a_ref[...], b_ref[...], dims)
        if nk == 1:
            o_ref[...] = part.astype(out_dtype)
        else:
            l = pl.program_id(2)

            @pl.when(l == 0)
            def _():
                acc_ref[...] = part

            @pl.when(l > 0)
            def _():
                acc_ref[...] += part

            @pl.when(l == nk - 1)
            def _():
                o_ref[...] = acc_ref[...].astype(out_dtype)

    return pl.pallas_call(
        body, name=name, grid=(m // tm, n // tn, nk),
        in_specs=[a_spec, b_spec], out_specs=o_spec, out_shape=jax.ShapeDtypeStruct(o_shape, out_dtype),
        scratch_shapes=[pltpu.VMEM((tm, tn) if nk > 1 else (8, LANES), F32)],
        compiler_params=_cparams("parallel", "parallel", "arbitrary"),
    )(a, b)


TB_NORM = 256


def _f_norm_mod(x, g, shift, scale):
    xn = x * lax.rsqrt(jnp.mean(x * x, axis=-1, keepdims=True) + RMS_EPS)
    return (xn * g) * (1.0 + scale) + shift


def _row_spec(width, tb=TB_NORM):
    return pl.BlockSpec((tb, width), lambda i: (i, 0))


def _vec_spec(width, rows=1):
    return pl.BlockSpec((rows, width), lambda i: (0, 0))


def _norm_mod_fwd(x, g, shift, scale, name):
    t = x.shape[0]

    def body(x_ref, g_ref, sh_ref, sc_ref, h_ref):
        h_ref[...] = _f_norm_mod(x_ref[...], g_ref[...], sh_ref[...], sc_ref[...]).astype(BF16)

    return pl.pallas_call(
        body, name=name, grid=(t // TB_NORM,),
        in_specs=[_row_spec(D), _vec_spec(D), _vec_spec(D), _vec_spec(D)], out_specs=_row_spec(D),
        out_shape=jax.ShapeDtypeStruct((t, D), BF16), compiler_params=_cparams("parallel"),
    )(x, g, shift, scale)


def _accum(ref, val, first):
    @pl.when(first)
    def _():
        ref[...] = val

    @pl.when(jnp.logical_not(first))
    def _():
        ref[...] += val


def _norm_mod_bwd(x, g, shift, scale, dh, dx_res, name, dh2=None):
    t = x.shape[0]
    dhs = [dh] if dh2 is None else [dh, dh2]

    def body(x_ref, g_ref, sh_ref, sc_ref, dr_ref, *refs):
        dh_refs, (dx_ref, dg_ref, dsh_ref, dsc_ref) = refs[:len(dhs)], refs[len(dhs):]
        _, vjp = jax.vjp(_f_norm_mod, x_ref[...], g_ref[...], sh_ref[...], sc_ref[...])
        dh_all = dh_refs[0][...]
        for r in dh_refs[1:]:
            dh_all = dh_all + r[...]
        dx, dg, dsh, dsc = vjp(dh_all)
        dx_ref[...] = dx + dr_ref[...]
        first = pl.program_id(0) == 0
        _accum(dg_ref, dg, first)
        _accum(dsh_ref, dsh, first)
        _accum(dsc_ref, dsc, first)

    vec = jax.ShapeDtypeStruct((1, D), F32)
    return pl.pallas_call(
        body, name=name, grid=(t // TB_NORM,),
        in_specs=[_row_spec(D), _vec_spec(D), _vec_spec(D), _vec_spec(D), _row_spec(D)] + [_row_spec(D)] * len(dhs),
        out_specs=[_row_spec(D), _vec_spec(D), _vec_spec(D), _vec_spec(D)],
        out_shape=[jax.ShapeDtypeStruct((t, D), F32), vec, vec, vec], compiler_params=_cparams("arbitrary"),
    )(x, g, shift, scale, dx_res, *dhs)


def _resid_gate(x, y, gate, name):
    t = x.shape[0]

    def body(x_ref, y_ref, g_ref, o_ref):
        o_ref[...] = x_ref[...] + g_ref[...] * y_ref[...]

    return pl.pallas_call(
        body, name=name, grid=(t // TB_NORM,),
        in_specs=[_row_spec(D), _row_spec(D), _vec_spec(D)], out_specs=_row_spec(D),
        out_shape=jax.ShapeDtypeStruct((t, D), F32), compiler_params=_cparams("parallel"),
    )(x, y, gate)


def _gate_bwd(dx, y, gate, name):
    t = dx.shape[0]

    def body(dx_ref, y_ref, g_ref, dy_ref, dg_ref):
        dxv = dx_ref[...]
        dy_ref[...] = (dxv * g_ref[...]).astype(BF16)
        _accum(dg_ref, jnp.sum(dxv * y_ref[...], axis=0, keepdims=True), pl.program_id(0) == 0)

    return pl.pallas_call(
        body, name=name, grid=(t // TB_NORM,),
        in_specs=[_row_spec(D), _row_spec(D), _vec_spec(D)], out_specs=[_row_spec(D), _vec_spec(D)],
        out_shape=[jax.ShapeDtypeStruct((t, D), BF16), jax.ShapeDtypeStruct((1, D), F32)],
        compiler_params=_cparams("arbitrary"),
    )(dx, y, gate)


def _f_final(x, g, target):
    xn = x * lax.rsqrt(jnp.mean(x * x, axis=-1, keepdims=True) + RMS_EPS)
    err = xn * g - target
    return 0.5 * jnp.sum(jnp.mean(err * err, axis=-1, keepdims=True), axis=0, keepdims=True)


def _final_loss_grad(x, g, target, name):
    t = x.shape[0]

    def body(x_ref, g_ref, t_ref, loss_ref, dx_ref, dg_ref):
        loss, vjp = jax.vjp(_f_final, x_ref[...], g_ref[...], t_ref[...])
        dx, dg, _ = vjp(jnp.ones((1, 1), F32))
        dx_ref[...] = dx
        first = pl.program_id(0) == 0
        _accum(dg_ref, dg, first)
        _accum(loss_ref, jnp.broadcast_to(loss, (1, LANES)), first)

    return pl.pallas_call(
        body, name=name, grid=(t // TB_NORM,),
        in_specs=[_row_spec(D), _vec_spec(D), _row_spec(D)],
        out_specs=[_vec_spec(LANES), _row_spec(D), _vec_spec(D)],
        out_shape=[jax.ShapeDtypeStruct((1, LANES), F32), jax.ShapeDtypeStruct((t, D), F32), jax.ShapeDtypeStruct((1, D), F32)],
        compiler_params=_cparams("arbitrary"),
    )(x, g, target)


def _group_selector():
    gi = lax.broadcasted_iota(jnp.int32, (LANES, D), 0)
    ci = lax.broadcasted_iota(jnp.int32, (LANES, D), 1)
    return (ci // (D // SG_GROUPS) == gi).astype(F32)


def _f_sg(p, ln_g, ln_b, w_s, bs_t):
    u, v, z = p[:, :D], p[:, D:2 * D], p[:, 2 * D:]
    u = jax.nn.gelu(u)
    vf = jax.nn.gelu(v)
    mean = jnp.mean(vf, axis=-1, keepdims=True)
    var = jnp.mean(jnp.square(vf - mean), axis=-1, keepdims=True)
    vn = (vf - mean) * lax.rsqrt(var + LN_EPS) * ln_g + ln_b
    ti = lax.broadcasted_iota(jnp.int32, (CHUNK, CHUNK), 0)
    si = lax.broadcasted_iota(jnp.int32, (CHUNK, CHUNK), 1)
    causal = si <= ti
    cg = D // SG_GROUPS
    f = jnp.concatenate(
        [_bdot_nn(jnp.where(causal, w_s[g], 0.0), vn[:, g * cg:(g + 1) * cg]) for g in range(SG_GROUPS)], axis=1)
    f = f + jnp.dot(bs_t, _group_selector(), precision=HIGHEST, preferred_element_type=F32)
    return u * f * jax.nn.silu(z)


def _sg_specs():
    return [pl.BlockSpec((CHUNK, 3 * D), lambda i: (i, 0)), _vec_spec(D), _vec_spec(D),
            pl.BlockSpec((SG_GROUPS, CHUNK, CHUNK), lambda i: (0, 0, 0)), _vec_spec(LANES, CHUNK)]


def _sg_fwd(p, ln_g, ln_b, w_s, bs_t, name):
    t = p.shape[0]

    def body(p_ref, lg_ref, lb_ref, w_ref, b_ref, o_ref):
        o_ref[...] = _f_sg(p_ref[...], lg_ref[...], lb_ref[...], w_ref[...], b_ref[...]).astype(BF16)

    return pl.pallas_call(
        body, name=name, grid=(t // CHUNK,), in_specs=_sg_specs(), out_specs=_row_spec(D, CHUNK),
        out_shape=jax.ShapeDtypeStruct((t, D), BF16), compiler_params=_cparams("parallel"),
    )(p, ln_g, ln_b, w_s, bs_t)


def _sg_bwd(p, ln_g, ln_b, w_s, bs_t, dout, name):
    t = p.shape[0]

    def body(p_ref, lg_ref, lb_ref, w_ref, b_ref, do_ref, dp_ref, dlg_ref, dlb_ref, dw_ref, db_ref):
        _, vjp = jax.vjp(_f_sg, p_ref[...], lg_ref[...], lb_ref[...], w_ref[...], b_ref[...])
        dp, dlg, dlb, dw, db = vjp(do_ref[...])
        dp_ref[...] = dp.astype(BF16)
        first = pl.program_id(0) == 0
        _accum(dlg_ref, dlg, first)
        _accum(dlb_ref, dlb, first)
        _accum(dw_ref, dw, first)
        _accum(db_ref, db, first)

    vec = jax.ShapeDtypeStruct((1, D), F32)
    return pl.pallas_call(
        body, name=name, grid=(t // CHUNK,), in_specs=_sg_specs() + [_row_spec(D, CHUNK)],
        out_specs=[pl.BlockSpec((CHUNK, 3 * D), lambda i: (i, 0)), _vec_spec(D), _vec_spec(D),
                   pl.BlockSpec((SG_GROUPS, CHUNK, CHUNK), lambda i: (0, 0, 0)), _vec_spec(LANES, CHUNK)],
        out_shape=[jax.ShapeDtypeStruct((t, 3 * D), BF16), vec, vec,
                   jax.ShapeDtypeStruct((SG_GROUPS, CHUNK, CHUNK), F32), jax.ShapeDtypeStruct((CHUNK, LANES), F32)],
        compiler_params=_cparams("arbitrary"),
    )(p, ln_g, ln_b, w_s, bs_t, dout)


SWA_COLS = 2 * D + 2 * KVW
KV_BLOCK = 2 * KVW


def _lane_roll(x, shift):
    return pltpu.roll(x, shift, 1)


def _rot_half(x):
    w = x.shape[1]
    lane = lax.broadcasted_iota(jnp.int32, x.shape, 1)
    return jnp.where(lane % HEAD < HEAD // 2, -_lane_roll(x, w - HEAD // 2), _lane_roll(x, HEAD // 2))


@jax.custom_vjp
def _rope(x, cos, sin):
    return x * cos + _rot_half(x) * sin


def _rope_fwd(x, cos, sin):
    return _rope(x, cos, sin), (cos, sin)


def _rope_bwd(res, ct):
    cos, sin = res
    return ct * cos - _rot_half(ct) * sin, jnp.zeros_like(cos), jnp.zeros_like(sin)


_rope.defvjp(_rope_fwd, _rope_bwd)


@jax.custom_vjp
def _swap_halves(x):
    return _lane_roll(x, HEAD)


_swap_halves.defvjp(lambda x: (_lane_roll(x, HEAD), None), lambda _, ct: (_lane_roll(ct, HEAD),))


def _f_swa(pq, pkv, cos, sin, cosp, sinp, sink_row, valid):
    reps = D // LANES
    q = _rope(pq[:, :D], jnp.tile(cos, (1, reps)), jnp.tile(sin, (1, reps))) * (HEAD ** -0.5)
    k = _rope(pq[:, D:D + KVW], jnp.tile(cos, (1, KVW // LANES)), jnp.tile(sin, (1, KVW // LANES)))
    kp = _rope(pkv[:, :KVW], jnp.tile(cosp, (1, KVW // LANES)), jnp.tile(sinp, (1, KVW // LANES)))
    v, vp, z = pq[:, D + KVW:D + 2 * KVW], pkv[:, KVW:], pq[:, D + 2 * KVW:]
    kcat = jnp.concatenate([kp, k], axis=0)
    vcat = jnp.concatenate([vp, v], axis=0)
    lane = lax.broadcasted_iota(jnp.int32, (2 * CHUNK, LANES), 1)
    lo = lane < HEAD
    hlane = lax.broadcasted_iota(jnp.int32, (1, LANES), 1)

    def halves(cat, g):
        blk = cat[:, (g // 2) * LANES:(g // 2 + 1) * LANES]
        other = _swap_halves(blk)
        if g % 2 == 0:
            return jnp.where(lo, blk, 0.0), jnp.where(lo, 0.0, other)
        return jnp.where(lo, other, 0.0), jnp.where(lo, 0.0, blk)

    def probs(s, head):
        sink = jnp.sum(jnp.where(hlane == head, sink_row, 0.0), axis=1, keepdims=True)
        s = jnp.where(valid, s, NEG)
        m = lax.stop_gradient(jnp.maximum(jnp.max(s, axis=1, keepdims=True), sink))
        e = jnp.exp(s - m)
        return e / (jnp.sum(e, axis=1, keepdims=True) + jnp.exp(sink - m))

    outs = []
    rep = N_HEADS // KV_HEADS
    for g in range(KV_HEADS):
        k_lo, k_hi = halves(kcat, g)
        v_lo, v_hi = halves(vcat, g)
        for j in range(g * rep // 2, (g + 1) * rep // 2):
            qp = q[:, j * LANES:(j + 1) * LANES]
            p_a = probs(_bdot_nt(qp, k_lo), 2 * j)
            p_b = probs(_bdot_nt(qp, k_hi), 2 * j + 1)
            outs.append(_bdot_nn(p_a, v_lo) + _bdot_nn(p_b, v_hi))
    return jnp.concatenate(outs, axis=1) * jax.nn.silu(z)


def _swa_valid(block):
    qi = lax.broadcasted_iota(jnp.int32, (CHUNK, 2 * CHUNK), 0)
    kj = lax.broadcasted_iota(jnp.int32, (CHUNK, 2 * CHUNK), 1)
    rel = qi + CHUNK - kj
    return (rel >= 0) & (rel < CHUNK) & ((kj >= CHUNK) | (block > 0))


def _swa_specs(blk):
    prev = lambda i: jnp.maximum(blk(i) - 1, 0)
    kv_col = D // KV_BLOCK
    return [pl.BlockSpec((CHUNK, SWA_COLS), lambda i: (blk(i), 0)),
            pl.BlockSpec((CHUNK, KV_BLOCK), lambda i: (prev(i), kv_col)),
            pl.BlockSpec((CHUNK, LANES), lambda i: (blk(i), 0)), pl.BlockSpec((CHUNK, LANES), lambda i: (blk(i), 0)),
            pl.BlockSpec((CHUNK, LANES), lambda i: (prev(i), 0)), pl.BlockSpec((CHUNK, LANES), lambda i: (prev(i), 0)),
            _vec_spec(LANES)]


def _swa_fwd(p, cos, sin, sink_row, name):
    t = p.shape[0]

    def body(pq_ref, pkv_ref, c_ref, s_ref, cp_ref, sp_ref, sk_ref, o_ref):
        valid = _swa_valid(pl.program_id(0))
        o_ref[...] = _f_swa(pq_ref[...], pkv_ref[...], c_ref[...], s_ref[...], cp_ref[...], sp_ref[...],
                            sk_ref[...], valid).astype(BF16)

    return pl.pallas_call(
        body, name=name, grid=(t // CHUNK,), in_specs=_swa_specs(lambda i: i), out_specs=_row_spec(D, CHUNK),
        out_shape=jax.ShapeDtypeStruct((t, D), BF16), compiler_params=_cparams("parallel"),
    )(p, p, cos, sin, cos, sin, sink_row)


def _swa_bwd(p, cos, sin, sink_row, dout, name):
    t = p.shape[0]
    nb = t // CHUNK
    blk = lambda i: nb - 1 - i

    def body(pq_ref, pkv_ref, c_ref, s_ref, cp_ref, sp_ref, sk_ref, do_ref, dp_ref, dsk_ref, pend_ref):
        i = pl.program_id(0)
        valid = _swa_valid(blk(i))
        f = functools.partial(_f_swa, valid=valid)
        _, vjp = jax.vjp(f, pq_ref[...], pkv_ref[...], c_ref[...], s_ref[...], cp_ref[...], sp_ref[...], sk_ref[...])
        dpq, dpkv, _, _, _, _, dsk = vjp(do_ref[...])

        @pl.when(i == 0)
        def _():
            pend_ref[...] = jnp.zeros_like(pend_ref)

        dp_ref[...] = jnp.concatenate(
            [dpq[:, :D], dpq[:, D:D + KV_BLOCK] + pend_ref[...], dpq[:, D + KV_BLOCK:]], axis=1).astype(BF16)
        pend_ref[...] = dpkv
        _accum(dsk_ref, dsk, i == 0)

    return pl.pallas_call(
        body, name=name, grid=(nb,),
        in_specs=_swa_specs(blk) + [pl.BlockSpec((CHUNK, D), lambda i: (blk(i), 0))],
        out_specs=[pl.BlockSpec((CHUNK, SWA_COLS), lambda i: (blk(i), 0)), _vec_spec(LANES)],
        out_shape=[jax.ShapeDtypeStruct((t, SWA_COLS), BF16), jax.ShapeDtypeStruct((1, LANES), F32)],
        scratch_shapes=[pltpu.VMEM((CHUNK, KV_BLOCK), F32)], compiler_params=_cparams("arbitrary"),
    )(p, p, cos, sin, cos, sin, sink_row, dout)


RW_MAIN = 4 * D
RW_LO = 2 * LORA_PAD
VM = LANES // N_HEADS
VD = HEAD // VM
S_ROWS = VD * HEAD
TB_RW = 128
TB_K = 32
TB_SCAN = 16


def _dim_major(a):
    return a.reshape(a.shape[:-1] + (N_HEADS, HEAD)).swapaxes(-1, -2).reshape(a.shape)


def _head_major(a):
    return a.reshape(a.shape[:-1] + (HEAD, N_HEADS)).swapaxes(-1, -2).reshape(a.shape)


def _compact(x):
    return x.reshape(x.shape[0], VD, LANES)


def _param_compact(w):
    return _dim_major(w).reshape(VD, LANES)


def _param_compact_inv(pc):
    return _head_major(pc.reshape(-1))


def _f_rwkv_lora(xs_lo, w0, a0, wl, al):
    decay = jnp.exp(-DECAY_SCALE * jax.nn.sigmoid(w0 + _bdot_nn(jnp.tanh(xs_lo[:, :LORA_PAD]), wl)))
    a = jax.nn.sigmoid(a0 + _bdot_nn(xs_lo[:, LORA_PAD:], al))
    return decay, a


def _prev_rows_spec(width, tb):
    return pl.BlockSpec((8, width), lambda i: (jnp.maximum(i * (tb // 8) - 1, 0), 0))


def _token_shift_lerp(p, prev8, mu, first):
    rows = lax.broadcasted_iota(jnp.int32, p.shape, 0)
    prev = jnp.where(first, 0.0, prev8[7:8, :])
    shifted = jnp.where(rows == 0, prev, pltpu.roll(p, 1, 0))
    return p + (shifted - p) * mu


def _rwkv_pre_fwd(p_main, p_lo, mu_main, mu_lo, w0, a0, wl, al, name):
    t = p_main.shape[0]
    tb = TB_RW

    def body(pm_ref, pmp_ref, pl_ref, plp_ref, mm_ref, ml_ref, w0_ref, a0_ref, wl_ref, al_ref,
             xm_ref, xl_ref, dec_ref, a_ref):
        first = pl.program_id(0) == 0
        xm_ref[...] = _token_shift_lerp(pm_ref[...], pmp_ref[...], mm_ref[...], first)
        xs_lo = _token_shift_lerp(pl_ref[...], plp_ref[...], ml_ref[...], first)
        xl_ref[...] = xs_lo
        dec_ref[...], a_ref[...] = _f_rwkv_lora(xs_lo, w0_ref[...], a0_ref[...], wl_ref[...], al_ref[...])

    return pl.pallas_call(
        body, name=name, grid=(t // tb,),
        in_specs=[_row_spec(RW_MAIN, tb), _prev_rows_spec(RW_MAIN, tb), _row_spec(RW_LO, tb), _prev_rows_spec(RW_LO, tb),
                  _vec_spec(RW_MAIN), _vec_spec(RW_LO), _vec_spec(D), _vec_spec(D),
                  _vec_spec(D, LORA_PAD), _vec_spec(D, LORA_PAD)],
        out_specs=[_row_spec(RW_MAIN, tb), _row_spec(RW_LO, tb), _row_spec(D, tb), _row_spec(D, tb)],
        out_shape=[jax.ShapeDtypeStruct((t, RW_MAIN), F32), jax.ShapeDtypeStruct((t, RW_LO), F32),
                   jax.ShapeDtypeStruct((t, D), F32), jax.ShapeDtypeStruct((t, D), F32)],
        compiler_params=_cparams("parallel"),
    )(p_main, p_main, p_lo, p_lo, mu_main, mu_lo, w0, a0, wl, al)


def _rwkv_lora_bwd(xs_lo, w0, a0, wl, al, ddecay, da, name):
    t = xs_lo.shape[0]
    tb = TB_NORM

    def body(x_ref, w0_ref, a0_ref, wl_ref, al_ref, dd_ref, da_ref, dx_ref, dw0_ref, da0_ref, dwl_ref, dal_ref):
        _, vjp = jax.vjp(_f_rwkv_lora, x_ref[...], w0_ref[...], a0_ref[...], wl_ref[...], al_ref[...])
        dx, dw0, da0, dwl, dal = vjp((dd_ref[...], da_ref[...]))
        dx_ref[...] = dx
        first = pl.program_id(0) == 0
        _accum(dw0_ref, dw0, first)
        _accum(da0_ref, da0, first)
        _accum(dwl_ref, dwl, first)
        _accum(dal_ref, dal, first)

    vec = jax.ShapeDtypeStruct((1, D), F32)
    lor = jax.ShapeDtypeStruct((LORA_PAD, D), F32)
    return pl.pallas_call(
        body, name=name, grid=(t // tb,),
        in_specs=[_row_spec(RW_LO), _vec_spec(D), _vec_spec(D), _vec_spec(D, LORA_PAD), _vec_spec(D, LORA_PAD),
                  _row_spec(D), _row_spec(D)],
        out_specs=[_row_spec(RW_LO), _vec_spec(D), _vec_spec(D), _vec_spec(D, LORA_PAD), _vec_spec(D, LORA_PAD)],
        out_shape=[jax.ShapeDtypeStruct((t, RW_LO), F32), vec, vec, lor, lor], compiler_params=_cparams("arbitrary"),
    )(xs_lo, w0, a0, wl, al, ddecay, da)


def _lerp_bwd(p, dxs, mu, name):
    t, width = p.shape
    tb = TB_RW
    nb = t // tb

    def body(p_ref, pp_ref, d_ref, dn_ref, mu_ref, dp_ref, dmu_ref):
        i = pl.program_id(0)
        pv, dv, mu_v = p_ref[...], d_ref[...], mu_ref[...]
        rows = lax.broadcasted_iota(jnp.int32, pv.shape, 0)
        prev = jnp.where(i == 0, 0.0, pp_ref[7:8, :])
        shifted = jnp.where(rows == 0, prev, pltpu.roll(pv, 1, 0))
        nxt = jnp.where(i == nb - 1, 0.0, dn_ref[0:1, :])
        d_next = jnp.where(rows == tb - 1, nxt, pltpu.roll(dv, tb - 1, 0))
        dp_ref[...] = (dv * (1.0 - mu_v) + d_next * mu_v).astype(BF16)
        _accum(dmu_ref, jnp.sum(dv * (shifted - pv), axis=0, keepdims=True), i == 0)

    return pl.pallas_call(
        body, name=name, grid=(nb,),
        in_specs=[_row_spec(width, tb), _prev_rows_spec(width, tb), _row_spec(width, tb),
                  pl.BlockSpec((8, width), lambda i: (jnp.minimum((i + 1) * (tb // 8), t // 8 - 1), 0)), _vec_spec(width)],
        out_specs=[_row_spec(width, tb), _vec_spec(width)],
        out_shape=[jax.ShapeDtypeStruct((t, width), BF16), jax.ShapeDtypeStruct((1, width), F32)],
        compiler_params=_cparams("arbitrary"),
    )(p, p, dxs, dxs, mu)


def _lane_group_sum2d(x):
    x = x + pltpu.roll(x, N_HEADS, 1)
    return x + pltpu.roll(x, 2 * N_HEADS, 1)


@jax.custom_vjp
def _lane_group_sum(x):
    return _lane_group_sum2d(x.reshape(-1, LANES)).reshape(x.shape)


_lane_group_sum.defvjp(lambda x: (_lane_group_sum(x), None), lambda _, ct: (_lane_group_sum(ct),))


def _head_sum(x):
    return _lane_group_sum(jnp.sum(x, axis=1, keepdims=True))


def _f_kprep(k, a, r, kkp, kap, rkp):
    kk = k * kkp
    kk = kk / jnp.maximum(jnp.sqrt(_head_sum(kk * kk)), 1e-12)
    k2 = k * (1.0 + (a - 1.0) * kap)
    return kk, k2, kk * a, _head_sum(r * k2 * rkp)


def _k_spec(rows=HEAD, tb=TB_K):
    return pl.BlockSpec((tb, rows, LANES), lambda i: (i, 0, 0))


def _kparam_spec(rows=HEAD):
    return pl.BlockSpec((rows, LANES), lambda i: (0, 0))


def _lane_group(shape):
    return lax.broadcasted_iota(jnp.int32, shape, len(shape) - 1) // N_HEADS


def _store_k_layout(ref, xc):
    x2 = xc.reshape(-1, LANES)
    group = _lane_group(x2.shape)
    for q in range(VM):
        one = jnp.where(group == q, x2, 0.0)
        ref[:, pl.ds(q, VD, stride=VM), :] = _lane_group_sum2d(one).reshape(xc.shape)


def _load_compact(ref):
    acc = None
    for q in range(VM):
        rows = _lane_group_sum(ref[:, pl.ds(q, VD, stride=VM), :])
        part = jnp.where(_lane_group(rows.shape) == q, rows, 0.0)
        acc = part if acc is None else acc + part
    return acc


def _rwkv_kprep_fwd(k, a, r, w, kkp, kap, rkp, name):
    t = k.shape[0]

    def body(k_ref, a_ref, r_ref, w_ref, kkp_ref, kap_ref, rkp_ref, kk_ref, k2_ref, b_ref, r4_ref, w4_ref, rk_ref):
        rv = r_ref[...]
        kk, k2, b, rk_ref[...] = _f_kprep(k_ref[...], a_ref[...], rv, kkp_ref[...], kap_ref[...], rkp_ref[...])
        for ref, val in ((kk_ref, kk), (k2_ref, k2), (b_ref, b), (r4_ref, rv), (w4_ref, w_ref[...])):
            _store_k_layout(ref, val)

    big = jax.ShapeDtypeStruct((t, HEAD, LANES), F32)
    return pl.pallas_call(
        body, name=name, grid=(t // TB_K,),
        in_specs=[_k_spec(VD)] * 4 + [_kparam_spec(VD)] * 3, out_specs=[_k_spec()] * 5 + [_k_spec(1)],
        out_shape=[big] * 5 + [jax.ShapeDtypeStruct((t, 1, LANES), F32)], compiler_params=_cparams("parallel"),
    )(k, a, r, w, kkp, kap, rkp)


def _rwkv_kprep_bwd(k, a, r, kkp, kap, rkp, dkk, dk2, db, drk, dr_scan, dw_scan, name):
    t = k.shape[0]

    def body(k_ref, a_ref, r_ref, kkp_ref, kap_ref, rkp_ref, dkk_ref, dk2_ref, db_ref, drk_ref, drs_ref, dws_ref,
             dk_ref, da_ref, dr_ref, dw_ref, dkkp_ref, dkap_ref, drkp_ref):
        _, vjp = jax.vjp(_f_kprep, k_ref[...], a_ref[...], r_ref[...], kkp_ref[...], kap_ref[...], rkp_ref[...])
        dk, da, dr, dkkp, dkap, drkp = vjp((_load_compact(dkk_ref), _load_compact(dk2_ref), _load_compact(db_ref),
                                            drk_ref[...]))
        dk_ref[...] = dk
        da_ref[...] = da
        dr_ref[...] = dr + _load_compact(drs_ref)
        dw_ref[...] = _load_compact(dws_ref)
        first = pl.program_id(0) == 0
        _accum(dkkp_ref, dkkp, first)
        _accum(dkap_ref, dkap, first)
        _accum(drkp_ref, drkp, first)

    cl = jax.ShapeDtypeStruct((t, VD, LANES), F32)
    par = jax.ShapeDtypeStruct((VD, LANES), F32)
    return pl.pallas_call(
        body, name=name, grid=(t // TB_K,),
        in_specs=[_k_spec(VD)] * 3 + [_kparam_spec(VD)] * 3 + [_k_spec()] * 3 + [_k_spec(1), _k_spec(), _k_spec()],
        out_specs=[_k_spec(VD)] * 4 + [_kparam_spec(VD)] * 3,
        out_shape=[cl] * 4 + [par] * 3, compiler_params=_cparams("arbitrary"),
    )(k, a, r, kkp, kap, rkp, dkk, dk2, db, drk, dr_scan, dw_scan)


def _f_post(y, v, rk, g, b):
    mean = _lane_group_sum(jnp.sum(y, axis=1, keepdims=True)) * (1.0 / HEAD)
    yc = y - mean
    var = _lane_group_sum(jnp.sum(yc * yc, axis=1, keepdims=True)) * (1.0 / HEAD)
    return yc * lax.rsqrt(var + GN_EPS) * g + b + rk * v


def _rwkv_post_fwd(y, v, rk, g, b, name):
    t = y.shape[0]

    def body(y_ref, v_ref, rk_ref, g_ref, b_ref, o_ref):
        o_ref[...] = _f_post(y_ref[...], v_ref[...], rk_ref[...], g_ref[...], b_ref[...])

    return pl.pallas_call(
        body, name=name, grid=(t // TB_K,),
        in_specs=[_k_spec(VD), _k_spec(VD), _k_spec(1), _kparam_spec(VD), _kparam_spec(VD)], out_specs=_k_spec(VD),
        out_shape=jax.ShapeDtypeStruct((t, VD, LANES), F32), compiler_params=_cparams("parallel"),
    )(y, v, rk, g, b)


def _rwkv_post_bwd(y, v, rk, g, b, do, name):
    t = y.shape[0]

    def body(y_ref, v_ref, rk_ref, g_ref, b_ref, do_ref, dy_ref, dv_ref, drk_ref, dg_ref, db_ref):
        _, vjp = jax.vjp(_f_post, y_ref[...], v_ref[...], rk_ref[...], g_ref[...], b_ref[...])
        dy, dv, drk, dg, db = vjp(do_ref[...])
        dy_ref[...] = dy
        dv_ref[...] = dv
        drk_ref[...] = drk
        first = pl.program_id(0) == 0
        _accum(dg_ref, dg, first)
        _accum(db_ref, db, first)

    vl = jax.ShapeDtypeStruct((t, VD, LANES), F32)
    par = jax.ShapeDtypeStruct((VD, LANES), F32)
    return pl.pallas_call(
        body, name=name, grid=(t // TB_K,),
        in_specs=[_k_spec(VD), _k_spec(VD), _k_spec(1), _kparam_spec(VD), _kparam_spec(VD), _k_spec(VD)],
        out_specs=[_k_spec(VD), _k_spec(VD), _k_spec(1), _kparam_spec(VD), _kparam_spec(VD)],
        out_shape=[vl, vl, jax.ShapeDtypeStruct((t, 1, LANES), F32), par, par], compiler_params=_cparams("arbitrary"),
    )(y, v, rk, g, b, do)


def _f_gate(o, z):
    return o * jax.nn.silu(z)


def _z_spec(tb=TB_NORM):
    return pl.BlockSpec((tb, D), lambda i: (i, 3))


def _rwkv_gate_fwd(o, xs_main, name):
    t = o.shape[0]

    def body(o_ref, z_ref, u_ref):
        u_ref[...] = _f_gate(o_ref[...], z_ref[...]).astype(BF16)

    return pl.pallas_call(
        body, name=name, grid=(t // TB_NORM,), in_specs=[_row_spec(D), _z_spec()], out_specs=_row_spec(D),
        out_shape=jax.ShapeDtypeStruct((t, D), BF16), compiler_params=_cparams("parallel"),
    )(o, xs_main)


def _rwkv_gate_bwd(o, xs_main, du, name):
    t = o.shape[0]

    def body(o_ref, z_ref, du_ref, do_ref, dz_ref):
        _, vjp = jax.vjp(_f_gate, o_ref[...], z_ref[...])
        do_ref[...], dz_ref[...] = vjp(du_ref[...])

    full = jax.ShapeDtypeStruct((t, D), F32)
    return pl.pallas_call(
        body, name=name, grid=(t // TB_NORM,), in_specs=[_row_spec(D), _z_spec(), _row_spec(D)],
        out_specs=[_row_spec(D), _row_spec(D)], out_shape=[full, full], compiler_params=_cparams("parallel"),
    )(o, xs_main, du)


def _colsum(x):
    return jnp.sum(x, axis=0, keepdims=True)


def _rwkv_scan_fwd(r4, w4, k24, kk4, b4, v, name):
    t = r4.shape[0]
    tb = TB_SCAN

    def body(r_ref, w_ref, k2_ref, kk_ref, b_ref, v_ref, y_ref, sall_ref, sa_ref, s_scr):
        @pl.when(pl.program_id(0) == 0)
        def _():
            s_scr[...] = jnp.zeros_like(s_scr)

        sall_ref[0] = s_scr[...]

        def step(tt, dst):
            kk = kk_ref[tt]
            sas = []
            for vd in range(VD):
                sa = _colsum(sall_ref[tt, pl.ds(vd * HEAD, HEAD), :] * kk)
                sa_ref[tt, pl.ds(vd, 1), :] = sa
                sas.append(sa)
            w, b, k2, r = w_ref[tt], b_ref[tt], k2_ref[tt], r_ref[tt]
            for vd in range(VD):
                rows = pl.ds(vd * HEAD, HEAD)
                s = sall_ref[tt, rows, :] * w - sas[vd] * b + v_ref[tt, pl.ds(vd, 1), :] * k2
                dst[rows, :] = s
                y_ref[tt, pl.ds(vd, 1), :] = _colsum(s * r)

        def loop_step(tt, carry):
            step(tt, sall_ref.at[tt + 1])
            return carry

        lax.fori_loop(0, tb - 1, loop_step, 0)
        step(tb - 1, s_scr)

    vl = jax.ShapeDtypeStruct((t, VD, LANES), F32)
    return pl.pallas_call(
        body, name=name, grid=(t // tb,),
        in_specs=[_k_spec(HEAD, tb)] * 5 + [_k_spec(VD, tb)],
        out_specs=[_k_spec(VD, tb), _k_spec(S_ROWS, tb), _k_spec(VD, tb)],
        out_shape=[vl, jax.ShapeDtypeStruct((t, S_ROWS, LANES), F32), vl],
        scratch_shapes=[pltpu.VMEM((S_ROWS, LANES), F32)], compiler_params=_cparams("arbitrary"),
    )(r4, w4, k24, kk4, b4, v)


def _rwkv_scan_bwd(dy, s_all, sa_all, r4, w4, k24, kk4, b4, v, name):
    t = r4.shape[0]
    tb = TB_SCAN
    nb = t // tb
    blk = lambda i: nb - 1 - i

    def body(dy_ref, sall_ref, sa_ref, r_ref, w_ref, k2_ref, kk_ref, b_ref, v_ref,
             dr_ref, dw_ref, dk2_ref, dkk_ref, db_ref, dv_ref, ds_scr):
        @pl.when(pl.program_id(0) == 0)
        def _():
            ds_scr[...] = jnp.zeros_like(ds_scr)

        def step(j, carry):
            tt = tb - 1 - j
            vrow = lambda ref, vd: ref[tt, pl.ds(vd, 1), :]
            srows = lambda vd: pl.ds(vd * HEAD, HEAD)
            r, k2, b = r_ref[tt], k2_ref[tt], b_ref[tt]
            dsas = []
            for vd in range(VD):
                ds = ds_scr[srows(vd), :] + vrow(dy_ref, vd) * r
                ds_scr[srows(vd), :] = ds
                dv_ref[tt, pl.ds(vd, 1), :] = _colsum(ds * k2)
                dsas.append(-_colsum(ds * b))
            zero = jnp.zeros((HEAD, LANES), F32)
            dk2, q, sady, vdy = zero, zero, 0.0, 0.0
            for vd in range(VD):
                dyv = vrow(dy_ref, vd)
                dk2 = dk2 + ds_scr[srows(vd), :] * vrow(v_ref, vd)
                q = q + sall_ref[tt, srows(vd), :] * dyv
                sady = sady + vrow(sa_ref, vd) * dyv
                vdy = vdy + vrow(v_ref, vd) * dyv
            dk2_ref[tt] = dk2
            dr_ref[tt] = w_ref[tt] * q - b_ref[tt] * sady + k2_ref[tt] * vdy
            dw, dkk = zero, zero
            for vd in range(VD):
                sp = sall_ref[tt, srows(vd), :]
                dw = dw + ds_scr[srows(vd), :] * sp
                dkk = dkk + sp * dsas[vd]
            dw_ref[tt] = dw
            dkk_ref[tt] = dkk
            w, kk = w_ref[tt], kk_ref[tt]
            db = zero
            for vd in range(VD):
                ds = ds_scr[srows(vd), :]
                db = db - ds * vrow(sa_ref, vd)
                ds_scr[srows(vd), :] = ds * w + dsas[vd] * kk
            db_ref[tt] = db
            return carry

        lax.fori_loop(0, tb, step, 0)

    rk = lambda rows: pl.BlockSpec((tb, rows, LANES), lambda i: (blk(i), 0, 0))
    big = jax.ShapeDtypeStruct((t, HEAD, LANES), F32)
    return pl.pallas_call(
        body, name=name, grid=(nb,),
        in_specs=[rk(VD), rk(S_ROWS), rk(VD)] + [rk(HEAD)] * 5 + [rk(VD)],
        out_specs=[rk(HEAD)] * 5 + [rk(VD)],
        out_shape=[big] * 5 + [jax.ShapeDtypeStruct((t, VD, LANES), F32)],
        scratch_shapes=[pltpu.VMEM((S_ROWS, LANES), F32)], compiler_params=_cparams("arbitrary"),
    )(dy, s_all, sa_all, r4, w4, k24, kk4, b4, v)


def _rwkv_mixer_fwd(p_main, p_lo, prm, tag):
    xs_main, xs_lo, decay, a = _rwkv_pre_fwd(p_main, p_lo, prm["mu_main"], prm["mu_lo"], prm["w0"], prm["a0"],
                                             prm["wl"], prm["al"], tag + "_pre")
    r, k, v, w, a = (_compact(x) for x in (xs_main[:, :D], xs_main[:, D:2 * D], xs_main[:, 2 * D:3 * D], decay, a))
    kk4, k24, b4, r4, w4, rk = _rwkv_kprep_fwd(k, a, r, w, prm["kkp"], prm["kap"], prm["rkp"], tag + "_kprep")
    y, s_all, sa_all = _rwkv_scan_fwd(r4, w4, k24, kk4, b4, v, tag + "_scan")
    o = _rwkv_post_fwd(y, v, rk, prm["gn_g"], prm["gn_b"], tag + "_post").reshape(-1, D)
    u = _rwkv_gate_fwd(o, xs_main, tag + "_gate")
    saved = dict(xs_main=xs_main, xs_lo=xs_lo, r=r, k=k, a=a, v=v, r4=r4, w4=w4, kk4=kk4, k24=k24, b4=b4, rk=rk,
                 y=y, s_all=s_all, sa_all=sa_all, o=o)
    return u, saved


def _rwkv_mixer_bwd(p_main, p_lo, prm, sv, du, tag):
    do, dz = _rwkv_gate_bwd(sv["o"], sv["xs_main"], du, tag + "_gate_b")
    dy, dv_post, drk, dgn_g, dgn_b = _rwkv_post_bwd(sv["y"], sv["v"], sv["rk"], prm["gn_g"], prm["gn_b"],
                                                    _compact(do), tag + "_post_b")
    dr_s, dw_s, dk24, dkk4, db4, dv_scan = _rwkv_scan_bwd(dy, sv["s_all"], sv["sa_all"], sv["r4"], sv["w4"], sv["k24"],
                                                          sv["kk4"], sv["b4"], sv["v"], tag + "_scan_b")
    dk, da, dr, dw, dkkp, dkap, drkp = _rwkv_kprep_bwd(sv["k"], sv["a"], sv["r"], prm["kkp"], prm["kap"], prm["rkp"],
                                                       dkk4, dk24, db4, drk, dr_s, dw_s, tag + "_kprep_b")
    flat = lambda xc: xc.reshape(-1, D)
    dxs_lo, dw0, da0, dwl, dal = _rwkv_lora_bwd(sv["xs_lo"], prm["w0"], prm["a0"], prm["wl"], prm["al"],
                                                flat(dw), flat(da), tag + "_lora_b")
    dxs_main = jnp.concatenate([flat(dr), flat(dk), flat(dv_post + dv_scan), dz], axis=1)
    dp_main, dmu_main = _lerp_bwd(p_main, dxs_main, prm["mu_main"], tag + "_lerp_main_b")
    dp_lo, dmu_lo = _lerp_bwd(p_lo, dxs_lo, prm["mu_lo"], tag + "_lerp_lo_b")
    grads = dict(mu_main=dmu_main, mu_lo=dmu_lo, w0=dw0, a0=da0, wl=dwl, al=dal, kkp=dkkp, kap=dkap, rkp=drkp,
                 gn_g=dgn_g, gn_b=dgn_b)
    return dp_main, dp_lo, grads


N_DEV = 8
N_CHIPS = 4
ANY = pl.BlockSpec(memory_space=pl.ANY)


def _place():
    return lax.axis_index("x"), lax.axis_index("y"), lax.axis_index("c")


def _remote(src, dst, send_sems, recv_sems, k, dev):
    return pltpu.make_async_remote_copy(src_ref=src, dst_ref=dst, send_sem=send_sems.at[k], recv_sem=recv_sems.at[k],
                                        device_id=dev, device_id_type=MESHT)


def _all_gather8(v, name):
    def body(buf_ref, out_ref, send_sems, recv_sems):
        del buf_ref
        x, y, c = _place()
        mine = out_ref.at[4 * x + 2 * y + c]
        peers = [(x ^ (k >> 2), y ^ ((k >> 1) & 1), c ^ (k & 1)) for k in range(1, N_DEV)]
        sends = [_remote(mine, mine, send_sems, recv_sems, k, peer) for k, peer in enumerate(peers)]
        for cp in sends:
            cp.start()
        for k, (px, py, pc) in enumerate(peers):
            _remote(mine, out_ref.at[4 * px + 2 * py + pc], send_sems, recv_sems, k, (x, y, c)).wait_recv()
        for cp in sends:
            cp.wait_send()

    return pl.pallas_call(
        body, name=name, in_specs=[ANY], out_specs=ANY, input_output_aliases={0: 0},
        out_shape=jax.ShapeDtypeStruct((N_DEV,) + v.shape, v.dtype),
        scratch_shapes=[pltpu.SemaphoreType.DMA((N_DEV - 1,)), pltpu.SemaphoreType.DMA((N_DEV - 1,))],
    )(jnp.broadcast_to(v[None], (N_DEV,) + v.shape))


def _other_chips(x, y):
    return [(1 - x, y), (x, 1 - y), (1 - x, 1 - y)]


def _chip_gather(v, name):
    def body(buf_ref, out_ref, send_sems, recv_sems):
        del buf_ref
        x, y, c = _place()
        mine = out_ref.at[2 * x + y, c]
        chips = _other_chips(x, y)
        sends = [_remote(mine, mine, send_sems, recv_sems, j, (cx, cy, c)) for j, (cx, cy) in enumerate(chips)]
        for cp in sends:
            cp.start()
        for j, (cx, cy) in enumerate(chips):
            landed = out_ref.at[2 * cx + cy, c]
            _remote(mine, landed, send_sems, recv_sems, j, (x, y, c)).wait_recv()
            fwd = _remote(landed, landed, send_sems, recv_sems, 3 + j, (x, y, 1 - c))
            fwd.start()
            sends.append(fwd)
        for j, (cx, cy) in enumerate(chips):
            _remote(mine, out_ref.at[2 * cx + cy, 1 - c], send_sems, recv_sems, 3 + j, (x, y, c)).wait_recv()
        for cp in sends:
            cp.wait_send()

    return pl.pallas_call(
        body, name=name, in_specs=[ANY], out_specs=ANY, input_output_aliases={0: 0},
        out_shape=jax.ShapeDtypeStruct((N_CHIPS,) + v.shape, v.dtype),
        scratch_shapes=[pltpu.SemaphoreType.DMA((6,)), pltpu.SemaphoreType.DMA((6,))],
    )(jnp.broadcast_to(v[None], (N_CHIPS,) + v.shape))


RS_W = 1024
RS_BLOCK_BYTES = 4 << 20


def _dma_sems(n):
    return [pltpu.SemaphoreType.DMA((n,)), pltpu.SemaphoreType.DMA((n,))]


def _rs_pair_exchange(gs, name):
    n = len(gs)

    def body(*refs):
        g_refs, got_refs, (send_sems, recv_sems) = refs[:n], refs[n:2 * n], refs[2 * n:]
        x, y, c = _place()
        sends = [_remote(g_refs[i].at[s, 1 - c], got_refs[i].at[s], send_sems, recv_sems, N_CHIPS * i + s, (x, y, 1 - c))
                 for i in range(n) for s in range(N_CHIPS)]
        for cp in sends:
            cp.start()
        for cp in sends:
            cp.wait()

    return pl.pallas_call(
        body, name=name, in_specs=[ANY] * n, out_specs=[ANY] * n,
        out_shape=[jax.ShapeDtypeStruct((N_CHIPS,) + g.shape[2:], g.dtype) for g in gs],
        scratch_shapes=_dma_sems(N_CHIPS * n),
    )(*gs)


def _rs_rows(rows, cols):
    cap = max(16, RS_BLOCK_BYTES // (N_CHIPS * 4 * cols))
    return rows if rows <= cap else max(d for d in range(16, cap + 1, 16) if rows % d == 0)


def _rs_pair_add(g, got, c_arr, name):
    _, _, rows, width = g.shape
    tr = _rs_rows(rows, width)

    def body(c_ref, g_ref, got_ref, p_ref):
        p_ref[...] = (g_ref[...] + got_ref[...]).astype(BF16)

    return pl.pallas_call(
        body, name=name,
        grid_spec=pltpu.PrefetchScalarGridSpec(
            num_scalar_prefetch=1, grid=(rows // tr,),
            in_specs=[pl.BlockSpec((N_CHIPS, None, tr, width), lambda i, c_ref: (0, c_ref[0], i, 0)),
                      pl.BlockSpec((N_CHIPS, tr, width), lambda i, c_ref: (0, i, 0))],
            out_specs=pl.BlockSpec((N_CHIPS, tr, width), lambda i, c_ref: (0, i, 0))),
        out_shape=jax.ShapeDtypeStruct((N_CHIPS, rows, width), BF16), compiler_params=_cparams("parallel"),
    )(c_arr, g, got)


def _rs_chip_exchange(ps, name):
    n = len(ps)

    def body(*refs):
        p_refs, q_refs, (send_sems, recv_sems) = refs[:n], refs[n:2 * n], refs[2 * n:]
        x, y, c = _place()
        me = 2 * x + y
        chips = _other_chips(x, y)
        sends = [_remote(p_refs[i].at[2 * cx + cy], q_refs[i].at[j], send_sems, recv_sems, 3 * i + j, (cx, cy, c))
                 for i in range(n) for j, (cx, cy) in enumerate(chips)]
        for cp in sends:
            cp.start()
        for i in range(n):
            for j in range(3):
                _remote(p_refs[i].at[me], q_refs[i].at[j], send_sems, recv_sems, 3 * i + j, (x, y, c)).wait_recv()
        for cp in sends:
            cp.wait_send()

    return pl.pallas_call(
        body, name=name, in_specs=[ANY] * n, out_specs=[ANY] * n,
        out_shape=[jax.ShapeDtypeStruct((3,) + p.shape[1:], p.dtype) for p in ps],
        scratch_shapes=_dma_sems(3 * n),
    )(*ps)


def _rs_chip_add(p, q, idx, name):
    _, rows, width = q.shape
    tr = _rs_rows(rows, width)

    def body(idx_ref, p_ref, q_ref, r_ref):
        qv = q_ref[...].astype(F32)
        r_ref[...] = ((p_ref[...].astype(F32) + qv[0]) + qv[1]) + qv[2]

    return pl.pallas_call(
        body, name=name,
        grid_spec=pltpu.PrefetchScalarGridSpec(
            num_scalar_prefetch=1, grid=(rows // tr,),
            in_specs=[pl.BlockSpec((None, tr, width), lambda i, idx_ref: (idx_ref[0], i, 0)),
                      pl.BlockSpec((3, tr, width), lambda i, idx_ref: (0, i, 0))],
            out_specs=pl.BlockSpec((None, tr, width), lambda i, idx_ref: (idx_ref[1], i, 0))),
        out_shape=jax.ShapeDtypeStruct((2, rows, width), F32), compiler_params=_cparams("parallel"),
    )(idx, p, q)


def _rs_pair_share(rs, name):
    n = len(rs)

    def body(*refs):
        out_refs, (send_sems, recv_sems) = refs[n:2 * n], refs[2 * n:]
        x, y, c = _place()
        sends = [_remote(out_refs[i].at[c], out_refs[i].at[c], send_sems, recv_sems, i, (x, y, 1 - c)) for i in range(n)]
        for cp in sends:
            cp.start()
        for i in range(n):
            _remote(out_refs[i].at[c], out_refs[i].at[1 - c], send_sems, recv_sems, i, (x, y, c)).wait_recv()
        for cp in sends:
            cp.wait_send()

    return pl.pallas_call(
        body, name=name, in_specs=[ANY] * n, out_specs=[ANY] * n, input_output_aliases={i: i for i in range(n)},
        out_shape=[jax.ShapeDtypeStruct(r.shape, r.dtype) for r in rs], scratch_shapes=_dma_sems(n),
    )(*rs)


def _reduce_scatter(gs, chip, core, tag):
    c_arr = core.astype(jnp.int32).reshape(1)
    idx = jnp.stack([chip, core]).astype(jnp.int32)
    gots = _rs_pair_exchange(gs, tag + "_pair_x")
    ps = [_rs_pair_add(g, got, c_arr, f"{tag}_pair_add{i}") for i, (g, got) in enumerate(zip(gs, gots))]
    qs = _rs_chip_exchange(ps, tag + "_chip_x")
    rs = [_rs_chip_add(p, q, idx, f"{tag}_chip_add{i}") for i, (p, q) in enumerate(zip(ps, qs))]
    return _rs_pair_share(rs, tag + "_share")


def _sum8(a, name):
    _, rows, width = a.shape
    tr = 8 * (rows // 8 if rows <= 64 else 7)
    assert rows % tr == 0

    def body(a_ref, o_ref):
        acc = a_ref[0]
        for d in range(1, N_DEV):
            acc = acc + a_ref[d]
        o_ref[...] = acc

    return pl.pallas_call(
        body, name=name, grid=(rows // tr,), in_specs=[pl.BlockSpec((N_DEV, tr, width), lambda i: (0, i, 0))],
        out_specs=pl.BlockSpec((tr, width), lambda i: (i, 0)), out_shape=jax.ShapeDtypeStruct((rows, width), F32),
        compiler_params=_cparams("parallel"),
    )(a)


MOD_COLS = 3 * D // N_CHIPS
MOD_TK = 512


def _mod_partial(c_all, mod_w, name):
    nk = D // MOD_TK

    def body(c_ref, w_ref, o_ref):
        l = pl.program_id(1)
        part = _bdot_nn(jax.nn.silu(c_ref[...]), w_ref[0])
        _accum(o_ref.at[0], part, l == 0)

    return pl.pallas_call(
        body, name=name, grid=(DEPTH, nk),
        in_specs=[pl.BlockSpec((N_DEV, MOD_TK), lambda i, l: (0, l)), pl.BlockSpec((1, MOD_TK, MOD_COLS), lambda i, l: (i, l, 0))],
        out_specs=pl.BlockSpec((1, N_DEV, MOD_COLS), lambda i, l: (i, 0, 0)),
        out_shape=jax.ShapeDtypeStruct((DEPTH, N_DEV, MOD_COLS), F32), compiler_params=_cparams("parallel", "arbitrary"),
    )(c_all, mod_w)


def _mod_w_grad(c_all, dmod, name):
    def body(c_ref, d_ref, o_ref):
        o_ref[0] = _dg(jax.nn.silu(c_ref[...]).astype(BF16), d_ref[0].astype(BF16), _TN)

    return pl.pallas_call(
        body, name=name, grid=(DEPTH, D // MOD_TK),
        in_specs=[pl.BlockSpec((N_DEV, MOD_TK), lambda i, l: (0, l)), pl.BlockSpec((1, N_DEV, MOD_COLS), lambda i, l: (i, 0, 0))],
        out_specs=pl.BlockSpec((1, MOD_TK, MOD_COLS), lambda i, l: (i, l, 0)),
        out_shape=jax.ShapeDtypeStruct((DEPTH, D, MOD_COLS), F32), compiler_params=_cparams("parallel", "parallel"),
    )(c_all, dmod)


ADAM_BLOCK_BYTES = 1 << 20


def _adamw(w, g, m, v, name):
    shape = w.shape
    cols = shape[-1]
    rows = w.size // cols
    w, g, m, v = (a.reshape(rows, cols) for a in (w, g, m, v))
    cap = max(8, ADAM_BLOCK_BYTES // (4 * cols))
    tr = rows if rows <= cap else max(d for d in range(8, cap + 1, 8) if rows % d == 0)
    c1 = 1.0 - ADAM_B1 ** ADAM_STEP
    c2 = 1.0 - ADAM_B2 ** ADAM_STEP

    def body(w_ref, g_ref, m_ref, v_ref, d_ref, nm_ref, nv_ref):
        gv = g_ref[...]
        mn = ADAM_B1 * m_ref[...] + (1.0 - ADAM_B1) * gv
        vn = ADAM_B2 * v_ref[...] + (1.0 - ADAM_B2) * (gv * gv)
        nm_ref[...] = mn
        nv_ref[...] = vn
        d_ref[...] = -ADAM_LR * ((mn / c1) / (jnp.sqrt(vn / c2) + ADAM_EPS) + ADAM_WD * w_ref[...])

    spec = pl.BlockSpec((tr, cols), lambda i: (i, 0))
    out = jax.ShapeDtypeStruct((rows, cols), F32)
    d, nm, nv = pl.pallas_call(
        body, name=name, grid=(rows // tr,), in_specs=[spec] * 4, out_specs=[spec] * 3, out_shape=[out] * 3,
        compiler_params=_cparams("parallel"),
    )(w, g, m, v)
    return d.reshape(shape), nm.reshape(shape), nv.reshape(shape)


W_NAMES = ("norm_g", "mod_w", "mod_b", "final_norm_g", "sg_w_in", "sg_w_out", "sg_ln_g", "sg_ln_b", "sg_w_spatial",
           "sg_b_spatial", "swa_w_in", "swa_w_out", "swa_sinks", "rwkv_w_in", "rwkv_w_out", "rwkv_mu", "rwkv_w0",
           "rwkv_w_lora", "rwkv_a0", "rwkv_a_lora", "rwkv_k_k", "rwkv_k_a", "rwkv_r_k", "rwkv_gn_g", "rwkv_gn_b")
SMALL = {"sg_ln_g": 1, "sg_ln_b": 1, "rwkv_mu": 1, "rwkv_w0": 1, "rwkv_w_lora": 2, "rwkv_a0": 1, "rwkv_a_lora": 2,
         "rwkv_k_k": 1, "rwkv_k_a": 1, "rwkv_gn_g": 1, "rwkv_gn_b": 1}
REPLICATED = ("norm_g", "final_norm_g", "sg_w_spatial", "sg_b_spatial", "swa_sinks", "rwkv_r_k")
KINDS = ("sg", "swa", "rwkv", "sg")


def _pad_to(flat, n):
    return jnp.pad(flat, (0, n - flat.shape[0]))


def _round_up(n, m):
    return -(-n // m) * m


def _join_shards(gathered, axis):
    return jnp.concatenate([gathered[s] for s in range(N_CHIPS)], axis=axis)


def _chip_blocks(full, axis):
    return jnp.stack(jnp.split(full, N_CHIPS, axis=axis)).reshape(N_CHIPS, -1)


def _gather_big(w, name):
    rows, cols = w.shape
    return _chip_gather(w.astype(BF16).reshape(2, rows // 2, cols), name).reshape(N_CHIPS, rows, cols)


def _gather_small(shards, name):
    flat = jnp.concatenate([shards[n].reshape(-1) for n in SMALL])
    rows = _round_up(flat.shape[0], 2 * 8 * LANES) // (2 * LANES)
    got = _chip_gather(_pad_to(flat, 2 * rows * LANES).reshape(2, rows, LANES), name).reshape(N_CHIPS, -1)
    out, off = {}, 0
    for n, axis in SMALL.items():
        size = shards[n].size
        out[n] = _join_shards(got[:, off:off + size].reshape((N_CHIPS,) + shards[n].shape), axis)
        off += size
    return out


def _lora_pad_rows(w):
    return jnp.pad(w, ((0, LORA_PAD - LORA), (0, 0)))


def _lo_cols(a):
    z = jnp.zeros(a.shape[:-1] + (LORA_PAD - LORA,), a.dtype)
    return jnp.concatenate([a[..., :LORA], z, a[..., LORA:], z], axis=-1)


def _lo_cols_inv(a):
    return jnp.concatenate([a[..., :LORA], a[..., LORA_PAD:LORA_PAD + LORA]], axis=-1)


def _rows_dim_major(w):
    return w.reshape(N_HEADS, HEAD, -1).swapaxes(0, 1).reshape(w.shape)


def _rows_head_major(w):
    return w.reshape(HEAD, N_HEADS, -1).swapaxes(0, 1).reshape(w.shape)


def kernel(x, c, positions, norm_g, mod_w, mod_b, final_norm_g, sg_w_in, sg_w_out, sg_ln_g, sg_ln_b, sg_w_spatial,
           sg_b_spatial, swa_w_in, swa_w_out, swa_sinks, rwkv_w_in, rwkv_w_out, rwkv_mu, rwkv_w0, rwkv_w_lora, rwkv_a0,
           rwkv_a_lora, rwkv_k_k, rwkv_k_a, rwkv_r_k, rwkv_gn_g, rwkv_gn_b, loss_target, m_norm_g, m_mod_w, m_mod_b,
           m_final_norm_g, m_sg_w_in, m_sg_w_out, m_sg_ln_g, m_sg_ln_b, m_sg_w_spatial, m_sg_b_spatial, m_swa_w_in,
           m_swa_w_out, m_swa_sinks, m_rwkv_w_in, m_rwkv_w_out, m_rwkv_mu, m_rwkv_w0, m_rwkv_w_lora, m_rwkv_a0,
           m_rwkv_a_lora, m_rwkv_k_k, m_rwkv_k_a, m_rwkv_r_k, m_rwkv_gn_g, m_rwkv_gn_b, v_norm_g, v_mod_w, v_mod_b,
           v_final_norm_g, v_sg_w_in, v_sg_w_out, v_sg_ln_g, v_sg_ln_b, v_sg_w_spatial, v_sg_b_spatial, v_swa_w_in,
           v_swa_w_out, v_swa_sinks, v_rwkv_w_in, v_rwkv_w_out, v_rwkv_mu, v_rwkv_w0, v_rwkv_w_lora, v_rwkv_a0,
           v_rwkv_a_lora, v_rwkv_k_k, v_rwkv_k_a, v_rwkv_r_k, v_rwkv_gn_g, v_rwkv_gn_b):
    given = dict(locals())
    w = {n: given[n] for n in W_NAMES}
    xi, yi, ci = _place()
    chip = 2 * xi + yi
    me = 4 * xi + 2 * yi + ci
    xs = [x[0]]

    c_all = _all_gather8(c, "gather_c")[:, 0, :]
    mod_part = _mod_partial(c_all, mod_w, "mod_fwd")
    mod_all = _all_gather8(mod_part, "gather_mod")[::2]
    mod_mine = lax.dynamic_index_in_dim(mod_all, me, axis=2, keepdims=False)
    mod = mod_mine.transpose(1, 0, 2).reshape(DEPTH, 3 * D) + mod_b
    shift, scale, gate = mod[:, :D], mod[:, D:2 * D], mod[:, 2 * D:]

    sg_in = [_gather_big(sg_w_in[j], f"gather_sg_w_in{j}") for j in range(2)]
    sg_out = [_gather_big(sg_w_out[j], f"gather_sg_w_out{j}").reshape(D, D) for j in range(2)]
    swa_in = _gather_big(swa_w_in[0], "gather_swa_w_in")
    swa_out = _gather_big(swa_w_out[0], "gather_swa_w_out").reshape(D, D)
    rw_in = _join_shards(_gather_big(rwkv_w_in[0], "gather_rwkv_w_in"), axis=1)
    rw_out = _gather_big(rwkv_w_out[0], "gather_rwkv_w_out").reshape(D, D)
    full = _gather_small(w, "gather_small")
    rw_main = _dim_major(rw_in[:, :RW_MAIN].reshape(D, 4, D)).reshape(D, RW_MAIN)
    rw_lo = _lo_cols(rw_in[:, RW_MAIN:])
    rw_out = _rows_dim_major(rw_out)
    mu = full["rwkv_mu"][0]
    rw_prm = dict(mu_main=_dim_major(mu[:RW_MAIN].reshape(4, D)).reshape(1, RW_MAIN), mu_lo=_lo_cols(mu[None, RW_MAIN:]),
                  w0=_dim_major(full["rwkv_w0"]), a0=_dim_major(full["rwkv_a0"]),
                  wl=_lora_pad_rows(_dim_major(full["rwkv_w_lora"][0])), al=_lora_pad_rows(_dim_major(full["rwkv_a_lora"][0])),
                  kkp=_param_compact(full["rwkv_k_k"][0]), kap=_param_compact(full["rwkv_k_a"][0]),
                  rkp=_param_compact(rwkv_r_k.reshape(-1)),
                  gn_g=_param_compact(full["rwkv_gn_g"][0]), gn_b=_param_compact(full["rwkv_gn_b"][0]))
    bs_t = [jnp.pad(sg_b_spatial[j].T, ((0, 0), (0, LANES - SG_GROUPS))) for j in range(2)]
    sink_row = jnp.pad(swa_sinks, ((0, 0), (0, LANES - N_HEADS)))
    inv_freq = ROPE_THETA ** (-jnp.arange(HEAD // 2, dtype=F32) / (HEAD // 2))
    ang = positions[0].astype(F32)[:, None] * inv_freq
    cos, sin = jnp.tile(jnp.cos(ang), (1, LANES * 2 // HEAD)), jnp.tile(jnp.sin(ang), (1, LANES * 2 // HEAD))

    def row(a, i):
        return a[i:i + 1]

    hs, ps, us, ys, rw_saved = [], [], [], [], None
    for i, kind in enumerate(KINDS):
        j = i // 3
        tag = f"l{i}_{kind}"
        h = _norm_mod_fwd(xs[i], row(norm_g, i), row(shift, i), row(scale, i), tag + "_norm")
        if kind == "sg":
            p = _matmul(h, sg_in[j], "nn", tag + "_in", blocked=True)
            u = _sg_fwd(p, row(full["sg_ln_g"], j), row(full["sg_ln_b"], j), sg_w_spatial[j], bs_t[j], tag + "_mix")
            w_out = sg_out[j]
        elif kind == "swa":
            p = _matmul(h, swa_in, "nn", tag + "_in", blocked=True)
            u = _swa_fwd(p, cos, sin, sink_row, tag + "_mix")
            w_out = swa_out
        else:
            p = (_matmul(h, rw_main, "nn", tag + "_in"), _matmul(h, rw_lo, "nn", tag + "_in_lo"))
            u, rw_saved = _rwkv_mixer_fwd(p[0], p[1], rw_prm, tag)
            w_out = rw_out
        y = _matmul(u, w_out, "nn", tag + "_out")
        xs.append(_resid_gate(xs[i], y, row(gate, i), tag + "_resid"))
        hs.append(h), ps.append(p), us.append(u), ys.append(y)

    loss_part, dx, d_final_g = _final_loss_grad(xs[DEPTH], final_norm_g[None], loss_target[0], "loss")
    loss = lax.psum(loss_part[0, 0], ("x", "y", "c"))

    gfull = {n: [None, None] for n in ("sg_ln_g", "sg_ln_b", "sg_w_spatial", "sg_b_spatial")}
    gbig = {}
    d_norm_g, d_mod = [None] * DEPTH, [None] * DEPTH
    for i in reversed(range(DEPTH)):
        kind, j = KINDS[i], i // 3
        tag = f"l{i}_{kind}_b"
        dy, d_gate = _gate_bwd(dx, ys[i], row(gate, i), tag + "_gate")
        w_out = {"sg": sg_out[j], "swa": swa_out, "rwkv": rw_out}[kind]
        du = _matmul(dy, w_out, "nt", tag + "_du")
        dw_out = _matmul(us[i], dy, "tn", tag + "_dwout").reshape(N_CHIPS, D // N_CHIPS, D)
        if kind == "sg":
            dp, dlg, dlb, dws, dbs = _sg_bwd(ps[i], row(full["sg_ln_g"], j), row(full["sg_ln_b"], j), sg_w_spatial[j],
                                             bs_t[j], du, tag + "_mix")
            gfull["sg_ln_g"][j], gfull["sg_ln_b"][j] = dlg[0], dlb[0]
            gfull["sg_w_spatial"][j], gfull["sg_b_spatial"][j] = dws, dbs[:, :SG_GROUPS].T
            gbig[f"sg_w_in{j}"] = _matmul(hs[i], dp, "tn", tag + "_dwin", blocked=True)
            gbig[f"sg_w_out{j}"] = dw_out
            dh, dh2 = _matmul(dp, sg_in[j], "nt", tag + "_dh", blocked=True), None
        elif kind == "swa":
            dp, dsk = _swa_bwd(ps[i], cos, sin, sink_row, du, tag + "_mix")
            gfull["swa_sinks"] = dsk[:, :N_HEADS]
            gbig["swa_w_in"] = _matmul(hs[i], dp, "tn", tag + "_dwin", blocked=True)
            gbig["swa_w_out"] = dw_out
            dh, dh2 = _matmul(dp, swa_in, "nt", tag + "_dh", blocked=True), None
        else:
            dpm, dpl, rg = _rwkv_mixer_bwd(ps[i][0], ps[i][1], rw_prm, rw_saved, du, tag)
            dw_main = _matmul(hs[i], dpm, "tn", tag + "_dwin")
            dw_lo = _matmul(hs[i], dpl, "tn", tag + "_dwin_lo")
            dw_main = _head_major(dw_main.reshape(D, 4, D)).reshape(D, RW_MAIN)
            dw_in = jnp.concatenate([dw_main, _lo_cols_inv(dw_lo)], axis=1)
            gbig["rwkv_w_in"] = dw_in.reshape(D, N_CHIPS, -1).transpose(1, 0, 2)
            gbig["rwkv_w_out"] = _rows_head_major(dw_out.reshape(D, D)).reshape(dw_out.shape)
            dmu_main = _head_major(rg["mu_main"].reshape(4, D)).reshape(1, RW_MAIN)
            gfull["rwkv_mu"] = jnp.concatenate([dmu_main, _lo_cols_inv(rg["mu_lo"])], axis=1)
            gfull["rwkv_w0"], gfull["rwkv_a0"] = _head_major(rg["w0"]), _head_major(rg["a0"])
            gfull["rwkv_w_lora"], gfull["rwkv_a_lora"] = _head_major(rg["wl"])[None, :LORA], _head_major(rg["al"])[None, :LORA]
            gfull["rwkv_k_k"], gfull["rwkv_k_a"] = _param_compact_inv(rg["kkp"])[None], _param_compact_inv(rg["kap"])[None]
            gfull["rwkv_r_k"] = _param_compact_inv(rg["rkp"]).reshape(1, N_HEADS, HEAD)
            gfull["rwkv_gn_g"], gfull["rwkv_gn_b"] = _param_compact_inv(rg["gn_g"])[None], _param_compact_inv(rg["gn_b"])[None]
            dh, dh2 = _matmul(dpm, rw_main, "nt", tag + "_dh"), _matmul(dpl, rw_lo, "nt", tag + "_dh_lo")
        dx, dg, dsh, dsc = _norm_mod_bwd(xs[i], row(norm_g, i), row(shift, i), row(scale, i), dh, dx, tag + "_norm", dh2)
        d_norm_g[i] = dg[0]
        d_mod[i] = jnp.concatenate([dsh[0], dsc[0], d_gate[0]])
    for n in ("sg_ln_g", "sg_ln_b", "sg_w_spatial", "sg_b_spatial"):
        gfull[n] = jnp.stack(gfull[n])
    gfull["norm_g"], gfull["final_norm_g"] = jnp.stack(d_norm_g), d_final_g[0]

    small = jnp.concatenate([_chip_blocks(gfull[n], axis) for n, axis in SMALL.items()], axis=1)
    small_rows = _round_up(small.shape[1], 2 * 16 * LANES) // (2 * LANES)
    small = jnp.pad(small, ((0, 0), (0, 2 * small_rows * LANES - small.shape[1])))
    rs_names = sorted(gbig)
    rs_in = [gbig[n].reshape(N_CHIPS, 2, gbig[n].shape[1] // 2, gbig[n].shape[2]) for n in rs_names]
    rs_out = _reduce_scatter(rs_in + [small.reshape(N_CHIPS, 2, small_rows, LANES)], chip, ci, "rs")
    red = {n: r.reshape(-1, r.shape[2]) for n, r in zip(rs_names, rs_out)}
    grads = {"sg_w_in": jnp.stack([red["sg_w_in0"], red["sg_w_in1"]]), "sg_w_out": jnp.stack([red["sg_w_out0"], red["sg_w_out1"]])}
    for n in ("swa_w_in", "swa_w_out", "rwkv_w_in", "rwkv_w_out"):
        grads[n] = red[n][None]
    small_red, off = rs_out[-1].reshape(-1), 0
    for n in SMALL:
        grads[n] = small_red[off:off + w[n].size].reshape(w[n].shape)
        off += w[n].size

    rep_flat = jnp.concatenate([jnp.stack(d_mod).reshape(-1)] + [gfull[n].reshape(-1) for n in REPLICATED])
    rep_rows = _round_up(rep_flat.shape[0], 56 * RS_W) // RS_W
    rep_all = _all_gather8(_pad_to(rep_flat, rep_rows * RS_W).reshape(rep_rows, RS_W), "gather_rep")
    rep_sum = _sum8(rep_all, "sum_rep").reshape(-1)
    grads["mod_b"] = rep_sum[:DEPTH * 3 * D].reshape(DEPTH, 3 * D)
    off = DEPTH * 3 * D
    for n in REPLICATED:
        grads[n] = rep_sum[off:off + w[n].size].reshape(w[n].shape)
        off += w[n].size
    dmod_all = rep_all.reshape(N_DEV, -1)[:, :DEPTH * 3 * D].reshape(N_DEV, DEPTH, 3 * D)
    dmod_cols = lax.dynamic_slice_in_dim(dmod_all, chip * MOD_COLS, MOD_COLS, axis=2).transpose(1, 0, 2)
    grads["mod_w"] = _mod_w_grad(c_all, dmod_cols, "mod_w_grad")

    deltas, new_m, new_v = {}, {}, {}
    for n in W_NAMES:
        deltas[n], new_m[n], new_v[n] = _adamw(w[n], grads[n], given["m_" + n], given["v_" + n], "adamw_" + n)
    return (loss, dx[None], *[grads[n] for n in W_NAMES], *[deltas[n] for n in W_NAMES],
            *[new_m[n] for n in W_NAMES], *[new_v[n] for n in W_NAMES])
```

```python
import functools
import math

import jax
import jax.numpy as jnp
from jax import lax
from jax.experimental import pallas as pl
from jax.experimental.pallas import tpu as pltpu

F32 = jnp.float32
BF16 = jnp.bfloat16
HIGHEST = lax.Precision.HIGHEST

D = 2048
DEPTH = 4
CHUNK = 128
SG_GROUPS = 16
HEAD = 64
N_HEADS = D // HEAD
KV_HEADS = 4
KVW = KV_HEADS * HEAD
ROPE_THETA = 10000.0
LORA = 96
LORA_PAD = 128
DECAY_SCALE = math.exp(-0.5)
GN_EPS = 64e-5
RMS_EPS = 1e-6
LN_EPS = 1e-5
ADAM_LR, ADAM_B1, ADAM_B2, ADAM_EPS, ADAM_WD, ADAM_STEP = 0.001, 0.9, 0.999, 1e-08, 0.01, 10
LANES = 128
SUB = 8
NEG = -1e30
VMEM_LIMIT = 56 * 1024 * 1024

MESHT = pl.DeviceIdType.MESH


def _cparams(*sem):
    return pltpu.CompilerParams(dimension_semantics=sem, vmem_limit_bytes=VMEM_LIMIT)


class _Rider:
    def __init__(self, arrays, n_sems, start, finish):
        self.arrays, self.n_sems, self.start, self.finish = list(arrays), n_sems, start, finish


def _ridden(res, rider):
    return (res, []) if rider is None else res


def _compute_call(body, args, *, name, grid, in_specs, out_specs, out_shape, semantics, scratch_shapes=(), rider=None):
    if rider is None:
        return pl.pallas_call(body, name=name, grid=grid, in_specs=in_specs, out_specs=out_specs, out_shape=out_shape,
                              scratch_shapes=list(scratch_shapes), compiler_params=_cparams(*semantics))(*args)
    single = not isinstance(out_shape, (list, tuple))
    o_specs, o_shapes = ([out_specs], [out_shape]) if single else (list(out_specs), list(out_shape))
    n_in, n_out, n_r, n_scr = len(in_specs), len(o_specs), len(rider.arrays), len(scratch_shapes)

    def with_rider(*refs):
        ins, outs = refs[:n_in], refs[n_in + n_r:n_in + n_r + n_out]
        ridden = refs[n_in + n_r + n_out:n_in + 2 * n_r + n_out]
        scratch, (send_sems, recv_sems) = refs[n_in + 2 * n_r + n_out:-2], refs[-2:]
        ids = [pl.program_id(d) for d in range(len(grid))]
        first = functools.reduce(jnp.logical_and, [i == 0 for i in ids])
        last = functools.reduce(jnp.logical_and, [i == g - 1 for i, g in zip(ids, grid)])

        @pl.when(first)
        def _():
            rider.start(ridden, send_sems, recv_sems)

        body(*ins, *outs, *scratch)

        @pl.when(last)
        def _():
            rider.finish(ridden, send_sems, recv_sems)

    any_spec = pl.BlockSpec(memory_space=pl.ANY)
    res = pl.pallas_call(
        with_rider, name=name, grid=grid, in_specs=list(in_specs) + [any_spec] * n_r, out_specs=o_specs + [any_spec] * n_r,
        out_shape=o_shapes + [jax.ShapeDtypeStruct(a.shape, a.dtype) for a in rider.arrays],
        input_output_aliases={n_in + i: n_out + i for i in range(n_r)},
        scratch_shapes=list(scratch_shapes) + [pltpu.SemaphoreType.DMA((rider.n_sems,))] * 2,
        compiler_params=_cparams(*["arbitrary"] * len(grid)),
    )(*args, *rider.arrays)
    return (res[0] if single else list(res[:n_out])), list(res[n_out:])


_NN = (((1,), (0,)), ((), ()))
_NT = (((1,), (1,)), ((), ()))
_TN = (((0,), (0,)), ((), ()))


def _dg(a, b, dims):
    return lax.dot_general(a, b, dims, preferred_element_type=F32)


@jax.custom_vjp
def _bdot_nn(a, b):
    return _dg(a.astype(BF16), b.astype(BF16), _NN)


def _bdot_nn_fwd(a, b):
    a, b = a.astype(BF16), b.astype(BF16)
    return _dg(a, b, _NN), (a, b)


def _bdot_nn_bwd(res, ct):
    a, b = res
    ct = ct.astype(BF16)
    return _dg(ct, b, _NT), _dg(a, ct, _TN)


_bdot_nn.defvjp(_bdot_nn_fwd, _bdot_nn_bwd)


@jax.custom_vjp
def _bdot_nt(a, b):
    return _dg(a.astype(BF16), b.astype(BF16), _NT)


def _bdot_nt_fwd(a, b):
    a, b = a.astype(BF16), b.astype(BF16)
    return _dg(a, b, _NT), (a, b)


def _bdot_nt_bwd(res, ct):
    a, b = res
    ct = ct.astype(BF16)
    return _dg(ct, b, _NN), _dg(ct, a, _TN)


_bdot_nt.defvjp(_bdot_nt_fwd, _bdot_nt_bwd)


def _tile(n, cap):
    if n <= cap:
        return n
    return max(d for d in range(LANES, cap + 1, LANES) if n % d == 0)


def _matmul(a, b, form, name, out_dtype=F32, blocked=False, rider=None, tm=1024, tn=512, tk=4096):
    if form == "nn":
        (m, k), n = a.shape, (N_CHIPS * b.shape[2] if blocked else b.shape[1])
    elif form == "nt":
        m, k, n = a.shape[0], a.shape[1], (b.shape[1] if blocked else b.shape[0])
    else:
        (k, m), n = a.shape, b.shape[1]
    per_chip = (k if form == "nt" else n) // N_CHIPS
    if blocked and form == "nt":
        tk = _tile(per_chip, tk)
    elif blocked:
        tn = _tile(per_chip, tn)
    tm, tn, tk = _tile(m, tm), _tile(n, tn), _tile(k, tk)
    assert m % tm == 0 and n % tn == 0 and k % tk == 0, (name, a.shape, b.shape)
    nk = k // tk
    dims = {"nn": _NN, "nt": _NT, "tn": _TN}[form]
    a_spec = pl.BlockSpec((tk, tm), lambda i, j, l: (l, i)) if form == "tn" else pl.BlockSpec((tm, tk), lambda i, j, l: (i, l))
    b_spec = pl.BlockSpec((tn, tk), lambda i, j, l: (j, l)) if form == "nt" else pl.BlockSpec((tk, tn), lambda i, j, l: (l, j))
    o_spec = pl.BlockSpec((tm, tn), lambda i, j, l: (i, j))
    o_shape = (m, n)
    if blocked and form == "nn":
        pc = per_chip // tn
        b_spec = pl.BlockSpec((None, tk, tn), lambda i, j, l: (j // pc, l, j % pc))
    elif blocked and form == "nt":
        pc = per_chip // tk
        b_spec = pl.BlockSpec((None, tn, tk), lambda i, j, l: (l // pc, j, l % pc))
    elif blocked:
        pc = per_chip // tn
        o_spec = pl.BlockSpec((None, tm, tn), lambda i, j, l: (j // pc, i, j % pc))
        o_shape = (N_CHIPS, m, per_chip)

    def body(a_ref, b_ref, o_ref, acc_ref):
        part = _dg(a_ref[...], b_ref[...], dims)
        if nk == 1:
            o_ref[...] = part.astype(out_dtype)
        else:
            l = pl.program_id(2)

            @pl.when(l == 0)
            def _():
                acc_ref[...] = part

            @pl.when(l > 0)
            def _():
                acc_ref[...] += part

            @pl.when(l == nk - 1)
            def _():
                o_ref[...] = acc_ref[...].astype(out_dtype)

    return _compute_call(
        body, (a, b), name=name, grid=(m // tm, n // tn, nk),
        in_specs=[a_spec, b_spec], out_specs=o_spec, out_shape=jax.ShapeDtypeStruct(o_shape, out_dtype),
        scratch_shapes=[pltpu.VMEM((tm, tn) if nk > 1 else (8, LANES), F32)],
        semantics=("parallel", "parallel", "arbitrary"), rider=rider)


TB_NORM = 256


def _f_norm_mod(x, g, shift, scale):
    xn = x * lax.rsqrt(jnp.mean(x * x, axis=-1, keepdims=True) + RMS_EPS)
    return (xn * g) * (1.0 + scale) + shift


def _row_spec(width, tb=TB_NORM):
    return pl.BlockSpec((tb, width), lambda i: (i, 0))


def _vec_spec(width, rows=1):
    return pl.BlockSpec((rows, width), lambda i: (0, 0))


def _norm_mod_fwd(x, g, shift, scale, name):
    t = x.shape[0]

    def body(x_ref, g_ref, sh_ref, sc_ref, h_ref):
        h_ref[...] = _f_norm_mod(x_ref[...], g_ref[...], sh_ref[...], sc_ref[...]).astype(BF16)

    return pl.pallas_call(
        body, name=name, grid=(t // TB_NORM,),
        in_specs=[_row_spec(D), _vec_spec(D), _vec_spec(D), _vec_spec(D)], out_specs=_row_spec(D),
        out_shape=jax.ShapeDtypeStruct((t, D), BF16), compiler_params=_cparams("parallel"),
    )(x, g, shift, scale)


def _accum(ref, val, first):
    @pl.when(first)
    def _():
        ref[...] = val

    @pl.when(jnp.logical_not(first))
    def _():
        ref[...] += val


def _norm_mod_bwd(x, g, shift, scale, dh, dx_res, name, dh2=None):
    t = x.shape[0]
    dhs = [dh] if dh2 is None else [dh, dh2]

    def body(x_ref, g_ref, sh_ref, sc_ref, dr_ref, *refs):
        dh_refs, (dx_ref, dg_ref, dsh_ref, dsc_ref) = refs[:len(dhs)], refs[len(dhs):]
        _, vjp = jax.vjp(_f_norm_mod, x_ref[...], g_ref[...], sh_ref[...], sc_ref[...])
        dh_all = dh_refs[0][...]
        for r in dh_refs[1:]:
            dh_all = dh_all + r[...]
        dx, dg, dsh, dsc = vjp(dh_all)
        dx_ref[...] = dx + dr_ref[...]
        first = pl.program_id(0) == 0
        _accum(dg_ref, dg, first)
        _accum(dsh_ref, dsh, first)
        _accum(dsc_ref, dsc, first)

    vec = jax.ShapeDtypeStruct((1, D), F32)
    return pl.pallas_call(
        body, name=name, grid=(t // TB_NORM,),
        in_specs=[_row_spec(D), _vec_spec(D), _vec_spec(D), _vec_spec(D), _row_spec(D)] + [_row_spec(D)] * len(dhs),
        out_specs=[_row_spec(D), _vec_spec(D), _vec_spec(D), _vec_spec(D)],
        out_shape=[jax.ShapeDtypeStruct((t, D), F32), vec, vec, vec], compiler_params=_cparams("arbitrary"),
    )(x, g, shift, scale, dx_res, *dhs)


def _resid_gate(x, y, gate, name):
    t = x.shape[0]

    def body(x_ref, y_ref, g_ref, o_ref):
        o_ref[...] = x_ref[...] + g_ref[...] * y_ref[...]

    return pl.pallas_call(
        body, name=name, grid=(t // TB_NORM,),
        in_specs=[_row_spec(D), _row_spec(D), _vec_spec(D)], out_specs=_row_spec(D),
        out_shape=jax.ShapeDtypeStruct((t, D), F32), compiler_params=_cparams("parallel"),
    )(x, y, gate)


def _gate_bwd(dx, y, gate, name):
    t = dx.shape[0]

    def body(dx_ref, y_ref, g_ref, dy_ref, dg_ref):
        dxv = dx_ref[...]
        dy_ref[...] = (dxv * g_ref[...]).astype(BF16)
        _accum(dg_ref, jnp.sum(dxv * y_ref[...], axis=0, keepdims=True), pl.program_id(0) == 0)

    return pl.pallas_call(
        body, name=name, grid=(t // TB_NORM,),
        in_specs=[_row_spec(D), _row_spec(D), _vec_spec(D)], out_specs=[_row_spec(D), _vec_spec(D)],
        out_shape=[jax.ShapeDtypeStruct((t, D), BF16), jax.ShapeDtypeStruct((1, D), F32)],
        compiler_params=_cparams("arbitrary"),
    )(dx, y, gate)


def _f_final(x, g, target):
    xn = x * lax.rsqrt(jnp.mean(x * x, axis=-1, keepdims=True) + RMS_EPS)
    err = xn * g - target
    return 0.5 * jnp.sum(jnp.mean(err * err, axis=-1, keepdims=True), axis=0, keepdims=True)


def _final_loss_grad(x, g, target, name):
    t = x.shape[0]

    def body(x_ref, g_ref, t_ref, loss_ref, dx_ref, dg_ref):
        loss, vjp = jax.vjp(_f_final, x_ref[...], g_ref[...], t_ref[...])
        dx, dg, _ = vjp(jnp.ones((1, 1), F32))
        dx_ref[...] = dx
        first = pl.program_id(0) == 0
        _accum(dg_ref, dg, first)
        _accum(loss_ref, jnp.broadcast_to(loss, (1, LANES)), first)

    return pl.pallas_call(
        body, name=name, grid=(t // TB_NORM,),
        in_specs=[_row_spec(D), _vec_spec(D), _row_spec(D)],
        out_specs=[_vec_spec(LANES), _row_spec(D), _vec_spec(D)],
        out_shape=[jax.ShapeDtypeStruct((1, LANES), F32), jax.ShapeDtypeStruct((t, D), F32), jax.ShapeDtypeStruct((1, D), F32)],
        compiler_params=_cparams("arbitrary"),
    )(x, g, target)


def _group_selector():
    gi = lax.broadcasted_iota(jnp.int32, (LANES, D), 0)
    ci = lax.broadcasted_iota(jnp.int32, (LANES, D), 1)
    return (ci // (D // SG_GROUPS) == gi).astype(F32)


def _f_sg(p, ln_g, ln_b, w_s, bs_t):
    u, v, z = p[:, :D], p[:, D:2 * D], p[:, 2 * D:]
    u = jax.nn.gelu(u)
    vf = jax.nn.gelu(v)
    mean = jnp.mean(vf, axis=-1, keepdims=True)
    var = jnp.mean(jnp.square(vf - mean), axis=-1, keepdims=True)
    vn = (vf - mean) * lax.rsqrt(var + LN_EPS) * ln_g + ln_b
    ti = lax.broadcasted_iota(jnp.int32, (CHUNK, CHUNK), 0)
    si = lax.broadcasted_iota(jnp.int32, (CHUNK, CHUNK), 1)
    causal = si <= ti
    cg = D // SG_GROUPS
    f = jnp.concatenate(
        [_bdot_nn(jnp.where(causal, w_s[g], 0.0), vn[:, g * cg:(g + 1) * cg]) for g in range(SG_GROUPS)], axis=1)
    f = f + jnp.dot(bs_t, _group_selector(), precision=HIGHEST, preferred_element_type=F32)
    return u * f * jax.nn.silu(z)


def _sg_specs():
    return [pl.BlockSpec((CHUNK, 3 * D), lambda i: (i, 0)), _vec_spec(D), _vec_spec(D),
            pl.BlockSpec((SG_GROUPS, CHUNK, CHUNK), lambda i: (0, 0, 0)), _vec_spec(LANES, CHUNK)]


def _sg_fwd(p, ln_g, ln_b, w_s, bs_t, name, rider=None):
    t = p.shape[0]

    def body(p_ref, lg_ref, lb_ref, w_ref, b_ref, o_ref):
        o_ref[...] = _f_sg(p_ref[...], lg_ref[...], lb_ref[...], w_ref[...], b_ref[...]).astype(BF16)

    return _compute_call(
        body, (p, ln_g, ln_b, w_s, bs_t), name=name, grid=(t // CHUNK,), in_specs=_sg_specs(),
        out_specs=_row_spec(D, CHUNK), out_shape=jax.ShapeDtypeStruct((t, D), BF16), semantics=("parallel",), rider=rider)


def _sg_bwd(p, ln_g, ln_b, w_s, bs_t, dout, name, rider=None):
    t = p.shape[0]

    def body(p_ref, lg_ref, lb_ref, w_ref, b_ref, do_ref, dp_ref, dlg_ref, dlb_ref, dw_ref, db_ref):
        _, vjp = jax.vjp(_f_sg, p_ref[...], lg_ref[...], lb_ref[...], w_ref[...], b_ref[...])
        dp, dlg, dlb, dw, db = vjp(do_ref[...])
        dp_ref[...] = dp.astype(BF16)
        first = pl.program_id(0) == 0
        _accum(dlg_ref, dlg, first)
        _accum(dlb_ref, dlb, first)
        _accum(dw_ref, dw, first)
        _accum(db_ref, db, first)

    vec = jax.ShapeDtypeStruct((1, D), F32)
    return _compute_call(
        body, (p, ln_g, ln_b, w_s, bs_t, dout), name=name, grid=(t // CHUNK,), in_specs=_sg_specs() + [_row_spec(D, CHUNK)],
        out_specs=[pl.BlockSpec((CHUNK, 3 * D), lambda i: (i, 0)), _vec_spec(D), _vec_spec(D),
                   pl.BlockSpec((SG_GROUPS, CHUNK, CHUNK), lambda i: (0, 0, 0)), _vec_spec(LANES, CHUNK)],
        out_shape=[jax.ShapeDtypeStruct((t, 3 * D), BF16), vec, vec,
                   jax.ShapeDtypeStruct((SG_GROUPS, CHUNK, CHUNK), F32), jax.ShapeDtypeStruct((CHUNK, LANES), F32)],
        semantics=("arbitrary",), rider=rider)


SWA_COLS = 2 * D + 2 * KVW
KV_BLOCK = 2 * KVW


def _lane_roll(x, shift):
    return pltpu.roll(x, shift, 1)


def _rot_half(x):
    w = x.shape[1]
    lane = lax.broadcasted_iota(jnp.int32, x.shape, 1)
    return jnp.where(lane % HEAD < HEAD // 2, -_lane_roll(x, w - HEAD // 2), _lane_roll(x, HEAD // 2))


@jax.custom_vjp
def _rope(x, cos, sin):
    return x * cos + _rot_half(x) * sin


def _rope_fwd(x, cos, sin):
    return _rope(x, cos, sin), (cos, sin)


def _rope_bwd(res, ct):
    cos, sin = res
    return ct * cos - _rot_half(ct) * sin, jnp.zeros_like(cos), jnp.zeros_like(sin)


_rope.defvjp(_rope_fwd, _rope_bwd)


@jax.custom_vjp
def _swap_halves(x):
    return _lane_roll(x, HEAD)


_swap_halves.defvjp(lambda x: (_lane_roll(x, HEAD), None), lambda _, ct: (_lane_roll(ct, HEAD),))


def _f_swa(pq, pkv, cos, sin, cosp, sinp, sink_row, valid):
    reps = D // LANES
    q = _rope(pq[:, :D], jnp.tile(cos, (1, reps)), jnp.tile(sin, (1, reps))) * (HEAD ** -0.5)
    k = _rope(pq[:, D:D + KVW], jnp.tile(cos, (1, KVW // LANES)), jnp.tile(sin, (1, KVW // LANES)))
    kp = _rope(pkv[:, :KVW], jnp.tile(cosp, (1, KVW // LANES)), jnp.tile(sinp, (1, KVW // LANES)))
    v, vp, z = pq[:, D + KVW:D + 2 * KVW], pkv[:, KVW:], pq[:, D + 2 * KVW:]
    kcat = jnp.concatenate([kp, k], axis=0)
    vcat = jnp.concatenate([vp, v], axis=0)
    lane = lax.broadcasted_iota(jnp.int32, (2 * CHUNK, LANES), 1)
    lo = lane < HEAD
    hlane = lax.broadcasted_iota(jnp.int32, (1, LANES), 1)

    def halves(cat, g):
        blk = cat[:, (g // 2) * LANES:(g // 2 + 1) * LANES]
        other = _swap_halves(blk)
        if g % 2 == 0:
            return jnp.where(lo, blk, 0.0), jnp.where(lo, 0.0, other)
        return jnp.where(lo, other, 0.0), jnp.where(lo, 0.0, blk)

    def probs(s, head):
        sink = jnp.sum(jnp.where(hlane == head, sink_row, 0.0), axis=1, keepdims=True)
        s = jnp.where(valid, s, NEG)
        m = lax.stop_gradient(jnp.maximum(jnp.max(s, axis=1, keepdims=True), sink))
        e = jnp.exp(s - m)
        return e / (jnp.sum(e, axis=1, keepdims=True) + jnp.exp(sink - m))

    outs = []
    rep = N_HEADS // KV_HEADS
    for g in range(KV_HEADS):
        k_lo, k_hi = halves(kcat, g)
        v_lo, v_hi = halves(vcat, g)
        for j in range(g * rep // 2, (g + 1) * rep // 2):
            qp = q[:, j * LANES:(j + 1) * LANES]
            p_a = probs(_bdot_nt(qp, k_lo), 2 * j)
            p_b = probs(_bdot_nt(qp, k_hi), 2 * j + 1)
            outs.append(_bdot_nn(p_a, v_lo) + _bdot_nn(p_b, v_hi))
    return jnp.concatenate(outs, axis=1) * jax.nn.silu(z)


def _swa_valid(block):
    qi = lax.broadcasted_iota(jnp.int32, (CHUNK, 2 * CHUNK), 0)
    kj = lax.broadcasted_iota(jnp.int32, (CHUNK, 2 * CHUNK), 1)
    rel = qi + CHUNK - kj
    return (rel >= 0) & (rel < CHUNK) & ((kj >= CHUNK) | (block > 0))


def _swa_specs(blk):
    prev = lambda i: jnp.maximum(blk(i) - 1, 0)
    kv_col = D // KV_BLOCK
    return [pl.BlockSpec((CHUNK, SWA_COLS), lambda i: (blk(i), 0)),
            pl.BlockSpec((CHUNK, KV_BLOCK), lambda i: (prev(i), kv_col)),
            pl.BlockSpec((CHUNK, LANES), lambda i: (blk(i), 0)), pl.BlockSpec((CHUNK, LANES), lambda i: (blk(i), 0)),
            pl.BlockSpec((CHUNK, LANES), lambda i: (prev(i), 0)), pl.BlockSpec((CHUNK, LANES), lambda i: (prev(i), 0)),
            _vec_spec(LANES)]


def _swa_fwd(p, cos, sin, sink_row, name, rider=None):
    t = p.shape[0]

    def body(pq_ref, pkv_ref, c_ref, s_ref, cp_ref, sp_ref, sk_ref, o_ref):
        valid = _swa_valid(pl.program_id(0))
        o_ref[...] = _f_swa(pq_ref[...], pkv_ref[...], c_ref[...], s_ref[...], cp_ref[...], sp_ref[...],
                            sk_ref[...], valid).astype(BF16)

    return _compute_call(
        body, (p, p, cos, sin, cos, sin, sink_row), name=name, grid=(t // CHUNK,), in_specs=_swa_specs(lambda i: i),
        out_specs=_row_spec(D, CHUNK), out_shape=jax.ShapeDtypeStruct((t, D), BF16), semantics=("parallel",), rider=rider)


def _swa_bwd(p, cos, sin, sink_row, dout, name, rider=None):
    t = p.shape[0]
    nb = t // CHUNK
    blk = lambda i: nb - 1 - i

    def body(pq_ref, pkv_ref, c_ref, s_ref, cp_ref, sp_ref, sk_ref, do_ref, dp_ref, dsk_ref, pend_ref):
        i = pl.program_id(0)
        valid = _swa_valid(blk(i))
        f = functools.partial(_f_swa, valid=valid)
        _, vjp = jax.vjp(f, pq_ref[...], pkv_ref[...], c_ref[...], s_ref[...], cp_ref[...], sp_ref[...], sk_ref[...])
        dpq, dpkv, _, _, _, _, dsk = vjp(do_ref[...])

        @pl.when(i == 0)
        def _():
            pend_ref[...] = jnp.zeros_like(pend_ref)

        dp_ref[...] = jnp.concatenate(
            [dpq[:, :D], dpq[:, D:D + KV_BLOCK] + pend_ref[...], dpq[:, D + KV_BLOCK:]], axis=1).astype(BF16)
        pend_ref[...] = dpkv
        _accum(dsk_ref, dsk, i == 0)

    return _compute_call(
        body, (p, p, cos, sin, cos, sin, sink_row, dout), name=name, grid=(nb,),
        in_specs=_swa_specs(blk) + [pl.BlockSpec((CHUNK, D), lambda i: (blk(i), 0))],
        out_specs=[pl.BlockSpec((CHUNK, SWA_COLS), lambda i: (blk(i), 0)), _vec_spec(LANES)],
        out_shape=[jax.ShapeDtypeStruct((t, SWA_COLS), BF16), jax.ShapeDtypeStruct((1, LANES), F32)],
        scratch_shapes=[pltpu.VMEM((CHUNK, KV_BLOCK), F32)], semantics=("arbitrary",), rider=rider)


RW_MAIN = 4 * D
RW_LO = 2 * LORA_PAD
VM = LANES // N_HEADS
VD = HEAD // VM
S_ROWS = VD * HEAD
TB_RW = 128
TB_K = 32
TB_SCAN = 16


def _dim_major(a):
    return a.reshape(a.shape[:-1] + (N_HEADS, HEAD)).swapaxes(-1, -2).reshape(a.shape)


def _head_major(a):
    return a.reshape(a.shape[:-1] + (HEAD, N_HEADS)).swapaxes(-1, -2).reshape(a.shape)


def _compact(x):
    return x.reshape(x.shape[0], VD, LANES)


def _param_compact(w):
    return _dim_major(w).reshape(VD, LANES)


def _param_compact_inv(pc):
    return _head_major(pc.reshape(-1))


def _f_rwkv_lora(xs_lo, w0, a0, wl, al):
    decay = jnp.exp(-DECAY_SCALE * jax.nn.sigmoid(w0 + _bdot_nn(jnp.tanh(xs_lo[:, :LORA_PAD]), wl)))
    a = jax.nn.sigmoid(a0 + _bdot_nn(xs_lo[:, LORA_PAD:], al))
    return decay, a


def _prev_rows_spec(width, tb):
    return pl.BlockSpec((8, width), lambda i: (jnp.maximum(i * (tb // 8) - 1, 0), 0))


def _token_shift_lerp(p, prev8, mu, first):
    rows = lax.broadcasted_iota(jnp.int32, p.shape, 0)
    prev = jnp.where(first, 0.0, prev8[7:8, :])
    shifted = jnp.where(rows == 0, prev, pltpu.roll(p, 1, 0))
    return p + (shifted - p) * mu


def _rwkv_pre_fwd(p_main, p_lo, mu_main, mu_lo, w0, a0, wl, al, name):
    t = p_main.shape[0]
    tb = TB_RW

    def body(pm_ref, pmp_ref, pl_ref, plp_ref, mm_ref, ml_ref, w0_ref, a0_ref, wl_ref, al_ref,
             xm_ref, xl_ref, dec_ref, a_ref):
        first = pl.program_id(0) == 0
        xm_ref[...] = _token_shift_lerp(pm_ref[...], pmp_ref[...], mm_ref[...], first)
        xs_lo = _token_shift_lerp(pl_ref[...], plp_ref[...], ml_ref[...], first)
        xl_ref[...] = xs_lo
        dec_ref[...], a_ref[...] = _f_rwkv_lora(xs_lo, w0_ref[...], a0_ref[...], wl_ref[...], al_ref[...])

    return pl.pallas_call(
        body, name=name, grid=(t // tb,),
        in_specs=[_row_spec(RW_MAIN, tb), _prev_rows_spec(RW_MAIN, tb), _row_spec(RW_LO, tb), _prev_rows_spec(RW_LO, tb),
                  _vec_spec(RW_MAIN), _vec_spec(RW_LO), _vec_spec(D), _vec_spec(D),
                  _vec_spec(D, LORA_PAD), _vec_spec(D, LORA_PAD)],
        out_specs=[_row_spec(RW_MAIN, tb), _row_spec(RW_LO, tb), _row_spec(D, tb), _row_spec(D, tb)],
        out_shape=[jax.ShapeDtypeStruct((t, RW_MAIN), F32), jax.ShapeDtypeStruct((t, RW_LO), F32),
                   jax.ShapeDtypeStruct((t, D), F32), jax.ShapeDtypeStruct((t, D), F32)],
        compiler_params=_cparams("parallel"),
    )(p_main, p_main, p_lo, p_lo, mu_main, mu_lo, w0, a0, wl, al)


def _rwkv_lora_bwd(xs_lo, w0, a0, wl, al, ddecay, da, name):
    t = xs_lo.shape[0]
    tb = TB_NORM

    def body(x_ref, w0_ref, a0_ref, wl_ref, al_ref, dd_ref, da_ref, dx_ref, dw0_ref, da0_ref, dwl_ref, dal_ref):
        _, vjp = jax.vjp(_f_rwkv_lora, x_ref[...], w0_ref[...], a0_ref[...], wl_ref[...], al_ref[...])
        dx, dw0, da0, dwl, dal = vjp((dd_ref[...], da_ref[...]))
        dx_ref[...] = dx
        first = pl.program_id(0) == 0
        _accum(dw0_ref, dw0, first)
        _accum(da0_ref, da0, first)
        _accum(dwl_ref, dwl, first)
        _accum(dal_ref, dal, first)

    vec = jax.ShapeDtypeStruct((1, D), F32)
    lor = jax.ShapeDtypeStruct((LORA_PAD, D), F32)
    return pl.pallas_call(
        body, name=name, grid=(t // tb,),
        in_specs=[_row_spec(RW_LO), _vec_spec(D), _vec_spec(D), _vec_spec(D, LORA_PAD), _vec_spec(D, LORA_PAD),
                  _row_spec(D), _row_spec(D)],
        out_specs=[_row_spec(RW_LO), _vec_spec(D), _vec_spec(D), _vec_spec(D, LORA_PAD), _vec_spec(D, LORA_PAD)],
        out_shape=[jax.ShapeDtypeStruct((t, RW_LO), F32), vec, vec, lor, lor], compiler_params=_cparams("arbitrary"),
    )(xs_lo, w0, a0, wl, al, ddecay, da)


def _lerp_bwd(p, dxs, mu, name):
    t, width = p.shape
    tb = TB_RW
    nb = t // tb

    def body(p_ref, pp_ref, d_ref, dn_ref, mu_ref, dp_ref, dmu_ref):
        i = pl.program_id(0)
        pv, dv, mu_v = p_ref[...], d_ref[...], mu_ref[...]
        rows = lax.broadcasted_iota(jnp.int32, pv.shape, 0)
        prev = jnp.where(i == 0, 0.0, pp_ref[7:8, :])
        shifted = jnp.where(rows == 0, prev, pltpu.roll(pv, 1, 0))
        nxt = jnp.where(i == nb - 1, 0.0, dn_ref[0:1, :])
        d_next = jnp.where(rows == tb - 1, nxt, pltpu.roll(dv, tb - 1, 0))
        dp_ref[...] = (dv * (1.0 - mu_v) + d_next * mu_v).astype(BF16)
        _accum(dmu_ref, jnp.sum(dv * (shifted - pv), axis=0, keepdims=True), i == 0)

    return pl.pallas_call(
        body, name=name, grid=(nb,),
        in_specs=[_row_spec(width, tb), _prev_rows_spec(width, tb), _row_spec(width, tb),
                  pl.BlockSpec((8, width), lambda i: (jnp.minimum((i + 1) * (tb // 8), t // 8 - 1), 0)), _vec_spec(width)],
        out_specs=[_row_spec(width, tb), _vec_spec(width)],
        out_shape=[jax.ShapeDtypeStruct((t, width), BF16), jax.ShapeDtypeStruct((1, width), F32)],
        compiler_params=_cparams("arbitrary"),
    )(p, p, dxs, dxs, mu)


def _lane_group_sum2d(x):
    x = x + pltpu.roll(x, N_HEADS, 1)
    return x + pltpu.roll(x, 2 * N_HEADS, 1)


@jax.custom_vjp
def _lane_group_sum(x):
    return _lane_group_sum2d(x.reshape(-1, LANES)).reshape(x.shape)


_lane_group_sum.defvjp(lambda x: (_lane_group_sum(x), None), lambda _, ct: (_lane_group_sum(ct),))


def _head_sum(x):
    return _lane_group_sum(jnp.sum(x, axis=1, keepdims=True))


def _f_kprep(k, a, r, kkp, kap, rkp):
    kk = k * kkp
    kk = kk / jnp.maximum(jnp.sqrt(_head_sum(kk * kk)), 1e-12)
    k2 = k * (1.0 + (a - 1.0) * kap)
    return kk, k2, kk * a, _head_sum(r * k2 * rkp)


def _k_spec(rows=HEAD, tb=TB_K):
    return pl.BlockSpec((tb, rows, LANES), lambda i: (i, 0, 0))


def _kparam_spec(rows=HEAD):
    return pl.BlockSpec((rows, LANES), lambda i: (0, 0))


def _lane_group(shape):
    return lax.broadcasted_iota(jnp.int32, shape, len(shape) - 1) // N_HEADS


def _store_k_layout(ref, xc):
    x2 = xc.reshape(-1, LANES)
    group = _lane_group(x2.shape)
    for q in range(VM):
        one = jnp.where(group == q, x2, 0.0)
        ref[:, pl.ds(q, VD, stride=VM), :] = _lane_group_sum2d(one).reshape(xc.shape)


def _load_compact(ref):
    acc = None
    for q in range(VM):
        rows = _lane_group_sum(ref[:, pl.ds(q, VD, stride=VM), :])
        part = jnp.where(_lane_group(rows.shape) == q, rows, 0.0)
        acc = part if acc is None else acc + part
    return acc


def _rwkv_kprep_fwd(k, a, r, w, kkp, kap, rkp, name):
    t = k.shape[0]

    def body(k_ref, a_ref, r_ref, w_ref, kkp_ref, kap_ref, rkp_ref, kk_ref, k2_ref, b_ref, r4_ref, w4_ref, rk_ref):
        rv = r_ref[...]
        kk, k2, b, rk_ref[...] = _f_kprep(k_ref[...], a_ref[...], rv, kkp_ref[...], kap_ref[...], rkp_ref[...])
        for ref, val in ((kk_ref, kk), (k2_ref, k2), (b_ref, b), (r4_ref, rv), (w4_ref, w_ref[...])):
            _store_k_layout(ref, val)

    big = jax.ShapeDtypeStruct((t, HEAD, LANES), F32)
    return pl.pallas_call(
        body, name=name, grid=(t // TB_K,),
        in_specs=[_k_spec(VD)] * 4 + [_kparam_spec(VD)] * 3, out_specs=[_k_spec()] * 5 + [_k_spec(1)],
        out_shape=[big] * 5 + [jax.ShapeDtypeStruct((t, 1, LANES), F32)], compiler_params=_cparams("parallel"),
    )(k, a, r, w, kkp, kap, rkp)


def _rwkv_kprep_bwd(k, a, r, kkp, kap, rkp, dkk, dk2, db, drk, dr_scan, dw_scan, name):
    t = k.shape[0]

    def body(k_ref, a_ref, r_ref, kkp_ref, kap_ref, rkp_ref, dkk_ref, dk2_ref, db_ref, drk_ref, drs_ref, dws_ref,
             dk_ref, da_ref, dr_ref, dw_ref, dkkp_ref, dkap_ref, drkp_ref):
        _, vjp = jax.vjp(_f_kprep, k_ref[...], a_ref[...], r_ref[...], kkp_ref[...], kap_ref[...], rkp_ref[...])
        dk, da, dr, dkkp, dkap, drkp = vjp((_load_compact(dkk_ref), _load_compact(dk2_ref), _load_compact(db_ref),
                                            drk_ref[...]))
        dk_ref[...] = dk
        da_ref[...] = da
        dr_ref[...] = dr + _load_compact(drs_ref)
        dw_ref[...] = _load_compact(dws_ref)
        first = pl.program_id(0) == 0
        _accum(dkkp_ref, dkkp, first)
        _accum(dkap_ref, dkap, first)
        _accum(drkp_ref, drkp, first)

    cl = jax.ShapeDtypeStruct((t, VD, LANES), F32)
    par = jax.ShapeDtypeStruct((VD, LANES), F32)
    return pl.pallas_call(
        body, name=name, grid=(t // TB_K,),
        in_specs=[_k_spec(VD)] * 3 + [_kparam_spec(VD)] * 3 + [_k_spec()] * 3 + [_k_spec(1), _k_spec(), _k_spec()],
        out_specs=[_k_spec(VD)] * 4 + [_kparam_spec(VD)] * 3,
        out_shape=[cl] * 4 + [par] * 3, compiler_params=_cparams("arbitrary"),
    )(k, a, r, kkp, kap, rkp, dkk, dk2, db, drk, dr_scan, dw_scan)


def _f_post(y, v, rk, g, b):
    mean = _lane_group_sum(jnp.sum(y, axis=1, keepdims=True)) * (1.0 / HEAD)
    yc = y - mean
    var = _lane_group_sum(jnp.sum(yc * yc, axis=1, keepdims=True)) * (1.0 / HEAD)
    return yc * lax.rsqrt(var + GN_EPS) * g + b + rk * v


def _rwkv_post_fwd(y, v, rk, g, b, name):
    t = y.shape[0]

    def body(y_ref, v_ref, rk_ref, g_ref, b_ref, o_ref):
        o_ref[...] = _f_post(y_ref[...], v_ref[...], rk_ref[...], g_ref[...], b_ref[...])

    return pl.pallas_call(
        body, name=name, grid=(t // TB_K,),
        in_specs=[_k_spec(VD), _k_spec(VD), _k_spec(1), _kparam_spec(VD), _kparam_spec(VD)], out_specs=_k_spec(VD),
        out_shape=jax.ShapeDtypeStruct((t, VD, LANES), F32), compiler_params=_cparams("parallel"),
    )(y, v, rk, g, b)


def _rwkv_post_bwd(y, v, rk, g, b, do, name):
    t = y.shape[0]

    def body(y_ref, v_ref, rk_ref, g_ref, b_ref, do_ref, dy_ref, dv_ref, drk_ref, dg_ref, db_ref):
        _, vjp = jax.vjp(_f_post, y_ref[...], v_ref[...], rk_ref[...], g_ref[...], b_ref[...])
        dy, dv, drk, dg, db = vjp(do_ref[...])
        dy_ref[...] = dy
        dv_ref[...] = dv
        drk_ref[...] = drk
        first = pl.program_id(0) == 0
        _accum(dg_ref, dg, first)
        _accum(db_ref, db, first)

    vl = jax.ShapeDtypeStruct((t, VD, LANES), F32)
    par = jax.ShapeDtypeStruct((VD, LANES), F32)
    return pl.pallas_call(
        body, name=name, grid=(t // TB_K,),
        in_specs=[_k_spec(VD), _k_spec(VD), _k_spec(1), _kparam_spec(VD), _kparam_spec(VD), _k_spec(VD)],
        out_specs=[_k_spec(VD), _k_spec(VD), _k_spec(1), _kparam_spec(VD), _kparam_spec(VD)],
        out_shape=[vl, vl, jax.ShapeDtypeStruct((t, 1, LANES), F32), par, par], compiler_params=_cparams("arbitrary"),
    )(y, v, rk, g, b, do)


def _f_gate(o, z):
    return o * jax.nn.silu(z)


def _z_spec(tb=TB_NORM):
    return pl.BlockSpec((tb, D), lambda i: (i, 3))


def _rwkv_gate_fwd(o, xs_main, name):
    t = o.shape[0]

    def body(o_ref, z_ref, u_ref):
        u_ref[...] = _f_gate(o_ref[...], z_ref[...]).astype(BF16)

    return pl.pallas_call(
        body, name=name, grid=(t // TB_NORM,), in_specs=[_row_spec(D), _z_spec()], out_specs=_row_spec(D),
        out_shape=jax.ShapeDtypeStruct((t, D), BF16), compiler_params=_cparams("parallel"),
    )(o, xs_main)


def _rwkv_gate_bwd(o, xs_main, du, name):
    t = o.shape[0]

    def body(o_ref, z_ref, du_ref, do_ref, dz_ref):
        _, vjp = jax.vjp(_f_gate, o_ref[...], z_ref[...])
        do_ref[...], dz_ref[...] = vjp(du_ref[...])

    full = jax.ShapeDtypeStruct((t, D), F32)
    return pl.pallas_call(
        body, name=name, grid=(t // TB_NORM,), in_specs=[_row_spec(D), _z_spec(), _row_spec(D)],
        out_specs=[_row_spec(D), _row_spec(D)], out_shape=[full, full], compiler_params=_cparams("parallel"),
    )(o, xs_main, du)


def _colsum(x):
    return jnp.sum(x, axis=0, keepdims=True)


def _rwkv_scan_fwd(r4, w4, k24, kk4, b4, v, name, rider=None):
    t = r4.shape[0]
    tb = TB_SCAN

    def body(r_ref, w_ref, k2_ref, kk_ref, b_ref, v_ref, y_ref, sall_ref, sa_ref, s_scr):
        @pl.when(pl.program_id(0) == 0)
        def _():
            s_scr[...] = jnp.zeros_like(s_scr)

        sall_ref[0] = s_scr[...]

        def step(tt, dst):
            kk = kk_ref[tt]
            sas = []
            for vd in range(VD):
                sa = _colsum(sall_ref[tt, pl.ds(vd * HEAD, HEAD), :] * kk)
                sa_ref[tt, pl.ds(vd, 1), :] = sa
                sas.append(sa)
            w, b, k2, r = w_ref[tt], b_ref[tt], k2_ref[tt], r_ref[tt]
            for vd in range(VD):
                rows = pl.ds(vd * HEAD, HEAD)
                s = sall_ref[tt, rows, :] * w - sas[vd] * b + v_ref[tt, pl.ds(vd, 1), :] * k2
                dst[rows, :] = s
                y_ref[tt, pl.ds(vd, 1), :] = _colsum(s * r)

        def loop_step(tt, carry):
            step(tt, sall_ref.at[tt + 1])
            return carry

        lax.fori_loop(0, tb - 1, loop_step, 0)
        step(tb - 1, s_scr)

    vl = jax.ShapeDtypeStruct((t, VD, LANES), F32)
    return _compute_call(
        body, (r4, w4, k24, kk4, b4, v), name=name, grid=(t // tb,),
        in_specs=[_k_spec(HEAD, tb)] * 5 + [_k_spec(VD, tb)],
        out_specs=[_k_spec(VD, tb), _k_spec(S_ROWS, tb), _k_spec(VD, tb)],
        out_shape=[vl, jax.ShapeDtypeStruct((t, S_ROWS, LANES), F32), vl],
        scratch_shapes=[pltpu.VMEM((S_ROWS, LANES), F32)], semantics=("arbitrary",), rider=rider)


def _rwkv_scan_bwd(dy, s_all, sa_all, r4, w4, k24, kk4, b4, v, name, rider=None):
    t = r4.shape[0]
    tb = TB_SCAN
    nb = t // tb
    blk = lambda i: nb - 1 - i

    def body(dy_ref, sall_ref, sa_ref, r_ref, w_ref, k2_ref, kk_ref, b_ref, v_ref,
             dr_ref, dw_ref, dk2_ref, dkk_ref, db_ref, dv_ref, ds_scr):
        @pl.when(pl.program_id(0) == 0)
        def _():
            ds_scr[...] = jnp.zeros_like(ds_scr)

        def step(j, carry):
            tt = tb - 1 - j
            vrow = lambda ref, vd: ref[tt, pl.ds(vd, 1), :]
            srows = lambda vd: pl.ds(vd * HEAD, HEAD)
            r, k2, b = r_ref[tt], k2_ref[tt], b_ref[tt]
            dsas = []
            for vd in range(VD):
                ds = ds_scr[srows(vd), :] + vrow(dy_ref, vd) * r
                ds_scr[srows(vd), :] = ds
                dv_ref[tt, pl.ds(vd, 1), :] = _colsum(ds * k2)
                dsas.append(-_colsum(ds * b))
            zero = jnp.zeros((HEAD, LANES), F32)
            dk2, q, sady, vdy = zero, zero, 0.0, 0.0
            for vd in range(VD):
                dyv = vrow(dy_ref, vd)
                dk2 = dk2 + ds_scr[srows(vd), :] * vrow(v_ref, vd)
                q = q + sall_ref[tt, srows(vd), :] * dyv
                sady = sady + vrow(sa_ref, vd) * dyv
                vdy = vdy + vrow(v_ref, vd) * dyv
            dk2_ref[tt] = dk2
            dr_ref[tt] = w_ref[tt] * q - b_ref[tt] * sady + k2_ref[tt] * vdy
            dw, dkk = zero, zero
            for vd in range(VD):
                sp = sall_ref[tt, srows(vd), :]
                dw = dw + ds_scr[srows(vd), :] * sp
                dkk = dkk + sp * dsas[vd]
            dw_ref[tt] = dw
            dkk_ref[tt] = dkk
            w, kk = w_ref[tt], kk_ref[tt]
            db = zero
            for vd in range(VD):
                ds = ds_scr[srows(vd), :]
                db = db - ds * vrow(sa_ref, vd)
                ds_scr[srows(vd), :] = ds * w + dsas[vd] * kk
            db_ref[tt] = db
            return carry

        lax.fori_loop(0, tb, step, 0)

    rk = lambda rows: pl.BlockSpec((tb, rows, LANES), lambda i: (blk(i), 0, 0))
    big = jax.ShapeDtypeStruct((t, HEAD, LANES), F32)
    return _compute_call(
        body, (dy, s_all, sa_all, r4, w4, k24, kk4, b4, v), name=name, grid=(nb,),
        in_specs=[rk(VD), rk(S_ROWS), rk(VD)] + [rk(HEAD)] * 5 + [rk(VD)],
        out_specs=[rk(HEAD)] * 5 + [rk(VD)],
        out_shape=[big] * 5 + [jax.ShapeDtypeStruct((t, VD, LANES), F32)],
        scratch_shapes=[pltpu.VMEM((S_ROWS, LANES), F32)], semantics=("arbitrary",), rider=rider)


def _rwkv_mixer_fwd(p_main, p_lo, prm, tag, rider):
    xs_main, xs_lo, decay, a = _rwkv_pre_fwd(p_main, p_lo, prm["mu_main"], prm["mu_lo"], prm["w0"], prm["a0"],
                                             prm["wl"], prm["al"], tag + "_pre")
    r, k, v, w, a = (_compact(x) for x in (xs_main[:, :D], xs_main[:, D:2 * D], xs_main[:, 2 * D:3 * D], decay, a))
    kk4, k24, b4, r4, w4, rk = _rwkv_kprep_fwd(k, a, r, w, prm["kkp"], prm["kap"], prm["rkp"], tag + "_kprep")
    (y, s_all, sa_all), ridden = _ridden(_rwkv_scan_fwd(r4, w4, k24, kk4, b4, v, tag + "_scan", rider), rider)
    o = _rwkv_post_fwd(y, v, rk, prm["gn_g"], prm["gn_b"], tag + "_post").reshape(-1, D)
    u = _rwkv_gate_fwd(o, xs_main, tag + "_gate")
    saved = dict(xs_main=xs_main, xs_lo=xs_lo, r=r, k=k, a=a, v=v, r4=r4, w4=w4, kk4=kk4, k24=k24, b4=b4, rk=rk,
                 y=y, s_all=s_all, sa_all=sa_all, o=o)
    return u, saved, ridden


def _rwkv_mixer_bwd(p_main, p_lo, prm, sv, du, tag, rider):
    do, dz = _rwkv_gate_bwd(sv["o"], sv["xs_main"], du, tag + "_gate_b")
    dy, dv_post, drk, dgn_g, dgn_b = _rwkv_post_bwd(sv["y"], sv["v"], sv["rk"], prm["gn_g"], prm["gn_b"],
                                                    _compact(do), tag + "_post_b")
    (dr_s, dw_s, dk24, dkk4, db4, dv_scan), ridden = _ridden(_rwkv_scan_bwd(
        dy, sv["s_all"], sv["sa_all"], sv["r4"], sv["w4"], sv["k24"], sv["kk4"], sv["b4"], sv["v"], tag + "_scan_b", rider), rider)
    dk, da, dr, dw, dkkp, dkap, drkp = _rwkv_kprep_bwd(sv["k"], sv["a"], sv["r"], prm["kkp"], prm["kap"], prm["rkp"],
                                                       dkk4, dk24, db4, drk, dr_s, dw_s, tag + "_kprep_b")
    flat = lambda xc: xc.reshape(-1, D)
    dxs_lo, dw0, da0, dwl, dal = _rwkv_lora_bwd(sv["xs_lo"], prm["w0"], prm["a0"], prm["wl"], prm["al"],
                                                flat(dw), flat(da), tag + "_lora_b")
    dxs_main = jnp.concatenate([flat(dr), flat(dk), flat(dv_post + dv_scan), dz], axis=1)
    dp_main, dmu_main = _lerp_bwd(p_main, dxs_main, prm["mu_main"], tag + "_lerp_main_b")
    dp_lo, dmu_lo = _lerp_bwd(p_lo, dxs_lo, prm["mu_lo"], tag + "_lerp_lo_b")
    grads = dict(mu_main=dmu_main, mu_lo=dmu_lo, w0=dw0, a0=da0, wl=dwl, al=dal, kkp=dkkp, kap=dkap, rkp=drkp,
                 gn_g=dgn_g, gn_b=dgn_b)
    return dp_main, dp_lo, grads, ridden


N_DEV = 8
N_CHIPS = 4
ANY = pl.BlockSpec(memory_space=pl.ANY)


def _place():
    return lax.axis_index("x"), lax.axis_index("y"), lax.axis_index("c")


def _remote(src, dst, send_sems, recv_sems, k, dev):
    return pltpu.make_async_remote_copy(src_ref=src, dst_ref=dst, send_sem=send_sems.at[k], recv_sem=recv_sems.at[k],
                                        device_id=dev, device_id_type=MESHT)


def _all_gather8(v, name):
    def body(buf_ref, out_ref, send_sems, recv_sems):
        del buf_ref
        x, y, c = _place()
        mine = out_ref.at[4 * x + 2 * y + c]
        peers = [(x ^ (k >> 2), y ^ ((k >> 1) & 1), c ^ (k & 1)) for k in range(1, N_DEV)]
        sends = [_remote(mine, mine, send_sems, recv_sems, k, peer) for k, peer in enumerate(peers)]
        for cp in sends:
            cp.start()
        for k, (px, py, pc) in enumerate(peers):
            _remote(mine, out_ref.at[4 * px + 2 * py + pc], send_sems, recv_sems, k, (x, y, c)).wait_recv()
        for cp in sends:
            cp.wait_send()

    return pl.pallas_call(
        body, name=name, in_specs=[ANY], out_specs=ANY, input_output_aliases={0: 0},
        out_shape=jax.ShapeDtypeStruct((N_DEV,) + v.shape, v.dtype),
        scratch_shapes=[pltpu.SemaphoreType.DMA((N_DEV - 1,)), pltpu.SemaphoreType.DMA((N_DEV - 1,))],
    )(jnp.broadcast_to(v[None], (N_DEV,) + v.shape))


def _other_chips(x, y):
    return [(1 - x, y), (x, 1 - y), (1 - x, 1 - y)]


GATHER_SEMS = 6


def _gather_buffer(v):
    return jnp.broadcast_to(v[None], (N_CHIPS,) + v.shape)


def _gather_start(bufs, send_sems, recv_sems):
    x, y, c = _place()
    for i, buf in enumerate(bufs):
        mine = buf.at[2 * x + y, c]
        for j, (cx, cy) in enumerate(_other_chips(x, y)):
            _remote(mine, mine, send_sems, recv_sems, GATHER_SEMS * i + j, (cx, cy, c)).start()


def _gather_finish(bufs, send_sems, recv_sems):
    x, y, c = _place()
    chips = _other_chips(x, y)
    passed = []
    for i, buf in enumerate(bufs):
        mine = buf.at[2 * x + y, c]
        for j, (cx, cy) in enumerate(chips):
            landed = buf.at[2 * cx + cy, c]
            _remote(mine, landed, send_sems, recv_sems, GATHER_SEMS * i + j, (x, y, c)).wait_recv()
            fwd = _remote(landed, landed, send_sems, recv_sems, GATHER_SEMS * i + 3 + j, (x, y, 1 - c))
            fwd.start()
            passed.append(fwd)
    for i, buf in enumerate(bufs):
        mine = buf.at[2 * x + y, c]
        for j, (cx, cy) in enumerate(chips):
            _remote(mine, buf.at[2 * cx + cy, 1 - c], send_sems, recv_sems, GATHER_SEMS * i + 3 + j, (x, y, c)).wait_recv()
            _remote(mine, mine, send_sems, recv_sems, GATHER_SEMS * i + j, (cx, cy, c)).wait_send()
    for fwd in passed:
        fwd.wait_send()


def _gather_rider(bufs):
    return _Rider(bufs, GATHER_SEMS * len(bufs), _gather_start, _gather_finish)


def _chip_gather(bufs, name):
    n = len(bufs)

    def body(*refs):
        out_refs, (send_sems, recv_sems) = refs[n:2 * n], refs[2 * n:]
        _gather_start(out_refs, send_sems, recv_sems)
        _gather_finish(out_refs, send_sems, recv_sems)

    return pl.pallas_call(
        body, name=name, in_specs=[ANY] * n, out_specs=[ANY] * n, input_output_aliases={i: i for i in range(n)},
        out_shape=[jax.ShapeDtypeStruct(b.shape, b.dtype) for b in bufs], scratch_shapes=_dma_sems(GATHER_SEMS * n),
    )(*bufs)


RS_W = 1024
RS_BLOCK_BYTES = 4 << 20


def _dma_sems(n):
    return [pltpu.SemaphoreType.DMA((n,)), pltpu.SemaphoreType.DMA((n,))]


def _rs_pair_exchange(gs, name):
    n = len(gs)

    def body(*refs):
        g_refs, got_refs, (send_sems, recv_sems) = refs[:n], refs[n:2 * n], refs[2 * n:]
        x, y, c = _place()
        sends = [_remote(g_refs[i].at[s, 1 - c], got_refs[i].at[s], send_sems, recv_sems, N_CHIPS * i + s, (x, y, 1 - c))
                 for i in range(n) for s in range(N_CHIPS)]
        for cp in sends:
            cp.start()
        for cp in sends:
            cp.wait()

    return pl.pallas_call(
        body, name=name, in_specs=[ANY] * n, out_specs=[ANY] * n,
        out_shape=[jax.ShapeDtypeStruct((N_CHIPS,) + g.shape[2:], g.dtype) for g in gs],
        scratch_shapes=_dma_sems(N_CHIPS * n),
    )(*gs)


def _rs_rows(rows, cols):
    cap = max(16, RS_BLOCK_BYTES // (N_CHIPS * 4 * cols))
    return rows if rows <= cap else max(d for d in range(16, cap + 1, 16) if rows % d == 0)


def _rs_pair_add(g, got, c_arr, name):
    _, _, rows, width = g.shape
    tr = _rs_rows(rows, width)

    def body(c_ref, g_ref, got_ref, p_ref):
        p_ref[...] = (g_ref[...] + got_ref[...]).astype(BF16)

    return pl.pallas_call(
        body, name=name,
        grid_spec=pltpu.PrefetchScalarGridSpec(
            num_scalar_prefetch=1, grid=(rows // tr,),
            in_specs=[pl.BlockSpec((N_CHIPS, None, tr, width), lambda i, c_ref: (0, c_ref[0], i, 0)),
                      pl.BlockSpec((N_CHIPS, tr, width), lambda i, c_ref: (0, i, 0))],
            out_specs=pl.BlockSpec((N_CHIPS, tr, width), lambda i, c_ref: (0, i, 0))),
        out_shape=jax.ShapeDtypeStruct((N_CHIPS, rows, width), BF16), compiler_params=_cparams("parallel"),
    )(c_arr, g, got)


def _chip_exchange_copies(refs, send_sems, recv_sems):
    n = len(refs) // 2
    x, y, c = _place()
    return [_remote(refs[i].at[2 * cx + cy], refs[n + i].at[j], send_sems, recv_sems, 3 * i + j, (cx, cy, c))
            for i in range(n) for j, (cx, cy) in enumerate(_other_chips(x, y))]


def _chip_exchange_start(refs, send_sems, recv_sems):
    for cp in _chip_exchange_copies(refs, send_sems, recv_sems):
        cp.start()


def _chip_exchange_finish(refs, send_sems, recv_sems):
    n = len(refs) // 2
    x, y, c = _place()
    for i in range(n):
        for j in range(3):
            _remote(refs[i].at[2 * x + y], refs[n + i].at[j], send_sems, recv_sems, 3 * i + j, (x, y, c)).wait_recv()
    for cp in _chip_exchange_copies(refs, send_sems, recv_sems):
        cp.wait_send()


def _chip_exchange_buffers(ps):
    return [lax.empty((3,) + p.shape[1:], p.dtype) for p in ps]


def _chip_exchange_rider(ps):
    return _Rider(list(ps) + _chip_exchange_buffers(ps), 3 * len(ps), _chip_exchange_start, _chip_exchange_finish)


def _rs_chip_exchange(ps, name):
    n = len(ps)

    def body(*refs):
        out_refs, (send_sems, recv_sems) = refs[2 * n:4 * n], refs[4 * n:]
        _chip_exchange_start(out_refs, send_sems, recv_sems)
        _chip_exchange_finish(out_refs, send_sems, recv_sems)

    arrays = list(ps) + _chip_exchange_buffers(ps)
    return pl.pallas_call(
        body, name=name, in_specs=[ANY] * (2 * n), out_specs=[ANY] * (2 * n),
        input_output_aliases={i: i for i in range(2 * n)},
        out_shape=[jax.ShapeDtypeStruct(a.shape, a.dtype) for a in arrays], scratch_shapes=_dma_sems(3 * n),
    )(*arrays)[n:]


def _rs_chip_add(p, q, idx, name):
    _, rows, width = q.shape
    tr = _rs_rows(rows, width)

    def body(idx_ref, p_ref, q_ref, r_ref):
        qv = q_ref[...].astype(F32)
        r_ref[...] = ((p_ref[...].astype(F32) + qv[0]) + qv[1]) + qv[2]

    return pl.pallas_call(
        body, name=name,
        grid_spec=pltpu.PrefetchScalarGridSpec(
            num_scalar_prefetch=1, grid=(rows // tr,),
            in_specs=[pl.BlockSpec((None, tr, width), lambda i, idx_ref: (idx_ref[0], i, 0)),
                      pl.BlockSpec((3, tr, width), lambda i, idx_ref: (0, i, 0))],
            out_specs=pl.BlockSpec((None, tr, width), lambda i, idx_ref: (idx_ref[1], i, 0))),
        out_shape=jax.ShapeDtypeStruct((2, rows, width), F32), compiler_params=_cparams("parallel"),
    )(idx, p, q)


def _rs_pair_share(rs, name):
    n = len(rs)

    def body(*refs):
        out_refs, (send_sems, recv_sems) = refs[n:2 * n], refs[2 * n:]
        x, y, c = _place()
        sends = [_remote(out_refs[i].at[c], out_refs[i].at[c], send_sems, recv_sems, i, (x, y, 1 - c)) for i in range(n)]
        for cp in sends:
            cp.start()
        for i in range(n):
            _remote(out_refs[i].at[c], out_refs[i].at[1 - c], send_sems, recv_sems, i, (x, y, c)).wait_recv()
        for cp in sends:
            cp.wait_send()

    return pl.pallas_call(
        body, name=name, in_specs=[ANY] * n, out_specs=[ANY] * n, input_output_aliases={i: i for i in range(n)},
        out_shape=[jax.ShapeDtypeStruct(r.shape, r.dtype) for r in rs], scratch_shapes=_dma_sems(n),
    )(*rs)


def _rs_pair_sums(gs, core, tag):
    c_arr = core.astype(jnp.int32).reshape(1)
    gots = _rs_pair_exchange(gs, tag + "_pair_x")
    return [_rs_pair_add(g, got, c_arr, f"{tag}_pair_add{i}") for i, (g, got) in enumerate(zip(gs, gots))]


def _rs_finish(ps, qs, chip, core, tag):
    idx = jnp.stack([chip, core]).astype(jnp.int32)
    rs = [_rs_chip_add(p, q, idx, f"{tag}_chip_add{i}") for i, (p, q) in enumerate(zip(ps, qs))]
    return _rs_pair_share(rs, tag + "_share")


def _sum8(a, name):
    _, rows, width = a.shape
    tr = 8 * (rows // 8 if rows <= 64 else 7)
    assert rows % tr == 0

    def body(a_ref, o_ref):
        acc = a_ref[0]
        for d in range(1, N_DEV):
            acc = acc + a_ref[d]
        o_ref[...] = acc

    return pl.pallas_call(
        body, name=name, grid=(rows // tr,), in_specs=[pl.BlockSpec((N_DEV, tr, width), lambda i: (0, i, 0))],
        out_specs=pl.BlockSpec((tr, width), lambda i: (i, 0)), out_shape=jax.ShapeDtypeStruct((rows, width), F32),
        compiler_params=_cparams("parallel"),
    )(a)


MOD_COLS = 3 * D // N_CHIPS
MOD_TK = 512


def _mod_partial(c_all, mod_w, name):
    nk = D // MOD_TK

    def body(c_ref, w_ref, o_ref):
        l = pl.program_id(1)
        part = _bdot_nn(jax.nn.silu(c_ref[...]), w_ref[0])
        _accum(o_ref.at[0], part, l == 0)

    return pl.pallas_call(
        body, name=name, grid=(DEPTH, nk),
        in_specs=[pl.BlockSpec((N_DEV, MOD_TK), lambda i, l: (0, l)), pl.BlockSpec((1, MOD_TK, MOD_COLS), lambda i, l: (i, l, 0))],
        out_specs=pl.BlockSpec((1, N_DEV, MOD_COLS), lambda i, l: (i, 0, 0)),
        out_shape=jax.ShapeDtypeStruct((DEPTH, N_DEV, MOD_COLS), F32), compiler_params=_cparams("parallel", "arbitrary"),
    )(c_all, mod_w)


def _mod_w_grad(c_all, dmod, name):
    def body(c_ref, d_ref, o_ref):
        o_ref[0] = _dg(jax.nn.silu(c_ref[...]).astype(BF16), d_ref[0].astype(BF16), _TN)

    return pl.pallas_call(
        body, name=name, grid=(DEPTH, D // MOD_TK),
        in_specs=[pl.BlockSpec((N_DEV, MOD_TK), lambda i, l: (0, l)), pl.BlockSpec((1, N_DEV, MOD_COLS), lambda i, l: (i, 0, 0))],
        out_specs=pl.BlockSpec((1, MOD_TK, MOD_COLS), lambda i, l: (i, l, 0)),
        out_shape=jax.ShapeDtypeStruct((DEPTH, D, MOD_COLS), F32), compiler_params=_cparams("parallel", "parallel"),
    )(c_all, dmod)


ADAM_BLOCK_BYTES = 1 << 20


def _adamw(w, g, m, v, name):
    shape = w.shape
    cols = shape[-1]
    rows = w.size // cols
    w, g, m, v = (a.reshape(rows, cols) for a in (w, g, m, v))
    cap = max(8, ADAM_BLOCK_BYTES // (4 * cols))
    tr = rows if rows <= cap else max(d for d in range(8, cap + 1, 8) if rows % d == 0)
    c1 = 1.0 - ADAM_B1 ** ADAM_STEP
    c2 = 1.0 - ADAM_B2 ** ADAM_STEP

    def body(w_ref, g_ref, m_ref, v_ref, d_ref, nm_ref, nv_ref):
        gv = g_ref[...]
        mn = ADAM_B1 * m_ref[...] + (1.0 - ADAM_B1) * gv
        vn = ADAM_B2 * v_ref[...] + (1.0 - ADAM_B2) * (gv * gv)
        nm_ref[...] = mn
        nv_ref[...] = vn
        d_ref[...] = -ADAM_LR * ((mn / c1) / (jnp.sqrt(vn / c2) + ADAM_EPS) + ADAM_WD * w_ref[...])

    spec = pl.BlockSpec((tr, cols), lambda i: (i, 0))
    out = jax.ShapeDtypeStruct((rows, cols), F32)
    d, nm, nv = pl.pallas_call(
        body, name=name, grid=(rows // tr,), in_specs=[spec] * 4, out_specs=[spec] * 3, out_shape=[out] * 3,
        compiler_params=_cparams("parallel"),
    )(w, g, m, v)
    return d.reshape(shape), nm.reshape(shape), nv.reshape(shape)


W_NAMES = ("norm_g", "mod_w", "mod_b", "final_norm_g", "sg_w_in", "sg_w_out", "sg_ln_g", "sg_ln_b", "sg_w_spatial",
           "sg_b_spatial", "swa_w_in", "swa_w_out", "swa_sinks", "rwkv_w_in", "rwkv_w_out", "rwkv_mu", "rwkv_w0",
           "rwkv_w_lora", "rwkv_a0", "rwkv_a_lora", "rwkv_k_k", "rwkv_k_a", "rwkv_r_k", "rwkv_gn_g", "rwkv_gn_b")
SMALL = {"sg_ln_g": 1, "sg_ln_b": 1, "rwkv_mu": 1, "rwkv_w0": 1, "rwkv_w_lora": 2, "rwkv_a0": 1, "rwkv_a_lora": 2,
         "rwkv_k_k": 1, "rwkv_k_a": 1, "rwkv_gn_g": 1, "rwkv_gn_b": 1}
REPLICATED = ("norm_g", "final_norm_g", "sg_w_spatial", "sg_b_spatial", "swa_sinks", "rwkv_r_k")
KINDS = ("sg", "swa", "rwkv", "sg")


def _pad_to(flat, n):
    return jnp.pad(flat, (0, n - flat.shape[0]))


def _round_up(n, m):
    return -(-n // m) * m


def _join_shards(gathered, axis):
    return jnp.concatenate([gathered[s] for s in range(N_CHIPS)], axis=axis)


def _chip_blocks(full, axis):
    return jnp.stack(jnp.split(full, N_CHIPS, axis=axis)).reshape(N_CHIPS, -1)


def _weight_buffer(w):
    rows, cols = w.shape
    return _gather_buffer(w.astype(BF16).reshape(2, rows // 2, cols))


def _chip_shards(buf):
    return buf.reshape(N_CHIPS, -1, buf.shape[-1])


def _small_buffer(shards):
    flat = jnp.concatenate([shards[n].reshape(-1) for n in SMALL])
    rows = _round_up(flat.shape[0], 2 * 8 * LANES) // (2 * LANES)
    return _gather_buffer(_pad_to(flat, 2 * rows * LANES).reshape(2, rows, LANES))


def _unpack_small(buf, shards):
    got = buf.reshape(N_CHIPS, -1)
    out, off = {}, 0
    for n, axis in SMALL.items():
        size = shards[n].size
        out[n] = _join_shards(got[:, off:off + size].reshape((N_CHIPS,) + shards[n].shape), axis)
        off += size
    return out


def _lora_pad_rows(w):
    return jnp.pad(w, ((0, LORA_PAD - LORA), (0, 0)))


def _lo_cols(a):
    z = jnp.zeros(a.shape[:-1] + (LORA_PAD - LORA,), a.dtype)
    return jnp.concatenate([a[..., :LORA], z, a[..., LORA:], z], axis=-1)


def _lo_cols_inv(a):
    return jnp.concatenate([a[..., :LORA], a[..., LORA_PAD:LORA_PAD + LORA]], axis=-1)


def _rows_dim_major(w):
    return w.reshape(N_HEADS, HEAD, -1).swapaxes(0, 1).reshape(w.shape)


def _rows_head_major(w):
    return w.reshape(HEAD, N_HEADS, -1).swapaxes(0, 1).reshape(w.shape)


def kernel(x, c, positions, norm_g, mod_w, mod_b, final_norm_g, sg_w_in, sg_w_out, sg_ln_g, sg_ln_b, sg_w_spatial,
           sg_b_spatial, swa_w_in, swa_w_out, swa_sinks, rwkv_w_in, rwkv_w_out, rwkv_mu, rwkv_w0, rwkv_w_lora, rwkv_a0,
           rwkv_a_lora, rwkv_k_k, rwkv_k_a, rwkv_r_k, rwkv_gn_g, rwkv_gn_b, loss_target, m_norm_g, m_mod_w, m_mod_b,
           m_final_norm_g, m_sg_w_in, m_sg_w_out, m_sg_ln_g, m_sg_ln_b, m_sg_w_spatial, m_sg_b_spatial, m_swa_w_in,
           m_swa_w_out, m_swa_sinks, m_rwkv_w_in, m_rwkv_w_out, m_rwkv_mu, m_rwkv_w0, m_rwkv_w_lora, m_rwkv_a0,
           m_rwkv_a_lora, m_rwkv_k_k, m_rwkv_k_a, m_rwkv_r_k, m_rwkv_gn_g, m_rwkv_gn_b, v_norm_g, v_mod_w, v_mod_b,
           v_final_norm_g, v_sg_w_in, v_sg_w_out, v_sg_ln_g, v_sg_ln_b, v_sg_w_spatial, v_sg_b_spatial, v_swa_w_in,
           v_swa_w_out, v_swa_sinks, v_rwkv_w_in, v_rwkv_w_out, v_rwkv_mu, v_rwkv_w0, v_rwkv_w_lora, v_rwkv_a0,
           v_rwkv_a_lora, v_rwkv_k_k, v_rwkv_k_a, v_rwkv_r_k, v_rwkv_gn_g, v_rwkv_gn_b):
    given = dict(locals())
    w = {n: given[n] for n in W_NAMES}
    xi, yi, ci = _place()
    chip = 2 * xi + yi
    me = 4 * xi + 2 * yi + ci
    xs = [x[0]]

    c_all = _all_gather8(c, "gather_c")[:, 0, :]
    mod_part = _mod_partial(c_all, mod_w, "mod_fwd")
    mod_all = _all_gather8(mod_part, "gather_mod")[::2]
    mod_mine = lax.dynamic_index_in_dim(mod_all, me, axis=2, keepdims=False)
    mod = mod_mine.transpose(1, 0, 2).reshape(DEPTH, 3 * D) + mod_b
    shift, scale, gate = mod[:, :D], mod[:, D:2 * D], mod[:, 2 * D:]

    shards = {"sg_w_in0": sg_w_in[0], "sg_w_out0": sg_w_out[0], "swa_w_in": swa_w_in[0], "swa_w_out": swa_w_out[0],
              "rwkv_w_in": rwkv_w_in[0], "rwkv_w_out": rwkv_w_out[0], "sg_w_in1": sg_w_in[1], "sg_w_out1": sg_w_out[1]}
    bufs = {n: _weight_buffer(s) for n, s in shards.items()}
    fwd_riders = {(0, "in"): ["swa_w_in"], (0, "mix"): ["swa_w_out"], (1, "in"): ["rwkv_w_out"], (1, "mix"): ["rwkv_w_in"],
                  (2, "mix"): ["sg_w_in1", "sg_w_out1"]}
    bufs["sg_w_in0"], bufs["sg_w_out0"], small_buf = _chip_gather(
        [bufs["sg_w_in0"], bufs["sg_w_out0"], _small_buffer(w)], "gather_l0")
    full = _unpack_small(small_buf, w)

    def riding(i, where):
        names = fwd_riders.get((i, where))
        return names, (None if names is None else _gather_rider([bufs[n] for n in names]))

    def arrived(names, ridden):
        for n, b in zip(names or [], ridden):
            bufs[n] = b

    sg_in = lambda j: _chip_shards(bufs[f"sg_w_in{j}"])
    sg_out = lambda j: bufs[f"sg_w_out{j}"].reshape(D, D)
    mu = full["rwkv_mu"][0]
    rw_prm = dict(mu_main=_dim_major(mu[:RW_MAIN].reshape(4, D)).reshape(1, RW_MAIN), mu_lo=_lo_cols(mu[None, RW_MAIN:]),
                  w0=_dim_major(full["rwkv_w0"]), a0=_dim_major(full["rwkv_a0"]),
                  wl=_lora_pad_rows(_dim_major(full["rwkv_w_lora"][0])), al=_lora_pad_rows(_dim_major(full["rwkv_a_lora"][0])),
                  kkp=_param_compact(full["rwkv_k_k"][0]), kap=_param_compact(full["rwkv_k_a"][0]),
                  rkp=_param_compact(rwkv_r_k.reshape(-1)),
                  gn_g=_param_compact(full["rwkv_gn_g"][0]), gn_b=_param_compact(full["rwkv_gn_b"][0]))
    bs_t = [jnp.pad(sg_b_spatial[j].T, ((0, 0), (0, LANES - SG_GROUPS))) for j in range(2)]
    sink_row = jnp.pad(swa_sinks, ((0, 0), (0, LANES - N_HEADS)))
    inv_freq = ROPE_THETA ** (-jnp.arange(HEAD // 2, dtype=F32) / (HEAD // 2))
    ang = positions[0].astype(F32)[:, None] * inv_freq
    cos, sin = jnp.tile(jnp.cos(ang), (1, LANES * 2 // HEAD)), jnp.tile(jnp.sin(ang), (1, LANES * 2 // HEAD))

    def row(a, i):
        return a[i:i + 1]

    hs, ps, us, ys, rw_saved = [], [], [], [], None
    for i, kind in enumerate(KINDS):
        j = i // 3
        tag = f"l{i}_{kind}"
        h = _norm_mod_fwd(xs[i], row(norm_g, i), row(shift, i), row(scale, i), tag + "_norm")
        names_in, rider_in = riding(i, "in")
        names_mix, rider_mix = riding(i, "mix")
        if kind == "sg":
            p, ridden = _ridden(_matmul(h, sg_in(j), "nn", tag + "_in", blocked=True, rider=rider_in), rider_in)
            arrived(names_in, ridden)
            u, ridden = _ridden(_sg_fwd(p, row(full["sg_ln_g"], j), row(full["sg_ln_b"], j), sg_w_spatial[j], bs_t[j],
                                        tag + "_mix", rider_mix), rider_mix)
            w_out = sg_out(j)
        elif kind == "swa":
            swa_in, swa_out = _chip_shards(bufs["swa_w_in"]), bufs["swa_w_out"].reshape(D, D)
            p, ridden = _ridden(_matmul(h, swa_in, "nn", tag + "_in", blocked=True, rider=rider_in), rider_in)
            arrived(names_in, ridden)
            u, ridden = _ridden(_swa_fwd(p, cos, sin, sink_row, tag + "_mix", rider_mix), rider_mix)
            w_out = swa_out
        else:
            rw_in = _join_shards(_chip_shards(bufs["rwkv_w_in"]), axis=1)
            rw_main = _dim_major(rw_in[:, :RW_MAIN].reshape(D, 4, D)).reshape(D, RW_MAIN)
            rw_lo = _lo_cols(rw_in[:, RW_MAIN:])
            rw_out = _rows_dim_major(bufs["rwkv_w_out"].reshape(D, D))
            p = (_matmul(h, rw_main, "nn", tag + "_in"), _matmul(h, rw_lo, "nn", tag + "_in_lo"))
            u, rw_saved, ridden = _rwkv_mixer_fwd(p[0], p[1], rw_prm, tag, rider_mix)
            w_out = rw_out
        arrived(names_mix, ridden)
        y = _matmul(u, w_out, "nn", tag + "_out")
        xs.append(_resid_gate(xs[i], y, row(gate, i), tag + "_resid"))
        hs.append(h), ps.append(p), us.append(u), ys.append(y)

    loss_part, dx, d_final_g = _final_loss_grad(xs[DEPTH], final_norm_g[None], loss_target[0], "loss")
    loss = lax.psum(loss_part[0, 0], ("x", "y", "c"))

    gfull = {n: [None, None] for n in ("sg_ln_g", "sg_ln_b", "sg_w_spatial", "sg_b_spatial")}
    gbig = {}
    d_norm_g, d_mod = [None] * DEPTH, [None] * DEPTH
    rs_p, rs_q, riding_names = {}, {}, []

    def rs_begin(names, tag):
        gs = [gbig[n].reshape(N_CHIPS, 2, gbig[n].shape[1] // 2, gbig[n].shape[2]) for n in names]
        rs_p.update(zip(names, _rs_pair_sums(gs, ci, tag)))

    for i in reversed(range(DEPTH)):
        kind, j = KINDS[i], i // 3
        tag = f"l{i}_{kind}_b"
        rider = _chip_exchange_rider([rs_p[n] for n in riding_names]) if riding_names else None
        dy, d_gate = _gate_bwd(dx, ys[i], row(gate, i), tag + "_gate")
        w_out = {"sg": sg_out(j), "swa": swa_out, "rwkv": rw_out}[kind]
        du = _matmul(dy, w_out, "nt", tag + "_du")
        dw_out = _matmul(us[i], dy, "tn", tag + "_dwout").reshape(N_CHIPS, D // N_CHIPS, D)
        if kind == "sg":
            (dp, dlg, dlb, dws, dbs), ridden = _ridden(
                _sg_bwd(ps[i], row(full["sg_ln_g"], j), row(full["sg_ln_b"], j), sg_w_spatial[j], bs_t[j], du,
                        tag + "_mix", rider), rider)
            gfull["sg_ln_g"][j], gfull["sg_ln_b"][j] = dlg[0], dlb[0]
            gfull["sg_w_spatial"][j], gfull["sg_b_spatial"][j] = dws, dbs[:, :SG_GROUPS].T
            gbig[f"sg_w_in{j}"] = _matmul(hs[i], dp, "tn", tag + "_dwin", blocked=True)
            gbig[f"sg_w_out{j}"] = dw_out
            dh, dh2 = _matmul(dp, sg_in(j), "nt", tag + "_dh", blocked=True), None
            mine = [f"sg_w_in{j}", f"sg_w_out{j}"]
        elif kind == "swa":
            (dp, dsk), ridden = _ridden(_swa_bwd(ps[i], cos, sin, sink_row, du, tag + "_mix", rider), rider)
            gfull["swa_sinks"] = dsk[:, :N_HEADS]
            gbig["swa_w_in"] = _matmul(hs[i], dp, "tn", tag + "_dwin", blocked=True)
            gbig["swa_w_out"] = dw_out
            dh, dh2 = _matmul(dp, swa_in, "nt", tag + "_dh", blocked=True), None
            mine = ["swa_w_in", "swa_w_out"]
        else:
            dpm, dpl, rg, ridden = _rwkv_mixer_bwd(ps[i][0], ps[i][1], rw_prm, rw_saved, du, tag, rider)
            mine = ["rwkv_w_in", "rwkv_w_out"]
            dw_main = _matmul(hs[i], dpm, "tn", tag + "_dwin")
            dw_lo = _matmul(hs[i], dpl, "tn", tag + "_dwin_lo")
            dw_main = _head_major(dw_main.reshape(D, 4, D)).reshape(D, RW_MAIN)
            dw_in = jnp.concatenate([dw_main, _lo_cols_inv(dw_lo)], axis=1)
            gbig["rwkv_w_in"] = dw_in.reshape(D, N_CHIPS, -1).transpose(1, 0, 2)
            gbig["rwkv_w_out"] = _rows_head_major(dw_out.reshape(D, D)).reshape(dw_out.shape)
            dmu_main = _head_major(rg["mu_main"].reshape(4, D)).reshape(1, RW_MAIN)
            gfull["rwkv_mu"] = jnp.concatenate([dmu_main, _lo_cols_inv(rg["mu_lo"])], axis=1)
            gfull["rwkv_w0"], gfull["rwkv_a0"] = _head_major(rg["w0"]), _head_major(rg["a0"])
            gfull["rwkv_w_lora"], gfull["rwkv_a_lora"] = _head_major(rg["wl"])[None, :LORA], _head_major(rg["al"])[None, :LORA]
            gfull["rwkv_k_k"], gfull["rwkv_k_a"] = _param_compact_inv(rg["kkp"])[None], _param_compact_inv(rg["kap"])[None]
            gfull["rwkv_r_k"] = _param_compact_inv(rg["rkp"]).reshape(1, N_HEADS, HEAD)
            gfull["rwkv_gn_g"], gfull["rwkv_gn_b"] = _param_compact_inv(rg["gn_g"])[None], _param_compact_inv(rg["gn_b"])[None]
            dh, dh2 = _matmul(dpm, rw_main, "nt", tag + "_dh"), _matmul(dpl, rw_lo, "nt", tag + "_dh_lo")
        dx, dg, dsh, dsc = _norm_mod_bwd(xs[i], row(norm_g, i), row(shift, i), row(scale, i), dh, dx, tag + "_norm", dh2)
        d_norm_g[i] = dg[0]
        d_mod[i] = jnp.concatenate([dsh[0], dsc[0], d_gate[0]])
        rs_q.update(zip(riding_names, ridden[len(riding_names):]))
        riding_names = mine
        if i > 0:
            rs_begin(mine, f"rs{i}")
    for n in ("sg_ln_g", "sg_ln_b", "sg_w_spatial", "sg_b_spatial"):
        gfull[n] = jnp.stack(gfull[n])
    gfull["norm_g"], gfull["final_norm_g"] = jnp.stack(d_norm_g), d_final_g[0]

    small = jnp.concatenate([_chip_blocks(gfull[n], axis) for n, axis in SMALL.items()], axis=1)
    small_rows = _round_up(small.shape[1], 2 * 16 * LANES) // LANES
    gbig["small"] = jnp.pad(small, ((0, 0), (0, small_rows * LANES - small.shape[1]))).reshape(N_CHIPS, small_rows, LANES)
    last = riding_names + ["small"]
    rs_begin(last, "rs0")
    rs_q.update(zip(last, _rs_chip_exchange([rs_p[n] for n in last], "rs0_chip_x")))
    rs_names = sorted(rs_p)
    rs_out = _rs_finish([rs_p[n] for n in rs_names], [rs_q[n] for n in rs_names], chip, ci, "rs")
    red = {n: r.reshape(-1, r.shape[2]) for n, r in zip(rs_names, rs_out)}
    grads = {"sg_w_in": jnp.stack([red["sg_w_in0"], red["sg_w_in1"]]), "sg_w_out": jnp.stack([red["sg_w_out0"], red["sg_w_out1"]])}
    for n in ("swa_w_in", "swa_w_out", "rwkv_w_in", "rwkv_w_out"):
        grads[n] = red[n][None]
    small_red, off = red["small"].reshape(-1), 0
    for n in SMALL:
        grads[n] = small_red[off:off + w[n].size].reshape(w[n].shape)
        off += w[n].size

    rep_flat = jnp.concatenate([jnp.stack(d_mod).reshape(-1)] + [gfull[n].reshape(-1) for n in REPLICATED])
    rep_rows = _round_up(rep_flat.shape[0], 56 * RS_W) // RS_W
    rep_all = _all_gather8(_pad_to(rep_flat, rep_rows * RS_W).reshape(rep_rows, RS_W), "gather_rep")
    rep_sum = _sum8(rep_all, "sum_rep").reshape(-1)
    grads["mod_b"] = rep_sum[:DEPTH * 3 * D].reshape(DEPTH, 3 * D)
    off = DEPTH * 3 * D
    for n in REPLICATED:
        grads[n] = rep_sum[off:off + w[n].size].reshape(w[n].shape)
        off += w[n].size
    dmod_all = rep_all.reshape(N_DEV, -1)[:, :DEPTH * 3 * D].reshape(N_DEV, DEPTH, 3 * D)
    dmod_cols = lax.dynamic_slice_in_dim(dmod_all, chip * MOD_COLS, MOD_COLS, axis=2).transpose(1, 0, 2)
    grads["mod_w"] = _mod_w_grad(c_all, dmod_cols, "mod_w_grad")

    deltas, new_m, new_v = {}, {}, {}
    for n in W_NAMES:
        deltas[n], new_m[n], new_v[n] = _adamw(w[n], grads[n], given["m_" + n], given["v_" + n], "adamw_" + n)
    return (loss, dx[None], *[grads[n] for n in W_NAMES], *[deltas[n] for n in W_NAMES],
            *[new_m[n] for n in W_NAMES], *[new_v[n] for n in W_NAMES])
```

```python
import functools
import math

import jax
import jax.numpy as jnp
from jax import lax
from jax.experimental import pallas as pl
from jax.experimental.pallas import tpu as pltpu

F32 = jnp.float32
BF16 = jnp.bfloat16
HIGHEST = lax.Precision.HIGHEST

D = 2048
DEPTH = 4
CHUNK = 128
SG_GROUPS = 16
HEAD = 64
N_HEADS = D // HEAD
KV_HEADS = 4
KVW = KV_HEADS * HEAD
ROPE_THETA = 10000.0
LORA = 96
LORA_PAD = 128
DECAY_SCALE = math.exp(-0.5)
GN_EPS = 64e-5
RMS_EPS = 1e-6
LN_EPS = 1e-5
ADAM_LR, ADAM_B1, ADAM_B2, ADAM_EPS, ADAM_WD, ADAM_STEP = 0.001, 0.9, 0.999, 1e-08, 0.01, 10
LANES = 128
SUB = 8
NEG = -1e30
VMEM_LIMIT = 56 * 1024 * 1024

MESHT = pl.DeviceIdType.MESH


def _cparams(*sem):
    return pltpu.CompilerParams(dimension_semantics=sem, vmem_limit_bytes=VMEM_LIMIT)


class _Rider:
    def __init__(self, arrays, n_sems, start, finish):
        self.arrays, self.n_sems, self.start, self.finish = list(arrays), n_sems, start, finish


def _ridden(res, rider):
    return (res, []) if rider is None else res


def _compute_call(body, args, *, name, grid, in_specs, out_specs, out_shape, semantics, scratch_shapes=(), rider=None):
    if rider is None:
        return pl.pallas_call(body, name=name, grid=grid, in_specs=in_specs, out_specs=out_specs, out_shape=out_shape,
                              scratch_shapes=list(scratch_shapes), compiler_params=_cparams(*semantics))(*args)
    single = not isinstance(out_shape, (list, tuple))
    o_specs, o_shapes = ([out_specs], [out_shape]) if single else (list(out_specs), list(out_shape))
    n_in, n_out, n_r, n_scr = len(in_specs), len(o_specs), len(rider.arrays), len(scratch_shapes)

    def with_rider(*refs):
        ins, outs = refs[:n_in], refs[n_in + n_r:n_in + n_r + n_out]
        ridden = refs[n_in + n_r + n_out:n_in + 2 * n_r + n_out]
        scratch, (send_sems, recv_sems) = refs[n_in + 2 * n_r + n_out:-2], refs[-2:]
        ids = [pl.program_id(d) for d in range(len(grid))]
        first = functools.reduce(jnp.logical_and, [i == 0 for i in ids])
        last = functools.reduce(jnp.logical_and, [i == g - 1 for i, g in zip(ids, grid)])

        @pl.when(first)
        def _():
            rider.start(ridden, send_sems, recv_sems)

        body(*ins, *outs, *scratch)

        @pl.when(last)
        def _():
            rider.finish(ridden, send_sems, recv_sems)

    any_spec = pl.BlockSpec(memory_space=pl.ANY)
    res = pl.pallas_call(
        with_rider, name=name, grid=grid, in_specs=list(in_specs) + [any_spec] * n_r, out_specs=o_specs + [any_spec] * n_r,
        out_shape=o_shapes + [jax.ShapeDtypeStruct(a.shape, a.dtype) for a in rider.arrays],
        input_output_aliases={n_in + i: n_out + i for i in range(n_r)},
        scratch_shapes=list(scratch_shapes) + [pltpu.SemaphoreType.DMA((rider.n_sems,))] * 2,
        compiler_params=_cparams(*["arbitrary"] * len(grid)),
    )(*args, *rider.arrays)
    return (res[0] if single else list(res[:n_out])), list(res[n_out:])


_NN = (((1,), (0,)), ((), ()))
_NT = (((1,), (1,)), ((), ()))
_TN = (((0,), (0,)), ((), ()))


def _dg(a, b, dims):
    return lax.dot_general(a, b, dims, preferred_element_type=F32)


@jax.custom_vjp
def _bdot_nn(a, b):
    return _dg(a.astype(BF16), b.astype(BF16), _NN)


def _bdot_nn_fwd(a, b):
    a, b = a.astype(BF16), b.astype(BF16)
    return _dg(a, b, _NN), (a, b)


def _bdot_nn_bwd(res, ct):
    a, b = res
    ct = ct.astype(BF16)
    return _dg(ct, b, _NT), _dg(a, ct, _TN)


_bdot_nn.defvjp(_bdot_nn_fwd, _bdot_nn_bwd)


@jax.custom_vjp
def _bdot_nt(a, b):
    return _dg(a.astype(BF16), b.astype(BF16), _NT)


def _bdot_nt_fwd(a, b):
    a, b = a.astype(BF16), b.astype(BF16)
    return _dg(a, b, _NT), (a, b)


def _bdot_nt_bwd(res, ct):
    a, b = res
    ct = ct.astype(BF16)
    return _dg(ct, b, _NN), _dg(ct, a, _TN)


_bdot_nt.defvjp(_bdot_nt_fwd, _bdot_nt_bwd)


def _tile(n, cap):
    if n <= cap:
        return n
    return max(d for d in range(LANES, cap + 1, LANES) if n % d == 0)


def _matmul(a, b, form, name, out_dtype=F32, blocked=False, rider=None, tm=1024, tn=512, tk=4096):
    if form == "nn":
        (m, k), n = a.shape, (N_CHIPS * b.shape[2] if blocked else b.shape[1])
    elif form == "nt":
        m, k, n = a.shape[0], a.shape[1], (b.shape[1] if blocked else b.shape[0])
    else:
        (k, m), n = a.shape, b.shape[1]
    per_chip = (k if form == "nt" else n) // N_CHIPS
    if blocked and form == "nt":
        tk = _tile(per_chip, tk)
    elif blocked:
        tn = _tile(per_chip, tn)
    tm, tn, tk = _tile(m, tm), _tile(n, tn), _tile(k, tk)
    assert m % tm == 0 and n % tn == 0 and k % tk == 0, (name, a.shape, b.shape)
    nk = k // tk
    dims = {"nn": _NN, "nt": _NT, "tn": _TN}[form]
    a_spec = pl.BlockSpec((tk, tm), lambda i, j, l: (l, i)) if form == "tn" else pl.BlockSpec((tm, tk), lambda i, j, l: (i, l))
    b_spec = pl.BlockSpec((tn, tk), lambda i, j, l: (j, l)) if form == "nt" else pl.BlockSpec((tk, tn), lambda i, j, l: (l, j))
    o_spec = pl.BlockSpec((tm, tn), lambda i, j, l: (i, j))
    o_shape = (m, n)
    if blocked and form == "nn":
        pc = per_chip // tn
        b_spec = pl.BlockSpec((None, tk, tn), lambda i, j, l: (j // pc, l, j % pc))
    elif blocked and form == "nt":
        pc = per_chip // tk
        b_spec = pl.BlockSpec((None, tn, tk), lambda i, j, l: (l // pc, j, l % pc))
    elif blocked:
        pc = per_chip // tn
        o_spec = pl.BlockSpec((None, tm, tn), lambda i, j, l: (j // pc, i, j % pc))
        o_shape = (N_CHIPS, m, per_chip)

    def body(a_ref, b_ref, o_ref, acc_ref):
        part = _dg(a_ref[...], b_ref[...], dims)
        if nk == 1:
            o_ref[...] = part.astype(out_dtype)
        else:
            l = pl.program_id(2)

            @pl.when(l == 0)
            def _():
                acc_ref[...] = part

            @pl.when(l > 0)
            def _():
                acc_ref[...] += part

            @pl.when(l == nk - 1)
            def _():
                o_ref[...] = acc_ref[...].astype(out_dtype)

    return _compute_call(
        body, (a, b), name=name, grid=(m // tm, n // tn, nk),
        in_specs=[a_spec, b_spec], out_specs=o_spec, out_shape=jax.ShapeDtypeStruct(o_shape, out_dtype),
        scratch_shapes=[pltpu.VMEM((tm, tn) if nk > 1 else (8, LANES), F32)],
        semantics=("parallel", "parallel", "arbitrary"), rider=rider)


TB_NORM = 256


def _f_norm_mod(x, g, shift, scale):
    xn = x * lax.rsqrt(jnp.mean(x * x, axis=-1, keepdims=True) + RMS_EPS)
    return (xn * g) * (1.0 + scale) + shift


def _row_spec(width, tb=TB_NORM):
    return pl.BlockSpec((tb, width), lambda i: (i, 0))


def _vec_spec(width, rows=1):
    return pl.BlockSpec((rows, width), lambda i: (0, 0))


def _norm_mod_fwd(x, g, shift, scale, name):
    t = x.shape[0]

    def body(x_ref, g_ref, sh_ref, sc_ref, h_ref):
        h_ref[...] = _f_norm_mod(x_ref[...], g_ref[...], sh_ref[...], sc_ref[...]).astype(BF16)

    return pl.pallas_call(
        body, name=name, grid=(t // TB_NORM,),
        in_specs=[_row_spec(D), _vec_spec(D), _vec_spec(D), _vec_spec(D)], out_specs=_row_spec(D),
        out_shape=jax.ShapeDtypeStruct((t, D), BF16), compiler_params=_cparams("parallel"),
    )(x, g, shift, scale)


def _accum(ref, val, first):
    @pl.when(first)
    def _():
        ref[...] = val

    @pl.when(jnp.logical_not(first))
    def _():
        ref[...] += val


def _norm_mod_bwd(x, g, shift, scale, dh, dx_res, name, dh2=None, rider=None):
    t = x.shape[0]
    dhs = [dh] if dh2 is None else [dh, dh2]

    def body(x_ref, g_ref, sh_ref, sc_ref, dr_ref, *refs):
        dh_refs, (dx_ref, dg_ref, dsh_ref, dsc_ref) = refs[:len(dhs)], refs[len(dhs):]
        _, vjp = jax.vjp(_f_norm_mod, x_ref[...], g_ref[...], sh_ref[...], sc_ref[...])
        dh_all = dh_refs[0][...]
        for r in dh_refs[1:]:
            dh_all = dh_all + r[...]
        dx, dg, dsh, dsc = vjp(dh_all)
        dx_ref[...] = dx + dr_ref[...]
        first = pl.program_id(0) == 0
        _accum(dg_ref, dg, first)
        _accum(dsh_ref, dsh, first)
        _accum(dsc_ref, dsc, first)

    vec = jax.ShapeDtypeStruct((1, D), F32)
    return _compute_call(
        body, (x, g, shift, scale, dx_res, *dhs), name=name, grid=(t // TB_NORM,),
        in_specs=[_row_spec(D), _vec_spec(D), _vec_spec(D), _vec_spec(D), _row_spec(D)] + [_row_spec(D)] * len(dhs),
        out_specs=[_row_spec(D), _vec_spec(D), _vec_spec(D), _vec_spec(D)],
        out_shape=[jax.ShapeDtypeStruct((t, D), F32), vec, vec, vec], semantics=("arbitrary",), rider=rider)


def _resid_gate(x, y, gate, name):
    t = x.shape[0]

    def body(x_ref, y_ref, g_ref, o_ref):
        o_ref[...] = x_ref[...] + g_ref[...] * y_ref[...]

    return pl.pallas_call(
        body, name=name, grid=(t // TB_NORM,),
        in_specs=[_row_spec(D), _row_spec(D), _vec_spec(D)], out_specs=_row_spec(D),
        out_shape=jax.ShapeDtypeStruct((t, D), F32), compiler_params=_cparams("parallel"),
    )(x, y, gate)


def _gate_bwd(dx, y, gate, name):
    t = dx.shape[0]

    def body(dx_ref, y_ref, g_ref, dy_ref, dg_ref):
        dxv = dx_ref[...]
        dy_ref[...] = (dxv * g_ref[...]).astype(BF16)
        _accum(dg_ref, jnp.sum(dxv * y_ref[...], axis=0, keepdims=True), pl.program_id(0) == 0)

    return pl.pallas_call(
        body, name=name, grid=(t // TB_NORM,),
        in_specs=[_row_spec(D), _row_spec(D), _vec_spec(D)], out_specs=[_row_spec(D), _vec_spec(D)],
        out_shape=[jax.ShapeDtypeStruct((t, D), BF16), jax.ShapeDtypeStruct((1, D), F32)],
        compiler_params=_cparams("arbitrary"),
    )(dx, y, gate)


def _f_final(x, g, target):
    xn = x * lax.rsqrt(jnp.mean(x * x, axis=-1, keepdims=True) + RMS_EPS)
    err = xn * g - target
    return 0.5 * jnp.sum(jnp.mean(err * err, axis=-1, keepdims=True), axis=0, keepdims=True)


def _final_loss_grad(x, g, target, name):
    t = x.shape[0]

    def body(x_ref, g_ref, t_ref, loss_ref, dx_ref, dg_ref):
        loss, vjp = jax.vjp(_f_final, x_ref[...], g_ref[...], t_ref[...])
        dx, dg, _ = vjp(jnp.ones((1, 1), F32))
        dx_ref[...] = dx
        first = pl.program_id(0) == 0
        _accum(dg_ref, dg, first)
        _accum(loss_ref, jnp.broadcast_to(loss, (1, LANES)), first)

    return pl.pallas_call(
        body, name=name, grid=(t // TB_NORM,),
        in_specs=[_row_spec(D), _vec_spec(D), _row_spec(D)],
        out_specs=[_vec_spec(LANES), _row_spec(D), _vec_spec(D)],
        out_shape=[jax.ShapeDtypeStruct((1, LANES), F32), jax.ShapeDtypeStruct((t, D), F32), jax.ShapeDtypeStruct((1, D), F32)],
        compiler_params=_cparams("arbitrary"),
    )(x, g, target)


def _group_selector():
    gi = lax.broadcasted_iota(jnp.int32, (LANES, D), 0)
    ci = lax.broadcasted_iota(jnp.int32, (LANES, D), 1)
    return (ci // (D // SG_GROUPS) == gi).astype(F32)


def _f_sg(p, ln_g, ln_b, w_s, bs_t):
    u, v, z = p[:, :D], p[:, D:2 * D], p[:, 2 * D:]
    u = jax.nn.gelu(u)
    vf = jax.nn.gelu(v)
    mean = jnp.mean(vf, axis=-1, keepdims=True)
    var = jnp.mean(jnp.square(vf - mean), axis=-1, keepdims=True)
    vn = (vf - mean) * lax.rsqrt(var + LN_EPS) * ln_g + ln_b
    ti = lax.broadcasted_iota(jnp.int32, (CHUNK, CHUNK), 0)
    si = lax.broadcasted_iota(jnp.int32, (CHUNK, CHUNK), 1)
    causal = si <= ti
    cg = D // SG_GROUPS
    f = jnp.concatenate(
        [_bdot_nn(jnp.where(causal, w_s[g], 0.0), vn[:, g * cg:(g + 1) * cg]) for g in range(SG_GROUPS)], axis=1)
    f = f + jnp.dot(bs_t, _group_selector(), precision=HIGHEST, preferred_element_type=F32)
    return u * f * jax.nn.silu(z)


def _sg_specs():
    return [pl.BlockSpec((CHUNK, 3 * D), lambda i: (i, 0)), _vec_spec(D), _vec_spec(D),
            pl.BlockSpec((SG_GROUPS, CHUNK, CHUNK), lambda i: (0, 0, 0)), _vec_spec(LANES, CHUNK)]


def _sg_fwd(p, ln_g, ln_b, w_s, bs_t, name, rider=None):
    t = p.shape[0]

    def body(p_ref, lg_ref, lb_ref, w_ref, b_ref, o_ref):
        o_ref[...] = _f_sg(p_ref[...], lg_ref[...], lb_ref[...], w_ref[...], b_ref[...]).astype(BF16)

    return _compute_call(
        body, (p, ln_g, ln_b, w_s, bs_t), name=name, grid=(t // CHUNK,), in_specs=_sg_specs(),
        out_specs=_row_spec(D, CHUNK), out_shape=jax.ShapeDtypeStruct((t, D), BF16), semantics=("parallel",), rider=rider)


def _sg_bwd(p, ln_g, ln_b, w_s, bs_t, dout, name, rider=None):
    t = p.shape[0]

    def body(p_ref, lg_ref, lb_ref, w_ref, b_ref, do_ref, dp_ref, dlg_ref, dlb_ref, dw_ref, db_ref):
        _, vjp = jax.vjp(_f_sg, p_ref[...], lg_ref[...], lb_ref[...], w_ref[...], b_ref[...])
        dp, dlg, dlb, dw, db = vjp(do_ref[...])
        dp_ref[...] = dp.astype(BF16)
        first = pl.program_id(0) == 0
        _accum(dlg_ref, dlg, first)
        _accum(dlb_ref, dlb, first)
        _accum(dw_ref, dw, first)
        _accum(db_ref, db, first)

    vec = jax.ShapeDtypeStruct((1, D), F32)
    return _compute_call(
        body, (p, ln_g, ln_b, w_s, bs_t, dout), name=name, grid=(t // CHUNK,), in_specs=_sg_specs() + [_row_spec(D, CHUNK)],
        out_specs=[pl.BlockSpec((CHUNK, 3 * D), lambda i: (i, 0)), _vec_spec(D), _vec_spec(D),
                   pl.BlockSpec((SG_GROUPS, CHUNK, CHUNK), lambda i: (0, 0, 0)), _vec_spec(LANES, CHUNK)],
        out_shape=[jax.ShapeDtypeStruct((t, 3 * D), BF16), vec, vec,
                   jax.ShapeDtypeStruct((SG_GROUPS, CHUNK, CHUNK), F32), jax.ShapeDtypeStruct((CHUNK, LANES), F32)],
        semantics=("arbitrary",), rider=rider)


SWA_COLS = 2 * D + 2 * KVW
KV_BLOCK = 2 * KVW


def _lane_roll(x, shift):
    return pltpu.roll(x, shift, 1)


def _rot_half(x):
    w = x.shape[1]
    lane = lax.broadcasted_iota(jnp.int32, x.shape, 1)
    return jnp.where(lane % HEAD < HEAD // 2, -_lane_roll(x, w - HEAD // 2), _lane_roll(x, HEAD // 2))


@jax.custom_vjp
def _rope(x, cos, sin):
    return x * cos + _rot_half(x) * sin


def _rope_fwd(x, cos, sin):
    return _rope(x, cos, sin), (cos, sin)


def _rope_bwd(res, ct):
    cos, sin = res
    return ct * cos - _rot_half(ct) * sin, jnp.zeros_like(cos), jnp.zeros_like(sin)


_rope.defvjp(_rope_fwd, _rope_bwd)


@jax.custom_vjp
def _swap_halves(x):
    return _lane_roll(x, HEAD)


_swap_halves.defvjp(lambda x: (_lane_roll(x, HEAD), None), lambda _, ct: (_lane_roll(ct, HEAD),))


def _f_swa(pq, pkv, cos, sin, cosp, sinp, sink_row, valid):
    reps = D // LANES
    q = _rope(pq[:, :D], jnp.tile(cos, (1, reps)), jnp.tile(sin, (1, reps))) * (HEAD ** -0.5)
    k = _rope(pq[:, D:D + KVW], jnp.tile(cos, (1, KVW // LANES)), jnp.tile(sin, (1, KVW // LANES)))
    kp = _rope(pkv[:, :KVW], jnp.tile(cosp, (1, KVW // LANES)), jnp.tile(sinp, (1, KVW // LANES)))
    v, vp, z = pq[:, D + KVW:D + 2 * KVW], pkv[:, KVW:], pq[:, D + 2 * KVW:]
    kcat = jnp.concatenate([kp, k], axis=0)
    vcat = jnp.concatenate([vp, v], axis=0)
    lane = lax.broadcasted_iota(jnp.int32, (2 * CHUNK, LANES), 1)
    lo = lane < HEAD
    hlane = lax.broadcasted_iota(jnp.int32, (1, LANES), 1)

    def halves(cat, g):
        blk = cat[:, (g // 2) * LANES:(g // 2 + 1) * LANES]
        other = _swap_halves(blk)
        if g % 2 == 0:
            return jnp.where(lo, blk, 0.0), jnp.where(lo, 0.0, other)
        return jnp.where(lo, other, 0.0), jnp.where(lo, 0.0, blk)

    pairs = N_HEADS // KV_HEADS // 2
    valid_g = jnp.tile(valid, (pairs, 1))

    def probs(s, heads):
        sink = jnp.concatenate(
            [jnp.broadcast_to(jnp.sum(jnp.where(hlane == h, sink_row, 0.0), axis=1, keepdims=True), (CHUNK, 1))
             for h in heads], axis=0)
        s = jnp.where(valid_g, s, NEG)
        m = lax.stop_gradient(jnp.maximum(jnp.max(s, axis=1, keepdims=True), sink))
        e = jnp.exp(s - m)
        return e / (jnp.sum(e, axis=1, keepdims=True) + jnp.exp(sink - m))

    outs = []
    for g in range(KV_HEADS):
        k_lo, k_hi = halves(kcat, g)
        v_lo, v_hi = halves(vcat, g)
        tiles = range(g * pairs, (g + 1) * pairs)
        qg = jnp.concatenate([q[:, j * LANES:(j + 1) * LANES] for j in tiles], axis=0)
        p_a = probs(_bdot_nt(qg, k_lo), [2 * j for j in tiles])
        p_b = probs(_bdot_nt(qg, k_hi), [2 * j + 1 for j in tiles])
        og = _bdot_nn(p_a, v_lo) + _bdot_nn(p_b, v_hi)
        outs += [og[n * CHUNK:(n + 1) * CHUNK] for n in range(pairs)]
    return jnp.concatenate(outs, axis=1) * jax.nn.silu(z)


def _swa_valid(block):
    qi = lax.broadcasted_iota(jnp.int32, (CHUNK, 2 * CHUNK), 0)
    kj = lax.broadcasted_iota(jnp.int32, (CHUNK, 2 * CHUNK), 1)
    rel = qi + CHUNK - kj
    return (rel >= 0) & (rel < CHUNK) & ((kj >= CHUNK) | (block > 0))


def _swa_specs(blk):
    prev = lambda i: jnp.maximum(blk(i) - 1, 0)
    kv_col = D // KV_BLOCK
    return [pl.BlockSpec((CHUNK, SWA_COLS), lambda i: (blk(i), 0)),
            pl.BlockSpec((CHUNK, KV_BLOCK), lambda i: (prev(i), kv_col)),
            pl.BlockSpec((CHUNK, LANES), lambda i: (blk(i), 0)), pl.BlockSpec((CHUNK, LANES), lambda i: (blk(i), 0)),
            pl.BlockSpec((CHUNK, LANES), lambda i: (prev(i), 0)), pl.BlockSpec((CHUNK, LANES), lambda i: (prev(i), 0)),
            _vec_spec(LANES)]


def _swa_fwd(p, cos, sin, sink_row, name, rider=None):
    t = p.shape[0]

    def body(pq_ref, pkv_ref, c_ref, s_ref, cp_ref, sp_ref, sk_ref, o_ref):
        valid = _swa_valid(pl.program_id(0))
        o_ref[...] = _f_swa(pq_ref[...], pkv_ref[...], c_ref[...], s_ref[...], cp_ref[...], sp_ref[...],
                            sk_ref[...], valid).astype(BF16)

    return _compute_call(
        body, (p, p, cos, sin, cos, sin, sink_row), name=name, grid=(t // CHUNK,), in_specs=_swa_specs(lambda i: i),
        out_specs=_row_spec(D, CHUNK), out_shape=jax.ShapeDtypeStruct((t, D), BF16), semantics=("parallel",), rider=rider)


def _swa_bwd(p, cos, sin, sink_row, dout, name, rider=None):
    t = p.shape[0]
    nb = t // CHUNK
    blk = lambda i: nb - 1 - i

    def body(pq_ref, pkv_ref, c_ref, s_ref, cp_ref, sp_ref, sk_ref, do_ref, dp_ref, dsk_ref, pend_ref):
        i = pl.program_id(0)
        valid = _swa_valid(blk(i))
        f = functools.partial(_f_swa, valid=valid)
        _, vjp = jax.vjp(f, pq_ref[...], pkv_ref[...], c_ref[...], s_ref[...], cp_ref[...], sp_ref[...], sk_ref[...])
        dpq, dpkv, _, _, _, _, dsk = vjp(do_ref[...])

        @pl.when(i == 0)
        def _():
            pend_ref[...] = jnp.zeros_like(pend_ref)

        dp_ref[...] = jnp.concatenate(
            [dpq[:, :D], dpq[:, D:D + KV_BLOCK] + pend_ref[...], dpq[:, D + KV_BLOCK:]], axis=1).astype(BF16)
        pend_ref[...] = dpkv
        _accum(dsk_ref, dsk, i == 0)

    return _compute_call(
        body, (p, p, cos, sin, cos, sin, sink_row, dout), name=name, grid=(nb,),
        in_specs=_swa_specs(blk) + [pl.BlockSpec((CHUNK, D), lambda i: (blk(i), 0))],
        out_specs=[pl.BlockSpec((CHUNK, SWA_COLS), lambda i: (blk(i), 0)), _vec_spec(LANES)],
        out_shape=[jax.ShapeDtypeStruct((t, SWA_COLS), BF16), jax.ShapeDtypeStruct((1, LANES), F32)],
        scratch_shapes=[pltpu.VMEM((CHUNK, KV_BLOCK), F32)], semantics=("arbitrary",), rider=rider)


RW_MAIN = 4 * D
RW_LO = 2 * LORA_PAD
VM = LANES // N_HEADS
VD = HEAD // VM
S_ROWS = VD * HEAD
TB_RW = 128
TB_K = 32
TB_SCAN = 16


def _dim_major(a):
    return a.reshape(a.shape[:-1] + (N_HEADS, HEAD)).swapaxes(-1, -2).reshape(a.shape)


def _head_major(a):
    return a.reshape(a.shape[:-1] + (HEAD, N_HEADS)).swapaxes(-1, -2).reshape(a.shape)


def _compact(x):
    return x.reshape(x.shape[0], VD, LANES)


def _param_compact(w):
    return _dim_major(w).reshape(VD, LANES)


def _param_compact_inv(pc):
    return _head_major(pc.reshape(-1))


def _f_rwkv_lora(xs_lo, w0, a0, wl, al):
    decay = jnp.exp(-DECAY_SCALE * jax.nn.sigmoid(w0 + _bdot_nn(jnp.tanh(xs_lo[:, :LORA_PAD]), wl)))
    a = jax.nn.sigmoid(a0 + _bdot_nn(xs_lo[:, LORA_PAD:], al))
    return decay, a


def _prev_rows_spec(width, tb):
    return pl.BlockSpec((8, width), lambda i: (jnp.maximum(i * (tb // 8) - 1, 0), 0))


def _token_shift_lerp(p, prev8, mu, first):
    rows = lax.broadcasted_iota(jnp.int32, p.shape, 0)
    prev = jnp.where(first, 0.0, prev8[7:8, :])
    shifted = jnp.where(rows == 0, prev, pltpu.roll(p, 1, 0))
    return p + (shifted - p) * mu


def _rwkv_pre_fwd(p_main, p_lo, mu_main, mu_lo, w0, a0, wl, al, name):
    t = p_main.shape[0]
    tb = TB_RW

    def body(pm_ref, pmp_ref, pl_ref, plp_ref, mm_ref, ml_ref, w0_ref, a0_ref, wl_ref, al_ref,
             xm_ref, xl_ref, dec_ref, a_ref):
        first = pl.program_id(0) == 0
        xm_ref[...] = _token_shift_lerp(pm_ref[...], pmp_ref[...], mm_ref[...], first)
        xs_lo = _token_shift_lerp(pl_ref[...], plp_ref[...], ml_ref[...], first)
        xl_ref[...] = xs_lo
        dec_ref[...], a_ref[...] = _f_rwkv_lora(xs_lo, w0_ref[...], a0_ref[...], wl_ref[...], al_ref[...])

    return pl.pallas_call(
        body, name=name, grid=(t // tb,),
        in_specs=[_row_spec(RW_MAIN, tb), _prev_rows_spec(RW_MAIN, tb), _row_spec(RW_LO, tb), _prev_rows_spec(RW_LO, tb),
                  _vec_spec(RW_MAIN), _vec_spec(RW_LO), _vec_spec(D), _vec_spec(D),
                  _vec_spec(D, LORA_PAD), _vec_spec(D, LORA_PAD)],
        out_specs=[_row_spec(RW_MAIN, tb), _row_spec(RW_LO, tb), _row_spec(D, tb), _row_spec(D, tb)],
        out_shape=[jax.ShapeDtypeStruct((t, RW_MAIN), F32), jax.ShapeDtypeStruct((t, RW_LO), F32),
                   jax.ShapeDtypeStruct((t, D), F32), jax.ShapeDtypeStruct((t, D), F32)],
        compiler_params=_cparams("parallel"),
    )(p_main, p_main, p_lo, p_lo, mu_main, mu_lo, w0, a0, wl, al)


def _rwkv_lora_bwd(xs_lo, w0, a0, wl, al, ddecay, da, name):
    t = xs_lo.shape[0]
    tb = TB_NORM

    def body(x_ref, w0_ref, a0_ref, wl_ref, al_ref, dd_ref, da_ref, dx_ref, dw0_ref, da0_ref, dwl_ref, dal_ref):
        _, vjp = jax.vjp(_f_rwkv_lora, x_ref[...], w0_ref[...], a0_ref[...], wl_ref[...], al_ref[...])
        dx, dw0, da0, dwl, dal = vjp((dd_ref[...], da_ref[...]))
        dx_ref[...] = dx
        first = pl.program_id(0) == 0
        _accum(dw0_ref, dw0, first)
        _accum(da0_ref, da0, first)
        _accum(dwl_ref, dwl, first)
        _accum(dal_ref, dal, first)

    vec = jax.ShapeDtypeStruct((1, D), F32)
    lor = jax.ShapeDtypeStruct((LORA_PAD, D), F32)
    return pl.pallas_call(
        body, name=name, grid=(t // tb,),
        in_specs=[_row_spec(RW_LO), _vec_spec(D), _vec_spec(D), _vec_spec(D, LORA_PAD), _vec_spec(D, LORA_PAD),
                  _row_spec(D), _row_spec(D)],
        out_specs=[_row_spec(RW_LO), _vec_spec(D), _vec_spec(D), _vec_spec(D, LORA_PAD), _vec_spec(D, LORA_PAD)],
        out_shape=[jax.ShapeDtypeStruct((t, RW_LO), F32), vec, vec, lor, lor], compiler_params=_cparams("arbitrary"),
    )(xs_lo, w0, a0, wl, al, ddecay, da)


def _lerp_bwd(p, dxs_groups, mu, name):
    t, width = p.shape
    tb = TB_RW
    nb = t // tb
    parts = [a for group in dxs_groups for a in group]

    def body(p_ref, pp_ref, mu_ref, *refs):
        d_refs, (dp_ref, dmu_ref) = refs[:2 * len(parts)], refs[2 * len(parts):]
        i = pl.program_id(0)

        def columns(k):
            pick = (lambda r: r[0:1, :]) if k else (lambda r: r[...])
            vals, at = [], 0
            for group in dxs_groups:
                vals.append(functools.reduce(jnp.add, [pick(d_refs[2 * (at + n) + k]) for n in range(len(group))]))
                at += len(group)
            return jnp.concatenate(vals, axis=1)

        pv, dv, mu_v = p_ref[...], columns(0), mu_ref[...]
        rows = lax.broadcasted_iota(jnp.int32, pv.shape, 0)
        prev = jnp.where(i == 0, 0.0, pp_ref[7:8, :])
        shifted = jnp.where(rows == 0, prev, pltpu.roll(pv, 1, 0))
        nxt = jnp.where(i == nb - 1, 0.0, columns(1))
        d_next = jnp.where(rows == tb - 1, nxt, pltpu.roll(dv, tb - 1, 0))
        dp_ref[...] = (dv * (1.0 - mu_v) + d_next * mu_v).astype(BF16)
        _accum(dmu_ref, jnp.sum(dv * (shifted - pv), axis=0, keepdims=True), i == 0)

    d_specs = []
    for a in parts:
        d_specs += [_row_spec(a.shape[1], tb),
                    pl.BlockSpec((8, a.shape[1]), lambda i: (jnp.minimum((i + 1) * (tb // 8), t // 8 - 1), 0))]
    return pl.pallas_call(
        body, name=name, grid=(nb,),
        in_specs=[_row_spec(width, tb), _prev_rows_spec(width, tb), _vec_spec(width)] + d_specs,
        out_specs=[_row_spec(width, tb), _vec_spec(width)],
        out_shape=[jax.ShapeDtypeStruct((t, width), BF16), jax.ShapeDtypeStruct((1, width), F32)],
        compiler_params=_cparams("arbitrary"),
    )(p, p, mu, *[a for a in parts for _ in range(2)])


def _lane_group_sum2d(x):
    x = x + pltpu.roll(x, N_HEADS, 1)
    return x + pltpu.roll(x, 2 * N_HEADS, 1)


@jax.custom_vjp
def _lane_group_sum(x):
    return _lane_group_sum2d(x.reshape(-1, LANES)).reshape(x.shape)


_lane_group_sum.defvjp(lambda x: (_lane_group_sum(x), None), lambda _, ct: (_lane_group_sum(ct),))


def _head_sum(x):
    return _lane_group_sum(jnp.sum(x, axis=1, keepdims=True))


def _f_kprep(k, a, r, kkp, kap, rkp):
    kk = k * kkp
    kk = kk / jnp.maximum(jnp.sqrt(_head_sum(kk * kk)), 1e-12)
    k2 = k * (1.0 + (a - 1.0) * kap)
    return kk, k2, kk * a, _head_sum(r * k2 * rkp)


def _k_spec(rows=HEAD, tb=TB_K):
    return pl.BlockSpec((tb, rows, LANES), lambda i: (i, 0, 0))


def _kparam_spec(rows=HEAD):
    return pl.BlockSpec((rows, LANES), lambda i: (0, 0))


def _lane_group(shape):
    return lax.broadcasted_iota(jnp.int32, shape, len(shape) - 1) // N_HEADS


def _store_k_layout(ref, xc):
    x2 = xc.reshape(-1, LANES)
    group = _lane_group(x2.shape)
    for q in range(VM):
        one = jnp.where(group == q, x2, 0.0)
        ref[:, pl.ds(q, VD, stride=VM), :] = _lane_group_sum2d(one).reshape(xc.shape)


def _load_compact(ref):
    acc = None
    for q in range(VM):
        rows = _lane_group_sum(ref[:, pl.ds(q, VD, stride=VM), :])
        part = jnp.where(_lane_group(rows.shape) == q, rows, 0.0)
        acc = part if acc is None else acc + part
    return acc


def _rwkv_kprep_fwd(k, a, r, w, kkp, kap, rkp, name):
    t = k.shape[0]

    def body(k_ref, a_ref, r_ref, w_ref, kkp_ref, kap_ref, rkp_ref, kk_ref, k2_ref, b_ref, r4_ref, w4_ref, rk_ref):
        rv = r_ref[...]
        kk, k2, b, rk_ref[...] = _f_kprep(k_ref[...], a_ref[...], rv, kkp_ref[...], kap_ref[...], rkp_ref[...])
        for ref, val in ((kk_ref, kk), (k2_ref, k2), (b_ref, b), (r4_ref, rv), (w4_ref, w_ref[...])):
            _store_k_layout(ref, val)

    big = jax.ShapeDtypeStruct((t, HEAD, LANES), F32)
    return pl.pallas_call(
        body, name=name, grid=(t // TB_K,),
        in_specs=[_k_spec(VD)] * 4 + [_kparam_spec(VD)] * 3, out_specs=[_k_spec()] * 5 + [_k_spec(1)],
        out_shape=[big] * 5 + [jax.ShapeDtypeStruct((t, 1, LANES), F32)], compiler_params=_cparams("parallel"),
    )(k, a, r, w, kkp, kap, rkp)


def _rwkv_kprep_bwd(k, a, r, kkp, kap, rkp, dkk, dk2, db, drk, dr_scan, dw_scan, name):
    t = k.shape[0]

    def body(k_ref, a_ref, r_ref, kkp_ref, kap_ref, rkp_ref, dkk_ref, dk2_ref, db_ref, drk_ref, drs_ref, dws_ref,
             dk_ref, da_ref, dr_ref, dw_ref, dkkp_ref, dkap_ref, drkp_ref):
        _, vjp = jax.vjp(_f_kprep, k_ref[...], a_ref[...], r_ref[...], kkp_ref[...], kap_ref[...], rkp_ref[...])
        dk, da, dr, dkkp, dkap, drkp = vjp((_load_compact(dkk_ref), _load_compact(dk2_ref), _load_compact(db_ref),
                                            drk_ref[...]))
        dk_ref[...] = dk
        da_ref[...] = da
        dr_ref[...] = dr + _load_compact(drs_ref)
        dw_ref[...] = _load_compact(dws_ref)
        first = pl.program_id(0) == 0
        _accum(dkkp_ref, dkkp, first)
        _accum(dkap_ref, dkap, first)
        _accum(drkp_ref, drkp, first)

    cl = jax.ShapeDtypeStruct((t, VD, LANES), F32)
    par = jax.ShapeDtypeStruct((VD, LANES), F32)
    return pl.pallas_call(
        body, name=name, grid=(t // TB_K,),
        in_specs=[_k_spec(VD)] * 3 + [_kparam_spec(VD)] * 3 + [_k_spec()] * 3 + [_k_spec(1), _k_spec(), _k_spec()],
        out_specs=[_k_spec(VD)] * 4 + [_kparam_spec(VD)] * 3,
        out_shape=[cl] * 4 + [par] * 3, compiler_params=_cparams("arbitrary"),
    )(k, a, r, kkp, kap, rkp, dkk, dk2, db, drk, dr_scan, dw_scan)


def _f_post(y, v, rk, g, b):
    mean = _lane_group_sum(jnp.sum(y, axis=1, keepdims=True)) * (1.0 / HEAD)
    yc = y - mean
    var = _lane_group_sum(jnp.sum(yc * yc, axis=1, keepdims=True)) * (1.0 / HEAD)
    return yc * lax.rsqrt(var + GN_EPS) * g + b + rk * v


def _rwkv_post_fwd(y, v, rk, g, b, name):
    t = y.shape[0]

    def body(y_ref, v_ref, rk_ref, g_ref, b_ref, o_ref):
        o_ref[...] = _f_post(y_ref[...], v_ref[...], rk_ref[...], g_ref[...], b_ref[...])

    return pl.pallas_call(
        body, name=name, grid=(t // TB_K,),
        in_specs=[_k_spec(VD), _k_spec(VD), _k_spec(1), _kparam_spec(VD), _kparam_spec(VD)], out_specs=_k_spec(VD),
        out_shape=jax.ShapeDtypeStruct((t, VD, LANES), F32), compiler_params=_cparams("parallel"),
    )(y, v, rk, g, b)


def _rwkv_post_bwd(y, v, rk, g, b, do, name):
    t = y.shape[0]

    def body(y_ref, v_ref, rk_ref, g_ref, b_ref, do_ref, dy_ref, dv_ref, drk_ref, dg_ref, db_ref):
        _, vjp = jax.vjp(_f_post, y_ref[...], v_ref[...], rk_ref[...], g_ref[...], b_ref[...])
        dy, dv, drk, dg, db = vjp(do_ref[...])
        dy_ref[...] = dy
        dv_ref[...] = dv
        drk_ref[...] = drk
        first = pl.program_id(0) == 0
        _accum(dg_ref, dg, first)
        _accum(db_ref, db, first)

    vl = jax.ShapeDtypeStruct((t, VD, LANES), F32)
    par = jax.ShapeDtypeStruct((VD, LANES), F32)
    return pl.pallas_call(
        body, name=name, grid=(t // TB_K,),
        in_specs=[_k_spec(VD), _k_spec(VD), _k_spec(1), _kparam_spec(VD), _kparam_spec(VD), _k_spec(VD)],
        out_specs=[_k_spec(VD), _k_spec(VD), _k_spec(1), _kparam_spec(VD), _kparam_spec(VD)],
        out_shape=[vl, vl, jax.ShapeDtypeStruct((t, 1, LANES), F32), par, par], compiler_params=_cparams("arbitrary"),
    )(y, v, rk, g, b, do)


def _f_gate(o, z):
    return o * jax.nn.silu(z)


def _z_spec(tb=TB_NORM):
    return pl.BlockSpec((tb, D), lambda i: (i, 3))


def _rwkv_gate_fwd(o, xs_main, name):
    t = o.shape[0]

    def body(o_ref, z_ref, u_ref):
        u_ref[...] = _f_gate(o_ref[...], z_ref[...]).astype(BF16)

    return pl.pallas_call(
        body, name=name, grid=(t // TB_NORM,), in_specs=[_row_spec(D), _z_spec()], out_specs=_row_spec(D),
        out_shape=jax.ShapeDtypeStruct((t, D), BF16), compiler_params=_cparams("parallel"),
    )(o, xs_main)


def _rwkv_gate_bwd(o, xs_main, du, name):
    t = o.shape[0]

    def body(o_ref, z_ref, du_ref, do_ref, dz_ref):
        _, vjp = jax.vjp(_f_gate, o_ref[...], z_ref[...])
        do_ref[...], dz_ref[...] = vjp(du_ref[...])

    full = jax.ShapeDtypeStruct((t, D), F32)
    return pl.pallas_call(
        body, name=name, grid=(t // TB_NORM,), in_specs=[_row_spec(D), _z_spec(), _row_spec(D)],
        out_specs=[_row_spec(D), _row_spec(D)], out_shape=[full, full], compiler_params=_cparams("parallel"),
    )(o, xs_main, du)


def _colsum(x):
    return jnp.sum(x, axis=0, keepdims=True)


def _rwkv_scan_fwd(r4, w4, k24, kk4, b4, v, name, rider=None):
    t = r4.shape[0]
    tb = TB_SCAN

    def body(r_ref, w_ref, k2_ref, kk_ref, b_ref, v_ref, y_ref, sall_ref, sa_ref, s_scr):
        @pl.when(pl.program_id(0) == 0)
        def _():
            s_scr[...] = jnp.zeros_like(s_scr)

        sall_ref[0] = s_scr[...]

        def step(tt, dst):
            kk = kk_ref[tt]
            sas = []
            for vd in range(VD):
                sa = _colsum(sall_ref[tt, pl.ds(vd * HEAD, HEAD), :] * kk)
                sa_ref[tt, pl.ds(vd, 1), :] = sa
                sas.append(sa)
            w, b, k2, r = w_ref[tt], b_ref[tt], k2_ref[tt], r_ref[tt]
            for vd in range(VD):
                rows = pl.ds(vd * HEAD, HEAD)
                s = sall_ref[tt, rows, :] * w - sas[vd] * b + v_ref[tt, pl.ds(vd, 1), :] * k2
                dst[rows, :] = s
                y_ref[tt, pl.ds(vd, 1), :] = _colsum(s * r)

        def loop_step(tt, carry):
            step(tt, sall_ref.at[tt + 1])
            return carry

        lax.fori_loop(0, tb - 1, loop_step, 0)
        step(tb - 1, s_scr)

    vl = jax.ShapeDtypeStruct((t, VD, LANES), F32)
    return _compute_call(
        body, (r4, w4, k24, kk4, b4, v), name=name, grid=(t // tb,),
        in_specs=[_k_spec(HEAD, tb)] * 5 + [_k_spec(VD, tb)],
        out_specs=[_k_spec(VD, tb), _k_spec(S_ROWS, tb), _k_spec(VD, tb)],
        out_shape=[vl, jax.ShapeDtypeStruct((t, S_ROWS, LANES), F32), vl],
        scratch_shapes=[pltpu.VMEM((S_ROWS, LANES), F32)], semantics=("arbitrary",), rider=rider)


def _rwkv_scan_bwd(dy, s_all, sa_all, r4, w4, k24, kk4, b4, v, name, rider=None):
    t = r4.shape[0]
    tb = TB_SCAN
    nb = t // tb
    blk = lambda i: nb - 1 - i

    def body(dy_ref, sall_ref, sa_ref, r_ref, w_ref, k2_ref, kk_ref, b_ref, v_ref,
             dr_ref, dw_ref, dk2_ref, dkk_ref, db_ref, dv_ref, ds_scr):
        @pl.when(pl.program_id(0) == 0)
        def _():
            ds_scr[...] = jnp.zeros_like(ds_scr)

        def step(j, carry):
            tt = tb - 1 - j
            vrow = lambda ref, vd: ref[tt, pl.ds(vd, 1), :]
            srows = lambda vd: pl.ds(vd * HEAD, HEAD)
            r, k2, b = r_ref[tt], k2_ref[tt], b_ref[tt]
            dsas = []
            for vd in range(VD):
                ds = ds_scr[srows(vd), :] + vrow(dy_ref, vd) * r
                ds_scr[srows(vd), :] = ds
                dv_ref[tt, pl.ds(vd, 1), :] = _colsum(ds * k2)
                dsas.append(-_colsum(ds * b))
            zero = jnp.zeros((HEAD, LANES), F32)
            dk2, q, sady, vdy = zero, zero, 0.0, 0.0
            for vd in range(VD):
                dyv = vrow(dy_ref, vd)
                dk2 = dk2 + ds_scr[srows(vd), :] * vrow(v_ref, vd)
                q = q + sall_ref[tt, srows(vd), :] * dyv
                sady = sady + vrow(sa_ref, vd) * dyv
                vdy = vdy + vrow(v_ref, vd) * dyv
            dk2_ref[tt] = dk2
            dr_ref[tt] = w_ref[tt] * q - b_ref[tt] * sady + k2_ref[tt] * vdy
            dw, dkk = zero, zero
            for vd in range(VD):
                sp = sall_ref[tt, srows(vd), :]
                dw = dw + ds_scr[srows(vd), :] * sp
                dkk = dkk + sp * dsas[vd]
            dw_ref[tt] = dw
            dkk_ref[tt] = dkk
            w, kk = w_ref[tt], kk_ref[tt]
            db = zero
            for vd in range(VD):
                ds = ds_scr[srows(vd), :]
                db = db - ds * vrow(sa_ref, vd)
                ds_scr[srows(vd), :] = ds * w + dsas[vd] * kk
            db_ref[tt] = db
            return carry

        lax.fori_loop(0, tb, step, 0)

    rk = lambda rows: pl.BlockSpec((tb, rows, LANES), lambda i: (blk(i), 0, 0))
    big = jax.ShapeDtypeStruct((t, HEAD, LANES), F32)
    return _compute_call(
        body, (dy, s_all, sa_all, r4, w4, k24, kk4, b4, v), name=name, grid=(nb,),
        in_specs=[rk(VD), rk(S_ROWS), rk(VD)] + [rk(HEAD)] * 5 + [rk(VD)],
        out_specs=[rk(HEAD)] * 5 + [rk(VD)],
        out_shape=[big] * 5 + [jax.ShapeDtypeStruct((t, VD, LANES), F32)],
        scratch_shapes=[pltpu.VMEM((S_ROWS, LANES), F32)], semantics=("arbitrary",), rider=rider)


def _rwkv_mixer_fwd(p_main, p_lo, prm, tag, rider):
    xs_main, xs_lo, decay, a = _rwkv_pre_fwd(p_main, p_lo, prm["mu_main"], prm["mu_lo"], prm["w0"], prm["a0"],
                                             prm["wl"], prm["al"], tag + "_pre")
    r, k, v, w, a = (_compact(x) for x in (xs_main[:, :D], xs_main[:, D:2 * D], xs_main[:, 2 * D:3 * D], decay, a))
    kk4, k24, b4, r4, w4, rk = _rwkv_kprep_fwd(k, a, r, w, prm["kkp"], prm["kap"], prm["rkp"], tag + "_kprep")
    (y, s_all, sa_all), ridden = _ridden(_rwkv_scan_fwd(r4, w4, k24, kk4, b4, v, tag + "_scan", rider), rider)
    o = _rwkv_post_fwd(y, v, rk, prm["gn_g"], prm["gn_b"], tag + "_post").reshape(-1, D)
    u = _rwkv_gate_fwd(o, xs_main, tag + "_gate")
    saved = dict(xs_main=xs_main, xs_lo=xs_lo, r=r, k=k, a=a, v=v, r4=r4, w4=w4, kk4=kk4, k24=k24, b4=b4, rk=rk,
                 y=y, s_all=s_all, sa_all=sa_all, o=o)
    return u, saved, ridden


def _rwkv_mixer_bwd(p_main, p_lo, prm, sv, du, tag, rider):
    do, dz = _rwkv_gate_bwd(sv["o"], sv["xs_main"], du, tag + "_gate_b")
    dy, dv_post, drk, dgn_g, dgn_b = _rwkv_post_bwd(sv["y"], sv["v"], sv["rk"], prm["gn_g"], prm["gn_b"],
                                                    _compact(do), tag + "_post_b")
    (dr_s, dw_s, dk24, dkk4, db4, dv_scan), ridden = _ridden(_rwkv_scan_bwd(
        dy, sv["s_all"], sv["sa_all"], sv["r4"], sv["w4"], sv["k24"], sv["kk4"], sv["b4"], sv["v"], tag + "_scan_b", rider), rider)
    dk, da, dr, dw, dkkp, dkap, drkp = _rwkv_kprep_bwd(sv["k"], sv["a"], sv["r"], prm["kkp"], prm["kap"], prm["rkp"],
                                                       dkk4, dk24, db4, drk, dr_s, dw_s, tag + "_kprep_b")
    flat = lambda xc: xc.reshape(-1, D)
    dxs_lo, dw0, da0, dwl, dal = _rwkv_lora_bwd(sv["xs_lo"], prm["w0"], prm["a0"], prm["wl"], prm["al"],
                                                flat(dw), flat(da), tag + "_lora_b")
    dxs_main = [[flat(dr)], [flat(dk)], [flat(dv_post), flat(dv_scan)], [dz]]
    dp_main, dmu_main = _lerp_bwd(p_main, dxs_main, prm["mu_main"], tag + "_lerp_main_b")
    dp_lo, dmu_lo = _lerp_bwd(p_lo, [[dxs_lo]], prm["mu_lo"], tag + "_lerp_lo_b")
    grads = dict(mu_main=dmu_main, mu_lo=dmu_lo, w0=dw0, a0=da0, wl=dwl, al=dal, kkp=dkkp, kap=dkap, rkp=drkp,
                 gn_g=dgn_g, gn_b=dgn_b)
    return dp_main, dp_lo, grads, ridden


N_DEV = 8
N_CHIPS = 4
ANY = pl.BlockSpec(memory_space=pl.ANY)


def _place():
    return lax.axis_index("x"), lax.axis_index("y"), lax.axis_index("c")


def _remote(src, dst, send_sems, recv_sems, k, dev):
    return pltpu.make_async_remote_copy(src_ref=src, dst_ref=dst, send_sem=send_sems.at[k], recv_sem=recv_sems.at[k],
                                        device_id=dev, device_id_type=MESHT)


def _all_gather8(v, name):
    def body(buf_ref, out_ref, send_sems, recv_sems):
        del buf_ref
        x, y, c = _place()
        mine = out_ref.at[4 * x + 2 * y + c]
        peers = [(x ^ (k >> 2), y ^ ((k >> 1) & 1), c ^ (k & 1)) for k in range(1, N_DEV)]
        sends = [_remote(mine, mine, send_sems, recv_sems, k, peer) for k, peer in enumerate(peers)]
        for cp in sends:
            cp.start()
        for k, (px, py, pc) in enumerate(peers):
            _remote(mine, out_ref.at[4 * px + 2 * py + pc], send_sems, recv_sems, k, (x, y, c)).wait_recv()
        for cp in sends:
            cp.wait_send()

    return pl.pallas_call(
        body, name=name, in_specs=[ANY], out_specs=ANY, input_output_aliases={0: 0},
        out_shape=jax.ShapeDtypeStruct((N_DEV,) + v.shape, v.dtype),
        scratch_shapes=[pltpu.SemaphoreType.DMA((N_DEV - 1,)), pltpu.SemaphoreType.DMA((N_DEV - 1,))],
    )(jnp.broadcast_to(v[None], (N_DEV,) + v.shape))


def _other_chips(x, y):
    return [(1 - x, y), (x, 1 - y), (1 - x, 1 - y)]


GATHER_SEMS = 6


def _gather_buffer(v):
    return jnp.broadcast_to(v[None], (N_CHIPS,) + v.shape)


def _gather_start(bufs, send_sems, recv_sems):
    x, y, c = _place()
    for i, buf in enumerate(bufs):
        mine = buf.at[2 * x + y, c]
        for j, (cx, cy) in enumerate(_other_chips(x, y)):
            _remote(mine, mine, send_sems, recv_sems, GATHER_SEMS * i + j, (cx, cy, c)).start()


def _gather_finish(bufs, send_sems, recv_sems):
    x, y, c = _place()
    chips = _other_chips(x, y)
    passed = []
    for i, buf in enumerate(bufs):
        mine = buf.at[2 * x + y, c]
        for j, (cx, cy) in enumerate(chips):
            landed = buf.at[2 * cx + cy, c]
            _remote(mine, landed, send_sems, recv_sems, GATHER_SEMS * i + j, (x, y, c)).wait_recv()
            fwd = _remote(landed, landed, send_sems, recv_sems, GATHER_SEMS * i + 3 + j, (x, y, 1 - c))
            fwd.start()
            passed.append(fwd)
    for i, buf in enumerate(bufs):
        mine = buf.at[2 * x + y, c]
        for j, (cx, cy) in enumerate(chips):
            _remote(mine, buf.at[2 * cx + cy, 1 - c], send_sems, recv_sems, GATHER_SEMS * i + 3 + j, (x, y, c)).wait_recv()
            _remote(mine, mine, send_sems, recv_sems, GATHER_SEMS * i + j, (cx, cy, c)).wait_send()
    for fwd in passed:
        fwd.wait_send()


def _gather_rider(bufs):
    return _Rider(bufs, GATHER_SEMS * len(bufs), _gather_start, _gather_finish)


def _chip_gather(bufs, name):
    n = len(bufs)

    def body(*refs):
        out_refs, (send_sems, recv_sems) = refs[n:2 * n], refs[2 * n:]
        _gather_start(out_refs, send_sems, recv_sems)
        _gather_finish(out_refs, send_sems, recv_sems)

    return pl.pallas_call(
        body, name=name, in_specs=[ANY] * n, out_specs=[ANY] * n, input_output_aliases={i: i for i in range(n)},
        out_shape=[jax.ShapeDtypeStruct(b.shape, b.dtype) for b in bufs], scratch_shapes=_dma_sems(GATHER_SEMS * n),
    )(*bufs)


RS_W = 1024
RS_BLOCK_BYTES = 4 << 20


def _dma_sems(n):
    return [pltpu.SemaphoreType.DMA((n,)), pltpu.SemaphoreType.DMA((n,))]


def _pair_exchange_copies(refs, send_sems, recv_sems):
    n = len(refs) // 2
    x, y, c = _place()
    return [_remote(refs[i].at[s, 1 - c], refs[n + i].at[s], send_sems, recv_sems, N_CHIPS * i + s, (x, y, 1 - c))
            for i in range(n) for s in range(N_CHIPS)]


def _pair_exchange_start(refs, send_sems, recv_sems):
    for cp in _pair_exchange_copies(refs, send_sems, recv_sems):
        cp.start()


def _pair_exchange_finish(refs, send_sems, recv_sems):
    for cp in _pair_exchange_copies(refs, send_sems, recv_sems):
        cp.wait()


def _pair_exchange_rider(gs):
    landing = [lax.empty((N_CHIPS,) + g.shape[2:], g.dtype) for g in gs]
    return _Rider(list(gs) + landing, N_CHIPS * len(gs), _pair_exchange_start, _pair_exchange_finish)


def _rs_rows(rows, cols):
    cap = max(16, RS_BLOCK_BYTES // (N_CHIPS * 4 * cols))
    return rows if rows <= cap else max(d for d in range(16, cap + 1, 16) if rows % d == 0)


def _rs_pair_add(g, got, c_arr, name):
    _, _, rows, width = g.shape
    tr = _rs_rows(rows, width)

    def body(c_ref, g_ref, got_ref, p_ref):
        p_ref[...] = (g_ref[...] + got_ref[...]).astype(BF16)

    return pl.pallas_call(
        body, name=name,
        grid_spec=pltpu.PrefetchScalarGridSpec(
            num_scalar_prefetch=1, grid=(rows // tr,),
            in_specs=[pl.BlockSpec((N_CHIPS, None, tr, width), lambda i, c_ref: (0, c_ref[0], i, 0)),
                      pl.BlockSpec((N_CHIPS, tr, width), lambda i, c_ref: (0, i, 0))],
            out_specs=pl.BlockSpec((N_CHIPS, tr, width), lambda i, c_ref: (0, i, 0))),
        out_shape=jax.ShapeDtypeStruct((N_CHIPS, rows, width), BF16), compiler_params=_cparams("parallel"),
    )(c_arr, g, got)


def _chip_exchange_copies(refs, send_sems, recv_sems):
    n = len(refs) // 2
    x, y, c = _place()
    return [_remote(refs[i].at[2 * cx + cy], refs[n + i].at[j], send_sems, recv_sems, 3 * i + j, (cx, cy, c))
            for i in range(n) for j, (cx, cy) in enumerate(_other_chips(x, y))]


def _chip_exchange_start(refs, send_sems, recv_sems):
    for cp in _chip_exchange_copies(refs, send_sems, recv_sems):
        cp.start()


def _chip_exchange_finish(refs, send_sems, recv_sems):
    n = len(refs) // 2
    x, y, c = _place()
    for i in range(n):
        for j in range(3):
            _remote(refs[i].at[2 * x + y], refs[n + i].at[j], send_sems, recv_sems, 3 * i + j, (x, y, c)).wait_recv()
    for cp in _chip_exchange_copies(refs, send_sems, recv_sems):
        cp.wait_send()


def _chip_exchange_buffers(ps):
    return [lax.empty((3,) + p.shape[1:], p.dtype) for p in ps]


def _chip_exchange_rider(ps):
    return _Rider(list(ps) + _chip_exchange_buffers(ps), 3 * len(ps), _chip_exchange_start, _chip_exchange_finish)


def _rs_chip_exchange(ps, name):
    n = len(ps)

    def body(*refs):
        out_refs, (send_sems, recv_sems) = refs[2 * n:4 * n], refs[4 * n:]
        _chip_exchange_start(out_refs, send_sems, recv_sems)
        _chip_exchange_finish(out_refs, send_sems, recv_sems)

    arrays = list(ps) + _chip_exchange_buffers(ps)
    return pl.pallas_call(
        body, name=name, in_specs=[ANY] * (2 * n), out_specs=[ANY] * (2 * n),
        input_output_aliases={i: i for i in range(2 * n)},
        out_shape=[jax.ShapeDtypeStruct(a.shape, a.dtype) for a in arrays], scratch_shapes=_dma_sems(3 * n),
    )(*arrays)[n:]


def _rs_chip_add(p, q, idx, name):
    _, rows, width = q.shape
    tr = _rs_rows(rows, width)

    def body(idx_ref, p_ref, q_ref, r_ref):
        qv = q_ref[...].astype(F32)
        r_ref[...] = ((p_ref[...].astype(F32) + qv[0]) + qv[1]) + qv[2]

    return pl.pallas_call(
        body, name=name,
        grid_spec=pltpu.PrefetchScalarGridSpec(
            num_scalar_prefetch=1, grid=(rows // tr,),
            in_specs=[pl.BlockSpec((None, tr, width), lambda i, idx_ref: (idx_ref[0], i, 0)),
                      pl.BlockSpec((3, tr, width), lambda i, idx_ref: (0, i, 0))],
            out_specs=pl.BlockSpec((None, tr, width), lambda i, idx_ref: (idx_ref[1], i, 0))),
        out_shape=jax.ShapeDtypeStruct((2, rows, width), F32), compiler_params=_cparams("parallel"),
    )(idx, p, q)


def _rs_pair_share(rs, name):
    n = len(rs)

    def body(*refs):
        out_refs, (send_sems, recv_sems) = refs[n:2 * n], refs[2 * n:]
        x, y, c = _place()
        sends = [_remote(out_refs[i].at[c], out_refs[i].at[c], send_sems, recv_sems, i, (x, y, 1 - c)) for i in range(n)]
        for cp in sends:
            cp.start()
        for i in range(n):
            _remote(out_refs[i].at[c], out_refs[i].at[1 - c], send_sems, recv_sems, i, (x, y, c)).wait_recv()
        for cp in sends:
            cp.wait_send()

    return pl.pallas_call(
        body, name=name, in_specs=[ANY] * n, out_specs=[ANY] * n, input_output_aliases={i: i for i in range(n)},
        out_shape=[jax.ShapeDtypeStruct(r.shape, r.dtype) for r in rs], scratch_shapes=_dma_sems(n),
    )(*rs)


def _rs_pair_sums(gs, gots, core, tag):
    c_arr = core.astype(jnp.int32).reshape(1)
    return [_rs_pair_add(g, got, c_arr, f"{tag}_pair_add{i}") for i, (g, got) in enumerate(zip(gs, gots))]


def _rs_finish(ps, qs, chip, core, tag):
    idx = jnp.stack([chip, core]).astype(jnp.int32)
    rs = [_rs_chip_add(p, q, idx, f"{tag}_chip_add{i}") for i, (p, q) in enumerate(zip(ps, qs))]
    return _rs_pair_share(rs, tag + "_share")


def _sum8(a, name):
    _, rows, width = a.shape
    tr = 8 * (rows // 8 if rows <= 64 else 7)
    assert rows % tr == 0

    def body(a_ref, o_ref):
        acc = a_ref[0]
        for d in range(1, N_DEV):
            acc = acc + a_ref[d]
        o_ref[...] = acc

    return pl.pallas_call(
        body, name=name, grid=(rows // tr,), in_specs=[pl.BlockSpec((N_DEV, tr, width), lambda i: (0, i, 0))],
        out_specs=pl.BlockSpec((tr, width), lambda i: (i, 0)), out_shape=jax.ShapeDtypeStruct((rows, width), F32),
        compiler_params=_cparams("parallel"),
    )(a)


MOD_COLS = 3 * D // N_CHIPS
MOD_TK = 512


def _mod_partial(c_all, mod_w, name):
    nk = D // MOD_TK

    def body(c_ref, w_ref, o_ref):
        l = pl.program_id(1)
        part = _bdot_nn(jax.nn.silu(c_ref[...]), w_ref[0])
        _accum(o_ref.at[0], part, l == 0)

    return pl.pallas_call(
        body, name=name, grid=(DEPTH, nk),
        in_specs=[pl.BlockSpec((N_DEV, MOD_TK), lambda i, l: (0, l)), pl.BlockSpec((1, MOD_TK, MOD_COLS), lambda i, l: (i, l, 0))],
        out_specs=pl.BlockSpec((1, N_DEV, MOD_COLS), lambda i, l: (i, 0, 0)),
        out_shape=jax.ShapeDtypeStruct((DEPTH, N_DEV, MOD_COLS), F32), compiler_params=_cparams("parallel", "arbitrary"),
    )(c_all, mod_w)


def _mod_w_grad(c_all, dmod, name):
    def body(c_ref, d_ref, o_ref):
        o_ref[0] = _dg(jax.nn.silu(c_ref[...]).astype(BF16), d_ref[0].astype(BF16), _TN)

    return pl.pallas_call(
        body, name=name, grid=(DEPTH, D // MOD_TK),
        in_specs=[pl.BlockSpec((N_DEV, MOD_TK), lambda i, l: (0, l)), pl.BlockSpec((1, N_DEV, MOD_COLS), lambda i, l: (i, 0, 0))],
        out_specs=pl.BlockSpec((1, MOD_TK, MOD_COLS), lambda i, l: (i, l, 0)),
        out_shape=jax.ShapeDtypeStruct((DEPTH, D, MOD_COLS), F32), compiler_params=_cparams("parallel", "parallel"),
    )(c_all, dmod)


ADAM_BLOCK_BYTES = 1 << 20


def _adamw(w, g, m, v, name):
    shape = w.shape
    cols = shape[-1]
    rows = w.size // cols
    w, g, m, v = (a.reshape(rows, cols) for a in (w, g, m, v))
    cap = max(8, ADAM_BLOCK_BYTES // (4 * cols))
    tr = rows if rows <= cap else max(d for d in range(8, cap + 1, 8) if rows % d == 0)
    c1 = 1.0 - ADAM_B1 ** ADAM_STEP
    c2 = 1.0 - ADAM_B2 ** ADAM_STEP

    def body(w_ref, g_ref, m_ref, v_ref, d_ref, nm_ref, nv_ref):
        gv = g_ref[...]
        mn = ADAM_B1 * m_ref[...] + (1.0 - ADAM_B1) * gv
        vn = ADAM_B2 * v_ref[...] + (1.0 - ADAM_B2) * (gv * gv)
        nm_ref[...] = mn
        nv_ref[...] = vn
        d_ref[...] = -ADAM_LR * ((mn / c1) / (jnp.sqrt(vn / c2) + ADAM_EPS) + ADAM_WD * w_ref[...])

    spec = pl.BlockSpec((tr, cols), lambda i: (i, 0))
    out = jax.ShapeDtypeStruct((rows, cols), F32)
    d, nm, nv = pl.pallas_call(
        body, name=name, grid=(rows // tr,), in_specs=[spec] * 4, out_specs=[spec] * 3, out_shape=[out] * 3,
        compiler_params=_cparams("parallel"),
    )(w, g, m, v)
    return d.reshape(shape), nm.reshape(shape), nv.reshape(shape)


W_NAMES = ("norm_g", "mod_w", "mod_b", "final_norm_g", "sg_w_in", "sg_w_out", "sg_ln_g", "sg_ln_b", "sg_w_spatial",
           "sg_b_spatial", "swa_w_in", "swa_w_out", "swa_sinks", "rwkv_w_in", "rwkv_w_out", "rwkv_mu", "rwkv_w0",
           "rwkv_w_lora", "rwkv_a0", "rwkv_a_lora", "rwkv_k_k", "rwkv_k_a", "rwkv_r_k", "rwkv_gn_g", "rwkv_gn_b")
SMALL = {"sg_ln_g": 1, "sg_ln_b": 1, "rwkv_mu": 1, "rwkv_w0": 1, "rwkv_w_lora": 2, "rwkv_a0": 1, "rwkv_a_lora": 2,
         "rwkv_k_k": 1, "rwkv_k_a": 1, "rwkv_gn_g": 1, "rwkv_gn_b": 1}
REPLICATED = ("norm_g", "final_norm_g", "sg_w_spatial", "sg_b_spatial", "swa_sinks", "rwkv_r_k")
KINDS = ("sg", "swa", "rwkv", "sg")


def _pad_to(flat, n):
    return jnp.pad(flat, (0, n - flat.shape[0]))


def _round_up(n, m):
    return -(-n // m) * m


def _join_shards(gathered, axis):
    return jnp.concatenate([gathered[s] for s in range(N_CHIPS)], axis=axis)


def _chip_blocks(full, axis):
    return jnp.stack(jnp.split(full, N_CHIPS, axis=axis)).reshape(N_CHIPS, -1)


def _weight_buffer(w):
    rows, cols = w.shape
    return _gather_buffer(w.astype(BF16).reshape(2, rows // 2, cols))


def _chip_shards(buf):
    return buf.reshape(N_CHIPS, -1, buf.shape[-1])


def _small_buffer(shards):
    flat = jnp.concatenate([shards[n].reshape(-1) for n in SMALL])
    rows = _round_up(flat.shape[0], 2 * 8 * LANES) // (2 * LANES)
    return _gather_buffer(_pad_to(flat, 2 * rows * LANES).reshape(2, rows, LANES))


def _unpack_small(buf, shards):
    got = buf.reshape(N_CHIPS, -1)
    out, off = {}, 0
    for n, axis in SMALL.items():
        size = shards[n].size
        out[n] = _join_shards(got[:, off:off + size].reshape((N_CHIPS,) + shards[n].shape), axis)
        off += size
    return out


def _lora_pad_rows(w):
    return jnp.pad(w, ((0, LORA_PAD - LORA), (0, 0)))


def _lo_cols(a):
    z = jnp.zeros(a.shape[:-1] + (LORA_PAD - LORA,), a.dtype)
    return jnp.concatenate([a[..., :LORA], z, a[..., LORA:], z], axis=-1)


def _lo_cols_inv(a):
    return jnp.concatenate([a[..., :LORA], a[..., LORA_PAD:LORA_PAD + LORA]], axis=-1)


def _rows_dim_major(w):
    return w.reshape(N_HEADS, HEAD, -1).swapaxes(0, 1).reshape(w.shape)


def _rows_head_major(w):
    return w.reshape(HEAD, N_HEADS, -1).swapaxes(0, 1).reshape(w.shape)


def kernel(x, c, positions, norm_g, mod_w, mod_b, final_norm_g, sg_w_in, sg_w_out, sg_ln_g, sg_ln_b, sg_w_spatial,
           sg_b_spatial, swa_w_in, swa_w_out, swa_sinks, rwkv_w_in, rwkv_w_out, rwkv_mu, rwkv_w0, rwkv_w_lora, rwkv_a0,
           rwkv_a_lora, rwkv_k_k, rwkv_k_a, rwkv_r_k, rwkv_gn_g, rwkv_gn_b, loss_target, m_norm_g, m_mod_w, m_mod_b,
           m_final_norm_g, m_sg_w_in, m_sg_w_out, m_sg_ln_g, m_sg_ln_b, m_sg_w_spatial, m_sg_b_spatial, m_swa_w_in,
           m_swa_w_out, m_swa_sinks, m_rwkv_w_in, m_rwkv_w_out, m_rwkv_mu, m_rwkv_w0, m_rwkv_w_lora, m_rwkv_a0,
           m_rwkv_a_lora, m_rwkv_k_k, m_rwkv_k_a, m_rwkv_r_k, m_rwkv_gn_g, m_rwkv_gn_b, v_norm_g, v_mod_w, v_mod_b,
           v_final_norm_g, v_sg_w_in, v_sg_w_out, v_sg_ln_g, v_sg_ln_b, v_sg_w_spatial, v_sg_b_spatial, v_swa_w_in,
           v_swa_w_out, v_swa_sinks, v_rwkv_w_in, v_rwkv_w_out, v_rwkv_mu, v_rwkv_w0, v_rwkv_w_lora, v_rwkv_a0,
           v_rwkv_a_lora, v_rwkv_k_k, v_rwkv_k_a, v_rwkv_r_k, v_rwkv_gn_g, v_rwkv_gn_b):
    given = dict(locals())
    w = {n: given[n] for n in W_NAMES}
    xi, yi, ci = _place()
    chip = 2 * xi + yi
    me = 4 * xi + 2 * yi + ci
    xs = [x[0]]

    c_all = _all_gather8(c, "gather_c")[:, 0, :]
    mod_part = _mod_partial(c_all, mod_w, "mod_fwd")
    mod_all = _all_gather8(mod_part, "gather_mod")[::2]
    mod_mine = lax.dynamic_index_in_dim(mod_all, me, axis=2, keepdims=False)
    mod = mod_mine.transpose(1, 0, 2).reshape(DEPTH, 3 * D) + mod_b
    shift, scale, gate = mod[:, :D], mod[:, D:2 * D], mod[:, 2 * D:]

    shards = {"sg_w_in0": sg_w_in[0], "sg_w_out0": sg_w_out[0], "swa_w_in": swa_w_in[0], "swa_w_out": swa_w_out[0],
              "rwkv_w_in": rwkv_w_in[0], "rwkv_w_out": rwkv_w_out[0], "sg_w_in1": sg_w_in[1], "sg_w_out1": sg_w_out[1]}
    bufs = {n: _weight_buffer(s) for n, s in shards.items()}
    fwd_riders = {(0, "in"): ["swa_w_in"], (0, "mix"): ["swa_w_out"], (1, "in"): ["rwkv_w_out"], (1, "mix"): ["rwkv_w_in"],
                  (2, "mix"): ["sg_w_in1", "sg_w_out1"]}
    bufs["sg_w_in0"], bufs["sg_w_out0"], small_buf = _chip_gather(
        [bufs["sg_w_in0"], bufs["sg_w_out0"], _small_buffer(w)], "gather_l0")
    full = _unpack_small(small_buf, w)

    def riding(i, where):
        names = fwd_riders.get((i, where))
        return names, (None if names is None else _gather_rider([bufs[n] for n in names]))

    def arrived(names, ridden):
        for n, b in zip(names or [], ridden):
            bufs[n] = b

    sg_in = lambda j: _chip_shards(bufs[f"sg_w_in{j}"])
    sg_out = lambda j: bufs[f"sg_w_out{j}"].reshape(D, D)
    mu = full["rwkv_mu"][0]
    rw_prm = dict(mu_main=_dim_major(mu[:RW_MAIN].reshape(4, D)).reshape(1, RW_MAIN), mu_lo=_lo_cols(mu[None, RW_MAIN:]),
                  w0=_dim_major(full["rwkv_w0"]), a0=_dim_major(full["rwkv_a0"]),
                  wl=_lora_pad_rows(_dim_major(full["rwkv_w_lora"][0])), al=_lora_pad_rows(_dim_major(full["rwkv_a_lora"][0])),
                  kkp=_param_compact(full["rwkv_k_k"][0]), kap=_param_compact(full["rwkv_k_a"][0]),
                  rkp=_param_compact(rwkv_r_k.reshape(-1)),
                  gn_g=_param_compact(full["rwkv_gn_g"][0]), gn_b=_param_compact(full["rwkv_gn_b"][0]))
    bs_t = [jnp.pad(sg_b_spatial[j].T, ((0, 0), (0, LANES - SG_GROUPS))) for j in range(2)]
    sink_row = jnp.pad(swa_sinks, ((0, 0), (0, LANES - N_HEADS)))
    inv_freq = ROPE_THETA ** (-jnp.arange(HEAD // 2, dtype=F32) / (HEAD // 2))
    ang = positions[0].astype(F32)[:, None] * inv_freq
    cos, sin = jnp.tile(jnp.cos(ang), (1, LANES * 2 // HEAD)), jnp.tile(jnp.sin(ang), (1, LANES * 2 // HEAD))

    def row(a, i):
        return a[i:i + 1]

    hs, ps, us, ys, rw_saved = [], [], [], [], None
    for i, kind in enumerate(KINDS):
        j = i // 3
        tag = f"l{i}_{kind}"
        h = _norm_mod_fwd(xs[i], row(norm_g, i), row(shift, i), row(scale, i), tag + "_norm")
        names_in, rider_in = riding(i, "in")
        names_mix, rider_mix = riding(i, "mix")
        if kind == "sg":
            p, ridden = _ridden(_matmul(h, sg_in(j), "nn", tag + "_in", blocked=True, rider=rider_in), rider_in)
            arrived(names_in, ridden)
            u, ridden = _ridden(_sg_fwd(p, row(full["sg_ln_g"], j), row(full["sg_ln_b"], j), sg_w_spatial[j], bs_t[j],
                                        tag + "_mix", rider_mix), rider_mix)
            w_out = sg_out(j)
        elif kind == "swa":
            swa_in, swa_out = _chip_shards(bufs["swa_w_in"]), bufs["swa_w_out"].reshape(D, D)
            p, ridden = _ridden(_matmul(h, swa_in, "nn", tag + "_in", blocked=True, rider=rider_in), rider_in)
            arrived(names_in, ridden)
            u, ridden = _ridden(_swa_fwd(p, cos, sin, sink_row, tag + "_mix", rider_mix), rider_mix)
            w_out = swa_out
        else:
            rw_in = _join_shards(_chip_shards(bufs["rwkv_w_in"]), axis=1)
            rw_main = _dim_major(rw_in[:, :RW_MAIN].reshape(D, 4, D)).reshape(D, RW_MAIN)
            rw_lo = _lo_cols(rw_in[:, RW_MAIN:])
            rw_out = _rows_dim_major(bufs["rwkv_w_out"].reshape(D, D))
            p = (_matmul(h, rw_main, "nn", tag + "_in"), _matmul(h, rw_lo, "nn", tag + "_in_lo"))
            u, rw_saved, ridden = _rwkv_mixer_fwd(p[0], p[1], rw_prm, tag, rider_mix)
            w_out = rw_out
        arrived(names_mix, ridden)
        y = _matmul(u, w_out, "nn", tag + "_out")
        xs.append(_resid_gate(xs[i], y, row(gate, i), tag + "_resid"))
        hs.append(h), ps.append(p), us.append(u), ys.append(y)

    loss_part, dx, d_final_g = _final_loss_grad(xs[DEPTH], final_norm_g[None], loss_target[0], "loss")
    loss = lax.psum(loss_part[0, 0], ("x", "y", "c"))

    gfull = {n: [None, None] for n in ("sg_ln_g", "sg_ln_b", "sg_w_spatial", "sg_b_spatial")}
    gbig = {}
    d_norm_g, d_mod = [None] * DEPTH, [None] * DEPTH
    rs_p, rs_q, riding_names = {}, {}, []

    for i in reversed(range(DEPTH)):
        kind, j = KINDS[i], i // 3
        tag = f"l{i}_{kind}_b"
        rider = _chip_exchange_rider([rs_p[n] for n in riding_names]) if riding_names else None
        dy, d_gate = _gate_bwd(dx, ys[i], row(gate, i), tag + "_gate")
        w_out = {"sg": sg_out(j), "swa": swa_out, "rwkv": rw_out}[kind]
        du = _matmul(dy, w_out, "nt", tag + "_du")
        dw_out = _matmul(us[i], dy, "tn", tag + "_dwout").reshape(N_CHIPS, D // N_CHIPS, D)
        if kind == "sg":
            (dp, dlg, dlb, dws, dbs), ridden = _ridden(
                _sg_bwd(ps[i], row(full["sg_ln_g"], j), row(full["sg_ln_b"], j), sg_w_spatial[j], bs_t[j], du,
                        tag + "_mix", rider), rider)
            gfull["sg_ln_g"][j], gfull["sg_ln_b"][j] = dlg[0], dlb[0]
            gfull["sg_w_spatial"][j], gfull["sg_b_spatial"][j] = dws, dbs[:, :SG_GROUPS].T
            gbig[f"sg_w_in{j}"] = _matmul(hs[i], dp, "tn", tag + "_dwin", blocked=True)
            gbig[f"sg_w_out{j}"] = dw_out
            dh, dh2 = _matmul(dp, sg_in(j), "nt", tag + "_dh", blocked=True), None
            mine = [f"sg_w_in{j}", f"sg_w_out{j}"]
        elif kind == "swa":
            (dp, dsk), ridden = _ridden(_swa_bwd(ps[i], cos, sin, sink_row, du, tag + "_mix", rider), rider)
            gfull["swa_sinks"] = dsk[:, :N_HEADS]
            gbig["swa_w_in"] = _matmul(hs[i], dp, "tn", tag + "_dwin", blocked=True)
            gbig["swa_w_out"] = dw_out
            dh, dh2 = _matmul(dp, swa_in, "nt", tag + "_dh", blocked=True), None
            mine = ["swa_w_in", "swa_w_out"]
        else:
            dpm, dpl, rg, ridden = _rwkv_mixer_bwd(ps[i][0], ps[i][1], rw_prm, rw_saved, du, tag, rider)
            mine = ["rwkv_w_in", "rwkv_w_out"]
            dw_main = _matmul(hs[i], dpm, "tn", tag + "_dwin")
            dw_lo = _matmul(hs[i], dpl, "tn", tag + "_dwin_lo")
            dw_main = _head_major(dw_main.reshape(D, 4, D)).reshape(D, RW_MAIN)
            dw_in = jnp.concatenate([dw_main, _lo_cols_inv(dw_lo)], axis=1)
            gbig["rwkv_w_in"] = dw_in.reshape(D, N_CHIPS, -1).transpose(1, 0, 2)
            gbig["rwkv_w_out"] = _rows_head_major(dw_out.reshape(D, D)).reshape(dw_out.shape)
            dmu_main = _head_major(rg["mu_main"].reshape(4, D)).reshape(1, RW_MAIN)
            gfull["rwkv_mu"] = jnp.concatenate([dmu_main, _lo_cols_inv(rg["mu_lo"])], axis=1)
            gfull["rwkv_w0"], gfull["rwkv_a0"] = _head_major(rg["w0"]), _head_major(rg["a0"])
            gfull["rwkv_w_lora"], gfull["rwkv_a_lora"] = _head_major(rg["wl"])[None, :LORA], _head_major(rg["al"])[None, :LORA]
            gfull["rwkv_k_k"], gfull["rwkv_k_a"] = _param_compact_inv(rg["kkp"])[None], _param_compact_inv(rg["kap"])[None]
            gfull["rwkv_r_k"] = _param_compact_inv(rg["rkp"]).reshape(1, N_HEADS, HEAD)
            gfull["rwkv_gn_g"], gfull["rwkv_gn_b"] = _param_compact_inv(rg["gn_g"])[None], _param_compact_inv(rg["gn_b"])[None]
            dh, dh2 = _matmul(dpm, rw_main, "nt", tag + "_dh"), _matmul(dpl, rw_lo, "nt", tag + "_dh_lo")
        rs_q.update(zip(riding_names, ridden[len(riding_names):]))
        if i == 0:
            for n in ("sg_ln_g", "sg_ln_b"):
                gfull[n] = jnp.stack(gfull[n])
            small = jnp.concatenate([_chip_blocks(gfull[n], axis) for n, axis in SMALL.items()], axis=1)
            small_rows = _round_up(small.shape[1], 2 * 16 * LANES) // LANES
            small = jnp.pad(small, ((0, 0), (0, small_rows * LANES - small.shape[1])))
            gbig["small"] = small.reshape(N_CHIPS, small_rows, LANES)
            mine = mine + ["small"]
        gs = [gbig[n].reshape(N_CHIPS, 2, gbig[n].shape[1] // 2, gbig[n].shape[2]) for n in mine]
        pair_rider = _pair_exchange_rider(gs)
        (dx, dg, dsh, dsc), gots = _norm_mod_bwd(xs[i], row(norm_g, i), row(shift, i), row(scale, i), dh, dx, tag + "_norm",
                                                 dh2, pair_rider)
        d_norm_g[i] = dg[0]
        d_mod[i] = jnp.concatenate([dsh[0], dsc[0], d_gate[0]])
        rs_p.update(zip(mine, _rs_pair_sums(gs, gots[len(gs):], ci, f"rs{i}")))
        riding_names = mine
    for n in ("sg_w_spatial", "sg_b_spatial"):
        gfull[n] = jnp.stack(gfull[n])
    gfull["norm_g"], gfull["final_norm_g"] = jnp.stack(d_norm_g), d_final_g[0]

    rs_q.update(zip(riding_names, _rs_chip_exchange([rs_p[n] for n in riding_names], "rs0_chip_x")))
    rs_names = sorted(rs_p)
    rs_out = _rs_finish([rs_p[n] for n in rs_names], [rs_q[n] for n in rs_names], chip, ci, "rs")
    red = {n: r.reshape(-1, r.shape[2]) for n, r in zip(rs_names, rs_out)}
    grads = {"sg_w_in": jnp.stack([red["sg_w_in0"], red["sg_w_in1"]]), "sg_w_out": jnp.stack([red["sg_w_out0"], red["sg_w_out1"]])}
    for n in ("swa_w_in", "swa_w_out", "rwkv_w_in", "rwkv_w_out"):
        grads[n] = red[n][None]
    small_red, off = red["small"].reshape(-1), 0
    for n in SMALL:
        grads[n] = small_red[off:off + w[n].size].reshape(w[n].shape)
        off += w[n].size

    rep_flat = jnp.concatenate([jnp.stack(d_mod).reshape(-1)] + [gfull[n].reshape(-1) for n in REPLICATED])
    rep_rows = _round_up(rep_flat.shape[0], 56 * RS_W) // RS_W
    rep_all = _all_gather8(_pad_to(rep_flat, rep_rows * RS_W).reshape(rep_rows, RS_W), "gather_rep")
    rep_sum = _sum8(rep_all, "sum_rep").reshape(-1)
    grads["mod_b"] = rep_sum[:DEPTH * 3 * D].reshape(DEPTH, 3 * D)
    off = DEPTH * 3 * D
    for n in REPLICATED:
        grads[n] = rep_sum[off:off + w[n].size].reshape(w[n].shape)
        off += w[n].size
    dmod_all = rep_all.reshape(N_DEV, -1)[:, :DEPTH * 3 * D].reshape(N_DEV, DEPTH, 3 * D)
    dmod_cols = lax.dynamic_slice_in_dim(dmod_all, chip * MOD_COLS, MOD_COLS, axis=2).transpose(1, 0, 2)
    grads["mod_w"] = _mod_w_grad(c_all, dmod_cols, "mod_w_grad")

    deltas, new_m, new_v = {}, {}, {}
    for n in W_NAMES:
        deltas[n], new_m[n], new_v[n] = _adamw(w[n], grads[n], given["m_" + n], given["v_" + n], "adamw_" + n)
    return (loss, dx[None], *[grads[n] for n in W_NAMES], *[deltas[n] for n in W_NAMES],
            *[new_m[n] for n in W_NAMES], *[new_v[n] for n in W_NAMES])
```

```python
import functools
import math

import jax
import jax.numpy as jnp
from jax import lax
from jax.experimental import pallas as pl
from jax.experimental.pallas import tpu as pltpu

F32 = jnp.float32
BF16 = jnp.bfloat16
HIGHEST = lax.Precision.HIGHEST

D = 2048
DEPTH = 4
CHUNK = 128
SG_GROUPS = 16
HEAD = 64
N_HEADS = D // HEAD
KV_HEADS = 4
KVW = KV_HEADS * HEAD
ROPE_THETA = 10000.0
LORA = 96
LORA_PAD = 128
DECAY_SCALE = math.exp(-0.5)
GN_EPS = 64e-5
RMS_EPS = 1e-6
LN_EPS = 1e-5
ADAM_LR, ADAM_B1, ADAM_B2, ADAM_EPS, ADAM_WD, ADAM_STEP = 0.001, 0.9, 0.999, 1e-08, 0.01, 10
LANES = 128
SUB = 8
NEG = -1e30
VMEM_LIMIT = 56 * 1024 * 1024

MESHT = pl.DeviceIdType.MESH


def _cparams(*sem):
    return pltpu.CompilerParams(dimension_semantics=sem, vmem_limit_bytes=VMEM_LIMIT)


class _Rider:
    def __init__(self, arrays, n_sems, start, finish):
        self.arrays, self.n_sems, self.start, self.finish = list(arrays), n_sems, start, finish


def _ridden(res, rider):
    return (res, []) if rider is None else res


def _compute_call(body, args, *, name, grid, in_specs, out_specs, out_shape, semantics, scratch_shapes=(), rider=None):
    if rider is None:
        return pl.pallas_call(body, name=name, grid=grid, in_specs=in_specs, out_specs=out_specs, out_shape=out_shape,
                              scratch_shapes=list(scratch_shapes), compiler_params=_cparams(*semantics))(*args)
    single = not isinstance(out_shape, (list, tuple))
    o_specs, o_shapes = ([out_specs], [out_shape]) if single else (list(out_specs), list(out_shape))
    n_in, n_out, n_r, n_scr = len(in_specs), len(o_specs), len(rider.arrays), len(scratch_shapes)

    def with_rider(*refs):
        ins, outs = refs[:n_in], refs[n_in + n_r:n_in + n_r + n_out]
        ridden = refs[n_in + n_r + n_out:n_in + 2 * n_r + n_out]
        scratch, (send_sems, recv_sems) = refs[n_in + 2 * n_r + n_out:-2], refs[-2:]
        ids = [pl.program_id(d) for d in range(len(grid))]
        first = functools.reduce(jnp.logical_and, [i == 0 for i in ids])
        last = functools.reduce(jnp.logical_and, [i == g - 1 for i, g in zip(ids, grid)])

        @pl.when(first)
        def _():
            rider.start(ridden, send_sems, recv_sems)

        body(*ins, *outs, *scratch)

        @pl.when(last)
        def _():
            rider.finish(ridden, send_sems, recv_sems)

    any_spec = pl.BlockSpec(memory_space=pl.ANY)
    res = pl.pallas_call(
        with_rider, name=name, grid=grid, in_specs=list(in_specs) + [any_spec] * n_r, out_specs=o_specs + [any_spec] * n_r,
        out_shape=o_shapes + [jax.ShapeDtypeStruct(a.shape, a.dtype) for a in rider.arrays],
        input_output_aliases={n_in + i: n_out + i for i in range(n_r)},
        scratch_shapes=list(scratch_shapes) + [pltpu.SemaphoreType.DMA((rider.n_sems,))] * 2,
        compiler_params=_cparams(*["arbitrary"] * len(grid)),
    )(*args, *rider.arrays)
    return (res[0] if single else list(res[:n_out])), list(res[n_out:])


_NN = (((1,), (0,)), ((), ()))
_NT = (((1,), (1,)), ((), ()))
_TN = (((0,), (0,)), ((), ()))


def _dg(a, b, dims):
    return lax.dot_general(a, b, dims, preferred_element_type=F32)


@jax.custom_vjp
def _bdot_nn(a, b):
    return _dg(a.astype(BF16), b.astype(BF16), _NN)


def _bdot_nn_fwd(a, b):
    a, b = a.astype(BF16), b.astype(BF16)
    return _dg(a, b, _NN), (a, b)


def _bdot_nn_bwd(res, ct):
    a, b = res
    ct = ct.astype(BF16)
    return _dg(ct, b, _NT), _dg(a, ct, _TN)


_bdot_nn.defvjp(_bdot_nn_fwd, _bdot_nn_bwd)


@jax.custom_vjp
def _bdot_nt(a, b):
    return _dg(a.astype(BF16), b.astype(BF16), _NT)


def _bdot_nt_fwd(a, b):
    a, b = a.astype(BF16), b.astype(BF16)
    return _dg(a, b, _NT), (a, b)


def _bdot_nt_bwd(res, ct):
    a, b = res
    ct = ct.astype(BF16)
    return _dg(ct, b, _NN), _dg(ct, a, _TN)


_bdot_nt.defvjp(_bdot_nt_fwd, _bdot_nt_bwd)


def _tile(n, cap):
    if n <= cap:
        return n
    return max(d for d in range(LANES, cap + 1, LANES) if n % d == 0)


def _matmul(a, b, form, name, out_dtype=F32, blocked=False, rider=None, tm=1024, tn=512, tk=4096):
    if form == "nn":
        (m, k), n = a.shape, (N_CHIPS * b.shape[2] if blocked else b.shape[1])
    elif form == "nt":
        m, k, n = a.shape[0], a.shape[1], (b.shape[1] if blocked else b.shape[0])
    else:
        (k, m), n = a.shape, b.shape[1]
    per_chip = (k if form == "nt" else n) // N_CHIPS
    if blocked and form == "nt":
        tk = _tile(per_chip, tk)
    elif blocked:
        tn = _tile(per_chip, tn)
    tm, tn, tk = _tile(m, tm), _tile(n, tn), _tile(k, tk)
    assert m % tm == 0 and n % tn == 0 and k % tk == 0, (name, a.shape, b.shape)
    nk = k // tk
    dims = {"nn": _NN, "nt": _NT, "tn": _TN}[form]
    a_spec = pl.BlockSpec((tk, tm), lambda i, j, l: (l, i)) if form == "tn" else pl.BlockSpec((tm, tk), lambda i, j, l: (i, l))
    b_spec = pl.BlockSpec((tn, tk), lambda i, j, l: (j, l)) if form == "nt" else pl.BlockSpec((tk, tn), lambda i, j, l: (l, j))
    o_spec = pl.BlockSpec((tm, tn), lambda i, j, l: (i, j))
    o_shape = (m, n)
    if blocked and form == "nn":
        pc = per_chip // tn
        b_spec = pl.BlockSpec((None, tk, tn), lambda i, j, l: (j // pc, l, j % pc))
    elif blocked and form == "nt":
        pc = per_chip // tk
        b_spec = pl.BlockSpec((None, tn, tk), lambda i, j, l: (l // pc, j, l % pc))
    elif blocked:
        pc = per_chip // tn
        o_spec = pl.BlockSpec((None, tm, tn), lambda i, j, l: (j // pc, i, j % pc))
        o_shape = (N_CHIPS, m, per_chip)

    def body(a_ref, b_ref, o_ref, acc_ref):
        part = _dg(a_ref[...], b_ref[...], dims)
        if nk == 1:
            o_ref[...] = part.astype(out_dtype)
        else:
            l = pl.program_id(2)

            @pl.when(l == 0)
            def _():
                acc_ref[...] = part

            @pl.when(l > 0)
            def _():
                acc_ref[...] += part

            @pl.when(l == nk - 1)
            def _():
                o_ref[...] = acc_ref[...].astype(out_dtype)

    return _compute_call(
        body, (a, b), name=name, grid=(m // tm, n // tn, nk),
        in_specs=[a_spec, b_spec], out_specs=o_spec, out_shape=jax.ShapeDtypeStruct(o_shape, out_dtype),
        scratch_shapes=[pltpu.VMEM((tm, tn) if nk > 1 else (8, LANES), F32)],
        semantics=("parallel", "parallel", "arbitrary"), rider=rider)


def _out_proj_resid(u, w_out, x, gate, name, tm=1024, tn=512):
    (m, k), n = u.shape, w_out.shape[1]

    def body(u_ref, w_ref, x_ref, g_ref, y_ref, xn_ref):
        y = _dg(u_ref[...], w_ref[...], _NN)
        y_ref[...] = y
        xn_ref[...] = x_ref[...] + g_ref[...] * y

    tile = pl.BlockSpec((tm, tn), lambda i, j: (i, j))
    out = jax.ShapeDtypeStruct((m, n), F32)
    return pl.pallas_call(
        body, name=name, grid=(m // tm, n // tn),
        in_specs=[pl.BlockSpec((tm, k), lambda i, j: (i, 0)), pl.BlockSpec((k, tn), lambda i, j: (0, j)), tile,
                  pl.BlockSpec((1, tn), lambda i, j: (0, j))],
        out_specs=[tile, tile], out_shape=[out, out], compiler_params=_cparams("parallel", "parallel"),
    )(u, w_out, x, gate)


TB_NORM = 256


def _f_norm_mod(x, g, shift, scale):
    xn = x * lax.rsqrt(jnp.mean(x * x, axis=-1, keepdims=True) + RMS_EPS)
    return (xn * g) * (1.0 + scale) + shift


def _row_spec(width, tb=TB_NORM):
    return pl.BlockSpec((tb, width), lambda i: (i, 0))


def _vec_spec(width, rows=1):
    return pl.BlockSpec((rows, width), lambda i: (0, 0))


def _norm_mod_fwd(x, g, shift, scale, name):
    t = x.shape[0]

    def body(x_ref, g_ref, sh_ref, sc_ref, h_ref):
        h_ref[...] = _f_norm_mod(x_ref[...], g_ref[...], sh_ref[...], sc_ref[...]).astype(BF16)

    return pl.pallas_call(
        body, name=name, grid=(t // TB_NORM,),
        in_specs=[_row_spec(D), _vec_spec(D), _vec_spec(D), _vec_spec(D)], out_specs=_row_spec(D),
        out_shape=jax.ShapeDtypeStruct((t, D), BF16), compiler_params=_cparams("parallel"),
    )(x, g, shift, scale)


def _accum(ref, val, first):
    @pl.when(first)
    def _():
        ref[...] = val

    @pl.when(jnp.logical_not(first))
    def _():
        ref[...] += val


def _norm_mod_bwd(x, g, shift, scale, dh, dx_res, name, dh2=None, rider=None):
    t = x.shape[0]
    dhs = [dh] if dh2 is None else [dh, dh2]

    def body(x_ref, g_ref, sh_ref, sc_ref, dr_ref, *refs):
        dh_refs, (dx_ref, dg_ref, dsh_ref, dsc_ref) = refs[:len(dhs)], refs[len(dhs):]
        _, vjp = jax.vjp(_f_norm_mod, x_ref[...], g_ref[...], sh_ref[...], sc_ref[...])
        dh_all = dh_refs[0][...]
        for r in dh_refs[1:]:
            dh_all = dh_all + r[...]
        dx, dg, dsh, dsc = vjp(dh_all)
        dx_ref[...] = dx + dr_ref[...]
        first = pl.program_id(0) == 0
        _accum(dg_ref, dg, first)
        _accum(dsh_ref, dsh, first)
        _accum(dsc_ref, dsc, first)

    vec = jax.ShapeDtypeStruct((1, D), F32)
    return _compute_call(
        body, (x, g, shift, scale, dx_res, *dhs), name=name, grid=(t // TB_NORM,),
        in_specs=[_row_spec(D), _vec_spec(D), _vec_spec(D), _vec_spec(D), _row_spec(D)] + [_row_spec(D)] * len(dhs),
        out_specs=[_row_spec(D), _vec_spec(D), _vec_spec(D), _vec_spec(D)],
        out_shape=[jax.ShapeDtypeStruct((t, D), F32), vec, vec, vec], semantics=("arbitrary",), rider=rider)


def _gate_bwd(dx, y, gate, name):
    t = dx.shape[0]

    def body(dx_ref, y_ref, g_ref, dy_ref, dg_ref):
        dxv = dx_ref[...]
        dy_ref[...] = (dxv * g_ref[...]).astype(BF16)
        _accum(dg_ref, jnp.sum(dxv * y_ref[...], axis=0, keepdims=True), pl.program_id(0) == 0)

    return pl.pallas_call(
        body, name=name, grid=(t // TB_NORM,),
        in_specs=[_row_spec(D), _row_spec(D), _vec_spec(D)], out_specs=[_row_spec(D), _vec_spec(D)],
        out_shape=[jax.ShapeDtypeStruct((t, D), BF16), jax.ShapeDtypeStruct((1, D), F32)],
        compiler_params=_cparams("arbitrary"),
    )(dx, y, gate)


def _f_final(x, g, target):
    xn = x * lax.rsqrt(jnp.mean(x * x, axis=-1, keepdims=True) + RMS_EPS)
    err = xn * g - target
    return 0.5 * jnp.sum(jnp.mean(err * err, axis=-1, keepdims=True), axis=0, keepdims=True)


def _final_loss_grad(x, g, target, name):
    t = x.shape[0]

    def body(x_ref, g_ref, t_ref, loss_ref, dx_ref, dg_ref):
        loss, vjp = jax.vjp(_f_final, x_ref[...], g_ref[...], t_ref[...])
        dx, dg, _ = vjp(jnp.ones((1, 1), F32))
        dx_ref[...] = dx
        first = pl.program_id(0) == 0
        _accum(dg_ref, dg, first)
        _accum(loss_ref, jnp.broadcast_to(loss, (1, LANES)), first)

    return pl.pallas_call(
        body, name=name, grid=(t // TB_NORM,),
        in_specs=[_row_spec(D), _vec_spec(D), _row_spec(D)],
        out_specs=[_vec_spec(LANES), _row_spec(D), _vec_spec(D)],
        out_shape=[jax.ShapeDtypeStruct((1, LANES), F32), jax.ShapeDtypeStruct((t, D), F32), jax.ShapeDtypeStruct((1, D), F32)],
        compiler_params=_cparams("arbitrary"),
    )(x, g, target)


def _group_selector():
    gi = lax.broadcasted_iota(jnp.int32, (LANES, D), 0)
    ci = lax.broadcasted_iota(jnp.int32, (LANES, D), 1)
    return (ci // (D // SG_GROUPS) == gi).astype(F32)


def _f_sg(p, ln_g, ln_b, w_s, bs_t):
    u, v, z = p[:, :D], p[:, D:2 * D], p[:, 2 * D:]
    u = jax.nn.gelu(u)
    vf = jax.nn.gelu(v)
    mean = jnp.mean(vf, axis=-1, keepdims=True)
    var = jnp.mean(jnp.square(vf - mean), axis=-1, keepdims=True)
    vn = (vf - mean) * lax.rsqrt(var + LN_EPS) * ln_g + ln_b
    ti = lax.broadcasted_iota(jnp.int32, (CHUNK, CHUNK), 0)
    si = lax.broadcasted_iota(jnp.int32, (CHUNK, CHUNK), 1)
    causal = si <= ti
    cg = D // SG_GROUPS
    f = jnp.concatenate(
        [_bdot_nn(jnp.where(causal, w_s[g], 0.0), vn[:, g * cg:(g + 1) * cg]) for g in range(SG_GROUPS)], axis=1)
    f = f + jnp.dot(bs_t, _group_selector(), precision=HIGHEST, preferred_element_type=F32)
    return u * f * jax.nn.silu(z)


def _sg_specs():
    return [pl.BlockSpec((CHUNK, 3 * D), lambda i: (i, 0)), _vec_spec(D), _vec_spec(D),
            pl.BlockSpec((SG_GROUPS, CHUNK, CHUNK), lambda i: (0, 0, 0)), _vec_spec(LANES, CHUNK)]


def _sg_fwd(p, ln_g, ln_b, w_s, bs_t, name, rider=None):
    t = p.shape[0]

    def body(p_ref, lg_ref, lb_ref, w_ref, b_ref, o_ref):
        o_ref[...] = _f_sg(p_ref[...], lg_ref[...], lb_ref[...], w_ref[...], b_ref[...]).astype(BF16)

    return _compute_call(
        body, (p, ln_g, ln_b, w_s, bs_t), name=name, grid=(t // CHUNK,), in_specs=_sg_specs(),
        out_specs=_row_spec(D, CHUNK), out_shape=jax.ShapeDtypeStruct((t, D), BF16), semantics=("parallel",), rider=rider)


def _sg_bwd(p, ln_g, ln_b, w_s, bs_t, dout, name, rider=None):
    t = p.shape[0]

    def body(p_ref, lg_ref, lb_ref, w_ref, b_ref, do_ref, dp_ref, dlg_ref, dlb_ref, dw_ref, db_ref):
        _, vjp = jax.vjp(_f_sg, p_ref[...], lg_ref[...], lb_ref[...], w_ref[...], b_ref[...])
        dp, dlg, dlb, dw, db = vjp(do_ref[...])
        dp_ref[...] = dp.astype(BF16)
        first = pl.program_id(0) == 0
        _accum(dlg_ref, dlg, first)
        _accum(dlb_ref, dlb, first)
        _accum(dw_ref, dw, first)
        _accum(db_ref, db, first)

    vec = jax.ShapeDtypeStruct((1, D), F32)
    return _compute_call(
        body, (p, ln_g, ln_b, w_s, bs_t, dout), name=name, grid=(t // CHUNK,), in_specs=_sg_specs() + [_row_spec(D, CHUNK)],
        out_specs=[pl.BlockSpec((CHUNK, 3 * D), lambda i: (i, 0)), _vec_spec(D), _vec_spec(D),
                   pl.BlockSpec((SG_GROUPS, CHUNK, CHUNK), lambda i: (0, 0, 0)), _vec_spec(LANES, CHUNK)],
        out_shape=[jax.ShapeDtypeStruct((t, 3 * D), BF16), vec, vec,
                   jax.ShapeDtypeStruct((SG_GROUPS, CHUNK, CHUNK), F32), jax.ShapeDtypeStruct((CHUNK, LANES), F32)],
        semantics=("arbitrary",), rider=rider)


SWA_COLS = 2 * D + 2 * KVW
KV_BLOCK = 2 * KVW


def _lane_roll(x, shift):
    return pltpu.roll(x, shift, 1)


def _rot_half(x):
    w = x.shape[1]
    lane = lax.broadcasted_iota(jnp.int32, x.shape, 1)
    return jnp.where(lane % HEAD < HEAD // 2, -_lane_roll(x, w - HEAD // 2), _lane_roll(x, HEAD // 2))


@jax.custom_vjp
def _rope(x, cos, sin):
    return x * cos + _rot_half(x) * sin


def _rope_fwd(x, cos, sin):
    return _rope(x, cos, sin), (cos, sin)


def _rope_bwd(res, ct):
    cos, sin = res
    return ct * cos - _rot_half(ct) * sin, jnp.zeros_like(cos), jnp.zeros_like(sin)


_rope.defvjp(_rope_fwd, _rope_bwd)


@jax.custom_vjp
def _swap_halves(x):
    return _lane_roll(x, HEAD)


_swap_halves.defvjp(lambda x: (_lane_roll(x, HEAD), None), lambda _, ct: (_lane_roll(ct, HEAD),))


def _f_swa(pq, pkv, cos, sin, cosp, sinp, sink_row, valid):
    reps = D // LANES
    q = _rope(pq[:, :D], jnp.tile(cos, (1, reps)), jnp.tile(sin, (1, reps))) * (HEAD ** -0.5)
    k = _rope(pq[:, D:D + KVW], jnp.tile(cos, (1, KVW // LANES)), jnp.tile(sin, (1, KVW // LANES)))
    kp = _rope(pkv[:, :KVW], jnp.tile(cosp, (1, KVW // LANES)), jnp.tile(sinp, (1, KVW // LANES)))
    v, vp, z = pq[:, D + KVW:D + 2 * KVW], pkv[:, KVW:], pq[:, D + 2 * KVW:]
    kcat = jnp.concatenate([kp, k], axis=0)
    vcat = jnp.concatenate([vp, v], axis=0)
    lane = lax.broadcasted_iota(jnp.int32, (2 * CHUNK, LANES), 1)
    lo = lane < HEAD
    hlane = lax.broadcasted_iota(jnp.int32, (1, LANES), 1)

    def halves(cat, g):
        blk = cat[:, (g // 2) * LANES:(g // 2 + 1) * LANES]
        other = _swap_halves(blk)
        if g % 2 == 0:
            return jnp.where(lo, blk, 0.0), jnp.where(lo, 0.0, other)
        return jnp.where(lo, other, 0.0), jnp.where(lo, 0.0, blk)

    pairs = N_HEADS // KV_HEADS // 2
    valid_g = jnp.tile(valid, (pairs, 1))

    def probs(s, heads):
        sink = jnp.concatenate(
            [jnp.broadcast_to(jnp.sum(jnp.where(hlane == h, sink_row, 0.0), axis=1, keepdims=True), (CHUNK, 1))
             for h in heads], axis=0)
        s = jnp.where(valid_g, s, NEG)
        m = lax.stop_gradient(jnp.maximum(jnp.max(s, axis=1, keepdims=True), sink))
        e = jnp.exp(s - m)
        return e / (jnp.sum(e, axis=1, keepdims=True) + jnp.exp(sink - m))

    outs = []
    for g in range(KV_HEADS):
        k_lo, k_hi = halves(kcat, g)
        v_lo, v_hi = halves(vcat, g)
        tiles = range(g * pairs, (g + 1) * pairs)
        qg = jnp.concatenate([q[:, j * LANES:(j + 1) * LANES] for j in tiles], axis=0)
        p_a = probs(_bdot_nt(qg, k_lo), [2 * j for j in tiles])
        p_b = probs(_bdot_nt(qg, k_hi), [2 * j + 1 for j in tiles])
        og = _bdot_nn(p_a, v_lo) + _bdot_nn(p_b, v_hi)
        outs += [og[n * CHUNK:(n + 1) * CHUNK] for n in range(pairs)]
    return jnp.concatenate(outs, axis=1) * jax.nn.silu(z)


def _swa_valid(block):
    qi = lax.broadcasted_iota(jnp.int32, (CHUNK, 2 * CHUNK), 0)
    kj = lax.broadcasted_iota(jnp.int32, (CHUNK, 2 * CHUNK), 1)
    rel = qi + CHUNK - kj
    return (rel >= 0) & (rel < CHUNK) & ((kj >= CHUNK) | (block > 0))


def _swa_specs(blk):
    prev = lambda i: jnp.maximum(blk(i) - 1, 0)
    kv_col = D // KV_BLOCK
    return [pl.BlockSpec((CHUNK, SWA_COLS), lambda i: (blk(i), 0)),
            pl.BlockSpec((CHUNK, KV_BLOCK), lambda i: (prev(i), kv_col)),
            pl.BlockSpec((CHUNK, LANES), lambda i: (blk(i), 0)), pl.BlockSpec((CHUNK, LANES), lambda i: (blk(i), 0)),
            pl.BlockSpec((CHUNK, LANES), lambda i: (prev(i), 0)), pl.BlockSpec((CHUNK, LANES), lambda i: (prev(i), 0)),
            _vec_spec(LANES)]


def _swa_fwd(p, cos, sin, sink_row, name, rider=None):
    t = p.shape[0]

    def body(pq_ref, pkv_ref, c_ref, s_ref, cp_ref, sp_ref, sk_ref, o_ref):
        valid = _swa_valid(pl.program_id(0))
        o_ref[...] = _f_swa(pq_ref[...], pkv_ref[...], c_ref[...], s_ref[...], cp_ref[...], sp_ref[...],
                            sk_ref[...], valid).astype(BF16)

    return _compute_call(
        body, (p, p, cos, sin, cos, sin, sink_row), name=name, grid=(t // CHUNK,), in_specs=_swa_specs(lambda i: i),
        out_specs=_row_spec(D, CHUNK), out_shape=jax.ShapeDtypeStruct((t, D), BF16), semantics=("parallel",), rider=rider)


def _swa_bwd(p, cos, sin, sink_row, dout, name, rider=None):
    t = p.shape[0]
    nb = t // CHUNK
    blk = lambda i: nb - 1 - i

    def body(pq_ref, pkv_ref, c_ref, s_ref, cp_ref, sp_ref, sk_ref, do_ref, dp_ref, dsk_ref, pend_ref):
        i = pl.program_id(0)
        valid = _swa_valid(blk(i))
        f = functools.partial(_f_swa, valid=valid)
        _, vjp = jax.vjp(f, pq_ref[...], pkv_ref[...], c_ref[...], s_ref[...], cp_ref[...], sp_ref[...], sk_ref[...])
        dpq, dpkv, _, _, _, _, dsk = vjp(do_ref[...])

        @pl.when(i == 0)
        def _():
            pend_ref[...] = jnp.zeros_like(pend_ref)

        dp_ref[...] = jnp.concatenate(
            [dpq[:, :D], dpq[:, D:D + KV_BLOCK] + pend_ref[...], dpq[:, D + KV_BLOCK:]], axis=1).astype(BF16)
        pend_ref[...] = dpkv
        _accum(dsk_ref, dsk, i == 0)

    return _compute_call(
        body, (p, p, cos, sin, cos, sin, sink_row, dout), name=name, grid=(nb,),
        in_specs=_swa_specs(blk) + [pl.BlockSpec((CHUNK, D), lambda i: (blk(i), 0))],
        out_specs=[pl.BlockSpec((CHUNK, SWA_COLS), lambda i: (blk(i), 0)), _vec_spec(LANES)],
        out_shape=[jax.ShapeDtypeStruct((t, SWA_COLS), BF16), jax.ShapeDtypeStruct((1, LANES), F32)],
        scratch_shapes=[pltpu.VMEM((CHUNK, KV_BLOCK), F32)], semantics=("arbitrary",), rider=rider)


RW_MAIN = 4 * D
RW_LO = 2 * LORA_PAD
VM = LANES // N_HEADS
VD = HEAD // VM
S_ROWS = VD * HEAD
TB_RW = 128
TB_K = 32
TB_SCAN = 16


def _dim_major(a):
    return a.reshape(a.shape[:-1] + (N_HEADS, HEAD)).swapaxes(-1, -2).reshape(a.shape)


def _head_major(a):
    return a.reshape(a.shape[:-1] + (HEAD, N_HEADS)).swapaxes(-1, -2).reshape(a.shape)


def _compact(x):
    return x.reshape(x.shape[0], VD, LANES)


def _param_compact(w):
    return _dim_major(w).reshape(VD, LANES)


def _param_compact_inv(pc):
    return _head_major(pc.reshape(-1))


def _f_rwkv_lora(xs_lo, w0, a0, wl, al):
    decay = jnp.exp(-DECAY_SCALE * jax.nn.sigmoid(w0 + _bdot_nn(jnp.tanh(xs_lo[:, :LORA_PAD]), wl)))
    a = jax.nn.sigmoid(a0 + _bdot_nn(xs_lo[:, LORA_PAD:], al))
    return decay, a


def _prev_rows_spec(width, tb):
    return pl.BlockSpec((8, width), lambda i: (jnp.maximum(i * (tb // 8) - 1, 0), 0))


def _token_shift_lerp(p, prev8, mu, first):
    rows = lax.broadcasted_iota(jnp.int32, p.shape, 0)
    prev = jnp.where(first, 0.0, prev8[7:8, :])
    shifted = jnp.where(rows == 0, prev, pltpu.roll(p, 1, 0))
    return p + (shifted - p) * mu


def _rwkv_pre_fwd(p_main, p_lo, mu_main, mu_lo, w0, a0, wl, al, name):
    t = p_main.shape[0]
    tb = TB_RW

    def body(pm_ref, pmp_ref, pl_ref, plp_ref, mm_ref, ml_ref, w0_ref, a0_ref, wl_ref, al_ref,
             xm_ref, xl_ref, dec_ref, a_ref):
        first = pl.program_id(0) == 0
        xm_ref[...] = _token_shift_lerp(pm_ref[...], pmp_ref[...], mm_ref[...], first)
        xs_lo = _token_shift_lerp(pl_ref[...], plp_ref[...], ml_ref[...], first)
        xl_ref[...] = xs_lo
        dec_ref[...], a_ref[...] = _f_rwkv_lora(xs_lo, w0_ref[...], a0_ref[...], wl_ref[...], al_ref[...])

    return pl.pallas_call(
        body, name=name, grid=(t // tb,),
        in_specs=[_row_spec(RW_MAIN, tb), _prev_rows_spec(RW_MAIN, tb), _row_spec(RW_LO, tb), _prev_rows_spec(RW_LO, tb),
                  _vec_spec(RW_MAIN), _vec_spec(RW_LO), _vec_spec(D), _vec_spec(D),
                  _vec_spec(D, LORA_PAD), _vec_spec(D, LORA_PAD)],
        out_specs=[_row_spec(RW_MAIN, tb), _row_spec(RW_LO, tb), _row_spec(D, tb), _row_spec(D, tb)],
        out_shape=[jax.ShapeDtypeStruct((t, RW_MAIN), F32), jax.ShapeDtypeStruct((t, RW_LO), F32),
                   jax.ShapeDtypeStruct((t, D), F32), jax.ShapeDtypeStruct((t, D), F32)],
        compiler_params=_cparams("parallel"),
    )(p_main, p_main, p_lo, p_lo, mu_main, mu_lo, w0, a0, wl, al)


def _rwkv_lora_bwd(xs_lo, w0, a0, wl, al, ddecay, da, name):
    t = xs_lo.shape[0]
    tb = TB_NORM

    def body(x_ref, w0_ref, a0_ref, wl_ref, al_ref, dd_ref, da_ref, dx_ref, dw0_ref, da0_ref, dwl_ref, dal_ref):
        _, vjp = jax.vjp(_f_rwkv_lora, x_ref[...], w0_ref[...], a0_ref[...], wl_ref[...], al_ref[...])
        dx, dw0, da0, dwl, dal = vjp((dd_ref[...], da_ref[...]))
        dx_ref[...] = dx
        first = pl.program_id(0) == 0
        _accum(dw0_ref, dw0, first)
        _accum(da0_ref, da0, first)
        _accum(dwl_ref, dwl, first)
        _accum(dal_ref, dal, first)

    vec = jax.ShapeDtypeStruct((1, D), F32)
    lor = jax.ShapeDtypeStruct((LORA_PAD, D), F32)
    return pl.pallas_call(
        body, name=name, grid=(t // tb,),
        in_specs=[_row_spec(RW_LO), _vec_spec(D), _vec_spec(D), _vec_spec(D, LORA_PAD), _vec_spec(D, LORA_PAD),
                  _row_spec(D), _row_spec(D)],
        out_specs=[_row_spec(RW_LO), _vec_spec(D), _vec_spec(D), _vec_spec(D, LORA_PAD), _vec_spec(D, LORA_PAD)],
        out_shape=[jax.ShapeDtypeStruct((t, RW_LO), F32), vec, vec, lor, lor], compiler_params=_cparams("arbitrary"),
    )(xs_lo, w0, a0, wl, al, ddecay, da)


def _lerp_bwd(p, dxs_groups, mu, name):
    t, width = p.shape
    tb = TB_RW
    nb = t // tb
    parts = [a for group in dxs_groups for a in group]

    def body(p_ref, pp_ref, mu_ref, *refs):
        d_refs, (dp_ref, dmu_ref) = refs[:2 * len(parts)], refs[2 * len(parts):]
        i = pl.program_id(0)

        def columns(k):
            pick = (lambda r: r[0:1, :]) if k else (lambda r: r[...])
            vals, at = [], 0
            for group in dxs_groups:
                vals.append(functools.reduce(jnp.add, [pick(d_refs[2 * (at + n) + k]) for n in range(len(group))]))
                at += len(group)
            return jnp.concatenate(vals, axis=1)

        pv, dv, mu_v = p_ref[...], columns(0), mu_ref[...]
        rows = lax.broadcasted_iota(jnp.int32, pv.shape, 0)
        prev = jnp.where(i == 0, 0.0, pp_ref[7:8, :])
        shifted = jnp.where(rows == 0, prev, pltpu.roll(pv, 1, 0))
        nxt = jnp.where(i == nb - 1, 0.0, columns(1))
        d_next = jnp.where(rows == tb - 1, nxt, pltpu.roll(dv, tb - 1, 0))
        dp_ref[...] = (dv * (1.0 - mu_v) + d_next * mu_v).astype(BF16)
        _accum(dmu_ref, jnp.sum(dv * (shifted - pv), axis=0, keepdims=True), i == 0)

    d_specs = []
    for a in parts:
        d_specs += [_row_spec(a.shape[1], tb),
                    pl.BlockSpec((8, a.shape[1]), lambda i: (jnp.minimum((i + 1) * (tb // 8), t // 8 - 1), 0))]
    return pl.pallas_call(
        body, name=name, grid=(nb,),
        in_specs=[_row_spec(width, tb), _prev_rows_spec(width, tb), _vec_spec(width)] + d_specs,
        out_specs=[_row_spec(width, tb), _vec_spec(width)],
        out_shape=[jax.ShapeDtypeStruct((t, width), BF16), jax.ShapeDtypeStruct((1, width), F32)],
        compiler_params=_cparams("arbitrary"),
    )(p, p, mu, *[a for a in parts for _ in range(2)])


def _lane_group_sum2d(x):
    x = x + pltpu.roll(x, N_HEADS, 1)
    return x + pltpu.roll(x, 2 * N_HEADS, 1)


@jax.custom_vjp
def _lane_group_sum(x):
    return _lane_group_sum2d(x.reshape(-1, LANES)).reshape(x.shape)


_lane_group_sum.defvjp(lambda x: (_lane_group_sum(x), None), lambda _, ct: (_lane_group_sum(ct),))


def _head_sum(x):
    return _lane_group_sum(jnp.sum(x, axis=1, keepdims=True))


def _f_kprep(k, a, r, kkp, kap, rkp):
    kk = k * kkp
    kk = kk / jnp.maximum(jnp.sqrt(_head_sum(kk * kk)), 1e-12)
    k2 = k * (1.0 + (a - 1.0) * kap)
    return kk, k2, kk * a, _head_sum(r * k2 * rkp)


def _k_spec(rows=HEAD, tb=TB_K):
    return pl.BlockSpec((tb, rows, LANES), lambda i: (i, 0, 0))


def _kparam_spec(rows=HEAD):
    return pl.BlockSpec((rows, LANES), lambda i: (0, 0))


def _lane_group(shape):
    return lax.broadcasted_iota(jnp.int32, shape, len(shape) - 1) // N_HEADS


def _store_k_layout(ref, xc):
    x2 = xc.reshape(-1, LANES)
    group = _lane_group(x2.shape)
    for q in range(VM):
        one = jnp.where(group == q, x2, 0.0)
        ref[:, pl.ds(q, VD, stride=VM), :] = _lane_group_sum2d(one).reshape(xc.shape)


def _load_compact(ref):
    acc = None
    for q in range(VM):
        rows = _lane_group_sum(ref[:, pl.ds(q, VD, stride=VM), :])
        part = jnp.where(_lane_group(rows.shape) == q, rows, 0.0)
        acc = part if acc is None else acc + part
    return acc


def _rwkv_kprep_fwd(k, a, r, w, kkp, kap, rkp, name):
    t = k.shape[0]

    def body(k_ref, a_ref, r_ref, w_ref, kkp_ref, kap_ref, rkp_ref, kk_ref, k2_ref, b_ref, r4_ref, w4_ref, rk_ref):
        rv = r_ref[...]
        kk, k2, b, rk_ref[...] = _f_kprep(k_ref[...], a_ref[...], rv, kkp_ref[...], kap_ref[...], rkp_ref[...])
        for ref, val in ((kk_ref, kk), (k2_ref, k2), (b_ref, b), (r4_ref, rv), (w4_ref, w_ref[...])):
            _store_k_layout(ref, val)

    big = jax.ShapeDtypeStruct((t, HEAD, LANES), F32)
    return pl.pallas_call(
        body, name=name, grid=(t // TB_K,),
        in_specs=[_k_spec(VD)] * 4 + [_kparam_spec(VD)] * 3, out_specs=[_k_spec()] * 5 + [_k_spec(1)],
        out_shape=[big] * 5 + [jax.ShapeDtypeStruct((t, 1, LANES), F32)], compiler_params=_cparams("parallel"),
    )(k, a, r, w, kkp, kap, rkp)


def _rwkv_kprep_bwd(k, a, r, kkp, kap, rkp, dkk, dk2, db, drk, dr_scan, dw_scan, name):
    t = k.shape[0]

    def body(k_ref, a_ref, r_ref, kkp_ref, kap_ref, rkp_ref, dkk_ref, dk2_ref, db_ref, drk_ref, drs_ref, dws_ref,
             dk_ref, da_ref, dr_ref, dw_ref, dkkp_ref, dkap_ref, drkp_ref):
        _, vjp = jax.vjp(_f_kprep, k_ref[...], a_ref[...], r_ref[...], kkp_ref[...], kap_ref[...], rkp_ref[...])
        dk, da, dr, dkkp, dkap, drkp = vjp((_load_compact(dkk_ref), _load_compact(dk2_ref), _load_compact(db_ref),
                                            drk_ref[...]))
        dk_ref[...] = dk
        da_ref[...] = da
        dr_ref[...] = dr + _load_compact(drs_ref)
        dw_ref[...] = _load_compact(dws_ref)
        first = pl.program_id(0) == 0
        _accum(dkkp_ref, dkkp, first)
        _accum(dkap_ref, dkap, first)
        _accum(drkp_ref, drkp, first)

    cl = jax.ShapeDtypeStruct((t, VD, LANES), F32)
    par = jax.ShapeDtypeStruct((VD, LANES), F32)
    return pl.pallas_call(
        body, name=name, grid=(t // TB_K,),
        in_specs=[_k_spec(VD)] * 3 + [_kparam_spec(VD)] * 3 + [_k_spec()] * 3 + [_k_spec(1), _k_spec(), _k_spec()],
        out_specs=[_k_spec(VD)] * 4 + [_kparam_spec(VD)] * 3,
        out_shape=[cl] * 4 + [par] * 3, compiler_params=_cparams("arbitrary"),
    )(k, a, r, kkp, kap, rkp, dkk, dk2, db, drk, dr_scan, dw_scan)


def _f_post(y, v, rk, g, b):
    mean = _lane_group_sum(jnp.sum(y, axis=1, keepdims=True)) * (1.0 / HEAD)
    yc = y - mean
    var = _lane_group_sum(jnp.sum(yc * yc, axis=1, keepdims=True)) * (1.0 / HEAD)
    return yc * lax.rsqrt(var + GN_EPS) * g + b + rk * v


def _rwkv_post_fwd(y, v, rk, g, b, name):
    t = y.shape[0]

    def body(y_ref, v_ref, rk_ref, g_ref, b_ref, o_ref):
        o_ref[...] = _f_post(y_ref[...], v_ref[...], rk_ref[...], g_ref[...], b_ref[...])

    return pl.pallas_call(
        body, name=name, grid=(t // TB_K,),
        in_specs=[_k_spec(VD), _k_spec(VD), _k_spec(1), _kparam_spec(VD), _kparam_spec(VD)], out_specs=_k_spec(VD),
        out_shape=jax.ShapeDtypeStruct((t, VD, LANES), F32), compiler_params=_cparams("parallel"),
    )(y, v, rk, g, b)


def _rwkv_post_bwd(y, v, rk, g, b, do, name):
    t = y.shape[0]

    def body(y_ref, v_ref, rk_ref, g_ref, b_ref, do_ref, dy_ref, dv_ref, drk_ref, dg_ref, db_ref):
        _, vjp = jax.vjp(_f_post, y_ref[...], v_ref[...], rk_ref[...], g_ref[...], b_ref[...])
        dy, dv, drk, dg, db = vjp(do_ref[...])
        dy_ref[...] = dy
        dv_ref[...] = dv
        drk_ref[...] = drk
        first = pl.program_id(0) == 0
        _accum(dg_ref, dg, first)
        _accum(db_ref, db, first)

    vl = jax.ShapeDtypeStruct((t, VD, LANES), F32)
    par = jax.ShapeDtypeStruct((VD, LANES), F32)
    return pl.pallas_call(
        body, name=name, grid=(t // TB_K,),
        in_specs=[_k_spec(VD), _k_spec(VD), _k_spec(1), _kparam_spec(VD), _kparam_spec(VD), _k_spec(VD)],
        out_specs=[_k_spec(VD), _k_spec(VD), _k_spec(1), _kparam_spec(VD), _kparam_spec(VD)],
        out_shape=[vl, vl, jax.ShapeDtypeStruct((t, 1, LANES), F32), par, par], compiler_params=_cparams("arbitrary"),
    )(y, v, rk, g, b, do)


def _f_gate(o, z):
    return o * jax.nn.silu(z)


def _z_spec(tb=TB_NORM):
    return pl.BlockSpec((tb, D), lambda i: (i, 3))


def _rwkv_gate_fwd(o, xs_main, name):
    t = o.shape[0]

    def body(o_ref, z_ref, u_ref):
        u_ref[...] = _f_gate(o_ref[...], z_ref[...]).astype(BF16)

    return pl.pallas_call(
        body, name=name, grid=(t // TB_NORM,), in_specs=[_row_spec(D), _z_spec()], out_specs=_row_spec(D),
        out_shape=jax.ShapeDtypeStruct((t, D), BF16), compiler_params=_cparams("parallel"),
    )(o, xs_main)


def _rwkv_gate_bwd(o, xs_main, du, name):
    t = o.shape[0]

    def body(o_ref, z_ref, du_ref, do_ref, dz_ref):
        _, vjp = jax.vjp(_f_gate, o_ref[...], z_ref[...])
        do_ref[...], dz_ref[...] = vjp(du_ref[...])

    full = jax.ShapeDtypeStruct((t, D), F32)
    return pl.pallas_call(
        body, name=name, grid=(t // TB_NORM,), in_specs=[_row_spec(D), _z_spec(), _row_spec(D)],
        out_specs=[_row_spec(D), _row_spec(D)], out_shape=[full, full], compiler_params=_cparams("parallel"),
    )(o, xs_main, du)


def _colsum(x):
    return jnp.sum(x, axis=0, keepdims=True)


def _rwkv_scan_fwd(r4, w4, k24, kk4, b4, v, name, rider=None):
    t = r4.shape[0]
    tb = TB_SCAN

    def body(r_ref, w_ref, k2_ref, kk_ref, b_ref, v_ref, y_ref, sall_ref, sa_ref, s_scr):
        @pl.when(pl.program_id(0) == 0)
        def _():
            s_scr[...] = jnp.zeros_like(s_scr)

        sall_ref[0] = s_scr[...]

        def step(tt, dst):
            kk = kk_ref[tt]
            sas = []
            for vd in range(VD):
                sa = _colsum(sall_ref[tt, pl.ds(vd * HEAD, HEAD), :] * kk)
                sa_ref[tt, pl.ds(vd, 1), :] = sa
                sas.append(sa)
            w, b, k2, r = w_ref[tt], b_ref[tt], k2_ref[tt], r_ref[tt]
            for vd in range(VD):
                rows = pl.ds(vd * HEAD, HEAD)
                s = sall_ref[tt, rows, :] * w - sas[vd] * b + v_ref[tt, pl.ds(vd, 1), :] * k2
                dst[rows, :] = s
                y_ref[tt, pl.ds(vd, 1), :] = _colsum(s * r)

        def loop_step(tt, carry):
            step(tt, sall_ref.at[tt + 1])
            return carry

        lax.fori_loop(0, tb - 1, loop_step, 0)
        step(tb - 1, s_scr)

    vl = jax.ShapeDtypeStruct((t, VD, LANES), F32)
    return _compute_call(
        body, (r4, w4, k24, kk4, b4, v), name=name, grid=(t // tb,),
        in_specs=[_k_spec(HEAD, tb)] * 5 + [_k_spec(VD, tb)],
        out_specs=[_k_spec(VD, tb), _k_spec(S_ROWS, tb), _k_spec(VD, tb)],
        out_shape=[vl, jax.ShapeDtypeStruct((t, S_ROWS, LANES), F32), vl],
        scratch_shapes=[pltpu.VMEM((S_ROWS, LANES), F32)], semantics=("arbitrary",), rider=rider)


def _rwkv_scan_bwd(dy, s_all, sa_all, r4, w4, k24, kk4, b4, v, name, rider=None):
    t = r4.shape[0]
    tb = TB_SCAN
    nb = t // tb
    blk = lambda i: nb - 1 - i

    def body(dy_ref, sall_ref, sa_ref, r_ref, w_ref, k2_ref, kk_ref, b_ref, v_ref,
             dr_ref, dw_ref, dk2_ref, dkk_ref, db_ref, dv_ref, ds_scr):
        @pl.when(pl.program_id(0) == 0)
        def _():
            ds_scr[...] = jnp.zeros_like(ds_scr)

        def step(j, carry):
            tt = tb - 1 - j
            vrow = lambda ref, vd: ref[tt, pl.ds(vd, 1), :]
            srows = lambda vd: pl.ds(vd * HEAD, HEAD)
            r, k2, b = r_ref[tt], k2_ref[tt], b_ref[tt]
            dsas = []
            for vd in range(VD):
                ds = ds_scr[srows(vd), :] + vrow(dy_ref, vd) * r
                ds_scr[srows(vd), :] = ds
                dv_ref[tt, pl.ds(vd, 1), :] = _colsum(ds * k2)
                dsas.append(-_colsum(ds * b))
            zero = jnp.zeros((HEAD, LANES), F32)
            dk2, q, sady, vdy = zero, zero, 0.0, 0.0
            for vd in range(VD):
                dyv = vrow(dy_ref, vd)
                dk2 = dk2 + ds_scr[srows(vd), :] * vrow(v_ref, vd)
                q = q + sall_ref[tt, srows(vd), :] * dyv
                sady = sady + vrow(sa_ref, vd) * dyv
                vdy = vdy + vrow(v_ref, vd) * dyv
            dk2_ref[tt] = dk2
            dr_ref[tt] = w_ref[tt] * q - b_ref[tt] * sady + k2_ref[tt] * vdy
            dw, dkk = zero, zero
            for vd in range(VD):
                sp = sall_ref[tt, srows(vd), :]
                dw = dw + ds_scr[srows(vd), :] * sp
                dkk = dkk + sp * dsas[vd]
            dw_ref[tt] = dw
            dkk_ref[tt] = dkk
            w, kk = w_ref[tt], kk_ref[tt]
            db = zero
            for vd in range(VD):
                ds = ds_scr[srows(vd), :]
                db = db - ds * vrow(sa_ref, vd)
                ds_scr[srows(vd), :] = ds * w + dsas[vd] * kk
            db_ref[tt] = db
            return carry

        lax.fori_loop(0, tb, step, 0)

    rk = lambda rows: pl.BlockSpec((tb, rows, LANES), lambda i: (blk(i), 0, 0))
    big = jax.ShapeDtypeStruct((t, HEAD, LANES), F32)
    return _compute_call(
        body, (dy, s_all, sa_all, r4, w4, k24, kk4, b4, v), name=name, grid=(nb,),
        in_specs=[rk(VD), rk(S_ROWS), rk(VD)] + [rk(HEAD)] * 5 + [rk(VD)],
        out_specs=[rk(HEAD)] * 5 + [rk(VD)],
        out_shape=[big] * 5 + [jax.ShapeDtypeStruct((t, VD, LANES), F32)],
        scratch_shapes=[pltpu.VMEM((S_ROWS, LANES), F32)], semantics=("arbitrary",), rider=rider)


def _rwkv_mixer_fwd(p_main, p_lo, prm, tag, rider):
    xs_main, xs_lo, decay, a = _rwkv_pre_fwd(p_main, p_lo, prm["mu_main"], prm["mu_lo"], prm["w0"], prm["a0"],
                                             prm["wl"], prm["al"], tag + "_pre")
    r, k, v, w, a = (_compact(x) for x in (xs_main[:, :D], xs_main[:, D:2 * D], xs_main[:, 2 * D:3 * D], decay, a))
    kk4, k24, b4, r4, w4, rk = _rwkv_kprep_fwd(k, a, r, w, prm["kkp"], prm["kap"], prm["rkp"], tag + "_kprep")
    (y, s_all, sa_all), ridden = _ridden(_rwkv_scan_fwd(r4, w4, k24, kk4, b4, v, tag + "_scan", rider), rider)
    o = _rwkv_post_fwd(y, v, rk, prm["gn_g"], prm["gn_b"], tag + "_post").reshape(-1, D)
    u = _rwkv_gate_fwd(o, xs_main, tag + "_gate")
    saved = dict(xs_main=xs_main, xs_lo=xs_lo, r=r, k=k, a=a, v=v, r4=r4, w4=w4, kk4=kk4, k24=k24, b4=b4, rk=rk,
                 y=y, s_all=s_all, sa_all=sa_all, o=o)
    return u, saved, ridden


def _rwkv_mixer_bwd(p_main, p_lo, prm, sv, du, tag, rider):
    do, dz = _rwkv_gate_bwd(sv["o"], sv["xs_main"], du, tag + "_gate_b")
    dy, dv_post, drk, dgn_g, dgn_b = _rwkv_post_bwd(sv["y"], sv["v"], sv["rk"], prm["gn_g"], prm["gn_b"],
                                                    _compact(do), tag + "_post_b")
    (dr_s, dw_s, dk24, dkk4, db4, dv_scan), ridden = _ridden(_rwkv_scan_bwd(
        dy, sv["s_all"], sv["sa_all"], sv["r4"], sv["w4"], sv["k24"], sv["kk4"], sv["b4"], sv["v"], tag + "_scan_b", rider), rider)
    dk, da, dr, dw, dkkp, dkap, drkp = _rwkv_kprep_bwd(sv["k"], sv["a"], sv["r"], prm["kkp"], prm["kap"], prm["rkp"],
                                                       dkk4, dk24, db4, drk, dr_s, dw_s, tag + "_kprep_b")
    flat = lambda xc: xc.reshape(-1, D)
    dxs_lo, dw0, da0, dwl, dal = _rwkv_lora_bwd(sv["xs_lo"], prm["w0"], prm["a0"], prm["wl"], prm["al"],
                                                flat(dw), flat(da), tag + "_lora_b")
    dxs_main = [[flat(dr)], [flat(dk)], [flat(dv_post), flat(dv_scan)], [dz]]
    dp_main, dmu_main = _lerp_bwd(p_main, dxs_main, prm["mu_main"], tag + "_lerp_main_b")
    dp_lo, dmu_lo = _lerp_bwd(p_lo, [[dxs_lo]], prm["mu_lo"], tag + "_lerp_lo_b")
    grads = dict(mu_main=dmu_main, mu_lo=dmu_lo, w0=dw0, a0=da0, wl=dwl, al=dal, kkp=dkkp, kap=dkap, rkp=drkp,
                 gn_g=dgn_g, gn_b=dgn_b)
    return dp_main, dp_lo, grads, ridden


N_DEV = 8
N_CHIPS = 4
ANY = pl.BlockSpec(memory_space=pl.ANY)


def _place():
    return lax.axis_index("x"), lax.axis_index("y"), lax.axis_index("c")


def _remote(src, dst, send_sems, recv_sems, k, dev):
    return pltpu.make_async_remote_copy(src_ref=src, dst_ref=dst, send_sem=send_sems.at[k], recv_sem=recv_sems.at[k],
                                        device_id=dev, device_id_type=MESHT)


def _all_gather8(v, name):
    def body(buf_ref, out_ref, send_sems, recv_sems):
        del buf_ref
        x, y, c = _place()
        mine = out_ref.at[4 * x + 2 * y + c]
        peers = [(x ^ (k >> 2), y ^ ((k >> 1) & 1), c ^ (k & 1)) for k in range(1, N_DEV)]
        sends = [_remote(mine, mine, send_sems, recv_sems, k, peer) for k, peer in enumerate(peers)]
        for cp in sends:
            cp.start()
        for k, (px, py, pc) in enumerate(peers):
            _remote(mine, out_ref.at[4 * px + 2 * py + pc], send_sems, recv_sems, k, (x, y, c)).wait_recv()
        for cp in sends:
            cp.wait_send()

    return pl.pallas_call(
        body, name=name, in_specs=[ANY], out_specs=ANY, input_output_aliases={0: 0},
        out_shape=jax.ShapeDtypeStruct((N_DEV,) + v.shape, v.dtype),
        scratch_shapes=[pltpu.SemaphoreType.DMA((N_DEV - 1,)), pltpu.SemaphoreType.DMA((N_DEV - 1,))],
    )(jnp.broadcast_to(v[None], (N_DEV,) + v.shape))


def _other_chips(x, y):
    return [(1 - x, y), (x, 1 - y), (1 - x, 1 - y)]


GATHER_SEMS = 6


def _gather_buffer(v):
    return jnp.broadcast_to(v[None], (N_CHIPS,) + v.shape)


def _gather_start(bufs, send_sems, recv_sems):
    x, y, c = _place()
    for i, buf in enumerate(bufs):
        mine = buf.at[2 * x + y, c]
        for j, (cx, cy) in enumerate(_other_chips(x, y)):
            _remote(mine, mine, send_sems, recv_sems, GATHER_SEMS * i + j, (cx, cy, c)).start()


def _gather_finish(bufs, send_sems, recv_sems):
    x, y, c = _place()
    chips = _other_chips(x, y)
    passed = []
    for i, buf in enumerate(bufs):
        mine = buf.at[2 * x + y, c]
        for j, (cx, cy) in enumerate(chips):
            landed = buf.at[2 * cx + cy, c]
            _remote(mine, landed, send_sems, recv_sems, GATHER_SEMS * i + j, (x, y, c)).wait_recv()
            fwd = _remote(landed, landed, send_sems, recv_sems, GATHER_SEMS * i + 3 + j, (x, y, 1 - c))
            fwd.start()
            passed.append(fwd)
    for i, buf in enumerate(bufs):
        mine = buf.at[2 * x + y, c]
        for j, (cx, cy) in enumerate(chips):
            _remote(mine, buf.at[2 * cx + cy, 1 - c], send_sems, recv_sems, GATHER_SEMS * i + 3 + j, (x, y, c)).wait_recv()
            _remote(mine, mine, send_sems, recv_sems, GATHER_SEMS * i + j, (cx, cy, c)).wait_send()
    for fwd in passed:
        fwd.wait_send()


def _gather_rider(bufs):
    return _Rider(bufs, GATHER_SEMS * len(bufs), _gather_start, _gather_finish)


def _chip_gather(bufs, name):
    n = len(bufs)

    def body(*refs):
        out_refs, (send_sems, recv_sems) = refs[n:2 * n], refs[2 * n:]
        _gather_start(out_refs, send_sems, recv_sems)
        _gather_finish(out_refs, send_sems, recv_sems)

    return pl.pallas_call(
        body, name=name, in_specs=[ANY] * n, out_specs=[ANY] * n, input_output_aliases={i: i for i in range(n)},
        out_shape=[jax.ShapeDtypeStruct(b.shape, b.dtype) for b in bufs], scratch_shapes=_dma_sems(GATHER_SEMS * n),
    )(*bufs)


RS_W = 1024
RS_BLOCK_BYTES = 4 << 20


def _dma_sems(n):
    return [pltpu.SemaphoreType.DMA((n,)), pltpu.SemaphoreType.DMA((n,))]


def _pair_exchange_copies(refs, send_sems, recv_sems):
    n = len(refs) // 2
    x, y, c = _place()
    return [_remote(refs[i].at[s, 1 - c], refs[n + i].at[s], send_sems, recv_sems, N_CHIPS * i + s, (x, y, 1 - c))
            for i in range(n) for s in range(N_CHIPS)]


def _pair_exchange_start(refs, send_sems, recv_sems):
    for cp in _pair_exchange_copies(refs, send_sems, recv_sems):
        cp.start()


def _pair_exchange_finish(refs, send_sems, recv_sems):
    for cp in _pair_exchange_copies(refs, send_sems, recv_sems):
        cp.wait()


def _pair_exchange_rider(gs):
    landing = [lax.empty((N_CHIPS,) + g.shape[2:], g.dtype) for g in gs]
    return _Rider(list(gs) + landing, N_CHIPS * len(gs), _pair_exchange_start, _pair_exchange_finish)


def _rs_rows(rows, cols):
    cap = max(16, RS_BLOCK_BYTES // (N_CHIPS * 4 * cols))
    return rows if rows <= cap else max(d for d in range(16, cap + 1, 16) if rows % d == 0)


def _rs_pair_add(g, got, c_arr, name):
    _, _, rows, width = g.shape
    tr = _rs_rows(rows, width)

    def body(c_ref, g_ref, got_ref, p_ref):
        p_ref[...] = (g_ref[...] + got_ref[...]).astype(BF16)

    return pl.pallas_call(
        body, name=name,
        grid_spec=pltpu.PrefetchScalarGridSpec(
            num_scalar_prefetch=1, grid=(rows // tr,),
            in_specs=[pl.BlockSpec((N_CHIPS, None, tr, width), lambda i, c_ref: (0, c_ref[0], i, 0)),
                      pl.BlockSpec((N_CHIPS, tr, width), lambda i, c_ref: (0, i, 0))],
            out_specs=pl.BlockSpec((N_CHIPS, tr, width), lambda i, c_ref: (0, i, 0))),
        out_shape=jax.ShapeDtypeStruct((N_CHIPS, rows, width), BF16), compiler_params=_cparams("parallel"),
    )(c_arr, g, got)


def _chip_exchange_copies(refs, send_sems, recv_sems):
    n = len(refs) // 2
    x, y, c = _place()
    return [_remote(refs[i].at[2 * cx + cy], refs[n + i].at[j], send_sems, recv_sems, 3 * i + j, (cx, cy, c))
            for i in range(n) for j, (cx, cy) in enumerate(_other_chips(x, y))]


def _chip_exchange_start(refs, send_sems, recv_sems):
    for cp in _chip_exchange_copies(refs, send_sems, recv_sems):
        cp.start()


def _chip_exchange_finish(refs, send_sems, recv_sems):
    n = len(refs) // 2
    x, y, c = _place()
    for i in range(n):
        for j in range(3):
            _remote(refs[i].at[2 * x + y], refs[n + i].at[j], send_sems, recv_sems, 3 * i + j, (x, y, c)).wait_recv()
    for cp in _chip_exchange_copies(refs, send_sems, recv_sems):
        cp.wait_send()


def _chip_exchange_buffers(ps):
    return [lax.empty((3,) + p.shape[1:], p.dtype) for p in ps]


def _chip_exchange_rider(ps):
    return _Rider(list(ps) + _chip_exchange_buffers(ps), 3 * len(ps), _chip_exchange_start, _chip_exchange_finish)


def _rs_chip_exchange(ps, name):
    n = len(ps)

    def body(*refs):
        out_refs, (send_sems, recv_sems) = refs[2 * n:4 * n], refs[4 * n:]
        _chip_exchange_start(out_refs, send_sems, recv_sems)
        _chip_exchange_finish(out_refs, send_sems, recv_sems)

    arrays = list(ps) + _chip_exchange_buffers(ps)
    return pl.pallas_call(
        body, name=name, in_specs=[ANY] * (2 * n), out_specs=[ANY] * (2 * n),
        input_output_aliases={i: i for i in range(2 * n)},
        out_shape=[jax.ShapeDtypeStruct(a.shape, a.dtype) for a in arrays], scratch_shapes=_dma_sems(3 * n),
    )(*arrays)


def _rs_chip_add(p, q, idx, name):
    _, rows, width = q.shape
    tr = _rs_rows(rows, width)

    def body(idx_ref, p_ref, q_ref, r_ref):
        qv = q_ref[...].astype(F32)
        r_ref[...] = ((p_ref[...].astype(F32) + qv[0]) + qv[1]) + qv[2]

    return pl.pallas_call(
        body, name=name,
        grid_spec=pltpu.PrefetchScalarGridSpec(
            num_scalar_prefetch=1, grid=(rows // tr,),
            in_specs=[pl.BlockSpec((None, tr, width), lambda i, idx_ref: (idx_ref[0], i, 0)),
                      pl.BlockSpec((3, tr, width), lambda i, idx_ref: (0, i, 0))],
            out_specs=pl.BlockSpec((None, tr, width), lambda i, idx_ref: (idx_ref[1], i, 0))),
        out_shape=jax.ShapeDtypeStruct((2, rows, width), F32), compiler_params=_cparams("parallel"),
    )(idx, p, q)


def _rs_pair_share(rs, name):
    n = len(rs)

    def body(*refs):
        out_refs, (send_sems, recv_sems) = refs[n:2 * n], refs[2 * n:]
        x, y, c = _place()
        sends = [_remote(out_refs[i].at[c], out_refs[i].at[c], send_sems, recv_sems, i, (x, y, 1 - c)) for i in range(n)]
        for cp in sends:
            cp.start()
        for i in range(n):
            _remote(out_refs[i].at[c], out_refs[i].at[1 - c], send_sems, recv_sems, i, (x, y, c)).wait_recv()
        for cp in sends:
            cp.wait_send()

    return pl.pallas_call(
        body, name=name, in_specs=[ANY] * n, out_specs=[ANY] * n, input_output_aliases={i: i for i in range(n)},
        out_shape=[jax.ShapeDtypeStruct(r.shape, r.dtype) for r in rs], scratch_shapes=_dma_sems(n),
    )(*rs)


def _rs_pair_sums(gs, gots, core, tag):
    c_arr = core.astype(jnp.int32).reshape(1)
    return [_rs_pair_add(g, got, c_arr, f"{tag}_pair_add{i}") for i, (g, got) in enumerate(zip(gs, gots))]


def _rs_finish(ps, qs, chip, core, tag):
    idx = jnp.stack([chip, core]).astype(jnp.int32)
    rs = [_rs_chip_add(p, q, idx, f"{tag}_chip_add{i}") for i, (p, q) in enumerate(zip(ps, qs))]
    return _rs_pair_share(rs, tag + "_share")


def _sum_leading(a, name):
    n, rows, width = a.shape
    cap = max(8, RS_BLOCK_BYTES // (n * 4 * width))
    tr = rows if rows <= cap else max(d for d in range(8, cap + 1, 8) if rows % d == 0)

    def body(a_ref, o_ref):
        acc = a_ref[0]
        for d in range(1, n):
            acc = acc + a_ref[d]
        o_ref[...] = acc

    return pl.pallas_call(
        body, name=name, grid=(rows // tr,), in_specs=[pl.BlockSpec((n, tr, width), lambda i: (0, i, 0))],
        out_specs=pl.BlockSpec((tr, width), lambda i: (i, 0)), out_shape=jax.ShapeDtypeStruct((rows, width), F32),
        compiler_params=_cparams("parallel"),
    )(a)


def _pair_swap(v, name):
    def body(v_ref, got_ref, send_sems, recv_sems):
        x, y, c = _place()
        cp = _remote(v_ref, got_ref, send_sems, recv_sems, 0, (x, y, 1 - c))
        cp.start()
        cp.wait()

    return pl.pallas_call(body, name=name, in_specs=[ANY], out_specs=ANY, out_shape=jax.ShapeDtypeStruct(v.shape, v.dtype),
                          scratch_shapes=_dma_sems(1))(v)


def _all_reduce_replicated(v, name):
    rows, width = v.shape
    pair = _sum_leading(jnp.stack([v, _pair_swap(v, name + "_swap")]), name + "_pair_add")
    (gathered,) = _chip_gather([_gather_buffer(pair.reshape(2, rows // 2, width))], name + "_gather")
    return _sum_leading(gathered.reshape(N_CHIPS, rows, width), name + "_chip_add")


MOD_COLS = 3 * D // N_CHIPS
MOD_TK = 512


def _mod_partial(c_all, mod_w, name):
    nk = D // MOD_TK

    def body(c_ref, w_ref, o_ref):
        l = pl.program_id(1)
        part = _bdot_nn(jax.nn.silu(c_ref[...]), w_ref[0])
        _accum(o_ref.at[0], part, l == 0)

    return pl.pallas_call(
        body, name=name, grid=(DEPTH, nk),
        in_specs=[pl.BlockSpec((N_DEV, MOD_TK), lambda i, l: (0, l)), pl.BlockSpec((1, MOD_TK, MOD_COLS), lambda i, l: (i, l, 0))],
        out_specs=pl.BlockSpec((1, N_DEV, MOD_COLS), lambda i, l: (i, 0, 0)),
        out_shape=jax.ShapeDtypeStruct((DEPTH, N_DEV, MOD_COLS), F32), compiler_params=_cparams("parallel", "arbitrary"),
    )(c_all, mod_w)


def _mod_w_grad(c_all, dmod, name):
    def body(c_ref, d_ref, o_ref):
        o_ref[0] = _dg(jax.nn.silu(c_ref[...]).astype(BF16), d_ref[0].astype(BF16), _TN)

    return pl.pallas_call(
        body, name=name, grid=(DEPTH, D // MOD_TK),
        in_specs=[pl.BlockSpec((N_DEV, MOD_TK), lambda i, l: (0, l)), pl.BlockSpec((1, N_DEV, MOD_COLS), lambda i, l: (i, 0, 0))],
        out_specs=pl.BlockSpec((1, MOD_TK, MOD_COLS), lambda i, l: (i, l, 0)),
        out_shape=jax.ShapeDtypeStruct((DEPTH, D, MOD_COLS), F32), compiler_params=_cparams("parallel", "parallel"),
    )(c_all, dmod)


ADAM_BLOCK_BYTES = 1 << 20


def _adamw(w, g, m, v, name):
    shape = w.shape
    cols = shape[-1]
    rows = w.size // cols
    w, g, m, v = (a.reshape(rows, cols) for a in (w, g, m, v))
    cap = max(8, ADAM_BLOCK_BYTES // (4 * cols))
    tr = rows if rows <= cap else max(d for d in range(8, cap + 1, 8) if rows % d == 0)
    c1 = 1.0 - ADAM_B1 ** ADAM_STEP
    c2 = 1.0 - ADAM_B2 ** ADAM_STEP

    def body(w_ref, g_ref, m_ref, v_ref, d_ref, nm_ref, nv_ref):
        gv = g_ref[...]
        mn = ADAM_B1 * m_ref[...] + (1.0 - ADAM_B1) * gv
        vn = ADAM_B2 * v_ref[...] + (1.0 - ADAM_B2) * (gv * gv)
        nm_ref[...] = mn
        nv_ref[...] = vn
        d_ref[...] = -ADAM_LR * ((mn / c1) / (jnp.sqrt(vn / c2) + ADAM_EPS) + ADAM_WD * w_ref[...])

    spec = pl.BlockSpec((tr, cols), lambda i: (i, 0))
    out = jax.ShapeDtypeStruct((rows, cols), F32)
    d, nm, nv = pl.pallas_call(
        body, name=name, grid=(rows // tr,), in_specs=[spec] * 4, out_specs=[spec] * 3, out_shape=[out] * 3,
        compiler_params=_cparams("parallel"),
    )(w, g, m, v)
    return d.reshape(shape), nm.reshape(shape), nv.reshape(shape)


W_NAMES = ("norm_g", "mod_w", "mod_b", "final_norm_g", "sg_w_in", "sg_w_out", "sg_ln_g", "sg_ln_b", "sg_w_spatial",
           "sg_b_spatial", "swa_w_in", "swa_w_out", "swa_sinks", "rwkv_w_in", "rwkv_w_out", "rwkv_mu", "rwkv_w0",
           "rwkv_w_lora", "rwkv_a0", "rwkv_a_lora", "rwkv_k_k", "rwkv_k_a", "rwkv_r_k", "rwkv_gn_g", "rwkv_gn_b")
SMALL = {"sg_ln_g": 1, "sg_ln_b": 1, "rwkv_mu": 1, "rwkv_w0": 1, "rwkv_w_lora": 2, "rwkv_a0": 1, "rwkv_a_lora": 2,
         "rwkv_k_k": 1, "rwkv_k_a": 1, "rwkv_gn_g": 1, "rwkv_gn_b": 1}
REPLICATED = ("norm_g", "final_norm_g", "sg_w_spatial", "sg_b_spatial", "swa_sinks", "rwkv_r_k")
KINDS = ("sg", "swa", "rwkv", "sg")


def _pad_to(flat, n):
    return jnp.pad(flat, (0, n - flat.shape[0]))


def _round_up(n, m):
    return -(-n // m) * m


def _join_shards(gathered, axis):
    return jnp.concatenate([gathered[s] for s in range(N_CHIPS)], axis=axis)


def _chip_blocks(full, axis):
    return jnp.stack(jnp.split(full, N_CHIPS, axis=axis)).reshape(N_CHIPS, -1)


def _weight_buffer(w):
    rows, cols = w.shape
    return _gather_buffer(w.astype(BF16).reshape(2, rows // 2, cols))


def _chip_shards(buf):
    return buf.reshape(N_CHIPS, -1, buf.shape[-1])


def _small_buffer(shards):
    flat = jnp.concatenate([shards[n].reshape(-1) for n in SMALL])
    rows = _round_up(flat.shape[0], 2 * 8 * LANES) // (2 * LANES)
    return _gather_buffer(_pad_to(flat, 2 * rows * LANES).reshape(2, rows, LANES))


def _unpack_small(buf, shards):
    got = buf.reshape(N_CHIPS, -1)
    out, off = {}, 0
    for n, axis in SMALL.items():
        size = shards[n].size
        out[n] = _join_shards(got[:, off:off + size].reshape((N_CHIPS,) + shards[n].shape), axis)
        off += size
    return out


def _lora_pad_rows(w):
    return jnp.pad(w, ((0, LORA_PAD - LORA), (0, 0)))


def _lo_cols(a):
    z = jnp.zeros(a.shape[:-1] + (LORA_PAD - LORA,), a.dtype)
    return jnp.concatenate([a[..., :LORA], z, a[..., LORA:], z], axis=-1)


def _lo_cols_inv(a):
    return jnp.concatenate([a[..., :LORA], a[..., LORA_PAD:LORA_PAD + LORA]], axis=-1)


def _rows_dim_major(w):
    return w.reshape(N_HEADS, HEAD, -1).swapaxes(0, 1).reshape(w.shape)


def _rows_head_major(w):
    return w.reshape(HEAD, N_HEADS, -1).swapaxes(0, 1).reshape(w.shape)


def kernel(x, c, positions, norm_g, mod_w, mod_b, final_norm_g, sg_w_in, sg_w_out, sg_ln_g, sg_ln_b, sg_w_spatial,
           sg_b_spatial, swa_w_in, swa_w_out, swa_sinks, rwkv_w_in, rwkv_w_out, rwkv_mu, rwkv_w0, rwkv_w_lora, rwkv_a0,
           rwkv_a_lora, rwkv_k_k, rwkv_k_a, rwkv_r_k, rwkv_gn_g, rwkv_gn_b, loss_target, m_norm_g, m_mod_w, m_mod_b,
           m_final_norm_g, m_sg_w_in, m_sg_w_out, m_sg_ln_g, m_sg_ln_b, m_sg_w_spatial, m_sg_b_spatial, m_swa_w_in,
           m_swa_w_out, m_swa_sinks, m_rwkv_w_in, m_rwkv_w_out, m_rwkv_mu, m_rwkv_w0, m_rwkv_w_lora, m_rwkv_a0,
           m_rwkv_a_lora, m_rwkv_k_k, m_rwkv_k_a, m_rwkv_r_k, m_rwkv_gn_g, m_rwkv_gn_b, v_norm_g, v_mod_w, v_mod_b,
           v_final_norm_g, v_sg_w_in, v_sg_w_out, v_sg_ln_g, v_sg_ln_b, v_sg_w_spatial, v_sg_b_spatial, v_swa_w_in,
           v_swa_w_out, v_swa_sinks, v_rwkv_w_in, v_rwkv_w_out, v_rwkv_mu, v_rwkv_w0, v_rwkv_w_lora, v_rwkv_a0,
           v_rwkv_a_lora, v_rwkv_k_k, v_rwkv_k_a, v_rwkv_r_k, v_rwkv_gn_g, v_rwkv_gn_b):
    given = dict(locals())
    w = {n: given[n] for n in W_NAMES}
    xi, yi, ci = _place()
    chip = 2 * xi + yi
    me = 4 * xi + 2 * yi + ci
    xs = [x[0]]

    c_all = _all_gather8(c, "gather_c")[:, 0, :]
    mod_part = _mod_partial(c_all, mod_w, "mod_fwd")
    mod_all = _all_gather8(mod_part, "gather_mod")[::2]
    mod_mine = lax.dynamic_index_in_dim(mod_all, me, axis=2, keepdims=False)
    mod = mod_mine.transpose(1, 0, 2).reshape(DEPTH, 3 * D) + mod_b
    shift, scale, gate = mod[:, :D], mod[:, D:2 * D], mod[:, 2 * D:]

    shards = {"sg_w_in0": sg_w_in[0], "sg_w_out0": sg_w_out[0], "swa_w_in": swa_w_in[0], "swa_w_out": swa_w_out[0],
              "rwkv_w_in": rwkv_w_in[0], "rwkv_w_out": rwkv_w_out[0], "sg_w_in1": sg_w_in[1], "sg_w_out1": sg_w_out[1]}
    bufs = {n: _weight_buffer(s) for n, s in shards.items()}
    fwd_riders = {(0, "in"): ["swa_w_in"], (0, "mix"): ["swa_w_out"], (1, "in"): ["rwkv_w_out"], (1, "mix"): ["rwkv_w_in"],
                  (2, "mix"): ["sg_w_in1", "sg_w_out1"]}
    bufs["sg_w_in0"], bufs["sg_w_out0"], small_buf = _chip_gather(
        [bufs["sg_w_in0"], bufs["sg_w_out0"], _small_buffer(w)], "gather_l0")
    full = _unpack_small(small_buf, w)

    def riding(i, where):
        names = fwd_riders.get((i, where))
        return names, (None if names is None else _gather_rider([bufs[n] for n in names]))

    def arrived(names, ridden):
        for n, b in zip(names or [], ridden):
            bufs[n] = b

    sg_in = lambda j: _chip_shards(bufs[f"sg_w_in{j}"])
    sg_out = lambda j: bufs[f"sg_w_out{j}"].reshape(D, D)
    mu = full["rwkv_mu"][0]
    rw_prm = dict(mu_main=_dim_major(mu[:RW_MAIN].reshape(4, D)).reshape(1, RW_MAIN), mu_lo=_lo_cols(mu[None, RW_MAIN:]),
                  w0=_dim_major(full["rwkv_w0"]), a0=_dim_major(full["rwkv_a0"]),
                  wl=_lora_pad_rows(_dim_major(full["rwkv_w_lora"][0])), al=_lora_pad_rows(_dim_major(full["rwkv_a_lora"][0])),
                  kkp=_param_compact(full["rwkv_k_k"][0]), kap=_param_compact(full["rwkv_k_a"][0]),
                  rkp=_param_compact(rwkv_r_k.reshape(-1)),
                  gn_g=_param_compact(full["rwkv_gn_g"][0]), gn_b=_param_compact(full["rwkv_gn_b"][0]))
    bs_t = [jnp.pad(sg_b_spatial[j].T, ((0, 0), (0, LANES - SG_GROUPS))) for j in range(2)]
    sink_row = jnp.pad(swa_sinks, ((0, 0), (0, LANES - N_HEADS)))
    inv_freq = ROPE_THETA ** (-jnp.arange(HEAD // 2, dtype=F32) / (HEAD // 2))
    ang = positions[0].astype(F32)[:, None] * inv_freq
    cos, sin = jnp.tile(jnp.cos(ang), (1, LANES * 2 // HEAD)), jnp.tile(jnp.sin(ang), (1, LANES * 2 // HEAD))

    def row(a, i):
        return a[i:i + 1]

    hs, ps, us, ys, rw_saved = [], [], [], [], None
    for i, kind in enumerate(KINDS):
        j = i // 3
        tag = f"l{i}_{kind}"
        h = _norm_mod_fwd(xs[i], row(norm_g, i), row(shift, i), row(scale, i), tag + "_norm")
        names_in, rider_in = riding(i, "in")
        names_mix, rider_mix = riding(i, "mix")
        if kind == "sg":
            p, ridden = _ridden(_matmul(h, sg_in(j), "nn", tag + "_in", blocked=True, rider=rider_in), rider_in)
            arrived(names_in, ridden)
            u, ridden = _ridden(_sg_fwd(p, row(full["sg_ln_g"], j), row(full["sg_ln_b"], j), sg_w_spatial[j], bs_t[j],
                                        tag + "_mix", rider_mix), rider_mix)
            w_out = sg_out(j)
        elif kind == "swa":
            swa_in, swa_out = _chip_shards(bufs["swa_w_in"]), bufs["swa_w_out"].reshape(D, D)
            p, ridden = _ridden(_matmul(h, swa_in, "nn", tag + "_in", blocked=True, rider=rider_in), rider_in)
            arrived(names_in, ridden)
            u, ridden = _ridden(_swa_fwd(p, cos, sin, sink_row, tag + "_mix", rider_mix), rider_mix)
            w_out = swa_out
        else:
            rw_in = _join_shards(_chip_shards(bufs["rwkv_w_in"]), axis=1)
            rw_main = _dim_major(rw_in[:, :RW_MAIN].reshape(D, 4, D)).reshape(D, RW_MAIN)
            rw_lo = _lo_cols(rw_in[:, RW_MAIN:])
            rw_out = _rows_dim_major(bufs["rwkv_w_out"].reshape(D, D))
            p = (_matmul(h, rw_main, "nn", tag + "_in"), _matmul(h, rw_lo, "nn", tag + "_in_lo"))
            u, rw_saved, ridden = _rwkv_mixer_fwd(p[0], p[1], rw_prm, tag, rider_mix)
            w_out = rw_out
        arrived(names_mix, ridden)
        y, x_next = _out_proj_resid(u, w_out, xs[i], row(gate, i), tag + "_out")
        xs.append(x_next)
        hs.append(h), ps.append(p), us.append(u), ys.append(y)

    loss_part, dx, d_final_g = _final_loss_grad(xs[DEPTH], final_norm_g[None], loss_target[0], "loss")
    loss = lax.psum(loss_part[0, 0], ("x", "y", "c"))

    gfull = {n: [None, None] for n in ("sg_ln_g", "sg_ln_b", "sg_w_spatial", "sg_b_spatial")}
    gbig = {}
    d_norm_g, d_mod = [None] * DEPTH, [None] * DEPTH
    rs_p, rs_q, riding_names = {}, {}, []

    for i in reversed(range(DEPTH)):
        kind, j = KINDS[i], i // 3
        tag = f"l{i}_{kind}_b"
        rider = _chip_exchange_rider([rs_p[n] for n in riding_names]) if riding_names else None
        dy, d_gate = _gate_bwd(dx, ys[i], row(gate, i), tag + "_gate")
        w_out = {"sg": sg_out(j), "swa": swa_out, "rwkv": rw_out}[kind]
        du = _matmul(dy, w_out, "nt", tag + "_du")
        dw_out = _matmul(us[i], dy, "tn", tag + "_dwout").reshape(N_CHIPS, D // N_CHIPS, D)
        if kind == "sg":
            (dp, dlg, dlb, dws, dbs), ridden = _ridden(
                _sg_bwd(ps[i], row(full["sg_ln_g"], j), row(full["sg_ln_b"], j), sg_w_spatial[j], bs_t[j], du,
                        tag + "_mix", rider), rider)
            gfull["sg_ln_g"][j], gfull["sg_ln_b"][j] = dlg[0], dlb[0]
            gfull["sg_w_spatial"][j], gfull["sg_b_spatial"][j] = dws, dbs[:, :SG_GROUPS].T
            gbig[f"sg_w_in{j}"] = _matmul(hs[i], dp, "tn", tag + "_dwin", blocked=True)
            gbig[f"sg_w_out{j}"] = dw_out
            dh, dh2 = _matmul(dp, sg_in(j), "nt", tag + "_dh", blocked=True), None
            mine = [f"sg_w_in{j}", f"sg_w_out{j}"]
        elif kind == "swa":
            (dp, dsk), ridden = _ridden(_swa_bwd(ps[i], cos, sin, sink_row, du, tag + "_mix", rider), rider)
            gfull["swa_sinks"] = dsk[:, :N_HEADS]
            gbig["swa_w_in"] = _matmul(hs[i], dp, "tn", tag + "_dwin", blocked=True)
            gbig["swa_w_out"] = dw_out
            dh, dh2 = _matmul(dp, swa_in, "nt", tag + "_dh", blocked=True), None
            mine = ["swa_w_in", "swa_w_out"]
        else:
            dpm, dpl, rg, ridden = _rwkv_mixer_bwd(ps[i][0], ps[i][1], rw_prm, rw_saved, du, tag, rider)
            mine = ["rwkv_w_in", "rwkv_w_out"]
            dw_main = _matmul(hs[i], dpm, "tn", tag + "_dwin")
            dw_lo = _matmul(hs[i], dpl, "tn", tag + "_dwin_lo")
            dw_main = _head_major(dw_main.reshape(D, 4, D)).reshape(D, RW_MAIN)
            dw_in = jnp.concatenate([dw_main, _lo_cols_inv(dw_lo)], axis=1)
            gbig["rwkv_w_in"] = dw_in.reshape(D, N_CHIPS, -1).transpose(1, 0, 2)
            gbig["rwkv_w_out"] = _rows_head_major(dw_out.reshape(D, D)).reshape(dw_out.shape)
            dmu_main = _head_major(rg["mu_main"].reshape(4, D)).reshape(1, RW_MAIN)
            gfull["rwkv_mu"] = jnp.concatenate([dmu_main, _lo_cols_inv(rg["mu_lo"])], axis=1)
            gfull["rwkv_w0"], gfull["rwkv_a0"] = _head_major(rg["w0"]), _head_major(rg["a0"])
            gfull["rwkv_w_lora"], gfull["rwkv_a_lora"] = _head_major(rg["wl"])[None, :LORA], _head_major(rg["al"])[None, :LORA]
            gfull["rwkv_k_k"], gfull["rwkv_k_a"] = _param_compact_inv(rg["kkp"])[None], _param_compact_inv(rg["kap"])[None]
            gfull["rwkv_r_k"] = _param_compact_inv(rg["rkp"]).reshape(1, N_HEADS, HEAD)
            gfull["rwkv_gn_g"], gfull["rwkv_gn_b"] = _param_compact_inv(rg["gn_g"])[None], _param_compact_inv(rg["gn_b"])[None]
            dh, dh2 = _matmul(dpm, rw_main, "nt", tag + "_dh"), _matmul(dpl, rw_lo, "nt", tag + "_dh_lo")
        rs_p.update(zip(riding_names, ridden[:len(riding_names)]))
        rs_q.update(zip(riding_names, ridden[len(riding_names):]))
        if i == 0:
            for n in ("sg_ln_g", "sg_ln_b"):
                gfull[n] = jnp.stack(gfull[n])
            small = jnp.concatenate([_chip_blocks(gfull[n], axis) for n, axis in SMALL.items()], axis=1)
            small_rows = _round_up(small.shape[1], 2 * 16 * LANES) // LANES
            small = jnp.pad(small, ((0, 0), (0, small_rows * LANES - small.shape[1])))
            gbig["small"] = small.reshape(N_CHIPS, small_rows, LANES)
            mine = mine + ["small"]
        gs = [gbig[n].reshape(N_CHIPS, 2, gbig[n].shape[1] // 2, gbig[n].shape[2]) for n in mine]
        pair_rider = _pair_exchange_rider(gs)
        (dx, dg, dsh, dsc), gots = _norm_mod_bwd(xs[i], row(norm_g, i), row(shift, i), row(scale, i), dh, dx, tag + "_norm",
                                                 dh2, pair_rider)
        d_norm_g[i] = dg[0]
        d_mod[i] = jnp.concatenate([dsh[0], dsc[0], d_gate[0]])
        rs_p.update(zip(mine, _rs_pair_sums(gots[:len(gs)], gots[len(gs):], ci, f"rs{i}")))
        riding_names = mine
    for n in ("sg_w_spatial", "sg_b_spatial"):
        gfull[n] = jnp.stack(gfull[n])
    gfull["norm_g"], gfull["final_norm_g"] = jnp.stack(d_norm_g), d_final_g[0]

    exchanged = _rs_chip_exchange([rs_p[n] for n in riding_names], "rs0_chip_x")
    rs_p.update(zip(riding_names, exchanged[:len(riding_names)]))
    rs_q.update(zip(riding_names, exchanged[len(riding_names):]))
    rs_names = sorted(rs_p)
    rs_out = _rs_finish([rs_p[n] for n in rs_names], [rs_q[n] for n in rs_names], chip, ci, "rs")
    red = {n: r.reshape(-1, r.shape[2]) for n, r in zip(rs_names, rs_out)}
    grads = {"sg_w_in": jnp.stack([red["sg_w_in0"], red["sg_w_in1"]]), "sg_w_out": jnp.stack([red["sg_w_out0"], red["sg_w_out1"]])}
    for n in ("swa_w_in", "swa_w_out", "rwkv_w_in", "rwkv_w_out"):
        grads[n] = red[n][None]
    small_red, off = red["small"].reshape(-1), 0
    for n in SMALL:
        grads[n] = small_red[off:off + w[n].size].reshape(w[n].shape)
        off += w[n].size

    rep_flat = jnp.concatenate([gfull[n].reshape(-1) for n in REPLICATED])
    rep_rows = _round_up(rep_flat.shape[0], 32 * RS_W) // RS_W
    rep_sum = _all_reduce_replicated(_pad_to(rep_flat, rep_rows * RS_W).reshape(rep_rows, RS_W), "rep").reshape(-1)
    off = 0
    for n in REPLICATED:
        grads[n] = rep_sum[off:off + w[n].size].reshape(w[n].shape)
        off += w[n].size
    dmod_all = _all_gather8(jnp.stack(d_mod).reshape(DEPTH * 3 * D // RS_W, RS_W), "gather_dmod")
    grads["mod_b"] = _sum_leading(dmod_all, "sum_dmod").reshape(DEPTH, 3 * D)
    dmod_all = dmod_all.reshape(N_DEV, DEPTH, 3 * D)
    dmod_cols = lax.dynamic_slice_in_dim(dmod_all, chip * MOD_COLS, MOD_COLS, axis=2).transpose(1, 0, 2)
    grads["mod_w"] = _mod_w_grad(c_all, dmod_cols, "mod_w_grad")

    deltas, new_m, new_v = {}, {}, {}
    for n in W_NAMES:
        deltas[n], new_m[n], new_v[n] = _adamw(w[n], grads[n], given["m_" + n], given["v_" + n], "adamw_" + n)
    return (loss, dx[None], *[grads[n] for n in W_NAMES], *[deltas[n] for n in W_NAMES],
            *[new_m[n] for n in W_NAMES], *[new_v[n] for n in W_NAMES])
```

```python
import functools
import math

import jax
import jax.numpy as jnp
from jax import lax
from jax.experimental import pallas as pl
from jax.experimental.pallas import tpu as pltpu

F32 = jnp.float32
BF16 = jnp.bfloat16
HIGHEST = lax.Precision.HIGHEST

D = 2048
DEPTH = 4
CHUNK = 128
SG_GROUPS = 16
HEAD = 64
N_HEADS = D // HEAD
KV_HEADS = 4
KVW = KV_HEADS * HEAD
ROPE_THETA = 10000.0
LORA = 96
LORA_PAD = 128
DECAY_SCALE = math.exp(-0.5)
GN_EPS = 64e-5
RMS_EPS = 1e-6
LN_EPS = 1e-5
ADAM_LR, ADAM_B1, ADAM_B2, ADAM_EPS, ADAM_WD, ADAM_STEP = 0.001, 0.9, 0.999, 1e-08, 0.01, 10
LANES = 128
SUB = 8
NEG = -1e30
VMEM_LIMIT = 56 * 1024 * 1024

MESHT = pl.DeviceIdType.MESH


def _cparams(*sem):
    return pltpu.CompilerParams(dimension_semantics=sem, vmem_limit_bytes=VMEM_LIMIT)


class _Rider:
    def __init__(self, arrays, n_sems, start, finish):
        self.arrays, self.n_sems, self.start, self.finish = list(arrays), n_sems, start, finish


def _ridden(res, rider):
    return (res, []) if rider is None else res


def _compute_call(body, args, *, name, grid, in_specs, out_specs, out_shape, semantics, scratch_shapes=(), rider=None):
    if rider is None:
        return pl.pallas_call(body, name=name, grid=grid, in_specs=in_specs, out_specs=out_specs, out_shape=out_shape,
                              scratch_shapes=list(scratch_shapes), compiler_params=_cparams(*semantics))(*args)
    single = not isinstance(out_shape, (list, tuple))
    o_specs, o_shapes = ([out_specs], [out_shape]) if single else (list(out_specs), list(out_shape))
    n_in, n_out, n_r, n_scr = len(in_specs), len(o_specs), len(rider.arrays), len(scratch_shapes)

    def with_rider(*refs):
        ins, outs = refs[:n_in], refs[n_in + n_r:n_in + n_r + n_out]
        ridden = refs[n_in + n_r + n_out:n_in + 2 * n_r + n_out]
        scratch, (send_sems, recv_sems) = refs[n_in + 2 * n_r + n_out:-2], refs[-2:]
        ids = [pl.program_id(d) for d in range(len(grid))]
        first = functools.reduce(jnp.logical_and, [i == 0 for i in ids])
        last = functools.reduce(jnp.logical_and, [i == g - 1 for i, g in zip(ids, grid)])

        @pl.when(first)
        def _():
            rider.start(ridden, send_sems, recv_sems)

        body(*ins, *outs, *scratch)

        @pl.when(last)
        def _():
            rider.finish(ridden, send_sems, recv_sems)

    any_spec = pl.BlockSpec(memory_space=pl.ANY)
    res = pl.pallas_call(
        with_rider, name=name, grid=grid, in_specs=list(in_specs) + [any_spec] * n_r, out_specs=o_specs + [any_spec] * n_r,
        out_shape=o_shapes + [jax.ShapeDtypeStruct(a.shape, a.dtype) for a in rider.arrays],
        input_output_aliases={n_in + i: n_out + i for i in range(n_r)},
        scratch_shapes=list(scratch_shapes) + [pltpu.SemaphoreType.DMA((rider.n_sems,))] * 2,
        compiler_params=_cparams(*["arbitrary"] * len(grid)),
    )(*args, *rider.arrays)
    return (res[0] if single else list(res[:n_out])), list(res[n_out:])


_NN = (((1,), (0,)), ((), ()))
_NT = (((1,), (1,)), ((), ()))
_TN = (((0,), (0,)), ((), ()))


def _dg(a, b, dims):
    return lax.dot_general(a, b, dims, preferred_element_type=F32)


@jax.custom_vjp
def _bdot_nn(a, b):
    return _dg(a.astype(BF16), b.astype(BF16), _NN)


def _bdot_nn_fwd(a, b):
    a, b = a.astype(BF16), b.astype(BF16)
    return _dg(a, b, _NN), (a, b)


def _bdot_nn_bwd(res, ct):
    a, b = res
    ct = ct.astype(BF16)
    return _dg(ct, b, _NT), _dg(a, ct, _TN)


_bdot_nn.defvjp(_bdot_nn_fwd, _bdot_nn_bwd)


@jax.custom_vjp
def _bdot_nt(a, b):
    return _dg(a.astype(BF16), b.astype(BF16), _NT)


def _bdot_nt_fwd(a, b):
    a, b = a.astype(BF16), b.astype(BF16)
    return _dg(a, b, _NT), (a, b)


def _bdot_nt_bwd(res, ct):
    a, b = res
    ct = ct.astype(BF16)
    return _dg(ct, b, _NN), _dg(ct, a, _TN)


_bdot_nt.defvjp(_bdot_nt_fwd, _bdot_nt_bwd)


def _tile(n, cap):
    if n <= cap:
        return n
    return max(d for d in range(LANES, cap + 1, LANES) if n % d == 0)


def _matmul(a, b, form, name, out_dtype=F32, blocked=False, rider=None, tm=1024, tn=512, tk=4096):
    if form == "nn":
        (m, k), n = a.shape, (N_CHIPS * b.shape[2] if blocked else b.shape[1])
    elif form == "nt":
        m, k, n = a.shape[0], a.shape[1], (b.shape[1] if blocked else b.shape[0])
    else:
        (k, m), n = a.shape, b.shape[1]
    per_chip = (k if form == "nt" else n) // N_CHIPS
    if blocked and form == "nt":
        tk = _tile(per_chip, tk)
    elif blocked:
        tn = _tile(per_chip, tn)
    tm, tn, tk = _tile(m, tm), _tile(n, tn), _tile(k, tk)
    assert m % tm == 0 and n % tn == 0 and k % tk == 0, (name, a.shape, b.shape)
    nk = k // tk
    dims = {"nn": _NN, "nt": _NT, "tn": _TN}[form]
    a_spec = pl.BlockSpec((tk, tm), lambda i, j, l: (l, i)) if form == "tn" else pl.BlockSpec((tm, tk), lambda i, j, l: (i, l))
    b_spec = pl.BlockSpec((tn, tk), lambda i, j, l: (j, l)) if form == "nt" else pl.BlockSpec((tk, tn), lambda i, j, l: (l, j))
    o_spec = pl.BlockSpec((tm, tn), lambda i, j, l: (i, j))
    o_shape = (m, n)
    if blocked and form == "nn":
        pc = per_chip // tn
        b_spec = pl.BlockSpec((None, tk, tn), lambda i, j, l: (j // pc, l, j % pc))
    elif blocked and form == "nt":
        pc = per_chip // tk
        b_spec = pl.BlockSpec((None, tn, tk), lambda i, j, l: (l // pc, j, l % pc))
    elif blocked:
        pc = per_chip // tn
        o_spec = pl.BlockSpec((None, tm, tn), lambda i, j, l: (j // pc, i, j % pc))
        o_shape = (N_CHIPS, m, per_chip)

    def body(a_ref, b_ref, o_ref, acc_ref):
        part = _dg(a_ref[...], b_ref[...], dims)
        if nk == 1:
            o_ref[...] = part.astype(out_dtype)
        else:
            l = pl.program_id(2)

            @pl.when(l == 0)
            def _():
                acc_ref[...] = part

            @pl.when(l > 0)
            def _():
                acc_ref[...] += part

            @pl.when(l == nk - 1)
            def _():
                o_ref[...] = acc_ref[...].astype(out_dtype)

    return _compute_call(
        body, (a, b), name=name, grid=(m // tm, n // tn, nk),
        in_specs=[a_spec, b_spec], out_specs=o_spec, out_shape=jax.ShapeDtypeStruct(o_shape, out_dtype),
        scratch_shapes=[pltpu.VMEM((tm, tn) if nk > 1 else (8, LANES), F32)],
        semantics=("parallel", "parallel", "arbitrary"), rider=rider)


def _out_proj_resid(u, w_out, x, gate, name, tm=1024, tn=512):
    (m, k), n = u.shape, w_out.shape[1]

    def body(u_ref, w_ref, x_ref, g_ref, y_ref, xn_ref):
        y = _dg(u_ref[...], w_ref[...], _NN)
        y_ref[...] = y
        xn_ref[...] = x_ref[...] + g_ref[...] * y

    tile = pl.BlockSpec((tm, tn), lambda i, j: (i, j))
    out = jax.ShapeDtypeStruct((m, n), F32)
    return pl.pallas_call(
        body, name=name, grid=(m // tm, n // tn),
        in_specs=[pl.BlockSpec((tm, k), lambda i, j: (i, 0)), pl.BlockSpec((k, tn), lambda i, j: (0, j)), tile,
                  pl.BlockSpec((1, tn), lambda i, j: (0, j))],
        out_specs=[tile, tile], out_shape=[out, out], compiler_params=_cparams("parallel", "parallel"),
    )(u, w_out, x, gate)


TB_NORM = 256


def _f_norm_mod(x, g, shift, scale):
    xn = x * lax.rsqrt(jnp.mean(x * x, axis=-1, keepdims=True) + RMS_EPS)
    return (xn * g) * (1.0 + scale) + shift


def _row_spec(width, tb=TB_NORM):
    return pl.BlockSpec((tb, width), lambda i: (i, 0))


def _vec_spec(width, rows=1):
    return pl.BlockSpec((rows, width), lambda i: (0, 0))


def _norm_mod_fwd(x, g, shift, scale, name):
    t = x.shape[0]

    def body(x_ref, g_ref, sh_ref, sc_ref, h_ref):
        h_ref[...] = _f_norm_mod(x_ref[...], g_ref[...], sh_ref[...], sc_ref[...]).astype(BF16)

    return pl.pallas_call(
        body, name=name, grid=(t // TB_NORM,),
        in_specs=[_row_spec(D), _vec_spec(D), _vec_spec(D), _vec_spec(D)], out_specs=_row_spec(D),
        out_shape=jax.ShapeDtypeStruct((t, D), BF16), compiler_params=_cparams("parallel"),
    )(x, g, shift, scale)


def _accum(ref, val, first):
    @pl.when(first)
    def _():
        ref[...] = val

    @pl.when(jnp.logical_not(first))
    def _():
        ref[...] += val


def _norm_mod_bwd(x, g, shift, scale, dh, dx_res, name, dh2=None, rider=None):
    t = x.shape[0]
    dhs = [dh] if dh2 is None else [dh, dh2]

    def body(x_ref, g_ref, sh_ref, sc_ref, dr_ref, *refs):
        dh_refs, (dx_ref, dg_ref, dsh_ref, dsc_ref) = refs[:len(dhs)], refs[len(dhs):]
        _, vjp = jax.vjp(_f_norm_mod, x_ref[...], g_ref[...], sh_ref[...], sc_ref[...])
        dh_all = dh_refs[0][...]
        for r in dh_refs[1:]:
            dh_all = dh_all + r[...]
        dx, dg, dsh, dsc = vjp(dh_all)
        dx_ref[...] = dx + dr_ref[...]
        first = pl.program_id(0) == 0
        _accum(dg_ref, dg, first)
        _accum(dsh_ref, dsh, first)
        _accum(dsc_ref, dsc, first)

    vec = jax.ShapeDtypeStruct((1, D), F32)
    return _compute_call(
        body, (x, g, shift, scale, dx_res, *dhs), name=name, grid=(t // TB_NORM,),
        in_specs=[_row_spec(D), _vec_spec(D), _vec_spec(D), _vec_spec(D), _row_spec(D)] + [_row_spec(D)] * len(dhs),
        out_specs=[_row_spec(D), _vec_spec(D), _vec_spec(D), _vec_spec(D)],
        out_shape=[jax.ShapeDtypeStruct((t, D), F32), vec, vec, vec], semantics=("arbitrary",), rider=rider)


def _gate_bwd(dx, y, gate, name):
    t = dx.shape[0]

    def body(dx_ref, y_ref, g_ref, dy_ref, dg_ref):
        dxv = dx_ref[...]
        dy_ref[...] = (dxv * g_ref[...]).astype(BF16)
        _accum(dg_ref, jnp.sum(dxv * y_ref[...], axis=0, keepdims=True), pl.program_id(0) == 0)

    return pl.pallas_call(
        body, name=name, grid=(t // TB_NORM,),
        in_specs=[_row_spec(D), _row_spec(D), _vec_spec(D)], out_specs=[_row_spec(D), _vec_spec(D)],
        out_shape=[jax.ShapeDtypeStruct((t, D), BF16), jax.ShapeDtypeStruct((1, D), F32)],
        compiler_params=_cparams("arbitrary"),
    )(dx, y, gate)


def _f_final(x, g, target):
    xn = x * lax.rsqrt(jnp.mean(x * x, axis=-1, keepdims=True) + RMS_EPS)
    err = xn * g - target
    return 0.5 * jnp.sum(jnp.mean(err * err, axis=-1, keepdims=True), axis=0, keepdims=True)


def _final_loss_grad(x, g, target, name):
    t = x.shape[0]

    def body(x_ref, g_ref, t_ref, loss_ref, dx_ref, dg_ref):
        loss, vjp = jax.vjp(_f_final, x_ref[...], g_ref[...], t_ref[...])
        dx, dg, _ = vjp(jnp.ones((1, 1), F32))
        dx_ref[...] = dx
        first = pl.program_id(0) == 0
        _accum(dg_ref, dg, first)
        _accum(loss_ref, jnp.broadcast_to(loss, (1, LANES)), first)

    return pl.pallas_call(
        body, name=name, grid=(t // TB_NORM,),
        in_specs=[_row_spec(D), _vec_spec(D), _row_spec(D)],
        out_specs=[_vec_spec(LANES), _row_spec(D), _vec_spec(D)],
        out_shape=[jax.ShapeDtypeStruct((1, LANES), F32), jax.ShapeDtypeStruct((t, D), F32), jax.ShapeDtypeStruct((1, D), F32)],
        compiler_params=_cparams("arbitrary"),
    )(x, g, target)


def _group_selector():
    gi = lax.broadcasted_iota(jnp.int32, (LANES, D), 0)
    ci = lax.broadcasted_iota(jnp.int32, (LANES, D), 1)
    return (ci // (D // SG_GROUPS) == gi).astype(F32)


def _f_sg(p, ln_g, ln_b, w_s, bs_t):
    u, v, z = p[:, :D], p[:, D:2 * D], p[:, 2 * D:]
    u = jax.nn.gelu(u)
    vf = jax.nn.gelu(v)
    mean = jnp.mean(vf, axis=-1, keepdims=True)
    var = jnp.mean(jnp.square(vf - mean), axis=-1, keepdims=True)
    vn = (vf - mean) * lax.rsqrt(var + LN_EPS) * ln_g + ln_b
    ti = lax.broadcasted_iota(jnp.int32, (CHUNK, CHUNK), 0)
    si = lax.broadcasted_iota(jnp.int32, (CHUNK, CHUNK), 1)
    causal = si <= ti
    cg = D // SG_GROUPS
    f = jnp.concatenate(
        [_bdot_nn(jnp.where(causal, w_s[g], 0.0), vn[:, g * cg:(g + 1) * cg]) for g in range(SG_GROUPS)], axis=1)
    f = f + jnp.dot(bs_t, _group_selector(), precision=HIGHEST, preferred_element_type=F32)
    return u * f * jax.nn.silu(z)


def _sg_specs():
    return [pl.BlockSpec((CHUNK, 3 * D), lambda i: (i, 0)), _vec_spec(D), _vec_spec(D),
            pl.BlockSpec((SG_GROUPS, CHUNK, CHUNK), lambda i: (0, 0, 0)), _vec_spec(LANES, CHUNK)]


def _sg_fwd(p, ln_g, ln_b, w_s, bs_t, name, rider=None):
    t = p.shape[0]

    def body(p_ref, lg_ref, lb_ref, w_ref, b_ref, o_ref):
        o_ref[...] = _f_sg(p_ref[...], lg_ref[...], lb_ref[...], w_ref[...], b_ref[...]).astype(BF16)

    return _compute_call(
        body, (p, ln_g, ln_b, w_s, bs_t), name=name, grid=(t // CHUNK,), in_specs=_sg_specs(),
        out_specs=_row_spec(D, CHUNK), out_shape=jax.ShapeDtypeStruct((t, D), BF16), semantics=("parallel",), rider=rider)


def _sg_bwd(p, ln_g, ln_b, w_s, bs_t, dout, name, rider=None):
    t = p.shape[0]

    def body(p_ref, lg_ref, lb_ref, w_ref, b_ref, do_ref, dp_ref, dlg_ref, dlb_ref, dw_ref, db_ref):
        _, vjp = jax.vjp(_f_sg, p_ref[...], lg_ref[...], lb_ref[...], w_ref[...], b_ref[...])
        dp, dlg, dlb, dw, db = vjp(do_ref[...])
        dp_ref[...] = dp.astype(BF16)
        first = pl.program_id(0) == 0
        _accum(dlg_ref, dlg, first)
        _accum(dlb_ref, dlb, first)
        _accum(dw_ref, dw, first)
        _accum(db_ref, db, first)

    vec = jax.ShapeDtypeStruct((1, D), F32)
    return _compute_call(
        body, (p, ln_g, ln_b, w_s, bs_t, dout), name=name, grid=(t // CHUNK,), in_specs=_sg_specs() + [_row_spec(D, CHUNK)],
        out_specs=[pl.BlockSpec((CHUNK, 3 * D), lambda i: (i, 0)), _vec_spec(D), _vec_spec(D),
                   pl.BlockSpec((SG_GROUPS, CHUNK, CHUNK), lambda i: (0, 0, 0)), _vec_spec(LANES, CHUNK)],
        out_shape=[jax.ShapeDtypeStruct((t, 3 * D), BF16), vec, vec,
                   jax.ShapeDtypeStruct((SG_GROUPS, CHUNK, CHUNK), F32), jax.ShapeDtypeStruct((CHUNK, LANES), F32)],
        semantics=("arbitrary",), rider=rider)


SWA_COLS = 2 * D + 2 * KVW
KV_BLOCK = 2 * KVW


def _lane_roll(x, shift):
    return pltpu.roll(x, shift, 1)


def _rot_half(x):
    w = x.shape[1]
    lane = lax.broadcasted_iota(jnp.int32, x.shape, 1)
    return jnp.where(lane % HEAD < HEAD // 2, -_lane_roll(x, w - HEAD // 2), _lane_roll(x, HEAD // 2))


@jax.custom_vjp
def _rope(x, cos, sin):
    return x * cos + _rot_half(x) * sin


def _rope_fwd(x, cos, sin):
    return _rope(x, cos, sin), (cos, sin)


def _rope_bwd(res, ct):
    cos, sin = res
    return ct * cos - _rot_half(ct) * sin, jnp.zeros_like(cos), jnp.zeros_like(sin)


_rope.defvjp(_rope_fwd, _rope_bwd)


@jax.custom_vjp
def _swap_halves(x):
    return _lane_roll(x, HEAD)


_swap_halves.defvjp(lambda x: (_lane_roll(x, HEAD), None), lambda _, ct: (_lane_roll(ct, HEAD),))


def _f_swa(pq, pkv, cos, sin, cosp, sinp, sink_row, valid):
    reps = D // LANES
    q = _rope(pq[:, :D], jnp.tile(cos, (1, reps)), jnp.tile(sin, (1, reps))) * (HEAD ** -0.5)
    k = _rope(pq[:, D:D + KVW], jnp.tile(cos, (1, KVW // LANES)), jnp.tile(sin, (1, KVW // LANES)))
    kp = _rope(pkv[:, :KVW], jnp.tile(cosp, (1, KVW // LANES)), jnp.tile(sinp, (1, KVW // LANES)))
    v, vp, z = pq[:, D + KVW:D + 2 * KVW], pkv[:, KVW:], pq[:, D + 2 * KVW:]
    kcat = jnp.concatenate([kp, k], axis=0)
    vcat = jnp.concatenate([vp, v], axis=0)
    lane = lax.broadcasted_iota(jnp.int32, (2 * CHUNK, LANES), 1)
    lo = lane < HEAD
    hlane = lax.broadcasted_iota(jnp.int32, (1, LANES), 1)

    def halves(cat, g):
        blk = cat[:, (g // 2) * LANES:(g // 2 + 1) * LANES]
        other = _swap_halves(blk)
        if g % 2 == 0:
            return jnp.where(lo, blk, 0.0), jnp.where(lo, 0.0, other)
        return jnp.where(lo, other, 0.0), jnp.where(lo, 0.0, blk)

    pairs = N_HEADS // KV_HEADS // 2
    valid_g = jnp.tile(valid, (pairs, 1))

    def probs(s, heads):
        sink = jnp.concatenate(
            [jnp.broadcast_to(jnp.sum(jnp.where(hlane == h, sink_row, 0.0), axis=1, keepdims=True), (CHUNK, 1))
             for h in heads], axis=0)
        s = jnp.where(valid_g, s, NEG)
        m = lax.stop_gradient(jnp.maximum(jnp.max(s, axis=1, keepdims=True), sink))
        e = jnp.exp(s - m)
        return e / (jnp.sum(e, axis=1, keepdims=True) + jnp.exp(sink - m))

    outs = []
    for g in range(KV_HEADS):
        k_lo, k_hi = halves(kcat, g)
        v_lo, v_hi = halves(vcat, g)
        tiles = range(g * pairs, (g + 1) * pairs)
        qg = jnp.concatenate([q[:, j * LANES:(j + 1) * LANES] for j in tiles], axis=0)
        p_a = probs(_bdot_nt(qg, k_lo), [2 * j for j in tiles])
        p_b = probs(_bdot_nt(qg, k_hi), [2 * j + 1 for j in tiles])
        og = _bdot_nn(p_a, v_lo) + _bdot_nn(p_b, v_hi)
        outs += [og[n * CHUNK:(n + 1) * CHUNK] for n in range(pairs)]
    return jnp.concatenate(outs, axis=1) * jax.nn.silu(z)


def _swa_valid(block):
    qi = lax.broadcasted_iota(jnp.int32, (CHUNK, 2 * CHUNK), 0)
    kj = lax.broadcasted_iota(jnp.int32, (CHUNK, 2 * CHUNK), 1)
    rel = qi + CHUNK - kj
    return (rel >= 0) & (rel < CHUNK) & ((kj >= CHUNK) | (block > 0))


def _swa_specs(blk):
    prev = lambda i: jnp.maximum(blk(i) - 1, 0)
    kv_col = D // KV_BLOCK
    return [pl.BlockSpec((CHUNK, SWA_COLS), lambda i: (blk(i), 0)),
            pl.BlockSpec((CHUNK, KV_BLOCK), lambda i: (prev(i), kv_col)),
            pl.BlockSpec((CHUNK, LANES), lambda i: (blk(i), 0)), pl.BlockSpec((CHUNK, LANES), lambda i: (blk(i), 0)),
            pl.BlockSpec((CHUNK, LANES), lambda i: (prev(i), 0)), pl.BlockSpec((CHUNK, LANES), lambda i: (prev(i), 0)),
            _vec_spec(LANES)]


def _swa_fwd(p, cos, sin, sink_row, name, rider=None):
    t = p.shape[0]

    def body(pq_ref, pkv_ref, c_ref, s_ref, cp_ref, sp_ref, sk_ref, o_ref):
        valid = _swa_valid(pl.program_id(0))
        o_ref[...] = _f_swa(pq_ref[...], pkv_ref[...], c_ref[...], s_ref[...], cp_ref[...], sp_ref[...],
                            sk_ref[...], valid).astype(BF16)

    return _compute_call(
        body, (p, p, cos, sin, cos, sin, sink_row), name=name, grid=(t // CHUNK,), in_specs=_swa_specs(lambda i: i),
        out_specs=_row_spec(D, CHUNK), out_shape=jax.ShapeDtypeStruct((t, D), BF16), semantics=("parallel",), rider=rider)


def _swa_bwd(p, cos, sin, sink_row, dout, name, rider=None):
    t = p.shape[0]
    nb = t // CHUNK
    blk = lambda i: nb - 1 - i

    def body(pq_ref, pkv_ref, c_ref, s_ref, cp_ref, sp_ref, sk_ref, do_ref, dp_ref, dsk_ref, pend_ref):
        i = pl.program_id(0)
        valid = _swa_valid(blk(i))
        f = functools.partial(_f_swa, valid=valid)
        _, vjp = jax.vjp(f, pq_ref[...], pkv_ref[...], c_ref[...], s_ref[...], cp_ref[...], sp_ref[...], sk_ref[...])
        dpq, dpkv, _, _, _, _, dsk = vjp(do_ref[...])

        @pl.when(i == 0)
        def _():
            pend_ref[...] = jnp.zeros_like(pend_ref)

        dp_ref[...] = jnp.concatenate(
            [dpq[:, :D], dpq[:, D:D + KV_BLOCK] + pend_ref[...], dpq[:, D + KV_BLOCK:]], axis=1).astype(BF16)
        pend_ref[...] = dpkv
        _accum(dsk_ref, dsk, i == 0)

    return _compute_call(
        body, (p, p, cos, sin, cos, sin, sink_row, dout), name=name, grid=(nb,),
        in_specs=_swa_specs(blk) + [pl.BlockSpec((CHUNK, D), lambda i: (blk(i), 0))],
        out_specs=[pl.BlockSpec((CHUNK, SWA_COLS), lambda i: (blk(i), 0)), _vec_spec(LANES)],
        out_shape=[jax.ShapeDtypeStruct((t, SWA_COLS), BF16), jax.ShapeDtypeStruct((1, LANES), F32)],
        scratch_shapes=[pltpu.VMEM((CHUNK, KV_BLOCK), F32)], semantics=("arbitrary",), rider=rider)


RW_MAIN = 4 * D
RW_LO = 2 * LORA_PAD
VM = LANES // N_HEADS
VD = HEAD // VM
S_ROWS = VD * HEAD
TB_RW = 128
TB_K = 32
TB_SCAN = 16


def _dim_major(a):
    return a.reshape(a.shape[:-1] + (N_HEADS, HEAD)).swapaxes(-1, -2).reshape(a.shape)


def _head_major(a):
    return a.reshape(a.shape[:-1] + (HEAD, N_HEADS)).swapaxes(-1, -2).reshape(a.shape)


def _param_compact(w):
    return _dim_major(w).reshape(VD, LANES)


def _param_compact_inv(pc):
    return _head_major(pc.reshape(-1))


def _f_rwkv_lora(xs_lo, w0, a0, wl, al):
    decay = jnp.exp(-DECAY_SCALE * jax.nn.sigmoid(w0 + _bdot_nn(jnp.tanh(xs_lo[:, :LORA_PAD]), wl)))
    a = jax.nn.sigmoid(a0 + _bdot_nn(xs_lo[:, LORA_PAD:], al))
    return decay, a


def _prev_rows_spec(width, tb):
    return pl.BlockSpec((8, width), lambda i: (jnp.maximum(i * (tb // 8) - 1, 0), 0))


def _token_shift_lerp(p, prev8, mu, first):
    rows = lax.broadcasted_iota(jnp.int32, p.shape, 0)
    prev = jnp.where(first, 0.0, prev8[7:8, :])
    shifted = jnp.where(rows == 0, prev, pltpu.roll(p, 1, 0))
    return p + (shifted - p) * mu


def _store_compact(ref, val):
    for j in range(VD):
        ref[:, j, :] = val[:, j * LANES:(j + 1) * LANES]


def _load_flat(ref, rows=slice(None)):
    if len(ref.shape) == 2:
        return ref[rows, :]
    return jnp.concatenate([ref[rows, j, :] for j in range(VD)], axis=1)


def _flat_spec(a, tb):
    return _row_spec(a.shape[1], tb) if a.ndim == 2 else _k_spec(VD, tb)


def _rwkv_pre_fwd(p_main, p_lo, mu_main, mu_lo, w0, a0, wl, al, name):
    t = p_main.shape[0]
    tb = TB_RW

    def body(pm_ref, pmp_ref, pl_ref, plp_ref, mm_ref, ml_ref, w0_ref, a0_ref, wl_ref, al_ref,
             r_ref, k_ref, v_ref, dec_ref, a_ref, z_ref, xl_ref):
        first = pl.program_id(0) == 0
        xs = _token_shift_lerp(pm_ref[...], pmp_ref[...], mm_ref[...], first)
        for n, ref in enumerate((r_ref, k_ref, v_ref)):
            _store_compact(ref, xs[:, n * D:(n + 1) * D])
        z_ref[...] = xs[:, 3 * D:]
        xs_lo = _token_shift_lerp(pl_ref[...], plp_ref[...], ml_ref[...], first)
        xl_ref[...] = xs_lo
        decay, a = _f_rwkv_lora(xs_lo, w0_ref[...], a0_ref[...], wl_ref[...], al_ref[...])
        _store_compact(dec_ref, decay)
        _store_compact(a_ref, a)

    cl = jax.ShapeDtypeStruct((t, VD, LANES), F32)
    return pl.pallas_call(
        body, name=name, grid=(t // tb,),
        in_specs=[_row_spec(RW_MAIN, tb), _prev_rows_spec(RW_MAIN, tb), _row_spec(RW_LO, tb), _prev_rows_spec(RW_LO, tb),
                  _vec_spec(RW_MAIN), _vec_spec(RW_LO), _vec_spec(D), _vec_spec(D),
                  _vec_spec(D, LORA_PAD), _vec_spec(D, LORA_PAD)],
        out_specs=[_k_spec(VD, tb)] * 5 + [_row_spec(D, tb), _row_spec(RW_LO, tb)],
        out_shape=[cl] * 5 + [jax.ShapeDtypeStruct((t, D), F32), jax.ShapeDtypeStruct((t, RW_LO), F32)],
        compiler_params=_cparams("parallel"),
    )(p_main, p_main, p_lo, p_lo, mu_main, mu_lo, w0, a0, wl, al)


def _rwkv_lora_bwd(xs_lo, w0, a0, wl, al, ddecay, da, name):
    t = xs_lo.shape[0]
    tb = TB_NORM

    def body(x_ref, w0_ref, a0_ref, wl_ref, al_ref, dd_ref, da_ref, dx_ref, dw0_ref, da0_ref, dwl_ref, dal_ref):
        _, vjp = jax.vjp(_f_rwkv_lora, x_ref[...], w0_ref[...], a0_ref[...], wl_ref[...], al_ref[...])
        dx, dw0, da0, dwl, dal = vjp((_load_flat(dd_ref), _load_flat(da_ref)))
        dx_ref[...] = dx
        first = pl.program_id(0) == 0
        _accum(dw0_ref, dw0, first)
        _accum(da0_ref, da0, first)
        _accum(dwl_ref, dwl, first)
        _accum(dal_ref, dal, first)

    vec = jax.ShapeDtypeStruct((1, D), F32)
    lor = jax.ShapeDtypeStruct((LORA_PAD, D), F32)
    return pl.pallas_call(
        body, name=name, grid=(t // tb,),
        in_specs=[_row_spec(RW_LO), _vec_spec(D), _vec_spec(D), _vec_spec(D, LORA_PAD), _vec_spec(D, LORA_PAD),
                  _k_spec(VD, tb), _k_spec(VD, tb)],
        out_specs=[_row_spec(RW_LO), _vec_spec(D), _vec_spec(D), _vec_spec(D, LORA_PAD), _vec_spec(D, LORA_PAD)],
        out_shape=[jax.ShapeDtypeStruct((t, RW_LO), F32), vec, vec, lor, lor], compiler_params=_cparams("arbitrary"),
    )(xs_lo, w0, a0, wl, al, ddecay, da)


def _lerp_bwd(p, dxs_groups, mu, name):
    t, width = p.shape
    tb = TB_RW
    nb = t // tb
    parts = [a for group in dxs_groups for a in group]

    def body(p_ref, pp_ref, mu_ref, *refs):
        d_refs, (dp_ref, dmu_ref) = refs[:2 * len(parts)], refs[2 * len(parts):]
        i = pl.program_id(0)

        def columns(k):
            pick = (lambda r: _load_flat(r, slice(0, 1))) if k else _load_flat
            vals, at = [], 0
            for group in dxs_groups:
                vals.append(functools.reduce(jnp.add, [pick(d_refs[2 * (at + n) + k]) for n in range(len(group))]))
                at += len(group)
            return jnp.concatenate(vals, axis=1)

        pv, dv, mu_v = p_ref[...], columns(0), mu_ref[...]
        rows = lax.broadcasted_iota(jnp.int32, pv.shape, 0)
        prev = jnp.where(i == 0, 0.0, pp_ref[7:8, :])
        shifted = jnp.where(rows == 0, prev, pltpu.roll(pv, 1, 0))
        nxt = jnp.where(i == nb - 1, 0.0, columns(1))
        d_next = jnp.where(rows == tb - 1, nxt, pltpu.roll(dv, tb - 1, 0))
        dp_ref[...] = (dv * (1.0 - mu_v) + d_next * mu_v).astype(BF16)
        _accum(dmu_ref, jnp.sum(dv * (shifted - pv), axis=0, keepdims=True), i == 0)

    d_specs = []
    for a in parts:
        after = lambda i, nd=a.ndim: (jnp.minimum((i + 1) * (tb // 8), t // 8 - 1),) + (0,) * (nd - 1)
        d_specs += [_flat_spec(a, tb), pl.BlockSpec((8,) + a.shape[1:], after)]
    return pl.pallas_call(
        body, name=name, grid=(nb,),
        in_specs=[_row_spec(width, tb), _prev_rows_spec(width, tb), _vec_spec(width)] + d_specs,
        out_specs=[_row_spec(width, tb), _vec_spec(width)],
        out_shape=[jax.ShapeDtypeStruct((t, width), BF16), jax.ShapeDtypeStruct((1, width), F32)],
        compiler_params=_cparams("arbitrary"),
    )(p, p, mu, *[a for a in parts for _ in range(2)])


def _lane_group_sum2d(x):
    x = x + pltpu.roll(x, N_HEADS, 1)
    return x + pltpu.roll(x, 2 * N_HEADS, 1)


@jax.custom_vjp
def _lane_group_sum(x):
    return _lane_group_sum2d(x.reshape(-1, LANES)).reshape(x.shape)


_lane_group_sum.defvjp(lambda x: (_lane_group_sum(x), None), lambda _, ct: (_lane_group_sum(ct),))


def _head_sum(x):
    return _lane_group_sum(jnp.sum(x, axis=1, keepdims=True))


def _f_kprep(k, a, r, kkp, kap, rkp):
    kk = k * kkp
    kk = kk / jnp.maximum(jnp.sqrt(_head_sum(kk * kk)), 1e-12)
    k2 = k * (1.0 + (a - 1.0) * kap)
    return kk, k2, kk * a, _head_sum(r * k2 * rkp)


def _k_spec(rows=HEAD, tb=TB_K):
    return pl.BlockSpec((tb, rows, LANES), lambda i: (i, 0, 0))


def _kparam_spec(rows=HEAD):
    return pl.BlockSpec((rows, LANES), lambda i: (0, 0))


def _lane_group(shape):
    return lax.broadcasted_iota(jnp.int32, shape, len(shape) - 1) // N_HEADS


def _store_k_layout(ref, xc):
    x2 = xc.reshape(-1, LANES)
    group = _lane_group(x2.shape)
    for q in range(VM):
        one = jnp.where(group == q, x2, 0.0)
        ref[:, pl.ds(q, VD, stride=VM), :] = _lane_group_sum2d(one).reshape(xc.shape)


def _load_compact(ref):
    acc = None
    for q in range(VM):
        rows = _lane_group_sum(ref[:, pl.ds(q, VD, stride=VM), :])
        part = jnp.where(_lane_group(rows.shape) == q, rows, 0.0)
        acc = part if acc is None else acc + part
    return acc


def _rwkv_kprep_fwd(k, a, r, w, kkp, kap, rkp, name):
    t = k.shape[0]

    def body(k_ref, a_ref, r_ref, w_ref, kkp_ref, kap_ref, rkp_ref, kk_ref, k2_ref, b_ref, r4_ref, w4_ref, rk_ref):
        rv = r_ref[...]
        kk, k2, b, rk_ref[...] = _f_kprep(k_ref[...], a_ref[...], rv, kkp_ref[...], kap_ref[...], rkp_ref[...])
        for ref, val in ((kk_ref, kk), (k2_ref, k2), (b_ref, b), (r4_ref, rv), (w4_ref, w_ref[...])):
            _store_k_layout(ref, val)

    big = jax.ShapeDtypeStruct((t, HEAD, LANES), F32)
    return pl.pallas_call(
        body, name=name, grid=(t // TB_K,),
        in_specs=[_k_spec(VD)] * 4 + [_kparam_spec(VD)] * 3, out_specs=[_k_spec()] * 5 + [_k_spec(1)],
        out_shape=[big] * 5 + [jax.ShapeDtypeStruct((t, 1, LANES), F32)], compiler_params=_cparams("parallel"),
    )(k, a, r, w, kkp, kap, rkp)


def _rwkv_kprep_bwd(k, a, r, kkp, kap, rkp, dkk, dk2, db, drk, dr_scan, dw_scan, name):
    t = k.shape[0]

    def body(k_ref, a_ref, r_ref, kkp_ref, kap_ref, rkp_ref, dkk_ref, dk2_ref, db_ref, drk_ref, drs_ref, dws_ref,
             dk_ref, da_ref, dr_ref, dw_ref, dkkp_ref, dkap_ref, drkp_ref):
        _, vjp = jax.vjp(_f_kprep, k_ref[...], a_ref[...], r_ref[...], kkp_ref[...], kap_ref[...], rkp_ref[...])
        dk, da, dr, dkkp, dkap, drkp = vjp((_load_compact(dkk_ref), _load_compact(dk2_ref), _load_compact(db_ref),
                                            drk_ref[...]))
        dk_ref[...] = dk
        da_ref[...] = da
        dr_ref[...] = dr + _load_compact(drs_ref)
        dw_ref[...] = _load_compact(dws_ref)
        first = pl.program_id(0) == 0
        _accum(dkkp_ref, dkkp, first)
        _accum(dkap_ref, dkap, first)
        _accum(drkp_ref, drkp, first)

    cl = jax.ShapeDtypeStruct((t, VD, LANES), F32)
    par = jax.ShapeDtypeStruct((VD, LANES), F32)
    return pl.pallas_call(
        body, name=name, grid=(t // TB_K,),
        in_specs=[_k_spec(VD)] * 3 + [_kparam_spec(VD)] * 3 + [_k_spec()] * 3 + [_k_spec(1), _k_spec(), _k_spec()],
        out_specs=[_k_spec(VD)] * 4 + [_kparam_spec(VD)] * 3,
        out_shape=[cl] * 4 + [par] * 3, compiler_params=_cparams("arbitrary"),
    )(k, a, r, kkp, kap, rkp, dkk, dk2, db, drk, dr_scan, dw_scan)


def _f_post(y, v, rk, g, b):
    mean = _lane_group_sum(jnp.sum(y, axis=1, keepdims=True)) * (1.0 / HEAD)
    yc = y - mean
    var = _lane_group_sum(jnp.sum(yc * yc, axis=1, keepdims=True)) * (1.0 / HEAD)
    return yc * lax.rsqrt(var + GN_EPS) * g + b + rk * v


def _rwkv_post_fwd(y, v, rk, g, b, name):
    t = y.shape[0]

    def body(y_ref, v_ref, rk_ref, g_ref, b_ref, o_ref):
        o_ref[...] = _f_post(y_ref[...], v_ref[...], rk_ref[...], g_ref[...], b_ref[...])

    return pl.pallas_call(
        body, name=name, grid=(t // TB_K,),
        in_specs=[_k_spec(VD), _k_spec(VD), _k_spec(1), _kparam_spec(VD), _kparam_spec(VD)], out_specs=_k_spec(VD),
        out_shape=jax.ShapeDtypeStruct((t, VD, LANES), F32), compiler_params=_cparams("parallel"),
    )(y, v, rk, g, b)


def _rwkv_post_bwd(y, v, rk, g, b, do, name):
    t = y.shape[0]

    def body(y_ref, v_ref, rk_ref, g_ref, b_ref, do_ref, dy_ref, dv_ref, drk_ref, dg_ref, db_ref):
        _, vjp = jax.vjp(_f_post, y_ref[...], v_ref[...], rk_ref[...], g_ref[...], b_ref[...])
        dy, dv, drk, dg, db = vjp(do_ref[...])
        dy_ref[...] = dy
        dv_ref[...] = dv
        drk_ref[...] = drk
        first = pl.program_id(0) == 0
        _accum(dg_ref, dg, first)
        _accum(db_ref, db, first)

    vl = jax.ShapeDtypeStruct((t, VD, LANES), F32)
    par = jax.ShapeDtypeStruct((VD, LANES), F32)
    return pl.pallas_call(
        body, name=name, grid=(t // TB_K,),
        in_specs=[_k_spec(VD), _k_spec(VD), _k_spec(1), _kparam_spec(VD), _kparam_spec(VD), _k_spec(VD)],
        out_specs=[_k_spec(VD), _k_spec(VD), _k_spec(1), _kparam_spec(VD), _kparam_spec(VD)],
        out_shape=[vl, vl, jax.ShapeDtypeStruct((t, 1, LANES), F32), par, par], compiler_params=_cparams("arbitrary"),
    )(y, v, rk, g, b, do)


def _f_gate(o, z):
    return o * jax.nn.silu(z)


def _rwkv_gate_fwd(o, z, name):
    t = z.shape[0]

    def body(o_ref, z_ref, u_ref):
        u_ref[...] = _f_gate(_load_flat(o_ref), z_ref[...]).astype(BF16)

    return pl.pallas_call(
        body, name=name, grid=(t // TB_NORM,), in_specs=[_k_spec(VD, TB_NORM), _row_spec(D)], out_specs=_row_spec(D),
        out_shape=jax.ShapeDtypeStruct((t, D), BF16), compiler_params=_cparams("parallel"),
    )(o, z)


def _rwkv_gate_bwd(o, z, du, name):
    t = z.shape[0]

    def body(o_ref, z_ref, du_ref, do_ref, dz_ref):
        _, vjp = jax.vjp(_f_gate, _load_flat(o_ref), z_ref[...])
        do, dz_ref[...] = vjp(du_ref[...])
        _store_compact(do_ref, do)

    return pl.pallas_call(
        body, name=name, grid=(t // TB_NORM,), in_specs=[_k_spec(VD, TB_NORM), _row_spec(D), _row_spec(D)],
        out_specs=[_k_spec(VD, TB_NORM), _row_spec(D)],
        out_shape=[jax.ShapeDtypeStruct((t, VD, LANES), F32), jax.ShapeDtypeStruct((t, D), F32)],
        compiler_params=_cparams("parallel"),
    )(o, z, du)


def _colsum(x):
    return jnp.sum(x, axis=0, keepdims=True)


def _rwkv_scan_fwd(r4, w4, k24, kk4, b4, v, name, rider=None):
    t = r4.shape[0]
    tb = TB_SCAN

    def body(r_ref, w_ref, k2_ref, kk_ref, b_ref, v_ref, y_ref, sall_ref, sa_ref, s_scr):
        @pl.when(pl.program_id(0) == 0)
        def _():
            s_scr[...] = jnp.zeros_like(s_scr)

        sall_ref[0] = s_scr[...]

        def step(tt, dst):
            kk = kk_ref[tt]
            sas = []
            for vd in range(VD):
                sa = _colsum(sall_ref[tt, pl.ds(vd * HEAD, HEAD), :] * kk)
                sa_ref[tt, pl.ds(vd, 1), :] = sa
                sas.append(sa)
            w, b, k2, r = w_ref[tt], b_ref[tt], k2_ref[tt], r_ref[tt]
            for vd in range(VD):
                rows = pl.ds(vd * HEAD, HEAD)
                s = sall_ref[tt, rows, :] * w - sas[vd] * b + v_ref[tt, pl.ds(vd, 1), :] * k2
                dst[rows, :] = s
                y_ref[tt, pl.ds(vd, 1), :] = _colsum(s * r)

        def loop_step(tt, carry):
            step(tt, sall_ref.at[tt + 1])
            return carry

        lax.fori_loop(0, tb - 1, loop_step, 0)
        step(tb - 1, s_scr)

    vl = jax.ShapeDtypeStruct((t, VD, LANES), F32)
    return _compute_call(
        body, (r4, w4, k24, kk4, b4, v), name=name, grid=(t // tb,),
        in_specs=[_k_spec(HEAD, tb)] * 5 + [_k_spec(VD, tb)],
        out_specs=[_k_spec(VD, tb), _k_spec(S_ROWS, tb), _k_spec(VD, tb)],
        out_shape=[vl, jax.ShapeDtypeStruct((t, S_ROWS, LANES), F32), vl],
        scratch_shapes=[pltpu.VMEM((S_ROWS, LANES), F32)], semantics=("arbitrary",), rider=rider)


def _rwkv_scan_bwd(dy, s_all, sa_all, r4, w4, k24, kk4, b4, v, name, rider=None):
    t = r4.shape[0]
    tb = TB_SCAN
    nb = t // tb
    blk = lambda i: nb - 1 - i

    def body(dy_ref, sall_ref, sa_ref, r_ref, w_ref, k2_ref, kk_ref, b_ref, v_ref,
             dr_ref, dw_ref, dk2_ref, dkk_ref, db_ref, dv_ref, ds_scr):
        @pl.when(pl.program_id(0) == 0)
        def _():
            ds_scr[...] = jnp.zeros_like(ds_scr)

        def step(j, carry):
            tt = tb - 1 - j
            vrow = lambda ref, vd: ref[tt, pl.ds(vd, 1), :]
            srows = lambda vd: pl.ds(vd * HEAD, HEAD)
            r, k2, b = r_ref[tt], k2_ref[tt], b_ref[tt]
            dsas = []
            for vd in range(VD):
                ds = ds_scr[srows(vd), :] + vrow(dy_ref, vd) * r
                ds_scr[srows(vd), :] = ds
                dv_ref[tt, pl.ds(vd, 1), :] = _colsum(ds * k2)
                dsas.append(-_colsum(ds * b))
            zero = jnp.zeros((HEAD, LANES), F32)
            dk2, q, sady, vdy = zero, zero, 0.0, 0.0
            for vd in range(VD):
                dyv = vrow(dy_ref, vd)
                dk2 = dk2 + ds_scr[srows(vd), :] * vrow(v_ref, vd)
                q = q + sall_ref[tt, srows(vd), :] * dyv
                sady = sady + vrow(sa_ref, vd) * dyv
                vdy = vdy + vrow(v_ref, vd) * dyv
            dk2_ref[tt] = dk2
            dr_ref[tt] = w_ref[tt] * q - b_ref[tt] * sady + k2_ref[tt] * vdy
            dw, dkk = zero, zero
            for vd in range(VD):
                sp = sall_ref[tt, srows(vd), :]
                dw = dw + ds_scr[srows(vd), :] * sp
                dkk = dkk + sp * dsas[vd]
            dw_ref[tt] = dw
            dkk_ref[tt] = dkk
            w, kk = w_ref[tt], kk_ref[tt]
            db = zero
            for vd in range(VD):
                ds = ds_scr[srows(vd), :]
                db = db - ds * vrow(sa_ref, vd)
                ds_scr[srows(vd), :] = ds * w + dsas[vd] * kk
            db_ref[tt] = db
            return carry

        lax.fori_loop(0, tb, step, 0)

    rk = lambda rows: pl.BlockSpec((tb, rows, LANES), lambda i: (blk(i), 0, 0))
    big = jax.ShapeDtypeStruct((t, HEAD, LANES), F32)
    return _compute_call(
        body, (dy, s_all, sa_all, r4, w4, k24, kk4, b4, v), name=name, grid=(nb,),
        in_specs=[rk(VD), rk(S_ROWS), rk(VD)] + [rk(HEAD)] * 5 + [rk(VD)],
        out_specs=[rk(HEAD)] * 5 + [rk(VD)],
        out_shape=[big] * 5 + [jax.ShapeDtypeStruct((t, VD, LANES), F32)],
        scratch_shapes=[pltpu.VMEM((S_ROWS, LANES), F32)], semantics=("arbitrary",), rider=rider)


def _rwkv_mixer_fwd(p_main, p_lo, prm, tag, rider):
    r, k, v, w, a, z, xs_lo = _rwkv_pre_fwd(p_main, p_lo, prm["mu_main"], prm["mu_lo"], prm["w0"], prm["a0"],
                                            prm["wl"], prm["al"], tag + "_pre")
    kk4, k24, b4, r4, w4, rk = _rwkv_kprep_fwd(k, a, r, w, prm["kkp"], prm["kap"], prm["rkp"], tag + "_kprep")
    (y, s_all, sa_all), ridden = _ridden(_rwkv_scan_fwd(r4, w4, k24, kk4, b4, v, tag + "_scan", rider), rider)
    o = _rwkv_post_fwd(y, v, rk, prm["gn_g"], prm["gn_b"], tag + "_post")
    u = _rwkv_gate_fwd(o, z, tag + "_gate")
    saved = dict(z=z, xs_lo=xs_lo, r=r, k=k, a=a, v=v, r4=r4, w4=w4, kk4=kk4, k24=k24, b4=b4, rk=rk,
                 y=y, s_all=s_all, sa_all=sa_all, o=o)
    return u, saved, ridden


def _rwkv_mixer_bwd(p_main, p_lo, prm, sv, du, tag, rider):
    do, dz = _rwkv_gate_bwd(sv["o"], sv["z"], du, tag + "_gate_b")
    dy, dv_post, drk, dgn_g, dgn_b = _rwkv_post_bwd(sv["y"], sv["v"], sv["rk"], prm["gn_g"], prm["gn_b"], do,
                                                    tag + "_post_b")
    (dr_s, dw_s, dk24, dkk4, db4, dv_scan), ridden = _ridden(_rwkv_scan_bwd(
        dy, sv["s_all"], sv["sa_all"], sv["r4"], sv["w4"], sv["k24"], sv["kk4"], sv["b4"], sv["v"], tag + "_scan_b", rider), rider)
    dk, da, dr, dw, dkkp, dkap, drkp = _rwkv_kprep_bwd(sv["k"], sv["a"], sv["r"], prm["kkp"], prm["kap"], prm["rkp"],
                                                       dkk4, dk24, db4, drk, dr_s, dw_s, tag + "_kprep_b")
    dxs_lo, dw0, da0, dwl, dal = _rwkv_lora_bwd(sv["xs_lo"], prm["w0"], prm["a0"], prm["wl"], prm["al"], dw, da,
                                                tag + "_lora_b")
    dxs_main = [[dr], [dk], [dv_post, dv_scan], [dz]]
    dp_main, dmu_main = _lerp_bwd(p_main, dxs_main, prm["mu_main"], tag + "_lerp_main_b")
    dp_lo, dmu_lo = _lerp_bwd(p_lo, [[dxs_lo]], prm["mu_lo"], tag + "_lerp_lo_b")
    grads = dict(mu_main=dmu_main, mu_lo=dmu_lo, w0=dw0, a0=da0, wl=dwl, al=dal, kkp=dkkp, kap=dkap, rkp=drkp,
                 gn_g=dgn_g, gn_b=dgn_b)
    return dp_main, dp_lo, grads, ridden


N_DEV = 8
N_CHIPS = 4
ANY = pl.BlockSpec(memory_space=pl.ANY)


def _place():
    return lax.axis_index("x"), lax.axis_index("y"), lax.axis_index("c")


def _remote(src, dst, send_sems, recv_sems, k, dev):
    return pltpu.make_async_remote_copy(src_ref=src, dst_ref=dst, send_sem=send_sems.at[k], recv_sem=recv_sems.at[k],
                                        device_id=dev, device_id_type=MESHT)


def _all_gather8(v, name):
    def body(buf_ref, out_ref, send_sems, recv_sems):
        del buf_ref
        x, y, c = _place()
        mine = out_ref.at[4 * x + 2 * y + c]
        peers = [(x ^ (k >> 2), y ^ ((k >> 1) & 1), c ^ (k & 1)) for k in range(1, N_DEV)]
        sends = [_remote(mine, mine, send_sems, recv_sems, k, peer) for k, peer in enumerate(peers)]
        for cp in sends:
            cp.start()
        for k, (px, py, pc) in enumerate(peers):
            _remote(mine, out_ref.at[4 * px + 2 * py + pc], send_sems, recv_sems, k, (x, y, c)).wait_recv()
        for cp in sends:
            cp.wait_send()

    return pl.pallas_call(
        body, name=name, in_specs=[ANY], out_specs=ANY, input_output_aliases={0: 0},
        out_shape=jax.ShapeDtypeStruct((N_DEV,) + v.shape, v.dtype),
        scratch_shapes=[pltpu.SemaphoreType.DMA((N_DEV - 1,)), pltpu.SemaphoreType.DMA((N_DEV - 1,))],
    )(jnp.broadcast_to(v[None], (N_DEV,) + v.shape))


def _other_chips(x, y):
    return [(1 - x, y), (x, 1 - y), (1 - x, 1 - y)]


GATHER_SEMS = 6


def _gather_buffer(v):
    return jnp.broadcast_to(v[None], (N_CHIPS,) + v.shape)


def _gather_start(bufs, send_sems, recv_sems):
    x, y, c = _place()
    for i, buf in enumerate(bufs):
        mine = buf.at[2 * x + y, c]
        for j, (cx, cy) in enumerate(_other_chips(x, y)):
            _remote(mine, mine, send_sems, recv_sems, GATHER_SEMS * i + j, (cx, cy, c)).start()


def _gather_finish(bufs, send_sems, recv_sems):
    x, y, c = _place()
    chips = _other_chips(x, y)
    passed = []
    for i, buf in enumerate(bufs):
        mine = buf.at[2 * x + y, c]
        for j, (cx, cy) in enumerate(chips):
            landed = buf.at[2 * cx + cy, c]
            _remote(mine, landed, send_sems, recv_sems, GATHER_SEMS * i + j, (x, y, c)).wait_recv()
            fwd = _remote(landed, landed, send_sems, recv_sems, GATHER_SEMS * i + 3 + j, (x, y, 1 - c))
            fwd.start()
            passed.append(fwd)
    for i, buf in enumerate(bufs):
        mine = buf.at[2 * x + y, c]
        for j, (cx, cy) in enumerate(chips):
            _remote(mine, buf.at[2 * cx + cy, 1 - c], send_sems, recv_sems, GATHER_SEMS * i + 3 + j, (x, y, c)).wait_recv()
            _remote(mine, mine, send_sems, recv_sems, GATHER_SEMS * i + j, (cx, cy, c)).wait_send()
    for fwd in passed:
        fwd.wait_send()


def _gather_rider(bufs):
    return _Rider(bufs, GATHER_SEMS * len(bufs), _gather_start, _gather_finish)


def _chip_gather(bufs, name):
    n = len(bufs)

    def body(*refs):
        out_refs, (send_sems, recv_sems) = refs[n:2 * n], refs[2 * n:]
        _gather_start(out_refs, send_sems, recv_sems)
        _gather_finish(out_refs, send_sems, recv_sems)

    return pl.pallas_call(
        body, name=name, in_specs=[ANY] * n, out_specs=[ANY] * n, input_output_aliases={i: i for i in range(n)},
        out_shape=[jax.ShapeDtypeStruct(b.shape, b.dtype) for b in bufs], scratch_shapes=_dma_sems(GATHER_SEMS * n),
    )(*bufs)


RS_W = 1024
RS_BLOCK_BYTES = 4 << 20


def _dma_sems(n):
    return [pltpu.SemaphoreType.DMA((n,)), pltpu.SemaphoreType.DMA((n,))]


def _pair_exchange_copies(refs, send_sems, recv_sems):
    n = len(refs) // 2
    x, y, c = _place()
    return [_remote(refs[i].at[s, 1 - c], refs[n + i].at[s], send_sems, recv_sems, N_CHIPS * i + s, (x, y, 1 - c))
            for i in range(n) for s in range(N_CHIPS)]


def _pair_exchange_start(refs, send_sems, recv_sems):
    for cp in _pair_exchange_copies(refs, send_sems, recv_sems):
        cp.start()


def _pair_exchange_finish(refs, send_sems, recv_sems):
    for cp in _pair_exchange_copies(refs, send_sems, recv_sems):
        cp.wait()


def _pair_exchange_rider(gs):
    landing = [lax.empty((N_CHIPS,) + g.shape[2:], g.dtype) for g in gs]
    return _Rider(list(gs) + landing, N_CHIPS * len(gs), _pair_exchange_start, _pair_exchange_finish)


def _rs_rows(rows, cols):
    cap = max(16, RS_BLOCK_BYTES // (N_CHIPS * 4 * cols))
    return rows if rows <= cap else max(d for d in range(16, cap + 1, 16) if rows % d == 0)


def _rs_pair_add(g, got, c_arr, name):
    _, _, rows, width = g.shape
    tr = _rs_rows(rows, width)

    def body(c_ref, g_ref, got_ref, p_ref):
        p_ref[...] = (g_ref[...] + got_ref[...]).astype(BF16)

    return pl.pallas_call(
        body, name=name,
        grid_spec=pltpu.PrefetchScalarGridSpec(
            num_scalar_prefetch=1, grid=(rows // tr,),
            in_specs=[pl.BlockSpec((N_CHIPS, None, tr, width), lambda i, c_ref: (0, c_ref[0], i, 0)),
                      pl.BlockSpec((N_CHIPS, tr, width), lambda i, c_ref: (0, i, 0))],
            out_specs=pl.BlockSpec((N_CHIPS, tr, width), lambda i, c_ref: (0, i, 0))),
        out_shape=jax.ShapeDtypeStruct((N_CHIPS, rows, width), BF16), compiler_params=_cparams("parallel"),
    )(c_arr, g, got)


def _chip_exchange_copies(refs, send_sems, recv_sems):
    n = len(refs) // 2
    x, y, c = _place()
    return [_remote(refs[i].at[2 * cx + cy], refs[n + i].at[j], send_sems, recv_sems, 3 * i + j, (cx, cy, c))
            for i in range(n) for j, (cx, cy) in enumerate(_other_chips(x, y))]


def _chip_exchange_start(refs, send_sems, recv_sems):
    for cp in _chip_exchange_copies(refs, send_sems, recv_sems):
        cp.start()


def _chip_exchange_finish(refs, send_sems, recv_sems):
    n = len(refs) // 2
    x, y, c = _place()
    for i in range(n):
        for j in range(3):
            _remote(refs[i].at[2 * x + y], refs[n + i].at[j], send_sems, recv_sems, 3 * i + j, (x, y, c)).wait_recv()
    for cp in _chip_exchange_copies(refs, send_sems, recv_sems):
        cp.wait_send()


def _chip_exchange_buffers(ps):
    return [lax.empty((3,) + p.shape[1:], p.dtype) for p in ps]


def _chip_exchange_rider(ps):
    return _Rider(list(ps) + _chip_exchange_buffers(ps), 3 * len(ps), _chip_exchange_start, _chip_exchange_finish)


def _rs_chip_exchange(ps, name):
    n = len(ps)

    def body(*refs):
        out_refs, (send_sems, recv_sems) = refs[2 * n:4 * n], refs[4 * n:]
        _chip_exchange_start(out_refs, send_sems, recv_sems)
        _chip_exchange_finish(out_refs, send_sems, recv_sems)

    arrays = list(ps) + _chip_exchange_buffers(ps)
    return pl.pallas_call(
        body, name=name, in_specs=[ANY] * (2 * n), out_specs=[ANY] * (2 * n),
        input_output_aliases={i: i for i in range(2 * n)},
        out_shape=[jax.ShapeDtypeStruct(a.shape, a.dtype) for a in arrays], scratch_shapes=_dma_sems(3 * n),
    )(*arrays)


def _rs_chip_add(p, q, idx, name):
    _, rows, width = q.shape
    tr = _rs_rows(rows, width)

    def body(idx_ref, p_ref, q_ref, r_ref):
        qv = q_ref[...].astype(F32)
        r_ref[...] = ((p_ref[...].astype(F32) + qv[0]) + qv[1]) + qv[2]

    return pl.pallas_call(
        body, name=name,
        grid_spec=pltpu.PrefetchScalarGridSpec(
            num_scalar_prefetch=1, grid=(rows // tr,),
            in_specs=[pl.BlockSpec((None, tr, width), lambda i, idx_ref: (idx_ref[0], i, 0)),
                      pl.BlockSpec((3, tr, width), lambda i, idx_ref: (0, i, 0))],
            out_specs=pl.BlockSpec((None, tr, width), lambda i, idx_ref: (idx_ref[1], i, 0))),
        out_shape=jax.ShapeDtypeStruct((2, rows, width), F32), compiler_params=_cparams("parallel"),
    )(idx, p, q)


def _rs_pair_share(rs, name):
    n = len(rs)

    def body(*refs):
        out_refs, (send_sems, recv_sems) = refs[n:2 * n], refs[2 * n:]
        x, y, c = _place()
        sends = [_remote(out_refs[i].at[c], out_refs[i].at[c], send_sems, recv_sems, i, (x, y, 1 - c)) for i in range(n)]
        for cp in sends:
            cp.start()
        for i in range(n):
            _remote(out_refs[i].at[c], out_refs[i].at[1 - c], send_sems, recv_sems, i, (x, y, c)).wait_recv()
        for cp in sends:
            cp.wait_send()

    return pl.pallas_call(
        body, name=name, in_specs=[ANY] * n, out_specs=[ANY] * n, input_output_aliases={i: i for i in range(n)},
        out_shape=[jax.ShapeDtypeStruct(r.shape, r.dtype) for r in rs], scratch_shapes=_dma_sems(n),
    )(*rs)


def _rs_pair_sums(gs, gots, core, tag):
    c_arr = core.astype(jnp.int32).reshape(1)
    return [_rs_pair_add(g, got, c_arr, f"{tag}_pair_add{i}") for i, (g, got) in enumerate(zip(gs, gots))]


def _rs_finish(ps, qs, chip, core, tag):
    idx = jnp.stack([chip, core]).astype(jnp.int32)
    rs = [_rs_chip_add(p, q, idx, f"{tag}_chip_add{i}") for i, (p, q) in enumerate(zip(ps, qs))]
    return _rs_pair_share(rs, tag + "_share")


def _sum_leading(a, name):
    n, rows, width = a.shape
    cap = max(8, RS_BLOCK_BYTES // (n * 4 * width))
    tr = rows if rows <= cap else max(d for d in range(8, cap + 1, 8) if rows % d == 0)

    def body(a_ref, o_ref):
        acc = a_ref[0]
        for d in range(1, n):
            acc = acc + a_ref[d]
        o_ref[...] = acc

    return pl.pallas_call(
        body, name=name, grid=(rows // tr,), in_specs=[pl.BlockSpec((n, tr, width), lambda i: (0, i, 0))],
        out_specs=pl.BlockSpec((tr, width), lambda i: (i, 0)), out_shape=jax.ShapeDtypeStruct((rows, width), F32),
        compiler_params=_cparams("parallel"),
    )(a)


def _pair_swap(v, name):
    def body(v_ref, got_ref, send_sems, recv_sems):
        x, y, c = _place()
        cp = _remote(v_ref, got_ref, send_sems, recv_sems, 0, (x, y, 1 - c))
        cp.start()
        cp.wait()

    return pl.pallas_call(body, name=name, in_specs=[ANY], out_specs=ANY, out_shape=jax.ShapeDtypeStruct(v.shape, v.dtype),
                          scratch_shapes=_dma_sems(1))(v)


def _all_reduce_replicated(v, name):
    rows, width = v.shape
    pair = _sum_leading(jnp.stack([v, _pair_swap(v, name + "_swap")]), name + "_pair_add")
    (gathered,) = _chip_gather([_gather_buffer(pair.reshape(2, rows // 2, width))], name + "_gather")
    return _sum_leading(gathered.reshape(N_CHIPS, rows, width), name + "_chip_add")


MOD_COLS = 3 * D // N_CHIPS
MOD_TK = 512


def _mod_partial(c_all, mod_w, name):
    nk = D // MOD_TK

    def body(c_ref, w_ref, o_ref):
        l = pl.program_id(1)
        part = _bdot_nn(jax.nn.silu(c_ref[...]), w_ref[0])
        _accum(o_ref.at[0], part, l == 0)

    return pl.pallas_call(
        body, name=name, grid=(DEPTH, nk),
        in_specs=[pl.BlockSpec((N_DEV, MOD_TK), lambda i, l: (0, l)), pl.BlockSpec((1, MOD_TK, MOD_COLS), lambda i, l: (i, l, 0))],
        out_specs=pl.BlockSpec((1, N_DEV, MOD_COLS), lambda i, l: (i, 0, 0)),
        out_shape=jax.ShapeDtypeStruct((DEPTH, N_DEV, MOD_COLS), F32), compiler_params=_cparams("parallel", "arbitrary"),
    )(c_all, mod_w)


def _mod_w_grad(c_all, dmod, name):
    def body(c_ref, d_ref, o_ref):
        o_ref[0] = _dg(jax.nn.silu(c_ref[...]).astype(BF16), d_ref[0].astype(BF16), _TN)

    return pl.pallas_call(
        body, name=name, grid=(DEPTH, D // MOD_TK),
        in_specs=[pl.BlockSpec((N_DEV, MOD_TK), lambda i, l: (0, l)), pl.BlockSpec((1, N_DEV, MOD_COLS), lambda i, l: (i, 0, 0))],
        out_specs=pl.BlockSpec((1, MOD_TK, MOD_COLS), lambda i, l: (i, l, 0)),
        out_shape=jax.ShapeDtypeStruct((DEPTH, D, MOD_COLS), F32), compiler_params=_cparams("parallel", "parallel"),
    )(c_all, dmod)


ADAM_BLOCK_BYTES = 1 << 20


def _adamw(w, g, m, v, name, rider=None):
    shape = w.shape
    cols = shape[-1]
    rows = w.size // cols
    w, g, m, v = (a.reshape(rows, cols) for a in (w, g, m, v))
    cap = max(8, ADAM_BLOCK_BYTES // (4 * cols))
    tr = rows if rows <= cap else max(d for d in range(8, cap + 1, 8) if rows % d == 0)
    c1 = 1.0 - ADAM_B1 ** ADAM_STEP
    c2 = 1.0 - ADAM_B2 ** ADAM_STEP

    def body(w_ref, g_ref, m_ref, v_ref, d_ref, nm_ref, nv_ref):
        gv = g_ref[...]
        mn = ADAM_B1 * m_ref[...] + (1.0 - ADAM_B1) * gv
        vn = ADAM_B2 * v_ref[...] + (1.0 - ADAM_B2) * (gv * gv)
        nm_ref[...] = mn
        nv_ref[...] = vn
        d_ref[...] = -ADAM_LR * ((mn / c1) / (jnp.sqrt(vn / c2) + ADAM_EPS) + ADAM_WD * w_ref[...])

    spec = pl.BlockSpec((tr, cols), lambda i: (i, 0))
    out = jax.ShapeDtypeStruct((rows, cols), F32)
    (d, nm, nv), ridden = _ridden(_compute_call(
        body, (w, g, m, v), name=name, grid=(rows // tr,), in_specs=[spec] * 4, out_specs=[spec] * 3, out_shape=[out] * 3,
        semantics=("parallel",), rider=rider), rider)
    res = (d.reshape(shape), nm.reshape(shape), nv.reshape(shape))
    return res if rider is None else (res, ridden)


W_NAMES = ("norm_g", "mod_w", "mod_b", "final_norm_g", "sg_w_in", "sg_w_out", "sg_ln_g", "sg_ln_b", "sg_w_spatial",
           "sg_b_spatial", "swa_w_in", "swa_w_out", "swa_sinks", "rwkv_w_in", "rwkv_w_out", "rwkv_mu", "rwkv_w0",
           "rwkv_w_lora", "rwkv_a0", "rwkv_a_lora", "rwkv_k_k", "rwkv_k_a", "rwkv_r_k", "rwkv_gn_g", "rwkv_gn_b")
SMALL = {"sg_ln_g": 1, "sg_ln_b": 1, "rwkv_mu": 1, "rwkv_w0": 1, "rwkv_w_lora": 2, "rwkv_a0": 1, "rwkv_a_lora": 2,
         "rwkv_k_k": 1, "rwkv_k_a": 1, "rwkv_gn_g": 1, "rwkv_gn_b": 1}
REPLICATED = ("norm_g", "final_norm_g", "sg_w_spatial", "sg_b_spatial", "swa_sinks", "rwkv_r_k")
KINDS = ("sg", "swa", "rwkv", "sg")


def _pad_to(flat, n):
    return jnp.pad(flat, (0, n - flat.shape[0]))


def _round_up(n, m):
    return -(-n // m) * m


def _join_shards(gathered, axis):
    return jnp.concatenate([gathered[s] for s in range(N_CHIPS)], axis=axis)


def _chip_blocks(full, axis):
    return jnp.stack(jnp.split(full, N_CHIPS, axis=axis)).reshape(N_CHIPS, -1)


def _weight_buffer(w):
    rows, cols = w.shape
    return _gather_buffer(w.astype(BF16).reshape(2, rows // 2, cols))


def _chip_shards(buf):
    return buf.reshape(N_CHIPS, -1, buf.shape[-1])


def _small_buffer(shards):
    flat = jnp.concatenate([shards[n].reshape(-1) for n in SMALL])
    rows = _round_up(flat.shape[0], 2 * 8 * LANES) // (2 * LANES)
    return _gather_buffer(_pad_to(flat, 2 * rows * LANES).reshape(2, rows, LANES))


def _unpack_small(buf, shards):
    got = buf.reshape(N_CHIPS, -1)
    out, off = {}, 0
    for n, axis in SMALL.items():
        size = shards[n].size
        out[n] = _join_shards(got[:, off:off + size].reshape((N_CHIPS,) + shards[n].shape), axis)
        off += size
    return out


def _lora_pad_rows(w):
    return jnp.pad(w, ((0, LORA_PAD - LORA), (0, 0)))


def _lo_cols(a):
    z = jnp.zeros(a.shape[:-1] + (LORA_PAD - LORA,), a.dtype)
    return jnp.concatenate([a[..., :LORA], z, a[..., LORA:], z], axis=-1)


def _lo_cols_inv(a):
    return jnp.concatenate([a[..., :LORA], a[..., LORA_PAD:LORA_PAD + LORA]], axis=-1)


def _rows_dim_major(w):
    return w.reshape(N_HEADS, HEAD, -1).swapaxes(0, 1).reshape(w.shape)


def _rows_head_major(w):
    return w.reshape(HEAD, N_HEADS, -1).swapaxes(0, 1).reshape(w.shape)


def kernel(x, c, positions, norm_g, mod_w, mod_b, final_norm_g, sg_w_in, sg_w_out, sg_ln_g, sg_ln_b, sg_w_spatial,
           sg_b_spatial, swa_w_in, swa_w_out, swa_sinks, rwkv_w_in, rwkv_w_out, rwkv_mu, rwkv_w0, rwkv_w_lora, rwkv_a0,
           rwkv_a_lora, rwkv_k_k, rwkv_k_a, rwkv_r_k, rwkv_gn_g, rwkv_gn_b, loss_target, m_norm_g, m_mod_w, m_mod_b,
           m_final_norm_g, m_sg_w_in, m_sg_w_out, m_sg_ln_g, m_sg_ln_b, m_sg_w_spatial, m_sg_b_spatial, m_swa_w_in,
           m_swa_w_out, m_swa_sinks, m_rwkv_w_in, m_rwkv_w_out, m_rwkv_mu, m_rwkv_w0, m_rwkv_w_lora, m_rwkv_a0,
           m_rwkv_a_lora, m_rwkv_k_k, m_rwkv_k_a, m_rwkv_r_k, m_rwkv_gn_g, m_rwkv_gn_b, v_norm_g, v_mod_w, v_mod_b,
           v_final_norm_g, v_sg_w_in, v_sg_w_out, v_sg_ln_g, v_sg_ln_b, v_sg_w_spatial, v_sg_b_spatial, v_swa_w_in,
           v_swa_w_out, v_swa_sinks, v_rwkv_w_in, v_rwkv_w_out, v_rwkv_mu, v_rwkv_w0, v_rwkv_w_lora, v_rwkv_a0,
           v_rwkv_a_lora, v_rwkv_k_k, v_rwkv_k_a, v_rwkv_r_k, v_rwkv_gn_g, v_rwkv_gn_b):
    given = dict(locals())
    w = {n: given[n] for n in W_NAMES}
    xi, yi, ci = _place()
    chip = 2 * xi + yi
    me = 4 * xi + 2 * yi + ci
    xs = [x[0]]

    c_all = _all_gather8(c, "gather_c")[:, 0, :]
    mod_part = _mod_partial(c_all, mod_w, "mod_fwd")
    mod_all = _all_gather8(mod_part, "gather_mod")[::2]
    mod_mine = lax.dynamic_index_in_dim(mod_all, me, axis=2, keepdims=False)
    mod = mod_mine.transpose(1, 0, 2).reshape(DEPTH, 3 * D) + mod_b
    shift, scale, gate = mod[:, :D], mod[:, D:2 * D], mod[:, 2 * D:]

    shards = {"sg_w_in0": sg_w_in[0], "sg_w_out0": sg_w_out[0], "swa_w_in": swa_w_in[0], "swa_w_out": swa_w_out[0],
              "rwkv_w_in": rwkv_w_in[0], "rwkv_w_out": rwkv_w_out[0], "sg_w_in1": sg_w_in[1], "sg_w_out1": sg_w_out[1]}
    bufs = {n: _weight_buffer(s) for n, s in shards.items()}
    fwd_riders = {(0, "in"): ["swa_w_in"], (0, "mix"): ["swa_w_out"], (1, "in"): ["rwkv_w_out"], (1, "mix"): ["rwkv_w_in"],
                  (2, "mix"): ["sg_w_in1", "sg_w_out1"]}
    bufs["sg_w_in0"], bufs["sg_w_out0"], small_buf = _chip_gather(
        [bufs["sg_w_in0"], bufs["sg_w_out0"], _small_buffer(w)], "gather_l0")
    full = _unpack_small(small_buf, w)

    def riding(i, where):
        names = fwd_riders.get((i, where))
        return names, (None if names is None else _gather_rider([bufs[n] for n in names]))

    def arrived(names, ridden):
        for n, b in zip(names or [], ridden):
            bufs[n] = b

    sg_in = lambda j: _chip_shards(bufs[f"sg_w_in{j}"])
    sg_out = lambda j: bufs[f"sg_w_out{j}"].reshape(D, D)
    mu = full["rwkv_mu"][0]
    rw_prm = dict(mu_main=_dim_major(mu[:RW_MAIN].reshape(4, D)).reshape(1, RW_MAIN), mu_lo=_lo_cols(mu[None, RW_MAIN:]),
                  w0=_dim_major(full["rwkv_w0"]), a0=_dim_major(full["rwkv_a0"]),
                  wl=_lora_pad_rows(_dim_major(full["rwkv_w_lora"][0])), al=_lora_pad_rows(_dim_major(full["rwkv_a_lora"][0])),
                  kkp=_param_compact(full["rwkv_k_k"][0]), kap=_param_compact(full["rwkv_k_a"][0]),
                  rkp=_param_compact(rwkv_r_k.reshape(-1)),
                  gn_g=_param_compact(full["rwkv_gn_g"][0]), gn_b=_param_compact(full["rwkv_gn_b"][0]))
    bs_t = [jnp.pad(sg_b_spatial[j].T, ((0, 0), (0, LANES - SG_GROUPS))) for j in range(2)]
    sink_row = jnp.pad(swa_sinks, ((0, 0), (0, LANES - N_HEADS)))
    inv_freq = ROPE_THETA ** (-jnp.arange(HEAD // 2, dtype=F32) / (HEAD // 2))
    ang = positions[0].astype(F32)[:, None] * inv_freq
    cos, sin = jnp.tile(jnp.cos(ang), (1, LANES * 2 // HEAD)), jnp.tile(jnp.sin(ang), (1, LANES * 2 // HEAD))

    def row(a, i):
        return a[i:i + 1]

    hs, ps, us, ys, rw_saved = [], [], [], [], None
    for i, kind in enumerate(KINDS):
        j = i // 3
        tag = f"l{i}_{kind}"
        h = _norm_mod_fwd(xs[i], row(norm_g, i), row(shift, i), row(scale, i), tag + "_norm")
        names_in, rider_in = riding(i, "in")
        names_mix, rider_mix = riding(i, "mix")
        if kind == "sg":
            p, ridden = _ridden(_matmul(h, sg_in(j), "nn", tag + "_in", blocked=True, rider=rider_in), rider_in)
            arrived(names_in, ridden)
            u, ridden = _ridden(_sg_fwd(p, row(full["sg_ln_g"], j), row(full["sg_ln_b"], j), sg_w_spatial[j], bs_t[j],
                                        tag + "_mix", rider_mix), rider_mix)
            w_out = sg_out(j)
        elif kind == "swa":
            swa_in, swa_out = _chip_shards(bufs["swa_w_in"]), bufs["swa_w_out"].reshape(D, D)
            p, ridden = _ridden(_matmul(h, swa_in, "nn", tag + "_in", blocked=True, rider=rider_in), rider_in)
            arrived(names_in, ridden)
            u, ridden = _ridden(_swa_fwd(p, cos, sin, sink_row, tag + "_mix", rider_mix), rider_mix)
            w_out = swa_out
        else:
            rw_in = _join_shards(_chip_shards(bufs["rwkv_w_in"]), axis=1)
            rw_main = _dim_major(rw_in[:, :RW_MAIN].reshape(D, 4, D)).reshape(D, RW_MAIN)
            rw_lo = _lo_cols(rw_in[:, RW_MAIN:])
            rw_out = _rows_dim_major(bufs["rwkv_w_out"].reshape(D, D))
            p = (_matmul(h, rw_main, "nn", tag + "_in"), _matmul(h, rw_lo, "nn", tag + "_in_lo"))
            u, rw_saved, ridden = _rwkv_mixer_fwd(p[0], p[1], rw_prm, tag, rider_mix)
            w_out = rw_out
        arrived(names_mix, ridden)
        y, x_next = _out_proj_resid(u, w_out, xs[i], row(gate, i), tag + "_out")
        xs.append(x_next)
        hs.append(h), ps.append(p), us.append(u), ys.append(y)

    loss_part, dx, d_final_g = _final_loss_grad(xs[DEPTH], final_norm_g[None], loss_target[0], "loss")
    loss = lax.psum(loss_part[0, 0], ("x", "y", "c"))

    gfull = {n: [None, None] for n in ("sg_ln_g", "sg_ln_b", "sg_w_spatial", "sg_b_spatial")}
    gbig = {}
    d_norm_g, d_mod = [None] * DEPTH, [None] * DEPTH
    rs_p, rs_q, riding_names = {}, {}, []

    for i in reversed(range(DEPTH)):
        kind, j = KINDS[i], i // 3
        tag = f"l{i}_{kind}_b"
        rider = _chip_exchange_rider([rs_p[n] for n in riding_names]) if riding_names else None
        dy, d_gate = _gate_bwd(dx, ys[i], row(gate, i), tag + "_gate")
        w_out = {"sg": sg_out(j), "swa": swa_out, "rwkv": rw_out}[kind]
        du = _matmul(dy, w_out, "nt", tag + "_du")
        dw_out = _matmul(us[i], dy, "tn", tag + "_dwout").reshape(N_CHIPS, D // N_CHIPS, D)
        if kind == "sg":
            (dp, dlg, dlb, dws, dbs), ridden = _ridden(
                _sg_bwd(ps[i], row(full["sg_ln_g"], j), row(full["sg_ln_b"], j), sg_w_spatial[j], bs_t[j], du,
                        tag + "_mix", rider), rider)
            gfull["sg_ln_g"][j], gfull["sg_ln_b"][j] = dlg[0], dlb[0]
            gfull["sg_w_spatial"][j], gfull["sg_b_spatial"][j] = dws, dbs[:, :SG_GROUPS].T
            gbig[f"sg_w_in{j}"] = _matmul(hs[i], dp, "tn", tag + "_dwin", blocked=True)
            gbig[f"sg_w_out{j}"] = dw_out
            dh, dh2 = _matmul(dp, sg_in(j), "nt", tag + "_dh", blocked=True), None
            mine = [f"sg_w_in{j}", f"sg_w_out{j}"]
        elif kind == "swa":
            (dp, dsk), ridden = _ridden(_swa_bwd(ps[i], cos, sin, sink_row, du, tag + "_mix", rider), rider)
            gfull["swa_sinks"] = dsk[:, :N_HEADS]
            gbig["swa_w_in"] = _matmul(hs[i], dp, "tn", tag + "_dwin", blocked=True)
            gbig["swa_w_out"] = dw_out
            dh, dh2 = _matmul(dp, swa_in, "nt", tag + "_dh", blocked=True), None
            mine = ["swa_w_in", "swa_w_out"]
        else:
            dpm, dpl, rg, ridden = _rwkv_mixer_bwd(ps[i][0], ps[i][1], rw_prm, rw_saved, du, tag, rider)
            mine = ["rwkv_w_in", "rwkv_w_out"]
            dw_main = _matmul(hs[i], dpm, "tn", tag + "_dwin")
            dw_lo = _matmul(hs[i], dpl, "tn", tag + "_dwin_lo")
            dw_main = _head_major(dw_main.reshape(D, 4, D)).reshape(D, RW_MAIN)
            dw_in = jnp.concatenate([dw_main, _lo_cols_inv(dw_lo)], axis=1)
            gbig["rwkv_w_in"] = dw_in.reshape(D, N_CHIPS, -1).transpose(1, 0, 2)
            gbig["rwkv_w_out"] = _rows_head_major(dw_out.reshape(D, D)).reshape(dw_out.shape)
            dmu_main = _head_major(rg["mu_main"].reshape(4, D)).reshape(1, RW_MAIN)
            gfull["rwkv_mu"] = jnp.concatenate([dmu_main, _lo_cols_inv(rg["mu_lo"])], axis=1)
            gfull["rwkv_w0"], gfull["rwkv_a0"] = _head_major(rg["w0"]), _head_major(rg["a0"])
            gfull["rwkv_w_lora"], gfull["rwkv_a_lora"] = _head_major(rg["wl"])[None, :LORA], _head_major(rg["al"])[None, :LORA]
            gfull["rwkv_k_k"], gfull["rwkv_k_a"] = _param_compact_inv(rg["kkp"])[None], _param_compact_inv(rg["kap"])[None]
            gfull["rwkv_r_k"] = _param_compact_inv(rg["rkp"]).reshape(1, N_HEADS, HEAD)
            gfull["rwkv_gn_g"], gfull["rwkv_gn_b"] = _param_compact_inv(rg["gn_g"])[None], _param_compact_inv(rg["gn_b"])[None]
            dh, dh2 = _matmul(dpm, rw_main, "nt", tag + "_dh"), _matmul(dpl, rw_lo, "nt", tag + "_dh_lo")
        rs_p.update(zip(riding_names, ridden[:len(riding_names)]))
        rs_q.update(zip(riding_names, ridden[len(riding_names):]))
        if i == 0:
            for n in ("sg_ln_g", "sg_ln_b"):
                gfull[n] = jnp.stack(gfull[n])
            small = jnp.concatenate([_chip_blocks(gfull[n], axis) for n, axis in SMALL.items()], axis=1)
            small_rows = _round_up(small.shape[1], 2 * 16 * LANES) // LANES
            small = jnp.pad(small, ((0, 0), (0, small_rows * LANES - small.shape[1])))
            gbig["small"] = small.reshape(N_CHIPS, small_rows, LANES)
            mine = mine + ["small"]
        gs = [gbig[n].reshape(N_CHIPS, 2, gbig[n].shape[1] // 2, gbig[n].shape[2]) for n in mine]
        pair_rider = _pair_exchange_rider(gs)
        (dx, dg, dsh, dsc), gots = _norm_mod_bwd(xs[i], row(norm_g, i), row(shift, i), row(scale, i), dh, dx, tag + "_norm",
                                                 dh2, pair_rider)
        d_norm_g[i] = dg[0]
        d_mod[i] = jnp.concatenate([dsh[0], dsc[0], d_gate[0]])
        rs_p.update(zip(mine, _rs_pair_sums(gots[:len(gs)], gots[len(gs):], ci, f"rs{i}")))
        riding_names = mine
    for n in ("sg_w_spatial", "sg_b_spatial"):
        gfull[n] = jnp.stack(gfull[n])
    gfull["norm_g"], gfull["final_norm_g"] = jnp.stack(d_norm_g), d_final_g[0]

    grads, deltas, new_m, new_v, red = {}, {}, {}, {}, {}

    def finish(names, tag):
        outs = _rs_finish([rs_p[n] for n in names], [rs_q[n] for n in names], chip, ci, tag)
        red.update({n: r.reshape(-1, r.shape[2]) for n, r in zip(names, outs)})

    def adamw(n, rider=None):
        res = _adamw(w[n], grads[n], given["m_" + n], given["v_" + n], "adamw_" + n, rider)
        (deltas[n], new_m[n], new_v[n]), ridden = _ridden(res, rider)
        return ridden

    def ride_exchange(names, on):
        ridden = adamw(on, _chip_exchange_rider([rs_p[n] for n in names]))
        rs_p.update(zip(names, ridden[:len(names)]))
        rs_q.update(zip(names, ridden[len(names):]))

    finish(sorted(set(rs_p) - set(riding_names)), "rs_a")
    for n in ("swa_w_in", "swa_w_out", "rwkv_w_in", "rwkv_w_out"):
        grads[n] = red[n][None]
    dmod_all = _all_gather8(jnp.stack(d_mod).reshape(DEPTH * 3 * D // RS_W, RS_W), "gather_dmod")
    grads["mod_b"] = _sum_leading(dmod_all, "sum_dmod").reshape(DEPTH, 3 * D)
    dmod_all = dmod_all.reshape(N_DEV, DEPTH, 3 * D)
    dmod_cols = lax.dynamic_slice_in_dim(dmod_all, chip * MOD_COLS, MOD_COLS, axis=2).transpose(1, 0, 2)
    grads["mod_w"] = _mod_w_grad(c_all, dmod_cols, "mod_w_grad")
    ride_exchange(["sg_w_in0"], on="mod_w")
    ride_exchange(["sg_w_out0", "small"], on="rwkv_w_in")
    finish(riding_names, "rs_b")
    grads["sg_w_in"] = jnp.stack([red["sg_w_in0"], red["sg_w_in1"]])
    grads["sg_w_out"] = jnp.stack([red["sg_w_out0"], red["sg_w_out1"]])
    small_red, off = red["small"].reshape(-1), 0
    for n in SMALL:
        grads[n] = small_red[off:off + w[n].size].reshape(w[n].shape)
        off += w[n].size

    rep_flat = jnp.concatenate([gfull[n].reshape(-1) for n in REPLICATED])
    rep_rows = _round_up(rep_flat.shape[0], 32 * RS_W) // RS_W
    rep_sum = _all_reduce_replicated(_pad_to(rep_flat, rep_rows * RS_W).reshape(rep_rows, RS_W), "rep").reshape(-1)
    off = 0
    for n in REPLICATED:
        grads[n] = rep_sum[off:off + w[n].size].reshape(w[n].shape)
        off += w[n].size

    for n in W_NAMES:
        if n not in deltas:
            adamw(n)
    return (loss, dx[None], *[grads[n] for n in W_NAMES], *[deltas[n] for n in W_NAMES],
            *[new_m[n] for n in W_NAMES], *[new_v[n] for n in W_NAMES])
```

```python
import functools
import math

import jax
import jax.numpy as jnp
from jax import lax
from jax.experimental import pallas as pl
from jax.experimental.pallas import tpu as pltpu

F32 = jnp.float32
BF16 = jnp.bfloat16
HIGHEST = lax.Precision.HIGHEST

D = 2048
DEPTH = 4
CHUNK = 128
SG_GROUPS = 16
HEAD = 64
N_HEADS = D // HEAD
KV_HEADS = 4
KVW = KV_HEADS * HEAD
ROPE_THETA = 10000.0
LORA = 96
LORA_PAD = 128
DECAY_SCALE = math.exp(-0.5)
GN_EPS = 64e-5
RMS_EPS = 1e-6
LN_EPS = 1e-5
ADAM_LR, ADAM_B1, ADAM_B2, ADAM_EPS, ADAM_WD, ADAM_STEP = 0.001, 0.9, 0.999, 1e-08, 0.01, 10
LANES = 128
SUB = 8
NEG = -1e30
VMEM_LIMIT = 56 * 1024 * 1024

MESHT = pl.DeviceIdType.MESH


def _cparams(*sem):
    return pltpu.CompilerParams(dimension_semantics=sem, vmem_limit_bytes=VMEM_LIMIT)


class _Rider:
    def __init__(self, arrays, n_sems, start, finish):
        self.arrays, self.n_sems, self.start, self.finish = list(arrays), n_sems, start, finish


def _ridden(res, rider):
    return (res, []) if rider is None else res


def _compute_call(body, args, *, name, grid, in_specs, out_specs, out_shape, semantics, scratch_shapes=(), rider=None):
    if rider is None:
        return pl.pallas_call(body, name=name, grid=grid, in_specs=in_specs, out_specs=out_specs, out_shape=out_shape,
                              scratch_shapes=list(scratch_shapes), compiler_params=_cparams(*semantics))(*args)
    single = not isinstance(out_shape, (list, tuple))
    o_specs, o_shapes = ([out_specs], [out_shape]) if single else (list(out_specs), list(out_shape))
    n_in, n_out, n_r, n_scr = len(in_specs), len(o_specs), len(rider.arrays), len(scratch_shapes)

    def with_rider(*refs):
        ins, outs = refs[:n_in], refs[n_in + n_r:n_in + n_r + n_out]
        ridden = refs[n_in + n_r + n_out:n_in + 2 * n_r + n_out]
        scratch, (send_sems, recv_sems) = refs[n_in + 2 * n_r + n_out:-2], refs[-2:]
        ids = [pl.program_id(d) for d in range(len(grid))]
        first = functools.reduce(jnp.logical_and, [i == 0 for i in ids])
        last = functools.reduce(jnp.logical_and, [i == g - 1 for i, g in zip(ids, grid)])

        @pl.when(first)
        def _():
            rider.start(ridden, send_sems, recv_sems)

        body(*ins, *outs, *scratch)

        @pl.when(last)
        def _():
            rider.finish(ridden, send_sems, recv_sems)

    any_spec = pl.BlockSpec(memory_space=pl.ANY)
    res = pl.pallas_call(
        with_rider, name=name, grid=grid, in_specs=list(in_specs) + [any_spec] * n_r, out_specs=o_specs + [any_spec] * n_r,
        out_shape=o_shapes + [jax.ShapeDtypeStruct(a.shape, a.dtype) for a in rider.arrays],
        input_output_aliases={n_in + i: n_out + i for i in range(n_r)},
        scratch_shapes=list(scratch_shapes) + [pltpu.SemaphoreType.DMA((rider.n_sems,))] * 2,
        compiler_params=_cparams(*["arbitrary"] * len(grid)),
    )(*args, *rider.arrays)
    return (res[0] if single else list(res[:n_out])), list(res[n_out:])


_NN = (((1,), (0,)), ((), ()))
_NT = (((1,), (1,)), ((), ()))
_TN = (((0,), (0,)), ((), ()))


def _dg(a, b, dims):
    return lax.dot_general(a, b, dims, preferred_element_type=F32)


@jax.custom_vjp
def _bdot_nn(a, b):
    return _dg(a.astype(BF16), b.astype(BF16), _NN)


def _bdot_nn_fwd(a, b):
    a, b = a.astype(BF16), b.astype(BF16)
    return _dg(a, b, _NN), (a, b)


def _bdot_nn_bwd(res, ct):
    a, b = res
    ct = ct.astype(BF16)
    return _dg(ct, b, _NT), _dg(a, ct, _TN)


_bdot_nn.defvjp(_bdot_nn_fwd, _bdot_nn_bwd)


@jax.custom_vjp
def _bdot_nt(a, b):
    return _dg(a.astype(BF16), b.astype(BF16), _NT)


def _bdot_nt_fwd(a, b):
    a, b = a.astype(BF16), b.astype(BF16)
    return _dg(a, b, _NT), (a, b)


def _bdot_nt_bwd(res, ct):
    a, b = res
    ct = ct.astype(BF16)
    return _dg(ct, b, _NN), _dg(ct, a, _TN)


_bdot_nt.defvjp(_bdot_nt_fwd, _bdot_nt_bwd)


def _tile(n, cap):
    if n <= cap:
        return n
    return max(d for d in range(LANES, cap + 1, LANES) if n % d == 0)


def _matmul(a, b, form, name, out_dtype=F32, blocked=False, rider=None, tm=1024, tn=512, tk=4096):
    if form == "nn":
        (m, k), n = a.shape, (N_CHIPS * b.shape[2] if blocked else b.shape[1])
    elif form == "nt":
        m, k, n = a.shape[0], a.shape[1], (b.shape[1] if blocked else b.shape[0])
    else:
        (k, m), n = a.shape, b.shape[1]
    per_chip = (k if form == "nt" else n) // N_CHIPS
    if blocked and form == "nt":
        tk = _tile(per_chip, tk)
    elif blocked:
        tn = _tile(per_chip, tn)
    tm, tn, tk = _tile(m, tm), _tile(n, tn), _tile(k, tk)
    assert m % tm == 0 and n % tn == 0 and k % tk == 0, (name, a.shape, b.shape)
    nk = k // tk
    dims = {"nn": _NN, "nt": _NT, "tn": _TN}[form]
    a_spec = pl.BlockSpec((tk, tm), lambda i, j, l: (l, i)) if form == "tn" else pl.BlockSpec((tm, tk), lambda i, j, l: (i, l))
    b_spec = pl.BlockSpec((tn, tk), lambda i, j, l: (j, l)) if form == "nt" else pl.BlockSpec((tk, tn), lambda i, j, l: (l, j))
    o_spec = pl.BlockSpec((tm, tn), lambda i, j, l: (i, j))
    o_shape = (m, n)
    if blocked and form == "nn":
        pc = per_chip // tn
        b_spec = pl.BlockSpec((None, tk, tn), lambda i, j, l: (j // pc, l, j % pc))
    elif blocked and form == "nt":
        pc = per_chip // tk
        b_spec = pl.BlockSpec((None, tn, tk), lambda i, j, l: (l // pc, j, l % pc))
    elif blocked:
        pc = per_chip // tn
        o_spec = pl.BlockSpec((None, tm, tn), lambda i, j, l: (j // pc, i, j % pc))
        o_shape = (N_CHIPS, m, per_chip)

    def body(a_ref, b_ref, o_ref, acc_ref):
        part = _dg(a_ref[...], b_ref[...], dims)
        if nk == 1:
            o_ref[...] = part.astype(out_dtype)
        else:
            l = pl.program_id(2)

            @pl.when(l == 0)
            def _():
                acc_ref[...] = part

            @pl.when(l > 0)
            def _():
                acc_ref[...] += part

            @pl.when(l == nk - 1)
            def _():
                o_ref[...] = acc_ref[...].astype(out_dtype)

    return _compute_call(
        body, (a, b), name=name, grid=(m // tm, n // tn, nk),
        in_specs=[a_spec, b_spec], out_specs=o_spec, out_shape=jax.ShapeDtypeStruct(o_shape, out_dtype),
        scratch_shapes=[pltpu.VMEM((tm, tn) if nk > 1 else (8, LANES), F32)],
        semantics=("parallel", "parallel", "arbitrary"), rider=rider)


def _out_proj_resid(u, w_out, x, gate, name, tm=1024, tn=512):
    (m, k), n = u.shape, w_out.shape[1]

    def body(u_ref, w_ref, x_ref, g_ref, y_ref, xn_ref):
        y = _dg(u_ref[...], w_ref[...], _NN)
        y_ref[...] = y
        xn_ref[...] = x_ref[...] + g_ref[...] * y

    tile = pl.BlockSpec((tm, tn), lambda i, j: (i, j))
    out = jax.ShapeDtypeStruct((m, n), F32)
    return pl.pallas_call(
        body, name=name, grid=(m // tm, n // tn),
        in_specs=[pl.BlockSpec((tm, k), lambda i, j: (i, 0)), pl.BlockSpec((k, tn), lambda i, j: (0, j)), tile,
                  pl.BlockSpec((1, tn), lambda i, j: (0, j))],
        out_specs=[tile, tile], out_shape=[out, out], compiler_params=_cparams("parallel", "parallel"),
    )(u, w_out, x, gate)


TB_NORM = 256


def _f_norm_mod(x, g, shift, scale):
    xn = x * lax.rsqrt(jnp.mean(x * x, axis=-1, keepdims=True) + RMS_EPS)
    return (xn * g) * (1.0 + scale) + shift


def _row_spec(width, tb=TB_NORM):
    return pl.BlockSpec((tb, width), lambda i: (i, 0))


def _vec_spec(width, rows=1):
    return pl.BlockSpec((rows, width), lambda i: (0, 0))


def _norm_mod_fwd(x, g, shift, scale, name):
    t = x.shape[0]

    def body(x_ref, g_ref, sh_ref, sc_ref, h_ref):
        h_ref[...] = _f_norm_mod(x_ref[...], g_ref[...], sh_ref[...], sc_ref[...]).astype(BF16)

    return pl.pallas_call(
        body, name=name, grid=(t // TB_NORM,),
        in_specs=[_row_spec(D), _vec_spec(D), _vec_spec(D), _vec_spec(D)], out_specs=_row_spec(D),
        out_shape=jax.ShapeDtypeStruct((t, D), BF16), compiler_params=_cparams("parallel"),
    )(x, g, shift, scale)


def _accum(ref, val, first):
    @pl.when(first)
    def _():
        ref[...] = val

    @pl.when(jnp.logical_not(first))
    def _():
        ref[...] += val


def _norm_mod_bwd(x, g, shift, scale, dh, dx_res, name, dh2=None, rider=None):
    t = x.shape[0]
    dhs = [dh] if dh2 is None else [dh, dh2]

    def body(x_ref, g_ref, sh_ref, sc_ref, dr_ref, *refs):
        dh_refs, (dx_ref, dg_ref, dsh_ref, dsc_ref) = refs[:len(dhs)], refs[len(dhs):]
        _, vjp = jax.vjp(_f_norm_mod, x_ref[...], g_ref[...], sh_ref[...], sc_ref[...])
        dh_all = dh_refs[0][...]
        for r in dh_refs[1:]:
            dh_all = dh_all + r[...]
        dx, dg, dsh, dsc = vjp(dh_all)
        dx_ref[...] = dx + dr_ref[...]
        first = pl.program_id(0) == 0
        _accum(dg_ref, dg, first)
        _accum(dsh_ref, dsh, first)
        _accum(dsc_ref, dsc, first)

    vec = jax.ShapeDtypeStruct((1, D), F32)
    return _compute_call(
        body, (x, g, shift, scale, dx_res, *dhs), name=name, grid=(t // TB_NORM,),
        in_specs=[_row_spec(D), _vec_spec(D), _vec_spec(D), _vec_spec(D), _row_spec(D)] + [_row_spec(D)] * len(dhs),
        out_specs=[_row_spec(D), _vec_spec(D), _vec_spec(D), _vec_spec(D)],
        out_shape=[jax.ShapeDtypeStruct((t, D), F32), vec, vec, vec], semantics=("arbitrary",), rider=rider)


def _gate_bwd(dx, y, gate, name):
    t = dx.shape[0]

    def body(dx_ref, y_ref, g_ref, dy_ref, dg_ref):
        dxv = dx_ref[...]
        dy_ref[...] = (dxv * g_ref[...]).astype(BF16)
        _accum(dg_ref, jnp.sum(dxv * y_ref[...], axis=0, keepdims=True), pl.program_id(0) == 0)

    return pl.pallas_call(
        body, name=name, grid=(t // TB_NORM,),
        in_specs=[_row_spec(D), _row_spec(D), _vec_spec(D)], out_specs=[_row_spec(D), _vec_spec(D)],
        out_shape=[jax.ShapeDtypeStruct((t, D), BF16), jax.ShapeDtypeStruct((1, D), F32)],
        compiler_params=_cparams("arbitrary"),
    )(dx, y, gate)


def _f_final(x, g, target):
    xn = x * lax.rsqrt(jnp.mean(x * x, axis=-1, keepdims=True) + RMS_EPS)
    err = xn * g - target
    return 0.5 * jnp.sum(jnp.mean(err * err, axis=-1, keepdims=True), axis=0, keepdims=True)


def _final_loss_grad(x, g, target, name):
    t = x.shape[0]

    def body(x_ref, g_ref, t_ref, loss_ref, dx_ref, dg_ref):
        loss, vjp = jax.vjp(_f_final, x_ref[...], g_ref[...], t_ref[...])
        dx, dg, _ = vjp(jnp.ones((1, 1), F32))
        dx_ref[...] = dx
        first = pl.program_id(0) == 0
        _accum(dg_ref, dg, first)
        _accum(loss_ref, jnp.broadcast_to(loss, (1, LANES)), first)

    return pl.pallas_call(
        body, name=name, grid=(t // TB_NORM,),
        in_specs=[_row_spec(D), _vec_spec(D), _row_spec(D)],
        out_specs=[_vec_spec(LANES), _row_spec(D), _vec_spec(D)],
        out_shape=[jax.ShapeDtypeStruct((1, LANES), F32), jax.ShapeDtypeStruct((t, D), F32), jax.ShapeDtypeStruct((1, D), F32)],
        compiler_params=_cparams("arbitrary"),
    )(x, g, target)


def _group_selector():
    gi = lax.broadcasted_iota(jnp.int32, (LANES, D), 0)
    ci = lax.broadcasted_iota(jnp.int32, (LANES, D), 1)
    return (ci // (D // SG_GROUPS) == gi).astype(F32)


def _f_sg(p, ln_g, ln_b, w_s, bs_t):
    u, v, z = p[:, :D], p[:, D:2 * D], p[:, 2 * D:]
    u = jax.nn.gelu(u)
    vf = jax.nn.gelu(v)
    mean = jnp.mean(vf, axis=-1, keepdims=True)
    var = jnp.mean(jnp.square(vf - mean), axis=-1, keepdims=True)
    vn = (vf - mean) * lax.rsqrt(var + LN_EPS) * ln_g + ln_b
    ti = lax.broadcasted_iota(jnp.int32, (CHUNK, CHUNK), 0)
    si = lax.broadcasted_iota(jnp.int32, (CHUNK, CHUNK), 1)
    causal = si <= ti
    cg = D // SG_GROUPS
    f = jnp.concatenate(
        [_bdot_nn(jnp.where(causal, w_s[g], 0.0), vn[:, g * cg:(g + 1) * cg]) for g in range(SG_GROUPS)], axis=1)
    f = f + jnp.dot(bs_t, _group_selector(), precision=HIGHEST, preferred_element_type=F32)
    return u * f * jax.nn.silu(z)


def _sg_specs():
    return [pl.BlockSpec((CHUNK, 3 * D), lambda i: (i, 0)), _vec_spec(D), _vec_spec(D),
            pl.BlockSpec((SG_GROUPS, CHUNK, CHUNK), lambda i: (0, 0, 0)), _vec_spec(LANES, CHUNK)]


def _sg_fwd(p, ln_g, ln_b, w_s, bs_t, name, rider=None):
    t = p.shape[0]

    def body(p_ref, lg_ref, lb_ref, w_ref, b_ref, o_ref):
        o_ref[...] = _f_sg(p_ref[...], lg_ref[...], lb_ref[...], w_ref[...], b_ref[...]).astype(BF16)

    return _compute_call(
        body, (p, ln_g, ln_b, w_s, bs_t), name=name, grid=(t // CHUNK,), in_specs=_sg_specs(),
        out_specs=_row_spec(D, CHUNK), out_shape=jax.ShapeDtypeStruct((t, D), BF16), semantics=("parallel",), rider=rider)


def _sg_bwd(p, ln_g, ln_b, w_s, bs_t, dout, name, rider=None):
    t = p.shape[0]

    def body(p_ref, lg_ref, lb_ref, w_ref, b_ref, do_ref, dp_ref, dlg_ref, dlb_ref, dw_ref, db_ref):
        _, vjp = jax.vjp(_f_sg, p_ref[...], lg_ref[...], lb_ref[...], w_ref[...], b_ref[...])
        dp, dlg, dlb, dw, db = vjp(do_ref[...])
        dp_ref[...] = dp.astype(BF16)
        first = pl.program_id(0) == 0
        _accum(dlg_ref, dlg, first)
        _accum(dlb_ref, dlb, first)
        _accum(dw_ref, dw, first)
        _accum(db_ref, db, first)

    vec = jax.ShapeDtypeStruct((1, D), F32)
    return _compute_call(
        body, (p, ln_g, ln_b, w_s, bs_t, dout), name=name, grid=(t // CHUNK,), in_specs=_sg_specs() + [_row_spec(D, CHUNK)],
        out_specs=[pl.BlockSpec((CHUNK, 3 * D), lambda i: (i, 0)), _vec_spec(D), _vec_spec(D),
                   pl.BlockSpec((SG_GROUPS, CHUNK, CHUNK), lambda i: (0, 0, 0)), _vec_spec(LANES, CHUNK)],
        out_shape=[jax.ShapeDtypeStruct((t, 3 * D), BF16), vec, vec,
                   jax.ShapeDtypeStruct((SG_GROUPS, CHUNK, CHUNK), F32), jax.ShapeDtypeStruct((CHUNK, LANES), F32)],
        semantics=("arbitrary",), rider=rider)


SWA_COLS = 2 * D + 2 * KVW
KV_BLOCK = 2 * KVW


def _lane_roll(x, shift):
    return pltpu.roll(x, shift, 1)


def _rot_half(x):
    w = x.shape[1]
    lane = lax.broadcasted_iota(jnp.int32, x.shape, 1)
    return jnp.where(lane % HEAD < HEAD // 2, -_lane_roll(x, w - HEAD // 2), _lane_roll(x, HEAD // 2))


@jax.custom_vjp
def _rope(x, cos, sin):
    return x * cos + _rot_half(x) * sin


def _rope_fwd(x, cos, sin):
    return _rope(x, cos, sin), (cos, sin)


def _rope_bwd(res, ct):
    cos, sin = res
    return ct * cos - _rot_half(ct) * sin, jnp.zeros_like(cos), jnp.zeros_like(sin)


_rope.defvjp(_rope_fwd, _rope_bwd)


@jax.custom_vjp
def _swap_halves(x):
    return _lane_roll(x, HEAD)


_swap_halves.defvjp(lambda x: (_lane_roll(x, HEAD), None), lambda _, ct: (_lane_roll(ct, HEAD),))


def _f_swa(pq, pkv, cos, sin, cosp, sinp, sink_row, valid):
    reps = D // LANES
    q = _rope(pq[:, :D], jnp.tile(cos, (1, reps)), jnp.tile(sin, (1, reps))) * (HEAD ** -0.5)
    k = _rope(pq[:, D:D + KVW], jnp.tile(cos, (1, KVW // LANES)), jnp.tile(sin, (1, KVW // LANES)))
    kp = _rope(pkv[:, :KVW], jnp.tile(cosp, (1, KVW // LANES)), jnp.tile(sinp, (1, KVW // LANES)))
    v, vp, z = pq[:, D + KVW:D + 2 * KVW], pkv[:, KVW:], pq[:, D + 2 * KVW:]
    kcat = jnp.concatenate([kp, k], axis=0)
    vcat = jnp.concatenate([vp, v], axis=0)
    lane = lax.broadcasted_iota(jnp.int32, (2 * CHUNK, LANES), 1)
    lo = lane < HEAD
    hlane = lax.broadcasted_iota(jnp.int32, (1, LANES), 1)

    def halves(cat, g):
        blk = cat[:, (g // 2) * LANES:(g // 2 + 1) * LANES]
        other = _swap_halves(blk)
        if g % 2 == 0:
            return jnp.where(lo, blk, 0.0), jnp.where(lo, 0.0, other)
        return jnp.where(lo, other, 0.0), jnp.where(lo, 0.0, blk)

    pairs = N_HEADS // KV_HEADS // 2
    valid_g = jnp.tile(valid, (pairs, 1))

    def probs(s, heads):
        sink = jnp.concatenate(
            [jnp.broadcast_to(jnp.sum(jnp.where(hlane == h, sink_row, 0.0), axis=1, keepdims=True), (CHUNK, 1))
             for h in heads], axis=0)
        s = jnp.where(valid_g, s, NEG)
        m = lax.stop_gradient(jnp.maximum(jnp.max(s, axis=1, keepdims=True), sink))
        e = jnp.exp(s - m)
        return e / (jnp.sum(e, axis=1, keepdims=True) + jnp.exp(sink - m))

    outs = []
    for g in range(KV_HEADS):
        k_lo, k_hi = halves(kcat, g)
        v_lo, v_hi = halves(vcat, g)
        tiles = range(g * pairs, (g + 1) * pairs)
        qg = jnp.concatenate([q[:, j * LANES:(j + 1) * LANES] for j in tiles], axis=0)
        p_a = probs(_bdot_nt(qg, k_lo), [2 * j for j in tiles])
        p_b = probs(_bdot_nt(qg, k_hi), [2 * j + 1 for j in tiles])
        og = _bdot_nn(p_a, v_lo) + _bdot_nn(p_b, v_hi)
        outs += [og[n * CHUNK:(n + 1) * CHUNK] for n in range(pairs)]
    return jnp.concatenate(outs, axis=1) * jax.nn.silu(z)


def _swa_valid(block):
    qi = lax.broadcasted_iota(jnp.int32, (CHUNK, 2 * CHUNK), 0)
    kj = lax.broadcasted_iota(jnp.int32, (CHUNK, 2 * CHUNK), 1)
    rel = qi + CHUNK - kj
    return (rel >= 0) & (rel < CHUNK) & ((kj >= CHUNK) | (block > 0))


def _swa_specs(blk):
    prev = lambda i: jnp.maximum(blk(i) - 1, 0)
    kv_col = D // KV_BLOCK
    return [pl.BlockSpec((CHUNK, SWA_COLS), lambda i: (blk(i), 0)),
            pl.BlockSpec((CHUNK, KV_BLOCK), lambda i: (prev(i), kv_col)),
            pl.BlockSpec((CHUNK, LANES), lambda i: (blk(i), 0)), pl.BlockSpec((CHUNK, LANES), lambda i: (blk(i), 0)),
            pl.BlockSpec((CHUNK, LANES), lambda i: (prev(i), 0)), pl.BlockSpec((CHUNK, LANES), lambda i: (prev(i), 0)),
            _vec_spec(LANES)]


def _swa_fwd(p, cos, sin, sink_row, name, rider=None):
    t = p.shape[0]

    def body(pq_ref, pkv_ref, c_ref, s_ref, cp_ref, sp_ref, sk_ref, o_ref):
        valid = _swa_valid(pl.program_id(0))
        o_ref[...] = _f_swa(pq_ref[...], pkv_ref[...], c_ref[...], s_ref[...], cp_ref[...], sp_ref[...],
                            sk_ref[...], valid).astype(BF16)

    return _compute_call(
        body, (p, p, cos, sin, cos, sin, sink_row), name=name, grid=(t // CHUNK,), in_specs=_swa_specs(lambda i: i),
        out_specs=_row_spec(D, CHUNK), out_shape=jax.ShapeDtypeStruct((t, D), BF16), semantics=("parallel",), rider=rider)


def _swa_bwd(p, cos, sin, sink_row, dout, name, rider=None):
    t = p.shape[0]
    nb = t // CHUNK
    blk = lambda i: nb - 1 - i

    def body(pq_ref, pkv_ref, c_ref, s_ref, cp_ref, sp_ref, sk_ref, do_ref, dp_ref, dsk_ref, pend_ref):
        i = pl.program_id(0)
        valid = _swa_valid(blk(i))
        f = functools.partial(_f_swa, valid=valid)
        _, vjp = jax.vjp(f, pq_ref[...], pkv_ref[...], c_ref[...], s_ref[...], cp_ref[...], sp_ref[...], sk_ref[...])
        dpq, dpkv, _, _, _, _, dsk = vjp(do_ref[...])

        @pl.when(i == 0)
        def _():
            pend_ref[...] = jnp.zeros_like(pend_ref)

        dp_ref[...] = jnp.concatenate(
            [dpq[:, :D], dpq[:, D:D + KV_BLOCK] + pend_ref[...], dpq[:, D + KV_BLOCK:]], axis=1).astype(BF16)
        pend_ref[...] = dpkv
        _accum(dsk_ref, dsk, i == 0)

    return _compute_call(
        body, (p, p, cos, sin, cos, sin, sink_row, dout), name=name, grid=(nb,),
        in_specs=_swa_specs(blk) + [pl.BlockSpec((CHUNK, D), lambda i: (blk(i), 0))],
        out_specs=[pl.BlockSpec((CHUNK, SWA_COLS), lambda i: (blk(i), 0)), _vec_spec(LANES)],
        out_shape=[jax.ShapeDtypeStruct((t, SWA_COLS), BF16), jax.ShapeDtypeStruct((1, LANES), F32)],
        scratch_shapes=[pltpu.VMEM((CHUNK, KV_BLOCK), F32)], semantics=("arbitrary",), rider=rider)


RW_MAIN = 4 * D
RW_LO = 2 * LORA_PAD
VM = LANES // N_HEADS
VD = HEAD // VM
S_ROWS = VD * HEAD
TB_RW = 128
TB_K = 32
TB_SCAN = 16


def _dim_major(a):
    return a.reshape(a.shape[:-1] + (N_HEADS, HEAD)).swapaxes(-1, -2).reshape(a.shape)


def _head_major(a):
    return a.reshape(a.shape[:-1] + (HEAD, N_HEADS)).swapaxes(-1, -2).reshape(a.shape)


def _param_compact(w):
    return _dim_major(w).reshape(VD, LANES)


def _param_compact_inv(pc):
    return _head_major(pc.reshape(-1))


def _f_rwkv_lora(xs_lo, w0, a0, wl, al):
    decay = jnp.exp(-DECAY_SCALE * jax.nn.sigmoid(w0 + _bdot_nn(jnp.tanh(xs_lo[:, :LORA_PAD]), wl)))
    a = jax.nn.sigmoid(a0 + _bdot_nn(xs_lo[:, LORA_PAD:], al))
    return decay, a


def _prev_rows_spec(width, tb):
    return pl.BlockSpec((8, width), lambda i: (jnp.maximum(i * (tb // 8) - 1, 0), 0))


def _token_shift_lerp(p, prev8, mu, first):
    rows = lax.broadcasted_iota(jnp.int32, p.shape, 0)
    prev = jnp.where(first, 0.0, prev8[7:8, :])
    shifted = jnp.where(rows == 0, prev, pltpu.roll(p, 1, 0))
    return p + (shifted - p) * mu


def _store_compact(ref, val):
    for j in range(VD):
        ref[:, j, :] = val[:, j * LANES:(j + 1) * LANES]


def _load_flat(ref, rows=slice(None)):
    if len(ref.shape) == 2:
        return ref[rows, :]
    return jnp.concatenate([ref[rows, j, :] for j in range(VD)], axis=1)


def _flat_spec(a, tb):
    return _row_spec(a.shape[1], tb) if a.ndim == 2 else _k_spec(VD, tb)


def _rwkv_pre_fwd(p_main, p_lo, mu_main, mu_lo, w0, a0, wl, al, name):
    t = p_main.shape[0]
    tb = TB_RW

    def body(pm_ref, pmp_ref, pl_ref, plp_ref, mm_ref, ml_ref, w0_ref, a0_ref, wl_ref, al_ref,
             r_ref, k_ref, v_ref, dec_ref, a_ref, z_ref, xl_ref):
        first = pl.program_id(0) == 0
        xs = _token_shift_lerp(pm_ref[...], pmp_ref[...], mm_ref[...], first)
        for n, ref in enumerate((r_ref, k_ref, v_ref)):
            _store_compact(ref, xs[:, n * D:(n + 1) * D])
        z_ref[...] = xs[:, 3 * D:]
        xs_lo = _token_shift_lerp(pl_ref[...], plp_ref[...], ml_ref[...], first)
        xl_ref[...] = xs_lo
        decay, a = _f_rwkv_lora(xs_lo, w0_ref[...], a0_ref[...], wl_ref[...], al_ref[...])
        _store_compact(dec_ref, decay)
        _store_compact(a_ref, a)

    cl = jax.ShapeDtypeStruct((t, VD, LANES), F32)
    return pl.pallas_call(
        body, name=name, grid=(t // tb,),
        in_specs=[_row_spec(RW_MAIN, tb), _prev_rows_spec(RW_MAIN, tb), _row_spec(RW_LO, tb), _prev_rows_spec(RW_LO, tb),
                  _vec_spec(RW_MAIN), _vec_spec(RW_LO), _vec_spec(D), _vec_spec(D),
                  _vec_spec(D, LORA_PAD), _vec_spec(D, LORA_PAD)],
        out_specs=[_k_spec(VD, tb)] * 5 + [_row_spec(D, tb), _row_spec(RW_LO, tb)],
        out_shape=[cl] * 5 + [jax.ShapeDtypeStruct((t, D), F32), jax.ShapeDtypeStruct((t, RW_LO), F32)],
        compiler_params=_cparams("parallel"),
    )(p_main, p_main, p_lo, p_lo, mu_main, mu_lo, w0, a0, wl, al)


def _rwkv_lora_bwd(xs_lo, w0, a0, wl, al, ddecay, da, name):
    t = xs_lo.shape[0]
    tb = TB_NORM

    def body(x_ref, w0_ref, a0_ref, wl_ref, al_ref, dd_ref, da_ref, dx_ref, dw0_ref, da0_ref, dwl_ref, dal_ref):
        _, vjp = jax.vjp(_f_rwkv_lora, x_ref[...], w0_ref[...], a0_ref[...], wl_ref[...], al_ref[...])
        dx, dw0, da0, dwl, dal = vjp((_load_flat(dd_ref), _load_flat(da_ref)))
        dx_ref[...] = dx
        first = pl.program_id(0) == 0
        _accum(dw0_ref, dw0, first)
        _accum(da0_ref, da0, first)
        _accum(dwl_ref, dwl, first)
        _accum(dal_ref, dal, first)

    vec = jax.ShapeDtypeStruct((1, D), F32)
    lor = jax.ShapeDtypeStruct((LORA_PAD, D), F32)
    return pl.pallas_call(
        body, name=name, grid=(t // tb,),
        in_specs=[_row_spec(RW_LO), _vec_spec(D), _vec_spec(D), _vec_spec(D, LORA_PAD), _vec_spec(D, LORA_PAD),
                  _k_spec(VD, tb), _k_spec(VD, tb)],
        out_specs=[_row_spec(RW_LO), _vec_spec(D), _vec_spec(D), _vec_spec(D, LORA_PAD), _vec_spec(D, LORA_PAD)],
        out_shape=[jax.ShapeDtypeStruct((t, RW_LO), F32), vec, vec, lor, lor], compiler_params=_cparams("arbitrary"),
    )(xs_lo, w0, a0, wl, al, ddecay, da)


def _lerp_bwd(p, dxs_groups, mu, name):
    t, width = p.shape
    tb = TB_RW
    nb = t // tb
    parts = [a for group in dxs_groups for a in group]

    def body(p_ref, pp_ref, mu_ref, *refs):
        d_refs, (dp_ref, dmu_ref) = refs[:2 * len(parts)], refs[2 * len(parts):]
        i = pl.program_id(0)

        def columns(k):
            pick = (lambda r: _load_flat(r, slice(0, 1))) if k else _load_flat
            vals, at = [], 0
            for group in dxs_groups:
                vals.append(functools.reduce(jnp.add, [pick(d_refs[2 * (at + n) + k]) for n in range(len(group))]))
                at += len(group)
            return jnp.concatenate(vals, axis=1)

        pv, dv, mu_v = p_ref[...], columns(0), mu_ref[...]
        rows = lax.broadcasted_iota(jnp.int32, pv.shape, 0)
        prev = jnp.where(i == 0, 0.0, pp_ref[7:8, :])
        shifted = jnp.where(rows == 0, prev, pltpu.roll(pv, 1, 0))
        nxt = jnp.where(i == nb - 1, 0.0, columns(1))
        d_next = jnp.where(rows == tb - 1, nxt, pltpu.roll(dv, tb - 1, 0))
        dp_ref[...] = (dv * (1.0 - mu_v) + d_next * mu_v).astype(BF16)
        _accum(dmu_ref, jnp.sum(dv * (shifted - pv), axis=0, keepdims=True), i == 0)

    d_specs = []
    for a in parts:
        after = lambda i, nd=a.ndim: (jnp.minimum((i + 1) * (tb // 8), t // 8 - 1),) + (0,) * (nd - 1)
        d_specs += [_flat_spec(a, tb), pl.BlockSpec((8,) + a.shape[1:], after)]
    return pl.pallas_call(
        body, name=name, grid=(nb,),
        in_specs=[_row_spec(width, tb), _prev_rows_spec(width, tb), _vec_spec(width)] + d_specs,
        out_specs=[_row_spec(width, tb), _vec_spec(width)],
        out_shape=[jax.ShapeDtypeStruct((t, width), BF16), jax.ShapeDtypeStruct((1, width), F32)],
        compiler_params=_cparams("arbitrary"),
    )(p, p, mu, *[a for a in parts for _ in range(2)])


def _lane_group_sum2d(x):
    x = x + pltpu.roll(x, N_HEADS, 1)
    return x + pltpu.roll(x, 2 * N_HEADS, 1)


@jax.custom_vjp
def _lane_group_sum(x):
    return _lane_group_sum2d(x.reshape(-1, LANES)).reshape(x.shape)


_lane_group_sum.defvjp(lambda x: (_lane_group_sum(x), None), lambda _, ct: (_lane_group_sum(ct),))


def _head_sum(x):
    return _lane_group_sum(jnp.sum(x, axis=1, keepdims=True))


def _f_kprep(k, a, r, kkp, kap, rkp):
    kk = k * kkp
    kk = kk / jnp.maximum(jnp.sqrt(_head_sum(kk * kk)), 1e-12)
    k2 = k * (1.0 + (a - 1.0) * kap)
    return kk, k2, kk * a, _head_sum(r * k2 * rkp)


def _k_spec(rows=HEAD, tb=TB_K):
    return pl.BlockSpec((tb, rows, LANES), lambda i: (i, 0, 0))


def _kparam_spec(rows=HEAD):
    return pl.BlockSpec((rows, LANES), lambda i: (0, 0))


def _lane_group(shape):
    return lax.broadcasted_iota(jnp.int32, shape, len(shape) - 1) // N_HEADS


def _store_k_layout(ref, xc):
    x2 = xc.reshape(-1, LANES)
    group = _lane_group(x2.shape)
    shifted = [x2] + [pltpu.roll(x2, N_HEADS * k, 1) for k in range(1, VM)]
    for q in range(VM):
        out = shifted[0]
        for k in range(1, VM):
            out = jnp.where(group == (q + k) % VM, shifted[k], out)
        ref[:, pl.ds(q, VD, stride=VM), :] = out.reshape(xc.shape)


def _load_compact(ref):
    shape = (ref.shape[0], VD, LANES)
    rows = [ref[:, pl.ds(q, VD, stride=VM), :].reshape(-1, LANES) for q in range(VM)]
    group = _lane_group(rows[0].shape)
    acc = None
    for k in range(VM):
        t = rows[-k % VM]
        for g in range(1, VM):
            t = jnp.where(group == g, rows[(g - k) % VM], t)
        if k:
            t = pltpu.roll(t, LANES - N_HEADS * k, 1)
        acc = t if acc is None else acc + t
    return acc.reshape(shape)


def _rwkv_kprep_fwd(k, a, r, w, kkp, kap, rkp, name):
    t = k.shape[0]

    def body(k_ref, a_ref, r_ref, w_ref, kkp_ref, kap_ref, rkp_ref, kk_ref, k2_ref, b_ref, r4_ref, w4_ref, rk_ref):
        rv = r_ref[...]
        kk, k2, b, rk_ref[...] = _f_kprep(k_ref[...], a_ref[...], rv, kkp_ref[...], kap_ref[...], rkp_ref[...])
        for ref, val in ((kk_ref, kk), (k2_ref, k2), (b_ref, b), (r4_ref, rv), (w4_ref, w_ref[...])):
            _store_k_layout(ref, val)

    big = jax.ShapeDtypeStruct((t, HEAD, LANES), F32)
    return pl.pallas_call(
        body, name=name, grid=(t // TB_K,),
        in_specs=[_k_spec(VD)] * 4 + [_kparam_spec(VD)] * 3, out_specs=[_k_spec()] * 5 + [_k_spec(1)],
        out_shape=[big] * 5 + [jax.ShapeDtypeStruct((t, 1, LANES), F32)], compiler_params=_cparams("parallel"),
    )(k, a, r, w, kkp, kap, rkp)


def _rwkv_kprep_bwd(k, a, r, kkp, kap, rkp, dkk, dk2, db, drk, dr_scan, dw_scan, name):
    t = k.shape[0]

    def body(k_ref, a_ref, r_ref, kkp_ref, kap_ref, rkp_ref, dkk_ref, dk2_ref, db_ref, drk_ref, drs_ref, dws_ref,
             dk_ref, da_ref, dr_ref, dw_ref, dkkp_ref, dkap_ref, drkp_ref):
        _, vjp = jax.vjp(_f_kprep, k_ref[...], a_ref[...], r_ref[...], kkp_ref[...], kap_ref[...], rkp_ref[...])
        dk, da, dr, dkkp, dkap, drkp = vjp((_load_compact(dkk_ref), _load_compact(dk2_ref), _load_compact(db_ref),
                                            drk_ref[...]))
        dk_ref[...] = dk
        da_ref[...] = da
        dr_ref[...] = dr + _load_compact(drs_ref)
        dw_ref[...] = _load_compact(dws_ref)
        first = pl.program_id(0) == 0
        _accum(dkkp_ref, dkkp, first)
        _accum(dkap_ref, dkap, first)
        _accum(drkp_ref, drkp, first)

    cl = jax.ShapeDtypeStruct((t, VD, LANES), F32)
    par = jax.ShapeDtypeStruct((VD, LANES), F32)
    return pl.pallas_call(
        body, name=name, grid=(t // TB_K,),
        in_specs=[_k_spec(VD)] * 3 + [_kparam_spec(VD)] * 3 + [_k_spec()] * 3 + [_k_spec(1), _k_spec(), _k_spec()],
        out_specs=[_k_spec(VD)] * 4 + [_kparam_spec(VD)] * 3,
        out_shape=[cl] * 4 + [par] * 3, compiler_params=_cparams("arbitrary"),
    )(k, a, r, kkp, kap, rkp, dkk, dk2, db, drk, dr_scan, dw_scan)


def _f_post(y, v, rk, g, b):
    mean = _lane_group_sum(jnp.sum(y, axis=1, keepdims=True)) * (1.0 / HEAD)
    yc = y - mean
    var = _lane_group_sum(jnp.sum(yc * yc, axis=1, keepdims=True)) * (1.0 / HEAD)
    return yc * lax.rsqrt(var + GN_EPS) * g + b + rk * v


def _rwkv_post_fwd(y, v, rk, g, b, name):
    t = y.shape[0]

    def body(y_ref, v_ref, rk_ref, g_ref, b_ref, o_ref):
        o_ref[...] = _f_post(y_ref[...], v_ref[...], rk_ref[...], g_ref[...], b_ref[...])

    return pl.pallas_call(
        body, name=name, grid=(t // TB_K,),
        in_specs=[_k_spec(VD), _k_spec(VD), _k_spec(1), _kparam_spec(VD), _kparam_spec(VD)], out_specs=_k_spec(VD),
        out_shape=jax.ShapeDtypeStruct((t, VD, LANES), F32), compiler_params=_cparams("parallel"),
    )(y, v, rk, g, b)


def _rwkv_post_bwd(y, v, rk, g, b, do, name):
    t = y.shape[0]

    def body(y_ref, v_ref, rk_ref, g_ref, b_ref, do_ref, dy_ref, dv_ref, drk_ref, dg_ref, db_ref):
        _, vjp = jax.vjp(_f_post, y_ref[...], v_ref[...], rk_ref[...], g_ref[...], b_ref[...])
        dy, dv, drk, dg, db = vjp(do_ref[...])
        dy_ref[...] = dy
        dv_ref[...] = dv
        drk_ref[...] = drk
        first = pl.program_id(0) == 0
        _accum(dg_ref, dg, first)
        _accum(db_ref, db, first)

    vl = jax.ShapeDtypeStruct((t, VD, LANES), F32)
    par = jax.ShapeDtypeStruct((VD, LANES), F32)
    return pl.pallas_call(
        body, name=name, grid=(t // TB_K,),
        in_specs=[_k_spec(VD), _k_spec(VD), _k_spec(1), _kparam_spec(VD), _kparam_spec(VD), _k_spec(VD)],
        out_specs=[_k_spec(VD), _k_spec(VD), _k_spec(1), _kparam_spec(VD), _kparam_spec(VD)],
        out_shape=[vl, vl, jax.ShapeDtypeStruct((t, 1, LANES), F32), par, par], compiler_params=_cparams("arbitrary"),
    )(y, v, rk, g, b, do)


def _f_gate(o, z):
    return o * jax.nn.silu(z)


def _rwkv_gate_fwd(o, z, name):
    t = z.shape[0]

    def body(o_ref, z_ref, u_ref):
        u_ref[...] = _f_gate(_load_flat(o_ref), z_ref[...]).astype(BF16)

    return pl.pallas_call(
        body, name=name, grid=(t // TB_NORM,), in_specs=[_k_spec(VD, TB_NORM), _row_spec(D)], out_specs=_row_spec(D),
        out_shape=jax.ShapeDtypeStruct((t, D), BF16), compiler_params=_cparams("parallel"),
    )(o, z)


def _rwkv_gate_bwd(o, z, du, name):
    t = z.shape[0]

    def body(o_ref, z_ref, du_ref, do_ref, dz_ref):
        _, vjp = jax.vjp(_f_gate, _load_flat(o_ref), z_ref[...])
        do, dz_ref[...] = vjp(du_ref[...])
        _store_compact(do_ref, do)

    return pl.pallas_call(
        body, name=name, grid=(t // TB_NORM,), in_specs=[_k_spec(VD, TB_NORM), _row_spec(D), _row_spec(D)],
        out_specs=[_k_spec(VD, TB_NORM), _row_spec(D)],
        out_shape=[jax.ShapeDtypeStruct((t, VD, LANES), F32), jax.ShapeDtypeStruct((t, D), F32)],
        compiler_params=_cparams("parallel"),
    )(o, z, du)


def _colsum(x):
    return jnp.sum(x, axis=0, keepdims=True)


def _rwkv_scan_fwd(r4, w4, k24, kk4, b4, v, name, rider=None):
    t = r4.shape[0]
    tb = TB_SCAN

    def body(r_ref, w_ref, k2_ref, kk_ref, b_ref, v_ref, y_ref, sall_ref, sa_ref, s_scr):
        @pl.when(pl.program_id(0) == 0)
        def _():
            s_scr[...] = jnp.zeros_like(s_scr)

        sall_ref[0] = s_scr[...]

        def step(tt, dst):
            kk = kk_ref[tt]
            sas = []
            for vd in range(VD):
                sa = _colsum(sall_ref[tt, pl.ds(vd * HEAD, HEAD), :] * kk)
                sa_ref[tt, pl.ds(vd, 1), :] = sa
                sas.append(sa)
            w, b, k2, r = w_ref[tt], b_ref[tt], k2_ref[tt], r_ref[tt]
            for vd in range(VD):
                rows = pl.ds(vd * HEAD, HEAD)
                s = sall_ref[tt, rows, :] * w - sas[vd] * b + v_ref[tt, pl.ds(vd, 1), :] * k2
                dst[rows, :] = s
                y_ref[tt, pl.ds(vd, 1), :] = _colsum(s * r)

        def loop_step(tt, carry):
            step(tt, sall_ref.at[tt + 1])
            return carry

        lax.fori_loop(0, tb - 1, loop_step, 0)
        step(tb - 1, s_scr)

    vl = jax.ShapeDtypeStruct((t, VD, LANES), F32)
    return _compute_call(
        body, (r4, w4, k24, kk4, b4, v), name=name, grid=(t // tb,),
        in_specs=[_k_spec(HEAD, tb)] * 5 + [_k_spec(VD, tb)],
        out_specs=[_k_spec(VD, tb), _k_spec(S_ROWS, tb), _k_spec(VD, tb)],
        out_shape=[vl, jax.ShapeDtypeStruct((t, S_ROWS, LANES), F32), vl],
        scratch_shapes=[pltpu.VMEM((S_ROWS, LANES), F32)], semantics=("arbitrary",), rider=rider)


def _rwkv_scan_bwd(dy, s_all, sa_all, r4, w4, k24, kk4, b4, v, name, rider=None):
    t = r4.shape[0]
    tb = TB_SCAN
    nb = t // tb
    blk = lambda i: nb - 1 - i

    def body(dy_ref, sall_ref, sa_ref, r_ref, w_ref, k2_ref, kk_ref, b_ref, v_ref,
             dr_ref, dw_ref, dk2_ref, dkk_ref, db_ref, dv_ref, ds_scr):
        @pl.when(pl.program_id(0) == 0)
        def _():
            ds_scr[...] = jnp.zeros_like(ds_scr)

        def step(j, carry):
            tt = tb - 1 - j
            vrow = lambda ref, vd: ref[tt, pl.ds(vd, 1), :]
            srows = lambda vd: pl.ds(vd * HEAD, HEAD)
            r, k2, b = r_ref[tt], k2_ref[tt], b_ref[tt]
            dsas = []
            for vd in range(VD):
                ds = ds_scr[srows(vd), :] + vrow(dy_ref, vd) * r
                ds_scr[srows(vd), :] = ds
                dv_ref[tt, pl.ds(vd, 1), :] = _colsum(ds * k2)
                dsas.append(-_colsum(ds * b))
            zero = jnp.zeros((HEAD, LANES), F32)
            dk2, q, sady, vdy = zero, zero, 0.0, 0.0
            for vd in range(VD):
                dyv = vrow(dy_ref, vd)
                dk2 = dk2 + ds_scr[srows(vd), :] * vrow(v_ref, vd)
                q = q + sall_ref[tt, srows(vd), :] * dyv
                sady = sady + vrow(sa_ref, vd) * dyv
                vdy = vdy + vrow(v_ref, vd) * dyv
            dk2_ref[tt] = dk2
            dr_ref[tt] = w_ref[tt] * q - b_ref[tt] * sady + k2_ref[tt] * vdy
            dw, dkk = zero, zero
            for vd in range(VD):
                sp = sall_ref[tt, srows(vd), :]
                dw = dw + ds_scr[srows(vd), :] * sp
                dkk = dkk + sp * dsas[vd]
            dw_ref[tt] = dw
            dkk_ref[tt] = dkk
            w, kk = w_ref[tt], kk_ref[tt]
            db = zero
            for vd in range(VD):
                ds = ds_scr[srows(vd), :]
                db = db - ds * vrow(sa_ref, vd)
                ds_scr[srows(vd), :] = ds * w + dsas[vd] * kk
            db_ref[tt] = db
            return carry

        lax.fori_loop(0, tb, step, 0)

    rk = lambda rows: pl.BlockSpec((tb, rows, LANES), lambda i: (blk(i), 0, 0))
    big = jax.ShapeDtypeStruct((t, HEAD, LANES), F32)
    return _compute_call(
        body, (dy, s_all, sa_all, r4, w4, k24, kk4, b4, v), name=name, grid=(nb,),
        in_specs=[rk(VD), rk(S_ROWS), rk(VD)] + [rk(HEAD)] * 5 + [rk(VD)],
        out_specs=[rk(HEAD)] * 5 + [rk(VD)],
        out_shape=[big] * 5 + [jax.ShapeDtypeStruct((t, VD, LANES), F32)],
        scratch_shapes=[pltpu.VMEM((S_ROWS, LANES), F32)], semantics=("arbitrary",), rider=rider)


def _rwkv_mixer_fwd(p_main, p_lo, prm, tag, rider):
    r, k, v, w, a, z, xs_lo = _rwkv_pre_fwd(p_main, p_lo, prm["mu_main"], prm["mu_lo"], prm["w0"], prm["a0"],
                                            prm["wl"], prm["al"], tag + "_pre")
    kk4, k24, b4, r4, w4, rk = _rwkv_kprep_fwd(k, a, r, w, prm["kkp"], prm["kap"], prm["rkp"], tag + "_kprep")
    (y, s_all, sa_all), ridden = _ridden(_rwkv_scan_fwd(r4, w4, k24, kk4, b4, v, tag + "_scan", rider), rider)
    o = _rwkv_post_fwd(y, v, rk, prm["gn_g"], prm["gn_b"], tag + "_post")
    u = _rwkv_gate_fwd(o, z, tag + "_gate")
    saved = dict(z=z, xs_lo=xs_lo, r=r, k=k, a=a, v=v, r4=r4, w4=w4, kk4=kk4, k24=k24, b4=b4, rk=rk,
                 y=y, s_all=s_all, sa_all=sa_all, o=o)
    return u, saved, ridden


def _rwkv_mixer_bwd(p_main, p_lo, prm, sv, du, tag, rider):
    do, dz = _rwkv_gate_bwd(sv["o"], sv["z"], du, tag + "_gate_b")
    dy, dv_post, drk, dgn_g, dgn_b = _rwkv_post_bwd(sv["y"], sv["v"], sv["rk"], prm["gn_g"], prm["gn_b"], do,
                                                    tag + "_post_b")
    (dr_s, dw_s, dk24, dkk4, db4, dv_scan), ridden = _ridden(_rwkv_scan_bwd(
        dy, sv["s_all"], sv["sa_all"], sv["r4"], sv["w4"], sv["k24"], sv["kk4"], sv["b4"], sv["v"], tag + "_scan_b", rider), rider)
    dk, da, dr, dw, dkkp, dkap, drkp = _rwkv_kprep_bwd(sv["k"], sv["a"], sv["r"], prm["kkp"], prm["kap"], prm["rkp"],
                                                       dkk4, dk24, db4, drk, dr_s, dw_s, tag + "_kprep_b")
    dxs_lo, dw0, da0, dwl, dal = _rwkv_lora_bwd(sv["xs_lo"], prm["w0"], prm["a0"], prm["wl"], prm["al"], dw, da,
                                                tag + "_lora_b")
    dxs_main = [[dr], [dk], [dv_post, dv_scan], [dz]]
    dp_main, dmu_main = _lerp_bwd(p_main, dxs_main, prm["mu_main"], tag + "_lerp_main_b")
    dp_lo, dmu_lo = _lerp_bwd(p_lo, [[dxs_lo]], prm["mu_lo"], tag + "_lerp_lo_b")
    grads = dict(mu_main=dmu_main, mu_lo=dmu_lo, w0=dw0, a0=da0, wl=dwl, al=dal, kkp=dkkp, kap=dkap, rkp=drkp,
                 gn_g=dgn_g, gn_b=dgn_b)
    return dp_main, dp_lo, grads, ridden


N_DEV = 8
N_CHIPS = 4
ANY = pl.BlockSpec(memory_space=pl.ANY)


def _place():
    return lax.axis_index("x"), lax.axis_index("y"), lax.axis_index("c")


def _remote(src, dst, send_sems, recv_sems, k, dev):
    return pltpu.make_async_remote_copy(src_ref=src, dst_ref=dst, send_sem=send_sems.at[k], recv_sem=recv_sems.at[k],
                                        device_id=dev, device_id_type=MESHT)


def _all_gather8(v, name):
    def body(buf_ref, out_ref, send_sems, recv_sems):
        del buf_ref
        x, y, c = _place()
        mine = out_ref.at[4 * x + 2 * y + c]
        peers = [(x ^ (k >> 2), y ^ ((k >> 1) & 1), c ^ (k & 1)) for k in range(1, N_DEV)]
        sends = [_remote(mine, mine, send_sems, recv_sems, k, peer) for k, peer in enumerate(peers)]
        for cp in sends:
            cp.start()
        for k, (px, py, pc) in enumerate(peers):
            _remote(mine, out_ref.at[4 * px + 2 * py + pc], send_sems, recv_sems, k, (x, y, c)).wait_recv()
        for cp in sends:
            cp.wait_send()

    return pl.pallas_call(
        body, name=name, in_specs=[ANY], out_specs=ANY, input_output_aliases={0: 0},
        out_shape=jax.ShapeDtypeStruct((N_DEV,) + v.shape, v.dtype),
        scratch_shapes=[pltpu.SemaphoreType.DMA((N_DEV - 1,)), pltpu.SemaphoreType.DMA((N_DEV - 1,))],
    )(jnp.broadcast_to(v[None], (N_DEV,) + v.shape))


def _other_chips(x, y):
    return [(1 - x, y), (x, 1 - y), (1 - x, 1 - y)]


GATHER_SEMS = 6


def _gather_buffer(v):
    return jnp.broadcast_to(v[None], (N_CHIPS,) + v.shape)


def _gather_start(bufs, send_sems, recv_sems):
    x, y, c = _place()
    for i, buf in enumerate(bufs):
        mine = buf.at[2 * x + y, c]
        for j, (cx, cy) in enumerate(_other_chips(x, y)):
            _remote(mine, mine, send_sems, recv_sems, GATHER_SEMS * i + j, (cx, cy, c)).start()


def _gather_finish(bufs, send_sems, recv_sems):
    x, y, c = _place()
    chips = _other_chips(x, y)
    passed = []
    for i, buf in enumerate(bufs):
        mine = buf.at[2 * x + y, c]
        for j, (cx, cy) in enumerate(chips):
            landed = buf.at[2 * cx + cy, c]
            _remote(mine, landed, send_sems, recv_sems, GATHER_SEMS * i + j, (x, y, c)).wait_recv()
            fwd = _remote(landed, landed, send_sems, recv_sems, GATHER_SEMS * i + 3 + j, (x, y, 1 - c))
            fwd.start()
            passed.append(fwd)
    for i, buf in enumerate(bufs):
        mine = buf.at[2 * x + y, c]
        for j, (cx, cy) in enumerate(chips):
            _remote(mine, buf.at[2 * cx + cy, 1 - c], send_sems, recv_sems, GATHER_SEMS * i + 3 + j, (x, y, c)).wait_recv()
            _remote(mine, mine, send_sems, recv_sems, GATHER_SEMS * i + j, (cx, cy, c)).wait_send()
    for fwd in passed:
        fwd.wait_send()


def _gather_rider(bufs):
    return _Rider(bufs, GATHER_SEMS * len(bufs), _gather_start, _gather_finish)


def _chip_gather(bufs, name):
    n = len(bufs)

    def body(*refs):
        out_refs, (send_sems, recv_sems) = refs[n:2 * n], refs[2 * n:]
        _gather_start(out_refs, send_sems, recv_sems)
        _gather_finish(out_refs, send_sems, recv_sems)

    return pl.pallas_call(
        body, name=name, in_specs=[ANY] * n, out_specs=[ANY] * n, input_output_aliases={i: i for i in range(n)},
        out_shape=[jax.ShapeDtypeStruct(b.shape, b.dtype) for b in bufs], scratch_shapes=_dma_sems(GATHER_SEMS * n),
    )(*bufs)


RS_W = 1024
RS_BLOCK_BYTES = 4 << 20


def _dma_sems(n):
    return [pltpu.SemaphoreType.DMA((n,)), pltpu.SemaphoreType.DMA((n,))]


def _pair_exchange_copies(refs, send_sems, recv_sems):
    n = len(refs) // 2
    x, y, c = _place()
    return [_remote(refs[i].at[s, 1 - c], refs[n + i].at[s], send_sems, recv_sems, N_CHIPS * i + s, (x, y, 1 - c))
            for i in range(n) for s in range(N_CHIPS)]


def _pair_exchange_start(refs, send_sems, recv_sems):
    for cp in _pair_exchange_copies(refs, send_sems, recv_sems):
        cp.start()


def _pair_exchange_finish(refs, send_sems, recv_sems):
    for cp in _pair_exchange_copies(refs, send_sems, recv_sems):
        cp.wait()


def _pair_exchange_rider(gs):
    landing = [lax.empty((N_CHIPS,) + g.shape[2:], g.dtype) for g in gs]
    return _Rider(list(gs) + landing, N_CHIPS * len(gs), _pair_exchange_start, _pair_exchange_finish)


def _rs_rows(rows, cols):
    cap = max(16, RS_BLOCK_BYTES // (N_CHIPS * 4 * cols))
    return rows if rows <= cap else max(d for d in range(16, cap + 1, 16) if rows % d == 0)


def _rs_pair_add(g, got, c_arr, name):
    _, _, rows, width = g.shape
    tr = _rs_rows(rows, width)

    def body(c_ref, g_ref, got_ref, p_ref):
        p_ref[...] = (g_ref[...] + got_ref[...]).astype(BF16)

    return pl.pallas_call(
        body, name=name,
        grid_spec=pltpu.PrefetchScalarGridSpec(
            num_scalar_prefetch=1, grid=(rows // tr,),
            in_specs=[pl.BlockSpec((N_CHIPS, None, tr, width), lambda i, c_ref: (0, c_ref[0], i, 0)),
                      pl.BlockSpec((N_CHIPS, tr, width), lambda i, c_ref: (0, i, 0))],
            out_specs=pl.BlockSpec((N_CHIPS, tr, width), lambda i, c_ref: (0, i, 0))),
        out_shape=jax.ShapeDtypeStruct((N_CHIPS, rows, width), BF16), compiler_params=_cparams("parallel"),
    )(c_arr, g, got)


def _chip_exchange_copies(refs, send_sems, recv_sems):
    n = len(refs) // 2
    x, y, c = _place()
    return [_remote(refs[i].at[2 * cx + cy], refs[n + i].at[j], send_sems, recv_sems, 3 * i + j, (cx, cy, c))
            for i in range(n) for j, (cx, cy) in enumerate(_other_chips(x, y))]


def _chip_exchange_start(refs, send_sems, recv_sems):
    for cp in _chip_exchange_copies(refs, send_sems, recv_sems):
        cp.start()


def _chip_exchange_finish(refs, send_sems, recv_sems):
    n = len(refs) // 2
    x, y, c = _place()
    for i in range(n):
        for j in range(3):
            _remote(refs[i].at[2 * x + y], refs[n + i].at[j], send_sems, recv_sems, 3 * i + j, (x, y, c)).wait_recv()
    for cp in _chip_exchange_copies(refs, send_sems, recv_sems):
        cp.wait_send()


def _chip_exchange_buffers(ps):
    return [lax.empty((3,) + p.shape[1:], p.dtype) for p in ps]


def _chip_exchange_rider(ps):
    return _Rider(list(ps) + _chip_exchange_buffers(ps), 3 * len(ps), _chip_exchange_start, _chip_exchange_finish)


def _rs_chip_exchange(ps, name):
    n = len(ps)

    def body(*refs):
        out_refs, (send_sems, recv_sems) = refs[2 * n:4 * n], refs[4 * n:]
        _chip_exchange_start(out_refs, send_sems, recv_sems)
        _chip_exchange_finish(out_refs, send_sems, recv_sems)

    arrays = list(ps) + _chip_exchange_buffers(ps)
    return pl.pallas_call(
        body, name=name, in_specs=[ANY] * (2 * n), out_specs=[ANY] * (2 * n),
        input_output_aliases={i: i for i in range(2 * n)},
        out_shape=[jax.ShapeDtypeStruct(a.shape, a.dtype) for a in arrays], scratch_shapes=_dma_sems(3 * n),
    )(*arrays)


def _rs_chip_add(p, q, idx, name):
    _, rows, width = q.shape
    tr = _rs_rows(rows, width)

    def body(idx_ref, p_ref, q_ref, r_ref):
        qv = q_ref[...].astype(F32)
        r_ref[...] = ((p_ref[...].astype(F32) + qv[0]) + qv[1]) + qv[2]

    return pl.pallas_call(
        body, name=name,
        grid_spec=pltpu.PrefetchScalarGridSpec(
            num_scalar_prefetch=1, grid=(rows // tr,),
            in_specs=[pl.BlockSpec((None, tr, width), lambda i, idx_ref: (idx_ref[0], i, 0)),
                      pl.BlockSpec((3, tr, width), lambda i, idx_ref: (0, i, 0))],
            out_specs=pl.BlockSpec((None, tr, width), lambda i, idx_ref: (idx_ref[1], i, 0))),
        out_shape=jax.ShapeDtypeStruct((2, rows, width), F32), compiler_params=_cparams("parallel"),
    )(idx, p, q)


def _rs_pair_share(rs, name):
    n = len(rs)

    def body(*refs):
        out_refs, (send_sems, recv_sems) = refs[n:2 * n], refs[2 * n:]
        x, y, c = _place()
        sends = [_remote(out_refs[i].at[c], out_refs[i].at[c], send_sems, recv_sems, i, (x, y, 1 - c)) for i in range(n)]
        for cp in sends:
            cp.start()
        for i in range(n):
            _remote(out_refs[i].at[c], out_refs[i].at[1 - c], send_sems, recv_sems, i, (x, y, c)).wait_recv()
        for cp in sends:
            cp.wait_send()

    return pl.pallas_call(
        body, name=name, in_specs=[ANY] * n, out_specs=[ANY] * n, input_output_aliases={i: i for i in range(n)},
        out_shape=[jax.ShapeDtypeStruct(r.shape, r.dtype) for r in rs], scratch_shapes=_dma_sems(n),
    )(*rs)


def _rs_pair_sums(gs, gots, core, tag):
    c_arr = core.astype(jnp.int32).reshape(1)
    return [_rs_pair_add(g, got, c_arr, f"{tag}_pair_add{i}") for i, (g, got) in enumerate(zip(gs, gots))]


def _rs_finish(ps, qs, chip, core, tag):
    idx = jnp.stack([chip, core]).astype(jnp.int32)
    rs = [_rs_chip_add(p, q, idx, f"{tag}_chip_add{i}") for i, (p, q) in enumerate(zip(ps, qs))]
    return _rs_pair_share(rs, tag + "_share")


def _sum_leading(a, name):
    n, rows, width = a.shape
    cap = max(8, RS_BLOCK_BYTES // (n * 4 * width))
    tr = rows if rows <= cap else max(d for d in range(8, cap + 1, 8) if rows % d == 0)

    def body(a_ref, o_ref):
        acc = a_ref[0]
        for d in range(1, n):
            acc = acc + a_ref[d]
        o_ref[...] = acc

    return pl.pallas_call(
        body, name=name, grid=(rows // tr,), in_specs=[pl.BlockSpec((n, tr, width), lambda i: (0, i, 0))],
        out_specs=pl.BlockSpec((tr, width), lambda i: (i, 0)), out_shape=jax.ShapeDtypeStruct((rows, width), F32),
        compiler_params=_cparams("parallel"),
    )(a)


def _pair_swap(v, name):
    def body(v_ref, got_ref, send_sems, recv_sems):
        x, y, c = _place()
        cp = _remote(v_ref, got_ref, send_sems, recv_sems, 0, (x, y, 1 - c))
        cp.start()
        cp.wait()

    return pl.pallas_call(body, name=name, in_specs=[ANY], out_specs=ANY, out_shape=jax.ShapeDtypeStruct(v.shape, v.dtype),
                          scratch_shapes=_dma_sems(1))(v)


def _all_reduce_replicated(v, name):
    rows, width = v.shape
    pair = _sum_leading(jnp.stack([v, _pair_swap(v, name + "_swap")]), name + "_pair_add")
    (gathered,) = _chip_gather([_gather_buffer(pair.reshape(2, rows // 2, width))], name + "_gather")
    return _sum_leading(gathered.reshape(N_CHIPS, rows, width), name + "_chip_add")


MOD_COLS = 3 * D // N_CHIPS
MOD_TK = 512


def _mod_partial(c_all, mod_w, name):
    nk = D // MOD_TK

    def body(c_ref, w_ref, o_ref):
        l = pl.program_id(1)
        part = _bdot_nn(jax.nn.silu(c_ref[...]), w_ref[0])
        _accum(o_ref.at[0], part, l == 0)

    return pl.pallas_call(
        body, name=name, grid=(DEPTH, nk),
        in_specs=[pl.BlockSpec((N_DEV, MOD_TK), lambda i, l: (0, l)), pl.BlockSpec((1, MOD_TK, MOD_COLS), lambda i, l: (i, l, 0))],
        out_specs=pl.BlockSpec((1, N_DEV, MOD_COLS), lambda i, l: (i, 0, 0)),
        out_shape=jax.ShapeDtypeStruct((DEPTH, N_DEV, MOD_COLS), F32), compiler_params=_cparams("parallel", "arbitrary"),
    )(c_all, mod_w)


def _mod_w_grad(c_all, dmod, name):
    def body(c_ref, d_ref, o_ref):
        o_ref[0] = _dg(jax.nn.silu(c_ref[...]).astype(BF16), d_ref[0].astype(BF16), _TN)

    return pl.pallas_call(
        body, name=name, grid=(DEPTH, D // MOD_TK),
        in_specs=[pl.BlockSpec((N_DEV, MOD_TK), lambda i, l: (0, l)), pl.BlockSpec((1, N_DEV, MOD_COLS), lambda i, l: (i, 0, 0))],
        out_specs=pl.BlockSpec((1, MOD_TK, MOD_COLS), lambda i, l: (i, l, 0)),
        out_shape=jax.ShapeDtypeStruct((DEPTH, D, MOD_COLS), F32), compiler_params=_cparams("parallel", "parallel"),
    )(c_all, dmod)


ADAM_BLOCK_BYTES = 1 << 20


def _adamw(w, g, m, v, name, rider=None):
    shape = w.shape
    cols = shape[-1]
    rows = w.size // cols
    w, g, m, v = (a.reshape(rows, cols) for a in (w, g, m, v))
    cap = max(8, ADAM_BLOCK_BYTES // (4 * cols))
    tr = rows if rows <= cap else max(d for d in range(8, cap + 1, 8) if rows % d == 0)
    c1 = 1.0 - ADAM_B1 ** ADAM_STEP
    c2 = 1.0 - ADAM_B2 ** ADAM_STEP

    def body(w_ref, g_ref, m_ref, v_ref, d_ref, nm_ref, nv_ref):
        gv = g_ref[...]
        mn = ADAM_B1 * m_ref[...] + (1.0 - ADAM_B1) * gv
        vn = ADAM_B2 * v_ref[...] + (1.0 - ADAM_B2) * (gv * gv)
        nm_ref[...] = mn
        nv_ref[...] = vn
        d_ref[...] = -ADAM_LR * ((mn / c1) / (jnp.sqrt(vn / c2) + ADAM_EPS) + ADAM_WD * w_ref[...])

    spec = pl.BlockSpec((tr, cols), lambda i: (i, 0))
    out = jax.ShapeDtypeStruct((rows, cols), F32)
    (d, nm, nv), ridden = _ridden(_compute_call(
        body, (w, g, m, v), name=name, grid=(rows // tr,), in_specs=[spec] * 4, out_specs=[spec] * 3, out_shape=[out] * 3,
        semantics=("parallel",), rider=rider), rider)
    res = (d.reshape(shape), nm.reshape(shape), nv.reshape(shape))
    return res if rider is None else (res, ridden)


W_NAMES = ("norm_g", "mod_w", "mod_b", "final_norm_g", "sg_w_in", "sg_w_out", "sg_ln_g", "sg_ln_b", "sg_w_spatial",
           "sg_b_spatial", "swa_w_in", "swa_w_out", "swa_sinks", "rwkv_w_in", "rwkv_w_out", "rwkv_mu", "rwkv_w0",
           "rwkv_w_lora", "rwkv_a0", "rwkv_a_lora", "rwkv_k_k", "rwkv_k_a", "rwkv_r_k", "rwkv_gn_g", "rwkv_gn_b")
SMALL = {"sg_ln_g": 1, "sg_ln_b": 1, "rwkv_mu": 1, "rwkv_w0": 1, "rwkv_w_lora": 2, "rwkv_a0": 1, "rwkv_a_lora": 2,
         "rwkv_k_k": 1, "rwkv_k_a": 1, "rwkv_gn_g": 1, "rwkv_gn_b": 1}
REPLICATED = ("norm_g", "final_norm_g", "sg_w_spatial", "sg_b_spatial", "swa_sinks", "rwkv_r_k")
KINDS = ("sg", "swa", "rwkv", "sg")


def _pad_to(flat, n):
    return jnp.pad(flat, (0, n - flat.shape[0]))


def _round_up(n, m):
    return -(-n // m) * m


def _join_shards(gathered, axis):
    return jnp.concatenate([gathered[s] for s in range(N_CHIPS)], axis=axis)


def _chip_blocks(full, axis):
    return jnp.stack(jnp.split(full, N_CHIPS, axis=axis)).reshape(N_CHIPS, -1)


def _weight_buffer(w):
    rows, cols = w.shape
    return _gather_buffer(w.astype(BF16).reshape(2, rows // 2, cols))


def _chip_shards(buf):
    return buf.reshape(N_CHIPS, -1, buf.shape[-1])


def _small_buffer(shards):
    flat = jnp.concatenate([shards[n].reshape(-1) for n in SMALL])
    rows = _round_up(flat.shape[0], 2 * 8 * LANES) // (2 * LANES)
    return _gather_buffer(_pad_to(flat, 2 * rows * LANES).reshape(2, rows, LANES))


def _unpack_small(buf, shards):
    got = buf.reshape(N_CHIPS, -1)
    out, off = {}, 0
    for n, axis in SMALL.items():
        size = shards[n].size
        out[n] = _join_shards(got[:, off:off + size].reshape((N_CHIPS,) + shards[n].shape), axis)
        off += size
    return out


def _lora_pad_rows(w):
    return jnp.pad(w, ((0, LORA_PAD - LORA), (0, 0)))


def _lo_cols(a):
    z = jnp.zeros(a.shape[:-1] + (LORA_PAD - LORA,), a.dtype)
    return jnp.concatenate([a[..., :LORA], z, a[..., LORA:], z], axis=-1)


def _lo_cols_inv(a):
    return jnp.concatenate([a[..., :LORA], a[..., LORA_PAD:LORA_PAD + LORA]], axis=-1)


def _rows_dim_major(w):
    return w.reshape(N_HEADS, HEAD, -1).swapaxes(0, 1).reshape(w.shape)


def _rows_head_major(w):
    return w.reshape(HEAD, N_HEADS, -1).swapaxes(0, 1).reshape(w.shape)


def kernel(x, c, positions, norm_g, mod_w, mod_b, final_norm_g, sg_w_in, sg_w_out, sg_ln_g, sg_ln_b, sg_w_spatial,
           sg_b_spatial, swa_w_in, swa_w_out, swa_sinks, rwkv_w_in, rwkv_w_out, rwkv_mu, rwkv_w0, rwkv_w_lora, rwkv_a0,
           rwkv_a_lora, rwkv_k_k, rwkv_k_a, rwkv_r_k, rwkv_gn_g, rwkv_gn_b, loss_target, m_norm_g, m_mod_w, m_mod_b,
           m_final_norm_g, m_sg_w_in, m_sg_w_out, m_sg_ln_g, m_sg_ln_b, m_sg_w_spatial, m_sg_b_spatial, m_swa_w_in,
           m_swa_w_out, m_swa_sinks, m_rwkv_w_in, m_rwkv_w_out, m_rwkv_mu, m_rwkv_w0, m_rwkv_w_lora, m_rwkv_a0,
           m_rwkv_a_lora, m_rwkv_k_k, m_rwkv_k_a, m_rwkv_r_k, m_rwkv_gn_g, m_rwkv_gn_b, v_norm_g, v_mod_w, v_mod_b,
           v_final_norm_g, v_sg_w_in, v_sg_w_out, v_sg_ln_g, v_sg_ln_b, v_sg_w_spatial, v_sg_b_spatial, v_swa_w_in,
           v_swa_w_out, v_swa_sinks, v_rwkv_w_in, v_rwkv_w_out, v_rwkv_mu, v_rwkv_w0, v_rwkv_w_lora, v_rwkv_a0,
           v_rwkv_a_lora, v_rwkv_k_k, v_rwkv_k_a, v_rwkv_r_k, v_rwkv_gn_g, v_rwkv_gn_b):
    given = dict(locals())
    w = {n: given[n] for n in W_NAMES}
    xi, yi, ci = _place()
    chip = 2 * xi + yi
    me = 4 * xi + 2 * yi + ci
    xs = [x[0]]

    c_all = _all_gather8(c, "gather_c")[:, 0, :]
    mod_part = _mod_partial(c_all, mod_w, "mod_fwd")
    mod_all = _all_gather8(mod_part, "gather_mod")[::2]
    mod_mine = lax.dynamic_index_in_dim(mod_all, me, axis=2, keepdims=False)
    mod = mod_mine.transpose(1, 0, 2).reshape(DEPTH, 3 * D) + mod_b
    shift, scale, gate = mod[:, :D], mod[:, D:2 * D], mod[:, 2 * D:]

    shards = {"sg_w_in0": sg_w_in[0], "sg_w_out0": sg_w_out[0], "swa_w_in": swa_w_in[0], "swa_w_out": swa_w_out[0],
              "rwkv_w_in": rwkv_w_in[0], "rwkv_w_out": rwkv_w_out[0], "sg_w_in1": sg_w_in[1], "sg_w_out1": sg_w_out[1]}
    bufs = {n: _weight_buffer(s) for n, s in shards.items()}
    fwd_riders = {(0, "in"): ["swa_w_in"], (0, "mix"): ["swa_w_out"], (1, "in"): ["rwkv_w_out"], (1, "mix"): ["rwkv_w_in"],
                  (2, "mix"): ["sg_w_in1", "sg_w_out1"]}
    bufs["sg_w_in0"], bufs["sg_w_out0"], small_buf = _chip_gather(
        [bufs["sg_w_in0"], bufs["sg_w_out0"], _small_buffer(w)], "gather_l0")
    full = _unpack_small(small_buf, w)

    def riding(i, where):
        names = fwd_riders.get((i, where))
        return names, (None if names is None else _gather_rider([bufs[n] for n in names]))

    def arrived(names, ridden):
        for n, b in zip(names or [], ridden):
            bufs[n] = b

    sg_in = lambda j: _chip_shards(bufs[f"sg_w_in{j}"])
    sg_out = lambda j: bufs[f"sg_w_out{j}"].reshape(D, D)
    mu = full["rwkv_mu"][0]
    rw_prm = dict(mu_main=_dim_major(mu[:RW_MAIN].reshape(4, D)).reshape(1, RW_MAIN), mu_lo=_lo_cols(mu[None, RW_MAIN:]),
                  w0=_dim_major(full["rwkv_w0"]), a0=_dim_major(full["rwkv_a0"]),
                  wl=_lora_pad_rows(_dim_major(full["rwkv_w_lora"][0])), al=_lora_pad_rows(_dim_major(full["rwkv_a_lora"][0])),
                  kkp=_param_compact(full["rwkv_k_k"][0]), kap=_param_compact(full["rwkv_k_a"][0]),
                  rkp=_param_compact(rwkv_r_k.reshape(-1)),
                  gn_g=_param_compact(full["rwkv_gn_g"][0]), gn_b=_param_compact(full["rwkv_gn_b"][0]))
    bs_t = [jnp.pad(sg_b_spatial[j].T, ((0, 0), (0, LANES - SG_GROUPS))) for j in range(2)]
    sink_row = jnp.pad(swa_sinks, ((0, 0), (0, LANES - N_HEADS)))
    inv_freq = ROPE_THETA ** (-jnp.arange(HEAD // 2, dtype=F32) / (HEAD // 2))
    ang = positions[0].astype(F32)[:, None] * inv_freq
    cos, sin = jnp.tile(jnp.cos(ang), (1, LANES * 2 // HEAD)), jnp.tile(jnp.sin(ang), (1, LANES * 2 // HEAD))

    def row(a, i):
        return a[i:i + 1]

    hs, ps, us, ys, rw_saved = [], [], [], [], None
    for i, kind in enumerate(KINDS):
        j = i // 3
        tag = f"l{i}_{kind}"
        h = _norm_mod_fwd(xs[i], row(norm_g, i), row(shift, i), row(scale, i), tag + "_norm")
        names_in, rider_in = riding(i, "in")
        names_mix, rider_mix = riding(i, "mix")
        if kind == "sg":
            p, ridden = _ridden(_matmul(h, sg_in(j), "nn", tag + "_in", blocked=True, rider=rider_in), rider_in)
            arrived(names_in, ridden)
            u, ridden = _ridden(_sg_fwd(p, row(full["sg_ln_g"], j), row(full["sg_ln_b"], j), sg_w_spatial[j], bs_t[j],
                                        tag + "_mix", rider_mix), rider_mix)
            w_out = sg_out(j)
        elif kind == "swa":
            swa_in, swa_out = _chip_shards(bufs["swa_w_in"]), bufs["swa_w_out"].reshape(D, D)
            p, ridden = _ridden(_matmul(h, swa_in, "nn", tag + "_in", blocked=True, rider=rider_in), rider_in)
            arrived(names_in, ridden)
            u, ridden = _ridden(_swa_fwd(p, cos, sin, sink_row, tag + "_mix", rider_mix), rider_mix)
            w_out = swa_out
        else:
            rw_in = _join_shards(_chip_shards(bufs["rwkv_w_in"]), axis=1)
            rw_main = _dim_major(rw_in[:, :RW_MAIN].reshape(D, 4, D)).reshape(D, RW_MAIN)
            rw_lo = _lo_cols(rw_in[:, RW_MAIN:])
            rw_out = _rows_dim_major(bufs["rwkv_w_out"].reshape(D, D))
            p = (_matmul(h, rw_main, "nn", tag + "_in"), _matmul(h, rw_lo, "nn", tag + "_in_lo"))
            u, rw_saved, ridden = _rwkv_mixer_fwd(p[0], p[1], rw_prm, tag, rider_mix)
            w_out = rw_out
        arrived(names_mix, ridden)
        y, x_next = _out_proj_resid(u, w_out, xs[i], row(gate, i), tag + "_out")
        xs.append(x_next)
        hs.append(h), ps.append(p), us.append(u), ys.append(y)

    loss_part, dx, d_final_g = _final_loss_grad(xs[DEPTH], final_norm_g[None], loss_target[0], "loss")
    loss = lax.psum(loss_part[0, 0], ("x", "y", "c"))

    gfull = {n: [None, None] for n in ("sg_ln_g", "sg_ln_b", "sg_w_spatial", "sg_b_spatial")}
    gbig = {}
    d_norm_g, d_mod = [None] * DEPTH, [None] * DEPTH
    rs_p, rs_q, riding_names = {}, {}, []

    for i in reversed(range(DEPTH)):
        kind, j = KINDS[i], i // 3
        tag = f"l{i}_{kind}_b"
        rider = _chip_exchange_rider([rs_p[n] for n in riding_names]) if riding_names else None
        dy, d_gate = _gate_bwd(dx, ys[i], row(gate, i), tag + "_gate")
        w_out = {"sg": sg_out(j), "swa": swa_out, "rwkv": rw_out}[kind]
        du = _matmul(dy, w_out, "nt", tag + "_du")
        dw_out = _matmul(us[i], dy, "tn", tag + "_dwout").reshape(N_CHIPS, D // N_CHIPS, D)
        if kind == "sg":
            (dp, dlg, dlb, dws, dbs), ridden = _ridden(
                _sg_bwd(ps[i], row(full["sg_ln_g"], j), row(full["sg_ln_b"], j), sg_w_spatial[j], bs_t[j], du,
                        tag + "_mix", rider), rider)
            gfull["sg_ln_g"][j], gfull["sg_ln_b"][j] = dlg[0], dlb[0]
            gfull["sg_w_spatial"][j], gfull["sg_b_spatial"][j] = dws, dbs[:, :SG_GROUPS].T
            gbig[f"sg_w_in{j}"] = _matmul(hs[i], dp, "tn", tag + "_dwin", blocked=True)
            gbig[f"sg_w_out{j}"] = dw_out
            dh, dh2 = _matmul(dp, sg_in(j), "nt", tag + "_dh", blocked=True), None
            mine = [f"sg_w_in{j}", f"sg_w_out{j}"]
        elif kind == "swa":
            (dp, dsk), ridden = _ridden(_swa_bwd(ps[i], cos, sin, sink_row, du, tag + "_mix", rider), rider)
            gfull["swa_sinks"] = dsk[:, :N_HEADS]
            gbig["swa_w_in"] = _matmul(hs[i], dp, "tn", tag + "_dwin", blocked=True)
            gbig["swa_w_out"] = dw_out
            dh, dh2 = _matmul(dp, swa_in, "nt", tag + "_dh", blocked=True), None
            mine = ["swa_w_in", "swa_w_out"]
        else:
            dpm, dpl, rg, ridden = _rwkv_mixer_bwd(ps[i][0], ps[i][1], rw_prm, rw_saved, du, tag, rider)
            mine = ["rwkv_w_in", "rwkv_w_out"]
            dw_main = _matmul(hs[i], dpm, "tn", tag + "_dwin")
            dw_lo = _matmul(hs[i], dpl, "tn", tag + "_dwin_lo")
            dw_main = _head_major(dw_main.reshape(D, 4, D)).reshape(D, RW_MAIN)
            dw_in = jnp.concatenate([dw_main, _lo_cols_inv(dw_lo)], axis=1)
            gbig["rwkv_w_in"] = dw_in.reshape(D, N_CHIPS, -1).transpose(1, 0, 2)
            gbig["rwkv_w_out"] = _rows_head_major(dw_out.reshape(D, D)).reshape(dw_out.shape)
            dmu_main = _head_major(rg["mu_main"].reshape(4, D)).reshape(1, RW_MAIN)
            gfull["rwkv_mu"] = jnp.concatenate([dmu_main, _lo_cols_inv(rg["mu_lo"])], axis=1)
            gfull["rwkv_w0"], gfull["rwkv_a0"] = _head_major(rg["w0"]), _head_major(rg["a0"])
            gfull["rwkv_w_lora"], gfull["rwkv_a_lora"] = _head_major(rg["wl"])[None, :LORA], _head_major(rg["al"])[None, :LORA]
            gfull["rwkv_k_k"], gfull["rwkv_k_a"] = _param_compact_inv(rg["kkp"])[None], _param_compact_inv(rg["kap"])[None]
            gfull["rwkv_r_k"] = _param_compact_inv(rg["rkp"]).reshape(1, N_HEADS, HEAD)
            gfull["rwkv_gn_g"], gfull["rwkv_gn_b"] = _param_compact_inv(rg["gn_g"])[None], _param_compact_inv(rg["gn_b"])[None]
            dh, dh2 = _matmul(dpm, rw_main, "nt", tag + "_dh"), _matmul(dpl, rw_lo, "nt", tag + "_dh_lo")
        rs_p.update(zip(riding_names, ridden[:len(riding_names)]))
        rs_q.update(zip(riding_names, ridden[len(riding_names):]))
        if i == 0:
            for n in ("sg_ln_g", "sg_ln_b"):
                gfull[n] = jnp.stack(gfull[n])
            small = jnp.concatenate([_chip_blocks(gfull[n], axis) for n, axis in SMALL.items()], axis=1)
            small_rows = _round_up(small.shape[1], 2 * 16 * LANES) // LANES
            small = jnp.pad(small, ((0, 0), (0, small_rows * LANES - small.shape[1])))
            gbig["small"] = small.reshape(N_CHIPS, small_rows, LANES)
            mine = mine + ["small"]
        gs = [gbig[n].reshape(N_CHIPS, 2, gbig[n].shape[1] // 2, gbig[n].shape[2]) for n in mine]
        pair_rider = _pair_exchange_rider(gs)
        (dx, dg, dsh, dsc), gots = _norm_mod_bwd(xs[i], row(norm_g, i), row(shift, i), row(scale, i), dh, dx, tag + "_norm",
                                                 dh2, pair_rider)
        d_norm_g[i] = dg[0]
        d_mod[i] = jnp.concatenate([dsh[0], dsc[0], d_gate[0]])
        rs_p.update(zip(mine, _rs_pair_sums(gots[:len(gs)], gots[len(gs):], ci, f"rs{i}")))
        riding_names = mine
    for n in ("sg_w_spatial", "sg_b_spatial"):
        gfull[n] = jnp.stack(gfull[n])
    gfull["norm_g"], gfull["final_norm_g"] = jnp.stack(d_norm_g), d_final_g[0]

    grads, deltas, new_m, new_v, red = {}, {}, {}, {}, {}

    def finish(names, tag):
        outs = _rs_finish([rs_p[n] for n in names], [rs_q[n] for n in names], chip, ci, tag)
        red.update({n: r.reshape(-1, r.shape[2]) for n, r in zip(names, outs)})

    def adamw(n, rider=None):
        res = _adamw(w[n], grads[n], given["m_" + n], given["v_" + n], "adamw_" + n, rider)
        (deltas[n], new_m[n], new_v[n]), ridden = _ridden(res, rider)
        return ridden

    def ride_exchange(names, on):
        ridden = adamw(on, _chip_exchange_rider([rs_p[n] for n in names]))
        rs_p.update(zip(names, ridden[:len(names)]))
        rs_q.update(zip(names, ridden[len(names):]))

    finish(sorted(set(rs_p) - set(riding_names)), "rs_a")
    for n in ("swa_w_in", "swa_w_out", "rwkv_w_in", "rwkv_w_out"):
        grads[n] = red[n][None]
    dmod_all = _all_gather8(jnp.stack(d_mod).reshape(DEPTH * 3 * D // RS_W, RS_W), "gather_dmod")
    grads["mod_b"] = _sum_leading(dmod_all, "sum_dmod").reshape(DEPTH, 3 * D)
    dmod_all = dmod_all.reshape(N_DEV, DEPTH, 3 * D)
    dmod_cols = lax.dynamic_slice_in_dim(dmod_all, chip * MOD_COLS, MOD_COLS, axis=2).transpose(1, 0, 2)
    grads["mod_w"] = _mod_w_grad(c_all, dmod_cols, "mod_w_grad")
    ride_exchange(["sg_w_in0"], on="mod_w")
    ride_exchange(["sg_w_out0", "small"], on="rwkv_w_in")
    finish(riding_names, "rs_b")
    grads["sg_w_in"] = jnp.stack([red["sg_w_in0"], red["sg_w_in1"]])
    grads["sg_w_out"] = jnp.stack([red["sg_w_out0"], red["sg_w_out1"]])
    small_red, off = red["small"].reshape(-1), 0
    for n in SMALL:
        grads[n] = small_red[off:off + w[n].size].reshape(w[n].shape)
        off += w[n].size

    rep_flat = jnp.concatenate([gfull[n].reshape(-1) for n in REPLICATED])
    rep_rows = _round_up(rep_flat.shape[0], 32 * RS_W) // RS_W
    rep_sum = _all_reduce_replicated(_pad_to(rep_flat, rep_rows * RS_W).reshape(rep_rows, RS_W), "rep").reshape(-1)
    off = 0
    for n in REPLICATED:
        grads[n] = rep_sum[off:off + w[n].size].reshape(w[n].shape)
        off += w[n].size

    for n in W_NAMES:
        if n not in deltas:
            adamw(n)
    return (loss, dx[None], *[grads[n] for n in W_NAMES], *[deltas[n] for n in W_NAMES],
            *[new_m[n] for n in W_NAMES], *[new_v[n] for n in W_NAMES])
```

```python
import functools
import math

import jax
import jax.numpy as jnp
from jax import lax
from jax.experimental import pallas as pl
from jax.experimental.pallas import tpu as pltpu

F32 = jnp.float32
BF16 = jnp.bfloat16
HIGHEST = lax.Precision.HIGHEST

D = 2048
DEPTH = 4
CHUNK = 128
SG_GROUPS = 16
HEAD = 64
N_HEADS = D // HEAD
KV_HEADS = 4
KVW = KV_HEADS * HEAD
ROPE_THETA = 10000.0
LORA = 96
LORA_PAD = 128
DECAY_SCALE = math.exp(-0.5)
GN_EPS = 64e-5
RMS_EPS = 1e-6
LN_EPS = 1e-5
ADAM_LR, ADAM_B1, ADAM_B2, ADAM_EPS, ADAM_WD, ADAM_STEP = 0.001, 0.9, 0.999, 1e-08, 0.01, 10
LANES = 128
NEG = -1e30
VMEM_LIMIT = 56 * 1024 * 1024

MESHT = pl.DeviceIdType.MESH


def _cparams(*sem):
    return pltpu.CompilerParams(dimension_semantics=sem, vmem_limit_bytes=VMEM_LIMIT)


class _Rider:
    def __init__(self, arrays, n_sems, start, finish):
        self.arrays, self.n_sems, self.start, self.finish = list(arrays), n_sems, start, finish


def _ridden(res, rider):
    return (res, []) if rider is None else res


def _compute_call(body, args, *, name, grid, in_specs, out_specs, out_shape, semantics, scratch_shapes=(), rider=None):
    if rider is None:
        return pl.pallas_call(body, name=name, grid=grid, in_specs=in_specs, out_specs=out_specs, out_shape=out_shape,
                              scratch_shapes=list(scratch_shapes), compiler_params=_cparams(*semantics))(*args)
    single = not isinstance(out_shape, (list, tuple))
    o_specs, o_shapes = ([out_specs], [out_shape]) if single else (list(out_specs), list(out_shape))
    n_in, n_out, n_r, n_scr = len(in_specs), len(o_specs), len(rider.arrays), len(scratch_shapes)

    def with_rider(*refs):
        ins, outs = refs[:n_in], refs[n_in + n_r:n_in + n_r + n_out]
        ridden = refs[n_in + n_r + n_out:n_in + 2 * n_r + n_out]
        scratch, (send_sems, recv_sems) = refs[n_in + 2 * n_r + n_out:-2], refs[-2:]
        ids = [pl.program_id(d) for d in range(len(grid))]
        first = functools.reduce(jnp.logical_and, [i == 0 for i in ids])
        last = functools.reduce(jnp.logical_and, [i == g - 1 for i, g in zip(ids, grid)])

        @pl.when(first)
        def _():
            rider.start(ridden, send_sems, recv_sems)

        body(*ins, *outs, *scratch)

        @pl.when(last)
        def _():
            rider.finish(ridden, send_sems, recv_sems)

    any_spec = pl.BlockSpec(memory_space=pl.ANY)
    res = pl.pallas_call(
        with_rider, name=name, grid=grid, in_specs=list(in_specs) + [any_spec] * n_r, out_specs=o_specs + [any_spec] * n_r,
        out_shape=o_shapes + [jax.ShapeDtypeStruct(a.shape, a.dtype) for a in rider.arrays],
        input_output_aliases={n_in + i: n_out + i for i in range(n_r)},
        scratch_shapes=list(scratch_shapes) + [pltpu.SemaphoreType.DMA((rider.n_sems,))] * 2,
        compiler_params=_cparams(*["arbitrary"] * len(grid)),
    )(*args, *rider.arrays)
    return (res[0] if single else list(res[:n_out])), list(res[n_out:])


_NN = (((1,), (0,)), ((), ()))
_NT = (((1,), (1,)), ((), ()))
_TN = (((0,), (0,)), ((), ()))


def _dg(a, b, dims):
    return lax.dot_general(a, b, dims, preferred_element_type=F32)


@jax.custom_vjp
def _bdot_nn(a, b):
    return _dg(a.astype(BF16), b.astype(BF16), _NN)


def _bdot_nn_fwd(a, b):
    a, b = a.astype(BF16), b.astype(BF16)
    return _dg(a, b, _NN), (a, b)


def _bdot_nn_bwd(res, ct):
    a, b = res
    ct = ct.astype(BF16)
    return _dg(ct, b, _NT), _dg(a, ct, _TN)


_bdot_nn.defvjp(_bdot_nn_fwd, _bdot_nn_bwd)


@jax.custom_vjp
def _bdot_nt(a, b):
    return _dg(a.astype(BF16), b.astype(BF16), _NT)


def _bdot_nt_fwd(a, b):
    a, b = a.astype(BF16), b.astype(BF16)
    return _dg(a, b, _NT), (a, b)


def _bdot_nt_bwd(res, ct):
    a, b = res
    ct = ct.astype(BF16)
    return _dg(ct, b, _NN), _dg(ct, a, _TN)


_bdot_nt.defvjp(_bdot_nt_fwd, _bdot_nt_bwd)


def _tile(n, cap):
    if n <= cap:
        return n
    return max(d for d in range(LANES, cap + 1, LANES) if n % d == 0)


def _matmul(a, b, form, name, out_dtype=F32, blocked=False, rider=None, tm=1024, tn=512, tk=4096):
    if form == "nn":
        (m, k), n = a.shape, (N_CHIPS * b.shape[2] if blocked else b.shape[1])
    elif form == "nt":
        m, k, n = a.shape[0], a.shape[1], (b.shape[1] if blocked else b.shape[0])
    else:
        (k, m), n = a.shape, b.shape[1]
    per_chip = (k if form == "nt" else n) // N_CHIPS
    if blocked and form == "nt":
        tk = _tile(per_chip, tk)
    elif blocked:
        tn = _tile(per_chip, tn)
    tm, tn, tk = _tile(m, tm), _tile(n, tn), _tile(k, tk)
    assert m % tm == 0 and n % tn == 0 and k % tk == 0, (name, a.shape, b.shape)
    nk = k // tk
    dims = {"nn": _NN, "nt": _NT, "tn": _TN}[form]
    a_spec = pl.BlockSpec((tk, tm), lambda i, j, l: (l, i)) if form == "tn" else pl.BlockSpec((tm, tk), lambda i, j, l: (i, l))
    b_spec = pl.BlockSpec((tn, tk), lambda i, j, l: (j, l)) if form == "nt" else pl.BlockSpec((tk, tn), lambda i, j, l: (l, j))
    o_spec = pl.BlockSpec((tm, tn), lambda i, j, l: (i, j))
    o_shape = (m, n)
    if blocked and form == "nn":
        pc = per_chip // tn
        b_spec = pl.BlockSpec((None, tk, tn), lambda i, j, l: (j // pc, l, j % pc))
    elif blocked and form == "nt":
        pc = per_chip // tk
        b_spec = pl.BlockSpec((None, tn, tk), lambda i, j, l: (l // pc, j, l % pc))
    elif blocked:
        pc = per_chip // tn
        o_spec = pl.BlockSpec((None, tm, tn), lambda i, j, l: (j // pc, i, j % pc))
        o_shape = (N_CHIPS, m, per_chip)

    def body(a_ref, b_ref, o_ref, acc_ref):
        part = _dg(a_ref[...], b_ref[...], dims)
        if nk == 1:
            o_ref[...] = part.astype(out_dtype)
        else:
            l = pl.program_id(2)

            @pl.when(l == 0)
            def _():
                acc_ref[...] = part

            @pl.when(l > 0)
            def _():
                acc_ref[...] += part

            @pl.when(l == nk - 1)
            def _():
                o_ref[...] = acc_ref[...].astype(out_dtype)

    return _compute_call(
        body, (a, b), name=name, grid=(m // tm, n // tn, nk),
        in_specs=[a_spec, b_spec], out_specs=o_spec, out_shape=jax.ShapeDtypeStruct(o_shape, out_dtype),
        scratch_shapes=[pltpu.VMEM((tm, tn) if nk > 1 else (8, LANES), F32)],
        semantics=("parallel", "parallel", "arbitrary"), rider=rider)


def _out_proj_resid(u, w_out, x, gate, name, tm=1024, tn=512):
    (m, k), n = u.shape, w_out.shape[1]

    def body(u_ref, w_ref, x_ref, g_ref, y_ref, xn_ref):
        y = _dg(u_ref[...], w_ref[...], _NN)
        y_ref[...] = y
        xn_ref[...] = x_ref[...] + g_ref[...] * y

    tile = pl.BlockSpec((tm, tn), lambda i, j: (i, j))
    out = jax.ShapeDtypeStruct((m, n), F32)
    return pl.pallas_call(
        body, name=name, grid=(m // tm, n // tn),
        in_specs=[pl.BlockSpec((tm, k), lambda i, j: (i, 0)), pl.BlockSpec((k, tn), lambda i, j: (0, j)), tile,
                  pl.BlockSpec((1, tn), lambda i, j: (0, j))],
        out_specs=[tile, tile], out_shape=[out, out], compiler_params=_cparams("parallel", "parallel"),
    )(u, w_out, x, gate)


TB_NORM = 256


def _f_norm_mod(x, g, shift, scale):
    xn = x * lax.rsqrt(jnp.mean(x * x, axis=-1, keepdims=True) + RMS_EPS)
    return (xn * g) * (1.0 + scale) + shift


def _row_spec(width, tb=TB_NORM):
    return pl.BlockSpec((tb, width), lambda i: (i, 0))


def _vec_spec(width, rows=1):
    return pl.BlockSpec((rows, width), lambda i: (0, 0))


def _norm_mod_fwd(x, g, shift, scale, name):
    t = x.shape[0]

    def body(x_ref, g_ref, sh_ref, sc_ref, h_ref):
        h_ref[...] = _f_norm_mod(x_ref[...], g_ref[...], sh_ref[...], sc_ref[...]).astype(BF16)

    return pl.pallas_call(
        body, name=name, grid=(t // TB_NORM,),
        in_specs=[_row_spec(D), _vec_spec(D), _vec_spec(D), _vec_spec(D)], out_specs=_row_spec(D),
        out_shape=jax.ShapeDtypeStruct((t, D), BF16), compiler_params=_cparams("parallel"),
    )(x, g, shift, scale)


def _accum(ref, val, first):
    @pl.when(first)
    def _():
        ref[...] = val

    @pl.when(jnp.logical_not(first))
    def _():
        ref[...] += val


def _gate_bwd_block(dx, y_ref, gate_ref, dy_ref, dgate_ref, first):
    dy_ref[...] = (dx * gate_ref[...]).astype(BF16)
    _accum(dgate_ref, jnp.sum(dx * y_ref[...], axis=0, keepdims=True), first)


def _gate_bwd_specs():
    return [_row_spec(D), _vec_spec(D)]


def _gate_bwd_shapes(t):
    return [jax.ShapeDtypeStruct((t, D), BF16), jax.ShapeDtypeStruct((1, D), F32)]


def _norm_mod_bwd(x, g, shift, scale, dh, dx_res, name, dh2=None, rider=None, below=None):
    t = x.shape[0]
    dhs = [dh] if dh2 is None else [dh, dh2]
    extra = [] if below is None else list(below)

    def body(x_ref, g_ref, sh_ref, sc_ref, dr_ref, *refs):
        dh_refs, refs = refs[:len(dhs)], refs[len(dhs):]
        below_refs, (dx_ref, dg_ref, dsh_ref, dsc_ref), below_out = refs[:len(extra)], refs[len(extra):len(extra) + 4], refs[len(extra) + 4:]
        _, vjp = jax.vjp(_f_norm_mod, x_ref[...], g_ref[...], sh_ref[...], sc_ref[...])
        dh_all = dh_refs[0][...]
        for r in dh_refs[1:]:
            dh_all = dh_all + r[...]
        dx, dg, dsh, dsc = vjp(dh_all)
        dx = dx + dr_ref[...]
        dx_ref[...] = dx
        first = pl.program_id(0) == 0
        _accum(dg_ref, dg, first)
        _accum(dsh_ref, dsh, first)
        _accum(dsc_ref, dsc, first)
        if below is not None:
            _gate_bwd_block(dx, *below_refs, *below_out, first)

    vec = jax.ShapeDtypeStruct((1, D), F32)
    return _compute_call(
        body, (x, g, shift, scale, dx_res, *dhs, *extra), name=name, grid=(t // TB_NORM,),
        in_specs=[_row_spec(D), _vec_spec(D), _vec_spec(D), _vec_spec(D), _row_spec(D)] + [_row_spec(D)] * len(dhs)
        + (_gate_bwd_specs() if extra else []),
        out_specs=[_row_spec(D), _vec_spec(D), _vec_spec(D), _vec_spec(D)] + (_gate_bwd_specs() if extra else []),
        out_shape=[jax.ShapeDtypeStruct((t, D), F32), vec, vec, vec] + (_gate_bwd_shapes(t) if extra else []),
        semantics=("arbitrary",), rider=rider)


def _f_final(x, g, target):
    xn = x * lax.rsqrt(jnp.mean(x * x, axis=-1, keepdims=True) + RMS_EPS)
    err = xn * g - target
    return 0.5 * jnp.sum(jnp.mean(err * err, axis=-1, keepdims=True), axis=0, keepdims=True)


def _final_loss_grad(x, g, target, y, gate, name):
    t = x.shape[0]

    def body(x_ref, g_ref, t_ref, y_ref, gate_ref, loss_ref, dx_ref, dg_ref, dy_ref, dgate_ref):
        loss, vjp = jax.vjp(_f_final, x_ref[...], g_ref[...], t_ref[...])
        dx, dg, _ = vjp(jnp.ones((1, 1), F32))
        dx_ref[...] = dx
        first = pl.program_id(0) == 0
        _accum(dg_ref, dg, first)
        _accum(loss_ref, jnp.broadcast_to(loss, (1, LANES)), first)
        _gate_bwd_block(dx, y_ref, gate_ref, dy_ref, dgate_ref, first)

    return pl.pallas_call(
        body, name=name, grid=(t // TB_NORM,),
        in_specs=[_row_spec(D), _vec_spec(D), _row_spec(D)] + _gate_bwd_specs(),
        out_specs=[_vec_spec(LANES), _row_spec(D), _vec_spec(D)] + _gate_bwd_specs(),
        out_shape=[jax.ShapeDtypeStruct((1, LANES), F32), jax.ShapeDtypeStruct((t, D), F32), jax.ShapeDtypeStruct((1, D), F32)]
        + _gate_bwd_shapes(t),
        compiler_params=_cparams("arbitrary"),
    )(x, g, target, y, gate)


def _group_selector():
    gi = lax.broadcasted_iota(jnp.int32, (LANES, D), 0)
    ci = lax.broadcasted_iota(jnp.int32, (LANES, D), 1)
    return (ci // (D // SG_GROUPS) == gi).astype(F32)


def _f_sg(p, ln_g, ln_b, w_s, bs_t):
    u, v, z = p[:, :D], p[:, D:2 * D], p[:, 2 * D:]
    u = jax.nn.gelu(u)
    vf = jax.nn.gelu(v)
    mean = jnp.mean(vf, axis=-1, keepdims=True)
    var = jnp.mean(jnp.square(vf - mean), axis=-1, keepdims=True)
    vn = (vf - mean) * lax.rsqrt(var + LN_EPS) * ln_g + ln_b
    ti = lax.broadcasted_iota(jnp.int32, (CHUNK, CHUNK), 0)
    si = lax.broadcasted_iota(jnp.int32, (CHUNK, CHUNK), 1)
    causal = si <= ti
    cg = D // SG_GROUPS
    f = jnp.concatenate(
        [_bdot_nn(jnp.where(causal, w_s[g], 0.0), vn[:, g * cg:(g + 1) * cg]) for g in range(SG_GROUPS)], axis=1)
    f = f + jnp.dot(bs_t, _group_selector(), precision=HIGHEST, preferred_element_type=F32)
    return u * f * jax.nn.silu(z)


def _sg_specs():
    return [pl.BlockSpec((CHUNK, 3 * D), lambda i: (i, 0)), _vec_spec(D), _vec_spec(D),
            pl.BlockSpec((SG_GROUPS, CHUNK, CHUNK), lambda i: (0, 0, 0)), _vec_spec(LANES, CHUNK)]


def _sg_fwd(p, ln_g, ln_b, w_s, bs_t, name, rider=None):
    t = p.shape[0]

    def body(p_ref, lg_ref, lb_ref, w_ref, b_ref, o_ref):
        o_ref[...] = _f_sg(p_ref[...], lg_ref[...], lb_ref[...], w_ref[...], b_ref[...]).astype(BF16)

    return _compute_call(
        body, (p, ln_g, ln_b, w_s, bs_t), name=name, grid=(t // CHUNK,), in_specs=_sg_specs(),
        out_specs=_row_spec(D, CHUNK), out_shape=jax.ShapeDtypeStruct((t, D), BF16), semantics=("parallel",), rider=rider)


def _sg_bwd(p, ln_g, ln_b, w_s, bs_t, dout, name, rider=None):
    t = p.shape[0]

    def body(p_ref, lg_ref, lb_ref, w_ref, b_ref, do_ref, dp_ref, dlg_ref, dlb_ref, dw_ref, db_ref):
        _, vjp = jax.vjp(_f_sg, p_ref[...], lg_ref[...], lb_ref[...], w_ref[...], b_ref[...])
        dp, dlg, dlb, dw, db = vjp(do_ref[...])
        dp_ref[...] = dp.astype(BF16)
        first = pl.program_id(0) == 0
        _accum(dlg_ref, dlg, first)
        _accum(dlb_ref, dlb, first)
        _accum(dw_ref, dw, first)
        _accum(db_ref, db, first)

    vec = jax.ShapeDtypeStruct((1, D), F32)
    return _compute_call(
        body, (p, ln_g, ln_b, w_s, bs_t, dout), name=name, grid=(t // CHUNK,), in_specs=_sg_specs() + [_row_spec(D, CHUNK)],
        out_specs=[pl.BlockSpec((CHUNK, 3 * D), lambda i: (i, 0)), _vec_spec(D), _vec_spec(D),
                   pl.BlockSpec((SG_GROUPS, CHUNK, CHUNK), lambda i: (0, 0, 0)), _vec_spec(LANES, CHUNK)],
        out_shape=[jax.ShapeDtypeStruct((t, 3 * D), BF16), vec, vec,
                   jax.ShapeDtypeStruct((SG_GROUPS, CHUNK, CHUNK), F32), jax.ShapeDtypeStruct((CHUNK, LANES), F32)],
        semantics=("arbitrary",), rider=rider)


SWA_COLS = 2 * D + 2 * KVW
KV_BLOCK = 2 * KVW


def _lane_roll(x, shift):
    return pltpu.roll(x, shift, 1)


def _rot_half(x):
    w = x.shape[1]
    lane = lax.broadcasted_iota(jnp.int32, x.shape, 1)
    return jnp.where(lane % HEAD < HEAD // 2, -_lane_roll(x, w - HEAD // 2), _lane_roll(x, HEAD // 2))


@jax.custom_vjp
def _rope(x, cos, sin):
    return x * cos + _rot_half(x) * sin


def _rope_fwd(x, cos, sin):
    return _rope(x, cos, sin), (cos, sin)


def _rope_bwd(res, ct):
    cos, sin = res
    return ct * cos - _rot_half(ct) * sin, jnp.zeros_like(cos), jnp.zeros_like(sin)


_rope.defvjp(_rope_fwd, _rope_bwd)


@jax.custom_vjp
def _swap_halves(x):
    return _lane_roll(x, HEAD)


_swap_halves.defvjp(lambda x: (_lane_roll(x, HEAD), None), lambda _, ct: (_lane_roll(ct, HEAD),))


def _f_swa(pq, pkv, cos, sin, cosp, sinp, sink_row, valid):
    reps = D // LANES
    q = _rope(pq[:, :D], jnp.tile(cos, (1, reps)), jnp.tile(sin, (1, reps))) * (HEAD ** -0.5)
    k = _rope(pq[:, D:D + KVW], jnp.tile(cos, (1, KVW // LANES)), jnp.tile(sin, (1, KVW // LANES)))
    kp = _rope(pkv[:, :KVW], jnp.tile(cosp, (1, KVW // LANES)), jnp.tile(sinp, (1, KVW // LANES)))
    v, vp, z = pq[:, D + KVW:D + 2 * KVW], pkv[:, KVW:], pq[:, D + 2 * KVW:]
    kcat = jnp.concatenate([kp, k], axis=0)
    vcat = jnp.concatenate([vp, v], axis=0)
    lane = lax.broadcasted_iota(jnp.int32, (2 * CHUNK, LANES), 1)
    lo = lane < HEAD
    hlane = lax.broadcasted_iota(jnp.int32, (1, LANES), 1)

    def halves(cat, g):
        blk = cat[:, (g // 2) * LANES:(g // 2 + 1) * LANES]
        other = _swap_halves(blk)
        if g % 2 == 0:
            return jnp.where(lo, blk, 0.0), jnp.where(lo, 0.0, other)
        return jnp.where(lo, other, 0.0), jnp.where(lo, 0.0, blk)

    pairs = N_HEADS // KV_HEADS // 2
    valid_g = jnp.tile(valid, (pairs, 1))

    def probs(s, heads):
        sink = jnp.concatenate(
            [jnp.broadcast_to(jnp.sum(jnp.where(hlane == h, sink_row, 0.0), axis=1, keepdims=True), (CHUNK, 1))
             for h in heads], axis=0)
        s = jnp.where(valid_g, s, NEG)
        m = lax.stop_gradient(jnp.maximum(jnp.max(s, axis=1, keepdims=True), sink))
        e = jnp.exp(s - m)
        return e / (jnp.sum(e, axis=1, keepdims=True) + jnp.exp(sink - m))

    outs = []
    for g in range(KV_HEADS):
        k_lo, k_hi = halves(kcat, g)
        v_lo, v_hi = halves(vcat, g)
        tiles = range(g * pairs, (g + 1) * pairs)
        qg = jnp.concatenate([q[:, j * LANES:(j + 1) * LANES] for j in tiles], axis=0)
        p_a = probs(_bdot_nt(qg, k_lo), [2 * j for j in tiles])
        p_b = probs(_bdot_nt(qg, k_hi), [2 * j + 1 for j in tiles])
        og = _bdot_nn(p_a, v_lo) + _bdot_nn(p_b, v_hi)
        outs += [og[n * CHUNK:(n + 1) * CHUNK] for n in range(pairs)]
    return jnp.concatenate(outs, axis=1) * jax.nn.silu(z)


def _swa_valid(block):
    qi = lax.broadcasted_iota(jnp.int32, (CHUNK, 2 * CHUNK), 0)
    kj = lax.broadcasted_iota(jnp.int32, (CHUNK, 2 * CHUNK), 1)
    rel = qi + CHUNK - kj
    return (rel >= 0) & (rel < CHUNK) & ((kj >= CHUNK) | (block > 0))


def _swa_specs(blk):
    prev = lambda i: jnp.maximum(blk(i) - 1, 0)
    kv_col = D // KV_BLOCK
    return [pl.BlockSpec((CHUNK, SWA_COLS), lambda i: (blk(i), 0)),
            pl.BlockSpec((CHUNK, KV_BLOCK), lambda i: (prev(i), kv_col)),
            pl.BlockSpec((CHUNK, LANES), lambda i: (blk(i), 0)), pl.BlockSpec((CHUNK, LANES), lambda i: (blk(i), 0)),
            pl.BlockSpec((CHUNK, LANES), lambda i: (prev(i), 0)), pl.BlockSpec((CHUNK, LANES), lambda i: (prev(i), 0)),
            _vec_spec(LANES)]


def _swa_fwd(p, cos, sin, sink_row, name, rider=None):
    t = p.shape[0]

    def body(pq_ref, pkv_ref, c_ref, s_ref, cp_ref, sp_ref, sk_ref, o_ref):
        valid = _swa_valid(pl.program_id(0))
        o_ref[...] = _f_swa(pq_ref[...], pkv_ref[...], c_ref[...], s_ref[...], cp_ref[...], sp_ref[...],
                            sk_ref[...], valid).astype(BF16)

    return _compute_call(
        body, (p, p, cos, sin, cos, sin, sink_row), name=name, grid=(t // CHUNK,), in_specs=_swa_specs(lambda i: i),
        out_specs=_row_spec(D, CHUNK), out_shape=jax.ShapeDtypeStruct((t, D), BF16), semantics=("parallel",), rider=rider)


def _swa_bwd(p, cos, sin, sink_row, dout, name, rider=None):
    t = p.shape[0]
    nb = t // CHUNK
    blk = lambda i: nb - 1 - i

    def body(pq_ref, pkv_ref, c_ref, s_ref, cp_ref, sp_ref, sk_ref, do_ref, dp_ref, dsk_ref, pend_ref):
        i = pl.program_id(0)
        valid = _swa_valid(blk(i))
        f = functools.partial(_f_swa, valid=valid)
        _, vjp = jax.vjp(f, pq_ref[...], pkv_ref[...], c_ref[...], s_ref[...], cp_ref[...], sp_ref[...], sk_ref[...])
        dpq, dpkv, _, _, _, _, dsk = vjp(do_ref[...])

        @pl.when(i == 0)
        def _():
            pend_ref[...] = jnp.zeros_like(pend_ref)

        dp_ref[...] = jnp.concatenate(
            [dpq[:, :D], dpq[:, D:D + KV_BLOCK] + pend_ref[...], dpq[:, D + KV_BLOCK:]], axis=1).astype(BF16)
        pend_ref[...] = dpkv
        _accum(dsk_ref, dsk, i == 0)

    return _compute_call(
        body, (p, p, cos, sin, cos, sin, sink_row, dout), name=name, grid=(nb,),
        in_specs=_swa_specs(blk) + [pl.BlockSpec((CHUNK, D), lambda i: (blk(i), 0))],
        out_specs=[pl.BlockSpec((CHUNK, SWA_COLS), lambda i: (blk(i), 0)), _vec_spec(LANES)],
        out_shape=[jax.ShapeDtypeStruct((t, SWA_COLS), BF16), jax.ShapeDtypeStruct((1, LANES), F32)],
        scratch_shapes=[pltpu.VMEM((CHUNK, KV_BLOCK), F32)], semantics=("arbitrary",), rider=rider)


RW_MAIN = 4 * D
RW_LO = 2 * LORA_PAD
VM = LANES // N_HEADS
VD = HEAD // VM
S_ROWS = VD * HEAD
TB_RW = 128
TB_K = 32
TB_SCAN = 16


def _dim_major(a):
    return a.reshape(a.shape[:-1] + (N_HEADS, HEAD)).swapaxes(-1, -2).reshape(a.shape)


def _head_major(a):
    return a.reshape(a.shape[:-1] + (HEAD, N_HEADS)).swapaxes(-1, -2).reshape(a.shape)


def _param_compact(w):
    return _dim_major(w).reshape(VD, LANES)


def _param_compact_inv(pc):
    return _head_major(pc.reshape(-1))


def _f_rwkv_lora(xs_lo, w0, a0, wl, al):
    decay = jnp.exp(-DECAY_SCALE * jax.nn.sigmoid(w0 + _bdot_nn(jnp.tanh(xs_lo[:, :LORA_PAD]), wl)))
    a = jax.nn.sigmoid(a0 + _bdot_nn(xs_lo[:, LORA_PAD:], al))
    return decay, a


def _prev_rows_spec(width, tb):
    return pl.BlockSpec((8, width), lambda i: (jnp.maximum(i * (tb // 8) - 1, 0), 0))


def _token_shift_lerp(p, prev8, mu, first):
    rows = lax.broadcasted_iota(jnp.int32, p.shape, 0)
    prev = jnp.where(first, 0.0, prev8[7:8, :])
    shifted = jnp.where(rows == 0, prev, pltpu.roll(p, 1, 0))
    return p + (shifted - p) * mu


def _store_compact(ref, val):
    for j in range(VD):
        ref[:, j, :] = val[:, j * LANES:(j + 1) * LANES]


def _load_flat(ref, rows=slice(None)):
    if len(ref.shape) == 2:
        return ref[rows, :]
    return jnp.concatenate([ref[rows, j, :] for j in range(VD)], axis=1)


def _flat_spec(a, tb):
    return _row_spec(a.shape[1], tb) if a.ndim == 2 else _k_spec(VD, tb)


def _rwkv_pre_fwd(p_main, p_lo, mu_main, mu_lo, w0, a0, wl, al, name):
    t = p_main.shape[0]
    tb = TB_RW

    def body(pm_ref, pmp_ref, pl_ref, plp_ref, mm_ref, ml_ref, w0_ref, a0_ref, wl_ref, al_ref,
             r_ref, k_ref, v_ref, dec_ref, a_ref, z_ref, xl_ref):
        first = pl.program_id(0) == 0
        xs = _token_shift_lerp(pm_ref[...], pmp_ref[...], mm_ref[...], first)
        for n, ref in enumerate((r_ref, k_ref, v_ref)):
            _store_compact(ref, xs[:, n * D:(n + 1) * D])
        z_ref[...] = xs[:, 3 * D:]
        xs_lo = _token_shift_lerp(pl_ref[...], plp_ref[...], ml_ref[...], first)
        xl_ref[...] = xs_lo
        decay, a = _f_rwkv_lora(xs_lo, w0_ref[...], a0_ref[...], wl_ref[...], al_ref[...])
        _store_compact(dec_ref, decay)
        _store_compact(a_ref, a)

    cl = jax.ShapeDtypeStruct((t, VD, LANES), F32)
    return pl.pallas_call(
        body, name=name, grid=(t // tb,),
        in_specs=[_row_spec(RW_MAIN, tb), _prev_rows_spec(RW_MAIN, tb), _row_spec(RW_LO, tb), _prev_rows_spec(RW_LO, tb),
                  _vec_spec(RW_MAIN), _vec_spec(RW_LO), _vec_spec(D), _vec_spec(D),
                  _vec_spec(D, LORA_PAD), _vec_spec(D, LORA_PAD)],
        out_specs=[_k_spec(VD, tb)] * 5 + [_row_spec(D, tb), _row_spec(RW_LO, tb)],
        out_shape=[cl] * 5 + [jax.ShapeDtypeStruct((t, D), F32), jax.ShapeDtypeStruct((t, RW_LO), F32)],
        compiler_params=_cparams("parallel"),
    )(p_main, p_main, p_lo, p_lo, mu_main, mu_lo, w0, a0, wl, al)


def _rwkv_lora_bwd(xs_lo, w0, a0, wl, al, ddecay, da, name):
    t = xs_lo.shape[0]
    tb = TB_NORM

    def body(x_ref, w0_ref, a0_ref, wl_ref, al_ref, dd_ref, da_ref, dx_ref, dw0_ref, da0_ref, dwl_ref, dal_ref):
        _, vjp = jax.vjp(_f_rwkv_lora, x_ref[...], w0_ref[...], a0_ref[...], wl_ref[...], al_ref[...])
        dx, dw0, da0, dwl, dal = vjp((_load_flat(dd_ref), _load_flat(da_ref)))
        dx_ref[...] = dx
        first = pl.program_id(0) == 0
        _accum(dw0_ref, dw0, first)
        _accum(da0_ref, da0, first)
        _accum(dwl_ref, dwl, first)
        _accum(dal_ref, dal, first)

    vec = jax.ShapeDtypeStruct((1, D), F32)
    lor = jax.ShapeDtypeStruct((LORA_PAD, D), F32)
    return pl.pallas_call(
        body, name=name, grid=(t // tb,),
        in_specs=[_row_spec(RW_LO), _vec_spec(D), _vec_spec(D), _vec_spec(D, LORA_PAD), _vec_spec(D, LORA_PAD),
                  _k_spec(VD, tb), _k_spec(VD, tb)],
        out_specs=[_row_spec(RW_LO), _vec_spec(D), _vec_spec(D), _vec_spec(D, LORA_PAD), _vec_spec(D, LORA_PAD)],
        out_shape=[jax.ShapeDtypeStruct((t, RW_LO), F32), vec, vec, lor, lor], compiler_params=_cparams("arbitrary"),
    )(xs_lo, w0, a0, wl, al, ddecay, da)


def _lerp_bwd(p, dxs_groups, mu, name):
    t, width = p.shape
    tb = TB_RW
    nb = t // tb
    parts = [a for group in dxs_groups for a in group]

    def body(p_ref, pp_ref, mu_ref, *refs):
        d_refs, (dp_ref, dmu_ref) = refs[:2 * len(parts)], refs[2 * len(parts):]
        i = pl.program_id(0)

        def columns(k):
            pick = (lambda r: _load_flat(r, slice(0, 1))) if k else _load_flat
            vals, at = [], 0
            for group in dxs_groups:
                vals.append(functools.reduce(jnp.add, [pick(d_refs[2 * (at + n) + k]) for n in range(len(group))]))
                at += len(group)
            return jnp.concatenate(vals, axis=1)

        pv, dv, mu_v = p_ref[...], columns(0), mu_ref[...]
        rows = lax.broadcasted_iota(jnp.int32, pv.shape, 0)
        prev = jnp.where(i == 0, 0.0, pp_ref[7:8, :])
        shifted = jnp.where(rows == 0, prev, pltpu.roll(pv, 1, 0))
        nxt = jnp.where(i == nb - 1, 0.0, columns(1))
        d_next = jnp.where(rows == tb - 1, nxt, pltpu.roll(dv, tb - 1, 0))
        dp_ref[...] = (dv * (1.0 - mu_v) + d_next * mu_v).astype(BF16)
        _accum(dmu_ref, jnp.sum(dv * (shifted - pv), axis=0, keepdims=True), i == 0)

    d_specs = []
    for a in parts:
        after = lambda i, nd=a.ndim: (jnp.minimum((i + 1) * (tb // 8), t // 8 - 1),) + (0,) * (nd - 1)
        d_specs += [_flat_spec(a, tb), pl.BlockSpec((8,) + a.shape[1:], after)]
    return pl.pallas_call(
        body, name=name, grid=(nb,),
        in_specs=[_row_spec(width, tb), _prev_rows_spec(width, tb), _vec_spec(width)] + d_specs,
        out_specs=[_row_spec(width, tb), _vec_spec(width)],
        out_shape=[jax.ShapeDtypeStruct((t, width), BF16), jax.ShapeDtypeStruct((1, width), F32)],
        compiler_params=_cparams("arbitrary"),
    )(p, p, mu, *[a for a in parts for _ in range(2)])


def _lane_group_sum2d(x):
    x = x + pltpu.roll(x, N_HEADS, 1)
    return x + pltpu.roll(x, 2 * N_HEADS, 1)


@jax.custom_vjp
def _lane_group_sum(x):
    return _lane_group_sum2d(x.reshape(-1, LANES)).reshape(x.shape)


_lane_group_sum.defvjp(lambda x: (_lane_group_sum(x), None), lambda _, ct: (_lane_group_sum(ct),))


def _head_sum(x):
    return _lane_group_sum(jnp.sum(x, axis=1, keepdims=True))


def _f_kprep(k, a, r, kkp, kap, rkp):
    kk = k * kkp
    kk = kk / jnp.maximum(jnp.sqrt(_head_sum(kk * kk)), 1e-12)
    k2 = k * (1.0 + (a - 1.0) * kap)
    return kk, k2, kk * a, _head_sum(r * k2 * rkp)


def _k_spec(rows=HEAD, tb=TB_K):
    return pl.BlockSpec((tb, rows, LANES), lambda i: (i, 0, 0))


def _kparam_spec(rows=HEAD):
    return pl.BlockSpec((rows, LANES), lambda i: (0, 0))


def _lane_group(shape):
    return lax.broadcasted_iota(jnp.int32, shape, len(shape) - 1) // N_HEADS


def _store_k_layout(ref, xc):
    x2 = xc.reshape(-1, LANES)
    group = _lane_group(x2.shape)
    shifted = [x2] + [pltpu.roll(x2, N_HEADS * k, 1) for k in range(1, VM)]
    for q in range(VM):
        out = shifted[0]
        for k in range(1, VM):
            out = jnp.where(group == (q + k) % VM, shifted[k], out)
        ref[:, pl.ds(q, VD, stride=VM), :] = out.reshape(xc.shape)


def _load_compact(ref):
    shape = (ref.shape[0], VD, LANES)
    rows = [ref[:, pl.ds(q, VD, stride=VM), :].reshape(-1, LANES) for q in range(VM)]
    group = _lane_group(rows[0].shape)
    acc = None
    for k in range(VM):
        t = rows[-k % VM]
        for g in range(1, VM):
            t = jnp.where(group == g, rows[(g - k) % VM], t)
        if k:
            t = pltpu.roll(t, LANES - N_HEADS * k, 1)
        acc = t if acc is None else acc + t
    return acc.reshape(shape)


def _rwkv_kprep_fwd(k, a, r, w, kkp, kap, rkp, name):
    t = k.shape[0]

    def body(k_ref, a_ref, r_ref, w_ref, kkp_ref, kap_ref, rkp_ref, kk_ref, k2_ref, b_ref, r4_ref, w4_ref, rk_ref):
        rv = r_ref[...]
        kk, k2, b, rk_ref[...] = _f_kprep(k_ref[...], a_ref[...], rv, kkp_ref[...], kap_ref[...], rkp_ref[...])
        for ref, val in ((kk_ref, kk), (k2_ref, k2), (b_ref, b), (r4_ref, rv), (w4_ref, w_ref[...])):
            _store_k_layout(ref, val)

    big = jax.ShapeDtypeStruct((t, HEAD, LANES), F32)
    return pl.pallas_call(
        body, name=name, grid=(t // TB_K,),
        in_specs=[_k_spec(VD)] * 4 + [_kparam_spec(VD)] * 3, out_specs=[_k_spec()] * 5 + [_k_spec(1)],
        out_shape=[big] * 5 + [jax.ShapeDtypeStruct((t, 1, LANES), F32)], compiler_params=_cparams("parallel"),
    )(k, a, r, w, kkp, kap, rkp)


def _rwkv_kprep_bwd(k, a, r, kkp, kap, rkp, dkk, dk2, db, drk, dr_scan, dw_scan, name):
    t = k.shape[0]

    def body(k_ref, a_ref, r_ref, kkp_ref, kap_ref, rkp_ref, dkk_ref, dk2_ref, db_ref, drk_ref, drs_ref, dws_ref,
             dk_ref, da_ref, dr_ref, dw_ref, dkkp_ref, dkap_ref, drkp_ref):
        _, vjp = jax.vjp(_f_kprep, k_ref[...], a_ref[...], r_ref[...], kkp_ref[...], kap_ref[...], rkp_ref[...])
        dk, da, dr, dkkp, dkap, drkp = vjp((_load_compact(dkk_ref), _load_compact(dk2_ref), _load_compact(db_ref),
                                            drk_ref[...]))
        dk_ref[...] = dk
        da_ref[...] = da
        dr_ref[...] = dr + _load_compact(drs_ref)
        dw_ref[...] = _load_compact(dws_ref)
        first = pl.program_id(0) == 0
        _accum(dkkp_ref, dkkp, first)
        _accum(dkap_ref, dkap, first)
        _accum(drkp_ref, drkp, first)

    cl = jax.ShapeDtypeStruct((t, VD, LANES), F32)
    par = jax.ShapeDtypeStruct((VD, LANES), F32)
    return pl.pallas_call(
        body, name=name, grid=(t // TB_K,),
        in_specs=[_k_spec(VD)] * 3 + [_kparam_spec(VD)] * 3 + [_k_spec()] * 3 + [_k_spec(1), _k_spec(), _k_spec()],
        out_specs=[_k_spec(VD)] * 4 + [_kparam_spec(VD)] * 3,
        out_shape=[cl] * 4 + [par] * 3, compiler_params=_cparams("arbitrary"),
    )(k, a, r, kkp, kap, rkp, dkk, dk2, db, drk, dr_scan, dw_scan)


def _f_post(y, v, rk, g, b):
    mean = _lane_group_sum(jnp.sum(y, axis=1, keepdims=True)) * (1.0 / HEAD)
    yc = y - mean
    var = _lane_group_sum(jnp.sum(yc * yc, axis=1, keepdims=True)) * (1.0 / HEAD)
    return yc * lax.rsqrt(var + GN_EPS) * g + b + rk * v


def _rwkv_post_fwd(y, v, rk, g, b, name):
    t = y.shape[0]

    def body(y_ref, v_ref, rk_ref, g_ref, b_ref, o_ref):
        o_ref[...] = _f_post(y_ref[...], v_ref[...], rk_ref[...], g_ref[...], b_ref[...])

    return pl.pallas_call(
        body, name=name, grid=(t // TB_K,),
        in_specs=[_k_spec(VD), _k_spec(VD), _k_spec(1), _kparam_spec(VD), _kparam_spec(VD)], out_specs=_k_spec(VD),
        out_shape=jax.ShapeDtypeStruct((t, VD, LANES), F32), compiler_params=_cparams("parallel"),
    )(y, v, rk, g, b)


def _rwkv_post_bwd(y, v, rk, g, b, do, name):
    t = y.shape[0]

    def body(y_ref, v_ref, rk_ref, g_ref, b_ref, do_ref, dy_ref, dv_ref, drk_ref, dg_ref, db_ref):
        _, vjp = jax.vjp(_f_post, y_ref[...], v_ref[...], rk_ref[...], g_ref[...], b_ref[...])
        dy, dv, drk, dg, db = vjp(do_ref[...])
        dy_ref[...] = dy
        dv_ref[...] = dv
        drk_ref[...] = drk
        first = pl.program_id(0) == 0
        _accum(dg_ref, dg, first)
        _accum(db_ref, db, first)

    vl = jax.ShapeDtypeStruct((t, VD, LANES), F32)
    par = jax.ShapeDtypeStruct((VD, LANES), F32)
    return pl.pallas_call(
        body, name=name, grid=(t // TB_K,),
        in_specs=[_k_spec(VD), _k_spec(VD), _k_spec(1), _kparam_spec(VD), _kparam_spec(VD), _k_spec(VD)],
        out_specs=[_k_spec(VD), _k_spec(VD), _k_spec(1), _kparam_spec(VD), _kparam_spec(VD)],
        out_shape=[vl, vl, jax.ShapeDtypeStruct((t, 1, LANES), F32), par, par], compiler_params=_cparams("arbitrary"),
    )(y, v, rk, g, b, do)


def _f_gate(o, z):
    return o * jax.nn.silu(z)


def _rwkv_gate_fwd(o, z, name):
    t = z.shape[0]

    def body(o_ref, z_ref, u_ref):
        u_ref[...] = _f_gate(_load_flat(o_ref), z_ref[...]).astype(BF16)

    return pl.pallas_call(
        body, name=name, grid=(t // TB_NORM,), in_specs=[_k_spec(VD, TB_NORM), _row_spec(D)], out_specs=_row_spec(D),
        out_shape=jax.ShapeDtypeStruct((t, D), BF16), compiler_params=_cparams("parallel"),
    )(o, z)


def _rwkv_gate_bwd(o, z, du, name):
    t = z.shape[0]

    def body(o_ref, z_ref, du_ref, do_ref, dz_ref):
        _, vjp = jax.vjp(_f_gate, _load_flat(o_ref), z_ref[...])
        do, dz_ref[...] = vjp(du_ref[...])
        _store_compact(do_ref, do)

    return pl.pallas_call(
        body, name=name, grid=(t // TB_NORM,), in_specs=[_k_spec(VD, TB_NORM), _row_spec(D), _row_spec(D)],
        out_specs=[_k_spec(VD, TB_NORM), _row_spec(D)],
        out_shape=[jax.ShapeDtypeStruct((t, VD, LANES), F32), jax.ShapeDtypeStruct((t, D), F32)],
        compiler_params=_cparams("parallel"),
    )(o, z, du)


def _colsum(x):
    return jnp.sum(x, axis=0, keepdims=True)


def _rwkv_scan_fwd(r4, w4, k24, kk4, b4, v, name, rider=None):
    t = r4.shape[0]
    tb = TB_SCAN

    def body(r_ref, w_ref, k2_ref, kk_ref, b_ref, v_ref, y_ref, sall_ref, sa_ref, s_scr):
        @pl.when(pl.program_id(0) == 0)
        def _():
            s_scr[...] = jnp.zeros_like(s_scr)

        sall_ref[0] = s_scr[...]

        def step(tt, dst):
            kk = kk_ref[tt]
            sas = []
            for vd in range(VD):
                sa = _colsum(sall_ref[tt, pl.ds(vd * HEAD, HEAD), :] * kk)
                sa_ref[tt, pl.ds(vd, 1), :] = sa
                sas.append(sa)
            w, b, k2, r = w_ref[tt], b_ref[tt], k2_ref[tt], r_ref[tt]
            for vd in range(VD):
                rows = pl.ds(vd * HEAD, HEAD)
                s = sall_ref[tt, rows, :] * w - sas[vd] * b + v_ref[tt, pl.ds(vd, 1), :] * k2
                dst[rows, :] = s
                y_ref[tt, pl.ds(vd, 1), :] = _colsum(s * r)

        def loop_step(tt, carry):
            step(tt, sall_ref.at[tt + 1])
            return carry

        lax.fori_loop(0, tb - 1, loop_step, 0)
        step(tb - 1, s_scr)

    vl = jax.ShapeDtypeStruct((t, VD, LANES), F32)
    return _compute_call(
        body, (r4, w4, k24, kk4, b4, v), name=name, grid=(t // tb,),
        in_specs=[_k_spec(HEAD, tb)] * 5 + [_k_spec(VD, tb)],
        out_specs=[_k_spec(VD, tb), _k_spec(S_ROWS, tb), _k_spec(VD, tb)],
        out_shape=[vl, jax.ShapeDtypeStruct((t, S_ROWS, LANES), F32), vl],
        scratch_shapes=[pltpu.VMEM((S_ROWS, LANES), F32)], semantics=("arbitrary",), rider=rider)


def _rwkv_scan_bwd(dy, s_all, sa_all, r4, w4, k24, kk4, b4, v, name, rider=None):
    t = r4.shape[0]
    tb = TB_SCAN
    nb = t // tb
    blk = lambda i: nb - 1 - i

    def body(dy_ref, sall_ref, sa_ref, r_ref, w_ref, k2_ref, kk_ref, b_ref, v_ref,
             dr_ref, dw_ref, dk2_ref, dkk_ref, db_ref, dv_ref, ds_scr):
        @pl.when(pl.program_id(0) == 0)
        def _():
            ds_scr[...] = jnp.zeros_like(ds_scr)

        def step(j, carry):
            tt = tb - 1 - j
            vrow = lambda ref, vd: ref[tt, pl.ds(vd, 1), :]
            srows = lambda vd: pl.ds(vd * HEAD, HEAD)
            r, k2, b = r_ref[tt], k2_ref[tt], b_ref[tt]
            dsas = []
            for vd in range(VD):
                ds = ds_scr[srows(vd), :] + vrow(dy_ref, vd) * r
                ds_scr[srows(vd), :] = ds
                dv_ref[tt, pl.ds(vd, 1), :] = _colsum(ds * k2)
                dsas.append(-_colsum(ds * b))
            zero = jnp.zeros((HEAD, LANES), F32)
            dk2, q, sady, vdy = zero, zero, 0.0, 0.0
            for vd in range(VD):
                dyv = vrow(dy_ref, vd)
                dk2 = dk2 + ds_scr[srows(vd), :] * vrow(v_ref, vd)
                q = q + sall_ref[tt, srows(vd), :] * dyv
                sady = sady + vrow(sa_ref, vd) * dyv
                vdy = vdy + vrow(v_ref, vd) * dyv
            dk2_ref[tt] = dk2
            dr_ref[tt] = w_ref[tt] * q - b_ref[tt] * sady + k2_ref[tt] * vdy
            dw, dkk = zero, zero
            for vd in range(VD):
                sp = sall_ref[tt, srows(vd), :]
                dw = dw + ds_scr[srows(vd), :] * sp
                dkk = dkk + sp * dsas[vd]
            dw_ref[tt] = dw
            dkk_ref[tt] = dkk
            w, kk = w_ref[tt], kk_ref[tt]
            db = zero
            for vd in range(VD):
                ds = ds_scr[srows(vd), :]
                db = db - ds * vrow(sa_ref, vd)
                ds_scr[srows(vd), :] = ds * w + dsas[vd] * kk
            db_ref[tt] = db
            return carry

        lax.fori_loop(0, tb, step, 0)

    rk = lambda rows: pl.BlockSpec((tb, rows, LANES), lambda i: (blk(i), 0, 0))
    big = jax.ShapeDtypeStruct((t, HEAD, LANES), F32)
    return _compute_call(
        body, (dy, s_all, sa_all, r4, w4, k24, kk4, b4, v), name=name, grid=(nb,),
        in_specs=[rk(VD), rk(S_ROWS), rk(VD)] + [rk(HEAD)] * 5 + [rk(VD)],
        out_specs=[rk(HEAD)] * 5 + [rk(VD)],
        out_shape=[big] * 5 + [jax.ShapeDtypeStruct((t, VD, LANES), F32)],
        scratch_shapes=[pltpu.VMEM((S_ROWS, LANES), F32)], semantics=("arbitrary",), rider=rider)


def _rwkv_mixer_fwd(p_main, p_lo, prm, tag, rider):
    r, k, v, w, a, z, xs_lo = _rwkv_pre_fwd(p_main, p_lo, prm["mu_main"], prm["mu_lo"], prm["w0"], prm["a0"],
                                            prm["wl"], prm["al"], tag + "_pre")
    kk4, k24, b4, r4, w4, rk = _rwkv_kprep_fwd(k, a, r, w, prm["kkp"], prm["kap"], prm["rkp"], tag + "_kprep")
    (y, s_all, sa_all), ridden = _ridden(_rwkv_scan_fwd(r4, w4, k24, kk4, b4, v, tag + "_scan", rider), rider)
    o = _rwkv_post_fwd(y, v, rk, prm["gn_g"], prm["gn_b"], tag + "_post")
    u = _rwkv_gate_fwd(o, z, tag + "_gate")
    saved = dict(z=z, xs_lo=xs_lo, r=r, k=k, a=a, v=v, r4=r4, w4=w4, kk4=kk4, k24=k24, b4=b4, rk=rk,
                 y=y, s_all=s_all, sa_all=sa_all, o=o)
    return u, saved, ridden


def _rwkv_mixer_bwd(p_main, p_lo, prm, sv, du, tag, rider):
    do, dz = _rwkv_gate_bwd(sv["o"], sv["z"], du, tag + "_gate_b")
    dy, dv_post, drk, dgn_g, dgn_b = _rwkv_post_bwd(sv["y"], sv["v"], sv["rk"], prm["gn_g"], prm["gn_b"], do,
                                                    tag + "_post_b")
    (dr_s, dw_s, dk24, dkk4, db4, dv_scan), ridden = _ridden(_rwkv_scan_bwd(
        dy, sv["s_all"], sv["sa_all"], sv["r4"], sv["w4"], sv["k24"], sv["kk4"], sv["b4"], sv["v"], tag + "_scan_b", rider), rider)
    dk, da, dr, dw, dkkp, dkap, drkp = _rwkv_kprep_bwd(sv["k"], sv["a"], sv["r"], prm["kkp"], prm["kap"], prm["rkp"],
                                                       dkk4, dk24, db4, drk, dr_s, dw_s, tag + "_kprep_b")
    dxs_lo, dw0, da0, dwl, dal = _rwkv_lora_bwd(sv["xs_lo"], prm["w0"], prm["a0"], prm["wl"], prm["al"], dw, da,
                                                tag + "_lora_b")
    dxs_main = [[dr], [dk], [dv_post, dv_scan], [dz]]
    dp_main, dmu_main = _lerp_bwd(p_main, dxs_main, prm["mu_main"], tag + "_lerp_main_b")
    dp_lo, dmu_lo = _lerp_bwd(p_lo, [[dxs_lo]], prm["mu_lo"], tag + "_lerp_lo_b")
    grads = dict(mu_main=dmu_main, mu_lo=dmu_lo, w0=dw0, a0=da0, wl=dwl, al=dal, kkp=dkkp, kap=dkap, rkp=drkp,
                 gn_g=dgn_g, gn_b=dgn_b)
    return dp_main, dp_lo, grads, ridden


N_DEV = 8
N_CHIPS = 4
ANY = pl.BlockSpec(memory_space=pl.ANY)


def _place():
    return lax.axis_index("x"), lax.axis_index("y"), lax.axis_index("c")


def _remote(src, dst, send_sems, recv_sems, k, dev):
    return pltpu.make_async_remote_copy(src_ref=src, dst_ref=dst, send_sem=send_sems.at[k], recv_sem=recv_sems.at[k],
                                        device_id=dev, device_id_type=MESHT)


def _all_gather8(v, name):
    def body(buf_ref, out_ref, send_sems, recv_sems):
        del buf_ref
        x, y, c = _place()
        mine = out_ref.at[4 * x + 2 * y + c]
        peers = [(x ^ (k >> 2), y ^ ((k >> 1) & 1), c ^ (k & 1)) for k in range(1, N_DEV)]
        sends = [_remote(mine, mine, send_sems, recv_sems, k, peer) for k, peer in enumerate(peers)]
        for cp in sends:
            cp.start()
        for k, (px, py, pc) in enumerate(peers):
            _remote(mine, out_ref.at[4 * px + 2 * py + pc], send_sems, recv_sems, k, (x, y, c)).wait_recv()
        for cp in sends:
            cp.wait_send()

    return pl.pallas_call(
        body, name=name, in_specs=[ANY], out_specs=ANY, input_output_aliases={0: 0},
        out_shape=jax.ShapeDtypeStruct((N_DEV,) + v.shape, v.dtype),
        scratch_shapes=[pltpu.SemaphoreType.DMA((N_DEV - 1,)), pltpu.SemaphoreType.DMA((N_DEV - 1,))],
    )(jnp.broadcast_to(v[None], (N_DEV,) + v.shape))


def _other_chips(x, y):
    return [(1 - x, y), (x, 1 - y), (1 - x, 1 - y)]


GATHER_SEMS = 6


def _gather_buffer(v):
    return jnp.broadcast_to(v[None], (N_CHIPS,) + v.shape)


def _gather_start(bufs, send_sems, recv_sems):
    x, y, c = _place()
    for i, buf in enumerate(bufs):
        mine = buf.at[2 * x + y, c]
        for j, (cx, cy) in enumerate(_other_chips(x, y)):
            _remote(mine, mine, send_sems, recv_sems, GATHER_SEMS * i + j, (cx, cy, c)).start()


def _gather_finish(bufs, send_sems, recv_sems):
    x, y, c = _place()
    chips = _other_chips(x, y)
    passed = []
    for i, buf in enumerate(bufs):
        mine = buf.at[2 * x + y, c]
        for j, (cx, cy) in enumerate(chips):
            landed = buf.at[2 * cx + cy, c]
            _remote(mine, landed, send_sems, recv_sems, GATHER_SEMS * i + j, (x, y, c)).wait_recv()
            fwd = _remote(landed, landed, send_sems, recv_sems, GATHER_SEMS * i + 3 + j, (x, y, 1 - c))
            fwd.start()
            passed.append(fwd)
    for i, buf in enumerate(bufs):
        mine = buf.at[2 * x + y, c]
        for j, (cx, cy) in enumerate(chips):
            _remote(mine, buf.at[2 * cx + cy, 1 - c], send_sems, recv_sems, GATHER_SEMS * i + 3 + j, (x, y, c)).wait_recv()
            _remote(mine, mine, send_sems, recv_sems, GATHER_SEMS * i + j, (cx, cy, c)).wait_send()
    for fwd in passed:
        fwd.wait_send()


def _gather_rider(bufs):
    return _Rider(bufs, GATHER_SEMS * len(bufs), _gather_start, _gather_finish)


def _chip_gather(bufs, name):
    n = len(bufs)

    def body(*refs):
        out_refs, (send_sems, recv_sems) = refs[n:2 * n], refs[2 * n:]
        _gather_start(out_refs, send_sems, recv_sems)
        _gather_finish(out_refs, send_sems, recv_sems)

    return pl.pallas_call(
        body, name=name, in_specs=[ANY] * n, out_specs=[ANY] * n, input_output_aliases={i: i for i in range(n)},
        out_shape=[jax.ShapeDtypeStruct(b.shape, b.dtype) for b in bufs], scratch_shapes=_dma_sems(GATHER_SEMS * n),
    )(*bufs)


RS_W = 1024
RS_BLOCK_BYTES = 4 << 20


def _dma_sems(n):
    return [pltpu.SemaphoreType.DMA((n,)), pltpu.SemaphoreType.DMA((n,))]


def _pair_exchange_copies(refs, send_sems, recv_sems):
    n = len(refs) // 2
    x, y, c = _place()
    return [_remote(refs[i].at[s, 1 - c], refs[n + i].at[s], send_sems, recv_sems, N_CHIPS * i + s, (x, y, 1 - c))
            for i in range(n) for s in range(N_CHIPS)]


def _pair_exchange_start(refs, send_sems, recv_sems):
    for cp in _pair_exchange_copies(refs, send_sems, recv_sems):
        cp.start()


def _pair_exchange_finish(refs, send_sems, recv_sems):
    for cp in _pair_exchange_copies(refs, send_sems, recv_sems):
        cp.wait()


def _pair_exchange_rider(gs):
    landing = [lax.empty((N_CHIPS,) + g.shape[2:], g.dtype) for g in gs]
    return _Rider(list(gs) + landing, N_CHIPS * len(gs), _pair_exchange_start, _pair_exchange_finish)


def _rs_rows(rows, cols):
    cap = max(16, RS_BLOCK_BYTES // (N_CHIPS * 4 * cols))
    return rows if rows <= cap else max(d for d in range(16, cap + 1, 16) if rows % d == 0)


def _rs_pair_add(g, got, c_arr, name):
    _, _, rows, width = g.shape
    tr = _rs_rows(rows, width)

    def body(c_ref, g_ref, got_ref, p_ref):
        p_ref[...] = (g_ref[...] + got_ref[...]).astype(BF16)

    return pl.pallas_call(
        body, name=name,
        grid_spec=pltpu.PrefetchScalarGridSpec(
            num_scalar_prefetch=1, grid=(rows // tr,),
            in_specs=[pl.BlockSpec((N_CHIPS, None, tr, width), lambda i, c_ref: (0, c_ref[0], i, 0)),
                      pl.BlockSpec((N_CHIPS, tr, width), lambda i, c_ref: (0, i, 0))],
            out_specs=pl.BlockSpec((N_CHIPS, tr, width), lambda i, c_ref: (0, i, 0))),
        out_shape=jax.ShapeDtypeStruct((N_CHIPS, rows, width), BF16), compiler_params=_cparams("parallel"),
    )(c_arr, g, got)


def _chip_exchange_copies(refs, send_sems, recv_sems):
    n = len(refs) // 2
    x, y, c = _place()
    return [_remote(refs[i].at[2 * cx + cy], refs[n + i].at[j], send_sems, recv_sems, 3 * i + j, (cx, cy, c))
            for i in range(n) for j, (cx, cy) in enumerate(_other_chips(x, y))]


def _chip_exchange_start(refs, send_sems, recv_sems):
    for cp in _chip_exchange_copies(refs, send_sems, recv_sems):
        cp.start()


def _chip_exchange_finish(refs, send_sems, recv_sems):
    n = len(refs) // 2
    x, y, c = _place()
    for i in range(n):
        for j in range(3):
            _remote(refs[i].at[2 * x + y], refs[n + i].at[j], send_sems, recv_sems, 3 * i + j, (x, y, c)).wait_recv()
    for cp in _chip_exchange_copies(refs, send_sems, recv_sems):
        cp.wait_send()


def _chip_exchange_buffers(ps):
    return [lax.empty((3,) + p.shape[1:], p.dtype) for p in ps]


def _chip_exchange_rider(ps):
    return _Rider(list(ps) + _chip_exchange_buffers(ps), 3 * len(ps), _chip_exchange_start, _chip_exchange_finish)


def _rs_chip_add(p, q, idx, name):
    _, rows, width = q.shape
    tr = _rs_rows(rows, width)

    def body(idx_ref, p_ref, q_ref, r_ref):
        qv = q_ref[...].astype(F32)
        r_ref[...] = ((p_ref[...].astype(F32) + qv[0]) + qv[1]) + qv[2]

    return pl.pallas_call(
        body, name=name,
        grid_spec=pltpu.PrefetchScalarGridSpec(
            num_scalar_prefetch=1, grid=(rows // tr,),
            in_specs=[pl.BlockSpec((None, tr, width), lambda i, idx_ref: (idx_ref[0], i, 0)),
                      pl.BlockSpec((3, tr, width), lambda i, idx_ref: (0, i, 0))],
            out_specs=pl.BlockSpec((None, tr, width), lambda i, idx_ref: (idx_ref[1], i, 0))),
        out_shape=jax.ShapeDtypeStruct((2, rows, width), F32), compiler_params=_cparams("parallel"),
    )(idx, p, q)


def _rs_pair_share(rs, name):
    n = len(rs)

    def body(*refs):
        out_refs, (send_sems, recv_sems) = refs[n:2 * n], refs[2 * n:]
        x, y, c = _place()
        sends = [_remote(out_refs[i].at[c], out_refs[i].at[c], send_sems, recv_sems, i, (x, y, 1 - c)) for i in range(n)]
        for cp in sends:
            cp.start()
        for i in range(n):
            _remote(out_refs[i].at[c], out_refs[i].at[1 - c], send_sems, recv_sems, i, (x, y, c)).wait_recv()
        for cp in sends:
            cp.wait_send()

    return pl.pallas_call(
        body, name=name, in_specs=[ANY] * n, out_specs=[ANY] * n, input_output_aliases={i: i for i in range(n)},
        out_shape=[jax.ShapeDtypeStruct(r.shape, r.dtype) for r in rs], scratch_shapes=_dma_sems(n),
    )(*rs)


def _rs_pair_sums(gs, gots, core, tag):
    c_arr = core.astype(jnp.int32).reshape(1)
    return [_rs_pair_add(g, got, c_arr, f"{tag}_pair_add{i}") for i, (g, got) in enumerate(zip(gs, gots))]


def _rs_finish(ps, qs, chip, core, tag):
    idx = jnp.stack([chip, core]).astype(jnp.int32)
    rs = [_rs_chip_add(p, q, idx, f"{tag}_chip_add{i}") for i, (p, q) in enumerate(zip(ps, qs))]
    return _rs_pair_share(rs, tag + "_share")


def _sum_leading(a, name):
    n, rows, width = a.shape
    cap = max(8, RS_BLOCK_BYTES // (n * 4 * width))
    tr = rows if rows <= cap else max(d for d in range(8, cap + 1, 8) if rows % d == 0)

    def body(a_ref, o_ref):
        acc = a_ref[0]
        for d in range(1, n):
            acc = acc + a_ref[d]
        o_ref[...] = acc

    return pl.pallas_call(
        body, name=name, grid=(rows // tr,), in_specs=[pl.BlockSpec((n, tr, width), lambda i: (0, i, 0))],
        out_specs=pl.BlockSpec((tr, width), lambda i: (i, 0)), out_shape=jax.ShapeDtypeStruct((rows, width), F32),
        compiler_params=_cparams("parallel"),
    )(a)


def _pair_swap(v, name):
    def body(v_ref, got_ref, send_sems, recv_sems):
        x, y, c = _place()
        cp = _remote(v_ref, got_ref, send_sems, recv_sems, 0, (x, y, 1 - c))
        cp.start()
        cp.wait()

    return pl.pallas_call(body, name=name, in_specs=[ANY], out_specs=ANY, out_shape=jax.ShapeDtypeStruct(v.shape, v.dtype),
                          scratch_shapes=_dma_sems(1))(v)


def _all_reduce_replicated(v, name):
    rows, width = v.shape
    pair = _sum_leading(jnp.stack([v, _pair_swap(v, name + "_swap")]), name + "_pair_add")
    (gathered,) = _chip_gather([_gather_buffer(pair.reshape(2, rows // 2, width))], name + "_gather")
    return _sum_leading(gathered.reshape(N_CHIPS, rows, width), name + "_chip_add")


MOD_COLS = 3 * D // N_CHIPS
MOD_TK = 512


def _mod_partial(c_all, mod_w, name):
    nk = D // MOD_TK

    def body(c_ref, w_ref, o_ref):
        l = pl.program_id(1)
        part = _bdot_nn(jax.nn.silu(c_ref[...]), w_ref[0])
        _accum(o_ref.at[0], part, l == 0)

    return pl.pallas_call(
        body, name=name, grid=(DEPTH, nk),
        in_specs=[pl.BlockSpec((N_DEV, MOD_TK), lambda i, l: (0, l)), pl.BlockSpec((1, MOD_TK, MOD_COLS), lambda i, l: (i, l, 0))],
        out_specs=pl.BlockSpec((1, N_DEV, MOD_COLS), lambda i, l: (i, 0, 0)),
        out_shape=jax.ShapeDtypeStruct((DEPTH, N_DEV, MOD_COLS), F32), compiler_params=_cparams("parallel", "arbitrary"),
    )(c_all, mod_w)


def _mod_w_grad(c_all, dmod, name):
    def body(c_ref, d_ref, o_ref):
        o_ref[0] = _dg(jax.nn.silu(c_ref[...]).astype(BF16), d_ref[0].astype(BF16), _TN)

    return pl.pallas_call(
        body, name=name, grid=(DEPTH, D // MOD_TK),
        in_specs=[pl.BlockSpec((N_DEV, MOD_TK), lambda i, l: (0, l)), pl.BlockSpec((1, N_DEV, MOD_COLS), lambda i, l: (i, 0, 0))],
        out_specs=pl.BlockSpec((1, MOD_TK, MOD_COLS), lambda i, l: (i, l, 0)),
        out_shape=jax.ShapeDtypeStruct((DEPTH, D, MOD_COLS), F32), compiler_params=_cparams("parallel", "parallel"),
    )(c_all, dmod)


ADAM_BLOCK_BYTES = 1 << 20


def _adamw(w, g, m, v, name, rider=None):
    shape = w.shape
    cols = shape[-1]
    rows = w.size // cols
    w, g, m, v = (a.reshape(rows, cols) for a in (w, g, m, v))
    cap = max(8, ADAM_BLOCK_BYTES // (4 * cols))
    tr = rows if rows <= cap else max(d for d in range(8, cap + 1, 8) if rows % d == 0)
    c1 = 1.0 - ADAM_B1 ** ADAM_STEP
    c2 = 1.0 - ADAM_B2 ** ADAM_STEP

    def body(w_ref, g_ref, m_ref, v_ref, d_ref, nm_ref, nv_ref):
        gv = g_ref[...]
        mn = ADAM_B1 * m_ref[...] + (1.0 - ADAM_B1) * gv
        vn = ADAM_B2 * v_ref[...] + (1.0 - ADAM_B2) * (gv * gv)
        nm_ref[...] = mn
        nv_ref[...] = vn
        d_ref[...] = -ADAM_LR * ((mn / c1) / (jnp.sqrt(vn / c2) + ADAM_EPS) + ADAM_WD * w_ref[...])

    spec = pl.BlockSpec((tr, cols), lambda i: (i, 0))
    out = jax.ShapeDtypeStruct((rows, cols), F32)
    (d, nm, nv), ridden = _ridden(_compute_call(
        body, (w, g, m, v), name=name, grid=(rows // tr,), in_specs=[spec] * 4, out_specs=[spec] * 3, out_shape=[out] * 3,
        semantics=("parallel",), rider=rider), rider)
    res = (d.reshape(shape), nm.reshape(shape), nv.reshape(shape))
    return res if rider is None else (res, ridden)


W_NAMES = ("norm_g", "mod_w", "mod_b", "final_norm_g", "sg_w_in", "sg_w_out", "sg_ln_g", "sg_ln_b", "sg_w_spatial",
           "sg_b_spatial", "swa_w_in", "swa_w_out", "swa_sinks", "rwkv_w_in", "rwkv_w_out", "rwkv_mu", "rwkv_w0",
           "rwkv_w_lora", "rwkv_a0", "rwkv_a_lora", "rwkv_k_k", "rwkv_k_a", "rwkv_r_k", "rwkv_gn_g", "rwkv_gn_b")
SMALL = {"sg_ln_g": 1, "sg_ln_b": 1, "rwkv_mu": 1, "rwkv_w0": 1, "rwkv_w_lora": 2, "rwkv_a0": 1, "rwkv_a_lora": 2,
         "rwkv_k_k": 1, "rwkv_k_a": 1, "rwkv_gn_g": 1, "rwkv_gn_b": 1}
REPLICATED = ("norm_g", "final_norm_g", "sg_w_spatial", "sg_b_spatial", "swa_sinks", "rwkv_r_k")
KINDS = ("sg", "swa", "rwkv", "sg")


def _pad_to(flat, n):
    return jnp.pad(flat, (0, n - flat.shape[0]))


def _round_up(n, m):
    return -(-n // m) * m


def _join_shards(gathered, axis):
    return jnp.concatenate([gathered[s] for s in range(N_CHIPS)], axis=axis)


def _chip_blocks(full, axis):
    return jnp.stack(jnp.split(full, N_CHIPS, axis=axis)).reshape(N_CHIPS, -1)


def _weight_buffer(w):
    rows, cols = w.shape
    return _gather_buffer(w.astype(BF16).reshape(2, rows // 2, cols))


def _chip_shards(buf):
    return buf.reshape(N_CHIPS, -1, buf.shape[-1])


def _small_buffer(shards):
    flat = jnp.concatenate([shards[n].reshape(-1) for n in SMALL])
    rows = _round_up(flat.shape[0], 2 * 8 * LANES) // (2 * LANES)
    return _gather_buffer(_pad_to(flat, 2 * rows * LANES).reshape(2, rows, LANES))


def _unpack_small(buf, shards):
    got = buf.reshape(N_CHIPS, -1)
    out, off = {}, 0
    for n, axis in SMALL.items():
        size = shards[n].size
        out[n] = _join_shards(got[:, off:off + size].reshape((N_CHIPS,) + shards[n].shape), axis)
        off += size
    return out


def _lora_pad_rows(w):
    return jnp.pad(w, ((0, LORA_PAD - LORA), (0, 0)))


def _lo_cols(a):
    z = jnp.zeros(a.shape[:-1] + (LORA_PAD - LORA,), a.dtype)
    return jnp.concatenate([a[..., :LORA], z, a[..., LORA:], z], axis=-1)


def _lo_cols_inv(a):
    return jnp.concatenate([a[..., :LORA], a[..., LORA_PAD:LORA_PAD + LORA]], axis=-1)


def _rows_dim_major(w):
    return w.reshape(N_HEADS, HEAD, -1).swapaxes(0, 1).reshape(w.shape)


def _rows_head_major(w):
    return w.reshape(HEAD, N_HEADS, -1).swapaxes(0, 1).reshape(w.shape)


def kernel(x, c, positions, norm_g, mod_w, mod_b, final_norm_g, sg_w_in, sg_w_out, sg_ln_g, sg_ln_b, sg_w_spatial,
           sg_b_spatial, swa_w_in, swa_w_out, swa_sinks, rwkv_w_in, rwkv_w_out, rwkv_mu, rwkv_w0, rwkv_w_lora, rwkv_a0,
           rwkv_a_lora, rwkv_k_k, rwkv_k_a, rwkv_r_k, rwkv_gn_g, rwkv_gn_b, loss_target, m_norm_g, m_mod_w, m_mod_b,
           m_final_norm_g, m_sg_w_in, m_sg_w_out, m_sg_ln_g, m_sg_ln_b, m_sg_w_spatial, m_sg_b_spatial, m_swa_w_in,
           m_swa_w_out, m_swa_sinks, m_rwkv_w_in, m_rwkv_w_out, m_rwkv_mu, m_rwkv_w0, m_rwkv_w_lora, m_rwkv_a0,
           m_rwkv_a_lora, m_rwkv_k_k, m_rwkv_k_a, m_rwkv_r_k, m_rwkv_gn_g, m_rwkv_gn_b, v_norm_g, v_mod_w, v_mod_b,
           v_final_norm_g, v_sg_w_in, v_sg_w_out, v_sg_ln_g, v_sg_ln_b, v_sg_w_spatial, v_sg_b_spatial, v_swa_w_in,
           v_swa_w_out, v_swa_sinks, v_rwkv_w_in, v_rwkv_w_out, v_rwkv_mu, v_rwkv_w0, v_rwkv_w_lora, v_rwkv_a0,
           v_rwkv_a_lora, v_rwkv_k_k, v_rwkv_k_a, v_rwkv_r_k, v_rwkv_gn_g, v_rwkv_gn_b):
    given = dict(locals())
    w = {n: given[n] for n in W_NAMES}
    xi, yi, ci = _place()
    chip = 2 * xi + yi
    me = 4 * xi + 2 * yi + ci
    xs = [x[0]]

    c_all = _all_gather8(c, "gather_c")[:, 0, :]
    mod_part = _mod_partial(c_all, mod_w, "mod_fwd")
    mod_all = _all_gather8(mod_part, "gather_mod")[::2]
    mod_mine = lax.dynamic_index_in_dim(mod_all, me, axis=2, keepdims=False)
    mod = mod_mine.transpose(1, 0, 2).reshape(DEPTH, 3 * D) + mod_b
    shift, scale, gate = mod[:, :D], mod[:, D:2 * D], mod[:, 2 * D:]

    shards = {"sg_w_in0": sg_w_in[0], "sg_w_out0": sg_w_out[0], "swa_w_in": swa_w_in[0], "swa_w_out": swa_w_out[0],
              "rwkv_w_in": rwkv_w_in[0], "rwkv_w_out": rwkv_w_out[0], "sg_w_in1": sg_w_in[1], "sg_w_out1": sg_w_out[1]}
    bufs = {n: _weight_buffer(s) for n, s in shards.items()}
    fwd_riders = {(0, "in"): ["swa_w_in"], (0, "mix"): ["swa_w_out"], (1, "in"): ["rwkv_w_out"], (1, "mix"): ["rwkv_w_in"],
                  (2, "mix"): ["sg_w_in1", "sg_w_out1"]}
    bufs["sg_w_in0"], bufs["sg_w_out0"], small_buf = _chip_gather(
        [bufs["sg_w_in0"], bufs["sg_w_out0"], _small_buffer(w)], "gather_l0")
    full = _unpack_small(small_buf, w)

    def riding(i, where):
        names = fwd_riders.get((i, where))
        return names, (None if names is None else _gather_rider([bufs[n] for n in names]))

    def arrived(names, ridden):
        for n, b in zip(names or [], ridden):
            bufs[n] = b

    sg_in = lambda j: _chip_shards(bufs[f"sg_w_in{j}"])
    sg_out = lambda j: bufs[f"sg_w_out{j}"].reshape(D, D)
    mu = full["rwkv_mu"][0]
    rw_prm = dict(mu_main=_dim_major(mu[:RW_MAIN].reshape(4, D)).reshape(1, RW_MAIN), mu_lo=_lo_cols(mu[None, RW_MAIN:]),
                  w0=_dim_major(full["rwkv_w0"]), a0=_dim_major(full["rwkv_a0"]),
                  wl=_lora_pad_rows(_dim_major(full["rwkv_w_lora"][0])), al=_lora_pad_rows(_dim_major(full["rwkv_a_lora"][0])),
                  kkp=_param_compact(full["rwkv_k_k"][0]), kap=_param_compact(full["rwkv_k_a"][0]),
                  rkp=_param_compact(rwkv_r_k.reshape(-1)),
                  gn_g=_param_compact(full["rwkv_gn_g"][0]), gn_b=_param_compact(full["rwkv_gn_b"][0]))
    bs_t = [jnp.pad(sg_b_spatial[j].T, ((0, 0), (0, LANES - SG_GROUPS))) for j in range(2)]
    sink_row = jnp.pad(swa_sinks, ((0, 0), (0, LANES - N_HEADS)))
    inv_freq = ROPE_THETA ** (-jnp.arange(HEAD // 2, dtype=F32) / (HEAD // 2))
    ang = positions[0].astype(F32)[:, None] * inv_freq
    cos, sin = jnp.tile(jnp.cos(ang), (1, LANES * 2 // HEAD)), jnp.tile(jnp.sin(ang), (1, LANES * 2 // HEAD))

    def row(a, i):
        return a[i:i + 1]

    hs, ps, us, ys, rw_saved = [], [], [], [], None
    for i, kind in enumerate(KINDS):
        j = i // 3
        tag = f"l{i}_{kind}"
        h = _norm_mod_fwd(xs[i], row(norm_g, i), row(shift, i), row(scale, i), tag + "_norm")
        names_in, rider_in = riding(i, "in")
        names_mix, rider_mix = riding(i, "mix")
        if kind == "sg":
            p, ridden = _ridden(_matmul(h, sg_in(j), "nn", tag + "_in", blocked=True, rider=rider_in), rider_in)
            arrived(names_in, ridden)
            u, ridden = _ridden(_sg_fwd(p, row(full["sg_ln_g"], j), row(full["sg_ln_b"], j), sg_w_spatial[j], bs_t[j],
                                        tag + "_mix", rider_mix), rider_mix)
            w_out = sg_out(j)
        elif kind == "swa":
            swa_in, swa_out = _chip_shards(bufs["swa_w_in"]), bufs["swa_w_out"].reshape(D, D)
            p, ridden = _ridden(_matmul(h, swa_in, "nn", tag + "_in", blocked=True, rider=rider_in), rider_in)
            arrived(names_in, ridden)
            u, ridden = _ridden(_swa_fwd(p, cos, sin, sink_row, tag + "_mix", rider_mix), rider_mix)
            w_out = swa_out
        else:
            rw_in = _join_shards(_chip_shards(bufs["rwkv_w_in"]), axis=1)
            rw_main = _dim_major(rw_in[:, :RW_MAIN].reshape(D, 4, D)).reshape(D, RW_MAIN)
            rw_lo = _lo_cols(rw_in[:, RW_MAIN:])
            rw_out = _rows_dim_major(bufs["rwkv_w_out"].reshape(D, D))
            p = (_matmul(h, rw_main, "nn", tag + "_in"), _matmul(h, rw_lo, "nn", tag + "_in_lo"))
            u, rw_saved, ridden = _rwkv_mixer_fwd(p[0], p[1], rw_prm, tag, rider_mix)
            w_out = rw_out
        arrived(names_mix, ridden)
        y, x_next = _out_proj_resid(u, w_out, xs[i], row(gate, i), tag + "_out")
        xs.append(x_next)
        hs.append(h), ps.append(p), us.append(u), ys.append(y)

    loss_part, dx, d_final_g, dy, d_gate = _final_loss_grad(xs[DEPTH], final_norm_g[None], loss_target[0], ys[DEPTH - 1],
                                                            row(gate, DEPTH - 1), "loss")
    loss = lax.psum(loss_part[0, 0], ("x", "y", "c"))

    gfull = {n: [None, None] for n in ("sg_ln_g", "sg_ln_b", "sg_w_spatial", "sg_b_spatial")}
    gbig = {}
    d_norm_g, d_mod = [None] * DEPTH, [None] * DEPTH
    rs_p, rs_q, riding_names = {}, {}, []

    for i in reversed(range(DEPTH)):
        kind, j = KINDS[i], i // 3
        tag = f"l{i}_{kind}_b"
        rider = _chip_exchange_rider([rs_p[n] for n in riding_names]) if riding_names else None
        w_out = {"sg": sg_out(j), "swa": swa_out, "rwkv": rw_out}[kind]
        du = _matmul(dy, w_out, "nt", tag + "_du")
        dw_out = _matmul(us[i], dy, "tn", tag + "_dwout").reshape(N_CHIPS, D // N_CHIPS, D)
        if kind == "sg":
            (dp, dlg, dlb, dws, dbs), ridden = _ridden(
                _sg_bwd(ps[i], row(full["sg_ln_g"], j), row(full["sg_ln_b"], j), sg_w_spatial[j], bs_t[j], du,
                        tag + "_mix", rider), rider)
            gfull["sg_ln_g"][j], gfull["sg_ln_b"][j] = dlg[0], dlb[0]
            gfull["sg_w_spatial"][j], gfull["sg_b_spatial"][j] = dws, dbs[:, :SG_GROUPS].T
            gbig[f"sg_w_in{j}"] = _matmul(hs[i], dp, "tn", tag + "_dwin", blocked=True)
            gbig[f"sg_w_out{j}"] = dw_out
            dh, dh2 = _matmul(dp, sg_in(j), "nt", tag + "_dh", blocked=True), None
            mine = [f"sg_w_in{j}", f"sg_w_out{j}"]
        elif kind == "swa":
            (dp, dsk), ridden = _ridden(_swa_bwd(ps[i], cos, sin, sink_row, du, tag + "_mix", rider), rider)
            gfull["swa_sinks"] = dsk[:, :N_HEADS]
            gbig["swa_w_in"] = _matmul(hs[i], dp, "tn", tag + "_dwin", blocked=True)
            gbig["swa_w_out"] = dw_out
            dh, dh2 = _matmul(dp, swa_in, "nt", tag + "_dh", blocked=True), None
            mine = ["swa_w_in", "swa_w_out"]
        else:
            dpm, dpl, rg, ridden = _rwkv_mixer_bwd(ps[i][0], ps[i][1], rw_prm, rw_saved, du, tag, rider)
            mine = ["rwkv_w_in", "rwkv_w_out"]
            dw_main = _matmul(hs[i], dpm, "tn", tag + "_dwin")
            dw_lo = _matmul(hs[i], dpl, "tn", tag + "_dwin_lo")
            dw_main = _head_major(dw_main.reshape(D, 4, D)).reshape(D, RW_MAIN)
            dw_in = jnp.concatenate([dw_main, _lo_cols_inv(dw_lo)], axis=1)
            gbig["rwkv_w_in"] = dw_in.reshape(D, N_CHIPS, -1).transpose(1, 0, 2)
            gbig["rwkv_w_out"] = _rows_head_major(dw_out.reshape(D, D)).reshape(dw_out.shape)
            dmu_main = _head_major(rg["mu_main"].reshape(4, D)).reshape(1, RW_MAIN)
            gfull["rwkv_mu"] = jnp.concatenate([dmu_main, _lo_cols_inv(rg["mu_lo"])], axis=1)
            gfull["rwkv_w0"], gfull["rwkv_a0"] = _head_major(rg["w0"]), _head_major(rg["a0"])
            gfull["rwkv_w_lora"], gfull["rwkv_a_lora"] = _head_major(rg["wl"])[None, :LORA], _head_major(rg["al"])[None, :LORA]
            gfull["rwkv_k_k"], gfull["rwkv_k_a"] = _param_compact_inv(rg["kkp"])[None], _param_compact_inv(rg["kap"])[None]
            gfull["rwkv_r_k"] = _param_compact_inv(rg["rkp"]).reshape(1, N_HEADS, HEAD)
            gfull["rwkv_gn_g"], gfull["rwkv_gn_b"] = _param_compact_inv(rg["gn_g"])[None], _param_compact_inv(rg["gn_b"])[None]
            dh, dh2 = _matmul(dpm, rw_main, "nt", tag + "_dh"), _matmul(dpl, rw_lo, "nt", tag + "_dh_lo")
        rs_p.update(zip(riding_names, ridden[:len(riding_names)]))
        rs_q.update(zip(riding_names, ridden[len(riding_names):]))
        if i == 0:
            for n in ("sg_ln_g", "sg_ln_b"):
                gfull[n] = jnp.stack(gfull[n])
            small = jnp.concatenate([_chip_blocks(gfull[n], axis) for n, axis in SMALL.items()], axis=1)
            small_rows = _round_up(small.shape[1], 2 * 16 * LANES) // LANES
            small = jnp.pad(small, ((0, 0), (0, small_rows * LANES - small.shape[1])))
            gbig["small"] = small.reshape(N_CHIPS, small_rows, LANES)
            mine = mine + ["small"]
        gs = [gbig[n].reshape(N_CHIPS, 2, gbig[n].shape[1] // 2, gbig[n].shape[2]) for n in mine]
        pair_rider = _pair_exchange_rider(gs)
        below = (ys[i - 1], row(gate, i - 1)) if i > 0 else None
        (dx, dg, dsh, dsc, *below_grads), gots = _norm_mod_bwd(xs[i], row(norm_g, i), row(shift, i), row(scale, i), dh, dx,
                                                               tag + "_norm", dh2, pair_rider, below)
        d_norm_g[i] = dg[0]
        d_mod[i] = jnp.concatenate([dsh[0], dsc[0], d_gate[0]])
        if below is not None:
            dy, d_gate = below_grads
        rs_p.update(zip(mine, _rs_pair_sums(gots[:len(gs)], gots[len(gs):], ci, f"rs{i}")))
        riding_names = mine
    for n in ("sg_w_spatial", "sg_b_spatial"):
        gfull[n] = jnp.stack(gfull[n])
    gfull["norm_g"], gfull["final_norm_g"] = jnp.stack(d_norm_g), d_final_g[0]

    grads, deltas, new_m, new_v, red = {}, {}, {}, {}, {}

    def finish(names, tag):
        outs = _rs_finish([rs_p[n] for n in names], [rs_q[n] for n in names], chip, ci, tag)
        red.update({n: r.reshape(-1, r.shape[2]) for n, r in zip(names, outs)})

    def adamw(n, rider=None):
        res = _adamw(w[n], grads[n], given["m_" + n], given["v_" + n], "adamw_" + n, rider)
        (deltas[n], new_m[n], new_v[n]), ridden = _ridden(res, rider)
        return ridden

    def ride_exchange(names, on):
        ridden = adamw(on, _chip_exchange_rider([rs_p[n] for n in names]))
        rs_p.update(zip(names, ridden[:len(names)]))
        rs_q.update(zip(names, ridden[len(names):]))

    finish(sorted(set(rs_p) - set(riding_names)), "rs_a")
    for n in ("swa_w_in", "swa_w_out", "rwkv_w_in", "rwkv_w_out"):
        grads[n] = red[n][None]
    dmod_all = _all_gather8(jnp.stack(d_mod).reshape(DEPTH * 3 * D // RS_W, RS_W), "gather_dmod")
    grads["mod_b"] = _sum_leading(dmod_all, "sum_dmod").reshape(DEPTH, 3 * D)
    dmod_all = dmod_all.reshape(N_DEV, DEPTH, 3 * D)
    dmod_cols = lax.dynamic_slice_in_dim(dmod_all, chip * MOD_COLS, MOD_COLS, axis=2).transpose(1, 0, 2)
    grads["mod_w"] = _mod_w_grad(c_all, dmod_cols, "mod_w_grad")
    ride_exchange(["sg_w_in0"], on="mod_w")
    ride_exchange(["sg_w_out0", "small"], on="rwkv_w_in")
    finish(riding_names, "rs_b")
    grads["sg_w_in"] = jnp.stack([red["sg_w_in0"], red["sg_w_in1"]])
    grads["sg_w_out"] = jnp.stack([red["sg_w_out0"], red["sg_w_out1"]])
    small_red, off = red["small"].reshape(-1), 0
    for n in SMALL:
        grads[n] = small_red[off:off + w[n].size].reshape(w[n].shape)
        off += w[n].size

    rep_flat = jnp.concatenate([gfull[n].reshape(-1) for n in REPLICATED])
    rep_rows = _round_up(rep_flat.shape[0], 32 * RS_W) // RS_W
    rep_sum = _all_reduce_replicated(_pad_to(rep_flat, rep_rows * RS_W).reshape(rep_rows, RS_W), "rep").reshape(-1)
    off = 0
    for n in REPLICATED:
        grads[n] = rep_sum[off:off + w[n].size].reshape(w[n].shape)
        off += w[n].size

    for n in W_NAMES:
        if n not in deltas:
            adamw(n)
    return (loss, dx[None], *[grads[n] for n in W_NAMES], *[deltas[n] for n in W_NAMES],
            *[new_m[n] for n in W_NAMES], *[new_v[n] for n in W_NAMES])
```

```python
import functools
import math

import jax
import jax.numpy as jnp
from jax import lax
from jax.experimental import pallas as pl
from jax.experimental.pallas import tpu as pltpu

F32 = jnp.float32
BF16 = jnp.bfloat16
HIGHEST = lax.Precision.HIGHEST

D = 2048
DEPTH = 4
CHUNK = 128
SG_GROUPS = 16
HEAD = 64
N_HEADS = D // HEAD
KV_HEADS = 4
KVW = KV_HEADS * HEAD
ROPE_THETA = 10000.0
LORA = 96
LORA_PAD = 128
DECAY_SCALE = math.exp(-0.5)
GN_EPS = 64e-5
RMS_EPS = 1e-6
LN_EPS = 1e-5
ADAM_LR, ADAM_B1, ADAM_B2, ADAM_EPS, ADAM_WD, ADAM_STEP = 0.001, 0.9, 0.999, 1e-08, 0.01, 10
LANES = 128
NEG = -1e30
VMEM_LIMIT = 56 * 1024 * 1024

MESHT = pl.DeviceIdType.MESH


def _cparams(*sem):
    return pltpu.CompilerParams(dimension_semantics=sem, vmem_limit_bytes=VMEM_LIMIT)


class _Rider:
    def __init__(self, arrays, n_sems, start, finish):
        self.arrays, self.n_sems, self.start, self.finish = list(arrays), n_sems, start, finish


def _ridden(res, rider):
    return (res, []) if rider is None else res


def _compute_call(body, args, *, name, grid, in_specs, out_specs, out_shape, semantics, scratch_shapes=(), rider=None):
    if rider is None:
        return pl.pallas_call(body, name=name, grid=grid, in_specs=in_specs, out_specs=out_specs, out_shape=out_shape,
                              scratch_shapes=list(scratch_shapes), compiler_params=_cparams(*semantics))(*args)
    single = not isinstance(out_shape, (list, tuple))
    o_specs, o_shapes = ([out_specs], [out_shape]) if single else (list(out_specs), list(out_shape))
    n_in, n_out, n_r = len(in_specs), len(o_specs), len(rider.arrays)

    def with_rider(*refs):
        ins, outs = refs[:n_in], refs[n_in + n_r:n_in + n_r + n_out]
        ridden = refs[n_in + n_r + n_out:n_in + 2 * n_r + n_out]
        scratch, (send_sems, recv_sems) = refs[n_in + 2 * n_r + n_out:-2], refs[-2:]
        ids = [pl.program_id(d) for d in range(len(grid))]
        first = functools.reduce(jnp.logical_and, [i == 0 for i in ids])
        last = functools.reduce(jnp.logical_and, [i == g - 1 for i, g in zip(ids, grid)])

        @pl.when(first)
        def _():
            rider.start(ridden, send_sems, recv_sems)

        body(*ins, *outs, *scratch)

        @pl.when(last)
        def _():
            rider.finish(ridden, send_sems, recv_sems)

    any_spec = pl.BlockSpec(memory_space=pl.ANY)
    res = pl.pallas_call(
        with_rider, name=name, grid=grid, in_specs=list(in_specs) + [any_spec] * n_r, out_specs=o_specs + [any_spec] * n_r,
        out_shape=o_shapes + [jax.ShapeDtypeStruct(a.shape, a.dtype) for a in rider.arrays],
        input_output_aliases={n_in + i: n_out + i for i in range(n_r)},
        scratch_shapes=list(scratch_shapes) + [pltpu.SemaphoreType.DMA((rider.n_sems,))] * 2,
        compiler_params=_cparams(*["arbitrary"] * len(grid)),
    )(*args, *rider.arrays)
    return (res[0] if single else list(res[:n_out])), list(res[n_out:])


_NN = (((1,), (0,)), ((), ()))
_NT = (((1,), (1,)), ((), ()))
_TN = (((0,), (0,)), ((), ()))


def _dg(a, b, dims):
    return lax.dot_general(a, b, dims, preferred_element_type=F32)


@jax.custom_vjp
def _bdot_nn(a, b):
    return _dg(a.astype(BF16), b.astype(BF16), _NN)


def _bdot_nn_fwd(a, b):
    a, b = a.astype(BF16), b.astype(BF16)
    return _dg(a, b, _NN), (a, b)


def _bdot_nn_bwd(res, ct):
    a, b = res
    ct = ct.astype(BF16)
    return _dg(ct, b, _NT), _dg(a, ct, _TN)


_bdot_nn.defvjp(_bdot_nn_fwd, _bdot_nn_bwd)


@jax.custom_vjp
def _bdot_nt(a, b):
    return _dg(a.astype(BF16), b.astype(BF16), _NT)


def _bdot_nt_fwd(a, b):
    a, b = a.astype(BF16), b.astype(BF16)
    return _dg(a, b, _NT), (a, b)


def _bdot_nt_bwd(res, ct):
    a, b = res
    ct = ct.astype(BF16)
    return _dg(ct, b, _NN), _dg(ct, a, _TN)


_bdot_nt.defvjp(_bdot_nt_fwd, _bdot_nt_bwd)


def _tile(n, cap):
    if n <= cap:
        return n
    return max(d for d in range(LANES, cap + 1, LANES) if n % d == 0)


def _matmul(a, b, form, name, out_dtype=F32, blocked=False, rider=None, tm=1024, tn=512, tk=4096):
    if form == "nn":
        (m, k), n = a.shape, (N_CHIPS * b.shape[2] if blocked else b.shape[1])
    elif form == "nt":
        m, k, n = a.shape[0], a.shape[1], (b.shape[1] if blocked else b.shape[0])
    else:
        (k, m), n = a.shape, b.shape[1]
    per_chip = (k if form == "nt" else n) // N_CHIPS
    if blocked and form == "nt":
        tk = _tile(per_chip, tk)
    elif blocked:
        tn = _tile(per_chip, tn)
    tm, tn, tk = _tile(m, tm), _tile(n, tn), _tile(k, tk)
    assert m % tm == 0 and n % tn == 0 and k % tk == 0, (name, a.shape, b.shape)
    nk = k // tk
    dims = {"nn": _NN, "nt": _NT, "tn": _TN}[form]
    a_spec = pl.BlockSpec((tk, tm), lambda i, j, l: (l, i)) if form == "tn" else pl.BlockSpec((tm, tk), lambda i, j, l: (i, l))
    b_spec = pl.BlockSpec((tn, tk), lambda i, j, l: (j, l)) if form == "nt" else pl.BlockSpec((tk, tn), lambda i, j, l: (l, j))
    o_spec = pl.BlockSpec((tm, tn), lambda i, j, l: (i, j))
    o_shape = (m, n)
    if blocked and form == "nn":
        pc = per_chip // tn
        b_spec = pl.BlockSpec((None, tk, tn), lambda i, j, l: (j // pc, l, j % pc))
    elif blocked and form == "nt":
        pc = per_chip // tk
        b_spec = pl.BlockSpec((None, tn, tk), lambda i, j, l: (l // pc, j, l % pc))
    elif blocked:
        pc = per_chip // tn
        o_spec = pl.BlockSpec((None, tm, tn), lambda i, j, l: (j // pc, i, j % pc))
        o_shape = (N_CHIPS, m, per_chip)

    def body(a_ref, b_ref, o_ref, acc_ref):
        part = _dg(a_ref[...], b_ref[...], dims)
        if nk == 1:
            o_ref[...] = part.astype(out_dtype)
        else:
            l = pl.program_id(2)

            @pl.when(l == 0)
            def _():
                acc_ref[...] = part

            @pl.when(l > 0)
            def _():
                acc_ref[...] += part

            @pl.when(l == nk - 1)
            def _():
                o_ref[...] = acc_ref[...].astype(out_dtype)

    return _compute_call(
        body, (a, b), name=name, grid=(m // tm, n // tn, nk),
        in_specs=[a_spec, b_spec], out_specs=o_spec, out_shape=jax.ShapeDtypeStruct(o_shape, out_dtype),
        scratch_shapes=[pltpu.VMEM((tm, tn) if nk > 1 else (8, LANES), F32)],
        semantics=("parallel", "parallel", "arbitrary"), rider=rider)


def _out_proj_resid(u, w_out, x, gate, name, tm=1024, tn=512):
    (m, k), n = u.shape, w_out.shape[1]

    def body(u_ref, w_ref, x_ref, g_ref, y_ref, xn_ref):
        y = _dg(u_ref[...], w_ref[...], _NN)
        y_ref[...] = y
        xn_ref[...] = x_ref[...] + g_ref[...] * y

    tile = pl.BlockSpec((tm, tn), lambda i, j: (i, j))
    out = jax.ShapeDtypeStruct((m, n), F32)
    return pl.pallas_call(
        body, name=name, grid=(m // tm, n // tn),
        in_specs=[pl.BlockSpec((tm, k), lambda i, j: (i, 0)), pl.BlockSpec((k, tn), lambda i, j: (0, j)), tile,
                  pl.BlockSpec((1, tn), lambda i, j: (0, j))],
        out_specs=[tile, tile], out_shape=[out, out], compiler_params=_cparams("parallel", "parallel"),
    )(u, w_out, x, gate)


TB_NORM = 256


def _f_norm_mod(x, g, shift, scale):
    xn = x * lax.rsqrt(jnp.mean(x * x, axis=-1, keepdims=True) + RMS_EPS)
    return (xn * g) * (1.0 + scale) + shift


def _row_spec(width, tb=TB_NORM):
    return pl.BlockSpec((tb, width), lambda i: (i, 0))


def _vec_spec(width, rows=1):
    return pl.BlockSpec((rows, width), lambda i: (0, 0))


def _norm_mod_fwd(x, g, shift, scale, name):
    t = x.shape[0]

    def body(x_ref, g_ref, sh_ref, sc_ref, h_ref):
        h_ref[...] = _f_norm_mod(x_ref[...], g_ref[...], sh_ref[...], sc_ref[...]).astype(BF16)

    return pl.pallas_call(
        body, name=name, grid=(t // TB_NORM,),
        in_specs=[_row_spec(D), _vec_spec(D), _vec_spec(D), _vec_spec(D)], out_specs=_row_spec(D),
        out_shape=jax.ShapeDtypeStruct((t, D), BF16), compiler_params=_cparams("parallel"),
    )(x, g, shift, scale)


def _accum(ref, val, first):
    @pl.when(first)
    def _():
        ref[...] = val

    @pl.when(jnp.logical_not(first))
    def _():
        ref[...] += val


def _gate_bwd_block(dx, y_ref, gate_ref, dy_ref, dgate_ref, first):
    dy_ref[...] = (dx * gate_ref[...]).astype(BF16)
    _accum(dgate_ref, jnp.sum(dx * y_ref[...], axis=0, keepdims=True), first)


def _gate_bwd_specs():
    return [_row_spec(D), _vec_spec(D)]


def _gate_bwd_shapes(t):
    return [jax.ShapeDtypeStruct((t, D), BF16), jax.ShapeDtypeStruct((1, D), F32)]


def _norm_mod_bwd(x, g, shift, scale, dh, dx_res, name, dh2=None, rider=None, below=None):
    t = x.shape[0]
    dhs = [dh] if dh2 is None else [dh, dh2]
    extra = [] if below is None else list(below)

    def body(x_ref, g_ref, sh_ref, sc_ref, dr_ref, *refs):
        dh_refs, refs = refs[:len(dhs)], refs[len(dhs):]
        below_refs, (dx_ref, dg_ref, dsh_ref, dsc_ref), below_out = refs[:len(extra)], refs[len(extra):len(extra) + 4], refs[len(extra) + 4:]
        _, vjp = jax.vjp(_f_norm_mod, x_ref[...], g_ref[...], sh_ref[...], sc_ref[...])
        dh_all = dh_refs[0][...]
        for r in dh_refs[1:]:
            dh_all = dh_all + r[...]
        dx, dg, dsh, dsc = vjp(dh_all)
        dx = dx + dr_ref[...]
        dx_ref[...] = dx
        first = pl.program_id(0) == 0
        _accum(dg_ref, dg, first)
        _accum(dsh_ref, dsh, first)
        _accum(dsc_ref, dsc, first)
        if below is not None:
            _gate_bwd_block(dx, *below_refs, *below_out, first)

    vec = jax.ShapeDtypeStruct((1, D), F32)
    return _compute_call(
        body, (x, g, shift, scale, dx_res, *dhs, *extra), name=name, grid=(t // TB_NORM,),
        in_specs=[_row_spec(D), _vec_spec(D), _vec_spec(D), _vec_spec(D), _row_spec(D)] + [_row_spec(D)] * len(dhs)
        + (_gate_bwd_specs() if extra else []),
        out_specs=[_row_spec(D), _vec_spec(D), _vec_spec(D), _vec_spec(D)] + (_gate_bwd_specs() if extra else []),
        out_shape=[jax.ShapeDtypeStruct((t, D), F32), vec, vec, vec] + (_gate_bwd_shapes(t) if extra else []),
        semantics=("arbitrary",), rider=rider)


def _f_final(x, g, target):
    xn = x * lax.rsqrt(jnp.mean(x * x, axis=-1, keepdims=True) + RMS_EPS)
    err = xn * g - target
    return 0.5 * jnp.sum(jnp.mean(err * err, axis=-1, keepdims=True), axis=0, keepdims=True)


def _final_loss_grad(x, g, target, y, gate, name):
    t = x.shape[0]

    def body(x_ref, g_ref, t_ref, y_ref, gate_ref, loss_ref, dx_ref, dg_ref, dy_ref, dgate_ref):
        loss, vjp = jax.vjp(_f_final, x_ref[...], g_ref[...], t_ref[...])
        dx, dg, _ = vjp(jnp.ones((1, 1), F32))
        dx_ref[...] = dx
        first = pl.program_id(0) == 0
        _accum(dg_ref, dg, first)
        _accum(loss_ref, jnp.broadcast_to(loss, (1, LANES)), first)
        _gate_bwd_block(dx, y_ref, gate_ref, dy_ref, dgate_ref, first)

    return pl.pallas_call(
        body, name=name, grid=(t // TB_NORM,),
        in_specs=[_row_spec(D), _vec_spec(D), _row_spec(D)] + _gate_bwd_specs(),
        out_specs=[_vec_spec(LANES), _row_spec(D), _vec_spec(D)] + _gate_bwd_specs(),
        out_shape=[jax.ShapeDtypeStruct((1, LANES), F32), jax.ShapeDtypeStruct((t, D), F32), jax.ShapeDtypeStruct((1, D), F32)]
        + _gate_bwd_shapes(t),
        compiler_params=_cparams("arbitrary"),
    )(x, g, target, y, gate)


def _group_selector():
    gi = lax.broadcasted_iota(jnp.int32, (LANES, D), 0)
    ci = lax.broadcasted_iota(jnp.int32, (LANES, D), 1)
    return (ci // (D // SG_GROUPS) == gi).astype(F32)


def _f_sg(p, ln_g, ln_b, w_s, bs_t):
    u, v, z = p[:, :D], p[:, D:2 * D], p[:, 2 * D:]
    u = jax.nn.gelu(u)
    vf = jax.nn.gelu(v)
    mean = jnp.mean(vf, axis=-1, keepdims=True)
    var = jnp.mean(jnp.square(vf - mean), axis=-1, keepdims=True)
    vn = (vf - mean) * lax.rsqrt(var + LN_EPS) * ln_g + ln_b
    ti = lax.broadcasted_iota(jnp.int32, (CHUNK, CHUNK), 0)
    si = lax.broadcasted_iota(jnp.int32, (CHUNK, CHUNK), 1)
    causal = si <= ti
    cg = D // SG_GROUPS
    f = jnp.concatenate(
        [_bdot_nn(jnp.where(causal, w_s[g], 0.0), vn[:, g * cg:(g + 1) * cg]) for g in range(SG_GROUPS)], axis=1)
    f = f + jnp.dot(bs_t, _group_selector(), precision=HIGHEST, preferred_element_type=F32)
    return u * f * jax.nn.silu(z)


def _sg_specs():
    return [pl.BlockSpec((CHUNK, 3 * D), lambda i: (i, 0)), _vec_spec(D), _vec_spec(D),
            pl.BlockSpec((SG_GROUPS, CHUNK, CHUNK), lambda i: (0, 0, 0)), _vec_spec(LANES, CHUNK)]


def _sg_fwd(p, ln_g, ln_b, w_s, bs_t, name, rider=None):
    t = p.shape[0]

    def body(p_ref, lg_ref, lb_ref, w_ref, b_ref, o_ref):
        o_ref[...] = _f_sg(p_ref[...], lg_ref[...], lb_ref[...], w_ref[...], b_ref[...]).astype(BF16)

    return _compute_call(
        body, (p, ln_g, ln_b, w_s, bs_t), name=name, grid=(t // CHUNK,), in_specs=_sg_specs(),
        out_specs=_row_spec(D, CHUNK), out_shape=jax.ShapeDtypeStruct((t, D), BF16), semantics=("parallel",), rider=rider)


def _sg_bwd(p, ln_g, ln_b, w_s, bs_t, dout, name, rider=None):
    t = p.shape[0]

    def body(p_ref, lg_ref, lb_ref, w_ref, b_ref, do_ref, dp_ref, dlg_ref, dlb_ref, dw_ref, db_ref):
        _, vjp = jax.vjp(_f_sg, p_ref[...], lg_ref[...], lb_ref[...], w_ref[...], b_ref[...])
        dp, dlg, dlb, dw, db = vjp(do_ref[...])
        dp_ref[...] = dp.astype(BF16)
        first = pl.program_id(0) == 0
        _accum(dlg_ref, dlg, first)
        _accum(dlb_ref, dlb, first)
        _accum(dw_ref, dw, first)
        _accum(db_ref, db, first)

    vec = jax.ShapeDtypeStruct((1, D), F32)
    return _compute_call(
        body, (p, ln_g, ln_b, w_s, bs_t, dout), name=name, grid=(t // CHUNK,), in_specs=_sg_specs() + [_row_spec(D, CHUNK)],
        out_specs=[pl.BlockSpec((CHUNK, 3 * D), lambda i: (i, 0)), _vec_spec(D), _vec_spec(D),
                   pl.BlockSpec((SG_GROUPS, CHUNK, CHUNK), lambda i: (0, 0, 0)), _vec_spec(LANES, CHUNK)],
        out_shape=[jax.ShapeDtypeStruct((t, 3 * D), BF16), vec, vec,
                   jax.ShapeDtypeStruct((SG_GROUPS, CHUNK, CHUNK), F32), jax.ShapeDtypeStruct((CHUNK, LANES), F32)],
        semantics=("arbitrary",), rider=rider)


SWA_COLS = 2 * D + 2 * KVW
KV_BLOCK = 2 * KVW


def _lane_roll(x, shift):
    return pltpu.roll(x, shift, 1)


def _rot_half(x):
    w = x.shape[1]
    lane = lax.broadcasted_iota(jnp.int32, x.shape, 1)
    return jnp.where(lane % HEAD < HEAD // 2, -_lane_roll(x, w - HEAD // 2), _lane_roll(x, HEAD // 2))


@jax.custom_vjp
def _rope(x, cos, sin):
    return x * cos + _rot_half(x) * sin


def _rope_fwd(x, cos, sin):
    return _rope(x, cos, sin), (cos, sin)


def _rope_bwd(res, ct):
    cos, sin = res
    return ct * cos - _rot_half(ct) * sin, jnp.zeros_like(cos), jnp.zeros_like(sin)


_rope.defvjp(_rope_fwd, _rope_bwd)


@jax.custom_vjp
def _swap_halves(x):
    return _lane_roll(x, HEAD)


_swap_halves.defvjp(lambda x: (_lane_roll(x, HEAD), None), lambda _, ct: (_lane_roll(ct, HEAD),))


def _f_swa(pq, pkv, cos, sin, cosp, sinp, sink_row, valid):
    reps = D // LANES
    q = _rope(pq[:, :D], jnp.tile(cos, (1, reps)), jnp.tile(sin, (1, reps))) * (HEAD ** -0.5)
    k = _rope(pq[:, D:D + KVW], jnp.tile(cos, (1, KVW // LANES)), jnp.tile(sin, (1, KVW // LANES)))
    kp = _rope(pkv[:, :KVW], jnp.tile(cosp, (1, KVW // LANES)), jnp.tile(sinp, (1, KVW // LANES)))
    v, vp, z = pq[:, D + KVW:D + 2 * KVW], pkv[:, KVW:], pq[:, D + 2 * KVW:]
    kcat = jnp.concatenate([kp, k], axis=0)
    vcat = jnp.concatenate([vp, v], axis=0)
    lane = lax.broadcasted_iota(jnp.int32, (2 * CHUNK, LANES), 1)
    lo = lane < HEAD
    hlane = lax.broadcasted_iota(jnp.int32, (1, LANES), 1)

    def halves(cat, g):
        blk = cat[:, (g // 2) * LANES:(g // 2 + 1) * LANES]
        other = _swap_halves(blk)
        if g % 2 == 0:
            return jnp.where(lo, blk, 0.0), jnp.where(lo, 0.0, other)
        return jnp.where(lo, other, 0.0), jnp.where(lo, 0.0, blk)

    pairs = N_HEADS // KV_HEADS // 2
    valid_g = jnp.tile(valid, (pairs, 1))

    def probs(s, heads):
        sink = jnp.concatenate(
            [jnp.broadcast_to(jnp.sum(jnp.where(hlane == h, sink_row, 0.0), axis=1, keepdims=True), (CHUNK, 1))
             for h in heads], axis=0)
        s = jnp.where(valid_g, s, NEG)
        m = lax.stop_gradient(jnp.maximum(jnp.max(s, axis=1, keepdims=True), sink))
        e = jnp.exp(s - m)
        return e / (jnp.sum(e, axis=1, keepdims=True) + jnp.exp(sink - m))

    outs = []
    for g in range(KV_HEADS):
        k_lo, k_hi = halves(kcat, g)
        v_lo, v_hi = halves(vcat, g)
        tiles = range(g * pairs, (g + 1) * pairs)
        qg = jnp.concatenate([q[:, j * LANES:(j + 1) * LANES] for j in tiles], axis=0)
        p_a = probs(_bdot_nt(qg, k_lo), [2 * j for j in tiles])
        p_b = probs(_bdot_nt(qg, k_hi), [2 * j + 1 for j in tiles])
        og = _bdot_nn(p_a, v_lo) + _bdot_nn(p_b, v_hi)
        outs += [og[n * CHUNK:(n + 1) * CHUNK] for n in range(pairs)]
    return jnp.concatenate(outs, axis=1) * jax.nn.silu(z)


def _swa_valid(block):
    qi = lax.broadcasted_iota(jnp.int32, (CHUNK, 2 * CHUNK), 0)
    kj = lax.broadcasted_iota(jnp.int32, (CHUNK, 2 * CHUNK), 1)
    rel = qi + CHUNK - kj
    return (rel >= 0) & (rel < CHUNK) & ((kj >= CHUNK) | (block > 0))


def _swa_specs(blk):
    prev = lambda i: jnp.maximum(blk(i) - 1, 0)
    kv_col = D // KV_BLOCK
    return [pl.BlockSpec((CHUNK, SWA_COLS), lambda i: (blk(i), 0)),
            pl.BlockSpec((CHUNK, KV_BLOCK), lambda i: (prev(i), kv_col)),
            pl.BlockSpec((CHUNK, LANES), lambda i: (blk(i), 0)), pl.BlockSpec((CHUNK, LANES), lambda i: (blk(i), 0)),
            pl.BlockSpec((CHUNK, LANES), lambda i: (prev(i), 0)), pl.BlockSpec((CHUNK, LANES), lambda i: (prev(i), 0)),
            _vec_spec(LANES)]


def _swa_fwd(p, cos, sin, sink_row, name, rider=None):
    t = p.shape[0]

    def body(pq_ref, pkv_ref, c_ref, s_ref, cp_ref, sp_ref, sk_ref, o_ref):
        valid = _swa_valid(pl.program_id(0))
        o_ref[...] = _f_swa(pq_ref[...], pkv_ref[...], c_ref[...], s_ref[...], cp_ref[...], sp_ref[...],
                            sk_ref[...], valid).astype(BF16)

    return _compute_call(
        body, (p, p, cos, sin, cos, sin, sink_row), name=name, grid=(t // CHUNK,), in_specs=_swa_specs(lambda i: i),
        out_specs=_row_spec(D, CHUNK), out_shape=jax.ShapeDtypeStruct((t, D), BF16), semantics=("parallel",), rider=rider)


def _swa_bwd(p, cos, sin, sink_row, dout, name, rider=None):
    t = p.shape[0]
    nb = t // CHUNK
    blk = lambda i: nb - 1 - i

    def body(pq_ref, pkv_ref, c_ref, s_ref, cp_ref, sp_ref, sk_ref, do_ref, dp_ref, dsk_ref, pend_ref):
        i = pl.program_id(0)
        valid = _swa_valid(blk(i))
        f = functools.partial(_f_swa, valid=valid)
        _, vjp = jax.vjp(f, pq_ref[...], pkv_ref[...], c_ref[...], s_ref[...], cp_ref[...], sp_ref[...], sk_ref[...])
        dpq, dpkv, _, _, _, _, dsk = vjp(do_ref[...])

        @pl.when(i == 0)
        def _():
            pend_ref[...] = jnp.zeros_like(pend_ref)

        dp_ref[...] = jnp.concatenate(
            [dpq[:, :D], dpq[:, D:D + KV_BLOCK] + pend_ref[...], dpq[:, D + KV_BLOCK:]], axis=1).astype(BF16)
        pend_ref[...] = dpkv
        _accum(dsk_ref, dsk, i == 0)

    return _compute_call(
        body, (p, p, cos, sin, cos, sin, sink_row, dout), name=name, grid=(nb,),
        in_specs=_swa_specs(blk) + [pl.BlockSpec((CHUNK, D), lambda i: (blk(i), 0))],
        out_specs=[pl.BlockSpec((CHUNK, SWA_COLS), lambda i: (blk(i), 0)), _vec_spec(LANES)],
        out_shape=[jax.ShapeDtypeStruct((t, SWA_COLS), BF16), jax.ShapeDtypeStruct((1, LANES), F32)],
        scratch_shapes=[pltpu.VMEM((CHUNK, KV_BLOCK), F32)], semantics=("arbitrary",), rider=rider)


RW_MAIN = 4 * D
RW_LO = 2 * LORA_PAD
VM = LANES // N_HEADS
VD = HEAD // VM
S_ROWS = VD * HEAD
TB_RW = 128
TB_K = 32
TB_SCAN = 16


def _dim_major(a):
    return a.reshape(a.shape[:-1] + (N_HEADS, HEAD)).swapaxes(-1, -2).reshape(a.shape)


def _head_major(a):
    return a.reshape(a.shape[:-1] + (HEAD, N_HEADS)).swapaxes(-1, -2).reshape(a.shape)


def _param_compact(w):
    return _dim_major(w).reshape(VD, LANES)


def _param_compact_inv(pc):
    return _head_major(pc.reshape(-1))


def _f_rwkv_lora(xs_lo, w0, a0, wl, al):
    decay = jnp.exp(-DECAY_SCALE * jax.nn.sigmoid(w0 + _bdot_nn(jnp.tanh(xs_lo[:, :LORA_PAD]), wl)))
    a = jax.nn.sigmoid(a0 + _bdot_nn(xs_lo[:, LORA_PAD:], al))
    return decay, a


def _prev_rows_spec(width, tb):
    return pl.BlockSpec((8, width), lambda i: (jnp.maximum(i * (tb // 8) - 1, 0), 0))


def _token_shift_lerp(p, prev8, mu, first):
    rows = lax.broadcasted_iota(jnp.int32, p.shape, 0)
    prev = jnp.where(first, 0.0, prev8[7:8, :])
    shifted = jnp.where(rows == 0, prev, pltpu.roll(p, 1, 0))
    return p + (shifted - p) * mu


def _store_compact(ref, val):
    for j in range(VD):
        ref[:, j, :] = val[:, j * LANES:(j + 1) * LANES]


def _load_flat(ref, rows=slice(None)):
    if len(ref.shape) == 2:
        return ref[rows, :]
    return jnp.concatenate([ref[rows, j, :] for j in range(VD)], axis=1)


def _flat_spec(a, tb):
    return _row_spec(a.shape[1], tb) if a.ndim == 2 else _k_spec(VD, tb)


def _rwkv_pre_fwd(p_main, p_lo, mu_main, mu_lo, w0, a0, wl, al, name):
    t = p_main.shape[0]
    tb = TB_RW

    def body(pm_ref, pmp_ref, pl_ref, plp_ref, mm_ref, ml_ref, w0_ref, a0_ref, wl_ref, al_ref,
             r_ref, k_ref, v_ref, dec_ref, a_ref, z_ref, xl_ref):
        first = pl.program_id(0) == 0
        xs = _token_shift_lerp(pm_ref[...], pmp_ref[...], mm_ref[...], first)
        for n, ref in enumerate((r_ref, k_ref, v_ref)):
            _store_compact(ref, xs[:, n * D:(n + 1) * D])
        z_ref[...] = xs[:, 3 * D:]
        xs_lo = _token_shift_lerp(pl_ref[...], plp_ref[...], ml_ref[...], first)
        xl_ref[...] = xs_lo
        decay, a = _f_rwkv_lora(xs_lo, w0_ref[...], a0_ref[...], wl_ref[...], al_ref[...])
        _store_compact(dec_ref, decay)
        _store_compact(a_ref, a)

    cl = jax.ShapeDtypeStruct((t, VD, LANES), F32)
    return pl.pallas_call(
        body, name=name, grid=(t // tb,),
        in_specs=[_row_spec(RW_MAIN, tb), _prev_rows_spec(RW_MAIN, tb), _row_spec(RW_LO, tb), _prev_rows_spec(RW_LO, tb),
                  _vec_spec(RW_MAIN), _vec_spec(RW_LO), _vec_spec(D), _vec_spec(D),
                  _vec_spec(D, LORA_PAD), _vec_spec(D, LORA_PAD)],
        out_specs=[_k_spec(VD, tb)] * 5 + [_row_spec(D, tb), _row_spec(RW_LO, tb)],
        out_shape=[cl] * 5 + [jax.ShapeDtypeStruct((t, D), F32), jax.ShapeDtypeStruct((t, RW_LO), F32)],
        compiler_params=_cparams("parallel"),
    )(p_main, p_main, p_lo, p_lo, mu_main, mu_lo, w0, a0, wl, al)


def _rwkv_lora_bwd(xs_lo, w0, a0, wl, al, ddecay, da, name):
    t = xs_lo.shape[0]
    tb = TB_NORM

    def body(x_ref, w0_ref, a0_ref, wl_ref, al_ref, dd_ref, da_ref, dx_ref, dw0_ref, da0_ref, dwl_ref, dal_ref):
        _, vjp = jax.vjp(_f_rwkv_lora, x_ref[...], w0_ref[...], a0_ref[...], wl_ref[...], al_ref[...])
        dx, dw0, da0, dwl, dal = vjp((_load_flat(dd_ref), _load_flat(da_ref)))
        dx_ref[...] = dx
        first = pl.program_id(0) == 0
        _accum(dw0_ref, dw0, first)
        _accum(da0_ref, da0, first)
        _accum(dwl_ref, dwl, first)
        _accum(dal_ref, dal, first)

    vec = jax.ShapeDtypeStruct((1, D), F32)
    lor = jax.ShapeDtypeStruct((LORA_PAD, D), F32)
    return pl.pallas_call(
        body, name=name, grid=(t // tb,),
        in_specs=[_row_spec(RW_LO), _vec_spec(D), _vec_spec(D), _vec_spec(D, LORA_PAD), _vec_spec(D, LORA_PAD),
                  _k_spec(VD, tb), _k_spec(VD, tb)],
        out_specs=[_row_spec(RW_LO), _vec_spec(D), _vec_spec(D), _vec_spec(D, LORA_PAD), _vec_spec(D, LORA_PAD)],
        out_shape=[jax.ShapeDtypeStruct((t, RW_LO), F32), vec, vec, lor, lor], compiler_params=_cparams("arbitrary"),
    )(xs_lo, w0, a0, wl, al, ddecay, da)


def _lerp_bwd(p, dxs_groups, mu, name):
    t, width = p.shape
    tb = TB_RW
    nb = t // tb
    parts = [a for group in dxs_groups for a in group]

    def body(p_ref, pp_ref, mu_ref, *refs):
        d_refs, (dp_ref, dmu_ref) = refs[:2 * len(parts)], refs[2 * len(parts):]
        i = pl.program_id(0)

        def columns(k):
            pick = (lambda r: _load_flat(r, slice(0, 1))) if k else _load_flat
            vals, at = [], 0
            for group in dxs_groups:
                vals.append(functools.reduce(jnp.add, [pick(d_refs[2 * (at + n) + k]) for n in range(len(group))]))
                at += len(group)
            return jnp.concatenate(vals, axis=1)

        pv, dv, mu_v = p_ref[...], columns(0), mu_ref[...]
        rows = lax.broadcasted_iota(jnp.int32, pv.shape, 0)
        prev = jnp.where(i == 0, 0.0, pp_ref[7:8, :])
        shifted = jnp.where(rows == 0, prev, pltpu.roll(pv, 1, 0))
        nxt = jnp.where(i == nb - 1, 0.0, columns(1))
        d_next = jnp.where(rows == tb - 1, nxt, pltpu.roll(dv, tb - 1, 0))
        dp_ref[...] = (dv * (1.0 - mu_v) + d_next * mu_v).astype(BF16)
        _accum(dmu_ref, jnp.sum(dv * (shifted - pv), axis=0, keepdims=True), i == 0)

    d_specs = []
    for a in parts:
        after = lambda i, nd=a.ndim: (jnp.minimum((i + 1) * (tb // 8), t // 8 - 1),) + (0,) * (nd - 1)
        d_specs += [_flat_spec(a, tb), pl.BlockSpec((8,) + a.shape[1:], after)]
    return pl.pallas_call(
        body, name=name, grid=(nb,),
        in_specs=[_row_spec(width, tb), _prev_rows_spec(width, tb), _vec_spec(width)] + d_specs,
        out_specs=[_row_spec(width, tb), _vec_spec(width)],
        out_shape=[jax.ShapeDtypeStruct((t, width), BF16), jax.ShapeDtypeStruct((1, width), F32)],
        compiler_params=_cparams("arbitrary"),
    )(p, p, mu, *[a for a in parts for _ in range(2)])


def _lane_group_sum2d(x):
    x = x + pltpu.roll(x, N_HEADS, 1)
    return x + pltpu.roll(x, 2 * N_HEADS, 1)


@jax.custom_vjp
def _lane_group_sum(x):
    return _lane_group_sum2d(x.reshape(-1, LANES)).reshape(x.shape)


_lane_group_sum.defvjp(lambda x: (_lane_group_sum(x), None), lambda _, ct: (_lane_group_sum(ct),))


def _head_sum(x):
    return _lane_group_sum(jnp.sum(x, axis=1, keepdims=True))


def _f_kprep(k, a, r, kkp, kap, rkp):
    kk = k * kkp
    kk = kk / jnp.maximum(jnp.sqrt(_head_sum(kk * kk)), 1e-12)
    k2 = k * (1.0 + (a - 1.0) * kap)
    return kk, k2, kk * a, _head_sum(r * k2 * rkp)


def _k_spec(rows=HEAD, tb=TB_K):
    return pl.BlockSpec((tb, rows, LANES), lambda i: (i, 0, 0))


def _kparam_spec(rows=HEAD):
    return pl.BlockSpec((rows, LANES), lambda i: (0, 0))


def _lane_group(shape):
    return lax.broadcasted_iota(jnp.int32, shape, len(shape) - 1) // N_HEADS


def _store_k_layout(ref, xc):
    x2 = xc.reshape(-1, LANES)
    group = _lane_group(x2.shape)
    shifted = [x2] + [pltpu.roll(x2, N_HEADS * k, 1) for k in range(1, VM)]
    for q in range(VM):
        out = shifted[0]
        for k in range(1, VM):
            out = jnp.where(group == (q + k) % VM, shifted[k], out)
        ref[:, pl.ds(q, VD, stride=VM), :] = out.reshape(xc.shape)


def _load_compact(ref):
    shape = (ref.shape[0], VD, LANES)
    rows = [ref[:, pl.ds(q, VD, stride=VM), :].reshape(-1, LANES) for q in range(VM)]
    group = _lane_group(rows[0].shape)
    acc = None
    for k in range(VM):
        t = rows[-k % VM]
        for g in range(1, VM):
            t = jnp.where(group == g, rows[(g - k) % VM], t)
        if k:
            t = pltpu.roll(t, LANES - N_HEADS * k, 1)
        acc = t if acc is None else acc + t
    return acc.reshape(shape)


def _rwkv_kprep_fwd(k, a, r, w, kkp, kap, rkp, name):
    t = k.shape[0]

    def body(k_ref, a_ref, r_ref, w_ref, kkp_ref, kap_ref, rkp_ref, kk_ref, k2_ref, b_ref, r4_ref, w4_ref, rk_ref):
        rv = r_ref[...]
        kk, k2, b, rk_ref[...] = _f_kprep(k_ref[...], a_ref[...], rv, kkp_ref[...], kap_ref[...], rkp_ref[...])
        for ref, val in ((kk_ref, kk), (k2_ref, k2), (b_ref, b), (r4_ref, rv), (w4_ref, w_ref[...])):
            _store_k_layout(ref, val)

    big = jax.ShapeDtypeStruct((t, HEAD, LANES), F32)
    return pl.pallas_call(
        body, name=name, grid=(t // TB_K,),
        in_specs=[_k_spec(VD)] * 4 + [_kparam_spec(VD)] * 3, out_specs=[_k_spec()] * 5 + [_k_spec(1)],
        out_shape=[big] * 5 + [jax.ShapeDtypeStruct((t, 1, LANES), F32)], compiler_params=_cparams("parallel"),
    )(k, a, r, w, kkp, kap, rkp)


def _rwkv_kprep_bwd(k, a, r, kkp, kap, rkp, dkk, dk2, db, drk, dr_scan, dw_scan, name):
    t = k.shape[0]

    def body(k_ref, a_ref, r_ref, kkp_ref, kap_ref, rkp_ref, dkk_ref, dk2_ref, db_ref, drk_ref, drs_ref, dws_ref,
             dk_ref, da_ref, dr_ref, dw_ref, dkkp_ref, dkap_ref, drkp_ref):
        _, vjp = jax.vjp(_f_kprep, k_ref[...], a_ref[...], r_ref[...], kkp_ref[...], kap_ref[...], rkp_ref[...])
        dk, da, dr, dkkp, dkap, drkp = vjp((_load_compact(dkk_ref), _load_compact(dk2_ref), _load_compact(db_ref),
                                            drk_ref[...]))
        dk_ref[...] = dk
        da_ref[...] = da
        dr_ref[...] = dr + _load_compact(drs_ref)
        dw_ref[...] = _load_compact(dws_ref)
        first = pl.program_id(0) == 0
        _accum(dkkp_ref, dkkp, first)
        _accum(dkap_ref, dkap, first)
        _accum(drkp_ref, drkp, first)

    cl = jax.ShapeDtypeStruct((t, VD, LANES), F32)
    par = jax.ShapeDtypeStruct((VD, LANES), F32)
    return pl.pallas_call(
        body, name=name, grid=(t // TB_K,),
        in_specs=[_k_spec(VD)] * 3 + [_kparam_spec(VD)] * 3 + [_k_spec()] * 3 + [_k_spec(1), _k_spec(), _k_spec()],
        out_specs=[_k_spec(VD)] * 4 + [_kparam_spec(VD)] * 3,
        out_shape=[cl] * 4 + [par] * 3, compiler_params=_cparams("arbitrary"),
    )(k, a, r, kkp, kap, rkp, dkk, dk2, db, drk, dr_scan, dw_scan)


def _f_post(y, v, rk, g, b):
    mean = _lane_group_sum(jnp.sum(y, axis=1, keepdims=True)) * (1.0 / HEAD)
    yc = y - mean
    var = _lane_group_sum(jnp.sum(yc * yc, axis=1, keepdims=True)) * (1.0 / HEAD)
    return yc * lax.rsqrt(var + GN_EPS) * g + b + rk * v


def _rwkv_post_fwd(y, v, rk, g, b, name):
    t = y.shape[0]

    def body(y_ref, v_ref, rk_ref, g_ref, b_ref, o_ref):
        o_ref[...] = _f_post(y_ref[...], v_ref[...], rk_ref[...], g_ref[...], b_ref[...])

    return pl.pallas_call(
        body, name=name, grid=(t // TB_K,),
        in_specs=[_k_spec(VD), _k_spec(VD), _k_spec(1), _kparam_spec(VD), _kparam_spec(VD)], out_specs=_k_spec(VD),
        out_shape=jax.ShapeDtypeStruct((t, VD, LANES), F32), compiler_params=_cparams("parallel"),
    )(y, v, rk, g, b)


def _rwkv_post_bwd(y, v, rk, g, b, do, name):
    t = y.shape[0]

    def body(y_ref, v_ref, rk_ref, g_ref, b_ref, do_ref, dy_ref, dv_ref, drk_ref, dg_ref, db_ref):
        _, vjp = jax.vjp(_f_post, y_ref[...], v_ref[...], rk_ref[...], g_ref[...], b_ref[...])
        dy, dv, drk, dg, db = vjp(do_ref[...])
        dy_ref[...] = dy
        dv_ref[...] = dv
        drk_ref[...] = drk
        first = pl.program_id(0) == 0
        _accum(dg_ref, dg, first)
        _accum(db_ref, db, first)

    vl = jax.ShapeDtypeStruct((t, VD, LANES), F32)
    par = jax.ShapeDtypeStruct((VD, LANES), F32)
    return pl.pallas_call(
        body, name=name, grid=(t // TB_K,),
        in_specs=[_k_spec(VD), _k_spec(VD), _k_spec(1), _kparam_spec(VD), _kparam_spec(VD), _k_spec(VD)],
        out_specs=[_k_spec(VD), _k_spec(VD), _k_spec(1), _kparam_spec(VD), _kparam_spec(VD)],
        out_shape=[vl, vl, jax.ShapeDtypeStruct((t, 1, LANES), F32), par, par], compiler_params=_cparams("arbitrary"),
    )(y, v, rk, g, b, do)


def _f_gate(o, z):
    return o * jax.nn.silu(z)


def _rwkv_gate_fwd(o, z, name):
    t = z.shape[0]

    def body(o_ref, z_ref, u_ref):
        u_ref[...] = _f_gate(_load_flat(o_ref), z_ref[...]).astype(BF16)

    return pl.pallas_call(
        body, name=name, grid=(t // TB_NORM,), in_specs=[_k_spec(VD, TB_NORM), _row_spec(D)], out_specs=_row_spec(D),
        out_shape=jax.ShapeDtypeStruct((t, D), BF16), compiler_params=_cparams("parallel"),
    )(o, z)


def _rwkv_gate_bwd(o, z, du, name):
    t = z.shape[0]

    def body(o_ref, z_ref, du_ref, do_ref, dz_ref):
        _, vjp = jax.vjp(_f_gate, _load_flat(o_ref), z_ref[...])
        do, dz_ref[...] = vjp(du_ref[...])
        _store_compact(do_ref, do)

    return pl.pallas_call(
        body, name=name, grid=(t // TB_NORM,), in_specs=[_k_spec(VD, TB_NORM), _row_spec(D), _row_spec(D)],
        out_specs=[_k_spec(VD, TB_NORM), _row_spec(D)],
        out_shape=[jax.ShapeDtypeStruct((t, VD, LANES), F32), jax.ShapeDtypeStruct((t, D), F32)],
        compiler_params=_cparams("parallel"),
    )(o, z, du)


def _colsum(x):
    return jnp.sum(x, axis=0, keepdims=True)


def _rwkv_scan_fwd(r4, w4, k24, kk4, b4, v, name, rider=None):
    t = r4.shape[0]
    tb = TB_SCAN

    def body(r_ref, w_ref, k2_ref, kk_ref, b_ref, v_ref, y_ref, sall_ref, sa_ref, s_scr):
        @pl.when(pl.program_id(0) == 0)
        def _():
            s_scr[...] = jnp.zeros_like(s_scr)

        sall_ref[0] = s_scr[...]

        def step(tt, dst):
            kk = kk_ref[tt]
            sas = []
            for vd in range(VD):
                sa = _colsum(sall_ref[tt, pl.ds(vd * HEAD, HEAD), :] * kk)
                sa_ref[tt, pl.ds(vd, 1), :] = sa
                sas.append(sa)
            w, b, k2, r = w_ref[tt], b_ref[tt], k2_ref[tt], r_ref[tt]
            for vd in range(VD):
                rows = pl.ds(vd * HEAD, HEAD)
                s = sall_ref[tt, rows, :] * w - sas[vd] * b + v_ref[tt, pl.ds(vd, 1), :] * k2
                dst[rows, :] = s
                y_ref[tt, pl.ds(vd, 1), :] = _colsum(s * r)

        def loop_step(tt, carry):
            step(tt, sall_ref.at[tt + 1])
            return carry

        lax.fori_loop(0, tb - 1, loop_step, 0)
        step(tb - 1, s_scr)

    vl = jax.ShapeDtypeStruct((t, VD, LANES), F32)
    return _compute_call(
        body, (r4, w4, k24, kk4, b4, v), name=name, grid=(t // tb,),
        in_specs=[_k_spec(HEAD, tb)] * 5 + [_k_spec(VD, tb)],
        out_specs=[_k_spec(VD, tb), _k_spec(S_ROWS, tb), _k_spec(VD, tb)],
        out_shape=[vl, jax.ShapeDtypeStruct((t, S_ROWS, LANES), F32), vl],
        scratch_shapes=[pltpu.VMEM((S_ROWS, LANES), F32)], semantics=("arbitrary",), rider=rider)


def _rwkv_scan_bwd(dy, s_all, sa_all, r4, w4, k24, kk4, b4, v, name, rider=None):
    t = r4.shape[0]
    tb = TB_SCAN
    nb = t // tb
    blk = lambda i: nb - 1 - i

    def body(dy_ref, sall_ref, sa_ref, r_ref, w_ref, k2_ref, kk_ref, b_ref, v_ref,
             dr_ref, dw_ref, dk2_ref, dkk_ref, db_ref, dv_ref, ds_scr):
        @pl.when(pl.program_id(0) == 0)
        def _():
            ds_scr[...] = jnp.zeros_like(ds_scr)

        def step(j, carry):
            tt = tb - 1 - j
            vrow = lambda ref, vd: ref[tt, pl.ds(vd, 1), :]
            srows = lambda vd: pl.ds(vd * HEAD, HEAD)
            r, k2, b = r_ref[tt], k2_ref[tt], b_ref[tt]
            dsas = []
            for vd in range(VD):
                ds = ds_scr[srows(vd), :] + vrow(dy_ref, vd) * r
                ds_scr[srows(vd), :] = ds
                dv_ref[tt, pl.ds(vd, 1), :] = _colsum(ds * k2)
                dsas.append(-_colsum(ds * b))
            zero = jnp.zeros((HEAD, LANES), F32)
            dk2, q, sady, vdy = zero, zero, 0.0, 0.0
            for vd in range(VD):
                dyv = vrow(dy_ref, vd)
                dk2 = dk2 + ds_scr[srows(vd), :] * vrow(v_ref, vd)
                q = q + sall_ref[tt, srows(vd), :] * dyv
                sady = sady + vrow(sa_ref, vd) * dyv
                vdy = vdy + vrow(v_ref, vd) * dyv
            dk2_ref[tt] = dk2
            dr_ref[tt] = w_ref[tt] * q - b_ref[tt] * sady + k2_ref[tt] * vdy
            dw, dkk = zero, zero
            for vd in range(VD):
                sp = sall_ref[tt, srows(vd), :]
                dw = dw + ds_scr[srows(vd), :] * sp
                dkk = dkk + sp * dsas[vd]
            dw_ref[tt] = dw
            dkk_ref[tt] = dkk
            w, kk = w_ref[tt], kk_ref[tt]
            db = zero
            for vd in range(VD):
                ds = ds_scr[srows(vd), :]
                db = db - ds * vrow(sa_ref, vd)
                ds_scr[srows(vd), :] = ds * w + dsas[vd] * kk
            db_ref[tt] = db
            return carry

        lax.fori_loop(0, tb, step, 0)

    rk = lambda rows: pl.BlockSpec((tb, rows, LANES), lambda i: (blk(i), 0, 0))
    big = jax.ShapeDtypeStruct((t, HEAD, LANES), F32)
    return _compute_call(
        body, (dy, s_all, sa_all, r4, w4, k24, kk4, b4, v), name=name, grid=(nb,),
        in_specs=[rk(VD), rk(S_ROWS), rk(VD)] + [rk(HEAD)] * 5 + [rk(VD)],
        out_specs=[rk(HEAD)] * 5 + [rk(VD)],
        out_shape=[big] * 5 + [jax.ShapeDtypeStruct((t, VD, LANES), F32)],
        scratch_shapes=[pltpu.VMEM((S_ROWS, LANES), F32)], semantics=("arbitrary",), rider=rider)


def _rwkv_mixer_fwd(p_main, p_lo, prm, tag, rider):
    r, k, v, w, a, z, xs_lo = _rwkv_pre_fwd(p_main, p_lo, prm["mu_main"], prm["mu_lo"], prm["w0"], prm["a0"],
                                            prm["wl"], prm["al"], tag + "_pre")
    kk4, k24, b4, r4, w4, rk = _rwkv_kprep_fwd(k, a, r, w, prm["kkp"], prm["kap"], prm["rkp"], tag + "_kprep")
    (y, s_all, sa_all), ridden = _ridden(_rwkv_scan_fwd(r4, w4, k24, kk4, b4, v, tag + "_scan", rider), rider)
    o = _rwkv_post_fwd(y, v, rk, prm["gn_g"], prm["gn_b"], tag + "_post")
    u = _rwkv_gate_fwd(o, z, tag + "_gate")
    saved = dict(z=z, xs_lo=xs_lo, r=r, k=k, a=a, v=v, r4=r4, w4=w4, kk4=kk4, k24=k24, b4=b4, rk=rk,
                 y=y, s_all=s_all, sa_all=sa_all, o=o)
    return u, saved, ridden


def _rwkv_mixer_bwd(p_main, p_lo, prm, sv, du, tag, rider):
    do, dz = _rwkv_gate_bwd(sv["o"], sv["z"], du, tag + "_gate_b")
    dy, dv_post, drk, dgn_g, dgn_b = _rwkv_post_bwd(sv["y"], sv["v"], sv["rk"], prm["gn_g"], prm["gn_b"], do,
                                                    tag + "_post_b")
    (dr_s, dw_s, dk24, dkk4, db4, dv_scan), ridden = _ridden(_rwkv_scan_bwd(
        dy, sv["s_all"], sv["sa_all"], sv["r4"], sv["w4"], sv["k24"], sv["kk4"], sv["b4"], sv["v"], tag + "_scan_b", rider), rider)
    dk, da, dr, dw, dkkp, dkap, drkp = _rwkv_kprep_bwd(sv["k"], sv["a"], sv["r"], prm["kkp"], prm["kap"], prm["rkp"],
                                                       dkk4, dk24, db4, drk, dr_s, dw_s, tag + "_kprep_b")
    dxs_lo, dw0, da0, dwl, dal = _rwkv_lora_bwd(sv["xs_lo"], prm["w0"], prm["a0"], prm["wl"], prm["al"], dw, da,
                                                tag + "_lora_b")
    dxs_main = [[dr], [dk], [dv_post, dv_scan], [dz]]
    dp_main, dmu_main = _lerp_bwd(p_main, dxs_main, prm["mu_main"], tag + "_lerp_main_b")
    dp_lo, dmu_lo = _lerp_bwd(p_lo, [[dxs_lo]], prm["mu_lo"], tag + "_lerp_lo_b")
    grads = dict(mu_main=dmu_main, mu_lo=dmu_lo, w0=dw0, a0=da0, wl=dwl, al=dal, kkp=dkkp, kap=dkap, rkp=drkp,
                 gn_g=dgn_g, gn_b=dgn_b)
    return dp_main, dp_lo, grads, ridden


N_DEV = 8
N_CHIPS = 4
ANY = pl.BlockSpec(memory_space=pl.ANY)


def _place():
    return lax.axis_index("x"), lax.axis_index("y"), lax.axis_index("c")


def _remote(src, dst, send_sems, recv_sems, k, dev):
    return pltpu.make_async_remote_copy(src_ref=src, dst_ref=dst, send_sem=send_sems.at[k], recv_sem=recv_sems.at[k],
                                        device_id=dev, device_id_type=MESHT)


def _all_gather8(v, name):
    def body(buf_ref, out_ref, send_sems, recv_sems):
        del buf_ref
        x, y, c = _place()
        mine = out_ref.at[4 * x + 2 * y + c]
        peers = [(x ^ (k >> 2), y ^ ((k >> 1) & 1), c ^ (k & 1)) for k in range(1, N_DEV)]
        sends = [_remote(mine, mine, send_sems, recv_sems, k, peer) for k, peer in enumerate(peers)]
        for cp in sends:
            cp.start()
        for k, (px, py, pc) in enumerate(peers):
            _remote(mine, out_ref.at[4 * px + 2 * py + pc], send_sems, recv_sems, k, (x, y, c)).wait_recv()
        for cp in sends:
            cp.wait_send()

    return pl.pallas_call(
        body, name=name, in_specs=[ANY], out_specs=ANY, input_output_aliases={0: 0},
        out_shape=jax.ShapeDtypeStruct((N_DEV,) + v.shape, v.dtype),
        scratch_shapes=[pltpu.SemaphoreType.DMA((N_DEV - 1,)), pltpu.SemaphoreType.DMA((N_DEV - 1,))],
    )(jnp.broadcast_to(v[None], (N_DEV,) + v.shape))


def _other_chips(x, y):
    return [(1 - x, y), (x, 1 - y), (1 - x, 1 - y)]


GATHER_SEMS = 6


def _gather_buffer(v):
    return jnp.broadcast_to(v[None], (N_CHIPS,) + v.shape)


def _gather_start(bufs, send_sems, recv_sems):
    x, y, c = _place()
    for i, buf in enumerate(bufs):
        mine = buf.at[2 * x + y, c]
        for j, (cx, cy) in enumerate(_other_chips(x, y)):
            _remote(mine, mine, send_sems, recv_sems, GATHER_SEMS * i + j, (cx, cy, c)).start()


def _gather_finish(bufs, send_sems, recv_sems):
    x, y, c = _place()
    chips = _other_chips(x, y)
    passed = []
    for i, buf in enumerate(bufs):
        mine = buf.at[2 * x + y, c]
        for j, (cx, cy) in enumerate(chips):
            landed = buf.at[2 * cx + cy, c]
            _remote(mine, landed, send_sems, recv_sems, GATHER_SEMS * i + j, (x, y, c)).wait_recv()
            fwd = _remote(landed, landed, send_sems, recv_sems, GATHER_SEMS * i + 3 + j, (x, y, 1 - c))
            fwd.start()
            passed.append(fwd)
    for i, buf in enumerate(bufs):
        mine = buf.at[2 * x + y, c]
        for j, (cx, cy) in enumerate(chips):
            _remote(mine, buf.at[2 * cx + cy, 1 - c], send_sems, recv_sems, GATHER_SEMS * i + 3 + j, (x, y, c)).wait_recv()
            _remote(mine, mine, send_sems, recv_sems, GATHER_SEMS * i + j, (cx, cy, c)).wait_send()
    for fwd in passed:
        fwd.wait_send()


def _gather_rider(bufs):
    return _Rider(bufs, GATHER_SEMS * len(bufs), _gather_start, _gather_finish)


def _chip_gather(bufs, name):
    n = len(bufs)

    def body(*refs):
        out_refs, (send_sems, recv_sems) = refs[n:2 * n], refs[2 * n:]
        _gather_start(out_refs, send_sems, recv_sems)
        _gather_finish(out_refs, send_sems, recv_sems)

    return pl.pallas_call(
        body, name=name, in_specs=[ANY] * n, out_specs=[ANY] * n, input_output_aliases={i: i for i in range(n)},
        out_shape=[jax.ShapeDtypeStruct(b.shape, b.dtype) for b in bufs], scratch_shapes=_dma_sems(GATHER_SEMS * n),
    )(*bufs)


RS_W = 1024
RS_BLOCK_BYTES = 4 << 20


def _dma_sems(n):
    return [pltpu.SemaphoreType.DMA((n,)), pltpu.SemaphoreType.DMA((n,))]


def _pair_exchange_copies(refs, send_sems, recv_sems):
    n = len(refs) // 2
    x, y, c = _place()
    return [_remote(refs[i].at[s, 1 - c], refs[n + i].at[s], send_sems, recv_sems, N_CHIPS * i + s, (x, y, 1 - c))
            for i in range(n) for s in range(N_CHIPS)]


def _pair_exchange_start(refs, send_sems, recv_sems):
    for cp in _pair_exchange_copies(refs, send_sems, recv_sems):
        cp.start()


def _pair_exchange_finish(refs, send_sems, recv_sems):
    for cp in _pair_exchange_copies(refs, send_sems, recv_sems):
        cp.wait()


def _pair_exchange_rider(gs):
    landing = [lax.empty((N_CHIPS,) + g.shape[2:], g.dtype) for g in gs]
    return _Rider(list(gs) + landing, N_CHIPS * len(gs), _pair_exchange_start, _pair_exchange_finish)


def _rs_rows(rows, cols):
    cap = max(16, RS_BLOCK_BYTES // (N_CHIPS * 4 * cols))
    return rows if rows <= cap else max(d for d in range(16, cap + 1, 16) if rows % d == 0)


def _rs_pair_add(g, got, c_arr, name):
    _, _, rows, width = g.shape
    tr = _rs_rows(rows, width)

    def body(c_ref, g_ref, got_ref, p_ref):
        p_ref[...] = (g_ref[...] + got_ref[...]).astype(BF16)

    return pl.pallas_call(
        body, name=name,
        grid_spec=pltpu.PrefetchScalarGridSpec(
            num_scalar_prefetch=1, grid=(rows // tr,),
            in_specs=[pl.BlockSpec((N_CHIPS, None, tr, width), lambda i, c_ref: (0, c_ref[0], i, 0)),
                      pl.BlockSpec((N_CHIPS, tr, width), lambda i, c_ref: (0, i, 0))],
            out_specs=pl.BlockSpec((N_CHIPS, tr, width), lambda i, c_ref: (0, i, 0))),
        out_shape=jax.ShapeDtypeStruct((N_CHIPS, rows, width), BF16), compiler_params=_cparams("parallel"),
    )(c_arr, g, got)


def _chip_exchange_copies(refs, send_sems, recv_sems):
    n = len(refs) // 2
    x, y, c = _place()
    return [_remote(refs[i].at[2 * cx + cy], refs[n + i].at[j], send_sems, recv_sems, 3 * i + j, (cx, cy, c))
            for i in range(n) for j, (cx, cy) in enumerate(_other_chips(x, y))]


def _chip_exchange_start(refs, send_sems, recv_sems):
    for cp in _chip_exchange_copies(refs, send_sems, recv_sems):
        cp.start()


def _chip_exchange_finish(refs, send_sems, recv_sems):
    n = len(refs) // 2
    x, y, c = _place()
    for i in range(n):
        for j in range(3):
            _remote(refs[i].at[2 * x + y], refs[n + i].at[j], send_sems, recv_sems, 3 * i + j, (x, y, c)).wait_recv()
    for cp in _chip_exchange_copies(refs, send_sems, recv_sems):
        cp.wait_send()


def _chip_exchange_buffers(ps):
    return [lax.empty((3,) + p.shape[1:], p.dtype) for p in ps]


def _chip_exchange_rider(ps):
    return _Rider(list(ps) + _chip_exchange_buffers(ps), 3 * len(ps), _chip_exchange_start, _chip_exchange_finish)


def _rs_chip_add(p, q, idx, name):
    _, rows, width = q.shape
    tr = _rs_rows(rows, width)

    def body(idx_ref, p_ref, q_ref, r_ref):
        qv = q_ref[...].astype(F32)
        r_ref[...] = ((p_ref[...].astype(F32) + qv[0]) + qv[1]) + qv[2]

    return pl.pallas_call(
        body, name=name,
        grid_spec=pltpu.PrefetchScalarGridSpec(
            num_scalar_prefetch=1, grid=(rows // tr,),
            in_specs=[pl.BlockSpec((None, tr, width), lambda i, idx_ref: (idx_ref[0], i, 0)),
                      pl.BlockSpec((3, tr, width), lambda i, idx_ref: (0, i, 0))],
            out_specs=pl.BlockSpec((None, tr, width), lambda i, idx_ref: (idx_ref[1], i, 0))),
        out_shape=jax.ShapeDtypeStruct((2, rows, width), F32), compiler_params=_cparams("parallel"),
    )(idx, p, q)


def _rs_pair_share(rs, name):
    n = len(rs)

    def body(*refs):
        out_refs, (send_sems, recv_sems) = refs[n:2 * n], refs[2 * n:]
        x, y, c = _place()
        sends = [_remote(out_refs[i].at[c], out_refs[i].at[c], send_sems, recv_sems, i, (x, y, 1 - c)) for i in range(n)]
        for cp in sends:
            cp.start()
        for i in range(n):
            _remote(out_refs[i].at[c], out_refs[i].at[1 - c], send_sems, recv_sems, i, (x, y, c)).wait_recv()
        for cp in sends:
            cp.wait_send()

    return pl.pallas_call(
        body, name=name, in_specs=[ANY] * n, out_specs=[ANY] * n, input_output_aliases={i: i for i in range(n)},
        out_shape=[jax.ShapeDtypeStruct(r.shape, r.dtype) for r in rs], scratch_shapes=_dma_sems(n),
    )(*rs)


def _rs_pair_sums(gs, gots, core, tag):
    c_arr = core.astype(jnp.int32).reshape(1)
    return [_rs_pair_add(g, got, c_arr, f"{tag}_pair_add{i}") for i, (g, got) in enumerate(zip(gs, gots))]


def _rs_finish(ps, qs, chip, core, tag):
    idx = jnp.stack([chip, core]).astype(jnp.int32)
    rs = [_rs_chip_add(p, q, idx, f"{tag}_chip_add{i}") for i, (p, q) in enumerate(zip(ps, qs))]
    return _rs_pair_share(rs, tag + "_share")


def _sum_leading(a, name):
    n, rows, width = a.shape
    cap = max(8, RS_BLOCK_BYTES // (n * 4 * width))
    tr = rows if rows <= cap else max(d for d in range(8, cap + 1, 8) if rows % d == 0)

    def body(a_ref, o_ref):
        acc = a_ref[0]
        for d in range(1, n):
            acc = acc + a_ref[d]
        o_ref[...] = acc

    return pl.pallas_call(
        body, name=name, grid=(rows // tr,), in_specs=[pl.BlockSpec((n, tr, width), lambda i: (0, i, 0))],
        out_specs=pl.BlockSpec((tr, width), lambda i: (i, 0)), out_shape=jax.ShapeDtypeStruct((rows, width), F32),
        compiler_params=_cparams("parallel"),
    )(a)


def _pair_swap(v, name):
    def body(v_ref, got_ref, send_sems, recv_sems):
        x, y, c = _place()
        cp = _remote(v_ref, got_ref, send_sems, recv_sems, 0, (x, y, 1 - c))
        cp.start()
        cp.wait()

    return pl.pallas_call(body, name=name, in_specs=[ANY], out_specs=ANY, out_shape=jax.ShapeDtypeStruct(v.shape, v.dtype),
                          scratch_shapes=_dma_sems(1))(v)


def _replicated_pair_sum(v, name):
    rows, width = v.shape
    pair = _sum_leading(jnp.stack([v, _pair_swap(v, name + "_swap")]), name + "_pair_add")
    return _gather_buffer(pair.reshape(2, rows // 2, width))


def _replicated_chip_sum(gathered, name):
    return _sum_leading(gathered.reshape(N_CHIPS, -1, gathered.shape[-1]), name + "_chip_add")


MOD_COLS = 3 * D // N_CHIPS
MOD_TK = 512


def _mod_partial(c_all, mod_w, name):
    nk = D // MOD_TK

    def body(c_ref, w_ref, o_ref):
        l = pl.program_id(1)
        part = _bdot_nn(jax.nn.silu(c_ref[...]), w_ref[0])
        _accum(o_ref.at[0], part, l == 0)

    return pl.pallas_call(
        body, name=name, grid=(DEPTH, nk),
        in_specs=[pl.BlockSpec((N_DEV, MOD_TK), lambda i, l: (0, l)), pl.BlockSpec((1, MOD_TK, MOD_COLS), lambda i, l: (i, l, 0))],
        out_specs=pl.BlockSpec((1, N_DEV, MOD_COLS), lambda i, l: (i, 0, 0)),
        out_shape=jax.ShapeDtypeStruct((DEPTH, N_DEV, MOD_COLS), F32), compiler_params=_cparams("parallel", "arbitrary"),
    )(c_all, mod_w)


def _mod_w_grad(c_all, dmod, name):
    def body(c_ref, d_ref, o_ref):
        o_ref[0] = _dg(jax.nn.silu(c_ref[...]).astype(BF16), d_ref[0].astype(BF16), _TN)

    return pl.pallas_call(
        body, name=name, grid=(DEPTH, D // MOD_TK),
        in_specs=[pl.BlockSpec((N_DEV, MOD_TK), lambda i, l: (0, l)), pl.BlockSpec((1, N_DEV, MOD_COLS), lambda i, l: (i, 0, 0))],
        out_specs=pl.BlockSpec((1, MOD_TK, MOD_COLS), lambda i, l: (i, l, 0)),
        out_shape=jax.ShapeDtypeStruct((DEPTH, D, MOD_COLS), F32), compiler_params=_cparams("parallel", "parallel"),
    )(c_all, dmod)


ADAM_BLOCK_BYTES = 1 << 20


def _adamw(w, g, m, v, name, rider=None):
    shape = w.shape
    cols = shape[-1]
    rows = w.size // cols
    w, g, m, v = (a.reshape(rows, cols) for a in (w, g, m, v))
    cap = max(8, ADAM_BLOCK_BYTES // (4 * cols))
    tr = rows if rows <= cap else max(d for d in range(8, cap + 1, 8) if rows % d == 0)
    c1 = 1.0 - ADAM_B1 ** ADAM_STEP
    c2 = 1.0 - ADAM_B2 ** ADAM_STEP

    def body(w_ref, g_ref, m_ref, v_ref, d_ref, nm_ref, nv_ref):
        gv = g_ref[...]
        mn = ADAM_B1 * m_ref[...] + (1.0 - ADAM_B1) * gv
        vn = ADAM_B2 * v_ref[...] + (1.0 - ADAM_B2) * (gv * gv)
        nm_ref[...] = mn
        nv_ref[...] = vn
        d_ref[...] = -ADAM_LR * ((mn / c1) / (jnp.sqrt(vn / c2) + ADAM_EPS) + ADAM_WD * w_ref[...])

    spec = pl.BlockSpec((tr, cols), lambda i: (i, 0))
    out = jax.ShapeDtypeStruct((rows, cols), F32)
    (d, nm, nv), ridden = _ridden(_compute_call(
        body, (w, g, m, v), name=name, grid=(rows // tr,), in_specs=[spec] * 4, out_specs=[spec] * 3, out_shape=[out] * 3,
        semantics=("parallel",), rider=rider), rider)
    res = (d.reshape(shape), nm.reshape(shape), nv.reshape(shape))
    return res if rider is None else (res, ridden)


W_NAMES = ("norm_g", "mod_w", "mod_b", "final_norm_g", "sg_w_in", "sg_w_out", "sg_ln_g", "sg_ln_b", "sg_w_spatial",
           "sg_b_spatial", "swa_w_in", "swa_w_out", "swa_sinks", "rwkv_w_in", "rwkv_w_out", "rwkv_mu", "rwkv_w0",
           "rwkv_w_lora", "rwkv_a0", "rwkv_a_lora", "rwkv_k_k", "rwkv_k_a", "rwkv_r_k", "rwkv_gn_g", "rwkv_gn_b")
SMALL = {"sg_ln_g": 1, "sg_ln_b": 1, "rwkv_mu": 1, "rwkv_w0": 1, "rwkv_w_lora": 2, "rwkv_a0": 1, "rwkv_a_lora": 2,
         "rwkv_k_k": 1, "rwkv_k_a": 1, "rwkv_gn_g": 1, "rwkv_gn_b": 1}
REPLICATED = ("norm_g", "final_norm_g", "sg_w_spatial", "sg_b_spatial", "swa_sinks", "rwkv_r_k")
KINDS = ("sg", "swa", "rwkv", "sg")


def _pad_to(flat, n):
    return jnp.pad(flat, (0, n - flat.shape[0]))


def _round_up(n, m):
    return -(-n // m) * m


def _join_shards(gathered, axis):
    return jnp.concatenate([gathered[s] for s in range(N_CHIPS)], axis=axis)


def _chip_blocks(full, axis):
    return jnp.stack(jnp.split(full, N_CHIPS, axis=axis)).reshape(N_CHIPS, -1)


def _weight_buffer(w):
    rows, cols = w.shape
    return _gather_buffer(w.astype(BF16).reshape(2, rows // 2, cols))


def _chip_shards(buf):
    return buf.reshape(N_CHIPS, -1, buf.shape[-1])


def _small_buffer(shards):
    flat = jnp.concatenate([shards[n].reshape(-1) for n in SMALL])
    rows = _round_up(flat.shape[0], 2 * 8 * LANES) // (2 * LANES)
    return _gather_buffer(_pad_to(flat, 2 * rows * LANES).reshape(2, rows, LANES))


def _unpack_small(buf, shards):
    got = buf.reshape(N_CHIPS, -1)
    out, off = {}, 0
    for n, axis in SMALL.items():
        size = shards[n].size
        out[n] = _join_shards(got[:, off:off + size].reshape((N_CHIPS,) + shards[n].shape), axis)
        off += size
    return out


def _lora_pad_rows(w):
    return jnp.pad(w, ((0, LORA_PAD - LORA), (0, 0)))


def _lo_cols(a):
    z = jnp.zeros(a.shape[:-1] + (LORA_PAD - LORA,), a.dtype)
    return jnp.concatenate([a[..., :LORA], z, a[..., LORA:], z], axis=-1)


def _lo_cols_inv(a):
    return jnp.concatenate([a[..., :LORA], a[..., LORA_PAD:LORA_PAD + LORA]], axis=-1)


def _rows_dim_major(w):
    return w.reshape(N_HEADS, HEAD, -1).swapaxes(0, 1).reshape(w.shape)


def _rows_head_major(w):
    return w.reshape(HEAD, N_HEADS, -1).swapaxes(0, 1).reshape(w.shape)


def kernel(x, c, positions, norm_g, mod_w, mod_b, final_norm_g, sg_w_in, sg_w_out, sg_ln_g, sg_ln_b, sg_w_spatial,
           sg_b_spatial, swa_w_in, swa_w_out, swa_sinks, rwkv_w_in, rwkv_w_out, rwkv_mu, rwkv_w0, rwkv_w_lora, rwkv_a0,
           rwkv_a_lora, rwkv_k_k, rwkv_k_a, rwkv_r_k, rwkv_gn_g, rwkv_gn_b, loss_target, m_norm_g, m_mod_w, m_mod_b,
           m_final_norm_g, m_sg_w_in, m_sg_w_out, m_sg_ln_g, m_sg_ln_b, m_sg_w_spatial, m_sg_b_spatial, m_swa_w_in,
           m_swa_w_out, m_swa_sinks, m_rwkv_w_in, m_rwkv_w_out, m_rwkv_mu, m_rwkv_w0, m_rwkv_w_lora, m_rwkv_a0,
           m_rwkv_a_lora, m_rwkv_k_k, m_rwkv_k_a, m_rwkv_r_k, m_rwkv_gn_g, m_rwkv_gn_b, v_norm_g, v_mod_w, v_mod_b,
           v_final_norm_g, v_sg_w_in, v_sg_w_out, v_sg_ln_g, v_sg_ln_b, v_sg_w_spatial, v_sg_b_spatial, v_swa_w_in,
           v_swa_w_out, v_swa_sinks, v_rwkv_w_in, v_rwkv_w_out, v_rwkv_mu, v_rwkv_w0, v_rwkv_w_lora, v_rwkv_a0,
           v_rwkv_a_lora, v_rwkv_k_k, v_rwkv_k_a, v_rwkv_r_k, v_rwkv_gn_g, v_rwkv_gn_b):
    given = dict(locals())
    w = {n: given[n] for n in W_NAMES}
    xi, yi, ci = _place()
    chip = 2 * xi + yi
    me = 4 * xi + 2 * yi + ci
    xs = [x[0]]

    c_all = _all_gather8(c, "gather_c")[:, 0, :]
    mod_part = _mod_partial(c_all, mod_w, "mod_fwd")
    mod_all = _all_gather8(mod_part, "gather_mod")[::2]
    mod_mine = lax.dynamic_index_in_dim(mod_all, me, axis=2, keepdims=False)
    mod = mod_mine.transpose(1, 0, 2).reshape(DEPTH, 3 * D) + mod_b
    shift, scale, gate = mod[:, :D], mod[:, D:2 * D], mod[:, 2 * D:]

    shards = {"sg_w_in0": sg_w_in[0], "sg_w_out0": sg_w_out[0], "swa_w_in": swa_w_in[0], "swa_w_out": swa_w_out[0],
              "rwkv_w_in": rwkv_w_in[0], "rwkv_w_out": rwkv_w_out[0], "sg_w_in1": sg_w_in[1], "sg_w_out1": sg_w_out[1]}
    bufs = {n: _weight_buffer(s) for n, s in shards.items()}
    fwd_riders = {(0, "in"): ["swa_w_in"], (0, "mix"): ["swa_w_out"], (1, "in"): ["rwkv_w_out"], (1, "mix"): ["rwkv_w_in"],
                  (2, "mix"): ["sg_w_in1", "sg_w_out1"]}
    bufs["sg_w_in0"], bufs["sg_w_out0"], small_buf = _chip_gather(
        [bufs["sg_w_in0"], bufs["sg_w_out0"], _small_buffer(w)], "gather_l0")
    full = _unpack_small(small_buf, w)

    def riding(i, where):
        names = fwd_riders.get((i, where))
        return names, (None if names is None else _gather_rider([bufs[n] for n in names]))

    def arrived(names, ridden):
        for n, b in zip(names or [], ridden):
            bufs[n] = b

    sg_in = lambda j: _chip_shards(bufs[f"sg_w_in{j}"])
    sg_out = lambda j: bufs[f"sg_w_out{j}"].reshape(D, D)
    mu = full["rwkv_mu"][0]
    rw_prm = dict(mu_main=_dim_major(mu[:RW_MAIN].reshape(4, D)).reshape(1, RW_MAIN), mu_lo=_lo_cols(mu[None, RW_MAIN:]),
                  w0=_dim_major(full["rwkv_w0"]), a0=_dim_major(full["rwkv_a0"]),
                  wl=_lora_pad_rows(_dim_major(full["rwkv_w_lora"][0])), al=_lora_pad_rows(_dim_major(full["rwkv_a_lora"][0])),
                  kkp=_param_compact(full["rwkv_k_k"][0]), kap=_param_compact(full["rwkv_k_a"][0]),
                  rkp=_param_compact(rwkv_r_k.reshape(-1)),
                  gn_g=_param_compact(full["rwkv_gn_g"][0]), gn_b=_param_compact(full["rwkv_gn_b"][0]))
    bs_t = [jnp.pad(sg_b_spatial[j].T, ((0, 0), (0, LANES - SG_GROUPS))) for j in range(2)]
    sink_row = jnp.pad(swa_sinks, ((0, 0), (0, LANES - N_HEADS)))
    inv_freq = ROPE_THETA ** (-jnp.arange(HEAD // 2, dtype=F32) / (HEAD // 2))
    ang = positions[0].astype(F32)[:, None] * inv_freq
    cos, sin = jnp.tile(jnp.cos(ang), (1, LANES * 2 // HEAD)), jnp.tile(jnp.sin(ang), (1, LANES * 2 // HEAD))

    def row(a, i):
        return a[i:i + 1]

    hs, ps, us, ys, rw_saved = [], [], [], [], None
    for i, kind in enumerate(KINDS):
        j = i // 3
        tag = f"l{i}_{kind}"
        h = _norm_mod_fwd(xs[i], row(norm_g, i), row(shift, i), row(scale, i), tag + "_norm")
        names_in, rider_in = riding(i, "in")
        names_mix, rider_mix = riding(i, "mix")
        if kind == "sg":
            p, ridden = _ridden(_matmul(h, sg_in(j), "nn", tag + "_in", blocked=True, rider=rider_in), rider_in)
            arrived(names_in, ridden)
            u, ridden = _ridden(_sg_fwd(p, row(full["sg_ln_g"], j), row(full["sg_ln_b"], j), sg_w_spatial[j], bs_t[j],
                                        tag + "_mix", rider_mix), rider_mix)
            w_out = sg_out(j)
        elif kind == "swa":
            swa_in, swa_out = _chip_shards(bufs["swa_w_in"]), bufs["swa_w_out"].reshape(D, D)
            p, ridden = _ridden(_matmul(h, swa_in, "nn", tag + "_in", blocked=True, rider=rider_in), rider_in)
            arrived(names_in, ridden)
            u, ridden = _ridden(_swa_fwd(p, cos, sin, sink_row, tag + "_mix", rider_mix), rider_mix)
            w_out = swa_out
        else:
            rw_in = _join_shards(_chip_shards(bufs["rwkv_w_in"]), axis=1)
            rw_main = _dim_major(rw_in[:, :RW_MAIN].reshape(D, 4, D)).reshape(D, RW_MAIN)
            rw_lo = _lo_cols(rw_in[:, RW_MAIN:])
            rw_out = _rows_dim_major(bufs["rwkv_w_out"].reshape(D, D))
            p = (_matmul(h, rw_main, "nn", tag + "_in"), _matmul(h, rw_lo, "nn", tag + "_in_lo"))
            u, rw_saved, ridden = _rwkv_mixer_fwd(p[0], p[1], rw_prm, tag, rider_mix)
            w_out = rw_out
        arrived(names_mix, ridden)
        y, x_next = _out_proj_resid(u, w_out, xs[i], row(gate, i), tag + "_out")
        xs.append(x_next)
        hs.append(h), ps.append(p), us.append(u), ys.append(y)

    loss_part, dx, d_final_g, dy, d_gate = _final_loss_grad(xs[DEPTH], final_norm_g[None], loss_target[0], ys[DEPTH - 1],
                                                            row(gate, DEPTH - 1), "loss")
    loss = lax.psum(loss_part[0, 0], ("x", "y", "c"))

    gfull = {n: [None, None] for n in ("sg_ln_g", "sg_ln_b", "sg_w_spatial", "sg_b_spatial")}
    gbig = {}
    d_norm_g, d_mod = [None] * DEPTH, [None] * DEPTH
    rs_p, rs_q, riding_names = {}, {}, []

    for i in reversed(range(DEPTH)):
        kind, j = KINDS[i], i // 3
        tag = f"l{i}_{kind}_b"
        rider = _chip_exchange_rider([rs_p[n] for n in riding_names]) if riding_names else None
        w_out = {"sg": sg_out(j), "swa": swa_out, "rwkv": rw_out}[kind]
        du = _matmul(dy, w_out, "nt", tag + "_du")
        dw_out = _matmul(us[i], dy, "tn", tag + "_dwout").reshape(N_CHIPS, D // N_CHIPS, D)
        if kind == "sg":
            (dp, dlg, dlb, dws, dbs), ridden = _ridden(
                _sg_bwd(ps[i], row(full["sg_ln_g"], j), row(full["sg_ln_b"], j), sg_w_spatial[j], bs_t[j], du,
                        tag + "_mix", rider), rider)
            gfull["sg_ln_g"][j], gfull["sg_ln_b"][j] = dlg[0], dlb[0]
            gfull["sg_w_spatial"][j], gfull["sg_b_spatial"][j] = dws, dbs[:, :SG_GROUPS].T
            gbig[f"sg_w_in{j}"] = _matmul(hs[i], dp, "tn", tag + "_dwin", blocked=True)
            gbig[f"sg_w_out{j}"] = dw_out
            dh, dh2 = _matmul(dp, sg_in(j), "nt", tag + "_dh", blocked=True), None
            mine = [f"sg_w_in{j}", f"sg_w_out{j}"]
        elif kind == "swa":
            (dp, dsk), ridden = _ridden(_swa_bwd(ps[i], cos, sin, sink_row, du, tag + "_mix", rider), rider)
            gfull["swa_sinks"] = dsk[:, :N_HEADS]
            gbig["swa_w_in"] = _matmul(hs[i], dp, "tn", tag + "_dwin", blocked=True)
            gbig["swa_w_out"] = dw_out
            dh, dh2 = _matmul(dp, swa_in, "nt", tag + "_dh", blocked=True), None
            mine = ["swa_w_in", "swa_w_out"]
        else:
            dpm, dpl, rg, ridden = _rwkv_mixer_bwd(ps[i][0], ps[i][1], rw_prm, rw_saved, du, tag, rider)
            mine = ["rwkv_w_in", "rwkv_w_out"]
            dw_main = _matmul(hs[i], dpm, "tn", tag + "_dwin")
            dw_lo = _matmul(hs[i], dpl, "tn", tag + "_dwin_lo")
            dw_main = _head_major(dw_main.reshape(D, 4, D)).reshape(D, RW_MAIN)
            dw_in = jnp.concatenate([dw_main, _lo_cols_inv(dw_lo)], axis=1)
            gbig["rwkv_w_in"] = dw_in.reshape(D, N_CHIPS, -1).transpose(1, 0, 2)
            gbig["rwkv_w_out"] = _rows_head_major(dw_out.reshape(D, D)).reshape(dw_out.shape)
            dmu_main = _head_major(rg["mu_main"].reshape(4, D)).reshape(1, RW_MAIN)
            gfull["rwkv_mu"] = jnp.concatenate([dmu_main, _lo_cols_inv(rg["mu_lo"])], axis=1)
            gfull["rwkv_w0"], gfull["rwkv_a0"] = _head_major(rg["w0"]), _head_major(rg["a0"])
            gfull["rwkv_w_lora"], gfull["rwkv_a_lora"] = _head_major(rg["wl"])[None, :LORA], _head_major(rg["al"])[None, :LORA]
            gfull["rwkv_k_k"], gfull["rwkv_k_a"] = _param_compact_inv(rg["kkp"])[None], _param_compact_inv(rg["kap"])[None]
            gfull["rwkv_r_k"] = _param_compact_inv(rg["rkp"]).reshape(1, N_HEADS, HEAD)
            gfull["rwkv_gn_g"], gfull["rwkv_gn_b"] = _param_compact_inv(rg["gn_g"])[None], _param_compact_inv(rg["gn_b"])[None]
            dh, dh2 = _matmul(dpm, rw_main, "nt", tag + "_dh"), _matmul(dpl, rw_lo, "nt", tag + "_dh_lo")
        rs_p.update(zip(riding_names, ridden[:len(riding_names)]))
        rs_q.update(zip(riding_names, ridden[len(riding_names):]))
        if i == 0:
            for n in ("sg_ln_g", "sg_ln_b"):
                gfull[n] = jnp.stack(gfull[n])
            small = jnp.concatenate([_chip_blocks(gfull[n], axis) for n, axis in SMALL.items()], axis=1)
            small_rows = _round_up(small.shape[1], 2 * 16 * LANES) // LANES
            small = jnp.pad(small, ((0, 0), (0, small_rows * LANES - small.shape[1])))
            gbig["small"] = small.reshape(N_CHIPS, small_rows, LANES)
            mine = mine + ["small"]
        gs = [gbig[n].reshape(N_CHIPS, 2, gbig[n].shape[1] // 2, gbig[n].shape[2]) for n in mine]
        pair_rider = _pair_exchange_rider(gs)
        below = (ys[i - 1], row(gate, i - 1)) if i > 0 else None
        (dx, dg, dsh, dsc, *below_grads), gots = _norm_mod_bwd(xs[i], row(norm_g, i), row(shift, i), row(scale, i), dh, dx,
                                                               tag + "_norm", dh2, pair_rider, below)
        d_norm_g[i] = dg[0]
        d_mod[i] = jnp.concatenate([dsh[0], dsc[0], d_gate[0]])
        if below is not None:
            dy, d_gate = below_grads
        rs_p.update(zip(mine, _rs_pair_sums(gots[:len(gs)], gots[len(gs):], ci, f"rs{i}")))
        riding_names = mine
    for n in ("sg_w_spatial", "sg_b_spatial"):
        gfull[n] = jnp.stack(gfull[n])
    gfull["norm_g"], gfull["final_norm_g"] = jnp.stack(d_norm_g), d_final_g[0]

    grads, deltas, new_m, new_v, red = {}, {}, {}, {}, {}

    def finish(names, tag):
        outs = _rs_finish([rs_p[n] for n in names], [rs_q[n] for n in names], chip, ci, tag)
        red.update({n: r.reshape(-1, r.shape[2]) for n, r in zip(names, outs)})

    def adamw(n, rider=None):
        res = _adamw(w[n], grads[n], given["m_" + n], given["v_" + n], "adamw_" + n, rider)
        (deltas[n], new_m[n], new_v[n]), ridden = _ridden(res, rider)
        return ridden

    def ride_exchange(names, on):
        ridden = adamw(on, _chip_exchange_rider([rs_p[n] for n in names]))
        rs_p.update(zip(names, ridden[:len(names)]))
        rs_q.update(zip(names, ridden[len(names):]))

    rep_flat = jnp.concatenate([gfull[n].reshape(-1) for n in REPLICATED])
    rep_rows = _round_up(rep_flat.shape[0], 32 * RS_W) // RS_W
    rep_buffer = _replicated_pair_sum(_pad_to(rep_flat, rep_rows * RS_W).reshape(rep_rows, RS_W), "rep")

    finish(sorted(set(rs_p) - set(riding_names)), "rs_a")
    for n in ("swa_w_in", "swa_w_out", "rwkv_w_in", "rwkv_w_out"):
        grads[n] = red[n][None]
    dmod_all = _all_gather8(jnp.stack(d_mod).reshape(DEPTH * 3 * D // RS_W, RS_W), "gather_dmod")
    grads["mod_b"] = _sum_leading(dmod_all, "sum_dmod").reshape(DEPTH, 3 * D)
    dmod_all = dmod_all.reshape(N_DEV, DEPTH, 3 * D)
    dmod_cols = lax.dynamic_slice_in_dim(dmod_all, chip * MOD_COLS, MOD_COLS, axis=2).transpose(1, 0, 2)
    grads["mod_w"] = _mod_w_grad(c_all, dmod_cols, "mod_w_grad")
    ride_exchange(["sg_w_in0"], on="mod_w")
    ride_exchange(["sg_w_out0", "small"], on="rwkv_w_in")
    finish(riding_names, "rs_b")
    grads["sg_w_in"] = jnp.stack([red["sg_w_in0"], red["sg_w_in1"]])
    grads["sg_w_out"] = jnp.stack([red["sg_w_out0"], red["sg_w_out1"]])
    small_red, off = red["small"].reshape(-1), 0
    for n in SMALL:
        grads[n] = small_red[off:off + w[n].size].reshape(w[n].shape)
        off += w[n].size

    (rep_gathered,) = adamw("sg_w_in", _gather_rider([rep_buffer]))
    rep_sum, off = _replicated_chip_sum(rep_gathered, "rep").reshape(-1), 0
    for n in REPLICATED:
        grads[n] = rep_sum[off:off + w[n].size].reshape(w[n].shape)
        off += w[n].size

    for n in W_NAMES:
        if n not in deltas:
            adamw(n)
    return (loss, dx[None], *[grads[n] for n in W_NAMES], *[deltas[n] for n in W_NAMES],
            *[new_m[n] for n in W_NAMES], *[new_v[n] for n in W_NAMES])
```

```python
import functools
import math

import jax
import jax.numpy as jnp
from jax import lax
from jax.experimental import pallas as pl
from jax.experimental.pallas import tpu as pltpu

F32 = jnp.float32
BF16 = jnp.bfloat16
HIGHEST = lax.Precision.HIGHEST

D = 2048
DEPTH = 4
CHUNK = 128
SG_GROUPS = 16
HEAD = 64
N_HEADS = D // HEAD
KV_HEADS = 4
KVW = KV_HEADS * HEAD
ROPE_THETA = 10000.0
LORA = 96
LORA_PAD = 128
DECAY_SCALE = math.exp(-0.5)
GN_EPS = 64e-5
RMS_EPS = 1e-6
LN_EPS = 1e-5
ADAM_LR, ADAM_B1, ADAM_B2, ADAM_EPS, ADAM_WD, ADAM_STEP = 0.001, 0.9, 0.999, 1e-08, 0.01, 10
LANES = 128
NEG = -1e30
VMEM_LIMIT = 56 * 1024 * 1024

MESHT = pl.DeviceIdType.MESH


def _cparams(*sem):
    return pltpu.CompilerParams(dimension_semantics=sem, vmem_limit_bytes=VMEM_LIMIT)


class _Rider:
    def __init__(self, arrays, n_sems, start, finish):
        self.arrays, self.n_sems, self.start, self.finish = list(arrays), n_sems, start, finish


def _ridden(res, rider):
    return (res, []) if rider is None else res


def _compute_call(body, args, *, name, grid, in_specs, out_specs, out_shape, semantics, scratch_shapes=(), rider=None):
    if rider is None:
        return pl.pallas_call(body, name=name, grid=grid, in_specs=in_specs, out_specs=out_specs, out_shape=out_shape,
                              scratch_shapes=list(scratch_shapes), compiler_params=_cparams(*semantics))(*args)
    single = not isinstance(out_shape, (list, tuple))
    o_specs, o_shapes = ([out_specs], [out_shape]) if single else (list(out_specs), list(out_shape))
    n_in, n_out, n_r = len(in_specs), len(o_specs), len(rider.arrays)

    def with_rider(*refs):
        ins, outs = refs[:n_in], refs[n_in + n_r:n_in + n_r + n_out]
        ridden = refs[n_in + n_r + n_out:n_in + 2 * n_r + n_out]
        scratch, (send_sems, recv_sems) = refs[n_in + 2 * n_r + n_out:-2], refs[-2:]
        ids = [pl.program_id(d) for d in range(len(grid))]
        first = functools.reduce(jnp.logical_and, [i == 0 for i in ids])
        last = functools.reduce(jnp.logical_and, [i == g - 1 for i, g in zip(ids, grid)])

        @pl.when(first)
        def _():
            rider.start(ridden, send_sems, recv_sems)

        body(*ins, *outs, *scratch)

        @pl.when(last)
        def _():
            rider.finish(ridden, send_sems, recv_sems)

    any_spec = pl.BlockSpec(memory_space=pl.ANY)
    res = pl.pallas_call(
        with_rider, name=name, grid=grid, in_specs=list(in_specs) + [any_spec] * n_r, out_specs=o_specs + [any_spec] * n_r,
        out_shape=o_shapes + [jax.ShapeDtypeStruct(a.shape, a.dtype) for a in rider.arrays],
        input_output_aliases={n_in + i: n_out + i for i in range(n_r)},
        scratch_shapes=list(scratch_shapes) + [pltpu.SemaphoreType.DMA((rider.n_sems,))] * 2,
        compiler_params=_cparams(*["arbitrary"] * len(grid)),
    )(*args, *rider.arrays)
    return (res[0] if single else list(res[:n_out])), list(res[n_out:])


_NN = (((1,), (0,)), ((), ()))
_NT = (((1,), (1,)), ((), ()))
_TN = (((0,), (0,)), ((), ()))


def _dg(a, b, dims):
    return lax.dot_general(a, b, dims, preferred_element_type=F32)


@jax.custom_vjp
def _bdot_nn(a, b):
    return _dg(a.astype(BF16), b.astype(BF16), _NN)


def _bdot_nn_fwd(a, b):
    a, b = a.astype(BF16), b.astype(BF16)
    return _dg(a, b, _NN), (a, b)


def _bdot_nn_bwd(res, ct):
    a, b = res
    ct = ct.astype(BF16)
    return _dg(ct, b, _NT), _dg(a, ct, _TN)


_bdot_nn.defvjp(_bdot_nn_fwd, _bdot_nn_bwd)


@jax.custom_vjp
def _bdot_nt(a, b):
    return _dg(a.astype(BF16), b.astype(BF16), _NT)


def _bdot_nt_fwd(a, b):
    a, b = a.astype(BF16), b.astype(BF16)
    return _dg(a, b, _NT), (a, b)


def _bdot_nt_bwd(res, ct):
    a, b = res
    ct = ct.astype(BF16)
    return _dg(ct, b, _NN), _dg(ct, a, _TN)


_bdot_nt.defvjp(_bdot_nt_fwd, _bdot_nt_bwd)


def _tile(n, cap):
    if n <= cap:
        return n
    return max(d for d in range(LANES, cap + 1, LANES) if n % d == 0)


def _matmul(a, b, form, name, out_dtype=F32, blocked=False, rider=None, tm=1024, tn=None, tk=4096):
    if form == "nn":
        (m, k), n = a.shape, (N_CHIPS * b.shape[2] if blocked else b.shape[1])
    elif form == "nt":
        m, k, n = a.shape[0], a.shape[1], (b.shape[1] if blocked else b.shape[0])
    else:
        (k, m), n = a.shape, b.shape[1]
    per_chip = (k if form == "nt" else n) // N_CHIPS
    tn = tn or (512 if form == "tn" else 1024)
    if blocked and form == "nt":
        tk = _tile(per_chip, tk)
    elif blocked:
        tn = _tile(per_chip, tn)
    tm, tn, tk = _tile(m, tm), _tile(n, tn), _tile(k, tk)
    assert m % tm == 0 and n % tn == 0 and k % tk == 0, (name, a.shape, b.shape)
    nk = k // tk
    dims = {"nn": _NN, "nt": _NT, "tn": _TN}[form]
    a_spec = pl.BlockSpec((tk, tm), lambda i, j, l: (l, i)) if form == "tn" else pl.BlockSpec((tm, tk), lambda i, j, l: (i, l))
    b_spec = pl.BlockSpec((tn, tk), lambda i, j, l: (j, l)) if form == "nt" else pl.BlockSpec((tk, tn), lambda i, j, l: (l, j))
    o_spec = pl.BlockSpec((tm, tn), lambda i, j, l: (i, j))
    o_shape = (m, n)
    if blocked and form == "nn":
        pc = per_chip // tn
        b_spec = pl.BlockSpec((None, tk, tn), lambda i, j, l: (j // pc, l, j % pc))
    elif blocked and form == "nt":
        pc = per_chip // tk
        b_spec = pl.BlockSpec((None, tn, tk), lambda i, j, l: (l // pc, j, l % pc))
    elif blocked:
        pc = per_chip // tn
        o_spec = pl.BlockSpec((None, tm, tn), lambda i, j, l: (j // pc, i, j % pc))
        o_shape = (N_CHIPS, m, per_chip)

    def body(a_ref, b_ref, o_ref, acc_ref):
        part = _dg(a_ref[...], b_ref[...], dims)
        if nk == 1:
            o_ref[...] = part.astype(out_dtype)
        else:
            l = pl.program_id(2)

            @pl.when(l == 0)
            def _():
                acc_ref[...] = part

            @pl.when(l > 0)
            def _():
                acc_ref[...] += part

            @pl.when(l == nk - 1)
            def _():
                o_ref[...] = acc_ref[...].astype(out_dtype)

    return _compute_call(
        body, (a, b), name=name, grid=(m // tm, n // tn, nk),
        in_specs=[a_spec, b_spec], out_specs=o_spec, out_shape=jax.ShapeDtypeStruct(o_shape, out_dtype),
        scratch_shapes=[pltpu.VMEM((tm, tn) if nk > 1 else (8, LANES), F32)],
        semantics=("parallel", "parallel", "arbitrary"), rider=rider)


def _out_proj_resid(u, w_out, x, gate, name, tm=1024, tn=512):
    (m, k), n = u.shape, w_out.shape[1]

    def body(u_ref, w_ref, x_ref, g_ref, y_ref, xn_ref):
        y = _dg(u_ref[...], w_ref[...], _NN)
        y_ref[...] = y
        xn_ref[...] = x_ref[...] + g_ref[...] * y

    tile = pl.BlockSpec((tm, tn), lambda i, j: (i, j))
    out = jax.ShapeDtypeStruct((m, n), F32)
    return pl.pallas_call(
        body, name=name, grid=(m // tm, n // tn),
        in_specs=[pl.BlockSpec((tm, k), lambda i, j: (i, 0)), pl.BlockSpec((k, tn), lambda i, j: (0, j)), tile,
                  pl.BlockSpec((1, tn), lambda i, j: (0, j))],
        out_specs=[tile, tile], out_shape=[out, out], compiler_params=_cparams("parallel", "parallel"),
    )(u, w_out, x, gate)


TB_NORM = 256


def _f_norm_mod(x, g, shift, scale):
    xn = x * lax.rsqrt(jnp.mean(x * x, axis=-1, keepdims=True) + RMS_EPS)
    return (xn * g) * (1.0 + scale) + shift


def _row_spec(width, tb=TB_NORM):
    return pl.BlockSpec((tb, width), lambda i: (i, 0))


def _vec_spec(width, rows=1):
    return pl.BlockSpec((rows, width), lambda i: (0, 0))


def _norm_mod_fwd(x, g, shift, scale, name):
    t = x.shape[0]

    def body(x_ref, g_ref, sh_ref, sc_ref, h_ref):
        h_ref[...] = _f_norm_mod(x_ref[...], g_ref[...], sh_ref[...], sc_ref[...]).astype(BF16)

    return pl.pallas_call(
        body, name=name, grid=(t // TB_NORM,),
        in_specs=[_row_spec(D), _vec_spec(D), _vec_spec(D), _vec_spec(D)], out_specs=_row_spec(D),
        out_shape=jax.ShapeDtypeStruct((t, D), BF16), compiler_params=_cparams("parallel"),
    )(x, g, shift, scale)


def _accum(ref, val, first):
    @pl.when(first)
    def _():
        ref[...] = val

    @pl.when(jnp.logical_not(first))
    def _():
        ref[...] += val


def _gate_bwd_block(dx, y_ref, gate_ref, dy_ref, dgate_ref, first):
    dy_ref[...] = (dx * gate_ref[...]).astype(BF16)
    _accum(dgate_ref, jnp.sum(dx * y_ref[...], axis=0, keepdims=True), first)


def _gate_bwd_specs():
    return [_row_spec(D), _vec_spec(D)]


def _gate_bwd_shapes(t):
    return [jax.ShapeDtypeStruct((t, D), BF16), jax.ShapeDtypeStruct((1, D), F32)]


def _norm_mod_bwd(x, g, shift, scale, dh, dx_res, name, dh2=None, rider=None, below=None):
    t = x.shape[0]
    dhs = [dh] if dh2 is None else [dh, dh2]
    extra = [] if below is None else list(below)

    def body(x_ref, g_ref, sh_ref, sc_ref, dr_ref, *refs):
        dh_refs, refs = refs[:len(dhs)], refs[len(dhs):]
        below_refs, (dx_ref, dg_ref, dsh_ref, dsc_ref), below_out = refs[:len(extra)], refs[len(extra):len(extra) + 4], refs[len(extra) + 4:]
        _, vjp = jax.vjp(_f_norm_mod, x_ref[...], g_ref[...], sh_ref[...], sc_ref[...])
        dh_all = dh_refs[0][...]
        for r in dh_refs[1:]:
            dh_all = dh_all + r[...]
        dx, dg, dsh, dsc = vjp(dh_all)
        dx = dx + dr_ref[...]
        dx_ref[...] = dx
        first = pl.program_id(0) == 0
        _accum(dg_ref, dg, first)
        _accum(dsh_ref, dsh, first)
        _accum(dsc_ref, dsc, first)
        if below is not None:
            _gate_bwd_block(dx, *below_refs, *below_out, first)

    vec = jax.ShapeDtypeStruct((1, D), F32)
    return _compute_call(
        body, (x, g, shift, scale, dx_res, *dhs, *extra), name=name, grid=(t // TB_NORM,),
        in_specs=[_row_spec(D), _vec_spec(D), _vec_spec(D), _vec_spec(D), _row_spec(D)] + [_row_spec(D)] * len(dhs)
        + (_gate_bwd_specs() if extra else []),
        out_specs=[_row_spec(D), _vec_spec(D), _vec_spec(D), _vec_spec(D)] + (_gate_bwd_specs() if extra else []),
        out_shape=[jax.ShapeDtypeStruct((t, D), F32), vec, vec, vec] + (_gate_bwd_shapes(t) if extra else []),
        semantics=("arbitrary",), rider=rider)


def _f_final(x, g, target):
    xn = x * lax.rsqrt(jnp.mean(x * x, axis=-1, keepdims=True) + RMS_EPS)
    err = xn * g - target
    return 0.5 * jnp.sum(jnp.mean(err * err, axis=-1, keepdims=True), axis=0, keepdims=True)


def _final_loss_grad(x, g, target, y, gate, name):
    t = x.shape[0]

    def body(x_ref, g_ref, t_ref, y_ref, gate_ref, loss_ref, dx_ref, dg_ref, dy_ref, dgate_ref):
        loss, vjp = jax.vjp(_f_final, x_ref[...], g_ref[...], t_ref[...])
        dx, dg, _ = vjp(jnp.ones((1, 1), F32))
        dx_ref[...] = dx
        first = pl.program_id(0) == 0
        _accum(dg_ref, dg, first)
        _accum(loss_ref, jnp.broadcast_to(loss, (1, LANES)), first)
        _gate_bwd_block(dx, y_ref, gate_ref, dy_ref, dgate_ref, first)

    return pl.pallas_call(
        body, name=name, grid=(t // TB_NORM,),
        in_specs=[_row_spec(D), _vec_spec(D), _row_spec(D)] + _gate_bwd_specs(),
        out_specs=[_vec_spec(LANES), _row_spec(D), _vec_spec(D)] + _gate_bwd_specs(),
        out_shape=[jax.ShapeDtypeStruct((1, LANES), F32), jax.ShapeDtypeStruct((t, D), F32), jax.ShapeDtypeStruct((1, D), F32)]
        + _gate_bwd_shapes(t),
        compiler_params=_cparams("arbitrary"),
    )(x, g, target, y, gate)


def _group_selector():
    gi = lax.broadcasted_iota(jnp.int32, (LANES, D), 0)
    ci = lax.broadcasted_iota(jnp.int32, (LANES, D), 1)
    return (ci // (D // SG_GROUPS) == gi).astype(F32)


def _f_sg(p, ln_g, ln_b, w_s, bs_t):
    u, v, z = p[:, :D], p[:, D:2 * D], p[:, 2 * D:]
    u = jax.nn.gelu(u)
    vf = jax.nn.gelu(v)
    mean = jnp.mean(vf, axis=-1, keepdims=True)
    var = jnp.mean(jnp.square(vf - mean), axis=-1, keepdims=True)
    vn = (vf - mean) * lax.rsqrt(var + LN_EPS) * ln_g + ln_b
    ti = lax.broadcasted_iota(jnp.int32, (CHUNK, CHUNK), 0)
    si = lax.broadcasted_iota(jnp.int32, (CHUNK, CHUNK), 1)
    causal = si <= ti
    cg = D // SG_GROUPS
    f = jnp.concatenate(
        [_bdot_nn(jnp.where(causal, w_s[g], 0.0), vn[:, g * cg:(g + 1) * cg]) for g in range(SG_GROUPS)], axis=1)
    f = f + jnp.dot(bs_t, _group_selector(), precision=HIGHEST, preferred_element_type=F32)
    return u * f * jax.nn.silu(z)


def _sg_specs():
    return [pl.BlockSpec((CHUNK, 3 * D), lambda i: (i, 0)), _vec_spec(D), _vec_spec(D),
            pl.BlockSpec((SG_GROUPS, CHUNK, CHUNK), lambda i: (0, 0, 0)), _vec_spec(LANES, CHUNK)]


def _sg_fwd(p, ln_g, ln_b, w_s, bs_t, name, rider=None):
    t = p.shape[0]

    def body(p_ref, lg_ref, lb_ref, w_ref, b_ref, o_ref):
        o_ref[...] = _f_sg(p_ref[...], lg_ref[...], lb_ref[...], w_ref[...], b_ref[...]).astype(BF16)

    return _compute_call(
        body, (p, ln_g, ln_b, w_s, bs_t), name=name, grid=(t // CHUNK,), in_specs=_sg_specs(),
        out_specs=_row_spec(D, CHUNK), out_shape=jax.ShapeDtypeStruct((t, D), BF16), semantics=("parallel",), rider=rider)


def _sg_bwd(p, ln_g, ln_b, w_s, bs_t, dout, name, rider=None):
    t = p.shape[0]

    def body(p_ref, lg_ref, lb_ref, w_ref, b_ref, do_ref, dp_ref, dlg_ref, dlb_ref, dw_ref, db_ref):
        _, vjp = jax.vjp(_f_sg, p_ref[...], lg_ref[...], lb_ref[...], w_ref[...], b_ref[...])
        dp, dlg, dlb, dw, db = vjp(do_ref[...])
        dp_ref[...] = dp.astype(BF16)
        first = pl.program_id(0) == 0
        _accum(dlg_ref, dlg, first)
        _accum(dlb_ref, dlb, first)
        _accum(dw_ref, dw, first)
        _accum(db_ref, db, first)

    vec = jax.ShapeDtypeStruct((1, D), F32)
    return _compute_call(
        body, (p, ln_g, ln_b, w_s, bs_t, dout), name=name, grid=(t // CHUNK,), in_specs=_sg_specs() + [_row_spec(D, CHUNK)],
        out_specs=[pl.BlockSpec((CHUNK, 3 * D), lambda i: (i, 0)), _vec_spec(D), _vec_spec(D),
                   pl.BlockSpec((SG_GROUPS, CHUNK, CHUNK), lambda i: (0, 0, 0)), _vec_spec(LANES, CHUNK)],
        out_shape=[jax.ShapeDtypeStruct((t, 3 * D), BF16), vec, vec,
                   jax.ShapeDtypeStruct((SG_GROUPS, CHUNK, CHUNK), F32), jax.ShapeDtypeStruct((CHUNK, LANES), F32)],
        semantics=("arbitrary",), rider=rider)


SWA_COLS = 2 * D + 2 * KVW
KV_BLOCK = 2 * KVW


def _lane_roll(x, shift):
    return pltpu.roll(x, shift, 1)


def _rot_half(x):
    w = x.shape[1]
    lane = lax.broadcasted_iota(jnp.int32, x.shape, 1)
    return jnp.where(lane % HEAD < HEAD // 2, -_lane_roll(x, w - HEAD // 2), _lane_roll(x, HEAD // 2))


@jax.custom_vjp
def _rope(x, cos, sin):
    return x * cos + _rot_half(x) * sin


def _rope_fwd(x, cos, sin):
    return _rope(x, cos, sin), (cos, sin)


def _rope_bwd(res, ct):
    cos, sin = res
    return ct * cos - _rot_half(ct) * sin, jnp.zeros_like(cos), jnp.zeros_like(sin)


_rope.defvjp(_rope_fwd, _rope_bwd)


@jax.custom_vjp
def _swap_halves(x):
    return _lane_roll(x, HEAD)


_swap_halves.defvjp(lambda x: (_lane_roll(x, HEAD), None), lambda _, ct: (_lane_roll(ct, HEAD),))


def _f_swa(pq, pkv, cos, sin, cosp, sinp, sink_row, valid):
    reps = D // LANES
    q = _rope(pq[:, :D], jnp.tile(cos, (1, reps)), jnp.tile(sin, (1, reps))) * (HEAD ** -0.5)
    k = _rope(pq[:, D:D + KVW], jnp.tile(cos, (1, KVW // LANES)), jnp.tile(sin, (1, KVW // LANES)))
    kp = _rope(pkv[:, :KVW], jnp.tile(cosp, (1, KVW // LANES)), jnp.tile(sinp, (1, KVW // LANES)))
    v, vp, z = pq[:, D + KVW:D + 2 * KVW], pkv[:, KVW:], pq[:, D + 2 * KVW:]
    kcat = jnp.concatenate([kp, k], axis=0)
    vcat = jnp.concatenate([vp, v], axis=0)
    lane = lax.broadcasted_iota(jnp.int32, (2 * CHUNK, LANES), 1)
    lo = lane < HEAD
    hlane = lax.broadcasted_iota(jnp.int32, (1, LANES), 1)

    def halves(cat, g):
        blk = cat[:, (g // 2) * LANES:(g // 2 + 1) * LANES]
        other = _swap_halves(blk)
        if g % 2 == 0:
            return jnp.where(lo, blk, 0.0), jnp.where(lo, 0.0, other)
        return jnp.where(lo, other, 0.0), jnp.where(lo, 0.0, blk)

    pairs = N_HEADS // KV_HEADS // 2
    valid_g = jnp.tile(valid, (pairs, 1))

    def probs(s, heads):
        sink = jnp.concatenate(
            [jnp.broadcast_to(jnp.sum(jnp.where(hlane == h, sink_row, 0.0), axis=1, keepdims=True), (CHUNK, 1))
             for h in heads], axis=0)
        s = jnp.where(valid_g, s, NEG)
        m = lax.stop_gradient(jnp.maximum(jnp.max(s, axis=1, keepdims=True), sink))
        e = jnp.exp(s - m)
        return e / (jnp.sum(e, axis=1, keepdims=True) + jnp.exp(sink - m))

    outs = []
    for g in range(KV_HEADS):
        k_lo, k_hi = halves(kcat, g)
        v_lo, v_hi = halves(vcat, g)
        tiles = range(g * pairs, (g + 1) * pairs)
        qg = jnp.concatenate([q[:, j * LANES:(j + 1) * LANES] for j in tiles], axis=0)
        p_a = probs(_bdot_nt(qg, k_lo), [2 * j for j in tiles])
        p_b = probs(_bdot_nt(qg, k_hi), [2 * j + 1 for j in tiles])
        og = _bdot_nn(p_a, v_lo) + _bdot_nn(p_b, v_hi)
        outs += [og[n * CHUNK:(n + 1) * CHUNK] for n in range(pairs)]
    return jnp.concatenate(outs, axis=1) * jax.nn.silu(z)


def _swa_valid(block):
    qi = lax.broadcasted_iota(jnp.int32, (CHUNK, 2 * CHUNK), 0)
    kj = lax.broadcasted_iota(jnp.int32, (CHUNK, 2 * CHUNK), 1)
    rel = qi + CHUNK - kj
    return (rel >= 0) & (rel < CHUNK) & ((kj >= CHUNK) | (block > 0))


def _swa_specs(blk):
    prev = lambda i: jnp.maximum(blk(i) - 1, 0)
    kv_col = D // KV_BLOCK
    return [pl.BlockSpec((CHUNK, SWA_COLS), lambda i: (blk(i), 0)),
            pl.BlockSpec((CHUNK, KV_BLOCK), lambda i: (prev(i), kv_col)),
            pl.BlockSpec((CHUNK, LANES), lambda i: (blk(i), 0)), pl.BlockSpec((CHUNK, LANES), lambda i: (blk(i), 0)),
            pl.BlockSpec((CHUNK, LANES), lambda i: (prev(i), 0)), pl.BlockSpec((CHUNK, LANES), lambda i: (prev(i), 0)),
            _vec_spec(LANES)]


def _swa_fwd(p, cos, sin, sink_row, name, rider=None):
    t = p.shape[0]

    def body(pq_ref, pkv_ref, c_ref, s_ref, cp_ref, sp_ref, sk_ref, o_ref):
        valid = _swa_valid(pl.program_id(0))
        o_ref[...] = _f_swa(pq_ref[...], pkv_ref[...], c_ref[...], s_ref[...], cp_ref[...], sp_ref[...],
                            sk_ref[...], valid).astype(BF16)

    return _compute_call(
        body, (p, p, cos, sin, cos, sin, sink_row), name=name, grid=(t // CHUNK,), in_specs=_swa_specs(lambda i: i),
        out_specs=_row_spec(D, CHUNK), out_shape=jax.ShapeDtypeStruct((t, D), BF16), semantics=("parallel",), rider=rider)


def _swa_bwd(p, cos, sin, sink_row, dout, name, rider=None):
    t = p.shape[0]
    nb = t // CHUNK
    blk = lambda i: nb - 1 - i

    def body(pq_ref, pkv_ref, c_ref, s_ref, cp_ref, sp_ref, sk_ref, do_ref, dp_ref, dsk_ref, pend_ref):
        i = pl.program_id(0)
        valid = _swa_valid(blk(i))
        f = functools.partial(_f_swa, valid=valid)
        _, vjp = jax.vjp(f, pq_ref[...], pkv_ref[...], c_ref[...], s_ref[...], cp_ref[...], sp_ref[...], sk_ref[...])
        dpq, dpkv, _, _, _, _, dsk = vjp(do_ref[...])

        @pl.when(i == 0)
        def _():
            pend_ref[...] = jnp.zeros_like(pend_ref)

        dp_ref[...] = jnp.concatenate(
            [dpq[:, :D], dpq[:, D:D + KV_BLOCK] + pend_ref[...], dpq[:, D + KV_BLOCK:]], axis=1).astype(BF16)
        pend_ref[...] = dpkv
        _accum(dsk_ref, dsk, i == 0)

    return _compute_call(
        body, (p, p, cos, sin, cos, sin, sink_row, dout), name=name, grid=(nb,),
        in_specs=_swa_specs(blk) + [pl.BlockSpec((CHUNK, D), lambda i: (blk(i), 0))],
        out_specs=[pl.BlockSpec((CHUNK, SWA_COLS), lambda i: (blk(i), 0)), _vec_spec(LANES)],
        out_shape=[jax.ShapeDtypeStruct((t, SWA_COLS), BF16), jax.ShapeDtypeStruct((1, LANES), F32)],
        scratch_shapes=[pltpu.VMEM((CHUNK, KV_BLOCK), F32)], semantics=("arbitrary",), rider=rider)


RW_MAIN = 4 * D
RW_LO = 2 * LORA_PAD
VM = LANES // N_HEADS
VD = HEAD // VM
S_ROWS = VD * HEAD
TB_RW = 128
TB_K = 32
TB_SCAN = 16


def _dim_major(a):
    return a.reshape(a.shape[:-1] + (N_HEADS, HEAD)).swapaxes(-1, -2).reshape(a.shape)


def _head_major(a):
    return a.reshape(a.shape[:-1] + (HEAD, N_HEADS)).swapaxes(-1, -2).reshape(a.shape)


def _param_compact(w):
    return _dim_major(w).reshape(VD, LANES)


def _param_compact_inv(pc):
    return _head_major(pc.reshape(-1))


def _f_rwkv_lora(xs_lo, w0, a0, wl, al):
    decay = jnp.exp(-DECAY_SCALE * jax.nn.sigmoid(w0 + _bdot_nn(jnp.tanh(xs_lo[:, :LORA_PAD]), wl)))
    a = jax.nn.sigmoid(a0 + _bdot_nn(xs_lo[:, LORA_PAD:], al))
    return decay, a


def _prev_rows_spec(width, tb):
    return pl.BlockSpec((8, width), lambda i: (jnp.maximum(i * (tb // 8) - 1, 0), 0))


def _token_shift_lerp(p, prev8, mu, first):
    rows = lax.broadcasted_iota(jnp.int32, p.shape, 0)
    prev = jnp.where(first, 0.0, prev8[7:8, :])
    shifted = jnp.where(rows == 0, prev, pltpu.roll(p, 1, 0))
    return p + (shifted - p) * mu


def _store_compact(ref, val):
    for j in range(VD):
        ref[:, j, :] = val[:, j * LANES:(j + 1) * LANES]


def _load_flat(ref, rows=slice(None)):
    if len(ref.shape) == 2:
        return ref[rows, :]
    return jnp.concatenate([ref[rows, j, :] for j in range(VD)], axis=1)


def _flat_spec(a, tb):
    return _row_spec(a.shape[1], tb) if a.ndim == 2 else _k_spec(VD, tb)


def _rwkv_pre_fwd(p_main, p_lo, mu_main, mu_lo, w0, a0, wl, al, name):
    t = p_main.shape[0]
    tb = TB_RW

    def body(pm_ref, pmp_ref, pl_ref, plp_ref, mm_ref, ml_ref, w0_ref, a0_ref, wl_ref, al_ref,
             r_ref, k_ref, v_ref, dec_ref, a_ref, z_ref, xl_ref):
        first = pl.program_id(0) == 0
        xs = _token_shift_lerp(pm_ref[...], pmp_ref[...], mm_ref[...], first)
        for n, ref in enumerate((r_ref, k_ref, v_ref)):
            _store_compact(ref, xs[:, n * D:(n + 1) * D])
        z_ref[...] = xs[:, 3 * D:]
        xs_lo = _token_shift_lerp(pl_ref[...], plp_ref[...], ml_ref[...], first)
        xl_ref[...] = xs_lo
        decay, a = _f_rwkv_lora(xs_lo, w0_ref[...], a0_ref[...], wl_ref[...], al_ref[...])
        _store_compact(dec_ref, decay)
        _store_compact(a_ref, a)

    cl = jax.ShapeDtypeStruct((t, VD, LANES), F32)
    return pl.pallas_call(
        body, name=name, grid=(t // tb,),
        in_specs=[_row_spec(RW_MAIN, tb), _prev_rows_spec(RW_MAIN, tb), _row_spec(RW_LO, tb), _prev_rows_spec(RW_LO, tb),
                  _vec_spec(RW_MAIN), _vec_spec(RW_LO), _vec_spec(D), _vec_spec(D),
                  _vec_spec(D, LORA_PAD), _vec_spec(D, LORA_PAD)],
        out_specs=[_k_spec(VD, tb)] * 5 + [_row_spec(D, tb), _row_spec(RW_LO, tb)],
        out_shape=[cl] * 5 + [jax.ShapeDtypeStruct((t, D), F32), jax.ShapeDtypeStruct((t, RW_LO), F32)],
        compiler_params=_cparams("parallel"),
    )(p_main, p_main, p_lo, p_lo, mu_main, mu_lo, w0, a0, wl, al)


def _rwkv_lora_bwd(xs_lo, w0, a0, wl, al, ddecay, da, name):
    t = xs_lo.shape[0]
    tb = TB_NORM

    def body(x_ref, w0_ref, a0_ref, wl_ref, al_ref, dd_ref, da_ref, dx_ref, dw0_ref, da0_ref, dwl_ref, dal_ref):
        _, vjp = jax.vjp(_f_rwkv_lora, x_ref[...], w0_ref[...], a0_ref[...], wl_ref[...], al_ref[...])
        dx, dw0, da0, dwl, dal = vjp((_load_flat(dd_ref), _load_flat(da_ref)))
        dx_ref[...] = dx
        first = pl.program_id(0) == 0
        _accum(dw0_ref, dw0, first)
        _accum(da0_ref, da0, first)
        _accum(dwl_ref, dwl, first)
        _accum(dal_ref, dal, first)

    vec = jax.ShapeDtypeStruct((1, D), F32)
    lor = jax.ShapeDtypeStruct((LORA_PAD, D), F32)
    return pl.pallas_call(
        body, name=name, grid=(t // tb,),
        in_specs=[_row_spec(RW_LO), _vec_spec(D), _vec_spec(D), _vec_spec(D, LORA_PAD), _vec_spec(D, LORA_PAD),
                  _k_spec(VD, tb), _k_spec(VD, tb)],
        out_specs=[_row_spec(RW_LO), _vec_spec(D), _vec_spec(D), _vec_spec(D, LORA_PAD), _vec_spec(D, LORA_PAD)],
        out_shape=[jax.ShapeDtypeStruct((t, RW_LO), F32), vec, vec, lor, lor], compiler_params=_cparams("arbitrary"),
    )(xs_lo, w0, a0, wl, al, ddecay, da)


def _lerp_bwd(p, dxs_groups, mu, name):
    t, width = p.shape
    tb = TB_RW
    nb = t // tb
    parts = [a for group in dxs_groups for a in group]

    def body(p_ref, pp_ref, mu_ref, *refs):
        d_refs, (dp_ref, dmu_ref) = refs[:2 * len(parts)], refs[2 * len(parts):]
        i = pl.program_id(0)

        def columns(k):
            pick = (lambda r: _load_flat(r, slice(0, 1))) if k else _load_flat
            vals, at = [], 0
            for group in dxs_groups:
                vals.append(functools.reduce(jnp.add, [pick(d_refs[2 * (at + n) + k]) for n in range(len(group))]))
                at += len(group)
            return jnp.concatenate(vals, axis=1)

        pv, dv, mu_v = p_ref[...], columns(0), mu_ref[...]
        rows = lax.broadcasted_iota(jnp.int32, pv.shape, 0)
        prev = jnp.where(i == 0, 0.0, pp_ref[7:8, :])
        shifted = jnp.where(rows == 0, prev, pltpu.roll(pv, 1, 0))
        nxt = jnp.where(i == nb - 1, 0.0, columns(1))
        d_next = jnp.where(rows == tb - 1, nxt, pltpu.roll(dv, tb - 1, 0))
        dp_ref[...] = (dv * (1.0 - mu_v) + d_next * mu_v).astype(BF16)
        _accum(dmu_ref, jnp.sum(dv * (shifted - pv), axis=0, keepdims=True), i == 0)

    d_specs = []
    for a in parts:
        after = lambda i, nd=a.ndim: (jnp.minimum((i + 1) * (tb // 8), t // 8 - 1),) + (0,) * (nd - 1)
        d_specs += [_flat_spec(a, tb), pl.BlockSpec((8,) + a.shape[1:], after)]
    return pl.pallas_call(
        body, name=name, grid=(nb,),
        in_specs=[_row_spec(width, tb), _prev_rows_spec(width, tb), _vec_spec(width)] + d_specs,
        out_specs=[_row_spec(width, tb), _vec_spec(width)],
        out_shape=[jax.ShapeDtypeStruct((t, width), BF16), jax.ShapeDtypeStruct((1, width), F32)],
        compiler_params=_cparams("arbitrary"),
    )(p, p, mu, *[a for a in parts for _ in range(2)])


def _lane_group_sum2d(x):
    x = x + pltpu.roll(x, N_HEADS, 1)
    return x + pltpu.roll(x, 2 * N_HEADS, 1)


@jax.custom_vjp
def _lane_group_sum(x):
    return _lane_group_sum2d(x.reshape(-1, LANES)).reshape(x.shape)


_lane_group_sum.defvjp(lambda x: (_lane_group_sum(x), None), lambda _, ct: (_lane_group_sum(ct),))


def _head_sum(x):
    return _lane_group_sum(jnp.sum(x, axis=1, keepdims=True))


def _f_kprep(k, a, r, kkp, kap, rkp):
    kk = k * kkp
    kk = kk / jnp.maximum(jnp.sqrt(_head_sum(kk * kk)), 1e-12)
    k2 = k * (1.0 + (a - 1.0) * kap)
    return kk, k2, kk * a, _head_sum(r * k2 * rkp)


def _k_spec(rows=HEAD, tb=TB_K):
    return pl.BlockSpec((tb, rows, LANES), lambda i: (i, 0, 0))


def _kparam_spec(rows=HEAD):
    return pl.BlockSpec((rows, LANES), lambda i: (0, 0))


def _lane_group(shape):
    return lax.broadcasted_iota(jnp.int32, shape, len(shape) - 1) // N_HEADS


def _store_k_layout(ref, xc):
    x2 = xc.reshape(-1, LANES)
    group = _lane_group(x2.shape)
    shifted = [x2] + [pltpu.roll(x2, N_HEADS * k, 1) for k in range(1, VM)]
    for q in range(VM):
        out = shifted[0]
        for k in range(1, VM):
            out = jnp.where(group == (q + k) % VM, shifted[k], out)
        ref[:, pl.ds(q, VD, stride=VM), :] = out.reshape(xc.shape)


def _load_compact(ref):
    shape = (ref.shape[0], VD, LANES)
    rows = [ref[:, pl.ds(q, VD, stride=VM), :].reshape(-1, LANES) for q in range(VM)]
    group = _lane_group(rows[0].shape)
    acc = None
    for k in range(VM):
        t = rows[-k % VM]
        for g in range(1, VM):
            t = jnp.where(group == g, rows[(g - k) % VM], t)
        if k:
            t = pltpu.roll(t, LANES - N_HEADS * k, 1)
        acc = t if acc is None else acc + t
    return acc.reshape(shape)


def _rwkv_kprep_fwd(k, a, r, w, kkp, kap, rkp, name):
    t = k.shape[0]

    def body(k_ref, a_ref, r_ref, w_ref, kkp_ref, kap_ref, rkp_ref, kk_ref, k2_ref, b_ref, r4_ref, w4_ref, rk_ref):
        rv = r_ref[...]
        kk, k2, b, rk_ref[...] = _f_kprep(k_ref[...], a_ref[...], rv, kkp_ref[...], kap_ref[...], rkp_ref[...])
        for ref, val in ((kk_ref, kk), (k2_ref, k2), (b_ref, b), (r4_ref, rv), (w4_ref, w_ref[...])):
            _store_k_layout(ref, val)

    big = jax.ShapeDtypeStruct((t, HEAD, LANES), F32)
    return pl.pallas_call(
        body, name=name, grid=(t // TB_K,),
        in_specs=[_k_spec(VD)] * 4 + [_kparam_spec(VD)] * 3, out_specs=[_k_spec()] * 5 + [_k_spec(1)],
        out_shape=[big] * 5 + [jax.ShapeDtypeStruct((t, 1, LANES), F32)], compiler_params=_cparams("parallel"),
    )(k, a, r, w, kkp, kap, rkp)


def _rwkv_kprep_bwd(k, a, r, kkp, kap, rkp, dkk, dk2, db, drk, dr_scan, dw_scan, name):
    t = k.shape[0]

    def body(k_ref, a_ref, r_ref, kkp_ref, kap_ref, rkp_ref, dkk_ref, dk2_ref, db_ref, drk_ref, drs_ref, dws_ref,
             dk_ref, da_ref, dr_ref, dw_ref, dkkp_ref, dkap_ref, drkp_ref):
        _, vjp = jax.vjp(_f_kprep, k_ref[...], a_ref[...], r_ref[...], kkp_ref[...], kap_ref[...], rkp_ref[...])
        dk, da, dr, dkkp, dkap, drkp = vjp((_load_compact(dkk_ref), _load_compact(dk2_ref), _load_compact(db_ref),
                                            drk_ref[...]))
        dk_ref[...] = dk
        da_ref[...] = da
        dr_ref[...] = dr + _load_compact(drs_ref)
        dw_ref[...] = _load_compact(dws_ref)
        first = pl.program_id(0) == 0
        _accum(dkkp_ref, dkkp, first)
        _accum(dkap_ref, dkap, first)
        _accum(drkp_ref, drkp, first)

    cl = jax.ShapeDtypeStruct((t, VD, LANES), F32)
    par = jax.ShapeDtypeStruct((VD, LANES), F32)
    return pl.pallas_call(
        body, name=name, grid=(t // TB_K,),
        in_specs=[_k_spec(VD)] * 3 + [_kparam_spec(VD)] * 3 + [_k_spec()] * 3 + [_k_spec(1), _k_spec(), _k_spec()],
        out_specs=[_k_spec(VD)] * 4 + [_kparam_spec(VD)] * 3,
        out_shape=[cl] * 4 + [par] * 3, compiler_params=_cparams("arbitrary"),
    )(k, a, r, kkp, kap, rkp, dkk, dk2, db, drk, dr_scan, dw_scan)


def _f_post(y, v, rk, g, b):
    mean = _lane_group_sum(jnp.sum(y, axis=1, keepdims=True)) * (1.0 / HEAD)
    yc = y - mean
    var = _lane_group_sum(jnp.sum(yc * yc, axis=1, keepdims=True)) * (1.0 / HEAD)
    return yc * lax.rsqrt(var + GN_EPS) * g + b + rk * v


def _rwkv_post_fwd(y, v, rk, g, b, name):
    t = y.shape[0]

    def body(y_ref, v_ref, rk_ref, g_ref, b_ref, o_ref):
        o_ref[...] = _f_post(y_ref[...], v_ref[...], rk_ref[...], g_ref[...], b_ref[...])

    return pl.pallas_call(
        body, name=name, grid=(t // TB_K,),
        in_specs=[_k_spec(VD), _k_spec(VD), _k_spec(1), _kparam_spec(VD), _kparam_spec(VD)], out_specs=_k_spec(VD),
        out_shape=jax.ShapeDtypeStruct((t, VD, LANES), F32), compiler_params=_cparams("parallel"),
    )(y, v, rk, g, b)


def _rwkv_post_bwd(y, v, rk, g, b, do, name):
    t = y.shape[0]

    def body(y_ref, v_ref, rk_ref, g_ref, b_ref, do_ref, dy_ref, dv_ref, drk_ref, dg_ref, db_ref):
        _, vjp = jax.vjp(_f_post, y_ref[...], v_ref[...], rk_ref[...], g_ref[...], b_ref[...])
        dy, dv, drk, dg, db = vjp(do_ref[...])
        dy_ref[...] = dy
        dv_ref[...] = dv
        drk_ref[...] = drk
        first = pl.program_id(0) == 0
        _accum(dg_ref, dg, first)
        _accum(db_ref, db, first)

    vl = jax.ShapeDtypeStruct((t, VD, LANES), F32)
    par = jax.ShapeDtypeStruct((VD, LANES), F32)
    return pl.pallas_call(
        body, name=name, grid=(t // TB_K,),
        in_specs=[_k_spec(VD), _k_spec(VD), _k_spec(1), _kparam_spec(VD), _kparam_spec(VD), _k_spec(VD)],
        out_specs=[_k_spec(VD), _k_spec(VD), _k_spec(1), _kparam_spec(VD), _kparam_spec(VD)],
        out_shape=[vl, vl, jax.ShapeDtypeStruct((t, 1, LANES), F32), par, par], compiler_params=_cparams("arbitrary"),
    )(y, v, rk, g, b, do)


def _f_gate(o, z):
    return o * jax.nn.silu(z)


def _rwkv_gate_fwd(o, z, name):
    t = z.shape[0]

    def body(o_ref, z_ref, u_ref):
        u_ref[...] = _f_gate(_load_flat(o_ref), z_ref[...]).astype(BF16)

    return pl.pallas_call(
        body, name=name, grid=(t // TB_NORM,), in_specs=[_k_spec(VD, TB_NORM), _row_spec(D)], out_specs=_row_spec(D),
        out_shape=jax.ShapeDtypeStruct((t, D), BF16), compiler_params=_cparams("parallel"),
    )(o, z)


def _rwkv_gate_bwd(o, z, du, name):
    t = z.shape[0]

    def body(o_ref, z_ref, du_ref, do_ref, dz_ref):
        _, vjp = jax.vjp(_f_gate, _load_flat(o_ref), z_ref[...])
        do, dz_ref[...] = vjp(du_ref[...])
        _store_compact(do_ref, do)

    return pl.pallas_call(
        body, name=name, grid=(t // TB_NORM,), in_specs=[_k_spec(VD, TB_NORM), _row_spec(D), _row_spec(D)],
        out_specs=[_k_spec(VD, TB_NORM), _row_spec(D)],
        out_shape=[jax.ShapeDtypeStruct((t, VD, LANES), F32), jax.ShapeDtypeStruct((t, D), F32)],
        compiler_params=_cparams("parallel"),
    )(o, z, du)


def _colsum(x):
    return jnp.sum(x, axis=0, keepdims=True)


def _rwkv_scan_fwd(r4, w4, k24, kk4, b4, v, name, rider=None):
    t = r4.shape[0]
    tb = TB_SCAN

    def body(r_ref, w_ref, k2_ref, kk_ref, b_ref, v_ref, y_ref, sall_ref, sa_ref, s_scr):
        @pl.when(pl.program_id(0) == 0)
        def _():
            s_scr[...] = jnp.zeros_like(s_scr)

        sall_ref[0] = s_scr[...]

        def step(tt, dst):
            kk = kk_ref[tt]
            sas = []
            for vd in range(VD):
                sa = _colsum(sall_ref[tt, pl.ds(vd * HEAD, HEAD), :] * kk)
                sa_ref[tt, pl.ds(vd, 1), :] = sa
                sas.append(sa)
            w, b, k2, r = w_ref[tt], b_ref[tt], k2_ref[tt], r_ref[tt]
            for vd in range(VD):
                rows = pl.ds(vd * HEAD, HEAD)
                s = sall_ref[tt, rows, :] * w - sas[vd] * b + v_ref[tt, pl.ds(vd, 1), :] * k2
                dst[rows, :] = s
                y_ref[tt, pl.ds(vd, 1), :] = _colsum(s * r)

        def loop_step(tt, carry):
            step(tt, sall_ref.at[tt + 1])
            return carry

        lax.fori_loop(0, tb - 1, loop_step, 0)
        step(tb - 1, s_scr)

    vl = jax.ShapeDtypeStruct((t, VD, LANES), F32)
    return _compute_call(
        body, (r4, w4, k24, kk4, b4, v), name=name, grid=(t // tb,),
        in_specs=[_k_spec(HEAD, tb)] * 5 + [_k_spec(VD, tb)],
        out_specs=[_k_spec(VD, tb), _k_spec(S_ROWS, tb), _k_spec(VD, tb)],
        out_shape=[vl, jax.ShapeDtypeStruct((t, S_ROWS, LANES), F32), vl],
        scratch_shapes=[pltpu.VMEM((S_ROWS, LANES), F32)], semantics=("arbitrary",), rider=rider)


def _rwkv_scan_bwd(dy, s_all, sa_all, r4, w4, k24, kk4, b4, v, name, rider=None):
    t = r4.shape[0]
    tb = TB_SCAN
    nb = t // tb
    blk = lambda i: nb - 1 - i

    def body(dy_ref, sall_ref, sa_ref, r_ref, w_ref, k2_ref, kk_ref, b_ref, v_ref,
             dr_ref, dw_ref, dk2_ref, dkk_ref, db_ref, dv_ref, ds_scr):
        @pl.when(pl.program_id(0) == 0)
        def _():
            ds_scr[...] = jnp.zeros_like(ds_scr)

        def step(j, carry):
            tt = tb - 1 - j
            vrow = lambda ref, vd: ref[tt, pl.ds(vd, 1), :]
            srows = lambda vd: pl.ds(vd * HEAD, HEAD)
            r, k2, b = r_ref[tt], k2_ref[tt], b_ref[tt]
            dsas = []
            for vd in range(VD):
                ds = ds_scr[srows(vd), :] + vrow(dy_ref, vd) * r
                ds_scr[srows(vd), :] = ds
                dv_ref[tt, pl.ds(vd, 1), :] = _colsum(ds * k2)
                dsas.append(-_colsum(ds * b))
            zero = jnp.zeros((HEAD, LANES), F32)
            dk2, q, sady, vdy = zero, zero, 0.0, 0.0
            for vd in range(VD):
                dyv = vrow(dy_ref, vd)
                dk2 = dk2 + ds_scr[srows(vd), :] * vrow(v_ref, vd)
                q = q + sall_ref[tt, srows(vd), :] * dyv
                sady = sady + vrow(sa_ref, vd) * dyv
                vdy = vdy + vrow(v_ref, vd) * dyv
            dk2_ref[tt] = dk2
            dr_ref[tt] = w_ref[tt] * q - b_ref[tt] * sady + k2_ref[tt] * vdy
            dw, dkk = zero, zero
            for vd in range(VD):
                sp = sall_ref[tt, srows(vd), :]
                dw = dw + ds_scr[srows(vd), :] * sp
                dkk = dkk + sp * dsas[vd]
            dw_ref[tt] = dw
            dkk_ref[tt] = dkk
            w, kk = w_ref[tt], kk_ref[tt]
            db = zero
            for vd in range(VD):
                ds = ds_scr[srows(vd), :]
                db = db - ds * vrow(sa_ref, vd)
                ds_scr[srows(vd), :] = ds * w + dsas[vd] * kk
            db_ref[tt] = db
            return carry

        lax.fori_loop(0, tb, step, 0)

    rk = lambda rows: pl.BlockSpec((tb, rows, LANES), lambda i: (blk(i), 0, 0))
    big = jax.ShapeDtypeStruct((t, HEAD, LANES), F32)
    return _compute_call(
        body, (dy, s_all, sa_all, r4, w4, k24, kk4, b4, v), name=name, grid=(nb,),
        in_specs=[rk(VD), rk(S_ROWS), rk(VD)] + [rk(HEAD)] * 5 + [rk(VD)],
        out_specs=[rk(HEAD)] * 5 + [rk(VD)],
        out_shape=[big] * 5 + [jax.ShapeDtypeStruct((t, VD, LANES), F32)],
        scratch_shapes=[pltpu.VMEM((S_ROWS, LANES), F32)], semantics=("arbitrary",), rider=rider)


def _rwkv_mixer_fwd(p_main, p_lo, prm, tag, rider):
    r, k, v, w, a, z, xs_lo = _rwkv_pre_fwd(p_main, p_lo, prm["mu_main"], prm["mu_lo"], prm["w0"], prm["a0"],
                                            prm["wl"], prm["al"], tag + "_pre")
    kk4, k24, b4, r4, w4, rk = _rwkv_kprep_fwd(k, a, r, w, prm["kkp"], prm["kap"], prm["rkp"], tag + "_kprep")
    (y, s_all, sa_all), ridden = _ridden(_rwkv_scan_fwd(r4, w4, k24, kk4, b4, v, tag + "_scan", rider), rider)
    o = _rwkv_post_fwd(y, v, rk, prm["gn_g"], prm["gn_b"], tag + "_post")
    u = _rwkv_gate_fwd(o, z, tag + "_gate")
    saved = dict(z=z, xs_lo=xs_lo, r=r, k=k, a=a, v=v, r4=r4, w4=w4, kk4=kk4, k24=k24, b4=b4, rk=rk,
                 y=y, s_all=s_all, sa_all=sa_all, o=o)
    return u, saved, ridden


def _rwkv_mixer_bwd(p_main, p_lo, prm, sv, du, tag, rider):
    do, dz = _rwkv_gate_bwd(sv["o"], sv["z"], du, tag + "_gate_b")
    dy, dv_post, drk, dgn_g, dgn_b = _rwkv_post_bwd(sv["y"], sv["v"], sv["rk"], prm["gn_g"], prm["gn_b"], do,
                                                    tag + "_post_b")
    (dr_s, dw_s, dk24, dkk4, db4, dv_scan), ridden = _ridden(_rwkv_scan_bwd(
        dy, sv["s_all"], sv["sa_all"], sv["r4"], sv["w4"], sv["k24"], sv["kk4"], sv["b4"], sv["v"], tag + "_scan_b", rider), rider)
    dk, da, dr, dw, dkkp, dkap, drkp = _rwkv_kprep_bwd(sv["k"], sv["a"], sv["r"], prm["kkp"], prm["kap"], prm["rkp"],
                                                       dkk4, dk24, db4, drk, dr_s, dw_s, tag + "_kprep_b")
    dxs_lo, dw0, da0, dwl, dal = _rwkv_lora_bwd(sv["xs_lo"], prm["w0"], prm["a0"], prm["wl"], prm["al"], dw, da,
                                                tag + "_lora_b")
    dxs_main = [[dr], [dk], [dv_post, dv_scan], [dz]]
    dp_main, dmu_main = _lerp_bwd(p_main, dxs_main, prm["mu_main"], tag + "_lerp_main_b")
    dp_lo, dmu_lo = _lerp_bwd(p_lo, [[dxs_lo]], prm["mu_lo"], tag + "_lerp_lo_b")
    grads = dict(mu_main=dmu_main, mu_lo=dmu_lo, w0=dw0, a0=da0, wl=dwl, al=dal, kkp=dkkp, kap=dkap, rkp=drkp,
                 gn_g=dgn_g, gn_b=dgn_b)
    return dp_main, dp_lo, grads, ridden


N_DEV = 8
N_CHIPS = 4
ANY = pl.BlockSpec(memory_space=pl.ANY)


def _place():
    return lax.axis_index("x"), lax.axis_index("y"), lax.axis_index("c")


def _remote(src, dst, send_sems, recv_sems, k, dev):
    return pltpu.make_async_remote_copy(src_ref=src, dst_ref=dst, send_sem=send_sems.at[k], recv_sem=recv_sems.at[k],
                                        device_id=dev, device_id_type=MESHT)


def _all_gather8(v, name):
    def body(buf_ref, out_ref, send_sems, recv_sems):
        del buf_ref
        x, y, c = _place()
        mine = out_ref.at[4 * x + 2 * y + c]
        peers = [(x ^ (k >> 2), y ^ ((k >> 1) & 1), c ^ (k & 1)) for k in range(1, N_DEV)]
        sends = [_remote(mine, mine, send_sems, recv_sems, k, peer) for k, peer in enumerate(peers)]
        for cp in sends:
            cp.start()
        for k, (px, py, pc) in enumerate(peers):
            _remote(mine, out_ref.at[4 * px + 2 * py + pc], send_sems, recv_sems, k, (x, y, c)).wait_recv()
        for cp in sends:
            cp.wait_send()

    return pl.pallas_call(
        body, name=name, in_specs=[ANY], out_specs=ANY, input_output_aliases={0: 0},
        out_shape=jax.ShapeDtypeStruct((N_DEV,) + v.shape, v.dtype),
        scratch_shapes=[pltpu.SemaphoreType.DMA((N_DEV - 1,)), pltpu.SemaphoreType.DMA((N_DEV - 1,))],
    )(jnp.broadcast_to(v[None], (N_DEV,) + v.shape))


def _other_chips(x, y):
    return [(1 - x, y), (x, 1 - y), (1 - x, 1 - y)]


GATHER_SEMS = 6


def _gather_buffer(v):
    return jnp.broadcast_to(v[None], (N_CHIPS,) + v.shape)


def _gather_start(bufs, send_sems, recv_sems):
    x, y, c = _place()
    for i, buf in enumerate(bufs):
        mine = buf.at[2 * x + y, c]
        for j, (cx, cy) in enumerate(_other_chips(x, y)):
            _remote(mine, mine, send_sems, recv_sems, GATHER_SEMS * i + j, (cx, cy, c)).start()


def _gather_finish(bufs, send_sems, recv_sems):
    x, y, c = _place()
    chips = _other_chips(x, y)
    passed = []
    for i, buf in enumerate(bufs):
        mine = buf.at[2 * x + y, c]
        for j, (cx, cy) in enumerate(chips):
            landed = buf.at[2 * cx + cy, c]
            _remote(mine, landed, send_sems, recv_sems, GATHER_SEMS * i + j, (x, y, c)).wait_recv()
            fwd = _remote(landed, landed, send_sems, recv_sems, GATHER_SEMS * i + 3 + j, (x, y, 1 - c))
            fwd.start()
            passed.append(fwd)
    for i, buf in enumerate(bufs):
        mine = buf.at[2 * x + y, c]
        for j, (cx, cy) in enumerate(chips):
            _remote(mine, buf.at[2 * cx + cy, 1 - c], send_sems, recv_sems, GATHER_SEMS * i + 3 + j, (x, y, c)).wait_recv()
            _remote(mine, mine, send_sems, recv_sems, GATHER_SEMS * i + j, (cx, cy, c)).wait_send()
    for fwd in passed:
        fwd.wait_send()


def _gather_rider(bufs):
    return _Rider(bufs, GATHER_SEMS * len(bufs), _gather_start, _gather_finish)


def _chip_gather(bufs, name):
    n = len(bufs)

    def body(*refs):
        out_refs, (send_sems, recv_sems) = refs[n:2 * n], refs[2 * n:]
        _gather_start(out_refs, send_sems, recv_sems)
        _gather_finish(out_refs, send_sems, recv_sems)

    return pl.pallas_call(
        body, name=name, in_specs=[ANY] * n, out_specs=[ANY] * n, input_output_aliases={i: i for i in range(n)},
        out_shape=[jax.ShapeDtypeStruct(b.shape, b.dtype) for b in bufs], scratch_shapes=_dma_sems(GATHER_SEMS * n),
    )(*bufs)


RS_W = 1024
RS_BLOCK_BYTES = 4 << 20


def _dma_sems(n):
    return [pltpu.SemaphoreType.DMA((n,)), pltpu.SemaphoreType.DMA((n,))]


def _pair_exchange_copies(refs, send_sems, recv_sems):
    n = len(refs) // 2
    x, y, c = _place()
    return [_remote(refs[i].at[s, 1 - c], refs[n + i].at[s], send_sems, recv_sems, N_CHIPS * i + s, (x, y, 1 - c))
            for i in range(n) for s in range(N_CHIPS)]


def _pair_exchange_start(refs, send_sems, recv_sems):
    for cp in _pair_exchange_copies(refs, send_sems, recv_sems):
        cp.start()


def _pair_exchange_finish(refs, send_sems, recv_sems):
    for cp in _pair_exchange_copies(refs, send_sems, recv_sems):
        cp.wait()


def _pair_exchange_rider(gs):
    landing = [lax.empty((N_CHIPS,) + g.shape[2:], g.dtype) for g in gs]
    return _Rider(list(gs) + landing, N_CHIPS * len(gs), _pair_exchange_start, _pair_exchange_finish)


def _rs_rows(rows, cols):
    cap = max(16, RS_BLOCK_BYTES // (N_CHIPS * 4 * cols))
    return rows if rows <= cap else max(d for d in range(16, cap + 1, 16) if rows % d == 0)


def _rs_pair_add(g, got, c_arr, name):
    _, _, rows, width = g.shape
    tr = _rs_rows(rows, width)

    def body(c_ref, g_ref, got_ref, p_ref):
        p_ref[...] = (g_ref[...] + got_ref[...]).astype(BF16)

    return pl.pallas_call(
        body, name=name,
        grid_spec=pltpu.PrefetchScalarGridSpec(
            num_scalar_prefetch=1, grid=(rows // tr,),
            in_specs=[pl.BlockSpec((N_CHIPS, None, tr, width), lambda i, c_ref: (0, c_ref[0], i, 0)),
                      pl.BlockSpec((N_CHIPS, tr, width), lambda i, c_ref: (0, i, 0))],
            out_specs=pl.BlockSpec((N_CHIPS, tr, width), lambda i, c_ref: (0, i, 0))),
        out_shape=jax.ShapeDtypeStruct((N_CHIPS, rows, width), BF16), compiler_params=_cparams("parallel"),
    )(c_arr, g, got)


def _chip_exchange_copies(refs, send_sems, recv_sems):
    n = len(refs) // 2
    x, y, c = _place()
    return [_remote(refs[i].at[2 * cx + cy], refs[n + i].at[j], send_sems, recv_sems, 3 * i + j, (cx, cy, c))
            for i in range(n) for j, (cx, cy) in enumerate(_other_chips(x, y))]


def _chip_exchange_start(refs, send_sems, recv_sems):
    for cp in _chip_exchange_copies(refs, send_sems, recv_sems):
        cp.start()


def _chip_exchange_finish(refs, send_sems, recv_sems):
    n = len(refs) // 2
    x, y, c = _place()
    for i in range(n):
        for j in range(3):
            _remote(refs[i].at[2 * x + y], refs[n + i].at[j], send_sems, recv_sems, 3 * i + j, (x, y, c)).wait_recv()
    for cp in _chip_exchange_copies(refs, send_sems, recv_sems):
        cp.wait_send()


def _chip_exchange_buffers(ps):
    return [lax.empty((3,) + p.shape[1:], p.dtype) for p in ps]


def _chip_exchange_rider(ps):
    return _Rider(list(ps) + _chip_exchange_buffers(ps), 3 * len(ps), _chip_exchange_start, _chip_exchange_finish)


def _rs_chip_add(p, q, idx, name):
    _, rows, width = q.shape
    tr = _rs_rows(rows, width)

    def body(idx_ref, p_ref, q_ref, r_ref):
        qv = q_ref[...].astype(F32)
        r_ref[...] = ((p_ref[...].astype(F32) + qv[0]) + qv[1]) + qv[2]

    return pl.pallas_call(
        body, name=name,
        grid_spec=pltpu.PrefetchScalarGridSpec(
            num_scalar_prefetch=1, grid=(rows // tr,),
            in_specs=[pl.BlockSpec((None, tr, width), lambda i, idx_ref: (idx_ref[0], i, 0)),
                      pl.BlockSpec((3, tr, width), lambda i, idx_ref: (0, i, 0))],
            out_specs=pl.BlockSpec((None, tr, width), lambda i, idx_ref: (idx_ref[1], i, 0))),
        out_shape=jax.ShapeDtypeStruct((2, rows, width), F32), compiler_params=_cparams("parallel"),
    )(idx, p, q)


def _rs_pair_share(rs, name):
    n = len(rs)

    def body(*refs):
        out_refs, (send_sems, recv_sems) = refs[n:2 * n], refs[2 * n:]
        x, y, c = _place()
        sends = [_remote(out_refs[i].at[c], out_refs[i].at[c], send_sems, recv_sems, i, (x, y, 1 - c)) for i in range(n)]
        for cp in sends:
            cp.start()
        for i in range(n):
            _remote(out_refs[i].at[c], out_refs[i].at[1 - c], send_sems, recv_sems, i, (x, y, c)).wait_recv()
        for cp in sends:
            cp.wait_send()

    return pl.pallas_call(
        body, name=name, in_specs=[ANY] * n, out_specs=[ANY] * n, input_output_aliases={i: i for i in range(n)},
        out_shape=[jax.ShapeDtypeStruct(r.shape, r.dtype) for r in rs], scratch_shapes=_dma_sems(n),
    )(*rs)


def _rs_pair_sums(gs, gots, core, tag):
    c_arr = core.astype(jnp.int32).reshape(1)
    return [_rs_pair_add(g, got, c_arr, f"{tag}_pair_add{i}") for i, (g, got) in enumerate(zip(gs, gots))]


def _rs_finish(ps, qs, chip, core, tag):
    idx = jnp.stack([chip, core]).astype(jnp.int32)
    rs = [_rs_chip_add(p, q, idx, f"{tag}_chip_add{i}") for i, (p, q) in enumerate(zip(ps, qs))]
    return _rs_pair_share(rs, tag + "_share")


def _sum_leading(a, name):
    n, rows, width = a.shape
    cap = max(8, RS_BLOCK_BYTES // (n * 4 * width))
    tr = rows if rows <= cap else max(d for d in range(8, cap + 1, 8) if rows % d == 0)

    def body(a_ref, o_ref):
        acc = a_ref[0]
        for d in range(1, n):
            acc = acc + a_ref[d]
        o_ref[...] = acc

    return pl.pallas_call(
        body, name=name, grid=(rows // tr,), in_specs=[pl.BlockSpec((n, tr, width), lambda i: (0, i, 0))],
        out_specs=pl.BlockSpec((tr, width), lambda i: (i, 0)), out_shape=jax.ShapeDtypeStruct((rows, width), F32),
        compiler_params=_cparams("parallel"),
    )(a)


def _pair_swap(v, name):
    def body(v_ref, got_ref, send_sems, recv_sems):
        x, y, c = _place()
        cp = _remote(v_ref, got_ref, send_sems, recv_sems, 0, (x, y, 1 - c))
        cp.start()
        cp.wait()

    return pl.pallas_call(body, name=name, in_specs=[ANY], out_specs=ANY, out_shape=jax.ShapeDtypeStruct(v.shape, v.dtype),
                          scratch_shapes=_dma_sems(1))(v)


def _replicated_pair_sum(v, name):
    rows, width = v.shape
    pair = _sum_leading(jnp.stack([v, _pair_swap(v, name + "_swap")]), name + "_pair_add")
    return _gather_buffer(pair.reshape(2, rows // 2, width))


def _replicated_chip_sum(gathered, name):
    return _sum_leading(gathered.reshape(N_CHIPS, -1, gathered.shape[-1]), name + "_chip_add")


MOD_COLS = 3 * D // N_CHIPS
MOD_TK = 512


def _mod_partial(c_all, mod_w, name):
    nk = D // MOD_TK

    def body(c_ref, w_ref, o_ref):
        l = pl.program_id(1)
        part = _bdot_nn(jax.nn.silu(c_ref[...]), w_ref[0])
        _accum(o_ref.at[0], part, l == 0)

    return pl.pallas_call(
        body, name=name, grid=(DEPTH, nk),
        in_specs=[pl.BlockSpec((N_DEV, MOD_TK), lambda i, l: (0, l)), pl.BlockSpec((1, MOD_TK, MOD_COLS), lambda i, l: (i, l, 0))],
        out_specs=pl.BlockSpec((1, N_DEV, MOD_COLS), lambda i, l: (i, 0, 0)),
        out_shape=jax.ShapeDtypeStruct((DEPTH, N_DEV, MOD_COLS), F32), compiler_params=_cparams("parallel", "arbitrary"),
    )(c_all, mod_w)


def _mod_w_grad(c_all, dmod, name):
    def body(c_ref, d_ref, o_ref):
        o_ref[0] = _dg(jax.nn.silu(c_ref[...]).astype(BF16), d_ref[0].astype(BF16), _TN)

    return pl.pallas_call(
        body, name=name, grid=(DEPTH, D // MOD_TK),
        in_specs=[pl.BlockSpec((N_DEV, MOD_TK), lambda i, l: (0, l)), pl.BlockSpec((1, N_DEV, MOD_COLS), lambda i, l: (i, 0, 0))],
        out_specs=pl.BlockSpec((1, MOD_TK, MOD_COLS), lambda i, l: (i, l, 0)),
        out_shape=jax.ShapeDtypeStruct((DEPTH, D, MOD_COLS), F32), compiler_params=_cparams("parallel", "parallel"),
    )(c_all, dmod)


ADAM_BLOCK_BYTES = 1 << 20


def _adamw(w, g, m, v, name, rider=None):
    shape = w.shape
    cols = shape[-1]
    rows = w.size // cols
    w, g, m, v = (a.reshape(rows, cols) for a in (w, g, m, v))
    cap = max(8, ADAM_BLOCK_BYTES // (4 * cols))
    tr = rows if rows <= cap else max(d for d in range(8, cap + 1, 8) if rows % d == 0)
    c1 = 1.0 - ADAM_B1 ** ADAM_STEP
    c2 = 1.0 - ADAM_B2 ** ADAM_STEP

    def body(w_ref, g_ref, m_ref, v_ref, d_ref, nm_ref, nv_ref):
        gv = g_ref[...]
        mn = ADAM_B1 * m_ref[...] + (1.0 - ADAM_B1) * gv
        vn = ADAM_B2 * v_ref[...] + (1.0 - ADAM_B2) * (gv * gv)
        nm_ref[...] = mn
        nv_ref[...] = vn
        d_ref[...] = -ADAM_LR * ((mn / c1) / (jnp.sqrt(vn / c2) + ADAM_EPS) + ADAM_WD * w_ref[...])

    spec = pl.BlockSpec((tr, cols), lambda i: (i, 0))
    out = jax.ShapeDtypeStruct((rows, cols), F32)
    (d, nm, nv), ridden = _ridden(_compute_call(
        body, (w, g, m, v), name=name, grid=(rows // tr,), in_specs=[spec] * 4, out_specs=[spec] * 3, out_shape=[out] * 3,
        semantics=("parallel",), rider=rider), rider)
    res = (d.reshape(shape), nm.reshape(shape), nv.reshape(shape))
    return res if rider is None else (res, ridden)


W_NAMES = ("norm_g", "mod_w", "mod_b", "final_norm_g", "sg_w_in", "sg_w_out", "sg_ln_g", "sg_ln_b", "sg_w_spatial",
           "sg_b_spatial", "swa_w_in", "swa_w_out", "swa_sinks", "rwkv_w_in", "rwkv_w_out", "rwkv_mu", "rwkv_w0",
           "rwkv_w_lora", "rwkv_a0", "rwkv_a_lora", "rwkv_k_k", "rwkv_k_a", "rwkv_r_k", "rwkv_gn_g", "rwkv_gn_b")
SMALL = {"sg_ln_g": 1, "sg_ln_b": 1, "rwkv_mu": 1, "rwkv_w0": 1, "rwkv_w_lora": 2, "rwkv_a0": 1, "rwkv_a_lora": 2,
         "rwkv_k_k": 1, "rwkv_k_a": 1, "rwkv_gn_g": 1, "rwkv_gn_b": 1}
REPLICATED = ("norm_g", "final_norm_g", "sg_w_spatial", "sg_b_spatial", "swa_sinks", "rwkv_r_k")
KINDS = ("sg", "swa", "rwkv", "sg")


def _pad_to(flat, n):
    return jnp.pad(flat, (0, n - flat.shape[0]))


def _round_up(n, m):
    return -(-n // m) * m


def _join_shards(gathered, axis):
    return jnp.concatenate([gathered[s] for s in range(N_CHIPS)], axis=axis)


def _chip_blocks(full, axis):
    return jnp.stack(jnp.split(full, N_CHIPS, axis=axis)).reshape(N_CHIPS, -1)


def _weight_buffer(w):
    rows, cols = w.shape
    return _gather_buffer(w.astype(BF16).reshape(2, rows // 2, cols))


def _chip_shards(buf):
    return buf.reshape(N_CHIPS, -1, buf.shape[-1])


def _small_buffer(shards):
    flat = jnp.concatenate([shards[n].reshape(-1) for n in SMALL])
    rows = _round_up(flat.shape[0], 2 * 8 * LANES) // (2 * LANES)
    return _gather_buffer(_pad_to(flat, 2 * rows * LANES).reshape(2, rows, LANES))


def _unpack_small(buf, shards):
    got = buf.reshape(N_CHIPS, -1)
    out, off = {}, 0
    for n, axis in SMALL.items():
        size = shards[n].size
        out[n] = _join_shards(got[:, off:off + size].reshape((N_CHIPS,) + shards[n].shape), axis)
        off += size
    return out


def _lora_pad_rows(w):
    return jnp.pad(w, ((0, LORA_PAD - LORA), (0, 0)))


def _lo_cols(a):
    z = jnp.zeros(a.shape[:-1] + (LORA_PAD - LORA,), a.dtype)
    return jnp.concatenate([a[..., :LORA], z, a[..., LORA:], z], axis=-1)


def _lo_cols_inv(a):
    return jnp.concatenate([a[..., :LORA], a[..., LORA_PAD:LORA_PAD + LORA]], axis=-1)


def _transposed(w_blocks):
    return jnp.swapaxes(w_blocks, 1, 2).reshape(-1, w_blocks.shape[1])


def _rows_dim_major(w):
    return w.reshape(N_HEADS, HEAD, -1).swapaxes(0, 1).reshape(w.shape)


def _rows_head_major(w):
    return w.reshape(HEAD, N_HEADS, -1).swapaxes(0, 1).reshape(w.shape)


def kernel(x, c, positions, norm_g, mod_w, mod_b, final_norm_g, sg_w_in, sg_w_out, sg_ln_g, sg_ln_b, sg_w_spatial,
           sg_b_spatial, swa_w_in, swa_w_out, swa_sinks, rwkv_w_in, rwkv_w_out, rwkv_mu, rwkv_w0, rwkv_w_lora, rwkv_a0,
           rwkv_a_lora, rwkv_k_k, rwkv_k_a, rwkv_r_k, rwkv_gn_g, rwkv_gn_b, loss_target, m_norm_g, m_mod_w, m_mod_b,
           m_final_norm_g, m_sg_w_in, m_sg_w_out, m_sg_ln_g, m_sg_ln_b, m_sg_w_spatial, m_sg_b_spatial, m_swa_w_in,
           m_swa_w_out, m_swa_sinks, m_rwkv_w_in, m_rwkv_w_out, m_rwkv_mu, m_rwkv_w0, m_rwkv_w_lora, m_rwkv_a0,
           m_rwkv_a_lora, m_rwkv_k_k, m_rwkv_k_a, m_rwkv_r_k, m_rwkv_gn_g, m_rwkv_gn_b, v_norm_g, v_mod_w, v_mod_b,
           v_final_norm_g, v_sg_w_in, v_sg_w_out, v_sg_ln_g, v_sg_ln_b, v_sg_w_spatial, v_sg_b_spatial, v_swa_w_in,
           v_swa_w_out, v_swa_sinks, v_rwkv_w_in, v_rwkv_w_out, v_rwkv_mu, v_rwkv_w0, v_rwkv_w_lora, v_rwkv_a0,
           v_rwkv_a_lora, v_rwkv_k_k, v_rwkv_k_a, v_rwkv_r_k, v_rwkv_gn_g, v_rwkv_gn_b):
    given = dict(locals())
    w = {n: given[n] for n in W_NAMES}
    xi, yi, ci = _place()
    chip = 2 * xi + yi
    me = 4 * xi + 2 * yi + ci
    xs = [x[0]]

    c_all = _all_gather8(c, "gather_c")[:, 0, :]
    mod_part = _mod_partial(c_all, mod_w, "mod_fwd")
    mod_all = _all_gather8(mod_part, "gather_mod")[::2]
    mod_mine = lax.dynamic_index_in_dim(mod_all, me, axis=2, keepdims=False)
    mod = mod_mine.transpose(1, 0, 2).reshape(DEPTH, 3 * D) + mod_b
    shift, scale, gate = mod[:, :D], mod[:, D:2 * D], mod[:, 2 * D:]

    shards = {"sg_w_in0": sg_w_in[0], "sg_w_out0": sg_w_out[0], "swa_w_in": swa_w_in[0], "swa_w_out": swa_w_out[0],
              "rwkv_w_in": rwkv_w_in[0], "rwkv_w_out": rwkv_w_out[0], "sg_w_in1": sg_w_in[1], "sg_w_out1": sg_w_out[1]}
    bufs = {n: _weight_buffer(s) for n, s in shards.items()}
    fwd_riders = {(0, "in"): ["swa_w_in"], (0, "mix"): ["swa_w_out"], (1, "in"): ["rwkv_w_out"], (1, "mix"): ["rwkv_w_in"],
                  (2, "mix"): ["sg_w_in1", "sg_w_out1"]}
    bufs["sg_w_in0"], bufs["sg_w_out0"], small_buf = _chip_gather(
        [bufs["sg_w_in0"], bufs["sg_w_out0"], _small_buffer(w)], "gather_l0")
    full = _unpack_small(small_buf, w)

    def riding(i, where):
        names = fwd_riders.get((i, where))
        return names, (None if names is None else _gather_rider([bufs[n] for n in names]))

    def arrived(names, ridden):
        for n, b in zip(names or [], ridden):
            bufs[n] = b

    sg_in = lambda j: _chip_shards(bufs[f"sg_w_in{j}"])
    sg_out = lambda j: bufs[f"sg_w_out{j}"].reshape(D, D)
    mu = full["rwkv_mu"][0]
    rw_prm = dict(mu_main=_dim_major(mu[:RW_MAIN].reshape(4, D)).reshape(1, RW_MAIN), mu_lo=_lo_cols(mu[None, RW_MAIN:]),
                  w0=_dim_major(full["rwkv_w0"]), a0=_dim_major(full["rwkv_a0"]),
                  wl=_lora_pad_rows(_dim_major(full["rwkv_w_lora"][0])), al=_lora_pad_rows(_dim_major(full["rwkv_a_lora"][0])),
                  kkp=_param_compact(full["rwkv_k_k"][0]), kap=_param_compact(full["rwkv_k_a"][0]),
                  rkp=_param_compact(rwkv_r_k.reshape(-1)),
                  gn_g=_param_compact(full["rwkv_gn_g"][0]), gn_b=_param_compact(full["rwkv_gn_b"][0]))
    bs_t = [jnp.pad(sg_b_spatial[j].T, ((0, 0), (0, LANES - SG_GROUPS))) for j in range(2)]
    sink_row = jnp.pad(swa_sinks, ((0, 0), (0, LANES - N_HEADS)))
    inv_freq = ROPE_THETA ** (-jnp.arange(HEAD // 2, dtype=F32) / (HEAD // 2))
    ang = positions[0].astype(F32)[:, None] * inv_freq
    cos, sin = jnp.tile(jnp.cos(ang), (1, LANES * 2 // HEAD)), jnp.tile(jnp.sin(ang), (1, LANES * 2 // HEAD))

    def row(a, i):
        return a[i:i + 1]

    hs, ps, us, ys, rw_saved = [], [], [], [], None
    for i, kind in enumerate(KINDS):
        j = i // 3
        tag = f"l{i}_{kind}"
        h = _norm_mod_fwd(xs[i], row(norm_g, i), row(shift, i), row(scale, i), tag + "_norm")
        names_in, rider_in = riding(i, "in")
        names_mix, rider_mix = riding(i, "mix")
        if kind == "sg":
            p, ridden = _ridden(_matmul(h, sg_in(j), "nn", tag + "_in", blocked=True, rider=rider_in), rider_in)
            arrived(names_in, ridden)
            u, ridden = _ridden(_sg_fwd(p, row(full["sg_ln_g"], j), row(full["sg_ln_b"], j), sg_w_spatial[j], bs_t[j],
                                        tag + "_mix", rider_mix), rider_mix)
            w_out = sg_out(j)
        elif kind == "swa":
            swa_in, swa_out = _chip_shards(bufs["swa_w_in"]), bufs["swa_w_out"].reshape(D, D)
            p, ridden = _ridden(_matmul(h, swa_in, "nn", tag + "_in", blocked=True, rider=rider_in), rider_in)
            arrived(names_in, ridden)
            u, ridden = _ridden(_swa_fwd(p, cos, sin, sink_row, tag + "_mix", rider_mix), rider_mix)
            w_out = swa_out
        else:
            rw_in = _join_shards(_chip_shards(bufs["rwkv_w_in"]), axis=1)
            rw_main = _dim_major(rw_in[:, :RW_MAIN].reshape(D, 4, D)).reshape(D, RW_MAIN)
            rw_lo = _lo_cols(rw_in[:, RW_MAIN:])
            rw_out = _rows_dim_major(bufs["rwkv_w_out"].reshape(D, D))
            p = (_matmul(h, rw_main, "nn", tag + "_in"), _matmul(h, rw_lo, "nn", tag + "_in_lo"))
            u, rw_saved, ridden = _rwkv_mixer_fwd(p[0], p[1], rw_prm, tag, rider_mix)
            w_out = rw_out
        arrived(names_mix, ridden)
        y, x_next = _out_proj_resid(u, w_out, xs[i], row(gate, i), tag + "_out")
        xs.append(x_next)
        hs.append(h), ps.append(p), us.append(u), ys.append(y)

    loss_part, dx, d_final_g, dy, d_gate = _final_loss_grad(xs[DEPTH], final_norm_g[None], loss_target[0], ys[DEPTH - 1],
                                                            row(gate, DEPTH - 1), "loss")
    loss = lax.psum(loss_part[0, 0], ("x", "y", "c"))

    gfull = {n: [None, None] for n in ("sg_ln_g", "sg_ln_b", "sg_w_spatial", "sg_b_spatial")}
    gbig = {}
    d_norm_g, d_mod = [None] * DEPTH, [None] * DEPTH
    rs_p, rs_q, riding_names = {}, {}, []

    for i in reversed(range(DEPTH)):
        kind, j = KINDS[i], i // 3
        tag = f"l{i}_{kind}_b"
        rider = _chip_exchange_rider([rs_p[n] for n in riding_names]) if riding_names else None
        w_out = {"sg": sg_out(j), "swa": swa_out, "rwkv": rw_out}[kind]
        du = _matmul(dy, w_out, "nt", tag + "_du")
        dw_out = _matmul(us[i], dy, "tn", tag + "_dwout").reshape(N_CHIPS, D // N_CHIPS, D)
        if kind == "sg":
            (dp, dlg, dlb, dws, dbs), ridden = _ridden(
                _sg_bwd(ps[i], row(full["sg_ln_g"], j), row(full["sg_ln_b"], j), sg_w_spatial[j], bs_t[j], du,
                        tag + "_mix", rider), rider)
            gfull["sg_ln_g"][j], gfull["sg_ln_b"][j] = dlg[0], dlb[0]
            gfull["sg_w_spatial"][j], gfull["sg_b_spatial"][j] = dws, dbs[:, :SG_GROUPS].T
            gbig[f"sg_w_in{j}"] = _matmul(hs[i], dp, "tn", tag + "_dwin", blocked=True)
            gbig[f"sg_w_out{j}"] = dw_out
            dh, dh2 = _matmul(dp, _transposed(sg_in(j)), "nn", tag + "_dh"), None
            mine = [f"sg_w_in{j}", f"sg_w_out{j}"]
        elif kind == "swa":
            (dp, dsk), ridden = _ridden(_swa_bwd(ps[i], cos, sin, sink_row, du, tag + "_mix", rider), rider)
            gfull["swa_sinks"] = dsk[:, :N_HEADS]
            gbig["swa_w_in"] = _matmul(hs[i], dp, "tn", tag + "_dwin", blocked=True)
            gbig["swa_w_out"] = dw_out
            dh, dh2 = _matmul(dp, _transposed(swa_in), "nn", tag + "_dh"), None
            mine = ["swa_w_in", "swa_w_out"]
        else:
            dpm, dpl, rg, ridden = _rwkv_mixer_bwd(ps[i][0], ps[i][1], rw_prm, rw_saved, du, tag, rider)
            mine = ["rwkv_w_in", "rwkv_w_out"]
            dw_main = _matmul(hs[i], dpm, "tn", tag + "_dwin")
            dw_lo = _matmul(hs[i], dpl, "tn", tag + "_dwin_lo")
            dw_main = _head_major(dw_main.reshape(D, 4, D)).reshape(D, RW_MAIN)
            dw_in = jnp.concatenate([dw_main, _lo_cols_inv(dw_lo)], axis=1)
            gbig["rwkv_w_in"] = dw_in.reshape(D, N_CHIPS, -1).transpose(1, 0, 2)
            gbig["rwkv_w_out"] = _rows_head_major(dw_out.reshape(D, D)).reshape(dw_out.shape)
            dmu_main = _head_major(rg["mu_main"].reshape(4, D)).reshape(1, RW_MAIN)
            gfull["rwkv_mu"] = jnp.concatenate([dmu_main, _lo_cols_inv(rg["mu_lo"])], axis=1)
            gfull["rwkv_w0"], gfull["rwkv_a0"] = _head_major(rg["w0"]), _head_major(rg["a0"])
            gfull["rwkv_w_lora"], gfull["rwkv_a_lora"] = _head_major(rg["wl"])[None, :LORA], _head_major(rg["al"])[None, :LORA]
            gfull["rwkv_k_k"], gfull["rwkv_k_a"] = _param_compact_inv(rg["kkp"])[None], _param_compact_inv(rg["kap"])[None]
            gfull["rwkv_r_k"] = _param_compact_inv(rg["rkp"]).reshape(1, N_HEADS, HEAD)
            gfull["rwkv_gn_g"], gfull["rwkv_gn_b"] = _param_compact_inv(rg["gn_g"])[None], _param_compact_inv(rg["gn_b"])[None]
            dh, dh2 = _matmul(dpm, rw_main, "nt", tag + "_dh"), _matmul(dpl, rw_lo, "nt", tag + "_dh_lo")
        rs_p.update(zip(riding_names, ridden[:len(riding_names)]))
        rs_q.update(zip(riding_names, ridden[len(riding_names):]))
        if i == 0:
            for n in ("sg_ln_g", "sg_ln_b"):
                gfull[n] = jnp.stack(gfull[n])
            small = jnp.concatenate([_chip_blocks(gfull[n], axis) for n, axis in SMALL.items()], axis=1)
            small_rows = _round_up(small.shape[1], 2 * 16 * LANES) // LANES
            small = jnp.pad(small, ((0, 0), (0, small_rows * LANES - small.shape[1])))
            gbig["small"] = small.reshape(N_CHIPS, small_rows, LANES)
            mine = mine + ["small"]
        gs = [gbig[n].reshape(N_CHIPS, 2, gbig[n].shape[1] // 2, gbig[n].shape[2]) for n in mine]
        pair_rider = _pair_exchange_rider(gs)
        below = (ys[i - 1], row(gate, i - 1)) if i > 0 else None
        (dx, dg, dsh, dsc, *below_grads), gots = _norm_mod_bwd(xs[i], row(norm_g, i), row(shift, i), row(scale, i), dh, dx,
                                                               tag + "_norm", dh2, pair_rider, below)
        d_norm_g[i] = dg[0]
        d_mod[i] = jnp.concatenate([dsh[0], dsc[0], d_gate[0]])
        if below is not None:
            dy, d_gate = below_grads
        rs_p.update(zip(mine, _rs_pair_sums(gots[:len(gs)], gots[len(gs):], ci, f"rs{i}")))
        riding_names = mine
    for n in ("sg_w_spatial", "sg_b_spatial"):
        gfull[n] = jnp.stack(gfull[n])
    gfull["norm_g"], gfull["final_norm_g"] = jnp.stack(d_norm_g), d_final_g[0]

    grads, deltas, new_m, new_v, red = {}, {}, {}, {}, {}

    def finish(names, tag):
        outs = _rs_finish([rs_p[n] for n in names], [rs_q[n] for n in names], chip, ci, tag)
        red.update({n: r.reshape(-1, r.shape[2]) for n, r in zip(names, outs)})

    def adamw(n, rider=None):
        res = _adamw(w[n], grads[n], given["m_" + n], given["v_" + n], "adamw_" + n, rider)
        (deltas[n], new_m[n], new_v[n]), ridden = _ridden(res, rider)
        return ridden

    def ride_exchange(names, on):
        ridden = adamw(on, _chip_exchange_rider([rs_p[n] for n in names]))
        rs_p.update(zip(names, ridden[:len(names)]))
        rs_q.update(zip(names, ridden[len(names):]))

    rep_flat = jnp.concatenate([gfull[n].reshape(-1) for n in REPLICATED])
    rep_rows = _round_up(rep_flat.shape[0], 32 * RS_W) // RS_W
    rep_buffer = _replicated_pair_sum(_pad_to(rep_flat, rep_rows * RS_W).reshape(rep_rows, RS_W), "rep")

    finish(sorted(set(rs_p) - set(riding_names)), "rs_a")
    for n in ("swa_w_in", "swa_w_out", "rwkv_w_in", "rwkv_w_out"):
        grads[n] = red[n][None]
    dmod_all = _all_gather8(jnp.stack(d_mod).reshape(DEPTH * 3 * D // RS_W, RS_W), "gather_dmod")
    grads["mod_b"] = _sum_leading(dmod_all, "sum_dmod").reshape(DEPTH, 3 * D)
    dmod_all = dmod_all.reshape(N_DEV, DEPTH, 3 * D)
    dmod_cols = lax.dynamic_slice_in_dim(dmod_all, chip * MOD_COLS, MOD_COLS, axis=2).transpose(1, 0, 2)
    grads["mod_w"] = _mod_w_grad(c_all, dmod_cols, "mod_w_grad")
    ride_exchange(["sg_w_in0"], on="mod_w")
    ride_exchange(["sg_w_out0", "small"], on="rwkv_w_in")
    finish(riding_names, "rs_b")
    grads["sg_w_in"] = jnp.stack([red["sg_w_in0"], red["sg_w_in1"]])
    grads["sg_w_out"] = jnp.stack([red["sg_w_out0"], red["sg_w_out1"]])
    small_red, off = red["small"].reshape(-1), 0
    for n in SMALL:
        grads[n] = small_red[off:off + w[n].size].reshape(w[n].shape)
        off += w[n].size

    (rep_gathered,) = adamw("sg_w_in", _gather_rider([rep_buffer]))
    rep_sum, off = _replicated_chip_sum(rep_gathered, "rep").reshape(-1), 0
    for n in REPLICATED:
        grads[n] = rep_sum[off:off + w[n].size].reshape(w[n].shape)
        off += w[n].size

    for n in W_NAMES:
        if n not in deltas:
            adamw(n)
    return (loss, dx[None], *[grads[n] for n in W_NAMES], *[deltas[n] for n in W_NAMES],
            *[new_m[n] for n in W_NAMES], *[new_v[n] for n in W_NAMES])
```

```python
import functools
import math

import jax
import jax.numpy as jnp
from jax import lax
from jax.experimental import pallas as pl
from jax.experimental.pallas import tpu as pltpu

F32 = jnp.float32
BF16 = jnp.bfloat16
HIGHEST = lax.Precision.HIGHEST

D = 2048
DEPTH = 4
CHUNK = 128
SG_GROUPS = 16
HEAD = 64
N_HEADS = D // HEAD
KV_HEADS = 4
KVW = KV_HEADS * HEAD
ROPE_THETA = 10000.0
LORA = 96
LORA_PAD = 128
DECAY_SCALE = math.exp(-0.5)
GN_EPS = 64e-5
RMS_EPS = 1e-6
LN_EPS = 1e-5
ADAM_LR, ADAM_B1, ADAM_B2, ADAM_EPS, ADAM_WD, ADAM_STEP = 0.001, 0.9, 0.999, 1e-08, 0.01, 10
LANES = 128
NEG = -1e30
VMEM_LIMIT = 56 * 1024 * 1024

MESHT = pl.DeviceIdType.MESH


def _cparams(*sem):
    return pltpu.CompilerParams(dimension_semantics=sem, vmem_limit_bytes=VMEM_LIMIT)


class _Rider:
    def __init__(self, arrays, n_sems, start, finish):
        self.arrays, self.n_sems, self.start, self.finish = list(arrays), n_sems, start, finish


def _ridden(res, rider):
    return (res, []) if rider is None else res


def _compute_call(body, args, *, name, grid, in_specs, out_specs, out_shape, semantics, scratch_shapes=(), rider=None):
    if rider is None:
        return pl.pallas_call(body, name=name, grid=grid, in_specs=in_specs, out_specs=out_specs, out_shape=out_shape,
                              scratch_shapes=list(scratch_shapes), compiler_params=_cparams(*semantics))(*args)
    single = not isinstance(out_shape, (list, tuple))
    o_specs, o_shapes = ([out_specs], [out_shape]) if single else (list(out_specs), list(out_shape))
    n_in, n_out, n_r = len(in_specs), len(o_specs), len(rider.arrays)

    def with_rider(*refs):
        ins, outs = refs[:n_in], refs[n_in + n_r:n_in + n_r + n_out]
        ridden = refs[n_in + n_r + n_out:n_in + 2 * n_r + n_out]
        scratch, (send_sems, recv_sems) = refs[n_in + 2 * n_r + n_out:-2], refs[-2:]
        ids = [pl.program_id(d) for d in range(len(grid))]
        first = functools.reduce(jnp.logical_and, [i == 0 for i in ids])
        last = functools.reduce(jnp.logical_and, [i == g - 1 for i, g in zip(ids, grid)])

        @pl.when(first)
        def _():
            rider.start(ridden, send_sems, recv_sems)

        body(*ins, *outs, *scratch)

        @pl.when(last)
        def _():
            rider.finish(ridden, send_sems, recv_sems)

    any_spec = pl.BlockSpec(memory_space=pl.ANY)
    res = pl.pallas_call(
        with_rider, name=name, grid=grid, in_specs=list(in_specs) + [any_spec] * n_r, out_specs=o_specs + [any_spec] * n_r,
        out_shape=o_shapes + [jax.ShapeDtypeStruct(a.shape, a.dtype) for a in rider.arrays],
        input_output_aliases={n_in + i: n_out + i for i in range(n_r)},
        scratch_shapes=list(scratch_shapes) + [pltpu.SemaphoreType.DMA((rider.n_sems,))] * 2,
        compiler_params=_cparams(*["arbitrary"] * len(grid)),
    )(*args, *rider.arrays)
    return (res[0] if single else list(res[:n_out])), list(res[n_out:])


_NN = (((1,), (0,)), ((), ()))
_NT = (((1,), (1,)), ((), ()))
_TN = (((0,), (0,)), ((), ()))


def _dg(a, b, dims):
    return lax.dot_general(a, b, dims, preferred_element_type=F32)


@jax.custom_vjp
def _bdot_nn(a, b):
    return _dg(a.astype(BF16), b.astype(BF16), _NN)


def _bdot_nn_fwd(a, b):
    a, b = a.astype(BF16), b.astype(BF16)
    return _dg(a, b, _NN), (a, b)


def _bdot_nn_bwd(res, ct):
    a, b = res
    ct = ct.astype(BF16)
    return _dg(ct, b, _NT), _dg(a, ct, _TN)


_bdot_nn.defvjp(_bdot_nn_fwd, _bdot_nn_bwd)


@jax.custom_vjp
def _bdot_nt(a, b):
    return _dg(a.astype(BF16), b.astype(BF16), _NT)


def _bdot_nt_fwd(a, b):
    a, b = a.astype(BF16), b.astype(BF16)
    return _dg(a, b, _NT), (a, b)


def _bdot_nt_bwd(res, ct):
    a, b = res
    ct = ct.astype(BF16)
    return _dg(ct, b, _NN), _dg(ct, a, _TN)


_bdot_nt.defvjp(_bdot_nt_fwd, _bdot_nt_bwd)


def _tile(n, cap):
    if n <= cap:
        return n
    return max(d for d in range(LANES, cap + 1, LANES) if n % d == 0)


def _matmul(a, b, form, name, out_dtype=F32, blocked=False, rider=None, tm=1024, tn=None, tk=4096):
    if form == "nn":
        (m, k), n = a.shape, (N_CHIPS * b.shape[2] if blocked else b.shape[1])
    elif form == "nt":
        m, k, n = a.shape[0], a.shape[1], (b.shape[1] if blocked else b.shape[0])
    else:
        (k, m), n = a.shape, b.shape[1]
    per_chip = (k if form == "nt" else n) // N_CHIPS
    tn = tn or (512 if form == "tn" else 1024)
    if blocked and form == "nt":
        tk = _tile(per_chip, tk)
    elif blocked:
        tn = _tile(per_chip, tn)
    tm, tn, tk = _tile(m, tm), _tile(n, tn), _tile(k, tk)
    assert m % tm == 0 and n % tn == 0 and k % tk == 0, (name, a.shape, b.shape)
    nk = k // tk
    dims = {"nn": _NN, "nt": _NT, "tn": _TN}[form]
    a_spec = pl.BlockSpec((tk, tm), lambda i, j, l: (l, i)) if form == "tn" else pl.BlockSpec((tm, tk), lambda i, j, l: (i, l))
    b_spec = pl.BlockSpec((tn, tk), lambda i, j, l: (j, l)) if form == "nt" else pl.BlockSpec((tk, tn), lambda i, j, l: (l, j))
    o_spec = pl.BlockSpec((tm, tn), lambda i, j, l: (i, j))
    o_shape = (m, n)
    if blocked and form == "nn":
        pc = per_chip // tn
        b_spec = pl.BlockSpec((None, tk, tn), lambda i, j, l: (j // pc, l, j % pc))
    elif blocked and form == "nt":
        pc = per_chip // tk
        b_spec = pl.BlockSpec((None, tn, tk), lambda i, j, l: (l // pc, j, l % pc))
    elif blocked:
        pc = per_chip // tn
        o_spec = pl.BlockSpec((None, tm, tn), lambda i, j, l: (j // pc, i, j % pc))
        o_shape = (N_CHIPS, m, per_chip)

    def body(a_ref, b_ref, o_ref, acc_ref):
        part = _dg(a_ref[...], b_ref[...], dims)
        if nk == 1:
            o_ref[...] = part.astype(out_dtype)
        else:
            l = pl.program_id(2)

            @pl.when(l == 0)
            def _():
                acc_ref[...] = part

            @pl.when(l > 0)
            def _():
                acc_ref[...] += part

            @pl.when(l == nk - 1)
            def _():
                o_ref[...] = acc_ref[...].astype(out_dtype)

    return _compute_call(
        body, (a, b), name=name, grid=(m // tm, n // tn, nk),
        in_specs=[a_spec, b_spec], out_specs=o_spec, out_shape=jax.ShapeDtypeStruct(o_shape, out_dtype),
        scratch_shapes=[pltpu.VMEM((tm, tn) if nk > 1 else (8, LANES), F32)],
        semantics=("parallel", "parallel", "arbitrary"), rider=rider)


def _out_proj_resid(u, w_out, x, gate, name, tm=1024, tn=512):
    (m, k), n = u.shape, w_out.shape[1]

    def body(u_ref, w_ref, x_ref, g_ref, y_ref, xn_ref):
        y = _dg(u_ref[...], w_ref[...], _NN)
        y_ref[...] = y
        xn_ref[...] = x_ref[...] + g_ref[...] * y

    tile = pl.BlockSpec((tm, tn), lambda i, j: (i, j))
    out = jax.ShapeDtypeStruct((m, n), F32)
    return pl.pallas_call(
        body, name=name, grid=(m // tm, n // tn),
        in_specs=[pl.BlockSpec((tm, k), lambda i, j: (i, 0)), pl.BlockSpec((k, tn), lambda i, j: (0, j)), tile,
                  pl.BlockSpec((1, tn), lambda i, j: (0, j))],
        out_specs=[tile, tile], out_shape=[out, out], compiler_params=_cparams("parallel", "parallel"),
    )(u, w_out, x, gate)


TB_NORM = 256


def _f_norm_mod(x, g, shift, scale):
    xn = x * lax.rsqrt(jnp.mean(x * x, axis=-1, keepdims=True) + RMS_EPS)
    return (xn * g) * (1.0 + scale) + shift


def _row_spec(width, tb=TB_NORM):
    return pl.BlockSpec((tb, width), lambda i: (i, 0))


def _vec_spec(width, rows=1):
    return pl.BlockSpec((rows, width), lambda i: (0, 0))


def _norm_mod_fwd(x, g, shift, scale, name):
    t = x.shape[0]

    def body(x_ref, g_ref, sh_ref, sc_ref, h_ref):
        h_ref[...] = _f_norm_mod(x_ref[...], g_ref[...], sh_ref[...], sc_ref[...]).astype(BF16)

    return pl.pallas_call(
        body, name=name, grid=(t // TB_NORM,),
        in_specs=[_row_spec(D), _vec_spec(D), _vec_spec(D), _vec_spec(D)], out_specs=_row_spec(D),
        out_shape=jax.ShapeDtypeStruct((t, D), BF16), compiler_params=_cparams("parallel"),
    )(x, g, shift, scale)


def _accum(ref, val, first):
    @pl.when(first)
    def _():
        ref[...] = val

    @pl.when(jnp.logical_not(first))
    def _():
        ref[...] += val


def _gate_bwd_block(dx, y_ref, gate_ref, dy_ref, dgate_ref, first):
    dy_ref[...] = (dx * gate_ref[...]).astype(BF16)
    _accum(dgate_ref, jnp.sum(dx * y_ref[...], axis=0, keepdims=True), first)


def _gate_bwd_specs():
    return [_row_spec(D), _vec_spec(D)]


def _gate_bwd_shapes(t):
    return [jax.ShapeDtypeStruct((t, D), BF16), jax.ShapeDtypeStruct((1, D), F32)]


def _norm_mod_bwd(x, g, shift, scale, dh, dx_res, name, dh2=None, rider=None, below=None):
    t = x.shape[0]
    dhs = [dh] if dh2 is None else [dh, dh2]
    extra = [] if below is None else list(below)

    def body(x_ref, g_ref, sh_ref, sc_ref, dr_ref, *refs):
        dh_refs, refs = refs[:len(dhs)], refs[len(dhs):]
        below_refs, (dx_ref, dg_ref, dsh_ref, dsc_ref), below_out = refs[:len(extra)], refs[len(extra):len(extra) + 4], refs[len(extra) + 4:]
        _, vjp = jax.vjp(_f_norm_mod, x_ref[...], g_ref[...], sh_ref[...], sc_ref[...])
        dh_all = dh_refs[0][...]
        for r in dh_refs[1:]:
            dh_all = dh_all + r[...]
        dx, dg, dsh, dsc = vjp(dh_all)
        dx = dx + dr_ref[...]
        dx_ref[...] = dx
        first = pl.program_id(0) == 0
        _accum(dg_ref, dg, first)
        _accum(dsh_ref, dsh, first)
        _accum(dsc_ref, dsc, first)
        if below is not None:
            _gate_bwd_block(dx, *below_refs, *below_out, first)

    vec = jax.ShapeDtypeStruct((1, D), F32)
    return _compute_call(
        body, (x, g, shift, scale, dx_res, *dhs, *extra), name=name, grid=(t // TB_NORM,),
        in_specs=[_row_spec(D), _vec_spec(D), _vec_spec(D), _vec_spec(D), _row_spec(D)] + [_row_spec(D)] * len(dhs)
        + (_gate_bwd_specs() if extra else []),
        out_specs=[_row_spec(D), _vec_spec(D), _vec_spec(D), _vec_spec(D)] + (_gate_bwd_specs() if extra else []),
        out_shape=[jax.ShapeDtypeStruct((t, D), F32), vec, vec, vec] + (_gate_bwd_shapes(t) if extra else []),
        semantics=("arbitrary",), rider=rider)


def _f_final(x, g, target):
    xn = x * lax.rsqrt(jnp.mean(x * x, axis=-1, keepdims=True) + RMS_EPS)
    err = xn * g - target
    return 0.5 * jnp.sum(jnp.mean(err * err, axis=-1, keepdims=True), axis=0, keepdims=True)


def _final_loss_grad(x, g, target, y, gate, name):
    t = x.shape[0]

    def body(x_ref, g_ref, t_ref, y_ref, gate_ref, loss_ref, dx_ref, dg_ref, dy_ref, dgate_ref):
        loss, vjp = jax.vjp(_f_final, x_ref[...], g_ref[...], t_ref[...])
        dx, dg, _ = vjp(jnp.ones((1, 1), F32))
        dx_ref[...] = dx
        first = pl.program_id(0) == 0
        _accum(dg_ref, dg, first)
        _accum(loss_ref, jnp.broadcast_to(loss, (1, LANES)), first)
        _gate_bwd_block(dx, y_ref, gate_ref, dy_ref, dgate_ref, first)

    return pl.pallas_call(
        body, name=name, grid=(t // TB_NORM,),
        in_specs=[_row_spec(D), _vec_spec(D), _row_spec(D)] + _gate_bwd_specs(),
        out_specs=[_vec_spec(LANES), _row_spec(D), _vec_spec(D)] + _gate_bwd_specs(),
        out_shape=[jax.ShapeDtypeStruct((1, LANES), F32), jax.ShapeDtypeStruct((t, D), F32), jax.ShapeDtypeStruct((1, D), F32)]
        + _gate_bwd_shapes(t),
        compiler_params=_cparams("arbitrary"),
    )(x, g, target, y, gate)


def _group_selector():
    gi = lax.broadcasted_iota(jnp.int32, (LANES, D), 0)
    ci = lax.broadcasted_iota(jnp.int32, (LANES, D), 1)
    return (ci // (D // SG_GROUPS) == gi).astype(F32)


def _f_sg(p, ln_g, ln_b, w_s, bs_t):
    u, v, z = p[:, :D], p[:, D:2 * D], p[:, 2 * D:]
    u = jax.nn.gelu(u)
    vf = jax.nn.gelu(v)
    mean = jnp.mean(vf, axis=-1, keepdims=True)
    var = jnp.mean(jnp.square(vf - mean), axis=-1, keepdims=True)
    vn = (vf - mean) * lax.rsqrt(var + LN_EPS) * ln_g + ln_b
    ti = lax.broadcasted_iota(jnp.int32, (CHUNK, CHUNK), 0)
    si = lax.broadcasted_iota(jnp.int32, (CHUNK, CHUNK), 1)
    causal = si <= ti
    cg = D // SG_GROUPS
    f = jnp.concatenate(
        [_bdot_nn(jnp.where(causal, w_s[g], 0.0), vn[:, g * cg:(g + 1) * cg]) for g in range(SG_GROUPS)], axis=1)
    f = f + jnp.dot(bs_t, _group_selector(), precision=HIGHEST, preferred_element_type=F32)
    return u * f * jax.nn.silu(z)


def _sg_specs():
    return [pl.BlockSpec((CHUNK, 3 * D), lambda i: (i, 0)), _vec_spec(D), _vec_spec(D),
            pl.BlockSpec((SG_GROUPS, CHUNK, CHUNK), lambda i: (0, 0, 0)), _vec_spec(LANES, CHUNK)]


def _sg_fwd(p, ln_g, ln_b, w_s, bs_t, name, rider=None):
    t = p.shape[0]

    def body(p_ref, lg_ref, lb_ref, w_ref, b_ref, o_ref):
        o_ref[...] = _f_sg(p_ref[...], lg_ref[...], lb_ref[...], w_ref[...], b_ref[...]).astype(BF16)

    return _compute_call(
        body, (p, ln_g, ln_b, w_s, bs_t), name=name, grid=(t // CHUNK,), in_specs=_sg_specs(),
        out_specs=_row_spec(D, CHUNK), out_shape=jax.ShapeDtypeStruct((t, D), BF16), semantics=("parallel",), rider=rider)


def _sg_bwd(p, ln_g, ln_b, w_s, bs_t, dout, name, rider=None):
    t = p.shape[0]

    def body(p_ref, lg_ref, lb_ref, w_ref, b_ref, do_ref, dp_ref, dlg_ref, dlb_ref, dw_ref, db_ref):
        _, vjp = jax.vjp(_f_sg, p_ref[...], lg_ref[...], lb_ref[...], w_ref[...], b_ref[...])
        dp, dlg, dlb, dw, db = vjp(do_ref[...])
        dp_ref[...] = dp.astype(BF16)
        first = pl.program_id(0) == 0
        _accum(dlg_ref, dlg, first)
        _accum(dlb_ref, dlb, first)
        _accum(dw_ref, dw, first)
        _accum(db_ref, db, first)

    vec = jax.ShapeDtypeStruct((1, D), F32)
    return _compute_call(
        body, (p, ln_g, ln_b, w_s, bs_t, dout), name=name, grid=(t // CHUNK,), in_specs=_sg_specs() + [_row_spec(D, CHUNK)],
        out_specs=[pl.BlockSpec((CHUNK, 3 * D), lambda i: (i, 0)), _vec_spec(D), _vec_spec(D),
                   pl.BlockSpec((SG_GROUPS, CHUNK, CHUNK), lambda i: (0, 0, 0)), _vec_spec(LANES, CHUNK)],
        out_shape=[jax.ShapeDtypeStruct((t, 3 * D), BF16), vec, vec,
                   jax.ShapeDtypeStruct((SG_GROUPS, CHUNK, CHUNK), F32), jax.ShapeDtypeStruct((CHUNK, LANES), F32)],
        semantics=("arbitrary",), rider=rider)


SWA_COLS = 2 * D + 2 * KVW
KV_BLOCK = 2 * KVW


def _lane_roll(x, shift):
    return pltpu.roll(x, shift, 1)


def _rot_half(x):
    w = x.shape[1]
    lane = lax.broadcasted_iota(jnp.int32, x.shape, 1)
    return jnp.where(lane % HEAD < HEAD // 2, -_lane_roll(x, w - HEAD // 2), _lane_roll(x, HEAD // 2))


@jax.custom_vjp
def _rope(x, cos, sin):
    return x * cos + _rot_half(x) * sin


def _rope_fwd(x, cos, sin):
    return _rope(x, cos, sin), (cos, sin)


def _rope_bwd(res, ct):
    cos, sin = res
    return ct * cos - _rot_half(ct) * sin, jnp.zeros_like(cos), jnp.zeros_like(sin)


_rope.defvjp(_rope_fwd, _rope_bwd)


@jax.custom_vjp
def _swap_halves(x):
    return _lane_roll(x, HEAD)


_swap_halves.defvjp(lambda x: (_lane_roll(x, HEAD), None), lambda _, ct: (_lane_roll(ct, HEAD),))


def _f_swa(pq, pkv, cos, sin, cosp, sinp, sink_row, valid):
    reps = D // LANES
    q = _rope(pq[:, :D], jnp.tile(cos, (1, reps)), jnp.tile(sin, (1, reps))) * (HEAD ** -0.5)
    k = _rope(pq[:, D:D + KVW], jnp.tile(cos, (1, KVW // LANES)), jnp.tile(sin, (1, KVW // LANES)))
    kp = _rope(pkv[:, :KVW], jnp.tile(cosp, (1, KVW // LANES)), jnp.tile(sinp, (1, KVW // LANES)))
    v, vp, z = pq[:, D + KVW:D + 2 * KVW], pkv[:, KVW:], pq[:, D + 2 * KVW:]
    kcat = jnp.concatenate([kp, k], axis=0)
    vcat = jnp.concatenate([vp, v], axis=0)
    lane = lax.broadcasted_iota(jnp.int32, (2 * CHUNK, LANES), 1)
    lo = lane < HEAD
    hlane = lax.broadcasted_iota(jnp.int32, (1, LANES), 1)

    def halves(cat, g):
        blk = cat[:, (g // 2) * LANES:(g // 2 + 1) * LANES]
        other = _swap_halves(blk)
        if g % 2 == 0:
            return jnp.where(lo, blk, 0.0), jnp.where(lo, 0.0, other)
        return jnp.where(lo, other, 0.0), jnp.where(lo, 0.0, blk)

    pairs = N_HEADS // KV_HEADS // 2
    valid_g = jnp.tile(valid, (pairs, 1))

    def probs(s, heads):
        sink = jnp.concatenate(
            [jnp.broadcast_to(jnp.sum(jnp.where(hlane == h, sink_row, 0.0), axis=1, keepdims=True), (CHUNK, 1))
             for h in heads], axis=0)
        s = jnp.where(valid_g, s, NEG)
        m = lax.stop_gradient(jnp.maximum(jnp.max(s, axis=1, keepdims=True), sink))
        e = jnp.exp(s - m)
        return e / (jnp.sum(e, axis=1, keepdims=True) + jnp.exp(sink - m))

    outs = []
    for g in range(KV_HEADS):
        k_lo, k_hi = halves(kcat, g)
        v_lo, v_hi = halves(vcat, g)
        tiles = range(g * pairs, (g + 1) * pairs)
        qg = jnp.concatenate([q[:, j * LANES:(j + 1) * LANES] for j in tiles], axis=0)
        p_a = probs(_bdot_nt(qg, k_lo), [2 * j for j in tiles])
        p_b = probs(_bdot_nt(qg, k_hi), [2 * j + 1 for j in tiles])
        og = _bdot_nn(p_a, v_lo) + _bdot_nn(p_b, v_hi)
        outs += [og[n * CHUNK:(n + 1) * CHUNK] for n in range(pairs)]
    return jnp.concatenate(outs, axis=1) * jax.nn.silu(z)


def _swa_valid(block):
    qi = lax.broadcasted_iota(jnp.int32, (CHUNK, 2 * CHUNK), 0)
    kj = lax.broadcasted_iota(jnp.int32, (CHUNK, 2 * CHUNK), 1)
    rel = qi + CHUNK - kj
    return (rel >= 0) & (rel < CHUNK) & ((kj >= CHUNK) | (block > 0))


def _swa_specs(blk):
    prev = lambda i: jnp.maximum(blk(i) - 1, 0)
    kv_col = D // KV_BLOCK
    return [pl.BlockSpec((CHUNK, SWA_COLS), lambda i: (blk(i), 0)),
            pl.BlockSpec((CHUNK, KV_BLOCK), lambda i: (prev(i), kv_col)),
            pl.BlockSpec((CHUNK, LANES), lambda i: (blk(i), 0)), pl.BlockSpec((CHUNK, LANES), lambda i: (blk(i), 0)),
            pl.BlockSpec((CHUNK, LANES), lambda i: (prev(i), 0)), pl.BlockSpec((CHUNK, LANES), lambda i: (prev(i), 0)),
            _vec_spec(LANES)]


def _swa_fwd(p, cos, sin, sink_row, name, rider=None):
    t = p.shape[0]

    def body(pq_ref, pkv_ref, c_ref, s_ref, cp_ref, sp_ref, sk_ref, o_ref):
        valid = _swa_valid(pl.program_id(0))
        o_ref[...] = _f_swa(pq_ref[...], pkv_ref[...], c_ref[...], s_ref[...], cp_ref[...], sp_ref[...],
                            sk_ref[...], valid).astype(BF16)

    return _compute_call(
        body, (p, p, cos, sin, cos, sin, sink_row), name=name, grid=(t // CHUNK,), in_specs=_swa_specs(lambda i: i),
        out_specs=_row_spec(D, CHUNK), out_shape=jax.ShapeDtypeStruct((t, D), BF16), semantics=("parallel",), rider=rider)


def _swa_bwd(p, cos, sin, sink_row, dout, name, rider=None):
    t = p.shape[0]
    nb = t // CHUNK
    blk = lambda i: nb - 1 - i

    def body(pq_ref, pkv_ref, c_ref, s_ref, cp_ref, sp_ref, sk_ref, do_ref, dp_ref, dsk_ref, pend_ref):
        i = pl.program_id(0)
        valid = _swa_valid(blk(i))
        f = functools.partial(_f_swa, valid=valid)
        _, vjp = jax.vjp(f, pq_ref[...], pkv_ref[...], c_ref[...], s_ref[...], cp_ref[...], sp_ref[...], sk_ref[...])
        dpq, dpkv, _, _, _, _, dsk = vjp(do_ref[...])

        @pl.when(i == 0)
        def _():
            pend_ref[...] = jnp.zeros_like(pend_ref)

        dp_ref[...] = jnp.concatenate(
            [dpq[:, :D], dpq[:, D:D + KV_BLOCK] + pend_ref[...], dpq[:, D + KV_BLOCK:]], axis=1).astype(BF16)
        pend_ref[...] = dpkv
        _accum(dsk_ref, dsk, i == 0)

    return _compute_call(
        body, (p, p, cos, sin, cos, sin, sink_row, dout), name=name, grid=(nb,),
        in_specs=_swa_specs(blk) + [pl.BlockSpec((CHUNK, D), lambda i: (blk(i), 0))],
        out_specs=[pl.BlockSpec((CHUNK, SWA_COLS), lambda i: (blk(i), 0)), _vec_spec(LANES)],
        out_shape=[jax.ShapeDtypeStruct((t, SWA_COLS), BF16), jax.ShapeDtypeStruct((1, LANES), F32)],
        scratch_shapes=[pltpu.VMEM((CHUNK, KV_BLOCK), F32)], semantics=("arbitrary",), rider=rider)


RW_MAIN = 4 * D
RW_LO = 2 * LORA_PAD
VM = LANES // N_HEADS
VD = HEAD // VM
S_ROWS = VD * HEAD
TB_RW = 128
TB_K = 32
TB_SCAN = 16


def _dim_major(a):
    return a.reshape(a.shape[:-1] + (N_HEADS, HEAD)).swapaxes(-1, -2).reshape(a.shape)


def _head_major(a):
    return a.reshape(a.shape[:-1] + (HEAD, N_HEADS)).swapaxes(-1, -2).reshape(a.shape)


def _param_compact(w):
    return _dim_major(w).reshape(VD, LANES)


def _param_compact_inv(pc):
    return _head_major(pc.reshape(-1))


def _f_rwkv_lora(xs_lo, w0, a0, wl, al):
    decay = jnp.exp(-DECAY_SCALE * jax.nn.sigmoid(w0 + _bdot_nn(jnp.tanh(xs_lo[:, :LORA_PAD]), wl)))
    a = jax.nn.sigmoid(a0 + _bdot_nn(xs_lo[:, LORA_PAD:], al))
    return decay, a


def _prev_rows_spec(width, tb):
    return pl.BlockSpec((8, width), lambda i: (jnp.maximum(i * (tb // 8) - 1, 0), 0))


def _token_shift_lerp(p, prev8, mu, first):
    rows = lax.broadcasted_iota(jnp.int32, p.shape, 0)
    prev = jnp.where(first, 0.0, prev8[7:8, :])
    shifted = jnp.where(rows == 0, prev, pltpu.roll(p, 1, 0))
    return p + (shifted - p) * mu


def _store_compact(ref, val):
    for j in range(VD):
        ref[:, j, :] = val[:, j * LANES:(j + 1) * LANES]


def _load_flat(ref, rows=slice(None)):
    if len(ref.shape) == 2:
        return ref[rows, :]
    return jnp.concatenate([ref[rows, j, :] for j in range(VD)], axis=1)


def _flat_spec(a, tb):
    return _row_spec(a.shape[1], tb) if a.ndim == 2 else _k_spec(VD, tb)


def _rwkv_pre_fwd(p_main, p_lo, mu_main, mu_lo, w0, a0, wl, al, name):
    t = p_main.shape[0]
    tb = TB_RW

    def body(pm_ref, pmp_ref, pl_ref, plp_ref, mm_ref, ml_ref, w0_ref, a0_ref, wl_ref, al_ref,
             r_ref, k_ref, v_ref, dec_ref, a_ref, z_ref, xl_ref):
        first = pl.program_id(0) == 0
        xs = _token_shift_lerp(pm_ref[...], pmp_ref[...], mm_ref[...], first)
        for n, ref in enumerate((r_ref, k_ref, v_ref)):
            _store_compact(ref, xs[:, n * D:(n + 1) * D])
        z_ref[...] = xs[:, 3 * D:]
        xs_lo = _token_shift_lerp(pl_ref[...], plp_ref[...], ml_ref[...], first)
        xl_ref[...] = xs_lo
        decay, a = _f_rwkv_lora(xs_lo, w0_ref[...], a0_ref[...], wl_ref[...], al_ref[...])
        _store_compact(dec_ref, decay)
        _store_compact(a_ref, a)

    cl = jax.ShapeDtypeStruct((t, VD, LANES), F32)
    return pl.pallas_call(
        body, name=name, grid=(t // tb,),
        in_specs=[_row_spec(RW_MAIN, tb), _prev_rows_spec(RW_MAIN, tb), _row_spec(RW_LO, tb), _prev_rows_spec(RW_LO, tb),
                  _vec_spec(RW_MAIN), _vec_spec(RW_LO), _vec_spec(D), _vec_spec(D),
                  _vec_spec(D, LORA_PAD), _vec_spec(D, LORA_PAD)],
        out_specs=[_k_spec(VD, tb)] * 5 + [_row_spec(D, tb), _row_spec(RW_LO, tb)],
        out_shape=[cl] * 5 + [jax.ShapeDtypeStruct((t, D), F32), jax.ShapeDtypeStruct((t, RW_LO), F32)],
        compiler_params=_cparams("parallel"),
    )(p_main, p_main, p_lo, p_lo, mu_main, mu_lo, w0, a0, wl, al)


def _rwkv_lora_bwd(xs_lo, w0, a0, wl, al, ddecay, da, name):
    t = xs_lo.shape[0]
    tb = TB_NORM

    def body(x_ref, w0_ref, a0_ref, wl_ref, al_ref, dd_ref, da_ref, dx_ref, dw0_ref, da0_ref, dwl_ref, dal_ref):
        _, vjp = jax.vjp(_f_rwkv_lora, x_ref[...], w0_ref[...], a0_ref[...], wl_ref[...], al_ref[...])
        dx, dw0, da0, dwl, dal = vjp((_load_flat(dd_ref), _load_flat(da_ref)))
        dx_ref[...] = dx
        first = pl.program_id(0) == 0
        _accum(dw0_ref, dw0, first)
        _accum(da0_ref, da0, first)
        _accum(dwl_ref, dwl, first)
        _accum(dal_ref, dal, first)

    vec = jax.ShapeDtypeStruct((1, D), F32)
    lor = jax.ShapeDtypeStruct((LORA_PAD, D), F32)
    return pl.pallas_call(
        body, name=name, grid=(t // tb,),
        in_specs=[_row_spec(RW_LO), _vec_spec(D), _vec_spec(D), _vec_spec(D, LORA_PAD), _vec_spec(D, LORA_PAD),
                  _k_spec(VD, tb), _k_spec(VD, tb)],
        out_specs=[_row_spec(RW_LO), _vec_spec(D), _vec_spec(D), _vec_spec(D, LORA_PAD), _vec_spec(D, LORA_PAD)],
        out_shape=[jax.ShapeDtypeStruct((t, RW_LO), F32), vec, vec, lor, lor], compiler_params=_cparams("arbitrary"),
    )(xs_lo, w0, a0, wl, al, ddecay, da)


def _lerp_bwd(p, dxs_groups, mu, name):
    t, width = p.shape
    tb = TB_RW
    nb = t // tb
    parts = [a for group in dxs_groups for a in group]

    def body(p_ref, pp_ref, mu_ref, *refs):
        d_refs, (dp_ref, dmu_ref) = refs[:2 * len(parts)], refs[2 * len(parts):]
        i = pl.program_id(0)

        def columns(k):
            pick = (lambda r: _load_flat(r, slice(0, 1))) if k else _load_flat
            vals, at = [], 0
            for group in dxs_groups:
                vals.append(functools.reduce(jnp.add, [pick(d_refs[2 * (at + n) + k]) for n in range(len(group))]))
                at += len(group)
            return jnp.concatenate(vals, axis=1)

        pv, dv, mu_v = p_ref[...], columns(0), mu_ref[...]
        rows = lax.broadcasted_iota(jnp.int32, pv.shape, 0)
        prev = jnp.where(i == 0, 0.0, pp_ref[7:8, :])
        shifted = jnp.where(rows == 0, prev, pltpu.roll(pv, 1, 0))
        nxt = jnp.where(i == nb - 1, 0.0, columns(1))
        d_next = jnp.where(rows == tb - 1, nxt, pltpu.roll(dv, tb - 1, 0))
        dp_ref[...] = (dv * (1.0 - mu_v) + d_next * mu_v).astype(BF16)
        _accum(dmu_ref, jnp.sum(dv * (shifted - pv), axis=0, keepdims=True), i == 0)

    d_specs = []
    for a in parts:
        after = lambda i, nd=a.ndim: (jnp.minimum((i + 1) * (tb // 8), t // 8 - 1),) + (0,) * (nd - 1)
        d_specs += [_flat_spec(a, tb), pl.BlockSpec((8,) + a.shape[1:], after)]
    return pl.pallas_call(
        body, name=name, grid=(nb,),
        in_specs=[_row_spec(width, tb), _prev_rows_spec(width, tb), _vec_spec(width)] + d_specs,
        out_specs=[_row_spec(width, tb), _vec_spec(width)],
        out_shape=[jax.ShapeDtypeStruct((t, width), BF16), jax.ShapeDtypeStruct((1, width), F32)],
        compiler_params=_cparams("arbitrary"),
    )(p, p, mu, *[a for a in parts for _ in range(2)])


def _lane_group_sum2d(x):
    x = x + pltpu.roll(x, N_HEADS, 1)
    return x + pltpu.roll(x, 2 * N_HEADS, 1)


@jax.custom_vjp
def _lane_group_sum(x):
    return _lane_group_sum2d(x.reshape(-1, LANES)).reshape(x.shape)


_lane_group_sum.defvjp(lambda x: (_lane_group_sum(x), None), lambda _, ct: (_lane_group_sum(ct),))


def _head_sum(x):
    return _lane_group_sum(jnp.sum(x, axis=1, keepdims=True))


def _f_kprep(k, a, r, kkp, kap, rkp):
    kk = k * kkp
    kk = kk / jnp.maximum(jnp.sqrt(_head_sum(kk * kk)), 1e-12)
    k2 = k * (1.0 + (a - 1.0) * kap)
    return kk, k2, kk * a, _head_sum(r * k2 * rkp)


def _k_spec(rows=HEAD, tb=TB_K):
    return pl.BlockSpec((tb, rows, LANES), lambda i: (i, 0, 0))


def _kparam_spec(rows=HEAD):
    return pl.BlockSpec((rows, LANES), lambda i: (0, 0))


def _lane_group(shape):
    return lax.broadcasted_iota(jnp.int32, shape, len(shape) - 1) // N_HEADS


def _store_k_layout(ref, xc):
    x2 = xc.reshape(-1, LANES)
    group = _lane_group(x2.shape)
    shifted = [x2] + [pltpu.roll(x2, N_HEADS * k, 1) for k in range(1, VM)]
    for q in range(VM):
        out = shifted[0]
        for k in range(1, VM):
            out = jnp.where(group == (q + k) % VM, shifted[k], out)
        ref[:, pl.ds(q, VD, stride=VM), :] = out.reshape(xc.shape)


def _load_compact(ref):
    shape = (ref.shape[0], VD, LANES)
    rows = [ref[:, pl.ds(q, VD, stride=VM), :].reshape(-1, LANES) for q in range(VM)]
    group = _lane_group(rows[0].shape)
    acc = None
    for k in range(VM):
        t = rows[-k % VM]
        for g in range(1, VM):
            t = jnp.where(group == g, rows[(g - k) % VM], t)
        if k:
            t = pltpu.roll(t, LANES - N_HEADS * k, 1)
        acc = t if acc is None else acc + t
    return acc.reshape(shape)


def _rwkv_kprep_fwd(k, a, r, w, kkp, kap, rkp, name):
    t = k.shape[0]

    def body(k_ref, a_ref, r_ref, w_ref, kkp_ref, kap_ref, rkp_ref, kk_ref, k2_ref, b_ref, r4_ref, w4_ref, rk_ref):
        rv = r_ref[...]
        kk, k2, b, rk_ref[...] = _f_kprep(k_ref[...], a_ref[...], rv, kkp_ref[...], kap_ref[...], rkp_ref[...])
        for ref, val in ((kk_ref, kk), (k2_ref, k2), (b_ref, b), (r4_ref, rv), (w4_ref, w_ref[...])):
            _store_k_layout(ref, val)

    big = jax.ShapeDtypeStruct((t, HEAD, LANES), F32)
    return pl.pallas_call(
        body, name=name, grid=(t // TB_K,),
        in_specs=[_k_spec(VD)] * 4 + [_kparam_spec(VD)] * 3, out_specs=[_k_spec()] * 5 + [_k_spec(1)],
        out_shape=[big] * 5 + [jax.ShapeDtypeStruct((t, 1, LANES), F32)], compiler_params=_cparams("parallel"),
    )(k, a, r, w, kkp, kap, rkp)


def _rwkv_kprep_bwd(k, a, r, kkp, kap, rkp, dkk, dk2, db, drk, dr_scan, dw_scan, name):
    t = k.shape[0]

    def body(k_ref, a_ref, r_ref, kkp_ref, kap_ref, rkp_ref, dkk_ref, dk2_ref, db_ref, drk_ref, drs_ref, dws_ref,
             dk_ref, da_ref, dr_ref, dw_ref, dkkp_ref, dkap_ref, drkp_ref):
        _, vjp = jax.vjp(_f_kprep, k_ref[...], a_ref[...], r_ref[...], kkp_ref[...], kap_ref[...], rkp_ref[...])
        dk, da, dr, dkkp, dkap, drkp = vjp((_load_compact(dkk_ref), _load_compact(dk2_ref), _load_compact(db_ref),
                                            drk_ref[...]))
        dk_ref[...] = dk
        da_ref[...] = da
        dr_ref[...] = dr + _load_compact(drs_ref)
        dw_ref[...] = _load_compact(dws_ref)
        first = pl.program_id(0) == 0
        _accum(dkkp_ref, dkkp, first)
        _accum(dkap_ref, dkap, first)
        _accum(drkp_ref, drkp, first)

    cl = jax.ShapeDtypeStruct((t, VD, LANES), F32)
    par = jax.ShapeDtypeStruct((VD, LANES), F32)
    return pl.pallas_call(
        body, name=name, grid=(t // TB_K,),
        in_specs=[_k_spec(VD)] * 3 + [_kparam_spec(VD)] * 3 + [_k_spec()] * 3 + [_k_spec(1), _k_spec(), _k_spec()],
        out_specs=[_k_spec(VD)] * 4 + [_kparam_spec(VD)] * 3,
        out_shape=[cl] * 4 + [par] * 3, compiler_params=_cparams("arbitrary"),
    )(k, a, r, kkp, kap, rkp, dkk, dk2, db, drk, dr_scan, dw_scan)


def _f_post(y, v, rk, g, b):
    mean = _lane_group_sum(jnp.sum(y, axis=1, keepdims=True)) * (1.0 / HEAD)
    yc = y - mean
    var = _lane_group_sum(jnp.sum(yc * yc, axis=1, keepdims=True)) * (1.0 / HEAD)
    return yc * lax.rsqrt(var + GN_EPS) * g + b + rk * v


def _rwkv_post_fwd(y, v, rk, g, b, name):
    t = y.shape[0]

    def body(y_ref, v_ref, rk_ref, g_ref, b_ref, o_ref):
        o_ref[...] = _f_post(y_ref[...], v_ref[...], rk_ref[...], g_ref[...], b_ref[...])

    return pl.pallas_call(
        body, name=name, grid=(t // TB_K,),
        in_specs=[_k_spec(VD), _k_spec(VD), _k_spec(1), _kparam_spec(VD), _kparam_spec(VD)], out_specs=_k_spec(VD),
        out_shape=jax.ShapeDtypeStruct((t, VD, LANES), F32), compiler_params=_cparams("parallel"),
    )(y, v, rk, g, b)


def _rwkv_post_bwd(y, v, rk, g, b, do, name):
    t = y.shape[0]

    def body(y_ref, v_ref, rk_ref, g_ref, b_ref, do_ref, dy_ref, dv_ref, drk_ref, dg_ref, db_ref):
        _, vjp = jax.vjp(_f_post, y_ref[...], v_ref[...], rk_ref[...], g_ref[...], b_ref[...])
        dy, dv, drk, dg, db = vjp(do_ref[...])
        dy_ref[...] = dy
        dv_ref[...] = dv
        drk_ref[...] = drk
        first = pl.program_id(0) == 0
        _accum(dg_ref, dg, first)
        _accum(db_ref, db, first)

    vl = jax.ShapeDtypeStruct((t, VD, LANES), F32)
    par = jax.ShapeDtypeStruct((VD, LANES), F32)
    return pl.pallas_call(
        body, name=name, grid=(t // TB_K,),
        in_specs=[_k_spec(VD), _k_spec(VD), _k_spec(1), _kparam_spec(VD), _kparam_spec(VD), _k_spec(VD)],
        out_specs=[_k_spec(VD), _k_spec(VD), _k_spec(1), _kparam_spec(VD), _kparam_spec(VD)],
        out_shape=[vl, vl, jax.ShapeDtypeStruct((t, 1, LANES), F32), par, par], compiler_params=_cparams("arbitrary"),
    )(y, v, rk, g, b, do)


def _f_gate(o, z):
    return o * jax.nn.silu(z)


def _rwkv_gate_fwd(o, z, name):
    t = z.shape[0]

    def body(o_ref, z_ref, u_ref):
        u_ref[...] = _f_gate(_load_flat(o_ref), z_ref[...]).astype(BF16)

    return pl.pallas_call(
        body, name=name, grid=(t // TB_NORM,), in_specs=[_k_spec(VD, TB_NORM), _row_spec(D)], out_specs=_row_spec(D),
        out_shape=jax.ShapeDtypeStruct((t, D), BF16), compiler_params=_cparams("parallel"),
    )(o, z)


def _rwkv_gate_bwd(o, z, du, name):
    t = z.shape[0]

    def body(o_ref, z_ref, du_ref, do_ref, dz_ref):
        _, vjp = jax.vjp(_f_gate, _load_flat(o_ref), z_ref[...])
        do, dz_ref[...] = vjp(du_ref[...])
        _store_compact(do_ref, do)

    return pl.pallas_call(
        body, name=name, grid=(t // TB_NORM,), in_specs=[_k_spec(VD, TB_NORM), _row_spec(D), _row_spec(D)],
        out_specs=[_k_spec(VD, TB_NORM), _row_spec(D)],
        out_shape=[jax.ShapeDtypeStruct((t, VD, LANES), F32), jax.ShapeDtypeStruct((t, D), F32)],
        compiler_params=_cparams("parallel"),
    )(o, z, du)


def _colsum(x):
    return jnp.sum(x, axis=0, keepdims=True)


def _rwkv_scan_fwd(r4, w4, k24, kk4, b4, v, name, rider=None):
    t = r4.shape[0]
    tb = TB_SCAN

    def body(r_ref, w_ref, k2_ref, kk_ref, b_ref, v_ref, y_ref, sall_ref, sa_ref, s_scr):
        @pl.when(pl.program_id(0) == 0)
        def _():
            s_scr[...] = jnp.zeros_like(s_scr)

        sall_ref[0] = s_scr[...]

        def step(tt, dst):
            kk = kk_ref[tt]
            sas = []
            for vd in range(VD):
                sa = _colsum(sall_ref[tt, pl.ds(vd * HEAD, HEAD), :] * kk)
                sa_ref[tt, pl.ds(vd, 1), :] = sa
                sas.append(sa)
            w, b, k2, r = w_ref[tt], b_ref[tt], k2_ref[tt], r_ref[tt]
            for vd in range(VD):
                rows = pl.ds(vd * HEAD, HEAD)
                s = sall_ref[tt, rows, :] * w - sas[vd] * b + v_ref[tt, pl.ds(vd, 1), :] * k2
                dst[rows, :] = s
                y_ref[tt, pl.ds(vd, 1), :] = _colsum(s * r)

        def loop_step(tt, carry):
            step(tt, sall_ref.at[tt + 1])
            return carry

        lax.fori_loop(0, tb - 1, loop_step, 0)
        step(tb - 1, s_scr)

    vl = jax.ShapeDtypeStruct((t, VD, LANES), F32)
    return _compute_call(
        body, (r4, w4, k24, kk4, b4, v), name=name, grid=(t // tb,),
        in_specs=[_k_spec(HEAD, tb)] * 5 + [_k_spec(VD, tb)],
        out_specs=[_k_spec(VD, tb), _k_spec(S_ROWS, tb), _k_spec(VD, tb)],
        out_shape=[vl, jax.ShapeDtypeStruct((t, S_ROWS, LANES), F32), vl],
        scratch_shapes=[pltpu.VMEM((S_ROWS, LANES), F32)], semantics=("arbitrary",), rider=rider)


def _rwkv_scan_bwd(dy, s_all, sa_all, r4, w4, k24, kk4, b4, v, name, rider=None):
    t = r4.shape[0]
    tb = TB_SCAN
    nb = t // tb
    blk = lambda i: nb - 1 - i

    def body(dy_ref, sall_ref, sa_ref, r_ref, w_ref, k2_ref, kk_ref, b_ref, v_ref,
             dr_ref, dw_ref, dk2_ref, dkk_ref, db_ref, dv_ref, ds_scr):
        @pl.when(pl.program_id(0) == 0)
        def _():
            ds_scr[...] = jnp.zeros_like(ds_scr)

        def step(j, carry):
            tt = tb - 1 - j
            vrow = lambda ref, vd: ref[tt, pl.ds(vd, 1), :]
            srows = lambda vd: pl.ds(vd * HEAD, HEAD)
            r, k2, b = r_ref[tt], k2_ref[tt], b_ref[tt]
            dsas = []
            for vd in range(VD):
                ds = ds_scr[srows(vd), :] + vrow(dy_ref, vd) * r
                ds_scr[srows(vd), :] = ds
                dv_ref[tt, pl.ds(vd, 1), :] = _colsum(ds * k2)
                dsas.append(-_colsum(ds * b))
            zero = jnp.zeros((HEAD, LANES), F32)
            dk2, q, sady, vdy = zero, zero, 0.0, 0.0
            for vd in range(VD):
                dyv = vrow(dy_ref, vd)
                dk2 = dk2 + ds_scr[srows(vd), :] * vrow(v_ref, vd)
                q = q + sall_ref[tt, srows(vd), :] * dyv
                sady = sady + vrow(sa_ref, vd) * dyv
                vdy = vdy + vrow(v_ref, vd) * dyv
            dk2_ref[tt] = dk2
            dr_ref[tt] = w_ref[tt] * q - b_ref[tt] * sady + k2_ref[tt] * vdy
            dw, dkk = zero, zero
            for vd in range(VD):
                sp = sall_ref[tt, srows(vd), :]
                dw = dw + ds_scr[srows(vd), :] * sp
                dkk = dkk + sp * dsas[vd]
            dw_ref[tt] = dw
            dkk_ref[tt] = dkk
            w, kk = w_ref[tt], kk_ref[tt]
            db = zero
            for vd in range(VD):
                ds = ds_scr[srows(vd), :]
                db = db - ds * vrow(sa_ref, vd)
                ds_scr[srows(vd), :] = ds * w + dsas[vd] * kk
            db_ref[tt] = db
            return carry

        lax.fori_loop(0, tb, step, 0)

    rk = lambda rows: pl.BlockSpec((tb, rows, LANES), lambda i: (blk(i), 0, 0))
    big = jax.ShapeDtypeStruct((t, HEAD, LANES), F32)
    return _compute_call(
        body, (dy, s_all, sa_all, r4, w4, k24, kk4, b4, v), name=name, grid=(nb,),
        in_specs=[rk(VD), rk(S_ROWS), rk(VD)] + [rk(HEAD)] * 5 + [rk(VD)],
        out_specs=[rk(HEAD)] * 5 + [rk(VD)],
        out_shape=[big] * 5 + [jax.ShapeDtypeStruct((t, VD, LANES), F32)],
        scratch_shapes=[pltpu.VMEM((S_ROWS, LANES), F32)], semantics=("arbitrary",), rider=rider)


def _rwkv_mixer_fwd(p_main, p_lo, prm, tag, rider):
    r, k, v, w, a, z, xs_lo = _rwkv_pre_fwd(p_main, p_lo, prm["mu_main"], prm["mu_lo"], prm["w0"], prm["a0"],
                                            prm["wl"], prm["al"], tag + "_pre")
    kk4, k24, b4, r4, w4, rk = _rwkv_kprep_fwd(k, a, r, w, prm["kkp"], prm["kap"], prm["rkp"], tag + "_kprep")
    (y, s_all, sa_all), ridden = _ridden(_rwkv_scan_fwd(r4, w4, k24, kk4, b4, v, tag + "_scan", rider), rider)
    o = _rwkv_post_fwd(y, v, rk, prm["gn_g"], prm["gn_b"], tag + "_post")
    u = _rwkv_gate_fwd(o, z, tag + "_gate")
    saved = dict(z=z, xs_lo=xs_lo, r=r, k=k, a=a, v=v, r4=r4, w4=w4, kk4=kk4, k24=k24, b4=b4, rk=rk,
                 y=y, s_all=s_all, sa_all=sa_all, o=o)
    return u, saved, ridden


def _rwkv_mixer_bwd(p_main, p_lo, prm, sv, du, tag, rider):
    do, dz = _rwkv_gate_bwd(sv["o"], sv["z"], du, tag + "_gate_b")
    dy, dv_post, drk, dgn_g, dgn_b = _rwkv_post_bwd(sv["y"], sv["v"], sv["rk"], prm["gn_g"], prm["gn_b"], do,
                                                    tag + "_post_b")
    (dr_s, dw_s, dk24, dkk4, db4, dv_scan), ridden = _ridden(_rwkv_scan_bwd(
        dy, sv["s_all"], sv["sa_all"], sv["r4"], sv["w4"], sv["k24"], sv["kk4"], sv["b4"], sv["v"], tag + "_scan_b", rider), rider)
    dk, da, dr, dw, dkkp, dkap, drkp = _rwkv_kprep_bwd(sv["k"], sv["a"], sv["r"], prm["kkp"], prm["kap"], prm["rkp"],
                                                       dkk4, dk24, db4, drk, dr_s, dw_s, tag + "_kprep_b")
    dxs_lo, dw0, da0, dwl, dal = _rwkv_lora_bwd(sv["xs_lo"], prm["w0"], prm["a0"], prm["wl"], prm["al"], dw, da,
                                                tag + "_lora_b")
    dxs_main = [[dr], [dk], [dv_post, dv_scan], [dz]]
    dp_main, dmu_main = _lerp_bwd(p_main, dxs_main, prm["mu_main"], tag + "_lerp_main_b")
    dp_lo, dmu_lo = _lerp_bwd(p_lo, [[dxs_lo]], prm["mu_lo"], tag + "_lerp_lo_b")
    grads = dict(mu_main=dmu_main, mu_lo=dmu_lo, w0=dw0, a0=da0, wl=dwl, al=dal, kkp=dkkp, kap=dkap, rkp=drkp,
                 gn_g=dgn_g, gn_b=dgn_b)
    return dp_main, dp_lo, grads, ridden


N_DEV = 8
N_CHIPS = 4
ANY = pl.BlockSpec(memory_space=pl.ANY)


def _place():
    return lax.axis_index("x"), lax.axis_index("y"), lax.axis_index("c")


def _remote(src, dst, send_sems, recv_sems, k, dev):
    return pltpu.make_async_remote_copy(src_ref=src, dst_ref=dst, send_sem=send_sems.at[k], recv_sem=recv_sems.at[k],
                                        device_id=dev, device_id_type=MESHT)


def _all_gather8(v, name):
    def body(buf_ref, out_ref, send_sems, recv_sems):
        del buf_ref
        x, y, c = _place()
        mine = out_ref.at[4 * x + 2 * y + c]
        peers = [(x ^ (k >> 2), y ^ ((k >> 1) & 1), c ^ (k & 1)) for k in range(1, N_DEV)]
        sends = [_remote(mine, mine, send_sems, recv_sems, k, peer) for k, peer in enumerate(peers)]
        for cp in sends:
            cp.start()
        for k, (px, py, pc) in enumerate(peers):
            _remote(mine, out_ref.at[4 * px + 2 * py + pc], send_sems, recv_sems, k, (x, y, c)).wait_recv()
        for cp in sends:
            cp.wait_send()

    return pl.pallas_call(
        body, name=name, in_specs=[ANY], out_specs=ANY, input_output_aliases={0: 0},
        out_shape=jax.ShapeDtypeStruct((N_DEV,) + v.shape, v.dtype),
        scratch_shapes=[pltpu.SemaphoreType.DMA((N_DEV - 1,)), pltpu.SemaphoreType.DMA((N_DEV - 1,))],
    )(jnp.broadcast_to(v[None], (N_DEV,) + v.shape))


def _other_chips(x, y):
    return [(1 - x, y), (x, 1 - y), (1 - x, 1 - y)]


GATHER_SEMS = 6


def _gather_buffer(v):
    return jnp.broadcast_to(v[None], (N_CHIPS,) + v.shape)


def _gather_start(bufs, send_sems, recv_sems):
    x, y, c = _place()
    for i, buf in enumerate(bufs):
        mine = buf.at[2 * x + y, c]
        for j, (cx, cy) in enumerate(_other_chips(x, y)):
            _remote(mine, mine, send_sems, recv_sems, GATHER_SEMS * i + j, (cx, cy, c)).start()


def _gather_finish(bufs, send_sems, recv_sems):
    x, y, c = _place()
    chips = _other_chips(x, y)
    passed = []
    for i, buf in enumerate(bufs):
        mine = buf.at[2 * x + y, c]
        for j, (cx, cy) in enumerate(chips):
            landed = buf.at[2 * cx + cy, c]
            _remote(mine, landed, send_sems, recv_sems, GATHER_SEMS * i + j, (x, y, c)).wait_recv()
            fwd = _remote(landed, landed, send_sems, recv_sems, GATHER_SEMS * i + 3 + j, (x, y, 1 - c))
            fwd.start()
            passed.append(fwd)
    for i, buf in enumerate(bufs):
        mine = buf.at[2 * x + y, c]
        for j, (cx, cy) in enumerate(chips):
            _remote(mine, buf.at[2 * cx + cy, 1 - c], send_sems, recv_sems, GATHER_SEMS * i + 3 + j, (x, y, c)).wait_recv()
            _remote(mine, mine, send_sems, recv_sems, GATHER_SEMS * i + j, (cx, cy, c)).wait_send()
    for fwd in passed:
        fwd.wait_send()


def _gather_rider(bufs):
    return _Rider(bufs, GATHER_SEMS * len(bufs), _gather_start, _gather_finish)


def _chip_gather(bufs, name):
    n = len(bufs)

    def body(*refs):
        out_refs, (send_sems, recv_sems) = refs[n:2 * n], refs[2 * n:]
        _gather_start(out_refs, send_sems, recv_sems)
        _gather_finish(out_refs, send_sems, recv_sems)

    return pl.pallas_call(
        body, name=name, in_specs=[ANY] * n, out_specs=[ANY] * n, input_output_aliases={i: i for i in range(n)},
        out_shape=[jax.ShapeDtypeStruct(b.shape, b.dtype) for b in bufs], scratch_shapes=_dma_sems(GATHER_SEMS * n),
    )(*bufs)


RS_W = 1024
RS_BLOCK_BYTES = 4 << 20


def _dma_sems(n):
    return [pltpu.SemaphoreType.DMA((n,)), pltpu.SemaphoreType.DMA((n,))]


def _pair_exchange_copies(refs, send_sems, recv_sems):
    n = len(refs) // 2
    x, y, c = _place()
    return [_remote(refs[i].at[s, 1 - c], refs[n + i].at[s], send_sems, recv_sems, N_CHIPS * i + s, (x, y, 1 - c))
            for i in range(n) for s in range(N_CHIPS)]


def _pair_exchange_start(refs, send_sems, recv_sems):
    for cp in _pair_exchange_copies(refs, send_sems, recv_sems):
        cp.start()


def _pair_exchange_finish(refs, send_sems, recv_sems):
    for cp in _pair_exchange_copies(refs, send_sems, recv_sems):
        cp.wait()


def _pair_exchange_rider(gs):
    landing = [lax.empty((N_CHIPS,) + g.shape[2:], g.dtype) for g in gs]
    return _Rider(list(gs) + landing, N_CHIPS * len(gs), _pair_exchange_start, _pair_exchange_finish)


def _rs_rows(rows, cols):
    cap = max(16, RS_BLOCK_BYTES // (N_CHIPS * 4 * cols))
    return rows if rows <= cap else max(d for d in range(16, cap + 1, 16) if rows % d == 0)


def _rs_pair_add(g, got, c_arr, name):
    _, _, rows, width = g.shape
    tr = _rs_rows(rows, width)

    def body(c_ref, g_ref, got_ref, p_ref):
        p_ref[...] = (g_ref[...] + got_ref[...]).astype(BF16)

    return pl.pallas_call(
        body, name=name,
        grid_spec=pltpu.PrefetchScalarGridSpec(
            num_scalar_prefetch=1, grid=(rows // tr,),
            in_specs=[pl.BlockSpec((N_CHIPS, None, tr, width), lambda i, c_ref: (0, c_ref[0], i, 0)),
                      pl.BlockSpec((N_CHIPS, tr, width), lambda i, c_ref: (0, i, 0))],
            out_specs=pl.BlockSpec((N_CHIPS, tr, width), lambda i, c_ref: (0, i, 0))),
        out_shape=jax.ShapeDtypeStruct((N_CHIPS, rows, width), BF16), compiler_params=_cparams("parallel"),
    )(c_arr, g, got)


def _chip_exchange_copies(refs, send_sems, recv_sems):
    n = len(refs) // 2
    x, y, c = _place()
    return [_remote(refs[i].at[2 * cx + cy], refs[n + i].at[j], send_sems, recv_sems, 3 * i + j, (cx, cy, c))
            for i in range(n) for j, (cx, cy) in enumerate(_other_chips(x, y))]


def _chip_exchange_start(refs, send_sems, recv_sems):
    for cp in _chip_exchange_copies(refs, send_sems, recv_sems):
        cp.start()


def _chip_exchange_finish(refs, send_sems, recv_sems):
    n = len(refs) // 2
    x, y, c = _place()
    for i in range(n):
        for j in range(3):
            _remote(refs[i].at[2 * x + y], refs[n + i].at[j], send_sems, recv_sems, 3 * i + j, (x, y, c)).wait_recv()
    for cp in _chip_exchange_copies(refs, send_sems, recv_sems):
        cp.wait_send()


def _chip_exchange_buffers(ps):
    return [lax.empty((3,) + p.shape[1:], p.dtype) for p in ps]


def _chip_exchange_rider(ps):
    return _Rider(list(ps) + _chip_exchange_buffers(ps), 3 * len(ps), _chip_exchange_start, _chip_exchange_finish)


def _rs_chip_add(p, q, idx, name):
    _, rows, width = q.shape
    tr = _rs_rows(rows, width)

    def body(idx_ref, p_ref, q_ref, r_ref):
        qv = q_ref[...].astype(F32)
        r_ref[...] = ((p_ref[...].astype(F32) + qv[0]) + qv[1]) + qv[2]

    return pl.pallas_call(
        body, name=name,
        grid_spec=pltpu.PrefetchScalarGridSpec(
            num_scalar_prefetch=1, grid=(rows // tr,),
            in_specs=[pl.BlockSpec((None, tr, width), lambda i, idx_ref: (idx_ref[0], i, 0)),
                      pl.BlockSpec((3, tr, width), lambda i, idx_ref: (0, i, 0))],
            out_specs=pl.BlockSpec((None, tr, width), lambda i, idx_ref: (idx_ref[1], i, 0))),
        out_shape=jax.ShapeDtypeStruct((2, rows, width), F32), compiler_params=_cparams("parallel"),
    )(idx, p, q)


def _rs_pair_share(rs, name):
    n = len(rs)

    def body(*refs):
        out_refs, (send_sems, recv_sems) = refs[n:2 * n], refs[2 * n:]
        x, y, c = _place()
        sends = [_remote(out_refs[i].at[c], out_refs[i].at[c], send_sems, recv_sems, i, (x, y, 1 - c)) for i in range(n)]
        for cp in sends:
            cp.start()
        for i in range(n):
            _remote(out_refs[i].at[c], out_refs[i].at[1 - c], send_sems, recv_sems, i, (x, y, c)).wait_recv()
        for cp in sends:
            cp.wait_send()

    return pl.pallas_call(
        body, name=name, in_specs=[ANY] * n, out_specs=[ANY] * n, input_output_aliases={i: i for i in range(n)},
        out_shape=[jax.ShapeDtypeStruct(r.shape, r.dtype) for r in rs], scratch_shapes=_dma_sems(n),
    )(*rs)


def _rs_pair_sums(gs, gots, core, tag):
    c_arr = core.astype(jnp.int32).reshape(1)
    return [_rs_pair_add(g, got, c_arr, f"{tag}_pair_add{i}") for i, (g, got) in enumerate(zip(gs, gots))]


def _rs_finish(ps, qs, chip, core, tag):
    idx = jnp.stack([chip, core]).astype(jnp.int32)
    rs = [_rs_chip_add(p, q, idx, f"{tag}_chip_add{i}") for i, (p, q) in enumerate(zip(ps, qs))]
    return _rs_pair_share(rs, tag + "_share")


def _sum_leading(a, name):
    n, rows, width = a.shape
    cap = max(8, RS_BLOCK_BYTES // (n * 4 * width))
    tr = rows if rows <= cap else max(d for d in range(8, cap + 1, 8) if rows % d == 0)

    def body(a_ref, o_ref):
        acc = a_ref[0]
        for d in range(1, n):
            acc = acc + a_ref[d]
        o_ref[...] = acc

    return pl.pallas_call(
        body, name=name, grid=(rows // tr,), in_specs=[pl.BlockSpec((n, tr, width), lambda i: (0, i, 0))],
        out_specs=pl.BlockSpec((tr, width), lambda i: (i, 0)), out_shape=jax.ShapeDtypeStruct((rows, width), F32),
        compiler_params=_cparams("parallel"),
    )(a)


def _pair_swap(v, name):
    def body(v_ref, got_ref, send_sems, recv_sems):
        x, y, c = _place()
        cp = _remote(v_ref, got_ref, send_sems, recv_sems, 0, (x, y, 1 - c))
        cp.start()
        cp.wait()

    return pl.pallas_call(body, name=name, in_specs=[ANY], out_specs=ANY, out_shape=jax.ShapeDtypeStruct(v.shape, v.dtype),
                          scratch_shapes=_dma_sems(1))(v)


def _replicated_pair_sum(v, name):
    rows, width = v.shape
    pair = _sum_leading(jnp.stack([v, _pair_swap(v, name + "_swap")]), name + "_pair_add")
    return _gather_buffer(pair.reshape(2, rows // 2, width))


def _replicated_chip_sum(gathered, name):
    return _sum_leading(gathered.reshape(N_CHIPS, -1, gathered.shape[-1]), name + "_chip_add")


MOD_COLS = 3 * D // N_CHIPS
MOD_TK = 512


def _mod_partial(c_all, mod_w, name):
    nk = D // MOD_TK

    def body(c_ref, w_ref, o_ref):
        l = pl.program_id(1)
        part = _bdot_nn(jax.nn.silu(c_ref[...]), w_ref[0])
        _accum(o_ref.at[0], part, l == 0)

    return pl.pallas_call(
        body, name=name, grid=(DEPTH, nk),
        in_specs=[pl.BlockSpec((N_DEV, MOD_TK), lambda i, l: (0, l)), pl.BlockSpec((1, MOD_TK, MOD_COLS), lambda i, l: (i, l, 0))],
        out_specs=pl.BlockSpec((1, N_DEV, MOD_COLS), lambda i, l: (i, 0, 0)),
        out_shape=jax.ShapeDtypeStruct((DEPTH, N_DEV, MOD_COLS), F32), compiler_params=_cparams("parallel", "arbitrary"),
    )(c_all, mod_w)


def _mod_w_grad(c_all, dmod, name):
    def body(c_ref, d_ref, o_ref):
        o_ref[0] = _dg(jax.nn.silu(c_ref[...]).astype(BF16), d_ref[0].astype(BF16), _TN)

    return pl.pallas_call(
        body, name=name, grid=(DEPTH, D // MOD_TK),
        in_specs=[pl.BlockSpec((N_DEV, MOD_TK), lambda i, l: (0, l)), pl.BlockSpec((1, N_DEV, MOD_COLS), lambda i, l: (i, 0, 0))],
        out_specs=pl.BlockSpec((1, MOD_TK, MOD_COLS), lambda i, l: (i, l, 0)),
        out_shape=jax.ShapeDtypeStruct((DEPTH, D, MOD_COLS), F32), compiler_params=_cparams("parallel", "parallel"),
    )(c_all, dmod)


ADAM_BLOCK_BYTES = 1 << 20


def _adamw(w, g, m, v, name, rider=None):
    shape = w.shape
    cols = shape[-1]
    rows = w.size // cols
    w, g, m, v = (a.reshape(rows, cols) for a in (w, g, m, v))
    cap = max(8, ADAM_BLOCK_BYTES // (4 * cols))
    tr = rows if rows <= cap else max(d for d in range(8, cap + 1, 8) if rows % d == 0)
    c1 = 1.0 - ADAM_B1 ** ADAM_STEP
    c2 = 1.0 - ADAM_B2 ** ADAM_STEP

    def body(w_ref, g_ref, m_ref, v_ref, d_ref, nm_ref, nv_ref):
        gv = g_ref[...]
        mn = ADAM_B1 * m_ref[...] + (1.0 - ADAM_B1) * gv
        vn = ADAM_B2 * v_ref[...] + (1.0 - ADAM_B2) * (gv * gv)
        nm_ref[...] = mn
        nv_ref[...] = vn
        d_ref[...] = -ADAM_LR * ((mn / c1) / (jnp.sqrt(vn / c2) + ADAM_EPS) + ADAM_WD * w_ref[...])

    spec = pl.BlockSpec((tr, cols), lambda i: (i, 0))
    out = jax.ShapeDtypeStruct((rows, cols), F32)
    (d, nm, nv), ridden = _ridden(_compute_call(
        body, (w, g, m, v), name=name, grid=(rows // tr,), in_specs=[spec] * 4, out_specs=[spec] * 3, out_shape=[out] * 3,
        semantics=("parallel",), rider=rider), rider)
    res = (d.reshape(shape), nm.reshape(shape), nv.reshape(shape))
    return res if rider is None else (res, ridden)


W_NAMES = ("norm_g", "mod_w", "mod_b", "final_norm_g", "sg_w_in", "sg_w_out", "sg_ln_g", "sg_ln_b", "sg_w_spatial",
           "sg_b_spatial", "swa_w_in", "swa_w_out", "swa_sinks", "rwkv_w_in", "rwkv_w_out", "rwkv_mu", "rwkv_w0",
           "rwkv_w_lora", "rwkv_a0", "rwkv_a_lora", "rwkv_k_k", "rwkv_k_a", "rwkv_r_k", "rwkv_gn_g", "rwkv_gn_b")
SMALL = {"sg_ln_g": 1, "sg_ln_b": 1, "rwkv_mu": 1, "rwkv_w0": 1, "rwkv_w_lora": 2, "rwkv_a0": 1, "rwkv_a_lora": 2,
         "rwkv_k_k": 1, "rwkv_k_a": 1, "rwkv_gn_g": 1, "rwkv_gn_b": 1}
REPLICATED = ("norm_g", "final_norm_g", "sg_w_spatial", "sg_b_spatial", "swa_sinks", "rwkv_r_k")
KINDS = ("sg", "swa", "rwkv", "sg")


def _pad_to(flat, n):
    return jnp.pad(flat, (0, n - flat.shape[0]))


def _round_up(n, m):
    return -(-n // m) * m


def _join_shards(gathered, axis):
    return jnp.concatenate([gathered[s] for s in range(N_CHIPS)], axis=axis)


def _chip_blocks(full, axis):
    return jnp.stack(jnp.split(full, N_CHIPS, axis=axis)).reshape(N_CHIPS, -1)


def _weight_buffer(w):
    rows, cols = w.shape
    return _gather_buffer(w.astype(BF16).reshape(2, rows // 2, cols))


def _chip_shards(buf):
    return buf.reshape(N_CHIPS, -1, buf.shape[-1])


def _small_buffer(shards):
    flat = jnp.concatenate([shards[n].reshape(-1) for n in SMALL])
    rows = _round_up(flat.shape[0], 2 * 8 * LANES) // (2 * LANES)
    return _gather_buffer(_pad_to(flat, 2 * rows * LANES).reshape(2, rows, LANES))


def _unpack_small(buf, shards):
    got = buf.reshape(N_CHIPS, -1)
    out, off = {}, 0
    for n, axis in SMALL.items():
        size = shards[n].size
        out[n] = _join_shards(got[:, off:off + size].reshape((N_CHIPS,) + shards[n].shape), axis)
        off += size
    return out


def _lora_pad_rows(w):
    return jnp.pad(w, ((0, LORA_PAD - LORA), (0, 0)))


def _lo_cols(a):
    z = jnp.zeros(a.shape[:-1] + (LORA_PAD - LORA,), a.dtype)
    return jnp.concatenate([a[..., :LORA], z, a[..., LORA:], z], axis=-1)


def _lo_cols_inv(a):
    return jnp.concatenate([a[..., :LORA], a[..., LORA_PAD:LORA_PAD + LORA]], axis=-1)


def _transposed(w_blocks):
    return jnp.swapaxes(w_blocks, 1, 2).reshape(-1, w_blocks.shape[1])


def _rows_dim_major(w):
    return w.reshape(N_HEADS, HEAD, -1).swapaxes(0, 1).reshape(w.shape)


def _rows_head_major(w):
    return w.reshape(HEAD, N_HEADS, -1).swapaxes(0, 1).reshape(w.shape)


def kernel(x, c, positions, norm_g, mod_w, mod_b, final_norm_g, sg_w_in, sg_w_out, sg_ln_g, sg_ln_b, sg_w_spatial,
           sg_b_spatial, swa_w_in, swa_w_out, swa_sinks, rwkv_w_in, rwkv_w_out, rwkv_mu, rwkv_w0, rwkv_w_lora, rwkv_a0,
           rwkv_a_lora, rwkv_k_k, rwkv_k_a, rwkv_r_k, rwkv_gn_g, rwkv_gn_b, loss_target, m_norm_g, m_mod_w, m_mod_b,
           m_final_norm_g, m_sg_w_in, m_sg_w_out, m_sg_ln_g, m_sg_ln_b, m_sg_w_spatial, m_sg_b_spatial, m_swa_w_in,
           m_swa_w_out, m_swa_sinks, m_rwkv_w_in, m_rwkv_w_out, m_rwkv_mu, m_rwkv_w0, m_rwkv_w_lora, m_rwkv_a0,
           m_rwkv_a_lora, m_rwkv_k_k, m_rwkv_k_a, m_rwkv_r_k, m_rwkv_gn_g, m_rwkv_gn_b, v_norm_g, v_mod_w, v_mod_b,
           v_final_norm_g, v_sg_w_in, v_sg_w_out, v_sg_ln_g, v_sg_ln_b, v_sg_w_spatial, v_sg_b_spatial, v_swa_w_in,
           v_swa_w_out, v_swa_sinks, v_rwkv_w_in, v_rwkv_w_out, v_rwkv_mu, v_rwkv_w0, v_rwkv_w_lora, v_rwkv_a0,
           v_rwkv_a_lora, v_rwkv_k_k, v_rwkv_k_a, v_rwkv_r_k, v_rwkv_gn_g, v_rwkv_gn_b):
    given = dict(locals())
    w = {n: given[n] for n in W_NAMES}
    xi, yi, ci = _place()
    chip = 2 * xi + yi
    me = 4 * xi + 2 * yi + ci
    xs = [x[0]]

    c_all = _all_gather8(c, "gather_c")[:, 0, :]
    mod_part = _mod_partial(c_all, mod_w, "mod_fwd")
    mod_all = _all_gather8(mod_part, "gather_mod")[::2]
    mod_mine = lax.dynamic_index_in_dim(mod_all, me, axis=2, keepdims=False)
    mod = mod_mine.transpose(1, 0, 2).reshape(DEPTH, 3 * D) + mod_b
    shift, scale, gate = mod[:, :D], mod[:, D:2 * D], mod[:, 2 * D:]

    shards = {"sg_w_in0": sg_w_in[0], "sg_w_out0": sg_w_out[0], "swa_w_in": swa_w_in[0], "swa_w_out": swa_w_out[0],
              "rwkv_w_in_a": rwkv_w_in[0, :D // 2], "rwkv_w_in_b": rwkv_w_in[0, D // 2:], "rwkv_w_out": rwkv_w_out[0],
              "sg_w_in1": sg_w_in[1], "sg_w_out1": sg_w_out[1]}
    bufs = {n: _weight_buffer(s) for n, s in shards.items()}
    fwd_riders = {(0, "in"): ["swa_w_in"], (0, "mix"): ["swa_w_out"], (1, "in"): ["rwkv_w_in_a"], (1, "mix"): ["rwkv_w_in_b"],
                  (2, "in"): ["rwkv_w_out"], (2, "mix"): ["sg_w_in1", "sg_w_out1"]}
    bufs["sg_w_in0"], bufs["sg_w_out0"], small_buf = _chip_gather(
        [bufs["sg_w_in0"], bufs["sg_w_out0"], _small_buffer(w)], "gather_l0")
    full = _unpack_small(small_buf, w)

    def riding(i, where):
        names = fwd_riders.get((i, where))
        return names, (None if names is None else _gather_rider([bufs[n] for n in names]))

    def arrived(names, ridden):
        for n, b in zip(names or [], ridden):
            bufs[n] = b

    sg_in = lambda j: _chip_shards(bufs[f"sg_w_in{j}"])
    sg_out = lambda j: bufs[f"sg_w_out{j}"].reshape(D, D)
    mu = full["rwkv_mu"][0]
    rw_prm = dict(mu_main=_dim_major(mu[:RW_MAIN].reshape(4, D)).reshape(1, RW_MAIN), mu_lo=_lo_cols(mu[None, RW_MAIN:]),
                  w0=_dim_major(full["rwkv_w0"]), a0=_dim_major(full["rwkv_a0"]),
                  wl=_lora_pad_rows(_dim_major(full["rwkv_w_lora"][0])), al=_lora_pad_rows(_dim_major(full["rwkv_a_lora"][0])),
                  kkp=_param_compact(full["rwkv_k_k"][0]), kap=_param_compact(full["rwkv_k_a"][0]),
                  rkp=_param_compact(rwkv_r_k.reshape(-1)),
                  gn_g=_param_compact(full["rwkv_gn_g"][0]), gn_b=_param_compact(full["rwkv_gn_b"][0]))
    bs_t = [jnp.pad(sg_b_spatial[j].T, ((0, 0), (0, LANES - SG_GROUPS))) for j in range(2)]
    sink_row = jnp.pad(swa_sinks, ((0, 0), (0, LANES - N_HEADS)))
    inv_freq = ROPE_THETA ** (-jnp.arange(HEAD // 2, dtype=F32) / (HEAD // 2))
    ang = positions[0].astype(F32)[:, None] * inv_freq
    cos, sin = jnp.tile(jnp.cos(ang), (1, LANES * 2 // HEAD)), jnp.tile(jnp.sin(ang), (1, LANES * 2 // HEAD))

    def row(a, i):
        return a[i:i + 1]

    hs, ps, us, ys, rw_saved = [], [], [], [], None
    for i, kind in enumerate(KINDS):
        j = i // 3
        tag = f"l{i}_{kind}"
        h = _norm_mod_fwd(xs[i], row(norm_g, i), row(shift, i), row(scale, i), tag + "_norm")
        names_in, rider_in = riding(i, "in")
        names_mix, rider_mix = riding(i, "mix")
        if kind == "sg":
            p, ridden = _ridden(_matmul(h, sg_in(j), "nn", tag + "_in", blocked=True, rider=rider_in), rider_in)
            arrived(names_in, ridden)
            u, ridden = _ridden(_sg_fwd(p, row(full["sg_ln_g"], j), row(full["sg_ln_b"], j), sg_w_spatial[j], bs_t[j],
                                        tag + "_mix", rider_mix), rider_mix)
            w_out = sg_out(j)
        elif kind == "swa":
            swa_in, swa_out = _chip_shards(bufs["swa_w_in"]), bufs["swa_w_out"].reshape(D, D)
            p, ridden = _ridden(_matmul(h, swa_in, "nn", tag + "_in", blocked=True, rider=rider_in), rider_in)
            arrived(names_in, ridden)
            u, ridden = _ridden(_swa_fwd(p, cos, sin, sink_row, tag + "_mix", rider_mix), rider_mix)
            w_out = swa_out
        else:
            rw_in = jnp.concatenate([_join_shards(_chip_shards(bufs[n]), axis=1) for n in ("rwkv_w_in_a", "rwkv_w_in_b")])
            rw_main = _dim_major(rw_in[:, :RW_MAIN].reshape(D, 4, D)).reshape(D, RW_MAIN)
            rw_lo = _lo_cols(rw_in[:, RW_MAIN:])
            p_main, ridden = _ridden(_matmul(h, rw_main, "nn", tag + "_in", rider=rider_in), rider_in)
            arrived(names_in, ridden)
            rw_out = _rows_dim_major(bufs["rwkv_w_out"].reshape(D, D))
            p = (p_main, _matmul(h, rw_lo, "nn", tag + "_in_lo"))
            u, rw_saved, ridden = _rwkv_mixer_fwd(p[0], p[1], rw_prm, tag, rider_mix)
            w_out = rw_out
        arrived(names_mix, ridden)
        y, x_next = _out_proj_resid(u, w_out, xs[i], row(gate, i), tag + "_out")
        xs.append(x_next)
        hs.append(h), ps.append(p), us.append(u), ys.append(y)

    loss_part, dx, d_final_g, dy, d_gate = _final_loss_grad(xs[DEPTH], final_norm_g[None], loss_target[0], ys[DEPTH - 1],
                                                            row(gate, DEPTH - 1), "loss")
    loss = lax.psum(loss_part[0, 0], ("x", "y", "c"))

    gfull = {n: [None, None] for n in ("sg_ln_g", "sg_ln_b", "sg_w_spatial", "sg_b_spatial")}
    gbig = {}
    d_norm_g, d_mod = [None] * DEPTH, [None] * DEPTH
    rs_p, rs_q, riding_names = {}, {}, []

    for i in reversed(range(DEPTH)):
        kind, j = KINDS[i], i // 3
        tag = f"l{i}_{kind}_b"
        rider = _chip_exchange_rider([rs_p[n] for n in riding_names]) if riding_names else None
        w_out = {"sg": sg_out(j), "swa": swa_out, "rwkv": rw_out}[kind]
        du = _matmul(dy, w_out, "nt", tag + "_du")
        dw_out = _matmul(us[i], dy, "tn", tag + "_dwout").reshape(N_CHIPS, D // N_CHIPS, D)
        if kind == "sg":
            (dp, dlg, dlb, dws, dbs), ridden = _ridden(
                _sg_bwd(ps[i], row(full["sg_ln_g"], j), row(full["sg_ln_b"], j), sg_w_spatial[j], bs_t[j], du,
                        tag + "_mix", rider), rider)
            gfull["sg_ln_g"][j], gfull["sg_ln_b"][j] = dlg[0], dlb[0]
            gfull["sg_w_spatial"][j], gfull["sg_b_spatial"][j] = dws, dbs[:, :SG_GROUPS].T
            gbig[f"sg_w_in{j}"] = _matmul(hs[i], dp, "tn", tag + "_dwin", blocked=True)
            gbig[f"sg_w_out{j}"] = dw_out
            dh, dh2 = _matmul(dp, _transposed(sg_in(j)), "nn", tag + "_dh"), None
            mine = [f"sg_w_in{j}", f"sg_w_out{j}"]
        elif kind == "swa":
            (dp, dsk), ridden = _ridden(_swa_bwd(ps[i], cos, sin, sink_row, du, tag + "_mix", rider), rider)
            gfull["swa_sinks"] = dsk[:, :N_HEADS]
            gbig["swa_w_in"] = _matmul(hs[i], dp, "tn", tag + "_dwin", blocked=True)
            gbig["swa_w_out"] = dw_out
            dh, dh2 = _matmul(dp, _transposed(swa_in), "nn", tag + "_dh"), None
            mine = ["swa_w_in", "swa_w_out"]
        else:
            dpm, dpl, rg, ridden = _rwkv_mixer_bwd(ps[i][0], ps[i][1], rw_prm, rw_saved, du, tag, rider)
            mine = ["rwkv_w_in", "rwkv_w_out"]
            dw_main = _matmul(hs[i], dpm, "tn", tag + "_dwin")
            dw_lo = _matmul(hs[i], dpl, "tn", tag + "_dwin_lo")
            dw_main = _head_major(dw_main.reshape(D, 4, D)).reshape(D, RW_MAIN)
            dw_in = jnp.concatenate([dw_main, _lo_cols_inv(dw_lo)], axis=1)
            gbig["rwkv_w_in"] = dw_in.reshape(D, N_CHIPS, -1).transpose(1, 0, 2)
            gbig["rwkv_w_out"] = _rows_head_major(dw_out.reshape(D, D)).reshape(dw_out.shape)
            dmu_main = _head_major(rg["mu_main"].reshape(4, D)).reshape(1, RW_MAIN)
            gfull["rwkv_mu"] = jnp.concatenate([dmu_main, _lo_cols_inv(rg["mu_lo"])], axis=1)
            gfull["rwkv_w0"], gfull["rwkv_a0"] = _head_major(rg["w0"]), _head_major(rg["a0"])
            gfull["rwkv_w_lora"], gfull["rwkv_a_lora"] = _head_major(rg["wl"])[None, :LORA], _head_major(rg["al"])[None, :LORA]
            gfull["rwkv_k_k"], gfull["rwkv_k_a"] = _param_compact_inv(rg["kkp"])[None], _param_compact_inv(rg["kap"])[None]
            gfull["rwkv_r_k"] = _param_compact_inv(rg["rkp"]).reshape(1, N_HEADS, HEAD)
            gfull["rwkv_gn_g"], gfull["rwkv_gn_b"] = _param_compact_inv(rg["gn_g"])[None], _param_compact_inv(rg["gn_b"])[None]
            dh, dh2 = _matmul(dpm, rw_main, "nt", tag + "_dh"), _matmul(dpl, rw_lo, "nt", tag + "_dh_lo")
        rs_p.update(zip(riding_names, ridden[:len(riding_names)]))
        rs_q.update(zip(riding_names, ridden[len(riding_names):]))
        if i == 0:
            for n in ("sg_ln_g", "sg_ln_b"):
                gfull[n] = jnp.stack(gfull[n])
            small = jnp.concatenate([_chip_blocks(gfull[n], axis) for n, axis in SMALL.items()], axis=1)
            small_rows = _round_up(small.shape[1], 2 * 16 * LANES) // LANES
            small = jnp.pad(small, ((0, 0), (0, small_rows * LANES - small.shape[1])))
            gbig["small"] = small.reshape(N_CHIPS, small_rows, LANES)
            mine = mine + ["small"]
        gs = [gbig[n].reshape(N_CHIPS, 2, gbig[n].shape[1] // 2, gbig[n].shape[2]) for n in mine]
        pair_rider = _pair_exchange_rider(gs)
        below = (ys[i - 1], row(gate, i - 1)) if i > 0 else None
        (dx, dg, dsh, dsc, *below_grads), gots = _norm_mod_bwd(xs[i], row(norm_g, i), row(shift, i), row(scale, i), dh, dx,
                                                               tag + "_norm", dh2, pair_rider, below)
        d_norm_g[i] = dg[0]
        d_mod[i] = jnp.concatenate([dsh[0], dsc[0], d_gate[0]])
        if below is not None:
            dy, d_gate = below_grads
        rs_p.update(zip(mine, _rs_pair_sums(gots[:len(gs)], gots[len(gs):], ci, f"rs{i}")))
        riding_names = mine
    for n in ("sg_w_spatial", "sg_b_spatial"):
        gfull[n] = jnp.stack(gfull[n])
    gfull["norm_g"], gfull["final_norm_g"] = jnp.stack(d_norm_g), d_final_g[0]

    grads, deltas, new_m, new_v, red = {}, {}, {}, {}, {}

    def finish(names, tag):
        outs = _rs_finish([rs_p[n] for n in names], [rs_q[n] for n in names], chip, ci, tag)
        red.update({n: r.reshape(-1, r.shape[2]) for n, r in zip(names, outs)})

    def adamw(n, rider=None):
        res = _adamw(w[n], grads[n], given["m_" + n], given["v_" + n], "adamw_" + n, rider)
        (deltas[n], new_m[n], new_v[n]), ridden = _ridden(res, rider)
        return ridden

    def ride_exchange(names, on):
        ridden = adamw(on, _chip_exchange_rider([rs_p[n] for n in names]))
        rs_p.update(zip(names, ridden[:len(names)]))
        rs_q.update(zip(names, ridden[len(names):]))

    rep_flat = jnp.concatenate([gfull[n].reshape(-1) for n in REPLICATED])
    rep_rows = _round_up(rep_flat.shape[0], 32 * RS_W) // RS_W
    rep_buffer = _replicated_pair_sum(_pad_to(rep_flat, rep_rows * RS_W).reshape(rep_rows, RS_W), "rep")

    finish(sorted(set(rs_p) - set(riding_names)), "rs_a")
    for n in ("swa_w_in", "swa_w_out", "rwkv_w_in", "rwkv_w_out"):
        grads[n] = red[n][None]
    dmod_all = _all_gather8(jnp.stack(d_mod).reshape(DEPTH * 3 * D // RS_W, RS_W), "gather_dmod")
    grads["mod_b"] = _sum_leading(dmod_all, "sum_dmod").reshape(DEPTH, 3 * D)
    dmod_all = dmod_all.reshape(N_DEV, DEPTH, 3 * D)
    dmod_cols = lax.dynamic_slice_in_dim(dmod_all, chip * MOD_COLS, MOD_COLS, axis=2).transpose(1, 0, 2)
    grads["mod_w"] = _mod_w_grad(c_all, dmod_cols, "mod_w_grad")
    ride_exchange(["sg_w_in0"], on="mod_w")
    ride_exchange(["sg_w_out0", "small"], on="rwkv_w_in")
    finish(riding_names, "rs_b")
    grads["sg_w_in"] = jnp.stack([red["sg_w_in0"], red["sg_w_in1"]])
    grads["sg_w_out"] = jnp.stack([red["sg_w_out0"], red["sg_w_out1"]])
    small_red, off = red["small"].reshape(-1), 0
    for n in SMALL:
        grads[n] = small_red[off:off + w[n].size].reshape(w[n].shape)
        off += w[n].size

    (rep_gathered,) = adamw("sg_w_in", _gather_rider([rep_buffer]))
    rep_sum, off = _replicated_chip_sum(rep_gathered, "rep").reshape(-1), 0
    for n in REPLICATED:
        grads[n] = rep_sum[off:off + w[n].size].reshape(w[n].shape)
        off += w[n].size

    for n in W_NAMES:
        if n not in deltas:
            adamw(n)
    return (loss, dx[None], *[grads[n] for n in W_NAMES], *[deltas[n] for n in W_NAMES],
            *[new_m[n] for n in W_NAMES], *[new_v[n] for n in W_NAMES])
```

```python
import functools
import math

import jax
import jax.numpy as jnp
from jax import lax
from jax.experimental import pallas as pl
from jax.experimental.pallas import tpu as pltpu

F32 = jnp.float32
BF16 = jnp.bfloat16
HIGHEST = lax.Precision.HIGHEST

D = 2048
DEPTH = 4
CHUNK = 128
SG_GROUPS = 16
HEAD = 64
N_HEADS = D // HEAD
KV_HEADS = 4
KVW = KV_HEADS * HEAD
ROPE_THETA = 10000.0
LORA = 96
LORA_PAD = 128
DECAY_SCALE = math.exp(-0.5)
GN_EPS = 64e-5
RMS_EPS = 1e-6
LN_EPS = 1e-5
ADAM_LR, ADAM_B1, ADAM_B2, ADAM_EPS, ADAM_WD, ADAM_STEP = 0.001, 0.9, 0.999, 1e-08, 0.01, 10
LANES = 128
NEG = -1e30
VMEM_LIMIT = 56 * 1024 * 1024

MESHT = pl.DeviceIdType.MESH


def _cparams(*sem):
    return pltpu.CompilerParams(dimension_semantics=sem, vmem_limit_bytes=VMEM_LIMIT)


class _Rider:
    def __init__(self, arrays, n_sems, start, finish):
        self.arrays, self.n_sems, self.start, self.finish = list(arrays), n_sems, start, finish


def _ridden(res, rider):
    return (res, []) if rider is None else res


def _compute_call(body, args, *, name, grid, in_specs, out_specs, out_shape, semantics, scratch_shapes=(), rider=None):
    if rider is None:
        return pl.pallas_call(body, name=name, grid=grid, in_specs=in_specs, out_specs=out_specs, out_shape=out_shape,
                              scratch_shapes=list(scratch_shapes), compiler_params=_cparams(*semantics))(*args)
    single = not isinstance(out_shape, (list, tuple))
    o_specs, o_shapes = ([out_specs], [out_shape]) if single else (list(out_specs), list(out_shape))
    n_in, n_out, n_r = len(in_specs), len(o_specs), len(rider.arrays)

    def with_rider(*refs):
        ins, outs = refs[:n_in], refs[n_in + n_r:n_in + n_r + n_out]
        ridden = refs[n_in + n_r + n_out:n_in + 2 * n_r + n_out]
        scratch, (send_sems, recv_sems) = refs[n_in + 2 * n_r + n_out:-2], refs[-2:]
        ids = [pl.program_id(d) for d in range(len(grid))]
        first = functools.reduce(jnp.logical_and, [i == 0 for i in ids])
        last = functools.reduce(jnp.logical_and, [i == g - 1 for i, g in zip(ids, grid)])

        @pl.when(first)
        def _():
            rider.start(ridden, send_sems, recv_sems)

        body(*ins, *outs, *scratch)

        @pl.when(last)
        def _():
            rider.finish(ridden, send_sems, recv_sems)

    any_spec = pl.BlockSpec(memory_space=pl.ANY)
    res = pl.pallas_call(
        with_rider, name=name, grid=grid, in_specs=list(in_specs) + [any_spec] * n_r, out_specs=o_specs + [any_spec] * n_r,
        out_shape=o_shapes + [jax.ShapeDtypeStruct(a.shape, a.dtype) for a in rider.arrays],
        input_output_aliases={n_in + i: n_out + i for i in range(n_r)},
        scratch_shapes=list(scratch_shapes) + [pltpu.SemaphoreType.DMA((rider.n_sems,))] * 2,
        compiler_params=_cparams(*["arbitrary"] * len(grid)),
    )(*args, *rider.arrays)
    return (res[0] if single else list(res[:n_out])), list(res[n_out:])


_NN = (((1,), (0,)), ((), ()))
_NT = (((1,), (1,)), ((), ()))
_TN = (((0,), (0,)), ((), ()))


def _dg(a, b, dims):
    return lax.dot_general(a, b, dims, preferred_element_type=F32)


@jax.custom_vjp
def _bdot_nn(a, b):
    return _dg(a.astype(BF16), b.astype(BF16), _NN)


def _bdot_nn_fwd(a, b):
    a, b = a.astype(BF16), b.astype(BF16)
    return _dg(a, b, _NN), (a, b)


def _bdot_nn_bwd(res, ct):
    a, b = res
    ct = ct.astype(BF16)
    return _dg(ct, b, _NT), _dg(a, ct, _TN)


_bdot_nn.defvjp(_bdot_nn_fwd, _bdot_nn_bwd)


@jax.custom_vjp
def _bdot_nt(a, b):
    return _dg(a.astype(BF16), b.astype(BF16), _NT)


def _bdot_nt_fwd(a, b):
    a, b = a.astype(BF16), b.astype(BF16)
    return _dg(a, b, _NT), (a, b)


def _bdot_nt_bwd(res, ct):
    a, b = res
    ct = ct.astype(BF16)
    return _dg(ct, b, _NN), _dg(ct, a, _TN)


_bdot_nt.defvjp(_bdot_nt_fwd, _bdot_nt_bwd)


def _tile(n, cap):
    if n <= cap:
        return n
    return max(d for d in range(LANES, cap + 1, LANES) if n % d == 0)


def _matmul(a, b, form, name, out_dtype=F32, blocked=False, rider=None, tm=1024, tn=None, tk=4096):
    if form == "nn":
        (m, k), n = a.shape, (N_CHIPS * b.shape[2] if blocked else b.shape[1])
    elif form == "nt":
        m, k, n = a.shape[0], a.shape[1], (b.shape[1] if blocked else b.shape[0])
    else:
        (k, m), n = a.shape, b.shape[1]
    per_chip = (k if form == "nt" else n) // N_CHIPS
    tn = tn or (512 if form == "tn" else 1024)
    if blocked and form == "nt":
        tk = _tile(per_chip, tk)
    elif blocked:
        tn = _tile(per_chip, tn)
    tm, tn, tk = _tile(m, tm), _tile(n, tn), _tile(k, tk)
    assert m % tm == 0 and n % tn == 0 and k % tk == 0, (name, a.shape, b.shape)
    nk = k // tk
    dims = {"nn": _NN, "nt": _NT, "tn": _TN}[form]
    a_spec = pl.BlockSpec((tk, tm), lambda i, j, l: (l, i)) if form == "tn" else pl.BlockSpec((tm, tk), lambda i, j, l: (i, l))
    b_spec = pl.BlockSpec((tn, tk), lambda i, j, l: (j, l)) if form == "nt" else pl.BlockSpec((tk, tn), lambda i, j, l: (l, j))
    o_spec = pl.BlockSpec((tm, tn), lambda i, j, l: (i, j))
    o_shape = (m, n)
    if blocked and form == "nn":
        pc = per_chip // tn
        b_spec = pl.BlockSpec((None, tk, tn), lambda i, j, l: (j // pc, l, j % pc))
    elif blocked and form == "nt":
        pc = per_chip // tk
        b_spec = pl.BlockSpec((None, tn, tk), lambda i, j, l: (l // pc, j, l % pc))
    elif blocked:
        pc = per_chip // tn
        o_spec = pl.BlockSpec((None, tm, tn), lambda i, j, l: (j // pc, i, j % pc))
        o_shape = (N_CHIPS, m, per_chip)

    def body(a_ref, b_ref, o_ref, acc_ref):
        part = _dg(a_ref[...], b_ref[...], dims)
        if nk == 1:
            o_ref[...] = part.astype(out_dtype)
        else:
            l = pl.program_id(2)

            @pl.when(l == 0)
            def _():
                acc_ref[...] = part

            @pl.when(l > 0)
            def _():
                acc_ref[...] += part

            @pl.when(l == nk - 1)
            def _():
                o_ref[...] = acc_ref[...].astype(out_dtype)

    return _compute_call(
        body, (a, b), name=name, grid=(m // tm, n // tn, nk),
        in_specs=[a_spec, b_spec], out_specs=o_spec, out_shape=jax.ShapeDtypeStruct(o_shape, out_dtype),
        scratch_shapes=[pltpu.VMEM((tm, tn) if nk > 1 else (8, LANES), F32)],
        semantics=("parallel", "parallel", "arbitrary"), rider=rider)


def _out_proj_resid(u, w_out, x, gate, name, tm=1024, tn=512):
    (m, k), n = u.shape, w_out.shape[1]

    def body(u_ref, w_ref, x_ref, g_ref, y_ref, xn_ref):
        y = _dg(u_ref[...], w_ref[...], _NN)
        y_ref[...] = y
        xn_ref[...] = x_ref[...] + g_ref[...] * y

    tile = pl.BlockSpec((tm, tn), lambda i, j: (i, j))
    out = jax.ShapeDtypeStruct((m, n), F32)
    return pl.pallas_call(
        body, name=name, grid=(m // tm, n // tn),
        in_specs=[pl.BlockSpec((tm, k), lambda i, j: (i, 0)), pl.BlockSpec((k, tn), lambda i, j: (0, j)), tile,
                  pl.BlockSpec((1, tn), lambda i, j: (0, j))],
        out_specs=[tile, tile], out_shape=[out, out], compiler_params=_cparams("parallel", "parallel"),
    )(u, w_out, x, gate)


TB_NORM = 256


def _f_norm_mod(x, g, shift, scale):
    xn = x * lax.rsqrt(jnp.mean(x * x, axis=-1, keepdims=True) + RMS_EPS)
    return (xn * g) * (1.0 + scale) + shift


def _row_spec(width, tb=TB_NORM):
    return pl.BlockSpec((tb, width), lambda i: (i, 0))


def _vec_spec(width, rows=1):
    return pl.BlockSpec((rows, width), lambda i: (0, 0))


def _norm_mod_fwd(x, g, shift, scale, name):
    t = x.shape[0]

    def body(x_ref, g_ref, sh_ref, sc_ref, h_ref):
        h_ref[...] = _f_norm_mod(x_ref[...], g_ref[...], sh_ref[...], sc_ref[...]).astype(BF16)

    return pl.pallas_call(
        body, name=name, grid=(t // TB_NORM,),
        in_specs=[_row_spec(D), _vec_spec(D), _vec_spec(D), _vec_spec(D)], out_specs=_row_spec(D),
        out_shape=jax.ShapeDtypeStruct((t, D), BF16), compiler_params=_cparams("parallel"),
    )(x, g, shift, scale)


def _accum(ref, val, first):
    @pl.when(first)
    def _():
        ref[...] = val

    @pl.when(jnp.logical_not(first))
    def _():
        ref[...] += val


def _gate_bwd_block(dx, y_ref, gate_ref, dy_ref, dgate_ref, first):
    dy_ref[...] = (dx * gate_ref[...]).astype(BF16)
    _accum(dgate_ref, jnp.sum(dx * y_ref[...], axis=0, keepdims=True), first)


def _gate_bwd_specs():
    return [_row_spec(D), _vec_spec(D)]


def _gate_bwd_shapes(t):
    return [jax.ShapeDtypeStruct((t, D), BF16), jax.ShapeDtypeStruct((1, D), F32)]


def _norm_mod_bwd(x, g, shift, scale, dh, dx_res, name, dh2=None, rider=None, below=None):
    t = x.shape[0]
    dhs = [dh] if dh2 is None else [dh, dh2]
    extra = [] if below is None else list(below)

    def body(x_ref, g_ref, sh_ref, sc_ref, dr_ref, *refs):
        dh_refs, refs = refs[:len(dhs)], refs[len(dhs):]
        below_refs, (dx_ref, dg_ref, dsh_ref, dsc_ref), below_out = refs[:len(extra)], refs[len(extra):len(extra) + 4], refs[len(extra) + 4:]
        _, vjp = jax.vjp(_f_norm_mod, x_ref[...], g_ref[...], sh_ref[...], sc_ref[...])
        dh_all = dh_refs[0][...]
        for r in dh_refs[1:]:
            dh_all = dh_all + r[...]
        dx, dg, dsh, dsc = vjp(dh_all)
        dx = dx + dr_ref[...]
        dx_ref[...] = dx
        first = pl.program_id(0) == 0
        _accum(dg_ref, dg, first)
        _accum(dsh_ref, dsh, first)
        _accum(dsc_ref, dsc, first)
        if below is not None:
            _gate_bwd_block(dx, *below_refs, *below_out, first)

    vec = jax.ShapeDtypeStruct((1, D), F32)
    return _compute_call(
        body, (x, g, shift, scale, dx_res, *dhs, *extra), name=name, grid=(t // TB_NORM,),
        in_specs=[_row_spec(D), _vec_spec(D), _vec_spec(D), _vec_spec(D), _row_spec(D)] + [_row_spec(D)] * len(dhs)
        + (_gate_bwd_specs() if extra else []),
        out_specs=[_row_spec(D), _vec_spec(D), _vec_spec(D), _vec_spec(D)] + (_gate_bwd_specs() if extra else []),
        out_shape=[jax.ShapeDtypeStruct((t, D), F32), vec, vec, vec] + (_gate_bwd_shapes(t) if extra else []),
        semantics=("arbitrary",), rider=rider)


def _f_final(x, g, target):
    xn = x * lax.rsqrt(jnp.mean(x * x, axis=-1, keepdims=True) + RMS_EPS)
    err = xn * g - target
    return 0.5 * jnp.sum(jnp.mean(err * err, axis=-1, keepdims=True), axis=0, keepdims=True)


def _final_loss_grad(x, g, target, y, gate, name):
    t = x.shape[0]

    def body(x_ref, g_ref, t_ref, y_ref, gate_ref, loss_ref, dx_ref, dg_ref, dy_ref, dgate_ref):
        loss, vjp = jax.vjp(_f_final, x_ref[...], g_ref[...], t_ref[...])
        dx, dg, _ = vjp(jnp.ones((1, 1), F32))
        dx_ref[...] = dx
        first = pl.program_id(0) == 0
        _accum(dg_ref, dg, first)
        _accum(loss_ref, jnp.broadcast_to(loss, (1, LANES)), first)
        _gate_bwd_block(dx, y_ref, gate_ref, dy_ref, dgate_ref, first)

    return pl.pallas_call(
        body, name=name, grid=(t // TB_NORM,),
        in_specs=[_row_spec(D), _vec_spec(D), _row_spec(D)] + _gate_bwd_specs(),
        out_specs=[_vec_spec(LANES), _row_spec(D), _vec_spec(D)] + _gate_bwd_specs(),
        out_shape=[jax.ShapeDtypeStruct((1, LANES), F32), jax.ShapeDtypeStruct((t, D), F32), jax.ShapeDtypeStruct((1, D), F32)]
        + _gate_bwd_shapes(t),
        compiler_params=_cparams("arbitrary"),
    )(x, g, target, y, gate)


def _group_selector():
    gi = lax.broadcasted_iota(jnp.int32, (LANES, D), 0)
    ci = lax.broadcasted_iota(jnp.int32, (LANES, D), 1)
    return (ci // (D // SG_GROUPS) == gi).astype(F32)


def _f_sg(p, ln_g, ln_b, w_s, bs_t):
    u, v, z = p[:, :D], p[:, D:2 * D], p[:, 2 * D:]
    u = jax.nn.gelu(u)
    vf = jax.nn.gelu(v)
    mean = jnp.mean(vf, axis=-1, keepdims=True)
    var = jnp.mean(jnp.square(vf - mean), axis=-1, keepdims=True)
    vn = (vf - mean) * lax.rsqrt(var + LN_EPS) * ln_g + ln_b
    ti = lax.broadcasted_iota(jnp.int32, (CHUNK, CHUNK), 0)
    si = lax.broadcasted_iota(jnp.int32, (CHUNK, CHUNK), 1)
    causal = si <= ti
    cg = D // SG_GROUPS
    f = jnp.concatenate(
        [_bdot_nn(jnp.where(causal, w_s[g], 0.0), vn[:, g * cg:(g + 1) * cg]) for g in range(SG_GROUPS)], axis=1)
    f = f + jnp.dot(bs_t, _group_selector(), precision=HIGHEST, preferred_element_type=F32)
    return u * f * jax.nn.silu(z)


def _sg_specs():
    return [pl.BlockSpec((CHUNK, 3 * D), lambda i: (i, 0)), _vec_spec(D), _vec_spec(D),
            pl.BlockSpec((SG_GROUPS, CHUNK, CHUNK), lambda i: (0, 0, 0)), _vec_spec(LANES, CHUNK)]


def _sg_fwd(p, ln_g, ln_b, w_s, bs_t, name, rider=None):
    t = p.shape[0]

    def body(p_ref, lg_ref, lb_ref, w_ref, b_ref, o_ref):
        o_ref[...] = _f_sg(p_ref[...], lg_ref[...], lb_ref[...], w_ref[...], b_ref[...]).astype(BF16)

    return _compute_call(
        body, (p, ln_g, ln_b, w_s, bs_t), name=name, grid=(t // CHUNK,), in_specs=_sg_specs(),
        out_specs=_row_spec(D, CHUNK), out_shape=jax.ShapeDtypeStruct((t, D), BF16), semantics=("parallel",), rider=rider)


def _sg_bwd(p, ln_g, ln_b, w_s, bs_t, dout, name, rider=None):
    t = p.shape[0]

    def body(p_ref, lg_ref, lb_ref, w_ref, b_ref, do_ref, dp_ref, dlg_ref, dlb_ref, dw_ref, db_ref):
        _, vjp = jax.vjp(_f_sg, p_ref[...], lg_ref[...], lb_ref[...], w_ref[...], b_ref[...])
        dp, dlg, dlb, dw, db = vjp(do_ref[...])
        dp_ref[...] = dp.astype(BF16)
        first = pl.program_id(0) == 0
        _accum(dlg_ref, dlg, first)
        _accum(dlb_ref, dlb, first)
        _accum(dw_ref, dw, first)
        _accum(db_ref, db, first)

    vec = jax.ShapeDtypeStruct((1, D), F32)
    return _compute_call(
        body, (p, ln_g, ln_b, w_s, bs_t, dout), name=name, grid=(t // CHUNK,), in_specs=_sg_specs() + [_row_spec(D, CHUNK)],
        out_specs=[pl.BlockSpec((CHUNK, 3 * D), lambda i: (i, 0)), _vec_spec(D), _vec_spec(D),
                   pl.BlockSpec((SG_GROUPS, CHUNK, CHUNK), lambda i: (0, 0, 0)), _vec_spec(LANES, CHUNK)],
        out_shape=[jax.ShapeDtypeStruct((t, 3 * D), BF16), vec, vec,
                   jax.ShapeDtypeStruct((SG_GROUPS, CHUNK, CHUNK), F32), jax.ShapeDtypeStruct((CHUNK, LANES), F32)],
        semantics=("arbitrary",), rider=rider)


SWA_COLS = 2 * D + 2 * KVW
KV_BLOCK = 2 * KVW


def _lane_roll(x, shift):
    return pltpu.roll(x, shift, 1)


def _rot_half(x):
    w = x.shape[1]
    lane = lax.broadcasted_iota(jnp.int32, x.shape, 1)
    return jnp.where(lane % HEAD < HEAD // 2, -_lane_roll(x, w - HEAD // 2), _lane_roll(x, HEAD // 2))


@jax.custom_vjp
def _rope(x, cos, sin):
    return x * cos + _rot_half(x) * sin


def _rope_fwd(x, cos, sin):
    return _rope(x, cos, sin), (cos, sin)


def _rope_bwd(res, ct):
    cos, sin = res
    return ct * cos - _rot_half(ct) * sin, jnp.zeros_like(cos), jnp.zeros_like(sin)


_rope.defvjp(_rope_fwd, _rope_bwd)


@jax.custom_vjp
def _swap_halves(x):
    return _lane_roll(x, HEAD)


_swap_halves.defvjp(lambda x: (_lane_roll(x, HEAD), None), lambda _, ct: (_lane_roll(ct, HEAD),))


def _f_swa(pq, pkv, cos, sin, cosp, sinp, sink_row, valid):
    reps = D // LANES
    q = _rope(pq[:, :D], jnp.tile(cos, (1, reps)), jnp.tile(sin, (1, reps))) * (HEAD ** -0.5)
    k = _rope(pq[:, D:D + KVW], jnp.tile(cos, (1, KVW // LANES)), jnp.tile(sin, (1, KVW // LANES)))
    kp = _rope(pkv[:, :KVW], jnp.tile(cosp, (1, KVW // LANES)), jnp.tile(sinp, (1, KVW // LANES)))
    v, vp, z = pq[:, D + KVW:D + 2 * KVW], pkv[:, KVW:], pq[:, D + 2 * KVW:]
    kcat = jnp.concatenate([kp, k], axis=0)
    vcat = jnp.concatenate([vp, v], axis=0)
    lane = lax.broadcasted_iota(jnp.int32, (2 * CHUNK, LANES), 1)
    lo = lane < HEAD
    hlane = lax.broadcasted_iota(jnp.int32, (1, LANES), 1)

    def halves(cat, g):
        blk = cat[:, (g // 2) * LANES:(g // 2 + 1) * LANES]
        other = _swap_halves(blk)
        if g % 2 == 0:
            return jnp.where(lo, blk, 0.0), jnp.where(lo, 0.0, other)
        return jnp.where(lo, other, 0.0), jnp.where(lo, 0.0, blk)

    pairs = N_HEADS // KV_HEADS // 2
    valid_g = jnp.tile(valid, (pairs, 1))

    def probs(s, heads):
        sink = jnp.concatenate(
            [jnp.broadcast_to(jnp.sum(jnp.where(hlane == h, sink_row, 0.0), axis=1, keepdims=True), (CHUNK, 1))
             for h in heads], axis=0)
        s = jnp.where(valid_g, s, NEG)
        m = lax.stop_gradient(jnp.maximum(jnp.max(s, axis=1, keepdims=True), sink))
        e = jnp.exp(s - m)
        return e / (jnp.sum(e, axis=1, keepdims=True) + jnp.exp(sink - m))

    outs = []
    for g in range(KV_HEADS):
        k_lo, k_hi = halves(kcat, g)
        v_lo, v_hi = halves(vcat, g)
        tiles = range(g * pairs, (g + 1) * pairs)
        qg = jnp.concatenate([q[:, j * LANES:(j + 1) * LANES] for j in tiles], axis=0)
        p_a = probs(_bdot_nt(qg, k_lo), [2 * j for j in tiles])
        p_b = probs(_bdot_nt(qg, k_hi), [2 * j + 1 for j in tiles])
        og = _bdot_nn(p_a, v_lo) + _bdot_nn(p_b, v_hi)
        outs += [og[n * CHUNK:(n + 1) * CHUNK] for n in range(pairs)]
    return jnp.concatenate(outs, axis=1) * jax.nn.silu(z)


def _swa_valid(block):
    qi = lax.broadcasted_iota(jnp.int32, (CHUNK, 2 * CHUNK), 0)
    kj = lax.broadcasted_iota(jnp.int32, (CHUNK, 2 * CHUNK), 1)
    rel = qi + CHUNK - kj
    return (rel >= 0) & (rel < CHUNK) & ((kj >= CHUNK) | (block > 0))


def _swa_specs(blk):
    prev = lambda i: jnp.maximum(blk(i) - 1, 0)
    kv_col = D // KV_BLOCK
    return [pl.BlockSpec((CHUNK, SWA_COLS), lambda i: (blk(i), 0)),
            pl.BlockSpec((CHUNK, KV_BLOCK), lambda i: (prev(i), kv_col)),
            pl.BlockSpec((CHUNK, LANES), lambda i: (blk(i), 0)), pl.BlockSpec((CHUNK, LANES), lambda i: (blk(i), 0)),
            pl.BlockSpec((CHUNK, LANES), lambda i: (prev(i), 0)), pl.BlockSpec((CHUNK, LANES), lambda i: (prev(i), 0)),
            _vec_spec(LANES)]


def _swa_fwd(p, cos, sin, sink_row, name, rider=None):
    t = p.shape[0]

    def body(pq_ref, pkv_ref, c_ref, s_ref, cp_ref, sp_ref, sk_ref, o_ref):
        valid = _swa_valid(pl.program_id(0))
        o_ref[...] = _f_swa(pq_ref[...], pkv_ref[...], c_ref[...], s_ref[...], cp_ref[...], sp_ref[...],
                            sk_ref[...], valid).astype(BF16)

    return _compute_call(
        body, (p, p, cos, sin, cos, sin, sink_row), name=name, grid=(t // CHUNK,), in_specs=_swa_specs(lambda i: i),
        out_specs=_row_spec(D, CHUNK), out_shape=jax.ShapeDtypeStruct((t, D), BF16), semantics=("parallel",), rider=rider)


def _swa_bwd(p, cos, sin, sink_row, dout, name, rider=None):
    t = p.shape[0]
    nb = t // CHUNK
    blk = lambda i: nb - 1 - i

    def body(pq_ref, pkv_ref, c_ref, s_ref, cp_ref, sp_ref, sk_ref, do_ref, dp_ref, dsk_ref, pend_ref):
        i = pl.program_id(0)
        valid = _swa_valid(blk(i))
        f = functools.partial(_f_swa, valid=valid)
        _, vjp = jax.vjp(f, pq_ref[...], pkv_ref[...], c_ref[...], s_ref[...], cp_ref[...], sp_ref[...], sk_ref[...])
        dpq, dpkv, _, _, _, _, dsk = vjp(do_ref[...])

        @pl.when(i == 0)
        def _():
            pend_ref[...] = jnp.zeros_like(pend_ref)

        dp_ref[...] = jnp.concatenate(
            [dpq[:, :D], dpq[:, D:D + KV_BLOCK] + pend_ref[...], dpq[:, D + KV_BLOCK:]], axis=1).astype(BF16)
        pend_ref[...] = dpkv
        _accum(dsk_ref, dsk, i == 0)

    return _compute_call(
        body, (p, p, cos, sin, cos, sin, sink_row, dout), name=name, grid=(nb,),
        in_specs=_swa_specs(blk) + [pl.BlockSpec((CHUNK, D), lambda i: (blk(i), 0))],
        out_specs=[pl.BlockSpec((CHUNK, SWA_COLS), lambda i: (blk(i), 0)), _vec_spec(LANES)],
        out_shape=[jax.ShapeDtypeStruct((t, SWA_COLS), BF16), jax.ShapeDtypeStruct((1, LANES), F32)],
        scratch_shapes=[pltpu.VMEM((CHUNK, KV_BLOCK), F32)], semantics=("arbitrary",), rider=rider)


RW_MAIN = 4 * D
RW_LO = 2 * LORA_PAD
VM = LANES // N_HEADS
VD = HEAD // VM
S_ROWS = VD * HEAD
TB_RW = 128
TB_K = 32
TB_SCAN = 16


def _dim_major(a):
    return a.reshape(a.shape[:-1] + (N_HEADS, HEAD)).swapaxes(-1, -2).reshape(a.shape)


def _head_major(a):
    return a.reshape(a.shape[:-1] + (HEAD, N_HEADS)).swapaxes(-1, -2).reshape(a.shape)


def _param_compact(w):
    return _dim_major(w).reshape(VD, LANES)


def _param_compact_inv(pc):
    return _head_major(pc.reshape(-1))


def _f_rwkv_lora(xs_lo, w0, a0, wl, al):
    decay = jnp.exp(-DECAY_SCALE * jax.nn.sigmoid(w0 + _bdot_nn(jnp.tanh(xs_lo[:, :LORA_PAD]), wl)))
    a = jax.nn.sigmoid(a0 + _bdot_nn(xs_lo[:, LORA_PAD:], al))
    return decay, a


def _prev_rows_spec(width, tb):
    return pl.BlockSpec((8, width), lambda i: (jnp.maximum(i * (tb // 8) - 1, 0), 0))


def _token_shift_lerp(p, prev8, mu, first):
    rows = lax.broadcasted_iota(jnp.int32, p.shape, 0)
    prev = jnp.where(first, 0.0, prev8[7:8, :])
    shifted = jnp.where(rows == 0, prev, pltpu.roll(p, 1, 0))
    return p + (shifted - p) * mu


def _store_compact(ref, val):
    for j in range(VD):
        ref[:, j, :] = val[:, j * LANES:(j + 1) * LANES]


def _load_flat(ref, rows=slice(None)):
    if len(ref.shape) == 2:
        return ref[rows, :]
    return jnp.concatenate([ref[rows, j, :] for j in range(VD)], axis=1)


def _flat_spec(a, tb):
    return _row_spec(a.shape[1], tb) if a.ndim == 2 else _k_spec(VD, tb)


def _rwkv_pre_fwd(p_main, p_lo, mu_main, mu_lo, w0, a0, wl, al, name):
    t = p_main.shape[0]
    tb = TB_RW

    def body(pm_ref, pmp_ref, pl_ref, plp_ref, mm_ref, ml_ref, w0_ref, a0_ref, wl_ref, al_ref,
             r_ref, k_ref, v_ref, dec_ref, a_ref, z_ref, xl_ref):
        first = pl.program_id(0) == 0
        xs = _token_shift_lerp(pm_ref[...], pmp_ref[...], mm_ref[...], first)
        for n, ref in enumerate((r_ref, k_ref, v_ref)):
            _store_compact(ref, xs[:, n * D:(n + 1) * D])
        z_ref[...] = xs[:, 3 * D:]
        xs_lo = _token_shift_lerp(pl_ref[...], plp_ref[...], ml_ref[...], first)
        xl_ref[...] = xs_lo
        decay, a = _f_rwkv_lora(xs_lo, w0_ref[...], a0_ref[...], wl_ref[...], al_ref[...])
        _store_compact(dec_ref, decay)
        _store_compact(a_ref, a)

    cl = jax.ShapeDtypeStruct((t, VD, LANES), F32)
    return pl.pallas_call(
        body, name=name, grid=(t // tb,),
        in_specs=[_row_spec(RW_MAIN, tb), _prev_rows_spec(RW_MAIN, tb), _row_spec(RW_LO, tb), _prev_rows_spec(RW_LO, tb),
                  _vec_spec(RW_MAIN), _vec_spec(RW_LO), _vec_spec(D), _vec_spec(D),
                  _vec_spec(D, LORA_PAD), _vec_spec(D, LORA_PAD)],
        out_specs=[_k_spec(VD, tb)] * 5 + [_row_spec(D, tb), _row_spec(RW_LO, tb)],
        out_shape=[cl] * 5 + [jax.ShapeDtypeStruct((t, D), F32), jax.ShapeDtypeStruct((t, RW_LO), F32)],
        compiler_params=_cparams("parallel"),
    )(p_main, p_main, p_lo, p_lo, mu_main, mu_lo, w0, a0, wl, al)


def _rwkv_lora_bwd(xs_lo, w0, a0, wl, al, ddecay, da, name):
    t = xs_lo.shape[0]
    tb = TB_NORM

    def body(x_ref, w0_ref, a0_ref, wl_ref, al_ref, dd_ref, da_ref, dx_ref, dw0_ref, da0_ref, dwl_ref, dal_ref):
        _, vjp = jax.vjp(_f_rwkv_lora, x_ref[...], w0_ref[...], a0_ref[...], wl_ref[...], al_ref[...])
        dx, dw0, da0, dwl, dal = vjp((_load_flat(dd_ref), _load_flat(da_ref)))
        dx_ref[...] = dx
        first = pl.program_id(0) == 0
        _accum(dw0_ref, dw0, first)
        _accum(da0_ref, da0, first)
        _accum(dwl_ref, dwl, first)
        _accum(dal_ref, dal, first)

    vec = jax.ShapeDtypeStruct((1, D), F32)
    lor = jax.ShapeDtypeStruct((LORA_PAD, D), F32)
    return pl.pallas_call(
        body, name=name, grid=(t // tb,),
        in_specs=[_row_spec(RW_LO), _vec_spec(D), _vec_spec(D), _vec_spec(D, LORA_PAD), _vec_spec(D, LORA_PAD),
                  _k_spec(VD, tb), _k_spec(VD, tb)],
        out_specs=[_row_spec(RW_LO), _vec_spec(D), _vec_spec(D), _vec_spec(D, LORA_PAD), _vec_spec(D, LORA_PAD)],
        out_shape=[jax.ShapeDtypeStruct((t, RW_LO), F32), vec, vec, lor, lor], compiler_params=_cparams("arbitrary"),
    )(xs_lo, w0, a0, wl, al, ddecay, da)


def _lerp_bwd(p, dxs_groups, mu, name):
    t, width = p.shape
    tb = TB_RW
    nb = t // tb
    parts = [a for group in dxs_groups for a in group]

    def body(p_ref, pp_ref, mu_ref, *refs):
        d_refs, (dp_ref, dmu_ref) = refs[:2 * len(parts)], refs[2 * len(parts):]
        i = pl.program_id(0)

        def columns(k):
            pick = (lambda r: _load_flat(r, slice(0, 1))) if k else _load_flat
            vals, at = [], 0
            for group in dxs_groups:
                vals.append(functools.reduce(jnp.add, [pick(d_refs[2 * (at + n) + k]) for n in range(len(group))]))
                at += len(group)
            return jnp.concatenate(vals, axis=1)

        pv, dv, mu_v = p_ref[...], columns(0), mu_ref[...]
        rows = lax.broadcasted_iota(jnp.int32, pv.shape, 0)
        prev = jnp.where(i == 0, 0.0, pp_ref[7:8, :])
        shifted = jnp.where(rows == 0, prev, pltpu.roll(pv, 1, 0))
        nxt = jnp.where(i == nb - 1, 0.0, columns(1))
        d_next = jnp.where(rows == tb - 1, nxt, pltpu.roll(dv, tb - 1, 0))
        dp_ref[...] = (dv * (1.0 - mu_v) + d_next * mu_v).astype(BF16)
        _accum(dmu_ref, jnp.sum(dv * (shifted - pv), axis=0, keepdims=True), i == 0)

    d_specs = []
    for a in parts:
        after = lambda i, nd=a.ndim: (jnp.minimum((i + 1) * (tb // 8), t // 8 - 1),) + (0,) * (nd - 1)
        d_specs += [_flat_spec(a, tb), pl.BlockSpec((8,) + a.shape[1:], after)]
    return pl.pallas_call(
        body, name=name, grid=(nb,),
        in_specs=[_row_spec(width, tb), _prev_rows_spec(width, tb), _vec_spec(width)] + d_specs,
        out_specs=[_row_spec(width, tb), _vec_spec(width)],
        out_shape=[jax.ShapeDtypeStruct((t, width), BF16), jax.ShapeDtypeStruct((1, width), F32)],
        compiler_params=_cparams("arbitrary"),
    )(p, p, mu, *[a for a in parts for _ in range(2)])


def _lane_group_sum2d(x):
    x = x + pltpu.roll(x, N_HEADS, 1)
    return x + pltpu.roll(x, 2 * N_HEADS, 1)


@jax.custom_vjp
def _lane_group_sum(x):
    return _lane_group_sum2d(x.reshape(-1, LANES)).reshape(x.shape)


_lane_group_sum.defvjp(lambda x: (_lane_group_sum(x), None), lambda _, ct: (_lane_group_sum(ct),))


def _head_sum(x):
    return _lane_group_sum(jnp.sum(x, axis=1, keepdims=True))


def _f_kprep(k, a, r, kkp, kap, rkp):
    kk = k * kkp
    kk = kk / jnp.maximum(jnp.sqrt(_head_sum(kk * kk)), 1e-12)
    k2 = k * (1.0 + (a - 1.0) * kap)
    return kk, k2, kk * a, _head_sum(r * k2 * rkp)


def _k_spec(rows=HEAD, tb=TB_K):
    return pl.BlockSpec((tb, rows, LANES), lambda i: (i, 0, 0))


def _kparam_spec(rows=HEAD):
    return pl.BlockSpec((rows, LANES), lambda i: (0, 0))


def _lane_group(shape):
    return lax.broadcasted_iota(jnp.int32, shape, len(shape) - 1) // N_HEADS


def _store_k_layout(ref, xc):
    x2 = xc.reshape(-1, LANES)
    group = _lane_group(x2.shape)
    shifted = [x2] + [pltpu.roll(x2, N_HEADS * k, 1) for k in range(1, VM)]
    for q in range(VM):
        out = shifted[0]
        for k in range(1, VM):
            out = jnp.where(group == (q + k) % VM, shifted[k], out)
        ref[:, pl.ds(q, VD, stride=VM), :] = out.reshape(xc.shape)


def _load_compact(ref):
    shape = (ref.shape[0], VD, LANES)
    rows = [ref[:, pl.ds(q, VD, stride=VM), :].reshape(-1, LANES) for q in range(VM)]
    group = _lane_group(rows[0].shape)
    acc = None
    for k in range(VM):
        t = rows[-k % VM]
        for g in range(1, VM):
            t = jnp.where(group == g, rows[(g - k) % VM], t)
        if k:
            t = pltpu.roll(t, LANES - N_HEADS * k, 1)
        acc = t if acc is None else acc + t
    return acc.reshape(shape)


def _rwkv_kprep_fwd(k, a, r, w, kkp, kap, rkp, name):
    t = k.shape[0]

    def body(k_ref, a_ref, r_ref, w_ref, kkp_ref, kap_ref, rkp_ref, kk_ref, k2_ref, b_ref, r4_ref, w4_ref, rk_ref):
        rv = r_ref[...]
        kk, k2, b, rk_ref[...] = _f_kprep(k_ref[...], a_ref[...], rv, kkp_ref[...], kap_ref[...], rkp_ref[...])
        for ref, val in ((kk_ref, kk), (k2_ref, k2), (b_ref, b), (r4_ref, rv), (w4_ref, w_ref[...])):
            _store_k_layout(ref, val)

    big = jax.ShapeDtypeStruct((t, HEAD, LANES), F32)
    return pl.pallas_call(
        body, name=name, grid=(t // TB_K,),
        in_specs=[_k_spec(VD)] * 4 + [_kparam_spec(VD)] * 3, out_specs=[_k_spec()] * 5 + [_k_spec(1)],
        out_shape=[big] * 5 + [jax.ShapeDtypeStruct((t, 1, LANES), F32)], compiler_params=_cparams("parallel"),
    )(k, a, r, w, kkp, kap, rkp)


def _rwkv_kprep_bwd(k, a, r, kkp, kap, rkp, dkk, dk2, db, drk, dr_scan, dw_scan, name):
    t = k.shape[0]

    def body(k_ref, a_ref, r_ref, kkp_ref, kap_ref, rkp_ref, dkk_ref, dk2_ref, db_ref, drk_ref, drs_ref, dws_ref,
             dk_ref, da_ref, dr_ref, dw_ref, dkkp_ref, dkap_ref, drkp_ref):
        _, vjp = jax.vjp(_f_kprep, k_ref[...], a_ref[...], r_ref[...], kkp_ref[...], kap_ref[...], rkp_ref[...])
        dk, da, dr, dkkp, dkap, drkp = vjp((_load_compact(dkk_ref), _load_compact(dk2_ref), _load_compact(db_ref),
                                            drk_ref[...]))
        dk_ref[...] = dk
        da_ref[...] = da
        dr_ref[...] = dr + _load_compact(drs_ref)
        dw_ref[...] = _load_compact(dws_ref)
        first = pl.program_id(0) == 0
        _accum(dkkp_ref, dkkp, first)
        _accum(dkap_ref, dkap, first)
        _accum(drkp_ref, drkp, first)

    cl = jax.ShapeDtypeStruct((t, VD, LANES), F32)
    par = jax.ShapeDtypeStruct((VD, LANES), F32)
    return pl.pallas_call(
        body, name=name, grid=(t // TB_K,),
        in_specs=[_k_spec(VD)] * 3 + [_kparam_spec(VD)] * 3 + [_k_spec()] * 3 + [_k_spec(1), _k_spec(), _k_spec()],
        out_specs=[_k_spec(VD)] * 4 + [_kparam_spec(VD)] * 3,
        out_shape=[cl] * 4 + [par] * 3, compiler_params=_cparams("arbitrary"),
    )(k, a, r, kkp, kap, rkp, dkk, dk2, db, drk, dr_scan, dw_scan)


def _f_post(y, v, rk, g, b):
    mean = _lane_group_sum(jnp.sum(y, axis=1, keepdims=True)) * (1.0 / HEAD)
    yc = y - mean
    var = _lane_group_sum(jnp.sum(yc * yc, axis=1, keepdims=True)) * (1.0 / HEAD)
    return yc * lax.rsqrt(var + GN_EPS) * g + b + rk * v


def _rwkv_post_fwd(y, v, rk, g, b, name):
    t = y.shape[0]

    def body(y_ref, v_ref, rk_ref, g_ref, b_ref, o_ref):
        o_ref[...] = _f_post(y_ref[...], v_ref[...], rk_ref[...], g_ref[...], b_ref[...])

    return pl.pallas_call(
        body, name=name, grid=(t // TB_K,),
        in_specs=[_k_spec(VD), _k_spec(VD), _k_spec(1), _kparam_spec(VD), _kparam_spec(VD)], out_specs=_k_spec(VD),
        out_shape=jax.ShapeDtypeStruct((t, VD, LANES), F32), compiler_params=_cparams("parallel"),
    )(y, v, rk, g, b)


def _rwkv_post_bwd(y, v, rk, g, b, do, name):
    t = y.shape[0]

    def body(y_ref, v_ref, rk_ref, g_ref, b_ref, do_ref, dy_ref, dv_ref, drk_ref, dg_ref, db_ref):
        _, vjp = jax.vjp(_f_post, y_ref[...], v_ref[...], rk_ref[...], g_ref[...], b_ref[...])
        dy, dv, drk, dg, db = vjp(do_ref[...])
        dy_ref[...] = dy
        dv_ref[...] = dv
        drk_ref[...] = drk
        first = pl.program_id(0) == 0
        _accum(dg_ref, dg, first)
        _accum(db_ref, db, first)

    vl = jax.ShapeDtypeStruct((t, VD, LANES), F32)
    par = jax.ShapeDtypeStruct((VD, LANES), F32)
    return pl.pallas_call(
        body, name=name, grid=(t // TB_K,),
        in_specs=[_k_spec(VD), _k_spec(VD), _k_spec(1), _kparam_spec(VD), _kparam_spec(VD), _k_spec(VD)],
        out_specs=[_k_spec(VD), _k_spec(VD), _k_spec(1), _kparam_spec(VD), _kparam_spec(VD)],
        out_shape=[vl, vl, jax.ShapeDtypeStruct((t, 1, LANES), F32), par, par], compiler_params=_cparams("arbitrary"),
    )(y, v, rk, g, b, do)


def _f_gate(o, z):
    return o * jax.nn.silu(z)


def _rwkv_gate_fwd(o, z, name):
    t = z.shape[0]

    def body(o_ref, z_ref, u_ref):
        u_ref[...] = _f_gate(_load_flat(o_ref), z_ref[...]).astype(BF16)

    return pl.pallas_call(
        body, name=name, grid=(t // TB_NORM,), in_specs=[_k_spec(VD, TB_NORM), _row_spec(D)], out_specs=_row_spec(D),
        out_shape=jax.ShapeDtypeStruct((t, D), BF16), compiler_params=_cparams("parallel"),
    )(o, z)


def _rwkv_gate_bwd(o, z, du, name):
    t = z.shape[0]

    def body(o_ref, z_ref, du_ref, do_ref, dz_ref):
        _, vjp = jax.vjp(_f_gate, _load_flat(o_ref), z_ref[...])
        do, dz_ref[...] = vjp(du_ref[...])
        _store_compact(do_ref, do)

    return pl.pallas_call(
        body, name=name, grid=(t // TB_NORM,), in_specs=[_k_spec(VD, TB_NORM), _row_spec(D), _row_spec(D)],
        out_specs=[_k_spec(VD, TB_NORM), _row_spec(D)],
        out_shape=[jax.ShapeDtypeStruct((t, VD, LANES), F32), jax.ShapeDtypeStruct((t, D), F32)],
        compiler_params=_cparams("parallel"),
    )(o, z, du)


def _colsum(x):
    return jnp.sum(x, axis=0, keepdims=True)


def _rwkv_scan_fwd(r4, w4, k24, kk4, b4, v, name, rider=None):
    t = r4.shape[0]
    tb = TB_SCAN

    def body(r_ref, w_ref, k2_ref, kk_ref, b_ref, v_ref, y_ref, sall_ref, sa_ref, s_scr):
        @pl.when(pl.program_id(0) == 0)
        def _():
            s_scr[...] = jnp.zeros_like(s_scr)

        sall_ref[0] = s_scr[...]

        def step(tt, dst):
            kk = kk_ref[tt]
            sas = []
            for vd in range(VD):
                sa = _colsum(sall_ref[tt, pl.ds(vd * HEAD, HEAD), :] * kk)
                sa_ref[tt, pl.ds(vd, 1), :] = sa
                sas.append(sa)
            w, b, k2, r = w_ref[tt], b_ref[tt], k2_ref[tt], r_ref[tt]
            for vd in range(VD):
                rows = pl.ds(vd * HEAD, HEAD)
                s = sall_ref[tt, rows, :] * w - sas[vd] * b + v_ref[tt, pl.ds(vd, 1), :] * k2
                dst[rows, :] = s
                y_ref[tt, pl.ds(vd, 1), :] = _colsum(s * r)

        def loop_step(tt, carry):
            step(tt, sall_ref.at[tt + 1])
            return carry

        lax.fori_loop(0, tb - 1, loop_step, 0)
        step(tb - 1, s_scr)

    vl = jax.ShapeDtypeStruct((t, VD, LANES), F32)
    return _compute_call(
        body, (r4, w4, k24, kk4, b4, v), name=name, grid=(t // tb,),
        in_specs=[_k_spec(HEAD, tb)] * 5 + [_k_spec(VD, tb)],
        out_specs=[_k_spec(VD, tb), _k_spec(S_ROWS, tb), _k_spec(VD, tb)],
        out_shape=[vl, jax.ShapeDtypeStruct((t, S_ROWS, LANES), F32), vl],
        scratch_shapes=[pltpu.VMEM((S_ROWS, LANES), F32)], semantics=("arbitrary",), rider=rider)


def _rwkv_scan_bwd(dy, s_all, sa_all, r4, w4, k24, kk4, b4, v, name, rider=None):
    t = r4.shape[0]
    tb = TB_SCAN
    nb = t // tb
    blk = lambda i: nb - 1 - i

    def body(dy_ref, sall_ref, sa_ref, r_ref, w_ref, k2_ref, kk_ref, b_ref, v_ref,
             dr_ref, dw_ref, dk2_ref, dkk_ref, db_ref, dv_ref, ds_scr):
        @pl.when(pl.program_id(0) == 0)
        def _():
            ds_scr[...] = jnp.zeros_like(ds_scr)

        def step(j, carry):
            tt = tb - 1 - j
            vrow = lambda ref, vd: ref[tt, pl.ds(vd, 1), :]
            srows = lambda vd: pl.ds(vd * HEAD, HEAD)
            r, k2, b = r_ref[tt], k2_ref[tt], b_ref[tt]
            dsas = []
            for vd in range(VD):
                ds = ds_scr[srows(vd), :] + vrow(dy_ref, vd) * r
                ds_scr[srows(vd), :] = ds
                dv_ref[tt, pl.ds(vd, 1), :] = _colsum(ds * k2)
                dsas.append(-_colsum(ds * b))
            zero = jnp.zeros((HEAD, LANES), F32)
            dk2, q, sady, vdy = zero, zero, 0.0, 0.0
            for vd in range(VD):
                dyv = vrow(dy_ref, vd)
                dk2 = dk2 + ds_scr[srows(vd), :] * vrow(v_ref, vd)
                q = q + sall_ref[tt, srows(vd), :] * dyv
                sady = sady + vrow(sa_ref, vd) * dyv
                vdy = vdy + vrow(v_ref, vd) * dyv
            dk2_ref[tt] = dk2
            dr_ref[tt] = w_ref[tt] * q - b_ref[tt] * sady + k2_ref[tt] * vdy
            dw, dkk = zero, zero
            for vd in range(VD):
                sp = sall_ref[tt, srows(vd), :]
                dw = dw + ds_scr[srows(vd), :] * sp
                dkk = dkk + sp * dsas[vd]
            dw_ref[tt] = dw
            dkk_ref[tt] = dkk
            w, kk = w_ref[tt], kk_ref[tt]
            db = zero
            for vd in range(VD):
                ds = ds_scr[srows(vd), :]
                db = db - ds * vrow(sa_ref, vd)
                ds_scr[srows(vd), :] = ds * w + dsas[vd] * kk
            db_ref[tt] = db
            return carry

        lax.fori_loop(0, tb, step, 0)

    rk = lambda rows: pl.BlockSpec((tb, rows, LANES), lambda i: (blk(i), 0, 0))
    big = jax.ShapeDtypeStruct((t, HEAD, LANES), F32)
    return _compute_call(
        body, (dy, s_all, sa_all, r4, w4, k24, kk4, b4, v), name=name, grid=(nb,),
        in_specs=[rk(VD), rk(S_ROWS), rk(VD)] + [rk(HEAD)] * 5 + [rk(VD)],
        out_specs=[rk(HEAD)] * 5 + [rk(VD)],
        out_shape=[big] * 5 + [jax.ShapeDtypeStruct((t, VD, LANES), F32)],
        scratch_shapes=[pltpu.VMEM((S_ROWS, LANES), F32)], semantics=("arbitrary",), rider=rider)


def _rwkv_mixer_fwd(p_main, p_lo, prm, tag, rider):
    r, k, v, w, a, z, xs_lo = _rwkv_pre_fwd(p_main, p_lo, prm["mu_main"], prm["mu_lo"], prm["w0"], prm["a0"],
                                            prm["wl"], prm["al"], tag + "_pre")
    kk4, k24, b4, r4, w4, rk = _rwkv_kprep_fwd(k, a, r, w, prm["kkp"], prm["kap"], prm["rkp"], tag + "_kprep")
    (y, s_all, sa_all), ridden = _ridden(_rwkv_scan_fwd(r4, w4, k24, kk4, b4, v, tag + "_scan", rider), rider)
    o = _rwkv_post_fwd(y, v, rk, prm["gn_g"], prm["gn_b"], tag + "_post")
    u = _rwkv_gate_fwd(o, z, tag + "_gate")
    saved = dict(z=z, xs_lo=xs_lo, r=r, k=k, a=a, v=v, r4=r4, w4=w4, kk4=kk4, k24=k24, b4=b4, rk=rk,
                 y=y, s_all=s_all, sa_all=sa_all, o=o)
    return u, saved, ridden


def _rwkv_mixer_bwd(p_main, p_lo, prm, sv, du, tag, rider):
    do, dz = _rwkv_gate_bwd(sv["o"], sv["z"], du, tag + "_gate_b")
    dy, dv_post, drk, dgn_g, dgn_b = _rwkv_post_bwd(sv["y"], sv["v"], sv["rk"], prm["gn_g"], prm["gn_b"], do,
                                                    tag + "_post_b")
    (dr_s, dw_s, dk24, dkk4, db4, dv_scan), ridden = _ridden(_rwkv_scan_bwd(
        dy, sv["s_all"], sv["sa_all"], sv["r4"], sv["w4"], sv["k24"], sv["kk4"], sv["b4"], sv["v"], tag + "_scan_b", rider), rider)
    dk, da, dr, dw, dkkp, dkap, drkp = _rwkv_kprep_bwd(sv["k"], sv["a"], sv["r"], prm["kkp"], prm["kap"], prm["rkp"],
                                                       dkk4, dk24, db4, drk, dr_s, dw_s, tag + "_kprep_b")
    dxs_lo, dw0, da0, dwl, dal = _rwkv_lora_bwd(sv["xs_lo"], prm["w0"], prm["a0"], prm["wl"], prm["al"], dw, da,
                                                tag + "_lora_b")
    dxs_main = [[dr], [dk], [dv_post, dv_scan], [dz]]
    dp_main, dmu_main = _lerp_bwd(p_main, dxs_main, prm["mu_main"], tag + "_lerp_main_b")
    dp_lo, dmu_lo = _lerp_bwd(p_lo, [[dxs_lo]], prm["mu_lo"], tag + "_lerp_lo_b")
    grads = dict(mu_main=dmu_main, mu_lo=dmu_lo, w0=dw0, a0=da0, wl=dwl, al=dal, kkp=dkkp, kap=dkap, rkp=drkp,
                 gn_g=dgn_g, gn_b=dgn_b)
    return dp_main, dp_lo, grads, ridden


N_DEV = 8
N_CHIPS = 4
ANY = pl.BlockSpec(memory_space=pl.ANY)


def _place():
    return lax.axis_index("x"), lax.axis_index("y"), lax.axis_index("c")


def _remote(src, dst, send_sems, recv_sems, k, dev):
    return pltpu.make_async_remote_copy(src_ref=src, dst_ref=dst, send_sem=send_sems.at[k], recv_sem=recv_sems.at[k],
                                        device_id=dev, device_id_type=MESHT)


def _all_gather8(v, name):
    def body(buf_ref, out_ref, send_sems, recv_sems):
        del buf_ref
        x, y, c = _place()
        mine = out_ref.at[4 * x + 2 * y + c]
        peers = [(x ^ (k >> 2), y ^ ((k >> 1) & 1), c ^ (k & 1)) for k in range(1, N_DEV)]
        sends = [_remote(mine, mine, send_sems, recv_sems, k, peer) for k, peer in enumerate(peers)]
        for cp in sends:
            cp.start()
        for k, (px, py, pc) in enumerate(peers):
            _remote(mine, out_ref.at[4 * px + 2 * py + pc], send_sems, recv_sems, k, (x, y, c)).wait_recv()
        for cp in sends:
            cp.wait_send()

    return pl.pallas_call(
        body, name=name, in_specs=[ANY], out_specs=ANY, input_output_aliases={0: 0},
        out_shape=jax.ShapeDtypeStruct((N_DEV,) + v.shape, v.dtype),
        scratch_shapes=[pltpu.SemaphoreType.DMA((N_DEV - 1,)), pltpu.SemaphoreType.DMA((N_DEV - 1,))],
    )(jnp.broadcast_to(v[None], (N_DEV,) + v.shape))


def _other_chips(x, y):
    return [(1 - x, y), (x, 1 - y), (1 - x, 1 - y)]


GATHER_SEMS = 6


def _gather_buffer(v):
    return jnp.broadcast_to(v[None], (N_CHIPS,) + v.shape)


def _gather_start(bufs, send_sems, recv_sems):
    x, y, c = _place()
    for i, buf in enumerate(bufs):
        mine = buf.at[2 * x + y, c]
        for j, (cx, cy) in enumerate(_other_chips(x, y)):
            _remote(mine, mine, send_sems, recv_sems, GATHER_SEMS * i + j, (cx, cy, c)).start()


def _gather_finish(bufs, send_sems, recv_sems):
    x, y, c = _place()
    chips = _other_chips(x, y)
    passed = []
    for i, buf in enumerate(bufs):
        mine = buf.at[2 * x + y, c]
        for j, (cx, cy) in enumerate(chips):
            landed = buf.at[2 * cx + cy, c]
            _remote(mine, landed, send_sems, recv_sems, GATHER_SEMS * i + j, (x, y, c)).wait_recv()
            fwd = _remote(landed, landed, send_sems, recv_sems, GATHER_SEMS * i + 3 + j, (x, y, 1 - c))
            fwd.start()
            passed.append(fwd)
    for i, buf in enumerate(bufs):
        mine = buf.at[2 * x + y, c]
        for j, (cx, cy) in enumerate(chips):
            _remote(mine, buf.at[2 * cx + cy, 1 - c], send_sems, recv_sems, GATHER_SEMS * i + 3 + j, (x, y, c)).wait_recv()
            _remote(mine, mine, send_sems, recv_sems, GATHER_SEMS * i + j, (cx, cy, c)).wait_send()
    for fwd in passed:
        fwd.wait_send()


def _gather_rider(bufs):
    return _Rider(bufs, GATHER_SEMS * len(bufs), _gather_start, _gather_finish)


def _chip_gather(bufs, name):
    n = len(bufs)

    def body(*refs):
        out_refs, (send_sems, recv_sems) = refs[n:2 * n], refs[2 * n:]
        _gather_start(out_refs, send_sems, recv_sems)
        _gather_finish(out_refs, send_sems, recv_sems)

    return pl.pallas_call(
        body, name=name, in_specs=[ANY] * n, out_specs=[ANY] * n, input_output_aliases={i: i for i in range(n)},
        out_shape=[jax.ShapeDtypeStruct(b.shape, b.dtype) for b in bufs], scratch_shapes=_dma_sems(GATHER_SEMS * n),
    )(*bufs)


RS_W = 1024
RS_BLOCK_BYTES = 4 << 20


def _dma_sems(n):
    return [pltpu.SemaphoreType.DMA((n,)), pltpu.SemaphoreType.DMA((n,))]


def _pair_exchange_copies(refs, send_sems, recv_sems):
    n = len(refs) // 2
    x, y, c = _place()
    return [_remote(refs[i].at[s, 1 - c], refs[n + i].at[s], send_sems, recv_sems, N_CHIPS * i + s, (x, y, 1 - c))
            for i in range(n) for s in range(N_CHIPS)]


def _pair_exchange_start(refs, send_sems, recv_sems):
    for cp in _pair_exchange_copies(refs, send_sems, recv_sems):
        cp.start()


def _pair_exchange_finish(refs, send_sems, recv_sems):
    for cp in _pair_exchange_copies(refs, send_sems, recv_sems):
        cp.wait()


def _pair_exchange_rider(gs):
    landing = [lax.empty((N_CHIPS,) + g.shape[2:], g.dtype) for g in gs]
    return _Rider(list(gs) + landing, N_CHIPS * len(gs), _pair_exchange_start, _pair_exchange_finish)


def _rs_rows(rows, cols):
    cap = max(16, RS_BLOCK_BYTES // (N_CHIPS * 4 * cols))
    return rows if rows <= cap else max(d for d in range(16, cap + 1, 16) if rows % d == 0)


def _rs_pair_add(g, got, c_arr, name):
    _, _, rows, width = g.shape
    tr = _rs_rows(rows, width)

    def body(c_ref, g_ref, got_ref, p_ref):
        p_ref[...] = (g_ref[...] + got_ref[...]).astype(BF16)

    return pl.pallas_call(
        body, name=name,
        grid_spec=pltpu.PrefetchScalarGridSpec(
            num_scalar_prefetch=1, grid=(rows // tr,),
            in_specs=[pl.BlockSpec((N_CHIPS, None, tr, width), lambda i, c_ref: (0, c_ref[0], i, 0)),
                      pl.BlockSpec((N_CHIPS, tr, width), lambda i, c_ref: (0, i, 0))],
            out_specs=pl.BlockSpec((N_CHIPS, tr, width), lambda i, c_ref: (0, i, 0))),
        out_shape=jax.ShapeDtypeStruct((N_CHIPS, rows, width), BF16), compiler_params=_cparams("parallel"),
    )(c_arr, g, got)


def _chip_exchange_copies(refs, send_sems, recv_sems):
    n = len(refs) // 2
    x, y, c = _place()
    return [_remote(refs[i].at[2 * cx + cy], refs[n + i].at[j], send_sems, recv_sems, 3 * i + j, (cx, cy, c))
            for i in range(n) for j, (cx, cy) in enumerate(_other_chips(x, y))]


def _chip_exchange_start(refs, send_sems, recv_sems):
    for cp in _chip_exchange_copies(refs, send_sems, recv_sems):
        cp.start()


def _chip_exchange_finish(refs, send_sems, recv_sems):
    n = len(refs) // 2
    x, y, c = _place()
    for i in range(n):
        for j in range(3):
            _remote(refs[i].at[2 * x + y], refs[n + i].at[j], send_sems, recv_sems, 3 * i + j, (x, y, c)).wait_recv()
    for cp in _chip_exchange_copies(refs, send_sems, recv_sems):
        cp.wait_send()


def _chip_exchange_buffers(ps):
    return [lax.empty((3,) + p.shape[1:], p.dtype) for p in ps]


def _chip_exchange_rider(ps):
    return _Rider(list(ps) + _chip_exchange_buffers(ps), 3 * len(ps), _chip_exchange_start, _chip_exchange_finish)


def _rs_chip_add(p, q, idx, name):
    _, rows, width = q.shape
    tr = _rs_rows(rows, width)

    def body(idx_ref, p_ref, q_ref, r_ref):
        qv = q_ref[...].astype(F32)
        r_ref[...] = ((p_ref[...].astype(F32) + qv[0]) + qv[1]) + qv[2]

    return pl.pallas_call(
        body, name=name,
        grid_spec=pltpu.PrefetchScalarGridSpec(
            num_scalar_prefetch=1, grid=(rows // tr,),
            in_specs=[pl.BlockSpec((None, tr, width), lambda i, idx_ref: (idx_ref[0], i, 0)),
                      pl.BlockSpec((3, tr, width), lambda i, idx_ref: (0, i, 0))],
            out_specs=pl.BlockSpec((None, tr, width), lambda i, idx_ref: (idx_ref[1], i, 0))),
        out_shape=jax.ShapeDtypeStruct((2, rows, width), F32), compiler_params=_cparams("parallel"),
    )(idx, p, q)


def _rs_pair_share(rs, name):
    n = len(rs)

    def body(*refs):
        out_refs, (send_sems, recv_sems) = refs[n:2 * n], refs[2 * n:]
        x, y, c = _place()
        sends = [_remote(out_refs[i].at[c], out_refs[i].at[c], send_sems, recv_sems, i, (x, y, 1 - c)) for i in range(n)]
        for cp in sends:
            cp.start()
        for i in range(n):
            _remote(out_refs[i].at[c], out_refs[i].at[1 - c], send_sems, recv_sems, i, (x, y, c)).wait_recv()
        for cp in sends:
            cp.wait_send()

    return pl.pallas_call(
        body, name=name, in_specs=[ANY] * n, out_specs=[ANY] * n, input_output_aliases={i: i for i in range(n)},
        out_shape=[jax.ShapeDtypeStruct(r.shape, r.dtype) for r in rs], scratch_shapes=_dma_sems(n),
    )(*rs)


def _rs_pair_sums(gs, gots, core, tag):
    c_arr = core.astype(jnp.int32).reshape(1)
    return [_rs_pair_add(g, got, c_arr, f"{tag}_pair_add{i}") for i, (g, got) in enumerate(zip(gs, gots))]


def _rs_finish(ps, qs, chip, core, tag):
    idx = jnp.stack([chip, core]).astype(jnp.int32)
    rs = [_rs_chip_add(p, q, idx, f"{tag}_chip_add{i}") for i, (p, q) in enumerate(zip(ps, qs))]
    return _rs_pair_share(rs, tag + "_share")


def _sum_leading(a, name):
    n, rows, width = a.shape
    cap = max(8, RS_BLOCK_BYTES // (n * 4 * width))
    tr = rows if rows <= cap else max(d for d in range(8, cap + 1, 8) if rows % d == 0)

    def body(a_ref, o_ref):
        acc = a_ref[0]
        for d in range(1, n):
            acc = acc + a_ref[d]
        o_ref[...] = acc

    return pl.pallas_call(
        body, name=name, grid=(rows // tr,), in_specs=[pl.BlockSpec((n, tr, width), lambda i: (0, i, 0))],
        out_specs=pl.BlockSpec((tr, width), lambda i: (i, 0)), out_shape=jax.ShapeDtypeStruct((rows, width), F32),
        compiler_params=_cparams("parallel"),
    )(a)


def _pair_swap(v, name):
    def body(v_ref, got_ref, send_sems, recv_sems):
        x, y, c = _place()
        cp = _remote(v_ref, got_ref, send_sems, recv_sems, 0, (x, y, 1 - c))
        cp.start()
        cp.wait()

    return pl.pallas_call(body, name=name, in_specs=[ANY], out_specs=ANY, out_shape=jax.ShapeDtypeStruct(v.shape, v.dtype),
                          scratch_shapes=_dma_sems(1))(v)


def _replicated_pair_sum(v, name):
    rows, width = v.shape
    pair = _sum_leading(jnp.stack([v, _pair_swap(v, name + "_swap")]), name + "_pair_add")
    return _gather_buffer(pair.reshape(2, rows // 2, width))


def _replicated_chip_sum(gathered, name):
    return _sum_leading(gathered.reshape(N_CHIPS, -1, gathered.shape[-1]), name + "_chip_add")


MOD_COLS = 3 * D // N_CHIPS
MOD_TK = 512


def _mod_partial(c_all, mod_w, name):
    nk = D // MOD_TK

    def body(c_ref, w_ref, o_ref):
        l = pl.program_id(1)
        part = _bdot_nn(jax.nn.silu(c_ref[...]), w_ref[0])
        _accum(o_ref.at[0], part, l == 0)

    return pl.pallas_call(
        body, name=name, grid=(DEPTH, nk),
        in_specs=[pl.BlockSpec((N_DEV, MOD_TK), lambda i, l: (0, l)), pl.BlockSpec((1, MOD_TK, MOD_COLS), lambda i, l: (i, l, 0))],
        out_specs=pl.BlockSpec((1, N_DEV, MOD_COLS), lambda i, l: (i, 0, 0)),
        out_shape=jax.ShapeDtypeStruct((DEPTH, N_DEV, MOD_COLS), F32), compiler_params=_cparams("parallel", "arbitrary"),
    )(c_all, mod_w)


def _mod_w_grad(c_all, dmod, name):
    def body(c_ref, d_ref, o_ref):
        o_ref[0] = _dg(jax.nn.silu(c_ref[...]).astype(BF16), d_ref[0].astype(BF16), _TN)

    return pl.pallas_call(
        body, name=name, grid=(DEPTH, D // MOD_TK),
        in_specs=[pl.BlockSpec((N_DEV, MOD_TK), lambda i, l: (0, l)), pl.BlockSpec((1, N_DEV, MOD_COLS), lambda i, l: (i, 0, 0))],
        out_specs=pl.BlockSpec((1, MOD_TK, MOD_COLS), lambda i, l: (i, l, 0)),
        out_shape=jax.ShapeDtypeStruct((DEPTH, D, MOD_COLS), F32), compiler_params=_cparams("parallel", "parallel"),
    )(c_all, dmod)


ADAM_BLOCK_BYTES = 1 << 20


def _adamw(w, g, m, v, name, rider=None):
    shape = w.shape
    cols = shape[-1]
    rows = w.size // cols
    w, g, m, v = (a.reshape(rows, cols) for a in (w, g, m, v))
    cap = max(8, ADAM_BLOCK_BYTES // (4 * cols))
    tr = rows if rows <= cap else max(d for d in range(8, cap + 1, 8) if rows % d == 0)
    c1 = 1.0 - ADAM_B1 ** ADAM_STEP
    c2 = 1.0 - ADAM_B2 ** ADAM_STEP

    def body(w_ref, g_ref, m_ref, v_ref, d_ref, nm_ref, nv_ref):
        gv = g_ref[...]
        mn = ADAM_B1 * m_ref[...] + (1.0 - ADAM_B1) * gv
        vn = ADAM_B2 * v_ref[...] + (1.0 - ADAM_B2) * (gv * gv)
        nm_ref[...] = mn
        nv_ref[...] = vn
        d_ref[...] = -ADAM_LR * ((mn / c1) / (jnp.sqrt(vn / c2) + ADAM_EPS) + ADAM_WD * w_ref[...])

    spec = pl.BlockSpec((tr, cols), lambda i: (i, 0))
    out = jax.ShapeDtypeStruct((rows, cols), F32)
    (d, nm, nv), ridden = _ridden(_compute_call(
        body, (w, g, m, v), name=name, grid=(rows // tr,), in_specs=[spec] * 4, out_specs=[spec] * 3, out_shape=[out] * 3,
        semantics=("parallel",), rider=rider), rider)
    res = (d.reshape(shape), nm.reshape(shape), nv.reshape(shape))
    return res if rider is None else (res, ridden)


W_NAMES = ("norm_g", "mod_w", "mod_b", "final_norm_g", "sg_w_in", "sg_w_out", "sg_ln_g", "sg_ln_b", "sg_w_spatial",
           "sg_b_spatial", "swa_w_in", "swa_w_out", "swa_sinks", "rwkv_w_in", "rwkv_w_out", "rwkv_mu", "rwkv_w0",
           "rwkv_w_lora", "rwkv_a0", "rwkv_a_lora", "rwkv_k_k", "rwkv_k_a", "rwkv_r_k", "rwkv_gn_g", "rwkv_gn_b")
SMALL = {"sg_ln_g": 1, "sg_ln_b": 1, "rwkv_mu": 1, "rwkv_w0": 1, "rwkv_w_lora": 2, "rwkv_a0": 1, "rwkv_a_lora": 2,
         "rwkv_k_k": 1, "rwkv_k_a": 1, "rwkv_gn_g": 1, "rwkv_gn_b": 1}
REPLICATED = ("norm_g", "final_norm_g", "sg_w_spatial", "sg_b_spatial", "swa_sinks", "rwkv_r_k")
KINDS = ("sg", "swa", "rwkv", "sg")
RW_SPLIT = 704


def _pad_to(flat, n):
    return jnp.pad(flat, (0, n - flat.shape[0]))


def _round_up(n, m):
    return -(-n // m) * m


def _join_shards(gathered, axis):
    return jnp.concatenate([gathered[s] for s in range(N_CHIPS)], axis=axis)


def _chip_blocks(full, axis):
    return jnp.stack(jnp.split(full, N_CHIPS, axis=axis)).reshape(N_CHIPS, -1)


def _weight_buffer(w):
    rows, cols = w.shape
    return _gather_buffer(w.astype(BF16).reshape(2, rows // 2, cols))


def _chip_shards(buf):
    return buf.reshape(N_CHIPS, -1, buf.shape[-1])


def _small_buffer(shards):
    flat = jnp.concatenate([shards[n].reshape(-1) for n in SMALL])
    rows = _round_up(flat.shape[0], 2 * 8 * LANES) // (2 * LANES)
    return _gather_buffer(_pad_to(flat, 2 * rows * LANES).reshape(2, rows, LANES))


def _unpack_small(buf, shards):
    got = buf.reshape(N_CHIPS, -1)
    out, off = {}, 0
    for n, axis in SMALL.items():
        size = shards[n].size
        out[n] = _join_shards(got[:, off:off + size].reshape((N_CHIPS,) + shards[n].shape), axis)
        off += size
    return out


def _lora_pad_rows(w):
    return jnp.pad(w, ((0, LORA_PAD - LORA), (0, 0)))


def _lo_cols(a):
    z = jnp.zeros(a.shape[:-1] + (LORA_PAD - LORA,), a.dtype)
    return jnp.concatenate([a[..., :LORA], z, a[..., LORA:], z], axis=-1)


def _lo_cols_inv(a):
    return jnp.concatenate([a[..., :LORA], a[..., LORA_PAD:LORA_PAD + LORA]], axis=-1)


def _transposed(w_blocks):
    return jnp.swapaxes(w_blocks, 1, 2).reshape(-1, w_blocks.shape[1])


def _rows_dim_major(w):
    return w.reshape(N_HEADS, HEAD, -1).swapaxes(0, 1).reshape(w.shape)


def _rows_head_major(w):
    return w.reshape(HEAD, N_HEADS, -1).swapaxes(0, 1).reshape(w.shape)


def kernel(x, c, positions, norm_g, mod_w, mod_b, final_norm_g, sg_w_in, sg_w_out, sg_ln_g, sg_ln_b, sg_w_spatial,
           sg_b_spatial, swa_w_in, swa_w_out, swa_sinks, rwkv_w_in, rwkv_w_out, rwkv_mu, rwkv_w0, rwkv_w_lora, rwkv_a0,
           rwkv_a_lora, rwkv_k_k, rwkv_k_a, rwkv_r_k, rwkv_gn_g, rwkv_gn_b, loss_target, m_norm_g, m_mod_w, m_mod_b,
           m_final_norm_g, m_sg_w_in, m_sg_w_out, m_sg_ln_g, m_sg_ln_b, m_sg_w_spatial, m_sg_b_spatial, m_swa_w_in,
           m_swa_w_out, m_swa_sinks, m_rwkv_w_in, m_rwkv_w_out, m_rwkv_mu, m_rwkv_w0, m_rwkv_w_lora, m_rwkv_a0,
           m_rwkv_a_lora, m_rwkv_k_k, m_rwkv_k_a, m_rwkv_r_k, m_rwkv_gn_g, m_rwkv_gn_b, v_norm_g, v_mod_w, v_mod_b,
           v_final_norm_g, v_sg_w_in, v_sg_w_out, v_sg_ln_g, v_sg_ln_b, v_sg_w_spatial, v_sg_b_spatial, v_swa_w_in,
           v_swa_w_out, v_swa_sinks, v_rwkv_w_in, v_rwkv_w_out, v_rwkv_mu, v_rwkv_w0, v_rwkv_w_lora, v_rwkv_a0,
           v_rwkv_a_lora, v_rwkv_k_k, v_rwkv_k_a, v_rwkv_r_k, v_rwkv_gn_g, v_rwkv_gn_b):
    given = dict(locals())
    w = {n: given[n] for n in W_NAMES}
    xi, yi, ci = _place()
    chip = 2 * xi + yi
    me = 4 * xi + 2 * yi + ci
    xs = [x[0]]

    c_all = _all_gather8(c, "gather_c")[:, 0, :]
    mod_part = _mod_partial(c_all, mod_w, "mod_fwd")
    mod_all = _all_gather8(mod_part, "gather_mod")[::2]
    mod_mine = lax.dynamic_index_in_dim(mod_all, me, axis=2, keepdims=False)
    mod = mod_mine.transpose(1, 0, 2).reshape(DEPTH, 3 * D) + mod_b
    shift, scale, gate = mod[:, :D], mod[:, D:2 * D], mod[:, 2 * D:]

    shards = {"sg_w_in0": sg_w_in[0], "sg_w_out0": sg_w_out[0], "swa_w_in": swa_w_in[0], "swa_w_out": swa_w_out[0],
              "rwkv_w_in_a": rwkv_w_in[0, :RW_SPLIT], "rwkv_w_in_b": rwkv_w_in[0, RW_SPLIT:], "rwkv_w_out": rwkv_w_out[0],
              "sg_w_in1": sg_w_in[1], "sg_w_out1": sg_w_out[1]}
    bufs = {n: _weight_buffer(s) for n, s in shards.items()}
    fwd_riders = {(0, "in"): ["swa_w_in"], (0, "mix"): ["swa_w_out"], (1, "in"): ["rwkv_w_in_a"], (1, "mix"): ["rwkv_w_in_b"],
                  (2, "in"): ["rwkv_w_out"], (2, "mix"): ["sg_w_in1", "sg_w_out1"]}
    bufs["sg_w_in0"], bufs["sg_w_out0"], small_buf = _chip_gather(
        [bufs["sg_w_in0"], bufs["sg_w_out0"], _small_buffer(w)], "gather_l0")
    full = _unpack_small(small_buf, w)

    def riding(i, where):
        names = fwd_riders.get((i, where))
        return names, (None if names is None else _gather_rider([bufs[n] for n in names]))

    def arrived(names, ridden):
        for n, b in zip(names or [], ridden):
            bufs[n] = b

    sg_in = lambda j: _chip_shards(bufs[f"sg_w_in{j}"])
    sg_out = lambda j: bufs[f"sg_w_out{j}"].reshape(D, D)
    mu = full["rwkv_mu"][0]
    rw_prm = dict(mu_main=_dim_major(mu[:RW_MAIN].reshape(4, D)).reshape(1, RW_MAIN), mu_lo=_lo_cols(mu[None, RW_MAIN:]),
                  w0=_dim_major(full["rwkv_w0"]), a0=_dim_major(full["rwkv_a0"]),
                  wl=_lora_pad_rows(_dim_major(full["rwkv_w_lora"][0])), al=_lora_pad_rows(_dim_major(full["rwkv_a_lora"][0])),
                  kkp=_param_compact(full["rwkv_k_k"][0]), kap=_param_compact(full["rwkv_k_a"][0]),
                  rkp=_param_compact(rwkv_r_k.reshape(-1)),
                  gn_g=_param_compact(full["rwkv_gn_g"][0]), gn_b=_param_compact(full["rwkv_gn_b"][0]))
    bs_t = [jnp.pad(sg_b_spatial[j].T, ((0, 0), (0, LANES - SG_GROUPS))) for j in range(2)]
    sink_row = jnp.pad(swa_sinks, ((0, 0), (0, LANES - N_HEADS)))
    inv_freq = ROPE_THETA ** (-jnp.arange(HEAD // 2, dtype=F32) / (HEAD // 2))
    ang = positions[0].astype(F32)[:, None] * inv_freq
    cos, sin = jnp.tile(jnp.cos(ang), (1, LANES * 2 // HEAD)), jnp.tile(jnp.sin(ang), (1, LANES * 2 // HEAD))

    def row(a, i):
        return a[i:i + 1]

    hs, ps, us, ys, rw_saved = [], [], [], [], None
    for i, kind in enumerate(KINDS):
        j = i // 3
        tag = f"l{i}_{kind}"
        h = _norm_mod_fwd(xs[i], row(norm_g, i), row(shift, i), row(scale, i), tag + "_norm")
        names_in, rider_in = riding(i, "in")
        names_mix, rider_mix = riding(i, "mix")
        if kind == "sg":
            p, ridden = _ridden(_matmul(h, sg_in(j), "nn", tag + "_in", blocked=True, rider=rider_in), rider_in)
            arrived(names_in, ridden)
            u, ridden = _ridden(_sg_fwd(p, row(full["sg_ln_g"], j), row(full["sg_ln_b"], j), sg_w_spatial[j], bs_t[j],
                                        tag + "_mix", rider_mix), rider_mix)
            w_out = sg_out(j)
        elif kind == "swa":
            swa_in, swa_out = _chip_shards(bufs["swa_w_in"]), bufs["swa_w_out"].reshape(D, D)
            p, ridden = _ridden(_matmul(h, swa_in, "nn", tag + "_in", blocked=True, rider=rider_in), rider_in)
            arrived(names_in, ridden)
            u, ridden = _ridden(_swa_fwd(p, cos, sin, sink_row, tag + "_mix", rider_mix), rider_mix)
            w_out = swa_out
        else:
            rw_in = jnp.concatenate([_join_shards(_chip_shards(bufs[n]), axis=1) for n in ("rwkv_w_in_a", "rwkv_w_in_b")])
            rw_main = _dim_major(rw_in[:, :RW_MAIN].reshape(D, 4, D)).reshape(D, RW_MAIN)
            rw_lo = _lo_cols(rw_in[:, RW_MAIN:])
            p_main, ridden = _ridden(_matmul(h, rw_main, "nn", tag + "_in", rider=rider_in), rider_in)
            arrived(names_in, ridden)
            rw_out = _rows_dim_major(bufs["rwkv_w_out"].reshape(D, D))
            p = (p_main, _matmul(h, rw_lo, "nn", tag + "_in_lo"))
            u, rw_saved, ridden = _rwkv_mixer_fwd(p[0], p[1], rw_prm, tag, rider_mix)
            w_out = rw_out
        arrived(names_mix, ridden)
        y, x_next = _out_proj_resid(u, w_out, xs[i], row(gate, i), tag + "_out")
        xs.append(x_next)
        hs.append(h), ps.append(p), us.append(u), ys.append(y)

    loss_part, dx, d_final_g, dy, d_gate = _final_loss_grad(xs[DEPTH], final_norm_g[None], loss_target[0], ys[DEPTH - 1],
                                                            row(gate, DEPTH - 1), "loss")
    loss = lax.psum(loss_part[0, 0], ("x", "y", "c"))

    gfull = {n: [None, None] for n in ("sg_ln_g", "sg_ln_b", "sg_w_spatial", "sg_b_spatial")}
    gbig = {}
    d_norm_g, d_mod = [None] * DEPTH, [None] * DEPTH
    rs_p, rs_q, riding_names = {}, {}, []

    for i in reversed(range(DEPTH)):
        kind, j = KINDS[i], i // 3
        tag = f"l{i}_{kind}_b"
        rider = _chip_exchange_rider([rs_p[n] for n in riding_names]) if riding_names else None
        w_out = {"sg": sg_out(j), "swa": swa_out, "rwkv": rw_out}[kind]
        du = _matmul(dy, w_out, "nt", tag + "_du")
        dw_out = _matmul(us[i], dy, "tn", tag + "_dwout").reshape(N_CHIPS, D // N_CHIPS, D)
        if kind == "sg":
            (dp, dlg, dlb, dws, dbs), ridden = _ridden(
                _sg_bwd(ps[i], row(full["sg_ln_g"], j), row(full["sg_ln_b"], j), sg_w_spatial[j], bs_t[j], du,
                        tag + "_mix", rider), rider)
            gfull["sg_ln_g"][j], gfull["sg_ln_b"][j] = dlg[0], dlb[0]
            gfull["sg_w_spatial"][j], gfull["sg_b_spatial"][j] = dws, dbs[:, :SG_GROUPS].T
            gbig[f"sg_w_in{j}"] = _matmul(hs[i], dp, "tn", tag + "_dwin", blocked=True)
            gbig[f"sg_w_out{j}"] = dw_out
            dh, dh2 = _matmul(dp, _transposed(sg_in(j)), "nn", tag + "_dh"), None
            mine = [f"sg_w_in{j}", f"sg_w_out{j}"]
        elif kind == "swa":
            (dp, dsk), ridden = _ridden(_swa_bwd(ps[i], cos, sin, sink_row, du, tag + "_mix", rider), rider)
            gfull["swa_sinks"] = dsk[:, :N_HEADS]
            gbig["swa_w_in"] = _matmul(hs[i], dp, "tn", tag + "_dwin", blocked=True)
            gbig["swa_w_out"] = dw_out
            dh, dh2 = _matmul(dp, _transposed(swa_in), "nn", tag + "_dh"), None
            mine = ["swa_w_in", "swa_w_out"]
        else:
            dpm, dpl, rg, ridden = _rwkv_mixer_bwd(ps[i][0], ps[i][1], rw_prm, rw_saved, du, tag, rider)
            mine = ["rwkv_w_in", "rwkv_w_out"]
            dw_main = _matmul(hs[i], dpm, "tn", tag + "_dwin")
            dw_lo = _matmul(hs[i], dpl, "tn", tag + "_dwin_lo")
            dw_main = _head_major(dw_main.reshape(D, 4, D)).reshape(D, RW_MAIN)
            dw_in = jnp.concatenate([dw_main, _lo_cols_inv(dw_lo)], axis=1)
            gbig["rwkv_w_in"] = dw_in.reshape(D, N_CHIPS, -1).transpose(1, 0, 2)
            gbig["rwkv_w_out"] = _rows_head_major(dw_out.reshape(D, D)).reshape(dw_out.shape)
            dmu_main = _head_major(rg["mu_main"].reshape(4, D)).reshape(1, RW_MAIN)
            gfull["rwkv_mu"] = jnp.concatenate([dmu_main, _lo_cols_inv(rg["mu_lo"])], axis=1)
            gfull["rwkv_w0"], gfull["rwkv_a0"] = _head_major(rg["w0"]), _head_major(rg["a0"])
            gfull["rwkv_w_lora"], gfull["rwkv_a_lora"] = _head_major(rg["wl"])[None, :LORA], _head_major(rg["al"])[None, :LORA]
            gfull["rwkv_k_k"], gfull["rwkv_k_a"] = _param_compact_inv(rg["kkp"])[None], _param_compact_inv(rg["kap"])[None]
            gfull["rwkv_r_k"] = _param_compact_inv(rg["rkp"]).reshape(1, N_HEADS, HEAD)
            gfull["rwkv_gn_g"], gfull["rwkv_gn_b"] = _param_compact_inv(rg["gn_g"])[None], _param_compact_inv(rg["gn_b"])[None]
            dh, dh2 = _matmul(dpm, rw_main, "nt", tag + "_dh"), _matmul(dpl, rw_lo, "nt", tag + "_dh_lo")
        rs_p.update(zip(riding_names, ridden[:len(riding_names)]))
        rs_q.update(zip(riding_names, ridden[len(riding_names):]))
        if i == 0:
            for n in ("sg_ln_g", "sg_ln_b"):
                gfull[n] = jnp.stack(gfull[n])
            small = jnp.concatenate([_chip_blocks(gfull[n], axis) for n, axis in SMALL.items()], axis=1)
            small_rows = _round_up(small.shape[1], 2 * 16 * LANES) // LANES
            small = jnp.pad(small, ((0, 0), (0, small_rows * LANES - small.shape[1])))
            gbig["small"] = small.reshape(N_CHIPS, small_rows, LANES)
            mine = mine + ["small"]
        gs = [gbig[n].reshape(N_CHIPS, 2, gbig[n].shape[1] // 2, gbig[n].shape[2]) for n in mine]
        pair_rider = _pair_exchange_rider(gs)
        below = (ys[i - 1], row(gate, i - 1)) if i > 0 else None
        (dx, dg, dsh, dsc, *below_grads), gots = _norm_mod_bwd(xs[i], row(norm_g, i), row(shift, i), row(scale, i), dh, dx,
                                                               tag + "_norm", dh2, pair_rider, below)
        d_norm_g[i] = dg[0]
        d_mod[i] = jnp.concatenate([dsh[0], dsc[0], d_gate[0]])
        if below is not None:
            dy, d_gate = below_grads
        rs_p.update(zip(mine, _rs_pair_sums(gots[:len(gs)], gots[len(gs):], ci, f"rs{i}")))
        riding_names = mine
    for n in ("sg_w_spatial", "sg_b_spatial"):
        gfull[n] = jnp.stack(gfull[n])
    gfull["norm_g"], gfull["final_norm_g"] = jnp.stack(d_norm_g), d_final_g[0]

    grads, deltas, new_m, new_v, red = {}, {}, {}, {}, {}

    def finish(names, tag):
        outs = _rs_finish([rs_p[n] for n in names], [rs_q[n] for n in names], chip, ci, tag)
        red.update({n: r.reshape(-1, r.shape[2]) for n, r in zip(names, outs)})

    def adamw(n, rider=None):
        res = _adamw(w[n], grads[n], given["m_" + n], given["v_" + n], "adamw_" + n, rider)
        (deltas[n], new_m[n], new_v[n]), ridden = _ridden(res, rider)
        return ridden

    def ride_exchange(names, on):
        ridden = adamw(on, _chip_exchange_rider([rs_p[n] for n in names]))
        rs_p.update(zip(names, ridden[:len(names)]))
        rs_q.update(zip(names, ridden[len(names):]))

    rep_flat = jnp.concatenate([gfull[n].reshape(-1) for n in REPLICATED])
    rep_rows = _round_up(rep_flat.shape[0], 32 * RS_W) // RS_W
    rep_buffer = _replicated_pair_sum(_pad_to(rep_flat, rep_rows * RS_W).reshape(rep_rows, RS_W), "rep")

    finish(sorted(set(rs_p) - set(riding_names)), "rs_a")
    for n in ("swa_w_in", "swa_w_out", "rwkv_w_in", "rwkv_w_out"):
        grads[n] = red[n][None]
    dmod_all = _all_gather8(jnp.stack(d_mod).reshape(DEPTH * 3 * D // RS_W, RS_W), "gather_dmod")
    grads["mod_b"] = _sum_leading(dmod_all, "sum_dmod").reshape(DEPTH, 3 * D)
    dmod_all = dmod_all.reshape(N_DEV, DEPTH, 3 * D)
    dmod_cols = lax.dynamic_slice_in_dim(dmod_all, chip * MOD_COLS, MOD_COLS, axis=2).transpose(1, 0, 2)
    grads["mod_w"] = _mod_w_grad(c_all, dmod_cols, "mod_w_grad")
    ride_exchange(["sg_w_in0"], on="mod_w")
    ride_exchange(["sg_w_out0", "small"], on="rwkv_w_in")
    finish(riding_names, "rs_b")
    grads["sg_w_in"] = jnp.stack([red["sg_w_in0"], red["sg_w_in1"]])
    grads["sg_w_out"] = jnp.stack([red["sg_w_out0"], red["sg_w_out1"]])
    small_red, off = red["small"].reshape(-1), 0
    for n in SMALL:
        grads[n] = small_red[off:off + w[n].size].reshape(w[n].shape)
        off += w[n].size

    (rep_gathered,) = adamw("sg_w_in", _gather_rider([rep_buffer]))
    rep_sum, off = _replicated_chip_sum(rep_gathered, "rep").reshape(-1), 0
    for n in REPLICATED:
        grads[n] = rep_sum[off:off + w[n].size].reshape(w[n].shape)
        off += w[n].size

    for n in W_NAMES:
        if n not in deltas:
            adamw(n)
    return (loss, dx[None], *[grads[n] for n in W_NAMES], *[deltas[n] for n in W_NAMES],
            *[new_m[n] for n in W_NAMES], *[new_v[n] for n in W_NAMES])
```

```python
import functools
import math

import jax
import jax.numpy as jnp
from jax import lax
from jax.experimental import pallas as pl
from jax.experimental.pallas import tpu as pltpu

F32 = jnp.float32
BF16 = jnp.bfloat16
HIGHEST = lax.Precision.HIGHEST

D = 2048
DEPTH = 4
CHUNK = 128
SG_GROUPS = 16
HEAD = 64
N_HEADS = D // HEAD
KV_HEADS = 4
KVW = KV_HEADS * HEAD
ROPE_THETA = 10000.0
LORA = 96
LORA_PAD = 128
DECAY_SCALE = math.exp(-0.5)
GN_EPS = 64e-5
RMS_EPS = 1e-6
LN_EPS = 1e-5
ADAM_LR, ADAM_B1, ADAM_B2, ADAM_EPS, ADAM_WD, ADAM_STEP = 0.001, 0.9, 0.999, 1e-08, 0.01, 10
LANES = 128
NEG = -1e30
VMEM_LIMIT = 56 * 1024 * 1024

MESHT = pl.DeviceIdType.MESH


def _cparams(*sem):
    return pltpu.CompilerParams(dimension_semantics=sem, vmem_limit_bytes=VMEM_LIMIT)


class _Rider:
    def __init__(self, arrays, n_sems, start, finish):
        self.arrays, self.n_sems, self.start, self.finish = list(arrays), n_sems, start, finish


def _ridden(res, rider):
    return (res, []) if rider is None else res


def _compute_call(body, args, *, name, grid, in_specs, out_specs, out_shape, semantics, scratch_shapes=(), rider=None):
    if rider is None:
        return pl.pallas_call(body, name=name, grid=grid, in_specs=in_specs, out_specs=out_specs, out_shape=out_shape,
                              scratch_shapes=list(scratch_shapes), compiler_params=_cparams(*semantics))(*args)
    single = not isinstance(out_shape, (list, tuple))
    o_specs, o_shapes = ([out_specs], [out_shape]) if single else (list(out_specs), list(out_shape))
    n_in, n_out, n_r = len(in_specs), len(o_specs), len(rider.arrays)

    def with_rider(*refs):
        ins, outs = refs[:n_in], refs[n_in + n_r:n_in + n_r + n_out]
        ridden = refs[n_in + n_r + n_out:n_in + 2 * n_r + n_out]
        scratch, (send_sems, recv_sems) = refs[n_in + 2 * n_r + n_out:-2], refs[-2:]
        ids = [pl.program_id(d) for d in range(len(grid))]
        first = functools.reduce(jnp.logical_and, [i == 0 for i in ids])
        last = functools.reduce(jnp.logical_and, [i == g - 1 for i, g in zip(ids, grid)])

        @pl.when(first)
        def _():
            rider.start(ridden, send_sems, recv_sems)

        body(*ins, *outs, *scratch)

        @pl.when(last)
        def _():
            rider.finish(ridden, send_sems, recv_sems)

    any_spec = pl.BlockSpec(memory_space=pl.ANY)
    res = pl.pallas_call(
        with_rider, name=name, grid=grid, in_specs=list(in_specs) + [any_spec] * n_r, out_specs=o_specs + [any_spec] * n_r,
        out_shape=o_shapes + [jax.ShapeDtypeStruct(a.shape, a.dtype) for a in rider.arrays],
        input_output_aliases={n_in + i: n_out + i for i in range(n_r)},
        scratch_shapes=list(scratch_shapes) + [pltpu.SemaphoreType.DMA((rider.n_sems,))] * 2,
        compiler_params=_cparams(*["arbitrary"] * len(grid)),
    )(*args, *rider.arrays)
    return (res[0] if single else list(res[:n_out])), list(res[n_out:])


_NN = (((1,), (0,)), ((), ()))
_NT = (((1,), (1,)), ((), ()))
_TN = (((0,), (0,)), ((), ()))


def _dg(a, b, dims):
    return lax.dot_general(a, b, dims, preferred_element_type=F32)


@jax.custom_vjp
def _bdot_nn(a, b):
    return _dg(a.astype(BF16), b.astype(BF16), _NN)


def _bdot_nn_fwd(a, b):
    a, b = a.astype(BF16), b.astype(BF16)
    return _dg(a, b, _NN), (a, b)


def _bdot_nn_bwd(res, ct):
    a, b = res
    ct = ct.astype(BF16)
    return _dg(ct, b, _NT), _dg(a, ct, _TN)


_bdot_nn.defvjp(_bdot_nn_fwd, _bdot_nn_bwd)


@jax.custom_vjp
def _bdot_nt(a, b):
    return _dg(a.astype(BF16), b.astype(BF16), _NT)


def _bdot_nt_fwd(a, b):
    a, b = a.astype(BF16), b.astype(BF16)
    return _dg(a, b, _NT), (a, b)


def _bdot_nt_bwd(res, ct):
    a, b = res
    ct = ct.astype(BF16)
    return _dg(ct, b, _NN), _dg(ct, a, _TN)


_bdot_nt.defvjp(_bdot_nt_fwd, _bdot_nt_bwd)


def _tile(n, cap):
    if n <= cap:
        return n
    return max(d for d in range(LANES, cap + 1, LANES) if n % d == 0)


def _matmul(a, b, form, name, out_dtype=F32, blocked=False, rider=None, tm=1024, tn=None, tk=4096):
    if form == "nn":
        (m, k), n = a.shape, (N_CHIPS * b.shape[2] if blocked else b.shape[1])
    elif form == "nt":
        m, k, n = a.shape[0], a.shape[1], (b.shape[1] if blocked else b.shape[0])
    else:
        (k, m), n = a.shape, b.shape[1]
    per_chip = (k if form == "nt" else n) // N_CHIPS
    tn = tn or (512 if form == "tn" else 1024)
    if blocked and form == "nt":
        tk = _tile(per_chip, tk)
    elif blocked:
        tn = _tile(per_chip, tn)
    tm, tn, tk = _tile(m, tm), _tile(n, tn), _tile(k, tk)
    assert m % tm == 0 and n % tn == 0 and k % tk == 0, (name, a.shape, b.shape)
    nk = k // tk
    dims = {"nn": _NN, "nt": _NT, "tn": _TN}[form]
    a_spec = pl.BlockSpec((tk, tm), lambda i, j, l: (l, i)) if form == "tn" else pl.BlockSpec((tm, tk), lambda i, j, l: (i, l))
    b_spec = pl.BlockSpec((tn, tk), lambda i, j, l: (j, l)) if form == "nt" else pl.BlockSpec((tk, tn), lambda i, j, l: (l, j))
    o_spec = pl.BlockSpec((tm, tn), lambda i, j, l: (i, j))
    o_shape = (m, n)
    if blocked and form == "nn":
        pc = per_chip // tn
        b_spec = pl.BlockSpec((None, tk, tn), lambda i, j, l: (j // pc, l, j % pc))
    elif blocked and form == "nt":
        pc = per_chip // tk
        b_spec = pl.BlockSpec((None, tn, tk), lambda i, j, l: (l // pc, j, l % pc))
    elif blocked:
        pc = per_chip // tn
        o_spec = pl.BlockSpec((None, tm, tn), lambda i, j, l: (j // pc, i, j % pc))
        o_shape = (N_CHIPS, m, per_chip)

    def body(a_ref, b_ref, o_ref, acc_ref):
        part = _dg(a_ref[...], b_ref[...], dims)
        if nk == 1:
            o_ref[...] = part.astype(out_dtype)
        else:
            l = pl.program_id(2)

            @pl.when(l == 0)
            def _():
                acc_ref[...] = part

            @pl.when(l > 0)
            def _():
                acc_ref[...] += part

            @pl.when(l == nk - 1)
            def _():
                o_ref[...] = acc_ref[...].astype(out_dtype)

    return _compute_call(
        body, (a, b), name=name, grid=(m // tm, n // tn, nk),
        in_specs=[a_spec, b_spec], out_specs=o_spec, out_shape=jax.ShapeDtypeStruct(o_shape, out_dtype),
        scratch_shapes=[pltpu.VMEM((tm, tn) if nk > 1 else (8, LANES), F32)],
        semantics=("parallel", "parallel", "arbitrary"), rider=rider)


def _out_proj_resid(u, w_out, x, gate, name, tm=1024, tn=512):
    (m, k), n = u.shape, w_out.shape[1]

    def body(u_ref, w_ref, x_ref, g_ref, y_ref, xn_ref):
        y = _dg(u_ref[...], w_ref[...], _NN)
        y_ref[...] = y
        xn_ref[...] = x_ref[...] + g_ref[...] * y

    tile = pl.BlockSpec((tm, tn), lambda i, j: (i, j))
    out = jax.ShapeDtypeStruct((m, n), F32)
    return pl.pallas_call(
        body, name=name, grid=(m // tm, n // tn),
        in_specs=[pl.BlockSpec((tm, k), lambda i, j: (i, 0)), pl.BlockSpec((k, tn), lambda i, j: (0, j)), tile,
                  pl.BlockSpec((1, tn), lambda i, j: (0, j))],
        out_specs=[tile, tile], out_shape=[out, out], compiler_params=_cparams("parallel", "parallel"),
    )(u, w_out, x, gate)


TB_NORM = 256


def _f_norm_mod(x, g, shift, scale):
    xn = x * lax.rsqrt(jnp.mean(x * x, axis=-1, keepdims=True) + RMS_EPS)
    return (xn * g) * (1.0 + scale) + shift


def _row_spec(width, tb=TB_NORM):
    return pl.BlockSpec((tb, width), lambda i: (i, 0))


def _vec_spec(width, rows=1):
    return pl.BlockSpec((rows, width), lambda i: (0, 0))


def _norm_mod_fwd(x, g, shift, scale, name):
    t = x.shape[0]

    def body(x_ref, g_ref, sh_ref, sc_ref, h_ref):
        h_ref[...] = _f_norm_mod(x_ref[...], g_ref[...], sh_ref[...], sc_ref[...]).astype(BF16)

    return pl.pallas_call(
        body, name=name, grid=(t // TB_NORM,),
        in_specs=[_row_spec(D), _vec_spec(D), _vec_spec(D), _vec_spec(D)], out_specs=_row_spec(D),
        out_shape=jax.ShapeDtypeStruct((t, D), BF16), compiler_params=_cparams("parallel"),
    )(x, g, shift, scale)


def _accum(ref, val, first):
    @pl.when(first)
    def _():
        ref[...] = val

    @pl.when(jnp.logical_not(first))
    def _():
        ref[...] += val


def _gate_bwd_block(dx, y_ref, gate_ref, dy_ref, dgate_ref, first):
    dy_ref[...] = (dx * gate_ref[...]).astype(BF16)
    _accum(dgate_ref, jnp.sum(dx * y_ref[...], axis=0, keepdims=True), first)


def _gate_bwd_specs():
    return [_row_spec(D), _vec_spec(D)]


def _gate_bwd_shapes(t):
    return [jax.ShapeDtypeStruct((t, D), BF16), jax.ShapeDtypeStruct((1, D), F32)]


def _norm_mod_bwd(x, g, shift, scale, dh, dx_res, name, dh2=None, rider=None, below=None):
    t = x.shape[0]
    dhs = [dh] if dh2 is None else [dh, dh2]
    extra = [] if below is None else list(below)

    def body(x_ref, g_ref, sh_ref, sc_ref, dr_ref, *refs):
        dh_refs, refs = refs[:len(dhs)], refs[len(dhs):]
        below_refs, (dx_ref, dg_ref, dsh_ref, dsc_ref), below_out = refs[:len(extra)], refs[len(extra):len(extra) + 4], refs[len(extra) + 4:]
        _, vjp = jax.vjp(_f_norm_mod, x_ref[...], g_ref[...], sh_ref[...], sc_ref[...])
        dh_all = dh_refs[0][...]
        for r in dh_refs[1:]:
            dh_all = dh_all + r[...]
        dx, dg, dsh, dsc = vjp(dh_all)
        dx = dx + dr_ref[...]
        dx_ref[...] = dx
        first = pl.program_id(0) == 0
        _accum(dg_ref, dg, first)
        _accum(dsh_ref, dsh, first)
        _accum(dsc_ref, dsc, first)
        if below is not None:
            _gate_bwd_block(dx, *below_refs, *below_out, first)

    vec = jax.ShapeDtypeStruct((1, D), F32)
    return _compute_call(
        body, (x, g, shift, scale, dx_res, *dhs, *extra), name=name, grid=(t // TB_NORM,),
        in_specs=[_row_spec(D), _vec_spec(D), _vec_spec(D), _vec_spec(D), _row_spec(D)] + [_row_spec(D)] * len(dhs)
        + (_gate_bwd_specs() if extra else []),
        out_specs=[_row_spec(D), _vec_spec(D), _vec_spec(D), _vec_spec(D)] + (_gate_bwd_specs() if extra else []),
        out_shape=[jax.ShapeDtypeStruct((t, D), F32), vec, vec, vec] + (_gate_bwd_shapes(t) if extra else []),
        semantics=("arbitrary",), rider=rider)


def _f_final(x, g, target):
    xn = x * lax.rsqrt(jnp.mean(x * x, axis=-1, keepdims=True) + RMS_EPS)
    err = xn * g - target
    return 0.5 * jnp.sum(jnp.mean(err * err, axis=-1, keepdims=True), axis=0, keepdims=True)


def _final_loss_grad(x, g, target, y, gate, name):
    t = x.shape[0]

    def body(x_ref, g_ref, t_ref, y_ref, gate_ref, loss_ref, dx_ref, dg_ref, dy_ref, dgate_ref):
        loss, vjp = jax.vjp(_f_final, x_ref[...], g_ref[...], t_ref[...])
        dx, dg, _ = vjp(jnp.ones((1, 1), F32))
        dx_ref[...] = dx
        first = pl.program_id(0) == 0
        _accum(dg_ref, dg, first)
        _accum(loss_ref, jnp.broadcast_to(loss, (1, LANES)), first)
        _gate_bwd_block(dx, y_ref, gate_ref, dy_ref, dgate_ref, first)

    return pl.pallas_call(
        body, name=name, grid=(t // TB_NORM,),
        in_specs=[_row_spec(D), _vec_spec(D), _row_spec(D)] + _gate_bwd_specs(),
        out_specs=[_vec_spec(LANES), _row_spec(D), _vec_spec(D)] + _gate_bwd_specs(),
        out_shape=[jax.ShapeDtypeStruct((1, LANES), F32), jax.ShapeDtypeStruct((t, D), F32), jax.ShapeDtypeStruct((1, D), F32)]
        + _gate_bwd_shapes(t),
        compiler_params=_cparams("arbitrary"),
    )(x, g, target, y, gate)


def _group_selector():
    gi = lax.broadcasted_iota(jnp.int32, (LANES, D), 0)
    ci = lax.broadcasted_iota(jnp.int32, (LANES, D), 1)
    return (ci // (D // SG_GROUPS) == gi).astype(F32)


def _f_sg(p, ln_g, ln_b, w_s, bs_t):
    u, v, z = p[:, :D], p[:, D:2 * D], p[:, 2 * D:]
    u = jax.nn.gelu(u)
    vf = jax.nn.gelu(v)
    mean = jnp.mean(vf, axis=-1, keepdims=True)
    var = jnp.mean(jnp.square(vf - mean), axis=-1, keepdims=True)
    vn = (vf - mean) * lax.rsqrt(var + LN_EPS) * ln_g + ln_b
    ti = lax.broadcasted_iota(jnp.int32, (CHUNK, CHUNK), 0)
    si = lax.broadcasted_iota(jnp.int32, (CHUNK, CHUNK), 1)
    causal = si <= ti
    cg = D // SG_GROUPS
    f = jnp.concatenate(
        [_bdot_nn(jnp.where(causal, w_s[g], 0.0), vn[:, g * cg:(g + 1) * cg]) for g in range(SG_GROUPS)], axis=1)
    f = f + jnp.dot(bs_t, _group_selector(), precision=HIGHEST, preferred_element_type=F32)
    return u * f * jax.nn.silu(z)


def _sg_specs():
    return [pl.BlockSpec((CHUNK, 3 * D), lambda i: (i, 0)), _vec_spec(D), _vec_spec(D),
            pl.BlockSpec((SG_GROUPS, CHUNK, CHUNK), lambda i: (0, 0, 0)), _vec_spec(LANES, CHUNK)]


def _sg_fwd(p, ln_g, ln_b, w_s, bs_t, name, rider=None):
    t = p.shape[0]

    def body(p_ref, lg_ref, lb_ref, w_ref, b_ref, o_ref):
        o_ref[...] = _f_sg(p_ref[...], lg_ref[...], lb_ref[...], w_ref[...], b_ref[...]).astype(BF16)

    return _compute_call(
        body, (p, ln_g, ln_b, w_s, bs_t), name=name, grid=(t // CHUNK,), in_specs=_sg_specs(),
        out_specs=_row_spec(D, CHUNK), out_shape=jax.ShapeDtypeStruct((t, D), BF16), semantics=("parallel",), rider=rider)


def _sg_bwd(p, ln_g, ln_b, w_s, bs_t, dout, name, rider=None):
    t = p.shape[0]

    def body(p_ref, lg_ref, lb_ref, w_ref, b_ref, do_ref, dp_ref, dlg_ref, dlb_ref, dw_ref, db_ref):
        _, vjp = jax.vjp(_f_sg, p_ref[...], lg_ref[...], lb_ref[...], w_ref[...], b_ref[...])
        dp, dlg, dlb, dw, db = vjp(do_ref[...])
        dp_ref[...] = dp.astype(BF16)
        first = pl.program_id(0) == 0
        _accum(dlg_ref, dlg, first)
        _accum(dlb_ref, dlb, first)
        _accum(dw_ref, dw, first)
        _accum(db_ref, db, first)

    vec = jax.ShapeDtypeStruct((1, D), F32)
    return _compute_call(
        body, (p, ln_g, ln_b, w_s, bs_t, dout), name=name, grid=(t // CHUNK,), in_specs=_sg_specs() + [_row_spec(D, CHUNK)],
        out_specs=[pl.BlockSpec((CHUNK, 3 * D), lambda i: (i, 0)), _vec_spec(D), _vec_spec(D),
                   pl.BlockSpec((SG_GROUPS, CHUNK, CHUNK), lambda i: (0, 0, 0)), _vec_spec(LANES, CHUNK)],
        out_shape=[jax.ShapeDtypeStruct((t, 3 * D), BF16), vec, vec,
                   jax.ShapeDtypeStruct((SG_GROUPS, CHUNK, CHUNK), F32), jax.ShapeDtypeStruct((CHUNK, LANES), F32)],
        semantics=("arbitrary",), rider=rider)


SWA_COLS = 2 * D + 2 * KVW
KV_BLOCK = 2 * KVW


def _lane_roll(x, shift):
    return pltpu.roll(x, shift, 1)


def _rot_half(x):
    w = x.shape[1]
    lane = lax.broadcasted_iota(jnp.int32, x.shape, 1)
    return jnp.where(lane % HEAD < HEAD // 2, -_lane_roll(x, w - HEAD // 2), _lane_roll(x, HEAD // 2))


@jax.custom_vjp
def _rope(x, cos, sin):
    return x * cos + _rot_half(x) * sin


def _rope_fwd(x, cos, sin):
    return _rope(x, cos, sin), (cos, sin)


def _rope_bwd(res, ct):
    cos, sin = res
    return ct * cos - _rot_half(ct) * sin, jnp.zeros_like(cos), jnp.zeros_like(sin)


_rope.defvjp(_rope_fwd, _rope_bwd)


@jax.custom_vjp
def _swap_halves(x):
    return _lane_roll(x, HEAD)


_swap_halves.defvjp(lambda x: (_lane_roll(x, HEAD), None), lambda _, ct: (_lane_roll(ct, HEAD),))


def _f_swa(pq, pkv, cos, sin, cosp, sinp, sink_row, valid):
    reps = D // LANES
    q = _rope(pq[:, :D], jnp.tile(cos, (1, reps)), jnp.tile(sin, (1, reps))) * (HEAD ** -0.5)
    k = _rope(pq[:, D:D + KVW], jnp.tile(cos, (1, KVW // LANES)), jnp.tile(sin, (1, KVW // LANES)))
    kp = _rope(pkv[:, :KVW], jnp.tile(cosp, (1, KVW // LANES)), jnp.tile(sinp, (1, KVW // LANES)))
    v, vp, z = pq[:, D + KVW:D + 2 * KVW], pkv[:, KVW:], pq[:, D + 2 * KVW:]
    kcat = jnp.concatenate([kp, k], axis=0)
    vcat = jnp.concatenate([vp, v], axis=0)
    lane = lax.broadcasted_iota(jnp.int32, (2 * CHUNK, LANES), 1)
    lo = lane < HEAD
    hlane = lax.broadcasted_iota(jnp.int32, (1, LANES), 1)

    def halves(cat, g):
        blk = cat[:, (g // 2) * LANES:(g // 2 + 1) * LANES]
        other = _swap_halves(blk)
        if g % 2 == 0:
            return jnp.where(lo, blk, 0.0), jnp.where(lo, 0.0, other)
        return jnp.where(lo, other, 0.0), jnp.where(lo, 0.0, blk)

    pairs = N_HEADS // KV_HEADS // 2
    valid_g = jnp.tile(valid, (pairs, 1))

    def probs(s, heads):
        sink = jnp.concatenate(
            [jnp.broadcast_to(jnp.sum(jnp.where(hlane == h, sink_row, 0.0), axis=1, keepdims=True), (CHUNK, 1))
             for h in heads], axis=0)
        s = jnp.where(valid_g, s, NEG)
        m = lax.stop_gradient(jnp.maximum(jnp.max(s, axis=1, keepdims=True), sink))
        e = jnp.exp(s - m)
        return e / (jnp.sum(e, axis=1, keepdims=True) + jnp.exp(sink - m))

    outs = []
    for g in range(KV_HEADS):
        k_lo, k_hi = halves(kcat, g)
        v_lo, v_hi = halves(vcat, g)
        tiles = range(g * pairs, (g + 1) * pairs)
        qg = jnp.concatenate([q[:, j * LANES:(j + 1) * LANES] for j in tiles], axis=0)
        p_a = probs(_bdot_nt(qg, k_lo), [2 * j for j in tiles])
        p_b = probs(_bdot_nt(qg, k_hi), [2 * j + 1 for j in tiles])
        og = _bdot_nn(p_a, v_lo) + _bdot_nn(p_b, v_hi)
        outs += [og[n * CHUNK:(n + 1) * CHUNK] for n in range(pairs)]
    return jnp.concatenate(outs, axis=1) * jax.nn.silu(z)


def _swa_valid(block):
    qi = lax.broadcasted_iota(jnp.int32, (CHUNK, 2 * CHUNK), 0)
    kj = lax.broadcasted_iota(jnp.int32, (CHUNK, 2 * CHUNK), 1)
    rel = qi + CHUNK - kj
    return (rel >= 0) & (rel < CHUNK) & ((kj >= CHUNK) | (block > 0))


def _swa_specs(blk):
    prev = lambda i: jnp.maximum(blk(i) - 1, 0)
    kv_col = D // KV_BLOCK
    return [pl.BlockSpec((CHUNK, SWA_COLS), lambda i: (blk(i), 0)),
            pl.BlockSpec((CHUNK, KV_BLOCK), lambda i: (prev(i), kv_col)),
            pl.BlockSpec((CHUNK, LANES), lambda i: (blk(i), 0)), pl.BlockSpec((CHUNK, LANES), lambda i: (blk(i), 0)),
            pl.BlockSpec((CHUNK, LANES), lambda i: (prev(i), 0)), pl.BlockSpec((CHUNK, LANES), lambda i: (prev(i), 0)),
            _vec_spec(LANES)]


def _swa_fwd(p, cos, sin, sink_row, name, rider=None):
    t = p.shape[0]

    def body(pq_ref, pkv_ref, c_ref, s_ref, cp_ref, sp_ref, sk_ref, o_ref):
        valid = _swa_valid(pl.program_id(0))
        o_ref[...] = _f_swa(pq_ref[...], pkv_ref[...], c_ref[...], s_ref[...], cp_ref[...], sp_ref[...],
                            sk_ref[...], valid).astype(BF16)

    return _compute_call(
        body, (p, p, cos, sin, cos, sin, sink_row), name=name, grid=(t // CHUNK,), in_specs=_swa_specs(lambda i: i),
        out_specs=_row_spec(D, CHUNK), out_shape=jax.ShapeDtypeStruct((t, D), BF16), semantics=("parallel",), rider=rider)


def _swa_bwd(p, cos, sin, sink_row, dout, name, rider=None):
    t = p.shape[0]
    nb = t // CHUNK
    blk = lambda i: nb - 1 - i

    def body(pq_ref, pkv_ref, c_ref, s_ref, cp_ref, sp_ref, sk_ref, do_ref, dp_ref, dsk_ref, pend_ref):
        i = pl.program_id(0)
        valid = _swa_valid(blk(i))
        f = functools.partial(_f_swa, valid=valid)
        _, vjp = jax.vjp(f, pq_ref[...], pkv_ref[...], c_ref[...], s_ref[...], cp_ref[...], sp_ref[...], sk_ref[...])
        dpq, dpkv, _, _, _, _, dsk = vjp(do_ref[...])

        @pl.when(i == 0)
        def _():
            pend_ref[...] = jnp.zeros_like(pend_ref)

        dp_ref[...] = jnp.concatenate(
            [dpq[:, :D], dpq[:, D:D + KV_BLOCK] + pend_ref[...], dpq[:, D + KV_BLOCK:]], axis=1).astype(BF16)
        pend_ref[...] = dpkv
        _accum(dsk_ref, dsk, i == 0)

    return _compute_call(
        body, (p, p, cos, sin, cos, sin, sink_row, dout), name=name, grid=(nb,),
        in_specs=_swa_specs(blk) + [pl.BlockSpec((CHUNK, D), lambda i: (blk(i), 0))],
        out_specs=[pl.BlockSpec((CHUNK, SWA_COLS), lambda i: (blk(i), 0)), _vec_spec(LANES)],
        out_shape=[jax.ShapeDtypeStruct((t, SWA_COLS), BF16), jax.ShapeDtypeStruct((1, LANES), F32)],
        scratch_shapes=[pltpu.VMEM((CHUNK, KV_BLOCK), F32)], semantics=("arbitrary",), rider=rider)


RW_MAIN = 4 * D
RW_LO = 2 * LORA_PAD
VM = LANES // N_HEADS
VD = HEAD // VM
S_ROWS = VD * HEAD
TB_RW = 128
TB_K = 32
TB_SCAN = 32


def _dim_major(a):
    return a.reshape(a.shape[:-1] + (N_HEADS, HEAD)).swapaxes(-1, -2).reshape(a.shape)


def _head_major(a):
    return a.reshape(a.shape[:-1] + (HEAD, N_HEADS)).swapaxes(-1, -2).reshape(a.shape)


def _param_compact(w):
    return _dim_major(w).reshape(VD, LANES)


def _param_compact_inv(pc):
    return _head_major(pc.reshape(-1))


def _f_rwkv_lora(xs_lo, w0, a0, wl, al):
    decay = jnp.exp(-DECAY_SCALE * jax.nn.sigmoid(w0 + _bdot_nn(jnp.tanh(xs_lo[:, :LORA_PAD]), wl)))
    a = jax.nn.sigmoid(a0 + _bdot_nn(xs_lo[:, LORA_PAD:], al))
    return decay, a


def _prev_rows_spec(width, tb):
    return pl.BlockSpec((8, width), lambda i: (jnp.maximum(i * (tb // 8) - 1, 0), 0))


def _token_shift_lerp(p, prev8, mu, first):
    rows = lax.broadcasted_iota(jnp.int32, p.shape, 0)
    prev = jnp.where(first, 0.0, prev8[7:8, :])
    shifted = jnp.where(rows == 0, prev, pltpu.roll(p, 1, 0))
    return p + (shifted - p) * mu


def _store_compact(ref, val):
    for j in range(VD):
        ref[:, j, :] = val[:, j * LANES:(j + 1) * LANES]


def _load_flat(ref, rows=slice(None)):
    if len(ref.shape) == 2:
        return ref[rows, :]
    return jnp.concatenate([ref[rows, j, :] for j in range(VD)], axis=1)


def _flat_spec(a, tb):
    return _row_spec(a.shape[1], tb) if a.ndim == 2 else _k_spec(VD, tb)


def _rwkv_pre_fwd(p_main, p_lo, mu_main, mu_lo, w0, a0, wl, al, name):
    t = p_main.shape[0]
    tb = TB_RW

    def body(pm_ref, pmp_ref, pl_ref, plp_ref, mm_ref, ml_ref, w0_ref, a0_ref, wl_ref, al_ref,
             r_ref, k_ref, v_ref, dec_ref, a_ref, z_ref, xl_ref):
        first = pl.program_id(0) == 0
        xs = _token_shift_lerp(pm_ref[...], pmp_ref[...], mm_ref[...], first)
        for n, ref in enumerate((r_ref, k_ref, v_ref)):
            _store_compact(ref, xs[:, n * D:(n + 1) * D])
        z_ref[...] = xs[:, 3 * D:]
        xs_lo = _token_shift_lerp(pl_ref[...], plp_ref[...], ml_ref[...], first)
        xl_ref[...] = xs_lo
        decay, a = _f_rwkv_lora(xs_lo, w0_ref[...], a0_ref[...], wl_ref[...], al_ref[...])
        _store_compact(dec_ref, decay)
        _store_compact(a_ref, a)

    cl = jax.ShapeDtypeStruct((t, VD, LANES), F32)
    return pl.pallas_call(
        body, name=name, grid=(t // tb,),
        in_specs=[_row_spec(RW_MAIN, tb), _prev_rows_spec(RW_MAIN, tb), _row_spec(RW_LO, tb), _prev_rows_spec(RW_LO, tb),
                  _vec_spec(RW_MAIN), _vec_spec(RW_LO), _vec_spec(D), _vec_spec(D),
                  _vec_spec(D, LORA_PAD), _vec_spec(D, LORA_PAD)],
        out_specs=[_k_spec(VD, tb)] * 5 + [_row_spec(D, tb), _row_spec(RW_LO, tb)],
        out_shape=[cl] * 5 + [jax.ShapeDtypeStruct((t, D), F32), jax.ShapeDtypeStruct((t, RW_LO), F32)],
        compiler_params=_cparams("parallel"),
    )(p_main, p_main, p_lo, p_lo, mu_main, mu_lo, w0, a0, wl, al)


def _rwkv_lora_bwd(xs_lo, w0, a0, wl, al, ddecay, da, name):
    t = xs_lo.shape[0]
    tb = TB_NORM

    def body(x_ref, w0_ref, a0_ref, wl_ref, al_ref, dd_ref, da_ref, dx_ref, dw0_ref, da0_ref, dwl_ref, dal_ref):
        _, vjp = jax.vjp(_f_rwkv_lora, x_ref[...], w0_ref[...], a0_ref[...], wl_ref[...], al_ref[...])
        dx, dw0, da0, dwl, dal = vjp((_load_flat(dd_ref), _load_flat(da_ref)))
        dx_ref[...] = dx
        first = pl.program_id(0) == 0
        _accum(dw0_ref, dw0, first)
        _accum(da0_ref, da0, first)
        _accum(dwl_ref, dwl, first)
        _accum(dal_ref, dal, first)

    vec = jax.ShapeDtypeStruct((1, D), F32)
    lor = jax.ShapeDtypeStruct((LORA_PAD, D), F32)
    return pl.pallas_call(
        body, name=name, grid=(t // tb,),
        in_specs=[_row_spec(RW_LO), _vec_spec(D), _vec_spec(D), _vec_spec(D, LORA_PAD), _vec_spec(D, LORA_PAD),
                  _k_spec(VD, tb), _k_spec(VD, tb)],
        out_specs=[_row_spec(RW_LO), _vec_spec(D), _vec_spec(D), _vec_spec(D, LORA_PAD), _vec_spec(D, LORA_PAD)],
        out_shape=[jax.ShapeDtypeStruct((t, RW_LO), F32), vec, vec, lor, lor], compiler_params=_cparams("arbitrary"),
    )(xs_lo, w0, a0, wl, al, ddecay, da)


def _lerp_bwd(p, dxs_groups, mu, name):
    t, width = p.shape
    tb = TB_RW
    nb = t // tb
    parts = [a for group in dxs_groups for a in group]

    def body(p_ref, pp_ref, mu_ref, *refs):
        d_refs, (dp_ref, dmu_ref) = refs[:2 * len(parts)], refs[2 * len(parts):]
        i = pl.program_id(0)

        def columns(k):
            pick = (lambda r: _load_flat(r, slice(0, 1))) if k else _load_flat
            vals, at = [], 0
            for group in dxs_groups:
                vals.append(functools.reduce(jnp.add, [pick(d_refs[2 * (at + n) + k]) for n in range(len(group))]))
                at += len(group)
            return jnp.concatenate(vals, axis=1)

        pv, dv, mu_v = p_ref[...], columns(0), mu_ref[...]
        rows = lax.broadcasted_iota(jnp.int32, pv.shape, 0)
        prev = jnp.where(i == 0, 0.0, pp_ref[7:8, :])
        shifted = jnp.where(rows == 0, prev, pltpu.roll(pv, 1, 0))
        nxt = jnp.where(i == nb - 1, 0.0, columns(1))
        d_next = jnp.where(rows == tb - 1, nxt, pltpu.roll(dv, tb - 1, 0))
        dp_ref[...] = (dv * (1.0 - mu_v) + d_next * mu_v).astype(BF16)
        _accum(dmu_ref, jnp.sum(dv * (shifted - pv), axis=0, keepdims=True), i == 0)

    d_specs = []
    for a in parts:
        after = lambda i, nd=a.ndim: (jnp.minimum((i + 1) * (tb // 8), t // 8 - 1),) + (0,) * (nd - 1)
        d_specs += [_flat_spec(a, tb), pl.BlockSpec((8,) + a.shape[1:], after)]
    return pl.pallas_call(
        body, name=name, grid=(nb,),
        in_specs=[_row_spec(width, tb), _prev_rows_spec(width, tb), _vec_spec(width)] + d_specs,
        out_specs=[_row_spec(width, tb), _vec_spec(width)],
        out_shape=[jax.ShapeDtypeStruct((t, width), BF16), jax.ShapeDtypeStruct((1, width), F32)],
        compiler_params=_cparams("arbitrary"),
    )(p, p, mu, *[a for a in parts for _ in range(2)])


def _lane_group_sum2d(x):
    x = x + pltpu.roll(x, N_HEADS, 1)
    return x + pltpu.roll(x, 2 * N_HEADS, 1)


@jax.custom_vjp
def _lane_group_sum(x):
    return _lane_group_sum2d(x.reshape(-1, LANES)).reshape(x.shape)


_lane_group_sum.defvjp(lambda x: (_lane_group_sum(x), None), lambda _, ct: (_lane_group_sum(ct),))


def _head_sum(x):
    return _lane_group_sum(jnp.sum(x, axis=1, keepdims=True))


def _f_kprep(k, a, r, kkp, kap, rkp):
    kk = k * kkp
    kk = kk / jnp.maximum(jnp.sqrt(_head_sum(kk * kk)), 1e-12)
    k2 = k * (1.0 + (a - 1.0) * kap)
    return kk, k2, kk * a, _head_sum(r * k2 * rkp)


def _k_spec(rows=HEAD, tb=TB_K):
    return pl.BlockSpec((tb, rows, LANES), lambda i: (i, 0, 0))


def _kparam_spec(rows=HEAD):
    return pl.BlockSpec((rows, LANES), lambda i: (0, 0))


def _lane_group(shape):
    return lax.broadcasted_iota(jnp.int32, shape, len(shape) - 1) // N_HEADS


def _store_k_layout(ref, xc):
    x2 = xc.reshape(-1, LANES)
    group = _lane_group(x2.shape)
    shifted = [x2] + [pltpu.roll(x2, N_HEADS * k, 1) for k in range(1, VM)]
    for q in range(VM):
        out = shifted[0]
        for k in range(1, VM):
            out = jnp.where(group == (q + k) % VM, shifted[k], out)
        ref[:, pl.ds(q, VD, stride=VM), :] = out.reshape(xc.shape)


def _load_compact(ref):
    shape = (ref.shape[0], VD, LANES)
    rows = [ref[:, pl.ds(q, VD, stride=VM), :].reshape(-1, LANES) for q in range(VM)]
    group = _lane_group(rows[0].shape)
    acc = None
    for k in range(VM):
        t = rows[-k % VM]
        for g in range(1, VM):
            t = jnp.where(group == g, rows[(g - k) % VM], t)
        if k:
            t = pltpu.roll(t, LANES - N_HEADS * k, 1)
        acc = t if acc is None else acc + t
    return acc.reshape(shape)


def _rwkv_kprep_fwd(k, a, r, w, kkp, kap, rkp, name):
    t = k.shape[0]

    def body(k_ref, a_ref, r_ref, w_ref, kkp_ref, kap_ref, rkp_ref, kk_ref, k2_ref, b_ref, r4_ref, w4_ref, rk_ref):
        rv = r_ref[...]
        kk, k2, b, rk_ref[...] = _f_kprep(k_ref[...], a_ref[...], rv, kkp_ref[...], kap_ref[...], rkp_ref[...])
        for ref, val in ((kk_ref, kk), (k2_ref, k2), (b_ref, b), (r4_ref, rv), (w4_ref, w_ref[...])):
            _store_k_layout(ref, val)

    big = jax.ShapeDtypeStruct((t, HEAD, LANES), F32)
    return pl.pallas_call(
        body, name=name, grid=(t // TB_K,),
        in_specs=[_k_spec(VD)] * 4 + [_kparam_spec(VD)] * 3, out_specs=[_k_spec()] * 5 + [_k_spec(1)],
        out_shape=[big] * 5 + [jax.ShapeDtypeStruct((t, 1, LANES), F32)], compiler_params=_cparams("parallel"),
    )(k, a, r, w, kkp, kap, rkp)


def _rwkv_kprep_bwd(k, a, r, kkp, kap, rkp, dkk, dk2, db, drk, dr_scan, dw_scan, name):
    t = k.shape[0]

    def body(k_ref, a_ref, r_ref, kkp_ref, kap_ref, rkp_ref, dkk_ref, dk2_ref, db_ref, drk_ref, drs_ref, dws_ref,
             dk_ref, da_ref, dr_ref, dw_ref, dkkp_ref, dkap_ref, drkp_ref):
        _, vjp = jax.vjp(_f_kprep, k_ref[...], a_ref[...], r_ref[...], kkp_ref[...], kap_ref[...], rkp_ref[...])
        dk, da, dr, dkkp, dkap, drkp = vjp((_load_compact(dkk_ref), _load_compact(dk2_ref), _load_compact(db_ref),
                                            drk_ref[...]))
        dk_ref[...] = dk
        da_ref[...] = da
        dr_ref[...] = dr + _load_compact(drs_ref)
        dw_ref[...] = _load_compact(dws_ref)
        first = pl.program_id(0) == 0
        _accum(dkkp_ref, dkkp, first)
        _accum(dkap_ref, dkap, first)
        _accum(drkp_ref, drkp, first)

    cl = jax.ShapeDtypeStruct((t, VD, LANES), F32)
    par = jax.ShapeDtypeStruct((VD, LANES), F32)
    return pl.pallas_call(
        body, name=name, grid=(t // TB_K,),
        in_specs=[_k_spec(VD)] * 3 + [_kparam_spec(VD)] * 3 + [_k_spec()] * 3 + [_k_spec(1), _k_spec(), _k_spec()],
        out_specs=[_k_spec(VD)] * 4 + [_kparam_spec(VD)] * 3,
        out_shape=[cl] * 4 + [par] * 3, compiler_params=_cparams("arbitrary"),
    )(k, a, r, kkp, kap, rkp, dkk, dk2, db, drk, dr_scan, dw_scan)


def _f_post(y, v, rk, g, b):
    mean = _lane_group_sum(jnp.sum(y, axis=1, keepdims=True)) * (1.0 / HEAD)
    yc = y - mean
    var = _lane_group_sum(jnp.sum(yc * yc, axis=1, keepdims=True)) * (1.0 / HEAD)
    return yc * lax.rsqrt(var + GN_EPS) * g + b + rk * v


def _rwkv_post_fwd(y, v, rk, g, b, name):
    t = y.shape[0]

    def body(y_ref, v_ref, rk_ref, g_ref, b_ref, o_ref):
        o_ref[...] = _f_post(y_ref[...], v_ref[...], rk_ref[...], g_ref[...], b_ref[...])

    return pl.pallas_call(
        body, name=name, grid=(t // TB_K,),
        in_specs=[_k_spec(VD), _k_spec(VD), _k_spec(1), _kparam_spec(VD), _kparam_spec(VD)], out_specs=_k_spec(VD),
        out_shape=jax.ShapeDtypeStruct((t, VD, LANES), F32), compiler_params=_cparams("parallel"),
    )(y, v, rk, g, b)


def _rwkv_post_bwd(y, v, rk, g, b, do, name):
    t = y.shape[0]

    def body(y_ref, v_ref, rk_ref, g_ref, b_ref, do_ref, dy_ref, dv_ref, drk_ref, dg_ref, db_ref):
        _, vjp = jax.vjp(_f_post, y_ref[...], v_ref[...], rk_ref[...], g_ref[...], b_ref[...])
        dy, dv, drk, dg, db = vjp(do_ref[...])
        dy_ref[...] = dy
        dv_ref[...] = dv
        drk_ref[...] = drk
        first = pl.program_id(0) == 0
        _accum(dg_ref, dg, first)
        _accum(db_ref, db, first)

    vl = jax.ShapeDtypeStruct((t, VD, LANES), F32)
    par = jax.ShapeDtypeStruct((VD, LANES), F32)
    return pl.pallas_call(
        body, name=name, grid=(t // TB_K,),
        in_specs=[_k_spec(VD), _k_spec(VD), _k_spec(1), _kparam_spec(VD), _kparam_spec(VD), _k_spec(VD)],
        out_specs=[_k_spec(VD), _k_spec(VD), _k_spec(1), _kparam_spec(VD), _kparam_spec(VD)],
        out_shape=[vl, vl, jax.ShapeDtypeStruct((t, 1, LANES), F32), par, par], compiler_params=_cparams("arbitrary"),
    )(y, v, rk, g, b, do)


def _f_gate(o, z):
    return o * jax.nn.silu(z)


def _rwkv_gate_fwd(o, z, name):
    t = z.shape[0]

    def body(o_ref, z_ref, u_ref):
        u_ref[...] = _f_gate(_load_flat(o_ref), z_ref[...]).astype(BF16)

    return pl.pallas_call(
        body, name=name, grid=(t // TB_NORM,), in_specs=[_k_spec(VD, TB_NORM), _row_spec(D)], out_specs=_row_spec(D),
        out_shape=jax.ShapeDtypeStruct((t, D), BF16), compiler_params=_cparams("parallel"),
    )(o, z)


def _rwkv_gate_bwd(o, z, du, name):
    t = z.shape[0]

    def body(o_ref, z_ref, du_ref, do_ref, dz_ref):
        _, vjp = jax.vjp(_f_gate, _load_flat(o_ref), z_ref[...])
        do, dz_ref[...] = vjp(du_ref[...])
        _store_compact(do_ref, do)

    return pl.pallas_call(
        body, name=name, grid=(t // TB_NORM,), in_specs=[_k_spec(VD, TB_NORM), _row_spec(D), _row_spec(D)],
        out_specs=[_k_spec(VD, TB_NORM), _row_spec(D)],
        out_shape=[jax.ShapeDtypeStruct((t, VD, LANES), F32), jax.ShapeDtypeStruct((t, D), F32)],
        compiler_params=_cparams("parallel"),
    )(o, z, du)


def _colsum(x):
    return jnp.sum(x, axis=0, keepdims=True)


def _rwkv_scan_fwd(r4, w4, k24, kk4, b4, v, name, rider=None):
    t = r4.shape[0]
    tb = TB_SCAN

    def body(r_ref, w_ref, k2_ref, kk_ref, b_ref, v_ref, y_ref, sall_ref, sa_ref, s_scr):
        @pl.when(pl.program_id(0) == 0)
        def _():
            s_scr[...] = jnp.zeros_like(s_scr)

        sall_ref[0] = s_scr[...]

        def step(tt, dst):
            kk = kk_ref[tt]
            sas = []
            for vd in range(VD):
                sa = _colsum(sall_ref[tt, pl.ds(vd * HEAD, HEAD), :] * kk)
                sa_ref[tt, pl.ds(vd, 1), :] = sa
                sas.append(sa)
            w, b, k2, r = w_ref[tt], b_ref[tt], k2_ref[tt], r_ref[tt]
            for vd in range(VD):
                rows = pl.ds(vd * HEAD, HEAD)
                s = sall_ref[tt, rows, :] * w - sas[vd] * b + v_ref[tt, pl.ds(vd, 1), :] * k2
                dst[rows, :] = s
                y_ref[tt, pl.ds(vd, 1), :] = _colsum(s * r)

        def loop_step(tt, carry):
            step(tt, sall_ref.at[tt + 1])
            return carry

        lax.fori_loop(0, tb - 1, loop_step, 0)
        step(tb - 1, s_scr)

    vl = jax.ShapeDtypeStruct((t, VD, LANES), F32)
    return _compute_call(
        body, (r4, w4, k24, kk4, b4, v), name=name, grid=(t // tb,),
        in_specs=[_k_spec(HEAD, tb)] * 5 + [_k_spec(VD, tb)],
        out_specs=[_k_spec(VD, tb), _k_spec(S_ROWS, tb), _k_spec(VD, tb)],
        out_shape=[vl, jax.ShapeDtypeStruct((t, S_ROWS, LANES), F32), vl],
        scratch_shapes=[pltpu.VMEM((S_ROWS, LANES), F32)], semantics=("arbitrary",), rider=rider)


def _rwkv_scan_bwd(dy, s_all, sa_all, r4, w4, k24, kk4, b4, v, name, rider=None):
    t = r4.shape[0]
    tb = TB_SCAN
    nb = t // tb
    blk = lambda i: nb - 1 - i

    def body(dy_ref, sall_ref, sa_ref, r_ref, w_ref, k2_ref, kk_ref, b_ref, v_ref,
             dr_ref, dw_ref, dk2_ref, dkk_ref, db_ref, dv_ref, ds_scr):
        @pl.when(pl.program_id(0) == 0)
        def _():
            ds_scr[...] = jnp.zeros_like(ds_scr)

        def step(j, carry):
            tt = tb - 1 - j
            vrow = lambda ref, vd: ref[tt, pl.ds(vd, 1), :]
            srows = lambda vd: pl.ds(vd * HEAD, HEAD)
            r, k2, b = r_ref[tt], k2_ref[tt], b_ref[tt]
            dsas = []
            for vd in range(VD):
                ds = ds_scr[srows(vd), :] + vrow(dy_ref, vd) * r
                ds_scr[srows(vd), :] = ds
                dv_ref[tt, pl.ds(vd, 1), :] = _colsum(ds * k2)
                dsas.append(-_colsum(ds * b))
            zero = jnp.zeros((HEAD, LANES), F32)
            dk2, q, sady, vdy = zero, zero, 0.0, 0.0
            for vd in range(VD):
                dyv = vrow(dy_ref, vd)
                dk2 = dk2 + ds_scr[srows(vd), :] * vrow(v_ref, vd)
                q = q + sall_ref[tt, srows(vd), :] * dyv
                sady = sady + vrow(sa_ref, vd) * dyv
                vdy = vdy + vrow(v_ref, vd) * dyv
            dk2_ref[tt] = dk2
            dr_ref[tt] = w_ref[tt] * q - b_ref[tt] * sady + k2_ref[tt] * vdy
            dw, dkk = zero, zero
            for vd in range(VD):
                sp = sall_ref[tt, srows(vd), :]
                dw = dw + ds_scr[srows(vd), :] * sp
                dkk = dkk + sp * dsas[vd]
            dw_ref[tt] = dw
            dkk_ref[tt] = dkk
            w, kk = w_ref[tt], kk_ref[tt]
            db = zero
            for vd in range(VD):
                ds = ds_scr[srows(vd), :]
                db = db - ds * vrow(sa_ref, vd)
                ds_scr[srows(vd), :] = ds * w + dsas[vd] * kk
            db_ref[tt] = db
            return carry

        lax.fori_loop(0, tb, step, 0)

    rk = lambda rows: pl.BlockSpec((tb, rows, LANES), lambda i: (blk(i), 0, 0))
    big = jax.ShapeDtypeStruct((t, HEAD, LANES), F32)
    return _compute_call(
        body, (dy, s_all, sa_all, r4, w4, k24, kk4, b4, v), name=name, grid=(nb,),
        in_specs=[rk(VD), rk(S_ROWS), rk(VD)] + [rk(HEAD)] * 5 + [rk(VD)],
        out_specs=[rk(HEAD)] * 5 + [rk(VD)],
        out_shape=[big] * 5 + [jax.ShapeDtypeStruct((t, VD, LANES), F32)],
        scratch_shapes=[pltpu.VMEM((S_ROWS, LANES), F32)], semantics=("arbitrary",), rider=rider)


def _rwkv_mixer_fwd(p_main, p_lo, prm, tag, rider):
    r, k, v, w, a, z, xs_lo = _rwkv_pre_fwd(p_main, p_lo, prm["mu_main"], prm["mu_lo"], prm["w0"], prm["a0"],
                                            prm["wl"], prm["al"], tag + "_pre")
    kk4, k24, b4, r4, w4, rk = _rwkv_kprep_fwd(k, a, r, w, prm["kkp"], prm["kap"], prm["rkp"], tag + "_kprep")
    (y, s_all, sa_all), ridden = _ridden(_rwkv_scan_fwd(r4, w4, k24, kk4, b4, v, tag + "_scan", rider), rider)
    o = _rwkv_post_fwd(y, v, rk, prm["gn_g"], prm["gn_b"], tag + "_post")
    u = _rwkv_gate_fwd(o, z, tag + "_gate")
    saved = dict(z=z, xs_lo=xs_lo, r=r, k=k, a=a, v=v, r4=r4, w4=w4, kk4=kk4, k24=k24, b4=b4, rk=rk,
                 y=y, s_all=s_all, sa_all=sa_all, o=o)
    return u, saved, ridden


def _rwkv_mixer_bwd(p_main, p_lo, prm, sv, du, tag, rider):
    do, dz = _rwkv_gate_bwd(sv["o"], sv["z"], du, tag + "_gate_b")
    dy, dv_post, drk, dgn_g, dgn_b = _rwkv_post_bwd(sv["y"], sv["v"], sv["rk"], prm["gn_g"], prm["gn_b"], do,
                                                    tag + "_post_b")
    (dr_s, dw_s, dk24, dkk4, db4, dv_scan), ridden = _ridden(_rwkv_scan_bwd(
        dy, sv["s_all"], sv["sa_all"], sv["r4"], sv["w4"], sv["k24"], sv["kk4"], sv["b4"], sv["v"], tag + "_scan_b", rider), rider)
    dk, da, dr, dw, dkkp, dkap, drkp = _rwkv_kprep_bwd(sv["k"], sv["a"], sv["r"], prm["kkp"], prm["kap"], prm["rkp"],
                                                       dkk4, dk24, db4, drk, dr_s, dw_s, tag + "_kprep_b")
    dxs_lo, dw0, da0, dwl, dal = _rwkv_lora_bwd(sv["xs_lo"], prm["w0"], prm["a0"], prm["wl"], prm["al"], dw, da,
                                                tag + "_lora_b")
    dxs_main = [[dr], [dk], [dv_post, dv_scan], [dz]]
    dp_main, dmu_main = _lerp_bwd(p_main, dxs_main, prm["mu_main"], tag + "_lerp_main_b")
    dp_lo, dmu_lo = _lerp_bwd(p_lo, [[dxs_lo]], prm["mu_lo"], tag + "_lerp_lo_b")
    grads = dict(mu_main=dmu_main, mu_lo=dmu_lo, w0=dw0, a0=da0, wl=dwl, al=dal, kkp=dkkp, kap=dkap, rkp=drkp,
                 gn_g=dgn_g, gn_b=dgn_b)
    return dp_main, dp_lo, grads, ridden


N_DEV = 8
N_CHIPS = 4
ANY = pl.BlockSpec(memory_space=pl.ANY)


def _place():
    return lax.axis_index("x"), lax.axis_index("y"), lax.axis_index("c")


def _remote(src, dst, send_sems, recv_sems, k, dev):
    return pltpu.make_async_remote_copy(src_ref=src, dst_ref=dst, send_sem=send_sems.at[k], recv_sem=recv_sems.at[k],
                                        device_id=dev, device_id_type=MESHT)


def _all_gather8(v, name):
    def body(buf_ref, out_ref, send_sems, recv_sems):
        del buf_ref
        x, y, c = _place()
        mine = out_ref.at[4 * x + 2 * y + c]
        peers = [(x ^ (k >> 2), y ^ ((k >> 1) & 1), c ^ (k & 1)) for k in range(1, N_DEV)]
        sends = [_remote(mine, mine, send_sems, recv_sems, k, peer) for k, peer in enumerate(peers)]
        for cp in sends:
            cp.start()
        for k, (px, py, pc) in enumerate(peers):
            _remote(mine, out_ref.at[4 * px + 2 * py + pc], send_sems, recv_sems, k, (x, y, c)).wait_recv()
        for cp in sends:
            cp.wait_send()

    return pl.pallas_call(
        body, name=name, in_specs=[ANY], out_specs=ANY, input_output_aliases={0: 0},
        out_shape=jax.ShapeDtypeStruct((N_DEV,) + v.shape, v.dtype),
        scratch_shapes=[pltpu.SemaphoreType.DMA((N_DEV - 1,)), pltpu.SemaphoreType.DMA((N_DEV - 1,))],
    )(jnp.broadcast_to(v[None], (N_DEV,) + v.shape))


def _other_chips(x, y):
    return [(1 - x, y), (x, 1 - y), (1 - x, 1 - y)]


GATHER_SEMS = 6


def _gather_buffer(v):
    return jnp.broadcast_to(v[None], (N_CHIPS,) + v.shape)


def _gather_start(bufs, send_sems, recv_sems):
    x, y, c = _place()
    for i, buf in enumerate(bufs):
        mine = buf.at[2 * x + y, c]
        for j, (cx, cy) in enumerate(_other_chips(x, y)):
            _remote(mine, mine, send_sems, recv_sems, GATHER_SEMS * i + j, (cx, cy, c)).start()


def _gather_finish(bufs, send_sems, recv_sems):
    x, y, c = _place()
    chips = _other_chips(x, y)
    passed = []
    for i, buf in enumerate(bufs):
        mine = buf.at[2 * x + y, c]
        for j, (cx, cy) in enumerate(chips):
            landed = buf.at[2 * cx + cy, c]
            _remote(mine, landed, send_sems, recv_sems, GATHER_SEMS * i + j, (x, y, c)).wait_recv()
            fwd = _remote(landed, landed, send_sems, recv_sems, GATHER_SEMS * i + 3 + j, (x, y, 1 - c))
            fwd.start()
            passed.append(fwd)
    for i, buf in enumerate(bufs):
        mine = buf.at[2 * x + y, c]
        for j, (cx, cy) in enumerate(chips):
            _remote(mine, buf.at[2 * cx + cy, 1 - c], send_sems, recv_sems, GATHER_SEMS * i + 3 + j, (x, y, c)).wait_recv()
            _remote(mine, mine, send_sems, recv_sems, GATHER_SEMS * i + j, (cx, cy, c)).wait_send()
    for fwd in passed:
        fwd.wait_send()


def _gather_rider(bufs):
    return _Rider(bufs, GATHER_SEMS * len(bufs), _gather_start, _gather_finish)


def _chip_gather(bufs, name):
    n = len(bufs)

    def body(*refs):
        out_refs, (send_sems, recv_sems) = refs[n:2 * n], refs[2 * n:]
        _gather_start(out_refs, send_sems, recv_sems)
        _gather_finish(out_refs, send_sems, recv_sems)

    return pl.pallas_call(
        body, name=name, in_specs=[ANY] * n, out_specs=[ANY] * n, input_output_aliases={i: i for i in range(n)},
        out_shape=[jax.ShapeDtypeStruct(b.shape, b.dtype) for b in bufs], scratch_shapes=_dma_sems(GATHER_SEMS * n),
    )(*bufs)


RS_W = 1024
RS_BLOCK_BYTES = 4 << 20


def _dma_sems(n):
    return [pltpu.SemaphoreType.DMA((n,)), pltpu.SemaphoreType.DMA((n,))]


def _pair_exchange_copies(refs, send_sems, recv_sems):
    n = len(refs) // 2
    x, y, c = _place()
    return [_remote(refs[i].at[s, 1 - c], refs[n + i].at[s], send_sems, recv_sems, N_CHIPS * i + s, (x, y, 1 - c))
            for i in range(n) for s in range(N_CHIPS)]


def _pair_exchange_start(refs, send_sems, recv_sems):
    for cp in _pair_exchange_copies(refs, send_sems, recv_sems):
        cp.start()


def _pair_exchange_finish(refs, send_sems, recv_sems):
    for cp in _pair_exchange_copies(refs, send_sems, recv_sems):
        cp.wait()


def _pair_exchange_rider(gs):
    landing = [lax.empty((N_CHIPS,) + g.shape[2:], g.dtype) for g in gs]
    return _Rider(list(gs) + landing, N_CHIPS * len(gs), _pair_exchange_start, _pair_exchange_finish)


def _rs_rows(rows, cols):
    cap = max(16, RS_BLOCK_BYTES // (N_CHIPS * 4 * cols))
    return rows if rows <= cap else max(d for d in range(16, cap + 1, 16) if rows % d == 0)


def _rs_pair_add(g, got, c_arr, name):
    _, _, rows, width = g.shape
    tr = _rs_rows(rows, width)

    def body(c_ref, g_ref, got_ref, p_ref):
        p_ref[...] = (g_ref[...] + got_ref[...]).astype(BF16)

    return pl.pallas_call(
        body, name=name,
        grid_spec=pltpu.PrefetchScalarGridSpec(
            num_scalar_prefetch=1, grid=(rows // tr,),
            in_specs=[pl.BlockSpec((N_CHIPS, None, tr, width), lambda i, c_ref: (0, c_ref[0], i, 0)),
                      pl.BlockSpec((N_CHIPS, tr, width), lambda i, c_ref: (0, i, 0))],
            out_specs=pl.BlockSpec((N_CHIPS, tr, width), lambda i, c_ref: (0, i, 0))),
        out_shape=jax.ShapeDtypeStruct((N_CHIPS, rows, width), BF16), compiler_params=_cparams("parallel"),
    )(c_arr, g, got)


def _chip_exchange_copies(refs, send_sems, recv_sems):
    n = len(refs) // 2
    x, y, c = _place()
    return [_remote(refs[i].at[2 * cx + cy], refs[n + i].at[j], send_sems, recv_sems, 3 * i + j, (cx, cy, c))
            for i in range(n) for j, (cx, cy) in enumerate(_other_chips(x, y))]


def _chip_exchange_start(refs, send_sems, recv_sems):
    for cp in _chip_exchange_copies(refs, send_sems, recv_sems):
        cp.start()


def _chip_exchange_finish(refs, send_sems, recv_sems):
    n = len(refs) // 2
    x, y, c = _place()
    for i in range(n):
        for j in range(3):
            _remote(refs[i].at[2 * x + y], refs[n + i].at[j], send_sems, recv_sems, 3 * i + j, (x, y, c)).wait_recv()
    for cp in _chip_exchange_copies(refs, send_sems, recv_sems):
        cp.wait_send()


def _chip_exchange_buffers(ps):
    return [lax.empty((3,) + p.shape[1:], p.dtype) for p in ps]


def _chip_exchange_rider(ps):
    return _Rider(list(ps) + _chip_exchange_buffers(ps), 3 * len(ps), _chip_exchange_start, _chip_exchange_finish)


def _rs_chip_add(p, q, idx, name):
    _, rows, width = q.shape
    tr = _rs_rows(rows, width)

    def body(idx_ref, p_ref, q_ref, r_ref):
        qv = q_ref[...].astype(F32)
        r_ref[...] = ((p_ref[...].astype(F32) + qv[0]) + qv[1]) + qv[2]

    return pl.pallas_call(
        body, name=name,
        grid_spec=pltpu.PrefetchScalarGridSpec(
            num_scalar_prefetch=1, grid=(rows // tr,),
            in_specs=[pl.BlockSpec((None, tr, width), lambda i, idx_ref: (idx_ref[0], i, 0)),
                      pl.BlockSpec((3, tr, width), lambda i, idx_ref: (0, i, 0))],
            out_specs=pl.BlockSpec((None, tr, width), lambda i, idx_ref: (idx_ref[1], i, 0))),
        out_shape=jax.ShapeDtypeStruct((2, rows, width), F32), compiler_params=_cparams("parallel"),
    )(idx, p, q)


def _rs_pair_share(rs, name):
    n = len(rs)

    def body(*refs):
        out_refs, (send_sems, recv_sems) = refs[n:2 * n], refs[2 * n:]
        x, y, c = _place()
        sends = [_remote(out_refs[i].at[c], out_refs[i].at[c], send_sems, recv_sems, i, (x, y, 1 - c)) for i in range(n)]
        for cp in sends:
            cp.start()
        for i in range(n):
            _remote(out_refs[i].at[c], out_refs[i].at[1 - c], send_sems, recv_sems, i, (x, y, c)).wait_recv()
        for cp in sends:
            cp.wait_send()

    return pl.pallas_call(
        body, name=name, in_specs=[ANY] * n, out_specs=[ANY] * n, input_output_aliases={i: i for i in range(n)},
        out_shape=[jax.ShapeDtypeStruct(r.shape, r.dtype) for r in rs], scratch_shapes=_dma_sems(n),
    )(*rs)


def _rs_pair_sums(gs, gots, core, tag):
    c_arr = core.astype(jnp.int32).reshape(1)
    return [_rs_pair_add(g, got, c_arr, f"{tag}_pair_add{i}") for i, (g, got) in enumerate(zip(gs, gots))]


def _rs_finish(ps, qs, chip, core, tag):
    idx = jnp.stack([chip, core]).astype(jnp.int32)
    rs = [_rs_chip_add(p, q, idx, f"{tag}_chip_add{i}") for i, (p, q) in enumerate(zip(ps, qs))]
    return _rs_pair_share(rs, tag + "_share")


def _sum_leading(a, name):
    n, rows, width = a.shape
    cap = max(8, RS_BLOCK_BYTES // (n * 4 * width))
    tr = rows if rows <= cap else max(d for d in range(8, cap + 1, 8) if rows % d == 0)

    def body(a_ref, o_ref):
        acc = a_ref[0]
        for d in range(1, n):
            acc = acc + a_ref[d]
        o_ref[...] = acc

    return pl.pallas_call(
        body, name=name, grid=(rows // tr,), in_specs=[pl.BlockSpec((n, tr, width), lambda i: (0, i, 0))],
        out_specs=pl.BlockSpec((tr, width), lambda i: (i, 0)), out_shape=jax.ShapeDtypeStruct((rows, width), F32),
        compiler_params=_cparams("parallel"),
    )(a)


def _pair_swap(v, name):
    def body(v_ref, got_ref, send_sems, recv_sems):
        x, y, c = _place()
        cp = _remote(v_ref, got_ref, send_sems, recv_sems, 0, (x, y, 1 - c))
        cp.start()
        cp.wait()

    return pl.pallas_call(body, name=name, in_specs=[ANY], out_specs=ANY, out_shape=jax.ShapeDtypeStruct(v.shape, v.dtype),
                          scratch_shapes=_dma_sems(1))(v)


def _replicated_pair_sum(v, name):
    rows, width = v.shape
    pair = _sum_leading(jnp.stack([v, _pair_swap(v, name + "_swap")]), name + "_pair_add")
    return _gather_buffer(pair.reshape(2, rows // 2, width))


def _replicated_chip_sum(gathered, name):
    return _sum_leading(gathered.reshape(N_CHIPS, -1, gathered.shape[-1]), name + "_chip_add")


MOD_COLS = 3 * D // N_CHIPS
MOD_TK = 512


def _mod_partial(c_all, mod_w, name):
    nk = D // MOD_TK

    def body(c_ref, w_ref, o_ref):
        l = pl.program_id(1)
        part = _bdot_nn(jax.nn.silu(c_ref[...]), w_ref[0])
        _accum(o_ref.at[0], part, l == 0)

    return pl.pallas_call(
        body, name=name, grid=(DEPTH, nk),
        in_specs=[pl.BlockSpec((N_DEV, MOD_TK), lambda i, l: (0, l)), pl.BlockSpec((1, MOD_TK, MOD_COLS), lambda i, l: (i, l, 0))],
        out_specs=pl.BlockSpec((1, N_DEV, MOD_COLS), lambda i, l: (i, 0, 0)),
        out_shape=jax.ShapeDtypeStruct((DEPTH, N_DEV, MOD_COLS), F32), compiler_params=_cparams("parallel", "arbitrary"),
    )(c_all, mod_w)


def _mod_w_grad(c_all, dmod, name):
    def body(c_ref, d_ref, o_ref):
        o_ref[0] = _dg(jax.nn.silu(c_ref[...]).astype(BF16), d_ref[0].astype(BF16), _TN)

    return pl.pallas_call(
        body, name=name, grid=(DEPTH, D // MOD_TK),
        in_specs=[pl.BlockSpec((N_DEV, MOD_TK), lambda i, l: (0, l)), pl.BlockSpec((1, N_DEV, MOD_COLS), lambda i, l: (i, 0, 0))],
        out_specs=pl.BlockSpec((1, MOD_TK, MOD_COLS), lambda i, l: (i, l, 0)),
        out_shape=jax.ShapeDtypeStruct((DEPTH, D, MOD_COLS), F32), compiler_params=_cparams("parallel", "parallel"),
    )(c_all, dmod)


ADAM_BLOCK_BYTES = 1 << 20


def _adamw(w, g, m, v, name, rider=None):
    shape = w.shape
    cols = shape[-1]
    rows = w.size // cols
    w, g, m, v = (a.reshape(rows, cols) for a in (w, g, m, v))
    cap = max(8, ADAM_BLOCK_BYTES // (4 * cols))
    tr = rows if rows <= cap else max(d for d in range(8, cap + 1, 8) if rows % d == 0)
    c1 = 1.0 - ADAM_B1 ** ADAM_STEP
    c2 = 1.0 - ADAM_B2 ** ADAM_STEP

    def body(w_ref, g_ref, m_ref, v_ref, d_ref, nm_ref, nv_ref):
        gv = g_ref[...]
        mn = ADAM_B1 * m_ref[...] + (1.0 - ADAM_B1) * gv
        vn = ADAM_B2 * v_ref[...] + (1.0 - ADAM_B2) * (gv * gv)
        nm_ref[...] = mn
        nv_ref[...] = vn
        d_ref[...] = -ADAM_LR * ((mn / c1) / (jnp.sqrt(vn / c2) + ADAM_EPS) + ADAM_WD * w_ref[...])

    spec = pl.BlockSpec((tr, cols), lambda i: (i, 0))
    out = jax.ShapeDtypeStruct((rows, cols), F32)
    (d, nm, nv), ridden = _ridden(_compute_call(
        body, (w, g, m, v), name=name, grid=(rows // tr,), in_specs=[spec] * 4, out_specs=[spec] * 3, out_shape=[out] * 3,
        semantics=("parallel",), rider=rider), rider)
    res = (d.reshape(shape), nm.reshape(shape), nv.reshape(shape))
    return res if rider is None else (res, ridden)


W_NAMES = ("norm_g", "mod_w", "mod_b", "final_norm_g", "sg_w_in", "sg_w_out", "sg_ln_g", "sg_ln_b", "sg_w_spatial",
           "sg_b_spatial", "swa_w_in", "swa_w_out", "swa_sinks", "rwkv_w_in", "rwkv_w_out", "rwkv_mu", "rwkv_w0",
           "rwkv_w_lora", "rwkv_a0", "rwkv_a_lora", "rwkv_k_k", "rwkv_k_a", "rwkv_r_k", "rwkv_gn_g", "rwkv_gn_b")
SMALL = {"sg_ln_g": 1, "sg_ln_b": 1, "rwkv_mu": 1, "rwkv_w0": 1, "rwkv_w_lora": 2, "rwkv_a0": 1, "rwkv_a_lora": 2,
         "rwkv_k_k": 1, "rwkv_k_a": 1, "rwkv_gn_g": 1, "rwkv_gn_b": 1}
REPLICATED = ("norm_g", "final_norm_g", "sg_w_spatial", "sg_b_spatial", "swa_sinks", "rwkv_r_k")
KINDS = ("sg", "swa", "rwkv", "sg")


def _pad_to(flat, n):
    return jnp.pad(flat, (0, n - flat.shape[0]))


def _round_up(n, m):
    return -(-n // m) * m


def _join_shards(gathered, axis):
    return jnp.concatenate([gathered[s] for s in range(N_CHIPS)], axis=axis)


def _chip_blocks(full, axis):
    return jnp.stack(jnp.split(full, N_CHIPS, axis=axis)).reshape(N_CHIPS, -1)


def _weight_buffer(w):
    rows, cols = w.shape
    return _gather_buffer(w.astype(BF16).reshape(2, rows // 2, cols))


def _chip_shards(buf):
    return buf.reshape(N_CHIPS, -1, buf.shape[-1])


def _small_buffer(shards):
    flat = jnp.concatenate([shards[n].reshape(-1) for n in SMALL])
    rows = _round_up(flat.shape[0], 2 * 8 * LANES) // (2 * LANES)
    return _gather_buffer(_pad_to(flat, 2 * rows * LANES).reshape(2, rows, LANES))


def _unpack_small(buf, shards):
    got = buf.reshape(N_CHIPS, -1)
    out, off = {}, 0
    for n, axis in SMALL.items():
        size = shards[n].size
        out[n] = _join_shards(got[:, off:off + size].reshape((N_CHIPS,) + shards[n].shape), axis)
        off += size
    return out


def _lora_pad_rows(w):
    return jnp.pad(w, ((0, LORA_PAD - LORA), (0, 0)))


def _lo_cols(a):
    z = jnp.zeros(a.shape[:-1] + (LORA_PAD - LORA,), a.dtype)
    return jnp.concatenate([a[..., :LORA], z, a[..., LORA:], z], axis=-1)


def _lo_cols_inv(a):
    return jnp.concatenate([a[..., :LORA], a[..., LORA_PAD:LORA_PAD + LORA]], axis=-1)


def _transposed(w_blocks):
    return jnp.swapaxes(w_blocks, 1, 2).reshape(-1, w_blocks.shape[1])


def _rows_dim_major(w):
    return w.reshape(N_HEADS, HEAD, -1).swapaxes(0, 1).reshape(w.shape)


def _rows_head_major(w):
    return w.reshape(HEAD, N_HEADS, -1).swapaxes(0, 1).reshape(w.shape)


def kernel(x, c, positions, norm_g, mod_w, mod_b, final_norm_g, sg_w_in, sg_w_out, sg_ln_g, sg_ln_b, sg_w_spatial,
           sg_b_spatial, swa_w_in, swa_w_out, swa_sinks, rwkv_w_in, rwkv_w_out, rwkv_mu, rwkv_w0, rwkv_w_lora, rwkv_a0,
           rwkv_a_lora, rwkv_k_k, rwkv_k_a, rwkv_r_k, rwkv_gn_g, rwkv_gn_b, loss_target, m_norm_g, m_mod_w, m_mod_b,
           m_final_norm_g, m_sg_w_in, m_sg_w_out, m_sg_ln_g, m_sg_ln_b, m_sg_w_spatial, m_sg_b_spatial, m_swa_w_in,
           m_swa_w_out, m_swa_sinks, m_rwkv_w_in, m_rwkv_w_out, m_rwkv_mu, m_rwkv_w0, m_rwkv_w_lora, m_rwkv_a0,
           m_rwkv_a_lora, m_rwkv_k_k, m_rwkv_k_a, m_rwkv_r_k, m_rwkv_gn_g, m_rwkv_gn_b, v_norm_g, v_mod_w, v_mod_b,
           v_final_norm_g, v_sg_w_in, v_sg_w_out, v_sg_ln_g, v_sg_ln_b, v_sg_w_spatial, v_sg_b_spatial, v_swa_w_in,
           v_swa_w_out, v_swa_sinks, v_rwkv_w_in, v_rwkv_w_out, v_rwkv_mu, v_rwkv_w0, v_rwkv_w_lora, v_rwkv_a0,
           v_rwkv_a_lora, v_rwkv_k_k, v_rwkv_k_a, v_rwkv_r_k, v_rwkv_gn_g, v_rwkv_gn_b):
    given = dict(locals())
    w = {n: given[n] for n in W_NAMES}
    xi, yi, ci = _place()
    chip = 2 * xi + yi
    me = 4 * xi + 2 * yi + ci
    xs = [x[0]]

    c_all = _all_gather8(c, "gather_c")[:, 0, :]
    mod_part = _mod_partial(c_all, mod_w, "mod_fwd")
    mod_all = _all_gather8(mod_part, "gather_mod")[::2]
    mod_mine = lax.dynamic_index_in_dim(mod_all, me, axis=2, keepdims=False)
    mod = mod_mine.transpose(1, 0, 2).reshape(DEPTH, 3 * D) + mod_b
    shift, scale, gate = mod[:, :D], mod[:, D:2 * D], mod[:, 2 * D:]

    shards = {"sg_w_in0": sg_w_in[0], "sg_w_out0": sg_w_out[0], "swa_w_in": swa_w_in[0], "swa_w_out": swa_w_out[0],
              "rwkv_w_in_a": rwkv_w_in[0, :D // 2], "rwkv_w_in_b": rwkv_w_in[0, D // 2:], "rwkv_w_out": rwkv_w_out[0],
              "sg_w_in1": sg_w_in[1], "sg_w_out1": sg_w_out[1]}
    bufs = {n: _weight_buffer(s) for n, s in shards.items()}
    fwd_riders = {(0, "in"): ["swa_w_in"], (0, "mix"): ["swa_w_out"], (1, "in"): ["rwkv_w_in_a"], (1, "mix"): ["rwkv_w_in_b"],
                  (2, "in"): ["rwkv_w_out"], (2, "mix"): ["sg_w_in1", "sg_w_out1"]}
    bufs["sg_w_in0"], bufs["sg_w_out0"], small_buf = _chip_gather(
        [bufs["sg_w_in0"], bufs["sg_w_out0"], _small_buffer(w)], "gather_l0")
    full = _unpack_small(small_buf, w)

    def riding(i, where):
        names = fwd_riders.get((i, where))
        return names, (None if names is None else _gather_rider([bufs[n] for n in names]))

    def arrived(names, ridden):
        for n, b in zip(names or [], ridden):
            bufs[n] = b

    sg_in = lambda j: _chip_shards(bufs[f"sg_w_in{j}"])
    sg_out = lambda j: bufs[f"sg_w_out{j}"].reshape(D, D)
    mu = full["rwkv_mu"][0]
    rw_prm = dict(mu_main=_dim_major(mu[:RW_MAIN].reshape(4, D)).reshape(1, RW_MAIN), mu_lo=_lo_cols(mu[None, RW_MAIN:]),
                  w0=_dim_major(full["rwkv_w0"]), a0=_dim_major(full["rwkv_a0"]),
                  wl=_lora_pad_rows(_dim_major(full["rwkv_w_lora"][0])), al=_lora_pad_rows(_dim_major(full["rwkv_a_lora"][0])),
                  kkp=_param_compact(full["rwkv_k_k"][0]), kap=_param_compact(full["rwkv_k_a"][0]),
                  rkp=_param_compact(rwkv_r_k.reshape(-1)),
                  gn_g=_param_compact(full["rwkv_gn_g"][0]), gn_b=_param_compact(full["rwkv_gn_b"][0]))
    bs_t = [jnp.pad(sg_b_spatial[j].T, ((0, 0), (0, LANES - SG_GROUPS))) for j in range(2)]
    sink_row = jnp.pad(swa_sinks, ((0, 0), (0, LANES - N_HEADS)))
    inv_freq = ROPE_THETA ** (-jnp.arange(HEAD // 2, dtype=F32) / (HEAD // 2))
    ang = positions[0].astype(F32)[:, None] * inv_freq
    cos, sin = jnp.tile(jnp.cos(ang), (1, LANES * 2 // HEAD)), jnp.tile(jnp.sin(ang), (1, LANES * 2 // HEAD))

    def row(a, i):
        return a[i:i + 1]

    hs, ps, us, ys, rw_saved = [], [], [], [], None
    for i, kind in enumerate(KINDS):
        j = i // 3
        tag = f"l{i}_{kind}"
        h = _norm_mod_fwd(xs[i], row(norm_g, i), row(shift, i), row(scale, i), tag + "_norm")
        names_in, rider_in = riding(i, "in")
        names_mix, rider_mix = riding(i, "mix")
        if kind == "sg":
            p, ridden = _ridden(_matmul(h, sg_in(j), "nn", tag + "_in", blocked=True, rider=rider_in), rider_in)
            arrived(names_in, ridden)
            u, ridden = _ridden(_sg_fwd(p, row(full["sg_ln_g"], j), row(full["sg_ln_b"], j), sg_w_spatial[j], bs_t[j],
                                        tag + "_mix", rider_mix), rider_mix)
            w_out = sg_out(j)
        elif kind == "swa":
            swa_in, swa_out = _chip_shards(bufs["swa_w_in"]), bufs["swa_w_out"].reshape(D, D)
            p, ridden = _ridden(_matmul(h, swa_in, "nn", tag + "_in", blocked=True, rider=rider_in), rider_in)
            arrived(names_in, ridden)
            u, ridden = _ridden(_swa_fwd(p, cos, sin, sink_row, tag + "_mix", rider_mix), rider_mix)
            w_out = swa_out
        else:
            rw_in = jnp.concatenate([_join_shards(_chip_shards(bufs[n]), axis=1) for n in ("rwkv_w_in_a", "rwkv_w_in_b")])
            rw_main = _dim_major(rw_in[:, :RW_MAIN].reshape(D, 4, D)).reshape(D, RW_MAIN)
            rw_lo = _lo_cols(rw_in[:, RW_MAIN:])
            p_main, ridden = _ridden(_matmul(h, rw_main, "nn", tag + "_in", rider=rider_in), rider_in)
            arrived(names_in, ridden)
            rw_out = _rows_dim_major(bufs["rwkv_w_out"].reshape(D, D))
            p = (p_main, _matmul(h, rw_lo, "nn", tag + "_in_lo"))
            u, rw_saved, ridden = _rwkv_mixer_fwd(p[0], p[1], rw_prm, tag, rider_mix)
            w_out = rw_out
        arrived(names_mix, ridden)
        y, x_next = _out_proj_resid(u, w_out, xs[i], row(gate, i), tag + "_out")
        xs.append(x_next)
        hs.append(h), ps.append(p), us.append(u), ys.append(y)

    loss_part, dx, d_final_g, dy, d_gate = _final_loss_grad(xs[DEPTH], final_norm_g[None], loss_target[0], ys[DEPTH - 1],
                                                            row(gate, DEPTH - 1), "loss")
    loss = lax.psum(loss_part[0, 0], ("x", "y", "c"))

    gfull = {n: [None, None] for n in ("sg_ln_g", "sg_ln_b", "sg_w_spatial", "sg_b_spatial")}
    gbig = {}
    d_norm_g, d_mod = [None] * DEPTH, [None] * DEPTH
    rs_p, rs_q, riding_names = {}, {}, []

    for i in reversed(range(DEPTH)):
        kind, j = KINDS[i], i // 3
        tag = f"l{i}_{kind}_b"
        rider = _chip_exchange_rider([rs_p[n] for n in riding_names]) if riding_names else None
        w_out = {"sg": sg_out(j), "swa": swa_out, "rwkv": rw_out}[kind]
        du = _matmul(dy, w_out, "nt", tag + "_du")
        dw_out = _matmul(us[i], dy, "tn", tag + "_dwout").reshape(N_CHIPS, D // N_CHIPS, D)
        if kind == "sg":
            (dp, dlg, dlb, dws, dbs), ridden = _ridden(
                _sg_bwd(ps[i], row(full["sg_ln_g"], j), row(full["sg_ln_b"], j), sg_w_spatial[j], bs_t[j], du,
                        tag + "_mix", rider), rider)
            gfull["sg_ln_g"][j], gfull["sg_ln_b"][j] = dlg[0], dlb[0]
            gfull["sg_w_spatial"][j], gfull["sg_b_spatial"][j] = dws, dbs[:, :SG_GROUPS].T
            gbig[f"sg_w_in{j}"] = _matmul(hs[i], dp, "tn", tag + "_dwin", blocked=True)
            gbig[f"sg_w_out{j}"] = dw_out
            dh, dh2 = _matmul(dp, _transposed(sg_in(j)), "nn", tag + "_dh"), None
            mine = [f"sg_w_in{j}", f"sg_w_out{j}"]
        elif kind == "swa":
            (dp, dsk), ridden = _ridden(_swa_bwd(ps[i], cos, sin, sink_row, du, tag + "_mix", rider), rider)
            gfull["swa_sinks"] = dsk[:, :N_HEADS]
            gbig["swa_w_in"] = _matmul(hs[i], dp, "tn", tag + "_dwin", blocked=True)
            gbig["swa_w_out"] = dw_out
            dh, dh2 = _matmul(dp, _transposed(swa_in), "nn", tag + "_dh"), None
            mine = ["swa_w_in", "swa_w_out"]
        else:
            dpm, dpl, rg, ridden = _rwkv_mixer_bwd(ps[i][0], ps[i][1], rw_prm, rw_saved, du, tag, rider)
            mine = ["rwkv_w_in", "rwkv_w_out"]
            dw_main = _matmul(hs[i], dpm, "tn", tag + "_dwin")
            dw_lo = _matmul(hs[i], dpl, "tn", tag + "_dwin_lo")
            dw_main = _head_major(dw_main.reshape(D, 4, D)).reshape(D, RW_MAIN)
            dw_in = jnp.concatenate([dw_main, _lo_cols_inv(dw_lo)], axis=1)
            gbig["rwkv_w_in"] = dw_in.reshape(D, N_CHIPS, -1).transpose(1, 0, 2)
            gbig["rwkv_w_out"] = _rows_head_major(dw_out.reshape(D, D)).reshape(dw_out.shape)
            dmu_main = _head_major(rg["mu_main"].reshape(4, D)).reshape(1, RW_MAIN)
            gfull["rwkv_mu"] = jnp.concatenate([dmu_main, _lo_cols_inv(rg["mu_lo"])], axis=1)
            gfull["rwkv_w0"], gfull["rwkv_a0"] = _head_major(rg["w0"]), _head_major(rg["a0"])
            gfull["rwkv_w_lora"], gfull["rwkv_a_lora"] = _head_major(rg["wl"])[None, :LORA], _head_major(rg["al"])[None, :LORA]
            gfull["rwkv_k_k"], gfull["rwkv_k_a"] = _param_compact_inv(rg["kkp"])[None], _param_compact_inv(rg["kap"])[None]
            gfull["rwkv_r_k"] = _param_compact_inv(rg["rkp"]).reshape(1, N_HEADS, HEAD)
            gfull["rwkv_gn_g"], gfull["rwkv_gn_b"] = _param_compact_inv(rg["gn_g"])[None], _param_compact_inv(rg["gn_b"])[None]
            dh, dh2 = _matmul(dpm, rw_main, "nt", tag + "_dh"), _matmul(dpl, rw_lo, "nt", tag + "_dh_lo")
        rs_p.update(zip(riding_names, ridden[:len(riding_names)]))
        rs_q.update(zip(riding_names, ridden[len(riding_names):]))
        if i == 0:
            for n in ("sg_ln_g", "sg_ln_b"):
                gfull[n] = jnp.stack(gfull[n])
            small = jnp.concatenate([_chip_blocks(gfull[n], axis) for n, axis in SMALL.items()], axis=1)
            small_rows = _round_up(small.shape[1], 2 * 16 * LANES) // LANES
            small = jnp.pad(small, ((0, 0), (0, small_rows * LANES - small.shape[1])))
            gbig["small"] = small.reshape(N_CHIPS, small_rows, LANES)
            mine = mine + ["small"]
        gs = [gbig[n].reshape(N_CHIPS, 2, gbig[n].shape[1] // 2, gbig[n].shape[2]) for n in mine]
        pair_rider = _pair_exchange_rider(gs)
        below = (ys[i - 1], row(gate, i - 1)) if i > 0 else None
        (dx, dg, dsh, dsc, *below_grads), gots = _norm_mod_bwd(xs[i], row(norm_g, i), row(shift, i), row(scale, i), dh, dx,
                                                               tag + "_norm", dh2, pair_rider, below)
        d_norm_g[i] = dg[0]
        d_mod[i] = jnp.concatenate([dsh[0], dsc[0], d_gate[0]])
        if below is not None:
            dy, d_gate = below_grads
        rs_p.update(zip(mine, _rs_pair_sums(gots[:len(gs)], gots[len(gs):], ci, f"rs{i}")))
        riding_names = mine
    for n in ("sg_w_spatial", "sg_b_spatial"):
        gfull[n] = jnp.stack(gfull[n])
    gfull["norm_g"], gfull["final_norm_g"] = jnp.stack(d_norm_g), d_final_g[0]

    grads, deltas, new_m, new_v, red = {}, {}, {}, {}, {}

    def finish(names, tag):
        outs = _rs_finish([rs_p[n] for n in names], [rs_q[n] for n in names], chip, ci, tag)
        red.update({n: r.reshape(-1, r.shape[2]) for n, r in zip(names, outs)})

    def adamw(n, rider=None):
        res = _adamw(w[n], grads[n], given["m_" + n], given["v_" + n], "adamw_" + n, rider)
        (deltas[n], new_m[n], new_v[n]), ridden = _ridden(res, rider)
        return ridden

    def ride_exchange(names, on):
        ridden = adamw(on, _chip_exchange_rider([rs_p[n] for n in names]))
        rs_p.update(zip(names, ridden[:len(names)]))
        rs_q.update(zip(names, ridden[len(names):]))

    rep_flat = jnp.concatenate([gfull[n].reshape(-1) for n in REPLICATED])
    rep_rows = _round_up(rep_flat.shape[0], 32 * RS_W) // RS_W
    rep_buffer = _replicated_pair_sum(_pad_to(rep_flat, rep_rows * RS_W).reshape(rep_rows, RS_W), "rep")

    finish(sorted(set(rs_p) - set(riding_names)), "rs_a")
    for n in ("swa_w_in", "swa_w_out", "rwkv_w_in", "rwkv_w_out"):
        grads[n] = red[n][None]
    dmod_all = _all_gather8(jnp.stack(d_mod).reshape(DEPTH * 3 * D // RS_W, RS_W), "gather_dmod")
    grads["mod_b"] = _sum_leading(dmod_all, "sum_dmod").reshape(DEPTH, 3 * D)
    dmod_all = dmod_all.reshape(N_DEV, DEPTH, 3 * D)
    dmod_cols = lax.dynamic_slice_in_dim(dmod_all, chip * MOD_COLS, MOD_COLS, axis=2).transpose(1, 0, 2)
    grads["mod_w"] = _mod_w_grad(c_all, dmod_cols, "mod_w_grad")
    ride_exchange(["sg_w_in0"], on="mod_w")
    ride_exchange(["sg_w_out0", "small"], on="rwkv_w_in")
    finish(riding_names, "rs_b")
    grads["sg_w_in"] = jnp.stack([red["sg_w_in0"], red["sg_w_in1"]])
    grads["sg_w_out"] = jnp.stack([red["sg_w_out0"], red["sg_w_out1"]])
    small_red, off = red["small"].reshape(-1), 0
    for n in SMALL:
        grads[n] = small_red[off:off + w[n].size].reshape(w[n].shape)
        off += w[n].size

    (rep_gathered,) = adamw("sg_w_in", _gather_rider([rep_buffer]))
    rep_sum, off = _replicated_chip_sum(rep_gathered, "rep").reshape(-1), 0
    for n in REPLICATED:
        grads[n] = rep_sum[off:off + w[n].size].reshape(w[n].shape)
        off += w[n].size

    for n in W_NAMES:
        if n not in deltas:
            adamw(n)
    return (loss, dx[None], *[grads[n] for n in W_NAMES], *[deltas[n] for n in W_NAMES],
            *[new_m[n] for n in W_NAMES], *[new_v[n] for n in W_NAMES])
```
